```python
import jax, jax.numpy as jnp
from jax import lax
import numpy as np

D_MODEL = 1024
BATCH = 8
SEQ = 8192
DEPTH = 2

CHUNK = 64
N_MEM = 256
A_GROUPS = 4
A_GROUP_DIM = 128
A_WIDTH = A_GROUPS * A_GROUP_DIM
A_BLOCK = 128
FOX_HEADS = 8
FOX_HEAD_DIM = 64
FOX_WIDTH = FOX_HEADS * FOX_HEAD_DIM
Q_BLOCK = 128
POOL_WINDOWS = (2, 4, 8, 16)
POOL_GROUP_DIM = 128
POOL_WIDTH = len(POOL_WINDOWS) * POOL_GROUP_DIM
CONV_WIDTH = 512
CONV_K = 3
MIX_WIDTH = 1024
EVEN_IN = 2 * A_WIDTH + 3 * FOX_WIDTH + FOX_HEADS
ODD_IN = POOL_WIDTH + 3 * CONV_WIDTH
XA_HEADS = 4
XA_HEAD_DIM = 128
XA_WIDTH = XA_HEADS * XA_HEAD_DIM
D_FF = -(-(8 * D_MODEL) // (3 * 256)) * 256
EPS = 1e-6
N_EVEN = (DEPTH + 1) // 2
N_ODD = DEPTH // 2

kernel_name = "hybrid_streaming_gmlp_fox_pool_conv"


def rms_norm(x, g):
    xf = x.astype(jnp.float32)
    y = xf * lax.rsqrt(jnp.mean(xf * xf, axis=-1, keepdims=True) + EPS)
    return (y * g.astype(jnp.float32)).astype(x.dtype)


def chunk_gmlp(u, v, g_v, w_s, b_s):
    B, S, _ = v.shape
    v = rms_norm(v.reshape(B, S, A_GROUPS, A_GROUP_DIM), g_v.reshape(A_GROUPS, A_GROUP_DIM))
    vb = v.reshape(B, S // A_BLOCK, A_BLOCK, A_GROUPS, A_GROUP_DIM)
    cpos = np.arange(A_BLOCK) // CHUNK
    mask = jnp.asarray(cpos[:, None] >= cpos[None, :]).astype(w_s.dtype)
    w = w_s * mask[None]
    s = jnp.einsum('gts,bnsgc->bntgc', w, vb) + b_s.T[None, None, :, :, None]
    return u * s.reshape(B, S, A_WIDTH)


def forgetting_attention(q, k, v, log_f):
    B, S, H, hd = q.shape
    nb = S // Q_BLOCK
    scale = 1.0 / np.sqrt(hd)
    c = jnp.cumsum(log_f.astype(jnp.float32), axis=1).transpose(0, 2, 1)
    qb = q.reshape(B, nb, Q_BLOCK, H, hd).transpose(1, 0, 2, 3, 4)
    cb = c.reshape(B, H, nb, Q_BLOCK).transpose(2, 0, 1, 3)
    kpos = jnp.arange(S)

    def block(args):
        qi, ci, i = args
        logits = jnp.einsum('bqhd,bkhd->bhqk', qi, k).astype(jnp.float32) * scale
        logits = logits + ci[..., :, None] - c[:, :, None, :]
        qpos = i * Q_BLOCK + jnp.arange(Q_BLOCK)
        allowed = kpos[None, :] <= qpos[:, None]
        logits = jnp.where(allowed[None, None], logits, -jnp.inf)
        p = jax.nn.softmax(logits, axis=-1)
        return jnp.einsum('bhqk,bkhd->bqhd', p.astype(v.dtype), v)

    out = lax.map(block, (qb, cb, jnp.arange(nb)))
    return out.transpose(1, 0, 2, 3, 4).reshape(B, S, H * hd)


def multiscale_pool(z, w_pool, s_pool):
    B, S, _ = z.shape
    zf = z.astype(jnp.float32)
    cs = jnp.cumsum(zf, axis=1)
    pos = jnp.arange(S)
    outs = []
    for g, w in enumerate(POOL_WINDOWS):
        sl = slice(g * POOL_GROUP_DIM, (g + 1) * POOL_GROUP_DIM)
        csg = cs[..., sl]
        lag = jnp.pad(csg, ((0, 0), (w, 0), (0, 0)))[:, :S]
        cnt = jnp.minimum(pos + 1, w).astype(jnp.float32)[None, :, None]
        outs.append((csg - lag) / cnt - zf[..., sl])
    p = jnp.concatenate(outs, axis=-1).astype(z.dtype)
    p = p.reshape(B, S, len(POOL_WINDOWS), POOL_GROUP_DIM)
    y = jnp.einsum('bsgc,gcd->bsgd', p, w_pool).reshape(B, S, POOL_WIDTH)
    return y * s_pool


def short_gated_conv(h, gate_b, gate_c, conv_w):
    S = h.shape[1]
    xg = gate_c * h
    xp = jnp.pad(xg, ((0, 0), (CONV_K - 1, 0), (0, 0)))
    conv = sum(conv_w[j] * xp[:, j:j + S] for j in range(CONV_K))
    return gate_b * conv


def even_mixer(h, w_in, b_f, g_v, w_s, b_s, g_qn, g_kn, w_out):
    B, S, _ = h.shape
    z = h @ w_in
    uv = jax.nn.gelu(z[..., :2 * A_WIDTH])
    u, v = uv[..., :A_WIDTH], uv[..., A_WIDTH:]
    o = 2 * A_WIDTH
    q = z[..., o:o + FOX_WIDTH].reshape(B, S, FOX_HEADS, FOX_HEAD_DIM)
    k = z[..., o + FOX_WIDTH:o + 2 * FOX_WIDTH].reshape(B, S, FOX_HEADS, FOX_HEAD_DIM)
    vv = z[..., o + 2 * FOX_WIDTH:o + 3 * FOX_WIDTH].reshape(B, S, FOX_HEADS, FOX_HEAD_DIM)
    f_logit = z[..., o + 3 * FOX_WIDTH:].astype(jnp.float32) + b_f.astype(jnp.float32)
    log_f = jax.nn.log_sigmoid(f_logit)
    q = rms_norm(q, g_qn)
    k = rms_norm(k, g_kn)
    y_a = chunk_gmlp(u, v, g_v, w_s, b_s)
    y_b = forgetting_attention(q, k, vv, log_f).astype(h.dtype)
    return jnp.concatenate([y_a, y_b], axis=-1) @ w_out


def odd_mixer(h, w_in, w_pool, s_pool, conv_w, w_out):
    z = h @ w_in
    zc = z[..., :POOL_WIDTH]
    o = POOL_WIDTH
    hd = z[..., o:o + CONV_WIDTH]
    gb = z[..., o + CONV_WIDTH:o + 2 * CONV_WIDTH]
    gc = z[..., o + 2 * CONV_WIDTH:]
    y_c = multiscale_pool(zc, w_pool, s_pool)
    y_d = short_gated_conv(hd, gb, gc, conv_w)
    return jnp.concatenate([y_c, y_d], axis=-1) @ w_out


def memory_cross_attention(h, m, w_q, w_kv, w_o, g_qn, g_kn):
    B, S, _ = h.shape
    M = m.shape[1]
    q = rms_norm((h @ w_q).reshape(B, S, XA_HEADS, XA_HEAD_DIM), g_qn)
    kv = m @ w_kv
    k = rms_norm(kv[..., :XA_WIDTH].reshape(B, M, XA_HEADS, XA_HEAD_DIM), g_kn)
    v = kv[..., XA_WIDTH:].reshape(B, M, XA_HEADS, XA_HEAD_DIM)
    logits = jnp.einsum('bshd,bmhd->bhsm', q, k).astype(jnp.float32) / np.sqrt(XA_HEAD_DIM)
    p = jax.nn.softmax(logits, axis=-1).astype(v.dtype)
    o = jnp.einsum('bhsm,bmhd->bshd', p, v).reshape(B, S, XA_WIDTH)
    return o @ w_o


def swiglu(h, w_gate, w_up, w_down):
    return (jax.nn.silu(h @ w_gate) * (h @ w_up)) @ w_down


def _fwd_setup_inputs(seed: int = 0) -> dict:
    key = jax.random.key(seed)
    ks = jax.random.split(key, 32)
    f32 = jnp.float32

    def nrm(k, shape, scale):
        return jax.random.normal(k, shape, f32) * scale

    def gain(k, shape):
        return 1.0 + 0.05 * jax.random.normal(k, shape, f32)

    L, NE, NO, D = DEPTH, N_EVEN, N_ODD, D_MODEL
    return {
        "x": nrm(ks[0], (BATCH, SEQ, D), 1.0),
        "mem": nrm(ks[1], (BATCH, N_MEM, D), 1.0),
        "g_mix": gain(ks[2], (L, D)),
        "g_xa": gain(ks[3], (L, D)),
        "g_mem": gain(ks[4], (L, D)),
        "xa_wq": nrm(ks[5], (L, D, XA_WIDTH), D ** -0.5),
        "xa_wkv": nrm(ks[6], (L, D, 2 * XA_WIDTH), D ** -0.5),
        "xa_wo": nrm(ks[7], (L, XA_WIDTH, D), XA_WIDTH ** -0.5),
        "xa_gq": gain(ks[8], (L, XA_HEAD_DIM)),
        "xa_gk": gain(ks[9], (L, XA_HEAD_DIM)),
        "g_ffn": gain(ks[10], (L, D)),
        "w_gate": nrm(ks[11], (L, D, D_FF), D ** -0.5),
        "w_up": nrm(ks[12], (L, D, D_FF), D ** -0.5),
        "w_down": nrm(ks[13], (L, D_FF, D), D_FF ** -0.5),
        "e_w_in": nrm(ks[14], (NE, D, EVEN_IN), D ** -0.5),
        "e_b_f": jnp.linspace(1.0, 6.0, FOX_HEADS, dtype=f32)[None] + 0.1 * jax.random.normal(ks[15], (NE, FOX_HEADS), f32),
        "e_g_v": gain(ks[16], (NE, A_WIDTH)),
        "e_w_s": nrm(ks[17], (NE, A_GROUPS, A_BLOCK, A_BLOCK), A_BLOCK ** -0.5),
        "e_b_s": gain(ks[18], (NE, A_GROUPS, A_BLOCK)),
        "e_g_qn": gain(ks[19], (NE, FOX_HEAD_DIM)),
        "e_g_kn": gain(ks[20], (NE, FOX_HEAD_DIM)),
        "e_w_out": nrm(ks[21], (NE, MIX_WIDTH, D), MIX_WIDTH ** -0.5),
        "o_w_in": nrm(ks[22], (NO, D, ODD_IN), D ** -0.5),
        "o_w_pool": nrm(ks[23], (NO, len(POOL_WINDOWS), POOL_GROUP_DIM, POOL_GROUP_DIM), POOL_GROUP_DIM ** -0.5),
        "o_s_pool": gain(ks[24], (NO, POOL_WIDTH)),
        "o_conv_w": nrm(ks[25], (NO, CONV_K, CONV_WIDTH), CONV_K ** -0.5),
        "o_w_out": nrm(ks[26], (NO, MIX_WIDTH, D), MIX_WIDTH ** -0.5),
    }


def _fwd_reference(x, mem, g_mix, g_xa, g_mem, xa_wq, xa_wkv, xa_wo, xa_gq, xa_gk,
              g_ffn, w_gate, w_up, w_down,
              e_w_in, e_b_f, e_g_v, e_w_s, e_b_s, e_g_qn, e_g_kn, e_w_out,
              o_w_in, o_w_pool, o_s_pool, o_conv_w, o_w_out):
    for layer in range(DEPTH):
        i = layer // 2
        h = rms_norm(x, g_mix[layer])
        if layer % 2 == 0:
            y = even_mixer(h, e_w_in[i], e_b_f[i], e_g_v[i], e_w_s[i], e_b_s[i],
                           e_g_qn[i], e_g_kn[i], e_w_out[i])
        else:
            y = odd_mixer(h, o_w_in[i], o_w_pool[i], o_s_pool[i], o_conv_w[i], o_w_out[i])
        x = x + y
        m = rms_norm(mem, g_mem[layer])
        x = x + memory_cross_attention(rms_norm(x, g_xa[layer]), m, xa_wq[layer], xa_wkv[layer],
                                       xa_wo[layer], xa_gq[layer], xa_gk[layer])
        x = x + swiglu(rms_norm(x, g_ffn[layer]), w_gate[layer], w_up[layer], w_down[layer])
    return x


import jax as _jax
import jax.numpy as _jnp

TWIN_FORMAT = 'train_step'
FWD_PARAMS = ['x', 'mem', 'g_mix', 'g_xa', 'g_mem', 'xa_wq', 'xa_wkv', 'xa_wo', 'xa_gq', 'xa_gk', 'g_ffn', 'w_gate', 'w_up', 'w_down', 'e_w_in', 'e_b_f', 'e_g_v', 'e_w_s', 'e_b_s', 'e_g_qn', 'e_g_kn', 'e_w_out', 'o_w_in', 'o_w_pool', 'o_s_pool', 'o_conv_w', 'o_w_out']
TWIN_WEIGHTS = ['g_mix', 'g_xa', 'g_mem', 'xa_wq', 'xa_wkv', 'xa_wo', 'xa_gq', 'xa_gk', 'g_ffn', 'w_gate', 'w_up', 'w_down', 'e_w_in', 'e_b_f', 'e_g_v', 'e_w_s', 'e_b_s', 'e_g_qn', 'e_g_kn', 'e_w_out', 'o_w_in', 'o_w_pool', 'o_s_pool', 'o_conv_w', 'o_w_out']
TWIN_DIFF_INPUT = 'x'
TWIN_INPUTS = ['x', 'mem', 'g_mix', 'g_xa', 'g_mem', 'xa_wq', 'xa_wkv', 'xa_wo', 'xa_gq', 'xa_gk', 'g_ffn', 'w_gate', 'w_up', 'w_down', 'e_w_in', 'e_b_f', 'e_g_v', 'e_w_s', 'e_b_s', 'e_g_qn', 'e_g_kn', 'e_w_out', 'o_w_in', 'o_w_pool', 'o_s_pool', 'o_conv_w', 'o_w_out', 'loss_target', 'm_g_mix', 'm_g_xa', 'm_g_mem', 'm_xa_wq', 'm_xa_wkv', 'm_xa_wo', 'm_xa_gq', 'm_xa_gk', 'm_g_ffn', 'm_w_gate', 'm_w_up', 'm_w_down', 'm_e_w_in', 'm_e_b_f', 'm_e_g_v', 'm_e_w_s', 'm_e_b_s', 'm_e_g_qn', 'm_e_g_kn', 'm_e_w_out', 'm_o_w_in', 'm_o_w_pool', 'm_o_s_pool', 'm_o_conv_w', 'm_o_w_out', 'v_g_mix', 'v_g_xa', 'v_g_mem', 'v_xa_wq', 'v_xa_wkv', 'v_xa_wo', 'v_xa_gq', 'v_xa_gk', 'v_g_ffn', 'v_w_gate', 'v_w_up', 'v_w_down', 'v_e_w_in', 'v_e_b_f', 'v_e_g_v', 'v_e_w_s', 'v_e_b_s', 'v_e_g_qn', 'v_e_g_kn', 'v_e_w_out', 'v_o_w_in', 'v_o_w_pool', 'v_o_s_pool', 'v_o_conv_w', 'v_o_w_out']
TWIN_OUTPUTS = ['loss', 'grad_x', 'grad_g_mix', 'grad_g_xa', 'grad_g_mem', 'grad_xa_wq', 'grad_xa_wkv', 'grad_xa_wo', 'grad_xa_gq', 'grad_xa_gk', 'grad_g_ffn', 'grad_w_gate', 'grad_w_up', 'grad_w_down', 'grad_e_w_in', 'grad_e_b_f', 'grad_e_g_v', 'grad_e_w_s', 'grad_e_b_s', 'grad_e_g_qn', 'grad_e_g_kn', 'grad_e_w_out', 'grad_o_w_in', 'grad_o_w_pool', 'grad_o_s_pool', 'grad_o_conv_w', 'grad_o_w_out', 'delta_g_mix', 'delta_g_xa', 'delta_g_mem', 'delta_xa_wq', 'delta_xa_wkv', 'delta_xa_wo', 'delta_xa_gq', 'delta_xa_gk', 'delta_g_ffn', 'delta_w_gate', 'delta_w_up', 'delta_w_down', 'delta_e_w_in', 'delta_e_b_f', 'delta_e_g_v', 'delta_e_w_s', 'delta_e_b_s', 'delta_e_g_qn', 'delta_e_g_kn', 'delta_e_w_out', 'delta_o_w_in', 'delta_o_w_pool', 'delta_o_s_pool', 'delta_o_conv_w', 'delta_o_w_out', 'new_m_g_mix', 'new_m_g_xa', 'new_m_g_mem', 'new_m_xa_wq', 'new_m_xa_wkv', 'new_m_xa_wo', 'new_m_xa_gq', 'new_m_xa_gk', 'new_m_g_ffn', 'new_m_w_gate', 'new_m_w_up', 'new_m_w_down', 'new_m_e_w_in', 'new_m_e_b_f', 'new_m_e_g_v', 'new_m_e_w_s', 'new_m_e_b_s', 'new_m_e_g_qn', 'new_m_e_g_kn', 'new_m_e_w_out', 'new_m_o_w_in', 'new_m_o_w_pool', 'new_m_o_s_pool', 'new_m_o_conv_w', 'new_m_o_w_out', 'new_v_g_mix', 'new_v_g_xa', 'new_v_g_mem', 'new_v_xa_wq', 'new_v_xa_wkv', 'new_v_xa_wo', 'new_v_xa_gq', 'new_v_xa_gk', 'new_v_g_ffn', 'new_v_w_gate', 'new_v_w_up', 'new_v_w_down', 'new_v_e_w_in', 'new_v_e_b_f', 'new_v_e_g_v', 'new_v_e_w_s', 'new_v_e_b_s', 'new_v_e_g_qn', 'new_v_e_g_kn', 'new_v_e_w_out', 'new_v_o_w_in', 'new_v_o_w_pool', 'new_v_o_s_pool', 'new_v_o_conv_w', 'new_v_o_w_out']
TWIN_LEAF_KINDS = {'loss': 'loss', 'grad_x': 'grad_x', 'grad_g_mix': 'grad_w', 'grad_g_xa': 'grad_w', 'grad_g_mem': 'grad_w', 'grad_xa_wq': 'grad_w', 'grad_xa_wkv': 'grad_w', 'grad_xa_wo': 'grad_w', 'grad_xa_gq': 'grad_w', 'grad_xa_gk': 'grad_w', 'grad_g_ffn': 'grad_w', 'grad_w_gate': 'grad_w', 'grad_w_up': 'grad_w', 'grad_w_down': 'grad_w', 'grad_e_w_in': 'grad_w', 'grad_e_b_f': 'grad_w', 'grad_e_g_v': 'grad_w', 'grad_e_w_s': 'grad_w', 'grad_e_b_s': 'grad_w', 'grad_e_g_qn': 'grad_w', 'grad_e_g_kn': 'grad_w', 'grad_e_w_out': 'grad_w', 'grad_o_w_in': 'grad_w', 'grad_o_w_pool': 'grad_w', 'grad_o_s_pool': 'grad_w', 'grad_o_conv_w': 'grad_w', 'grad_o_w_out': 'grad_w', 'delta_g_mix': 'delta_w', 'delta_g_xa': 'delta_w', 'delta_g_mem': 'delta_w', 'delta_xa_wq': 'delta_w', 'delta_xa_wkv': 'delta_w', 'delta_xa_wo': 'delta_w', 'delta_xa_gq': 'delta_w', 'delta_xa_gk': 'delta_w', 'delta_g_ffn': 'delta_w', 'delta_w_gate': 'delta_w', 'delta_w_up': 'delta_w', 'delta_w_down': 'delta_w', 'delta_e_w_in': 'delta_w', 'delta_e_b_f': 'delta_w', 'delta_e_g_v': 'delta_w', 'delta_e_w_s': 'delta_w', 'delta_e_b_s': 'delta_w', 'delta_e_g_qn': 'delta_w', 'delta_e_g_kn': 'delta_w', 'delta_e_w_out': 'delta_w', 'delta_o_w_in': 'delta_w', 'delta_o_w_pool': 'delta_w', 'delta_o_s_pool': 'delta_w', 'delta_o_conv_w': 'delta_w', 'delta_o_w_out': 'delta_w', 'new_m_g_mix': 'new_m', 'new_m_g_xa': 'new_m', 'new_m_g_mem': 'new_m', 'new_m_xa_wq': 'new_m', 'new_m_xa_wkv': 'new_m', 'new_m_xa_wo': 'new_m', 'new_m_xa_gq': 'new_m', 'new_m_xa_gk': 'new_m', 'new_m_g_ffn': 'new_m', 'new_m_w_gate': 'new_m', 'new_m_w_up': 'new_m', 'new_m_w_down': 'new_m', 'new_m_e_w_in': 'new_m', 'new_m_e_b_f': 'new_m', 'new_m_e_g_v': 'new_m', 'new_m_e_w_s': 'new_m', 'new_m_e_b_s': 'new_m', 'new_m_e_g_qn': 'new_m', 'new_m_e_g_kn': 'new_m', 'new_m_e_w_out': 'new_m', 'new_m_o_w_in': 'new_m', 'new_m_o_w_pool': 'new_m', 'new_m_o_s_pool': 'new_m', 'new_m_o_conv_w': 'new_m', 'new_m_o_w_out': 'new_m', 'new_v_g_mix': 'new_v', 'new_v_g_xa': 'new_v', 'new_v_g_mem': 'new_v', 'new_v_xa_wq': 'new_v', 'new_v_xa_wkv': 'new_v', 'new_v_xa_wo': 'new_v', 'new_v_xa_gq': 'new_v', 'new_v_xa_gk': 'new_v', 'new_v_g_ffn': 'new_v', 'new_v_w_gate': 'new_v', 'new_v_w_up': 'new_v', 'new_v_w_down': 'new_v', 'new_v_e_w_in': 'new_v', 'new_v_e_b_f': 'new_v', 'new_v_e_g_v': 'new_v', 'new_v_e_w_s': 'new_v', 'new_v_e_b_s': 'new_v', 'new_v_e_g_qn': 'new_v', 'new_v_e_g_kn': 'new_v', 'new_v_e_w_out': 'new_v', 'new_v_o_w_in': 'new_v', 'new_v_o_w_pool': 'new_v', 'new_v_o_s_pool': 'new_v', 'new_v_o_conv_w': 'new_v', 'new_v_o_w_out': 'new_v'}


def _forward(args):
    return _fwd_reference(*[args[k] for k in FWD_PARAMS])


def _output_shape():
    def fwd():
        inp = _fwd_setup_inputs(0)
        return _fwd_reference(*[inp[k] for k in FWD_PARAMS])
    out = _jax.eval_shape(fwd)
    return out.shape, out.dtype

N_MICROBATCH = 1
ADAM_LR = 0.001
ADAM_B1 = 0.9
ADAM_B2 = 0.999
ADAM_EPS = 1e-08
ADAM_WD = 0.01
ADAM_STEP = 10
PER_EXAMPLE_BATCH_AXIS = {'x': 0, 'mem': 0, 'loss_target': 0}
SHARED_INPUTS = []
_WEIGHT_DTYPES = {'g_mix': _jnp.float32, 'g_xa': _jnp.float32, 'g_mem': _jnp.float32, 'xa_wq': _jnp.float32, 'xa_wkv': _jnp.float32, 'xa_wo': _jnp.float32, 'xa_gq': _jnp.float32, 'xa_gk': _jnp.float32, 'g_ffn': _jnp.float32, 'w_gate': _jnp.float32, 'w_up': _jnp.float32, 'w_down': _jnp.float32, 'e_w_in': _jnp.float32, 'e_b_f': _jnp.float32, 'e_g_v': _jnp.float32, 'e_w_s': _jnp.float32, 'e_b_s': _jnp.float32, 'e_g_qn': _jnp.float32, 'e_g_kn': _jnp.float32, 'e_w_out': _jnp.float32, 'o_w_in': _jnp.float32, 'o_w_pool': _jnp.float32, 'o_s_pool': _jnp.float32, 'o_conv_w': _jnp.float32, 'o_w_out': _jnp.float32}
MOMENT_SCALE = {'g_mix': 9.316775e+01, 'g_xa': 1.677187e-01, 'g_mem': 6.674673e-01, 'xa_wq': 2.363298e-01, 'xa_wkv': 3.693647e-01, 'xa_wo': 3.460983e-01, 'xa_gq': 4.915280e+00, 'xa_gk': 4.978847e+00, 'g_ffn': 5.026267e+01, 'w_gate': 8.096053e-01, 'w_up': 6.004476e-01, 'w_down': 9.784933e-01, 'e_w_in': 1.047057e+00, 'e_b_f': 1.465117e+02, 'e_g_v': 1.956262e+01, 'e_w_s': 1.002061e+01, 'e_b_s': 2.804441e+01, 'e_g_qn': 1.607671e+01, 'e_g_kn': 1.593775e+01, 'e_w_out': 3.732110e+00, 'o_w_in': 2.784944e+00, 'o_w_pool': 5.116510e+00, 'o_s_pool': 4.972970e+01, 'o_conv_w': 3.903045e+01, 'o_w_out': 2.532931e+00}


def _to_microbatches(a, axis):
    t = _jnp.moveaxis(a, axis, 0)
    t = t.reshape((N_MICROBATCH, t.shape[0] // N_MICROBATCH) + t.shape[1:])
    return _jnp.moveaxis(t, 1, axis + 1)


def setup_inputs(seed: int = 0) -> dict:
    inp = _fwd_setup_inputs(seed)
    key = _jax.random.fold_in(_jax.random.key(seed), 7919)
    shape, _ = _output_shape()
    out = dict(inp)
    out["loss_target"] = _jax.random.normal(_jax.random.fold_in(key, 0), shape, _jnp.float32)
    for i, name in enumerate(TWIN_WEIGHTS):
        w = inp[name].astype(_jnp.float32)
        if MOMENT_SCALE is None:
            s = _jnp.sqrt(_jnp.mean(_jnp.square(w)) + 1e-30)
        else:
            s = MOMENT_SCALE[name]
        km, kv = _jax.random.split(_jax.random.fold_in(key, i + 1))
        out[name] = w
        out["m_" + name] = s * _jax.random.normal(km, w.shape, _jnp.float32)
        out["v_" + name] = (s * s) * _jax.random.uniform(kv, w.shape, _jnp.float32, 0.5, 1.5)
    if N_MICROBATCH > 1:
        for name, axis in PER_EXAMPLE_BATCH_AXIS.items():
            out[name] = _to_microbatches(out[name], axis)
    return {'x': out['x'], 'mem': out['mem'], 'g_mix': out['g_mix'], 'g_xa': out['g_xa'], 'g_mem': out['g_mem'], 'xa_wq': out['xa_wq'], 'xa_wkv': out['xa_wkv'], 'xa_wo': out['xa_wo'], 'xa_gq': out['xa_gq'], 'xa_gk': out['xa_gk'], 'g_ffn': out['g_ffn'], 'w_gate': out['w_gate'], 'w_up': out['w_up'], 'w_down': out['w_down'], 'e_w_in': out['e_w_in'], 'e_b_f': out['e_b_f'], 'e_g_v': out['e_g_v'], 'e_w_s': out['e_w_s'], 'e_b_s': out['e_b_s'], 'e_g_qn': out['e_g_qn'], 'e_g_kn': out['e_g_kn'], 'e_w_out': out['e_w_out'], 'o_w_in': out['o_w_in'], 'o_w_pool': out['o_w_pool'], 'o_s_pool': out['o_s_pool'], 'o_conv_w': out['o_conv_w'], 'o_w_out': out['o_w_out'], 'loss_target': out['loss_target'], 'm_g_mix': out['m_g_mix'], 'm_g_xa': out['m_g_xa'], 'm_g_mem': out['m_g_mem'], 'm_xa_wq': out['m_xa_wq'], 'm_xa_wkv': out['m_xa_wkv'], 'm_xa_wo': out['m_xa_wo'], 'm_xa_gq': out['m_xa_gq'], 'm_xa_gk': out['m_xa_gk'], 'm_g_ffn': out['m_g_ffn'], 'm_w_gate': out['m_w_gate'], 'm_w_up': out['m_w_up'], 'm_w_down': out['m_w_down'], 'm_e_w_in': out['m_e_w_in'], 'm_e_b_f': out['m_e_b_f'], 'm_e_g_v': out['m_e_g_v'], 'm_e_w_s': out['m_e_w_s'], 'm_e_b_s': out['m_e_b_s'], 'm_e_g_qn': out['m_e_g_qn'], 'm_e_g_kn': out['m_e_g_kn'], 'm_e_w_out': out['m_e_w_out'], 'm_o_w_in': out['m_o_w_in'], 'm_o_w_pool': out['m_o_w_pool'], 'm_o_s_pool': out['m_o_s_pool'], 'm_o_conv_w': out['m_o_conv_w'], 'm_o_w_out': out['m_o_w_out'], 'v_g_mix': out['v_g_mix'], 'v_g_xa': out['v_g_xa'], 'v_g_mem': out['v_g_mem'], 'v_xa_wq': out['v_xa_wq'], 'v_xa_wkv': out['v_xa_wkv'], 'v_xa_wo': out['v_xa_wo'], 'v_xa_gq': out['v_xa_gq'], 'v_xa_gk': out['v_xa_gk'], 'v_g_ffn': out['v_g_ffn'], 'v_w_gate': out['v_w_gate'], 'v_w_up': out['v_w_up'], 'v_w_down': out['v_w_down'], 'v_e_w_in': out['v_e_w_in'], 'v_e_b_f': out['v_e_b_f'], 'v_e_g_v': out['v_e_g_v'], 'v_e_w_s': out['v_e_w_s'], 'v_e_b_s': out['v_e_b_s'], 'v_e_g_qn': out['v_e_g_qn'], 'v_e_g_kn': out['v_e_g_kn'], 'v_e_w_out': out['v_e_w_out'], 'v_o_w_in': out['v_o_w_in'], 'v_o_w_pool': out['v_o_w_pool'], 'v_o_s_pool': out['v_o_s_pool'], 'v_o_conv_w': out['v_o_conv_w'], 'v_o_w_out': out['v_o_w_out']}


def _loss(weights, diff, rest, loss_target):
    with _jax.named_scope("forward"):
        args = {**rest, TWIN_DIFF_INPUT: diff, **{k: w.astype(_WEIGHT_DTYPES[k]) for k, w in weights.items()}}
        y = _forward(args)
    with _jax.named_scope("loss_head"):
        err = _jnp.square(y.astype(_jnp.float32) - loss_target)
        return 0.5 * _jnp.sum(_jnp.mean(err, axis=-1)) if err.ndim else 0.5 * err


def _adamw(w, g, m, v):
    m = ADAM_B1 * m + (1.0 - ADAM_B1) * g
    v = ADAM_B2 * v + (1.0 - ADAM_B2) * _jnp.square(g)
    m_hat = m / (1.0 - ADAM_B1 ** ADAM_STEP)
    v_hat = v / (1.0 - ADAM_B2 ** ADAM_STEP)
    delta = -ADAM_LR * (m_hat / (_jnp.sqrt(v_hat) + ADAM_EPS) + ADAM_WD * w)
    return delta, m, v


def reference(x, mem, g_mix, g_xa, g_mem, xa_wq, xa_wkv, xa_wo, xa_gq, xa_gk, g_ffn, w_gate, w_up, w_down, e_w_in, e_b_f, e_g_v, e_w_s, e_b_s, e_g_qn, e_g_kn, e_w_out, o_w_in, o_w_pool, o_s_pool, o_conv_w, o_w_out, loss_target, m_g_mix, m_g_xa, m_g_mem, m_xa_wq, m_xa_wkv, m_xa_wo, m_xa_gq, m_xa_gk, m_g_ffn, m_w_gate, m_w_up, m_w_down, m_e_w_in, m_e_b_f, m_e_g_v, m_e_w_s, m_e_b_s, m_e_g_qn, m_e_g_kn, m_e_w_out, m_o_w_in, m_o_w_pool, m_o_s_pool, m_o_conv_w, m_o_w_out, v_g_mix, v_g_xa, v_g_mem, v_xa_wq, v_xa_wkv, v_xa_wo, v_xa_gq, v_xa_gk, v_g_ffn, v_w_gate, v_w_up, v_w_down, v_e_w_in, v_e_b_f, v_e_g_v, v_e_w_s, v_e_b_s, v_e_g_qn, v_e_g_kn, v_e_w_out, v_o_w_in, v_o_w_pool, v_o_s_pool, v_o_conv_w, v_o_w_out):
    given = dict(x=x, mem=mem, g_mix=g_mix, g_xa=g_xa, g_mem=g_mem, xa_wq=xa_wq, xa_wkv=xa_wkv, xa_wo=xa_wo, xa_gq=xa_gq, xa_gk=xa_gk, g_ffn=g_ffn, w_gate=w_gate, w_up=w_up, w_down=w_down, e_w_in=e_w_in, e_b_f=e_b_f, e_g_v=e_g_v, e_w_s=e_w_s, e_b_s=e_b_s, e_g_qn=e_g_qn, e_g_kn=e_g_kn, e_w_out=e_w_out, o_w_in=o_w_in, o_w_pool=o_w_pool, o_s_pool=o_s_pool, o_conv_w=o_conv_w, o_w_out=o_w_out, loss_target=loss_target, m_g_mix=m_g_mix, m_g_xa=m_g_xa, m_g_mem=m_g_mem, m_xa_wq=m_xa_wq, m_xa_wkv=m_xa_wkv, m_xa_wo=m_xa_wo, m_xa_gq=m_xa_gq, m_xa_gk=m_xa_gk, m_g_ffn=m_g_ffn, m_w_gate=m_w_gate, m_w_up=m_w_up, m_w_down=m_w_down, m_e_w_in=m_e_w_in, m_e_b_f=m_e_b_f, m_e_g_v=m_e_g_v, m_e_w_s=m_e_w_s, m_e_b_s=m_e_b_s, m_e_g_qn=m_e_g_qn, m_e_g_kn=m_e_g_kn, m_e_w_out=m_e_w_out, m_o_w_in=m_o_w_in, m_o_w_pool=m_o_w_pool, m_o_s_pool=m_o_s_pool, m_o_conv_w=m_o_conv_w, m_o_w_out=m_o_w_out, v_g_mix=v_g_mix, v_g_xa=v_g_xa, v_g_mem=v_g_mem, v_xa_wq=v_xa_wq, v_xa_wkv=v_xa_wkv, v_xa_wo=v_xa_wo, v_xa_gq=v_xa_gq, v_xa_gk=v_xa_gk, v_g_ffn=v_g_ffn, v_w_gate=v_w_gate, v_w_up=v_w_up, v_w_down=v_w_down, v_e_w_in=v_e_w_in, v_e_b_f=v_e_b_f, v_e_g_v=v_e_g_v, v_e_w_s=v_e_w_s, v_e_b_s=v_e_b_s, v_e_g_qn=v_e_g_qn, v_e_g_kn=v_e_g_kn, v_e_w_out=v_e_w_out, v_o_w_in=v_o_w_in, v_o_w_pool=v_o_w_pool, v_o_s_pool=v_o_s_pool, v_o_conv_w=v_o_conv_w, v_o_w_out=v_o_w_out)
    weights = {n: given[n] for n in TWIN_WEIGHTS}
    shared = {n: given[n] for n in SHARED_INPUTS}
    per_example = {n: given[n] for n in ['x', 'mem']}
    grad_fn = _jax.value_and_grad(_loss, argnums=(0, 1))

    def one_microbatch(ex, loss_target):
        ex = dict(ex)
        diff = ex.pop(TWIN_DIFF_INPUT)
        return grad_fn(weights, diff, {**shared, **ex}, loss_target)

    if N_MICROBATCH == 1:
        loss, (grad_w, grad_x) = one_microbatch(per_example, given["loss_target"])
    else:
        def body(carry, xs):
            loss_sum, grad_sum = carry
            l_k, (gw_k, gx_k) = one_microbatch(xs[0], xs[1])
            with _jax.named_scope("update"):
                return (loss_sum + l_k, _jax.tree.map(_jnp.add, grad_sum, gw_k)), gx_k

        init = (_jnp.zeros((), _jnp.float32), _jax.tree.map(_jnp.zeros_like, weights))
        (loss, grad_w), grad_x = _jax.lax.scan(body, init, (per_example, given["loss_target"]))
    with _jax.named_scope("update"):
        delta_w, new_m, new_v = {}, {}, {}
        for n in TWIN_WEIGHTS:
            delta_w[n], new_m[n], new_v[n] = _adamw(weights[n], grad_w[n], given["m_" + n], given["v_" + n])
    return (loss, grad_x, *[grad_w[n] for n in TWIN_WEIGHTS], *[delta_w[n] for n in TWIN_WEIGHTS],
            *[new_m[n] for n in TWIN_WEIGHTS], *[new_v[n] for n in TWIN_WEIGHTS])
```

```python
import functools
import math

import numpy as np
import jax
import jax.numpy as jnp
from jax import lax
from jax.experimental import pallas as pl
from jax.experimental.pallas import tpu as pltpu

F32 = jnp.float32
BF16 = jnp.bfloat16
_MXU = jnp.bfloat16

D_MODEL = 1024
N_DEV = 8
LANES = 128
SUBLANES = 8
HALO = 16
CHUNK = 64
A_GROUPS, A_GROUP_DIM, A_WIDTH, A_BLOCK = 4, 128, 512, 128
FOX_HEADS, FOX_HEAD_DIM, FOX_WIDTH = 8, 64, 512
POOL_WINDOWS = (2, 4, 8, 16)
POOL_WIDTH, CONV_WIDTH, CONV_K = 512, 512, 3
EVEN_IN, EVEN_IN_PAD, ODD_IN = 2568, 2688, 2048
XA_HEADS, XA_HEAD_DIM, XA_WIDTH = 4, 128, 512
D_FF = 2816
EPS = 1e-6
ADAM_LR, ADAM_B1, ADAM_B2, ADAM_EPS, ADAM_WD, ADAM_STEP = 0.001, 0.9, 0.999, 1e-08, 0.01, 10
VMEM_LIMIT = 48 * 1024 * 1024


def _cp(sem):
    return pltpu.CompilerParams(dimension_semantics=sem, vmem_limit_bytes=VMEM_LIMIT)


def _tile(n, cap, q):
    best = None
    for d in range(q, min(n, cap) + 1, q):
        if n % d == 0:
            best = d
    return best if best is not None else n


def _mx(a):
    return a.astype(_MXU)


def _dot(a, b):
    return jnp.dot(_mx(a), _mx(b), preferred_element_type=F32)


def _dot_nt(a, b):
    return lax.dot_general(_mx(a), _mx(b), (((1,), (1,)), ((), ())), preferred_element_type=F32)


def _dot_tn(a, b):
    return lax.dot_general(_mx(a), _mx(b), (((0,), (0,)), ((), ())), preferred_element_type=F32)


def _dot_f32(a, b):
    return jnp.dot(a, b, precision=lax.Precision.HIGHEST, preferred_element_type=F32)


def _lane(shape):
    return lax.broadcasted_iota(jnp.int32, shape, len(shape) - 1)


def _row(shape):
    return lax.broadcasted_iota(jnp.int32, shape, len(shape) - 2)


def _mm(name, pairs, mode, out_dtype=F32, residual=None, tm_cap=512, tn_cap=1536, tk_cap=2048):
    a0, b0 = pairs[0]
    if mode == "nn":
        (M, K), N = a0.shape, b0.shape[1]
    elif mode == "nt":
        (M, K), N = a0.shape, b0.shape[0]
    else:
        (K, M), N = a0.shape, b0.shape[1]
    if mode == "tn":
        tm, tk = _tile(M, tm_cap, LANES), _tile(K, 512, 16)
    else:
        tm, tk = _tile(M, tm_cap, 16), _tile(K, tk_cap, LANES)
    tn = _tile(N, tn_cap, LANES)
    nk = K // tk
    npairs = len(pairs)
    dot = {"nn": _dot, "nt": _dot_nt, "tn": _dot_tn}[mode]

    def body(*refs):
        ab = refs[:2 * npairs]
        res_ref = refs[2 * npairs] if residual is not None else None
        o_ref, acc = refs[-2], refs[-1]
        k = pl.program_id(2)

        @pl.when(k == 0)
        def _():
            acc[...] = jnp.zeros_like(acc)

        for p in range(npairs):
            acc[...] += dot(ab[2 * p][...], ab[2 * p + 1][...])

        @pl.when(k == nk - 1)
        def _():
            out = acc[...]
            if res_ref is not None:
                out = out + res_ref[...]
            o_ref[...] = out.astype(o_ref.dtype)

    if mode == "nn":
        a_spec = pl.BlockSpec((tm, tk), lambda j, i, k: (i, k))
        b_spec = pl.BlockSpec((tk, tn), lambda j, i, k: (k, j))
    elif mode == "nt":
        a_spec = pl.BlockSpec((tm, tk), lambda j, i, k: (i, k))
        b_spec = pl.BlockSpec((tn, tk), lambda j, i, k: (j, k))
    else:
        a_spec = pl.BlockSpec((tk, tm), lambda j, i, k: (k, i))
        b_spec = pl.BlockSpec((tk, tn), lambda j, i, k: (k, j))
    o_spec = pl.BlockSpec((tm, tn), lambda j, i, k: (i, j))
    in_specs, args = [], []
    for a, b in pairs:
        in_specs += [a_spec, b_spec]
        args += [a, b]
    if residual is not None:
        in_specs.append(o_spec)
        args.append(residual)
    return pl.pallas_call(
        body, name=name, grid=(N // tn, M // tm, nk),
        in_specs=in_specs, out_specs=o_spec,
        out_shape=jax.ShapeDtypeStruct((M, N), out_dtype),
        scratch_shapes=[pltpu.VMEM((tm, tn), F32)],
        compiler_params=_cp(("parallel", "parallel", "arbitrary")),
    )(*args)


def _rms_rows(x):
    return lax.rsqrt(jnp.mean(x * x, axis=-1, keepdims=True) + EPS)


def _norm_fwd(name, x, g):
    T, Dm = x.shape
    tm = _tile(T, 512, 16)

    def body(x_ref, g_ref, o_ref):
        xv = x_ref[...]
        o_ref[...] = ((xv * _rms_rows(xv)) * g_ref[...]).astype(o_ref.dtype)

    return pl.pallas_call(
        body, name=name, grid=(T // tm,),
        in_specs=[pl.BlockSpec((tm, Dm), lambda i: (i, 0)), pl.BlockSpec((1, Dm), lambda i: (0, 0))],
        out_specs=pl.BlockSpec((tm, Dm), lambda i: (i, 0)),
        out_shape=jax.ShapeDtypeStruct((T, Dm), _MXU),
        compiler_params=_cp(("parallel",)),
    )(x, g.reshape(1, Dm))


def _norm_bwd(name, dh, x, g, res):
    T, Dm = x.shape
    tm = _tile(T, 512, 8)

    def body(dh_ref, x_ref, g_ref, res_ref, dx_ref, dg_ref):
        xv, dhv = x_ref[...], dh_ref[...]
        r = _rms_rows(xv)
        xhat = xv * r
        dxhat = dhv * g_ref[...]
        dx_ref[...] = res_ref[...] + r * (dxhat - xhat * jnp.mean(dxhat * xhat, axis=-1, keepdims=True))

        @pl.when(pl.program_id(0) == 0)
        def _():
            dg_ref[...] = jnp.zeros_like(dg_ref)

        dg_ref[...] += jnp.sum(dhv * xhat, axis=0, keepdims=True)

    blk = pl.BlockSpec((tm, Dm), lambda i: (i, 0))
    vec = pl.BlockSpec((1, Dm), lambda i: (0, 0))
    return pl.pallas_call(
        body, name=name, grid=(T // tm,),
        in_specs=[blk, blk, vec, blk], out_specs=[blk, vec],
        out_shape=[jax.ShapeDtypeStruct((T, Dm), F32), jax.ShapeDtypeStruct((1, Dm), F32)],
        compiler_params=_cp(("arbitrary",)),
    )(dh, x, g.reshape(1, Dm), res)


def _gelu(x):
    return jax.nn.gelu(x)


def _gelu_grad(x):
    c0, c1 = math.sqrt(2.0 / math.pi), 0.044715
    t = jnp.tanh(c0 * (x + c1 * x * x * x))
    return 0.5 * (1.0 + t) + 0.5 * x * (1.0 - t * t) * c0 * (1.0 + 3.0 * c1 * x * x)


def _head64_rms(x):
    lo = _lane(x.shape) < FOX_HEAD_DIM
    xx = x * x
    sa = jnp.sum(jnp.where(lo, xx, 0.0), axis=-1, keepdims=True)
    sb = jnp.sum(jnp.where(lo, 0.0, xx), axis=-1, keepdims=True)
    inv = 1.0 / FOX_HEAD_DIM
    return jnp.where(lo, lax.rsqrt(sa * inv + EPS), lax.rsqrt(sb * inv + EPS))


def _head64_mean(x):
    lo = _lane(x.shape) < FOX_HEAD_DIM
    sa = jnp.sum(jnp.where(lo, x, 0.0), axis=-1, keepdims=True)
    sb = jnp.sum(jnp.where(lo, 0.0, x), axis=-1, keepdims=True)
    return jnp.where(lo, sa, sb) * (1.0 / FOX_HEAD_DIM)


def _masked_ws(ws_ref, g):
    w = ws_ref[g]
    keep = (_row(w.shape) // CHUNK) >= (_lane(w.shape) // CHUNK)
    return jnp.where(keep, w, 0.0)


def _even_pre(name, z, g_v, w_s, bs_cols, g_qn2, g_kn2, b_f_pad):
    T = z.shape[0]
    tm = _tile(T, 256, A_BLOCK)
    nb = tm // A_BLOCK

    def body(z_ref, gv_ref, ws_ref, bs_ref, gq_ref, gk_ref, bf_ref, ya_ref, q_ref, k_ref, v_ref, c_ref, ct_ref, carry):
        i = pl.program_id(0)

        @pl.when(i == 0)
        def _():
            carry[...] = jnp.zeros_like(carry)

        wm = [_mx(_masked_ws(ws_ref, g)) for g in range(A_GROUPS)]
        for g in range(A_GROUPS):
            sl = slice(g * A_GROUP_DIM, (g + 1) * A_GROUP_DIM)
            u = _gelu(z_ref[:, sl])
            v = _gelu(z_ref[:, A_WIDTH + g * A_GROUP_DIM:A_WIDTH + (g + 1) * A_GROUP_DIM])
            vn = _mx((v * _rms_rows(v)) * gv_ref[:, sl])
            for n in range(nb):
                rows = slice(n * A_BLOCK, (n + 1) * A_BLOCK)
                s = jnp.dot(wm[g], vn[rows], preferred_element_type=F32) + bs_ref[:, g:g + 1]
                ya_ref[rows, sl] = (u[rows] * s).astype(ya_ref.dtype)
        o = 2 * A_WIDTH
        for j in range(FOX_WIDTH // LANES):
            sl = slice(j * LANES, (j + 1) * LANES)
            q = z_ref[:, o + j * LANES:o + (j + 1) * LANES]
            q_ref[:, sl] = ((q * _head64_rms(q)) * gq_ref[...]).astype(q_ref.dtype)
            k = z_ref[:, o + FOX_WIDTH + j * LANES:o + FOX_WIDTH + (j + 1) * LANES]
            k_ref[:, sl] = ((k * _head64_rms(k)) * gk_ref[...]).astype(k_ref.dtype)
            v_ref[:, sl] = z_ref[:, o + 2 * FOX_WIDTH + j * LANES:o + 2 * FOX_WIDTH + (j + 1) * LANES].astype(v_ref.dtype)
        zf = z_ref[:, o + 3 * FOX_WIDTH:o + 3 * FOX_WIDTH + LANES]
        logf = jnp.where(_lane(zf.shape) < FOX_HEADS, jax.nn.log_sigmoid(zf + bf_ref[...]), 0.0)
        tri = (_row((tm, tm)) >= _lane((tm, tm))).astype(F32)
        c = _dot_f32(tri, logf) + carry[0:1, :]
        c_ref[...] = c
        ct_ref[...] = c.T[0:SUBLANES, :]
        carry[0:1, :] = c[tm - 1:tm, :]

    wide = lambda n: pl.BlockSpec((tm, n), lambda i: (i, 0))
    full = lambda s: pl.BlockSpec(s, lambda i: tuple(0 for _ in s))
    out512 = jax.ShapeDtypeStruct((T, 512), _MXU)
    return pl.pallas_call(
        body, name=name, grid=(T // tm,),
        in_specs=[wide(EVEN_IN_PAD), full((1, A_WIDTH)), full((A_GROUPS, A_BLOCK, A_BLOCK)), full((A_BLOCK, LANES)),
                  full((1, LANES)), full((1, LANES)), full((1, LANES))],
        out_specs=[wide(512), wide(512), wide(512), wide(512), wide(LANES), pl.BlockSpec((SUBLANES, tm), lambda i: (0, i))],
        out_shape=[out512, out512, out512, out512, jax.ShapeDtypeStruct((T, LANES), F32), jax.ShapeDtypeStruct((SUBLANES, T), F32)],
        scratch_shapes=[pltpu.VMEM((SUBLANES, LANES), F32)],
        compiler_params=_cp(("arbitrary",)),
    )(z, g_v, w_s, bs_cols, g_qn2, g_kn2, b_f_pad)


def _tri_steps(n, by_rows):
    if by_rows:
        pairs = [(q, k) for q in range(n) for k in range(q + 1)]
    else:
        pairs = [(q, k) for k in range(n) for q in range(k, n)]
    return (jnp.asarray(np.array([p[0] for p in pairs], np.int32)), jnp.asarray(np.array([p[1] for p in pairs], np.int32)))


def _fox_logits(q, k, c_col, c_row, qi, ki, tq, tk):
    s = _dot_nt(q, k) * (1.0 / math.sqrt(FOX_HEAD_DIM)) + (c_col - c_row)
    allowed = (qi * tq + _row((tq, tk))) >= (ki * tk + _lane((tq, tk)))
    return jnp.where(allowed, s, -jnp.inf)


def _fox_fwd(name, qn, kn, vv, c, ct):
    T = qn.shape[0]
    tq = _tile(T, 512, LANES)
    nblk = T // tq
    qs, ks = _tri_steps(nblk, True)
    nsteps = int(qs.shape[0])
    npair = FOX_WIDTH // LANES

    def body(qs_ref, ks_ref, q_ref, k_ref, v_ref, c_ref, ct_ref, yb_ref, yf_ref, lse_ref, acc, m_s, l_s):
        s_id = pl.program_id(0)
        qi, ki = qs_ref[s_id], ks_ref[s_id]

        @pl.when(ki == 0)
        def _():
            acc[...] = jnp.zeros_like(acc)
            m_s[...] = jnp.full_like(m_s, -jnp.inf)
            l_s[...] = jnp.zeros_like(l_s)

        lo = _lane((tq, LANES)) < FOX_HEAD_DIM
        for j in range(npair):
            sl = slice(j * LANES, (j + 1) * LANES)
            q2, k2, v2 = q_ref[:, sl], k_ref[:, sl], v_ref[:, sl]
            outs = []
            for e in range(2):
                h = 2 * j + e
                qh = jnp.where(lo if e == 0 else ~lo, q2, jnp.zeros_like(q2))
                s = _fox_logits(qh, k2, c_ref[:, h:h + 1], ct_ref[h:h + 1, :], qi, ki, tq, tq)
                m_old = m_s[h]
                m_new = jnp.maximum(m_old, jnp.max(s, axis=-1, keepdims=True))
                alpha = jnp.exp(m_old - m_new)
                p = jnp.exp(s - m_new)
                l_s[h] = alpha * l_s[h] + jnp.sum(p, axis=-1, keepdims=True)
                m_s[h] = m_new
                outs.append((alpha, _dot(p, v2)))
            alpha2 = jnp.where(lo, outs[0][0], outs[1][0])
            acc[:, sl] = alpha2 * acc[:, sl] + jnp.where(lo, outs[0][1], outs[1][1])

        @pl.when(ki == qi)
        def _():
            lse = jnp.zeros((tq, LANES), F32)
            for j in range(npair):
                sl = slice(j * LANES, (j + 1) * LANES)
                inv = jnp.where(lo, 1.0 / l_s[2 * j], 1.0 / l_s[2 * j + 1])
                y = acc[:, sl] * inv
                yf_ref[:, sl] = y
                yb_ref[:, sl] = y.astype(yb_ref.dtype)
            for h in range(FOX_HEADS):
                lse = jnp.where(_lane((tq, LANES)) == h, m_s[h] + jnp.log(l_s[h]), lse)
            lse_ref[...] = lse

    qmap = lambda s, qs_r, ks_r: (qs_r[s], 0)
    kmap = lambda s, qs_r, ks_r: (ks_r[s], 0)
    grid_spec = pltpu.PrefetchScalarGridSpec(
        num_scalar_prefetch=2, grid=(nsteps,),
        in_specs=[pl.BlockSpec((tq, 512), qmap), pl.BlockSpec((tq, 512), kmap), pl.BlockSpec((tq, 512), kmap),
                  pl.BlockSpec((tq, LANES), qmap), pl.BlockSpec((SUBLANES, tq), lambda s, qs_r, ks_r: (0, ks_r[s]))],
        out_specs=[pl.BlockSpec((tq, 512), qmap), pl.BlockSpec((tq, 512), qmap), pl.BlockSpec((tq, LANES), qmap)],
        scratch_shapes=[pltpu.VMEM((tq, 512), F32), pltpu.VMEM((FOX_HEADS, tq, 1), F32), pltpu.VMEM((FOX_HEADS, tq, 1), F32)],
    )
    return pl.pallas_call(
        body, name=name, grid_spec=grid_spec,
        out_shape=[jax.ShapeDtypeStruct((T, 512), _MXU), jax.ShapeDtypeStruct((T, 512), F32), jax.ShapeDtypeStruct((T, LANES), F32)],
        compiler_params=_cp(("arbitrary",)),
    )(qs, ks, qn, kn, vv, c, ct)


def _shift_down(ext, k, tm):
    return pltpu.roll(ext, k, axis=0)[HALO:HALO + tm]


def _pool_p(z_ext, g, tm, pos):
    w = POOL_WINDOWS[g]
    s, span = z_ext, 1
    while span < w:
        s = s + pltpu.roll(s, span, axis=0)
        span *= 2
    cnt = jnp.minimum(pos + 1, w).astype(F32)
    return s[HALO:HALO + tm] / cnt - z_ext[HALO:HALO + tm]


def _odd_fwd(name, z, w_pool, s_pool, conv_w8):
    T = z.shape[0]
    tm = _tile(T, 256, HALO)
    r = tm // HALO

    def body(z_ref, zb_ref, wp_ref, sp_ref, cw_ref, y_ref):
        i = pl.program_id(0)
        first = i == 0
        pos = i * tm + _row((tm, LANES))
        for g in range(len(POOL_WINDOWS)):
            sl = slice(g * LANES, (g + 1) * LANES)
            halo = jnp.where(first, 0.0, zb_ref[:, sl])
            z_ext = jnp.concatenate([halo, z_ref[:, sl]], axis=0)
            p = _pool_p(z_ext, g, tm, pos)
            y_ref[:, sl] = (_dot(p, wp_ref[g]) * sp_ref[:, sl]).astype(y_ref.dtype)
        for j in range(CONV_WIDTH // LANES):
            sl = slice(j * LANES, (j + 1) * LANES)
            hd = slice(POOL_WIDTH + j * LANES, POOL_WIDTH + (j + 1) * LANES)
            gb = slice(POOL_WIDTH + CONV_WIDTH + j * LANES, POOL_WIDTH + CONV_WIDTH + (j + 1) * LANES)
            gc = slice(POOL_WIDTH + 2 * CONV_WIDTH + j * LANES, POOL_WIDTH + 2 * CONV_WIDTH + (j + 1) * LANES)
            xg_b = jnp.where(first, 0.0, zb_ref[:, gc] * zb_ref[:, hd])
            xg = jnp.concatenate([xg_b, z_ref[:, gc] * z_ref[:, hd]], axis=0)
            conv = (cw_ref[0:1, sl] * _shift_down(xg, 2, tm) + cw_ref[1:2, sl] * _shift_down(xg, 1, tm)
                    + cw_ref[2:3, sl] * xg[HALO:HALO + tm])
            y_ref[:, POOL_WIDTH + j * LANES:POOL_WIDTH + (j + 1) * LANES] = (z_ref[:, gb] * conv).astype(y_ref.dtype)

    full = lambda s: pl.BlockSpec(s, lambda i: tuple(0 for _ in s))
    return pl.pallas_call(
        body, name=name, grid=(T // tm,),
        in_specs=[pl.BlockSpec((tm, ODD_IN), lambda i: (i, 0)),
                  pl.BlockSpec((HALO, ODD_IN), lambda i: (jnp.maximum(i * r - 1, 0), 0)),
                  full((4, LANES, LANES)), full((1, POOL_WIDTH)), full((SUBLANES, CONV_WIDTH))],
        out_specs=pl.BlockSpec((tm, 1024), lambda i: (i, 0)),
        out_shape=jax.ShapeDtypeStruct((T, 1024), _MXU),
        compiler_params=_cp(("parallel",)),
    )(z, z, w_pool, s_pool, conv_w8)


def _mem_fwd(name, mem, g_mem, wkv, gk):
    M = mem.shape[0]

    def body(mem_ref, g_ref, w_ref, gk_ref, m_ref, kv_ref, kn_ref, v_ref):
        x = mem_ref[...]
        m = ((x * _rms_rows(x)) * g_ref[...]).astype(m_ref.dtype)
        m_ref[...] = m
        kv = _dot(m, w_ref[...])
        kv_ref[...] = kv
        for h in range(XA_HEADS):
            sl = slice(h * LANES, (h + 1) * LANES)
            k = kv[:, sl]
            kn_ref[:, sl] = ((k * _rms_rows(k)) * gk_ref[...]).astype(kn_ref.dtype)
        v_ref[...] = kv[:, XA_WIDTH:].astype(v_ref.dtype)

    return pl.pallas_call(
        body, name=name,
        out_shape=[jax.ShapeDtypeStruct((M, D_MODEL), _MXU), jax.ShapeDtypeStruct((M, 2 * XA_WIDTH), F32),
                   jax.ShapeDtypeStruct((M, XA_WIDTH), _MXU), jax.ShapeDtypeStruct((M, XA_WIDTH), _MXU)],
        compiler_params=pltpu.CompilerParams(vmem_limit_bytes=VMEM_LIMIT),
    )(mem, g_mem.reshape(1, D_MODEL), wkv, gk.reshape(1, XA_HEAD_DIM))


def _xa_probs(qx, gq, kn_h):
    r = _rms_rows(qx)
    qhat = qx * r
    qn = qhat * gq
    s = _dot_nt(qn, kn_h) * (1.0 / math.sqrt(XA_HEAD_DIM))
    s = s - jnp.max(s, axis=-1, keepdims=True)
    e = jnp.exp(s)
    return qhat, r, qn, e / jnp.sum(e, axis=-1, keepdims=True)


def _xa_fwd(name, qx, gq, kn, vx):
    T, M = qx.shape[0], kn.shape[0]
    tm = _tile(T, 512, 16)

    def body(q_ref, gq_ref, k_ref, v_ref, o_ref):
        for h in range(XA_HEADS):
            sl = slice(h * LANES, (h + 1) * LANES)
            _, _, _, p = _xa_probs(q_ref[:, sl], gq_ref[...], k_ref[:, sl])
            o_ref[:, sl] = _dot(p, v_ref[:, sl]).astype(o_ref.dtype)

    full = lambda s: pl.BlockSpec(s, lambda i: tuple(0 for _ in s))
    return pl.pallas_call(
        body, name=name, grid=(T // tm,),
        in_specs=[pl.BlockSpec((tm, XA_WIDTH), lambda i: (i, 0)), full((1, LANES)), full((M, XA_WIDTH)), full((M, XA_WIDTH))],
        out_specs=pl.BlockSpec((tm, XA_WIDTH), lambda i: (i, 0)),
        out_shape=jax.ShapeDtypeStruct((T, XA_WIDTH), _MXU),
        compiler_params=_cp(("parallel",)),
    )(qx, gq.reshape(1, XA_HEAD_DIM), kn, vx)


def _ffn_up(name, h, wg, wu):
    T, K = h.shape
    N = wg.shape[1]
    tm, tn = _tile(T, 512, 16), _tile(N, 1536, LANES)

    def body(h_ref, wg_ref, wu_ref, a_ref, b_ref, s_ref):
        hv = h_ref[...]
        a, b = _dot(hv, wg_ref[...]), _dot(hv, wu_ref[...])
        a_ref[...] = a
        b_ref[...] = b
        s_ref[...] = (jax.nn.silu(a) * b).astype(s_ref.dtype)

    w_spec = pl.BlockSpec((K, tn), lambda j, i: (0, j))
    o_spec = pl.BlockSpec((tm, tn), lambda j, i: (i, j))
    return pl.pallas_call(
        body, name=name, grid=(N // tn, T // tm),
        in_specs=[pl.BlockSpec((tm, K), lambda j, i: (i, 0)), w_spec, w_spec],
        out_specs=[o_spec, o_spec, o_spec],
        out_shape=[jax.ShapeDtypeStruct((T, N), F32), jax.ShapeDtypeStruct((T, N), F32), jax.ShapeDtypeStruct((T, N), _MXU)],
        compiler_params=_cp(("parallel", "parallel")),
    )(h, wg, wu)


def _loss_grad(name, y, target):
    T, Dm = y.shape
    tm = _tile(T, 512, 8)

    def body(y_ref, t_ref, dy_ref, l_ref):
        e = y_ref[...] - t_ref[...]
        dy_ref[...] = e * (1.0 / Dm)

        @pl.when(pl.program_id(0) == 0)
        def _():
            l_ref[...] = jnp.zeros_like(l_ref)

        l_ref[...] += jnp.sum(e * e, axis=0, keepdims=True) * (0.5 / Dm)

    blk = pl.BlockSpec((tm, Dm), lambda i: (i, 0))
    return pl.pallas_call(
        body, name=name, grid=(T // tm,),
        in_specs=[blk, blk], out_specs=[blk, pl.BlockSpec((1, Dm), lambda i: (0, 0))],
        out_shape=[jax.ShapeDtypeStruct((T, Dm), F32), jax.ShapeDtypeStruct((1, Dm), F32)],
        compiler_params=_cp(("arbitrary",)),
    )(y, target)


def _pad_lanes(v, n=LANES):
    v = v.reshape(1, -1)
    return jnp.pad(v, ((0, 0), (0, n - v.shape[1])))


def _layer_fwd(l, x, mem, W):
    sv = {"x0": x}
    h1 = _norm_fwd(f"norm_mix{l}", x, W["g_mix"])
    sv["h1"] = h1
    if l % 2 == 0:
        z = _mm(f"e_in{l}", [(h1, W["w_in"])], "nn")
        ya, qn, kn, vv, c, ct = _even_pre(f"even_pre{l}", z, W["g_v"], W["w_s"], W["bs_cols"], W["g_qn2"], W["g_kn2"], W["b_f_pad"])
        yb, ybf, lse = _fox_fwd(f"fox_fwd{l}", qn, kn, vv, c, ct)
        sv.update(z=z, ya=ya, qn=qn, kn=kn, vv=vv, c=c, ct=ct, yb=yb, ybf=ybf, lse=lse)
        x1 = _mm(f"e_out{l}", [(ya, W["w_out"][:A_WIDTH]), (yb, W["w_out"][A_WIDTH:])], "nn", residual=x)
    else:
        z = _mm(f"o_in{l}", [(h1, W["w_in"])], "nn")
        ycat = _odd_fwd(f"odd_fwd{l}", z, W["w_pool"], W["s_pool"], W["conv_w8"])
        sv.update(z=z, ycat=ycat)
        x1 = _mm(f"o_out{l}", [(ycat, W["w_out"])], "nn", residual=x)
    sv["x1"] = x1
    h2 = _norm_fwd(f"norm_xa{l}", x1, W["g_xa"])
    qx = _mm(f"xa_q{l}", [(h2, W["wq"])], "nn")
    m, kv, kn_x, vx = _mem_fwd(f"mem_fwd{l}", mem, W["g_mem"], W["wkv"], W["gk"])
    o = _xa_fwd(f"xa_fwd{l}", qx, W["gq"], kn_x, vx)
    x2 = _mm(f"xa_o{l}", [(o, W["wo"])], "nn", residual=x1)
    sv.update(h2=h2, qx=qx, m=m, kv=kv, kn_x=kn_x, vx=vx, o=o, x2=x2)
    h3 = _norm_fwd(f"norm_ffn{l}", x2, W["g_ffn"])
    a, b, s = _ffn_up(f"ffn_up{l}", h3, W["w_gate"], W["w_up"])
    x3 = _mm(f"ffn_down{l}", [(s, W["w_down"])], "nn", residual=x2)
    sv.update(h3=h3, a=a, b=b, s=s)
    return x3, sv


def _layer_weights(l, P):
    i = l // 2
    W = {k: P[k][l] for k in ("g_mix", "g_xa", "g_mem", "g_ffn")}
    W.update(wq=_mx(P["xa_wq"][l]), wkv=_mx(P["xa_wkv"][l]), wo=_mx(P["xa_wo"][l]), gq=P["xa_gq"][l], gk=P["xa_gk"][l],
             w_gate=_mx(P["w_gate"][l]), w_up=_mx(P["w_up"][l]), w_down=_mx(P["w_down"][l]))
    if l % 2 == 0:
        w_in = jnp.pad(_mx(P["e_w_in"][i]), ((0, 0), (0, EVEN_IN_PAD - EVEN_IN)))
        W.update(w_in=w_in, g_v=P["e_g_v"][i].reshape(1, A_WIDTH), w_s=P["e_w_s"][i],
                 bs_cols=jnp.pad(P["e_b_s"][i].T, ((0, 0), (0, LANES - A_GROUPS))),
                 g_qn2=jnp.tile(P["e_g_qn"][i], 2).reshape(1, LANES), g_kn2=jnp.tile(P["e_g_kn"][i], 2).reshape(1, LANES),
                 b_f_pad=_pad_lanes(P["e_b_f"][i]), w_out=_mx(P["e_w_out"][i]))
    else:
        W.update(w_in=_mx(P["o_w_in"][i]), w_pool=_mx(P["o_w_pool"][i]), s_pool=P["o_s_pool"][i].reshape(1, POOL_WIDTH),
                 conv_w8=jnp.pad(P["o_conv_w"][i], ((0, SUBLANES - CONV_K), (0, 0))), w_out=_mx(P["o_w_out"][i]))
    return W


def _ffn_bwd_act(name, g, wd, a, b):
    T, N = a.shape
    K = g.shape[1]
    tm, tn = _tile(T, 512, 16), _tile(N, 1536, LANES)

    def body(g_ref, wd_ref, a_ref, b_ref, da_ref, db_ref):
        ds = _dot_nt(g_ref[...], wd_ref[...])
        av = a_ref[...]
        sig = jax.nn.sigmoid(av)
        da_ref[...] = (ds * b_ref[...] * (sig * (1.0 + av * (1.0 - sig)))).astype(da_ref.dtype)
        db_ref[...] = (ds * (av * sig)).astype(db_ref.dtype)

    o_spec = pl.BlockSpec((tm, tn), lambda j, i: (i, j))
    return pl.pallas_call(
        body, name=name, grid=(N // tn, T // tm),
        in_specs=[pl.BlockSpec((tm, K), lambda j, i: (i, 0)), pl.BlockSpec((tn, K), lambda j, i: (j, 0)), o_spec, o_spec],
        out_specs=[o_spec, o_spec],
        out_shape=[jax.ShapeDtypeStruct((T, N), _MXU), jax.ShapeDtypeStruct((T, N), _MXU)],
        compiler_params=_cp(("parallel", "parallel")),
    )(g, wd, a, b)


def _xa_bwd(name, qx, do, gq, kn, vx):
    T, M = qx.shape[0], kn.shape[0]
    tm = _tile(T, 512, 16)

    def body(q_ref, do_ref, gq_ref, k_ref, v_ref, dq_ref, dk_ref, dv_ref, dg_ref):
        @pl.when(pl.program_id(0) == 0)
        def _():
            dk_ref[...] = jnp.zeros_like(dk_ref)
            dv_ref[...] = jnp.zeros_like(dv_ref)
            dg_ref[...] = jnp.zeros_like(dg_ref)

        gqv = gq_ref[...]
        for h in range(XA_HEADS):
            sl = slice(h * LANES, (h + 1) * LANES)
            qhat, r, qn, p = _xa_probs(q_ref[:, sl], gqv, k_ref[:, sl])
            do_h = do_ref[:, sl]
            dp = _dot_nt(do_h, v_ref[:, sl])
            dv_ref[:, sl] += _dot_tn(p, do_h)
            dsm = p * (dp - jnp.sum(p * dp, axis=-1, keepdims=True)) * (1.0 / math.sqrt(XA_HEAD_DIM))
            dqn = _dot(dsm, k_ref[:, sl])
            dk_ref[:, sl] += _dot_tn(dsm, qn)
            dg_ref[...] += jnp.sum(dqn * qhat, axis=0, keepdims=True)
            dxh = dqn * gqv
            dq_ref[:, sl] = (r * (dxh - qhat * jnp.mean(dxh * qhat, axis=-1, keepdims=True))).astype(dq_ref.dtype)

    full = lambda s: pl.BlockSpec(s, lambda i: tuple(0 for _ in s))
    blk = pl.BlockSpec((tm, XA_WIDTH), lambda i: (i, 0))
    return pl.pallas_call(
        body, name=name, grid=(T // tm,),
        in_specs=[blk, blk, full((1, LANES)), full((M, XA_WIDTH)), full((M, XA_WIDTH))],
        out_specs=[blk, full((M, XA_WIDTH)), full((M, XA_WIDTH)), full((1, LANES))],
        out_shape=[jax.ShapeDtypeStruct((T, XA_WIDTH), _MXU), jax.ShapeDtypeStruct((M, XA_WIDTH), F32),
                   jax.ShapeDtypeStruct((M, XA_WIDTH), F32), jax.ShapeDtypeStruct((1, LANES), F32)],
        compiler_params=_cp(("arbitrary",)),
    )(qx, do, gq.reshape(1, XA_HEAD_DIM), kn, vx)


def _mem_bwd(name, mem, g_mem, m, kv, gk, dkn, dvx, wkv):
    M = mem.shape[0]

    def body(mem_ref, g_ref, m_ref, kv_ref, gk_ref, dkn_ref, dvx_ref, w_ref, dw_ref, dgk_ref, dgm_ref):
        gkv = gk_ref[...]
        dgk = jnp.zeros((1, LANES), F32)
        parts = []
        for h in range(XA_HEADS):
            sl = slice(h * LANES, (h + 1) * LANES)
            k = kv_ref[:, sl]
            r = _rms_rows(k)
            khat = k * r
            dkn_h = dkn_ref[:, sl]
            dgk = dgk + jnp.sum(dkn_h * khat, axis=0, keepdims=True)
            dxh = dkn_h * gkv
            parts.append(r * (dxh - khat * jnp.mean(dxh * khat, axis=-1, keepdims=True)))
        dgk_ref[...] = dgk
        dkv = jnp.concatenate(parts + [dvx_ref[...]], axis=1)
        dw_ref[...] = _dot_tn(m_ref[...], dkv)
        dm = _dot_nt(dkv, w_ref[...])
        x = mem_ref[...]
        dgm_ref[...] = jnp.sum(dm * (x * _rms_rows(x)), axis=0, keepdims=True)

    return pl.pallas_call(
        body, name=name,
        out_shape=[jax.ShapeDtypeStruct((D_MODEL, 2 * XA_WIDTH), F32), jax.ShapeDtypeStruct((1, LANES), F32),
                   jax.ShapeDtypeStruct((1, D_MODEL), F32)],
        compiler_params=pltpu.CompilerParams(vmem_limit_bytes=VMEM_LIMIT),
    )(mem, g_mem.reshape(1, D_MODEL), m, kv, gk.reshape(1, XA_HEAD_DIM), dkn, dvx, wkv)


def _fox_bwd_common(q2, k2, v2, dy2, yf2, c_ref, ct_ref, lse_ref, e, h, lo, qi, ki, tq):
    sel = lo if e == 0 else ~lo
    qh = jnp.where(sel, q2, jnp.zeros_like(q2))
    kh = jnp.where(sel, k2, jnp.zeros_like(k2))
    dyh = jnp.where(sel, dy2, 0.0)
    s = _fox_logits(qh, k2, c_ref[:, h:h + 1], ct_ref[h:h + 1, :], qi, ki, tq, tq)
    p = jnp.exp(s - lse_ref[:, h:h + 1])
    dp = _dot_nt(dyh, v2)
    delta = jnp.sum(dyh * yf2, axis=-1, keepdims=True)
    return p, p * (dp - delta), qh, kh, dyh


def _fox_bwd_dq(name, qn, kn, vv, c, ct, lse, dycat, ybf):
    T = qn.shape[0]
    tq = _tile(T, 512, LANES)
    nblk = T // tq
    qs, ks = _tri_steps(nblk, True)
    scale = 1.0 / math.sqrt(FOX_HEAD_DIM)

    def body(qs_ref, ks_ref, q_ref, k_ref, v_ref, c_ref, ct_ref, lse_ref, dy_ref, yf_ref, dq_ref, drow_ref):
        s_id = pl.program_id(0)
        qi, ki = qs_ref[s_id], ks_ref[s_id]

        @pl.when(ki == 0)
        def _():
            dq_ref[...] = jnp.zeros_like(dq_ref)
            drow_ref[...] = jnp.zeros_like(drow_ref)

        lo = _lane((tq, LANES)) < FOX_HEAD_DIM
        for j in range(FOX_WIDTH // LANES):
            sl = slice(j * LANES, (j + 1) * LANES)
            q2, k2, v2, dy2, yf2 = q_ref[:, sl], k_ref[:, sl], v_ref[:, sl], dy_ref[:, sl], yf_ref[:, sl]
            tot = None
            for e in range(2):
                h = 2 * j + e
                _, ds, _, kh, _ = _fox_bwd_common(q2, k2, v2, dy2, yf2, c_ref, ct_ref, lse_ref, e, h, lo, qi, ki, tq)
                drow_ref[...] += jnp.where(_lane((tq, LANES)) == h, jnp.sum(ds, axis=-1, keepdims=True), 0.0)
                part = _dot(ds, kh)
                tot = part if tot is None else tot + part
            dq_ref[:, sl] += tot * scale

    qmap = lambda s, a, b: (a[s], 0)
    kmap = lambda s, a, b: (b[s], 0)
    grid_spec = pltpu.PrefetchScalarGridSpec(
        num_scalar_prefetch=2, grid=(int(qs.shape[0]),),
        in_specs=[pl.BlockSpec((tq, 512), qmap), pl.BlockSpec((tq, 512), kmap), pl.BlockSpec((tq, 512), kmap),
                  pl.BlockSpec((tq, LANES), qmap), pl.BlockSpec((SUBLANES, tq), lambda s, a, b: (0, b[s])),
                  pl.BlockSpec((tq, LANES), qmap), pl.BlockSpec((tq, 512), lambda s, a, b: (a[s], 1)), pl.BlockSpec((tq, 512), qmap)],
        out_specs=[pl.BlockSpec((tq, 512), qmap), pl.BlockSpec((tq, LANES), qmap)],
    )
    return pl.pallas_call(
        body, name=name, grid_spec=grid_spec,
        out_shape=[jax.ShapeDtypeStruct((T, 512), F32), jax.ShapeDtypeStruct((T, LANES), F32)],
        compiler_params=_cp(("arbitrary",)),
    )(qs, ks, qn, kn, vv, c, ct, lse, dycat, ybf)


def _fox_bwd_dkv(name, qn, kn, vv, c, ct, lse, dycat, ybf):
    T = qn.shape[0]
    tq = _tile(T, 512, LANES)
    nblk = T // tq
    qs, ks = _tri_steps(nblk, False)
    scale = 1.0 / math.sqrt(FOX_HEAD_DIM)

    def body(qs_ref, ks_ref, q_ref, k_ref, v_ref, c_ref, ct_ref, lse_ref, dy_ref, yf_ref, dk_ref, dv_ref, dct_ref):
        s_id = pl.program_id(0)
        qi, ki = qs_ref[s_id], ks_ref[s_id]

        @pl.when(qi == ki)
        def _():
            dk_ref[...] = jnp.zeros_like(dk_ref)
            dv_ref[...] = jnp.zeros_like(dv_ref)
            dct_ref[...] = jnp.zeros_like(dct_ref)

        lo = _lane((tq, LANES)) < FOX_HEAD_DIM
        for j in range(FOX_WIDTH // LANES):
            sl = slice(j * LANES, (j + 1) * LANES)
            q2, k2, v2, dy2, yf2 = q_ref[:, sl], k_ref[:, sl], v_ref[:, sl], dy_ref[:, sl], yf_ref[:, sl]
            dk_t, dv_t = None, None
            for e in range(2):
                h = 2 * j + e
                p, ds, qh, _, dyh = _fox_bwd_common(q2, k2, v2, dy2, yf2, c_ref, ct_ref, lse_ref, e, h, lo, qi, ki, tq)
                a, b = _dot_tn(ds, qh), _dot_tn(p, dyh)
                dk_t = a if dk_t is None else dk_t + a
                dv_t = b if dv_t is None else dv_t + b
                dct_ref[h:h + 1, :] -= jnp.sum(ds, axis=0, keepdims=True)
            dk_ref[:, sl] += dk_t * scale
            dv_ref[:, sl] += dv_t

    qmap = lambda s, a, b: (a[s], 0)
    kmap = lambda s, a, b: (b[s], 0)
    ctmap = lambda s, a, b: (0, b[s])
    grid_spec = pltpu.PrefetchScalarGridSpec(
        num_scalar_prefetch=2, grid=(int(qs.shape[0]),),
        in_specs=[pl.BlockSpec((tq, 512), qmap), pl.BlockSpec((tq, 512), kmap), pl.BlockSpec((tq, 512), kmap),
                  pl.BlockSpec((tq, LANES), qmap), pl.BlockSpec((SUBLANES, tq), ctmap),
                  pl.BlockSpec((tq, LANES), qmap), pl.BlockSpec((tq, 512), lambda s, a, b: (a[s], 1)), pl.BlockSpec((tq, 512), qmap)],
        out_specs=[pl.BlockSpec((tq, 512), kmap), pl.BlockSpec((tq, 512), kmap), pl.BlockSpec((SUBLANES, tq), ctmap)],
    )
    return pl.pallas_call(
        body, name=name, grid_spec=grid_spec,
        out_shape=[jax.ShapeDtypeStruct((T, 512), F32), jax.ShapeDtypeStruct((T, 512), F32), jax.ShapeDtypeStruct((SUBLANES, T), F32)],
        compiler_params=_cp(("arbitrary",)),
    )(qs, ks, qn, kn, vv, c, ct, lse, dycat, ybf)


def _fold64(row):
    return row + pltpu.roll(row, FOX_HEAD_DIM, axis=1)


def _even_bwd(name, z, dycat, dqn, dkn, dvv, dct, drow, g_v, w_s, bs_cols, g_qn2, g_kn2, b_f_pad):
    T = z.shape[0]
    tm = _tile(T, 256, A_BLOCK)
    nb, nt = tm // A_BLOCK, T // tm

    def body(z_ref, dya_ref, dq_ref, dk_ref, dv_ref, dct_ref, drow_ref, gv_ref, ws_ref, bs_ref, gq_ref, gk_ref, bf_ref,
             dz_ref, dws_ref, dbs_ref, dgv_ref, dgq_ref, dgk_ref, dbf_ref, carry):
        @pl.when(pl.program_id(0) == 0)
        def _():
            for r in (dws_ref, dbs_ref, dgv_ref, dgq_ref, dgk_ref, dbf_ref, carry):
                r[...] = jnp.zeros_like(r)

        keep = (_row((A_BLOCK, A_BLOCK)) // CHUNK) >= (_lane((A_BLOCK, A_BLOCK)) // CHUNK)
        for g in range(A_GROUPS):
            sl = slice(g * A_GROUP_DIM, (g + 1) * A_GROUP_DIM)
            wm = _mx(_masked_ws(ws_ref, g))
            zu = z_ref[:, sl]
            zv = z_ref[:, A_WIDTH + g * A_GROUP_DIM:A_WIDTH + (g + 1) * A_GROUP_DIM]
            u, v = _gelu(zu), _gelu(zv)
            r = _rms_rows(v)
            vhat = v * r
            gvv = gv_ref[:, sl]
            vn = _mx(vhat * gvv)
            dya = dya_ref[:, sl]
            du_parts, dvn_parts = [], []
            for n in range(nb):
                rows = slice(n * A_BLOCK, (n + 1) * A_BLOCK)
                s = jnp.dot(wm, vn[rows], preferred_element_type=F32) + bs_ref[:, g:g + 1]
                du_parts.append(dya[rows] * s)
                d_s = dya[rows] * u[rows]
                dbs_ref[...] += jnp.where(_lane((A_BLOCK, LANES)) == g, jnp.sum(d_s, axis=-1, keepdims=True), 0.0)
                dws_ref[g] += jnp.where(keep, _dot_nt(d_s, vn[rows]), 0.0)
                dvn_parts.append(_dot_tn(wm, d_s))
            dvn = jnp.concatenate(dvn_parts, axis=0)
            dgv_ref[:, sl] += jnp.sum(dvn * vhat, axis=0, keepdims=True)
            dxh = dvn * gvv
            dv = r * (dxh - vhat * jnp.mean(dxh * vhat, axis=-1, keepdims=True))
            dz_ref[:, sl] = (jnp.concatenate(du_parts, axis=0) * _gelu_grad(zu)).astype(dz_ref.dtype)
            dz_ref[:, A_WIDTH + g * A_GROUP_DIM:A_WIDTH + (g + 1) * A_GROUP_DIM] = (dv * _gelu_grad(zv)).astype(dz_ref.dtype)
        o = 2 * A_WIDTH
        for j in range(FOX_WIDTH // LANES):
            sl = slice(j * LANES, (j + 1) * LANES)
            for (off, d_ref, g_ref, dg_ref) in ((o, dq_ref, gq_ref, dgq_ref), (o + FOX_WIDTH, dk_ref, gk_ref, dgk_ref)):
                zq = z_ref[:, off + j * LANES:off + (j + 1) * LANES]
                r = _head64_rms(zq)
                qhat = zq * r
                dn = d_ref[:, sl]
                dg_ref[...] += _fold64(jnp.sum(dn * qhat, axis=0, keepdims=True))
                dxh = dn * g_ref[...]
                dz_ref[:, off + j * LANES:off + (j + 1) * LANES] = (r * (dxh - qhat * _head64_mean(dxh * qhat))).astype(dz_ref.dtype)
            dz_ref[:, o + 2 * FOX_WIDTH + j * LANES:o + 2 * FOX_WIDTH + (j + 1) * LANES] = dv_ref[:, sl].astype(dz_ref.dtype)
        dct_v = dct_ref[...] + drow_ref[...].T[0:SUBLANES, :]
        upper = (_row((tm, tm)) >= _lane((tm, tm))).astype(F32)
        dlf_t = _dot_f32(dct_v, upper) + carry[:, 0:1]
        carry[...] += jnp.sum(dct_v, axis=1, keepdims=True)
        dlf = jnp.concatenate([dlf_t, jnp.zeros((LANES - SUBLANES, tm), F32)], axis=0).T
        zf = z_ref[:, o + 3 * FOX_WIDTH:o + 3 * FOX_WIDTH + LANES]
        dzf = dlf * jax.nn.sigmoid(-(zf + bf_ref[...]))
        dbf_ref[...] += jnp.sum(dzf, axis=0, keepdims=True)
        dz_ref[:, o + 3 * FOX_WIDTH:o + 3 * FOX_WIDTH + LANES] = dzf.astype(dz_ref.dtype)

    rev = lambda i: nt - 1 - i
    wide = lambda n, col=0: pl.BlockSpec((tm, n), lambda i: (rev(i), col))
    full = lambda s: pl.BlockSpec(s, lambda i: tuple(0 for _ in s))
    vec = full((1, LANES))
    return pl.pallas_call(
        body, name=name, grid=(nt,),
        in_specs=[wide(EVEN_IN_PAD), wide(512), wide(512), wide(512), wide(512), pl.BlockSpec((SUBLANES, tm), lambda i: (0, rev(i))),
                  wide(LANES), full((1, A_WIDTH)), full((A_GROUPS, A_BLOCK, A_BLOCK)), full((A_BLOCK, LANES)), vec, vec, vec],
        out_specs=[wide(EVEN_IN_PAD), full((A_GROUPS, A_BLOCK, A_BLOCK)), full((A_BLOCK, LANES)), full((1, A_WIDTH)), vec, vec, vec],
        out_shape=[jax.ShapeDtypeStruct((T, EVEN_IN_PAD), _MXU), jax.ShapeDtypeStruct((A_GROUPS, A_BLOCK, A_BLOCK), F32),
                   jax.ShapeDtypeStruct((A_BLOCK, LANES), F32), jax.ShapeDtypeStruct((1, A_WIDTH), F32),
                   jax.ShapeDtypeStruct((1, LANES), F32), jax.ShapeDtypeStruct((1, LANES), F32), jax.ShapeDtypeStruct((1, LANES), F32)],
        scratch_shapes=[pltpu.VMEM((SUBLANES, LANES), F32)],
        compiler_params=_cp(("arbitrary",)),
    )(z, dycat, dqn, dkn, dvv, dct, drow, g_v, w_s, bs_cols, g_qn2, g_kn2, b_f_pad)


def _shift_up(ext, k, tm):
    return pltpu.roll(ext, ext.shape[0] - k, axis=0)[0:tm]


def _odd_bwd(name, z, dycat, w_pool, s_pool, conv_w8):
    T = z.shape[0]
    tm = _tile(T, 256, HALO)
    r, nt = tm // HALO, T // tm
    n_ext = tm + HALO

    def body(z_ref, zb_ref, zn_ref, dy_ref, dyn_ref, wp_ref, sp_ref, cw_ref, dz_ref, dwp_ref, dsp_ref, dcw_ref):
        i = pl.program_id(0)

        @pl.when(i == 0)
        def _():
            for rr in (dwp_ref, dsp_ref, dcw_ref):
                rr[...] = jnp.zeros_like(rr)

        first, last = i == 0, i == nt - 1
        pos = i * tm + _row((tm, LANES))
        pos_ext = i * tm + _row((n_ext, LANES))
        for g, w in enumerate(POOL_WINDOWS):
            sl = slice(g * LANES, (g + 1) * LANES)
            z_ext = jnp.concatenate([jnp.where(first, 0.0, zb_ref[:, sl]), z_ref[:, sl]], axis=0)
            p = _pool_p(z_ext, g, tm, pos)
            spv = sp_ref[:, sl]
            dyc = dy_ref[:, sl]
            dsp_ref[:, sl] += jnp.sum(dyc * _dot(p, wp_ref[g]), axis=0, keepdims=True)
            dpw_ext = jnp.concatenate([dyc, jnp.where(last, 0.0, dyn_ref[:, sl])], axis=0) * spv
            dwp_ref[g] += _dot_tn(p, dpw_ext[0:tm])
            dp_ext = _dot_nt(dpw_ext, wp_ref[g])
            f = dp_ext / jnp.minimum(pos_ext + 1, w).astype(F32)
            span = 1
            while span < w:
                f = f + pltpu.roll(f, n_ext - span, axis=0)
                span *= 2
            dz_ref[:, sl] = (f[0:tm] - dp_ext[0:tm]).astype(dz_ref.dtype)
        for j in range(CONV_WIDTH // LANES):
            sl = slice(j * LANES, (j + 1) * LANES)
            hd = slice(POOL_WIDTH + j * LANES, POOL_WIDTH + (j + 1) * LANES)
            gb = slice(POOL_WIDTH + CONV_WIDTH + j * LANES, POOL_WIDTH + CONV_WIDTH + (j + 1) * LANES)
            gc = slice(POOL_WIDTH + 2 * CONV_WIDTH + j * LANES, POOL_WIDTH + 2 * CONV_WIDTH + (j + 1) * LANES)
            ysl = slice(POOL_WIDTH + j * LANES, POOL_WIDTH + (j + 1) * LANES)
            hdv, gbv, gcv = z_ref[:, hd], z_ref[:, gb], z_ref[:, gc]
            xg = gcv * hdv
            xg_ext = jnp.concatenate([jnp.where(first, 0.0, zb_ref[:, gc] * zb_ref[:, hd]), xg], axis=0)
            xg1, xg2 = _shift_down(xg_ext, 1, tm), _shift_down(xg_ext, 2, tm)
            w0, w1, w2 = cw_ref[0:1, sl], cw_ref[1:2, sl], cw_ref[2:3, sl]
            conv = w0 * xg2 + w1 * xg1 + w2 * xg
            dyd = dy_ref[:, ysl]
            dconv = dyd * gbv
            dconv_ext = jnp.concatenate([dconv, jnp.where(last, 0.0, dyn_ref[:, ysl] * zn_ref[:, gb])], axis=0)
            dcw_ref[0:1, sl] += jnp.sum(dconv * xg2, axis=0, keepdims=True)
            dcw_ref[1:2, sl] += jnp.sum(dconv * xg1, axis=0, keepdims=True)
            dcw_ref[2:3, sl] += jnp.sum(dconv * xg, axis=0, keepdims=True)
            dxg = w2 * dconv + w1 * _shift_up(dconv_ext, 1, tm) + w0 * _shift_up(dconv_ext, 2, tm)
            dz_ref[:, hd] = (dxg * gcv).astype(dz_ref.dtype)
            dz_ref[:, gb] = (dyd * conv).astype(dz_ref.dtype)
            dz_ref[:, gc] = (dxg * hdv).astype(dz_ref.dtype)

    full = lambda s: pl.BlockSpec(s, lambda i: tuple(0 for _ in s))
    back = lambda n: pl.BlockSpec((HALO, n), lambda i: (jnp.maximum(i * r - 1, 0), 0))
    nxt = lambda n: pl.BlockSpec((HALO, n), lambda i: (jnp.minimum((i + 1) * r, T // HALO - 1), 0))
    return pl.pallas_call(
        body, name=name, grid=(nt,),
        in_specs=[pl.BlockSpec((tm, ODD_IN), lambda i: (i, 0)), back(ODD_IN), nxt(ODD_IN),
                  pl.BlockSpec((tm, 1024), lambda i: (i, 0)), nxt(1024),
                  full((4, LANES, LANES)), full((1, POOL_WIDTH)), full((SUBLANES, CONV_WIDTH))],
        out_specs=[pl.BlockSpec((tm, ODD_IN), lambda i: (i, 0)), full((4, LANES, LANES)), full((1, POOL_WIDTH)), full((SUBLANES, CONV_WIDTH))],
        out_shape=[jax.ShapeDtypeStruct((T, ODD_IN), _MXU), jax.ShapeDtypeStruct((4, LANES, LANES), F32),
                   jax.ShapeDtypeStruct((1, POOL_WIDTH), F32), jax.ShapeDtypeStruct((SUBLANES, CONV_WIDTH), F32)],
        compiler_params=_cp(("arbitrary",)),
    )(z, z, z, dycat, dycat, w_pool, s_pool, conv_w8)


def _layer_bwd(l, g3, mem, W, sv):
    i = l // 2
    G = {}
    da, db = _ffn_bwd_act(f"ffn_bwd_act{l}", g3, W["w_down"], sv["a"], sv["b"])
    G["w_down"] = _mm(f"d_w_down{l}", [(sv["s"], g3)], "tn")
    dh3 = _mm(f"d_h3{l}", [(da, W["w_gate"]), (db, W["w_up"])], "nt")
    G["w_gate"] = _mm(f"d_w_gate{l}", [(sv["h3"], da)], "tn")
    G["w_up"] = _mm(f"d_w_up{l}", [(sv["h3"], db)], "tn")
    g2, dg = _norm_bwd(f"norm_ffn_bwd{l}", dh3, sv["x2"], W["g_ffn"], g3)
    G["g_ffn"] = dg[0]

    do = _mm(f"d_o{l}", [(g2, W["wo"])], "nt", out_dtype=_MXU)
    G["xa_wo"] = _mm(f"d_wo{l}", [(sv["o"], g2)], "tn")
    dqx, dkn, dvx, dgq = _xa_bwd(f"xa_bwd{l}", sv["qx"], do, W["gq"], sv["kn_x"], sv["vx"])
    G["xa_gq"] = dgq[0]
    dh2 = _mm(f"d_h2{l}", [(dqx, W["wq"])], "nt")
    G["xa_wq"] = _mm(f"d_wq{l}", [(sv["h2"], dqx)], "tn")
    g1, dg = _norm_bwd(f"norm_xa_bwd{l}", dh2, sv["x1"], W["g_xa"], g2)
    G["g_xa"] = dg[0]
    dwkv, dgk, dgm = _mem_bwd(f"mem_bwd{l}", mem, W["g_mem"], sv["m"], sv["kv"], W["gk"], dkn, dvx, W["wkv"])
    G["xa_wkv"], G["xa_gk"], G["g_mem"] = dwkv, dgk[0], dgm[0]

    dycat = _mm(f"d_ycat{l}", [(g1, W["w_out"])], "nt")
    if l % 2 == 0:
        G["e_w_out"] = jnp.concatenate([_mm(f"d_e_wout_a{l}", [(sv["ya"], g1)], "tn"), _mm(f"d_e_wout_b{l}", [(sv["yb"], g1)], "tn")], axis=0)
        fa = (sv["qn"], sv["kn"], sv["vv"], sv["c"], sv["ct"], sv["lse"], dycat, sv["ybf"])
        dqn, drow = _fox_bwd_dq(f"fox_bwd_dq{l}", *fa)
        dkn_f, dvv, dct = _fox_bwd_dkv(f"fox_bwd_dkv{l}", *fa)
        dz, dws, dbs, dgv, dgq2, dgk2, dbf = _even_bwd(f"even_bwd{l}", sv["z"], dycat, dqn, dkn_f, dvv, dct, drow, W["g_v"], W["w_s"],
                                                      W["bs_cols"], W["g_qn2"], W["g_kn2"], W["b_f_pad"])
        G.update(e_w_s=dws, e_b_s=dbs[:, :A_GROUPS].T, e_g_v=dgv[0], e_g_qn=dgq2[0, :FOX_HEAD_DIM], e_g_kn=dgk2[0, :FOX_HEAD_DIM],
                 e_b_f=dbf[0, :FOX_HEADS])
        G["e_w_in"] = _mm(f"d_e_win{l}", [(sv["h1"], dz)], "tn")[:, :EVEN_IN]
    else:
        G["o_w_out"] = _mm(f"d_o_wout{l}", [(sv["ycat"], g1)], "tn")
        dz, dwp, dsp, dcw = _odd_bwd(f"odd_bwd{l}", sv["z"], dycat, W["w_pool"], W["s_pool"], W["conv_w8"])
        G.update(o_w_pool=dwp, o_s_pool=dsp[0], o_conv_w=dcw[:CONV_K])
        G["o_w_in"] = _mm(f"d_o_win{l}", [(sv["h1"], dz)], "tn")
    dh1 = _mm(f"d_h1{l}", [(dz, W["w_in"])], "nt")
    g0, dg = _norm_bwd(f"norm_mix_bwd{l}", dh1, sv["x0"], W["g_mix"], g1)
    G["g_mix"] = dg[0]
    return g0, G


PACK_W = 1024
MESH_T = pl.DeviceIdType.MESH
_ANY = pl.BlockSpec(memory_space=pl.ANY)


def _my_place():
    x, y, c = lax.axis_index("x"), lax.axis_index("y"), lax.axis_index("c")
    return x, y, c


def _flip(v, bit):
    return 1 - v if bit else v


def _all_gather(name, blk):
    rows = blk.shape[0]

    def body(x_ref, out_ref, send_sems, recv_sems, local_sem):
        x, y, c = _my_place()
        me, sibling = (x, y, c), (x, y, 1 - c)
        chips = [(1 - x, y), (x, 1 - y), (1 - x, 1 - y)]

        def slab(px, py, pc):
            return out_ref.at[4 * px + 2 * py + pc]

        def copy(k, block, to, src=None):
            return pltpu.make_async_remote_copy(
                src_ref=slab(*block) if src is None else src, dst_ref=slab(*block),
                send_sem=send_sems.at[k], recv_sem=recv_sems.at[k], device_id=to, device_id_type=MESH_T)

        mine = pltpu.make_async_copy(x_ref, slab(*me), local_sem)
        mine.start()
        first = [copy(0, me, sibling, src=x_ref)]
        first += [copy(1 + j, me, (*chip, c), src=x_ref) for j, chip in enumerate(chips)]
        for cp in first:
            cp.start()
        passed = [copy(4 + j, (*chip, c), sibling) for j, chip in enumerate(chips)]
        for j, chip in enumerate(chips):
            copy(1 + j, (*chip, c), me).wait_recv()
            passed[j].start()
        copy(0, sibling, me).wait_recv()
        for j, chip in enumerate(chips):
            copy(4 + j, (*chip, 1 - c), me).wait_recv()
        for cp in first + passed:
            cp.wait_send()
        mine.wait()

    return pl.pallas_call(
        body, name=name,
        out_shape=jax.ShapeDtypeStruct((N_DEV, rows, PACK_W), blk.dtype),
        in_specs=[_ANY], out_specs=_ANY,
        scratch_shapes=[pltpu.SemaphoreType.DMA((7,)), pltpu.SemaphoreType.DMA((7,)), pltpu.SemaphoreType.DMA],
    )(blk)


def _all_to_all(name, g):
    rows = g.shape[1]

    def body(g_ref, out_ref, send_sems, recv_sems, local_sem):
        x, y, c = _my_place()
        me = 4 * x + 2 * y + c
        mine = pltpu.make_async_copy(g_ref.at[me], out_ref.at[me], local_sem)
        mine.start()
        sends, peers = [], []
        for m in range(1, N_DEV):
            px, py, pc = _flip(x, m & 4), _flip(y, m & 2), _flip(c, m & 1)
            pidx = 4 * px + 2 * py + pc
            cp = pltpu.make_async_remote_copy(
                src_ref=g_ref.at[pidx], dst_ref=out_ref.at[me], send_sem=send_sems.at[m - 1], recv_sem=recv_sems.at[m - 1],
                device_id=(px, py, pc), device_id_type=MESH_T)
            cp.start()
            sends.append(cp)
            peers.append(pidx)
        for m in range(1, N_DEV):
            pidx = peers[m - 1]
            pltpu.make_async_remote_copy(
                src_ref=g_ref.at[pidx], dst_ref=out_ref.at[pidx], send_sem=send_sems.at[m - 1], recv_sem=recv_sems.at[m - 1],
                device_id=(x, y, c), device_id_type=MESH_T).wait_recv()
        for cp in sends:
            cp.wait_send()
        mine.wait()

    return pl.pallas_call(
        body, name=name,
        out_shape=jax.ShapeDtypeStruct((N_DEV, rows, PACK_W), g.dtype),
        in_specs=[_ANY], out_specs=_ANY,
        scratch_shapes=[pltpu.SemaphoreType.DMA((7,)), pltpu.SemaphoreType.DMA((7,)), pltpu.SemaphoreType.DMA],
    )(g)


def _adamw(name, parts, w, m, v):
    rows = w.shape[0]
    tr = _tile(rows, 160, SUBLANES)

    def body(p_ref, w_ref, m_ref, v_ref, g_ref, d_ref, mo_ref, vo_ref):
        g = p_ref[0]
        for s in range(1, N_DEV):
            g = g + p_ref[s]
        mn = ADAM_B1 * m_ref[...] + (1.0 - ADAM_B1) * g
        vn = ADAM_B2 * v_ref[...] + (1.0 - ADAM_B2) * jnp.square(g)
        m_hat = mn / (1.0 - ADAM_B1 ** ADAM_STEP)
        v_hat = vn / (1.0 - ADAM_B2 ** ADAM_STEP)
        g_ref[...] = g
        d_ref[...] = -ADAM_LR * (m_hat / (jnp.sqrt(v_hat) + ADAM_EPS) + ADAM_WD * w_ref[...])
        mo_ref[...] = mn
        vo_ref[...] = vn

    blk = pl.BlockSpec((tr, PACK_W), lambda i: (i, 0))
    shp = jax.ShapeDtypeStruct((rows, PACK_W), F32)
    return pl.pallas_call(
        body, name=name, grid=(rows // tr,),
        in_specs=[pl.BlockSpec((N_DEV, tr, PACK_W), lambda i: (0, i, 0)), blk, blk, blk],
        out_specs=[blk, blk, blk, blk], out_shape=[shp, shp, shp, shp],
        compiler_params=_cp(("parallel",)),
    )(parts, w, m, v)


SHARDED = dict(xa_wq=1, xa_wkv=1, xa_wo=2, w_gate=2, w_up=2, w_down=1, e_w_in=2, e_w_out=1, o_w_in=2, o_s_pool=1, o_conv_w=2, o_w_out=1)
REPLICATED = ("g_mix", "g_xa", "g_mem", "xa_gq", "xa_gk", "g_ffn", "e_b_f", "e_g_v", "e_w_s", "e_b_s", "e_g_qn", "e_g_kn", "o_w_pool")
WEIGHTS = ("g_mix", "g_xa", "g_mem", "xa_wq", "xa_wkv", "xa_wo", "xa_gq", "xa_gk", "g_ffn", "w_gate", "w_up", "w_down", "e_w_in", "e_b_f",
           "e_g_v", "e_w_s", "e_b_s", "e_g_qn", "e_g_kn", "e_w_out", "o_w_in", "o_w_pool", "o_s_pool", "o_conv_w", "o_w_out")


def _rows_for(n, mult):
    return -(-n // (PACK_W * mult)) * mult


def _pack(arrs, rows, dtype):
    flat = jnp.concatenate([a.reshape(-1).astype(dtype) for a in arrs])
    return jnp.pad(flat, (0, rows * PACK_W - flat.shape[0])).reshape(rows, PACK_W)


def _unpack(slab, shapes):
    flat, out, off = slab.reshape(-1), [], 0
    for s in shapes:
        n = math.prod(s)
        out.append(flat[off:off + n].reshape(s))
        off += n
    return out


def _to_blocks(full, axis):
    s = full.shape
    return jnp.moveaxis(full.reshape(s[:axis] + (N_DEV, s[axis] // N_DEV) + s[axis + 1:]), axis, 0)


def _from_blocks(blocks, axis):
    b = jnp.moveaxis(blocks, 0, axis)
    s = b.shape
    return b.reshape(s[:axis] + (s[axis] * s[axis + 1],) + s[axis + 2:])


def kernel(x, mem, g_mix, g_xa, g_mem, xa_wq, xa_wkv, xa_wo, xa_gq, xa_gk, g_ffn, w_gate, w_up, w_down, e_w_in, e_b_f, e_g_v, e_w_s, e_b_s, e_g_qn, e_g_kn, e_w_out, o_w_in, o_w_pool, o_s_pool, o_conv_w, o_w_out, loss_target, m_g_mix, m_g_xa, m_g_mem, m_xa_wq, m_xa_wkv, m_xa_wo, m_xa_gq, m_xa_gk, m_g_ffn, m_w_gate, m_w_up, m_w_down, m_e_w_in, m_e_b_f, m_e_g_v, m_e_w_s, m_e_b_s, m_e_g_qn, m_e_g_kn, m_e_w_out, m_o_w_in, m_o_w_pool, m_o_s_pool, m_o_conv_w, m_o_w_out, v_g_mix, v_g_xa, v_g_mem, v_xa_wq, v_xa_wkv, v_xa_wo, v_xa_gq, v_xa_gk, v_g_ffn, v_w_gate, v_w_up, v_w_down, v_e_w_in, v_e_b_f, v_e_g_v, v_e_w_s, v_e_b_s, v_e_g_qn, v_e_g_kn, v_e_w_out, v_o_w_in, v_o_w_pool, v_o_s_pool, v_o_conv_w, v_o_w_out):
    args = dict(locals())
    Wt = {n: args[n] for n in WEIGHTS}
    Mo = {n: args["m_" + n] for n in WEIGHTS}
    Vo = {n: args["v_" + n] for n in WEIGHTS}
    sh_names, rep_names = tuple(SHARDED), REPLICATED
    sh_shapes = [Wt[n].shape for n in sh_names]
    rep_shapes = [Wt[n].shape for n in rep_names]
    sh_rows = _rows_for(sum(math.prod(s) for s in sh_shapes), 64)
    rep_rows = _rows_for(sum(math.prod(s) for s in rep_shapes) + 1, 16)

    gathered = _all_gather("gather_weights", _pack([Wt[n] for n in sh_names], sh_rows, _MXU))
    flat = gathered.reshape(N_DEV, -1)
    P, off = {n: Wt[n] for n in rep_names}, 0
    for n, s in zip(sh_names, sh_shapes):
        sz = math.prod(s)
        P[n] = _from_blocks(flat[:, off:off + sz].reshape((N_DEV,) + s), SHARDED[n])
        off += sz

    xk, mem2, svs, Ws = x[0], mem[0], [], []
    for l in range(2):
        Ws.append(_layer_weights(l, P))
        xk, sv = _layer_fwd(l, xk, mem2, Ws[l])
        svs.append(sv)
    g, loss_row = _loss_grad("loss_grad", xk, loss_target[0])
    G = {}
    for l in (1, 0):
        g, Gl = _layer_bwd(l, g, mem2, Ws[l], svs[l])
        for k2, v2 in Gl.items():
            G.setdefault(k2, {})[l] = v2
    grad_x = g[None]
    full_g = {n: jnp.stack([G[n][l] for l in sorted(G[n])]).reshape(P[n].shape) for n in WEIGHTS}

    send = jnp.concatenate([_to_blocks(full_g[n], SHARDED[n]).reshape(N_DEV, -1) for n in sh_names], axis=1)
    send = jnp.pad(send, ((0, 0), (0, sh_rows * PACK_W - send.shape[1]))).reshape(N_DEV, sh_rows, PACK_W)
    sh_parts = _all_to_all("exchange_grads", send)
    rep_parts = _all_gather("gather_small_grads", _pack([full_g[n] for n in rep_names] + [jnp.sum(loss_row).reshape(1)], rep_rows, F32))

    outs = {}
    for tag, parts, names, shapes, rows in (("sharded", sh_parts, sh_names, sh_shapes, sh_rows), ("replicated", rep_parts, rep_names, rep_shapes, rep_rows)):
        res = _adamw("adamw_" + tag, parts, _pack([Wt[n] for n in names], rows, F32), _pack([Mo[n] for n in names], rows, F32),
                     _pack([Vo[n] for n in names], rows, F32))
        for kind, slab in zip(("grad", "delta", "new_m", "new_v"), res):
            for n, a in zip(names, _unpack(slab, shapes)):
                outs[kind + "_" + n] = a
        if tag == "replicated":
            loss = res[0].reshape(-1)[sum(math.prod(s) for s in shapes)]
    return (loss, grad_x, *[outs[k + "_" + n] for k in ("grad", "delta", "new_m", "new_v") for n in WEIGHTS])
```

```python
import functools
import math

import numpy as np
import jax
import jax.numpy as jnp
from jax import lax
from jax.experimental import pallas as pl
from jax.experimental.pallas import tpu as pltpu

F32 = jnp.float32
BF16 = jnp.bfloat16
_MXU = jnp.bfloat16

D_MODEL = 1024
N_DEV = 8
LANES = 128
SUBLANES = 8
HALO = 16
CHUNK = 64
A_GROUPS, A_GROUP_DIM, A_WIDTH, A_BLOCK = 4, 128, 512, 128
FOX_HEADS, FOX_HEAD_DIM, FOX_WIDTH = 8, 64, 512
POOL_WINDOWS = (2, 4, 8, 16)
POOL_WIDTH, CONV_WIDTH, CONV_K = 512, 512, 3
EVEN_IN, EVEN_IN_PAD, ODD_IN = 2568, 2688, 2048
XA_HEADS, XA_HEAD_DIM, XA_WIDTH = 4, 128, 512
D_FF = 2816
EPS = 1e-6
ADAM_LR, ADAM_B1, ADAM_B2, ADAM_EPS, ADAM_WD, ADAM_STEP = 0.001, 0.9, 0.999, 1e-08, 0.01, 10
VMEM_LIMIT = 48 * 1024 * 1024


def _cp(sem):
    return pltpu.CompilerParams(dimension_semantics=sem, vmem_limit_bytes=VMEM_LIMIT)


def _tile(n, cap, q):
    best = None
    for d in range(q, min(n, cap) + 1, q):
        if n % d == 0:
            best = d
    return best if best is not None else n


def _mx(a):
    return a.astype(_MXU)


def _dot(a, b):
    return jnp.dot(_mx(a), _mx(b), preferred_element_type=F32)


def _dot_nt(a, b):
    return lax.dot_general(_mx(a), _mx(b), (((1,), (1,)), ((), ())), preferred_element_type=F32)


def _dot_tn(a, b):
    return lax.dot_general(_mx(a), _mx(b), (((0,), (0,)), ((), ())), preferred_element_type=F32)


def _dot_f32(a, b):
    return jnp.dot(a, b, precision=lax.Precision.HIGHEST, preferred_element_type=F32)


def _lane(shape):
    return lax.broadcasted_iota(jnp.int32, shape, len(shape) - 1)


def _row(shape):
    return lax.broadcasted_iota(jnp.int32, shape, len(shape) - 2)


def _mm(name, pairs, mode, out_dtype=F32, residual=None, tm_cap=512, tn_cap=1536, tk_cap=2048):
    a0, b0 = pairs[0]
    if mode == "nn":
        (M, K), N = a0.shape, b0.shape[1]
    elif mode == "nt":
        (M, K), N = a0.shape, b0.shape[0]
    else:
        (K, M), N = a0.shape, b0.shape[1]
    if mode == "tn":
        tm, tk = _tile(M, tm_cap, LANES), _tile(K, 512, 16)
    else:
        tm, tk = _tile(M, tm_cap, 16), _tile(K, tk_cap, LANES)
    tn = _tile(N, tn_cap, LANES)
    nk = K // tk
    npairs = len(pairs)
    dot = {"nn": _dot, "nt": _dot_nt, "tn": _dot_tn}[mode]

    def body(*refs):
        ab = refs[:2 * npairs]
        res_ref = refs[2 * npairs] if residual is not None else None
        o_ref, acc = refs[-2], refs[-1]
        k = pl.program_id(2)

        @pl.when(k == 0)
        def _():
            acc[...] = jnp.zeros_like(acc)

        for p in range(npairs):
            acc[...] += dot(ab[2 * p][...], ab[2 * p + 1][...])

        @pl.when(k == nk - 1)
        def _():
            out = acc[...]
            if res_ref is not None:
                out = out + res_ref[...]
            o_ref[...] = out.astype(o_ref.dtype)

    if mode == "nn":
        a_spec = pl.BlockSpec((tm, tk), lambda j, i, k: (i, k))
        b_spec = pl.BlockSpec((tk, tn), lambda j, i, k: (k, j))
    elif mode == "nt":
        a_spec = pl.BlockSpec((tm, tk), lambda j, i, k: (i, k))
        b_spec = pl.BlockSpec((tn, tk), lambda j, i, k: (j, k))
    else:
        a_spec = pl.BlockSpec((tk, tm), lambda j, i, k: (k, i))
        b_spec = pl.BlockSpec((tk, tn), lambda j, i, k: (k, j))
    o_spec = pl.BlockSpec((tm, tn), lambda j, i, k: (i, j))
    in_specs, args = [], []
    for a, b in pairs:
        in_specs += [a_spec, b_spec]
        args += [a, b]
    if residual is not None:
        in_specs.append(o_spec)
        args.append(residual)
    return pl.pallas_call(
        body, name=name, grid=(N // tn, M // tm, nk),
        in_specs=in_specs, out_specs=o_spec,
        out_shape=jax.ShapeDtypeStruct((M, N), out_dtype),
        scratch_shapes=[pltpu.VMEM((tm, tn), F32)],
        compiler_params=_cp(("parallel", "parallel", "arbitrary")),
    )(*args)


def _rms_rows(x):
    return lax.rsqrt(jnp.mean(x * x, axis=-1, keepdims=True) + EPS)


def _norm_fwd(name, x, g):
    T, Dm = x.shape
    tm = _tile(T, 512, 16)

    def body(x_ref, g_ref, o_ref):
        xv = x_ref[...]
        o_ref[...] = ((xv * _rms_rows(xv)) * g_ref[...]).astype(o_ref.dtype)

    return pl.pallas_call(
        body, name=name, grid=(T // tm,),
        in_specs=[pl.BlockSpec((tm, Dm), lambda i: (i, 0)), pl.BlockSpec((1, Dm), lambda i: (0, 0))],
        out_specs=pl.BlockSpec((tm, Dm), lambda i: (i, 0)),
        out_shape=jax.ShapeDtypeStruct((T, Dm), _MXU),
        compiler_params=_cp(("parallel",)),
    )(x, g.reshape(1, Dm))


def _norm_bwd(name, dh, x, g, res):
    T, Dm = x.shape
    tm = _tile(T, 512, 8)

    def body(dh_ref, x_ref, g_ref, res_ref, dx_ref, dg_ref):
        xv, dhv = x_ref[...], dh_ref[...]
        r = _rms_rows(xv)
        xhat = xv * r
        dxhat = dhv * g_ref[...]
        dx_ref[...] = res_ref[...] + r * (dxhat - xhat * jnp.mean(dxhat * xhat, axis=-1, keepdims=True))

        @pl.when(pl.program_id(0) == 0)
        def _():
            dg_ref[...] = jnp.zeros_like(dg_ref)

        dg_ref[...] += jnp.sum(dhv * xhat, axis=0, keepdims=True)

    blk = pl.BlockSpec((tm, Dm), lambda i: (i, 0))
    vec = pl.BlockSpec((1, Dm), lambda i: (0, 0))
    return pl.pallas_call(
        body, name=name, grid=(T // tm,),
        in_specs=[blk, blk, vec, blk], out_specs=[blk, vec],
        out_shape=[jax.ShapeDtypeStruct((T, Dm), F32), jax.ShapeDtypeStruct((1, Dm), F32)],
        compiler_params=_cp(("arbitrary",)),
    )(dh, x, g.reshape(1, Dm), res)


def _gelu(x):
    return jax.nn.gelu(x)


def _gelu_grad(x):
    c0, c1 = math.sqrt(2.0 / math.pi), 0.044715
    t = jnp.tanh(c0 * (x + c1 * x * x * x))
    return 0.5 * (1.0 + t) + 0.5 * x * (1.0 - t * t) * c0 * (1.0 + 3.0 * c1 * x * x)


def _head64_rms(x):
    lo = _lane(x.shape) < FOX_HEAD_DIM
    xx = x * x
    sa = jnp.sum(jnp.where(lo, xx, 0.0), axis=-1, keepdims=True)
    sb = jnp.sum(jnp.where(lo, 0.0, xx), axis=-1, keepdims=True)
    inv = 1.0 / FOX_HEAD_DIM
    return jnp.where(lo, lax.rsqrt(sa * inv + EPS), lax.rsqrt(sb * inv + EPS))


def _head64_mean(x):
    lo = _lane(x.shape) < FOX_HEAD_DIM
    sa = jnp.sum(jnp.where(lo, x, 0.0), axis=-1, keepdims=True)
    sb = jnp.sum(jnp.where(lo, 0.0, x), axis=-1, keepdims=True)
    return jnp.where(lo, sa, sb) * (1.0 / FOX_HEAD_DIM)


def _masked_ws(ws_ref, g):
    w = ws_ref[g]
    keep = (_row(w.shape) // CHUNK) >= (_lane(w.shape) // CHUNK)
    return jnp.where(keep, w, 0.0)


def _even_pre(name, z, g_v, w_s, bs_cols, g_qn2, g_kn2, b_f_pad):
    T = z.shape[0]
    tm = _tile(T, 256, A_BLOCK)
    nb = tm // A_BLOCK

    def body(z_ref, gv_ref, ws_ref, bs_ref, gq_ref, gk_ref, bf_ref, ya_ref, q_ref, k_ref, v_ref, c_ref, ct_ref, carry):
        i = pl.program_id(0)

        @pl.when(i == 0)
        def _():
            carry[...] = jnp.zeros_like(carry)

        wm = [_mx(_masked_ws(ws_ref, g)) for g in range(A_GROUPS)]
        for g in range(A_GROUPS):
            sl = slice(g * A_GROUP_DIM, (g + 1) * A_GROUP_DIM)
            u = _gelu(z_ref[:, sl])
            v = _gelu(z_ref[:, A_WIDTH + g * A_GROUP_DIM:A_WIDTH + (g + 1) * A_GROUP_DIM])
            vn = _mx((v * _rms_rows(v)) * gv_ref[:, sl])
            for n in range(nb):
                rows = slice(n * A_BLOCK, (n + 1) * A_BLOCK)
                s = jnp.dot(wm[g], vn[rows], preferred_element_type=F32) + bs_ref[:, g:g + 1]
                ya_ref[rows, sl] = (u[rows] * s).astype(ya_ref.dtype)
        o = 2 * A_WIDTH
        for j in range(FOX_WIDTH // LANES):
            sl = slice(j * LANES, (j + 1) * LANES)
            q = z_ref[:, o + j * LANES:o + (j + 1) * LANES]
            q_ref[:, sl] = ((q * _head64_rms(q)) * gq_ref[...]).astype(q_ref.dtype)
            k = z_ref[:, o + FOX_WIDTH + j * LANES:o + FOX_WIDTH + (j + 1) * LANES]
            k_ref[:, sl] = ((k * _head64_rms(k)) * gk_ref[...]).astype(k_ref.dtype)
            v_ref[:, sl] = z_ref[:, o + 2 * FOX_WIDTH + j * LANES:o + 2 * FOX_WIDTH + (j + 1) * LANES].astype(v_ref.dtype)
        zf = z_ref[:, o + 3 * FOX_WIDTH:o + 3 * FOX_WIDTH + LANES]
        logf = jnp.where(_lane(zf.shape) < FOX_HEADS, jax.nn.log_sigmoid(zf + bf_ref[...]), 0.0)
        tri = (_row((tm, tm)) >= _lane((tm, tm))).astype(F32)
        c = _dot_f32(tri, logf) + carry[0:1, :]
        c_ref[...] = c
        ct_ref[...] = c.T[0:SUBLANES, :]
        carry[0:1, :] = c[tm - 1:tm, :]

    wide = lambda n: pl.BlockSpec((tm, n), lambda i: (i, 0))
    full = lambda s: pl.BlockSpec(s, lambda i: tuple(0 for _ in s))
    out512 = jax.ShapeDtypeStruct((T, 512), _MXU)
    return pl.pallas_call(
        body, name=name, grid=(T // tm,),
        in_specs=[wide(EVEN_IN_PAD), full((1, A_WIDTH)), full((A_GROUPS, A_BLOCK, A_BLOCK)), full((A_BLOCK, LANES)),
                  full((1, LANES)), full((1, LANES)), full((1, LANES))],
        out_specs=[wide(512), wide(512), wide(512), wide(512), wide(LANES), pl.BlockSpec((SUBLANES, tm), lambda i: (0, i))],
        out_shape=[out512, out512, out512, out512, jax.ShapeDtypeStruct((T, LANES), F32), jax.ShapeDtypeStruct((SUBLANES, T), F32)],
        scratch_shapes=[pltpu.VMEM((SUBLANES, LANES), F32)],
        compiler_params=_cp(("arbitrary",)),
    )(z, g_v, w_s, bs_cols, g_qn2, g_kn2, b_f_pad)


def _tri_steps(n, by_rows):
    if by_rows:
        pairs = [(q, k) for q in range(n) for k in range(q + 1)]
    else:
        pairs = [(q, k) for k in range(n) for q in range(k, n)]
    return (jnp.asarray(np.array([p[0] for p in pairs], np.int32)), jnp.asarray(np.array([p[1] for p in pairs], np.int32)))


def _fox_logits(q, k, c_col, c_row, qi, ki, tq, tk):
    s = _dot_nt(q, k) * (1.0 / math.sqrt(FOX_HEAD_DIM)) + (c_col - c_row)
    allowed = (qi * tq + _row((tq, tk))) >= (ki * tk + _lane((tq, tk)))
    return jnp.where(allowed, s, -jnp.inf)


def _fox_fwd(name, qn, kn, vv, c, ct):
    T = qn.shape[0]
    tq = _tile(T, 512, LANES)
    nblk = T // tq
    qs, ks = _tri_steps(nblk, True)
    nsteps = int(qs.shape[0])
    npair = FOX_WIDTH // LANES

    def body(qs_ref, ks_ref, q_ref, k_ref, v_ref, c_ref, ct_ref, yb_ref, yf_ref, lse_ref, acc, m_s, l_s):
        s_id = pl.program_id(0)
        qi, ki = qs_ref[s_id], ks_ref[s_id]

        @pl.when(ki == 0)
        def _():
            acc[...] = jnp.zeros_like(acc)
            m_s[...] = jnp.full_like(m_s, -jnp.inf)
            l_s[...] = jnp.zeros_like(l_s)

        lo = _lane((tq, LANES)) < FOX_HEAD_DIM
        for j in range(npair):
            sl = slice(j * LANES, (j + 1) * LANES)
            q2, k2, v2 = q_ref[:, sl], k_ref[:, sl], v_ref[:, sl]
            outs = []
            for e in range(2):
                h = 2 * j + e
                qh = jnp.where(lo if e == 0 else ~lo, q2, jnp.zeros_like(q2))
                s = _fox_logits(qh, k2, c_ref[:, h:h + 1], ct_ref[h:h + 1, :], qi, ki, tq, tq)
                m_old = m_s[h]
                m_new = jnp.maximum(m_old, jnp.max(s, axis=-1, keepdims=True))
                alpha = jnp.exp(m_old - m_new)
                p = jnp.exp(s - m_new)
                l_s[h] = alpha * l_s[h] + jnp.sum(p, axis=-1, keepdims=True)
                m_s[h] = m_new
                outs.append((alpha, _dot(p, v2)))
            alpha2 = jnp.where(lo, outs[0][0], outs[1][0])
            acc[:, sl] = alpha2 * acc[:, sl] + jnp.where(lo, outs[0][1], outs[1][1])

        @pl.when(ki == qi)
        def _():
            lse = jnp.zeros((tq, LANES), F32)
            for j in range(npair):
                sl = slice(j * LANES, (j + 1) * LANES)
                inv = jnp.where(lo, 1.0 / l_s[2 * j], 1.0 / l_s[2 * j + 1])
                y = acc[:, sl] * inv
                yf_ref[:, sl] = y
                yb_ref[:, sl] = y.astype(yb_ref.dtype)
            for h in range(FOX_HEADS):
                lse = jnp.where(_lane((tq, LANES)) == h, m_s[h] + jnp.log(l_s[h]), lse)
            lse_ref[...] = lse

    qmap = lambda s, qs_r, ks_r: (qs_r[s], 0)
    kmap = lambda s, qs_r, ks_r: (ks_r[s], 0)
    grid_spec = pltpu.PrefetchScalarGridSpec(
        num_scalar_prefetch=2, grid=(nsteps,),
        in_specs=[pl.BlockSpec((tq, 512), qmap), pl.BlockSpec((tq, 512), kmap), pl.BlockSpec((tq, 512), kmap),
                  pl.BlockSpec((tq, LANES), qmap), pl.BlockSpec((SUBLANES, tq), lambda s, qs_r, ks_r: (0, ks_r[s]))],
        out_specs=[pl.BlockSpec((tq, 512), qmap), pl.BlockSpec((tq, 512), qmap), pl.BlockSpec((tq, LANES), qmap)],
        scratch_shapes=[pltpu.VMEM((tq, 512), F32), pltpu.VMEM((FOX_HEADS, tq, 1), F32), pltpu.VMEM((FOX_HEADS, tq, 1), F32)],
    )
    return pl.pallas_call(
        body, name=name, grid_spec=grid_spec,
        out_shape=[jax.ShapeDtypeStruct((T, 512), _MXU), jax.ShapeDtypeStruct((T, 512), F32), jax.ShapeDtypeStruct((T, LANES), F32)],
        compiler_params=_cp(("arbitrary",)),
    )(qs, ks, qn, kn, vv, c, ct)


def _shift_down(ext, k, tm):
    return pltpu.roll(ext, k, axis=0)[HALO:HALO + tm]


def _pool_p(z_ext, g, tm, pos):
    w = POOL_WINDOWS[g]
    s, span = z_ext, 1
    while span < w:
        s = s + pltpu.roll(s, span, axis=0)
        span *= 2
    cnt = jnp.minimum(pos + 1, w).astype(F32)
    return s[HALO:HALO + tm] / cnt - z_ext[HALO:HALO + tm]


def _odd_fwd(name, z, w_pool, s_pool, conv_w8):
    T = z.shape[0]
    tm = _tile(T, 256, HALO)
    r = tm // HALO

    def body(z_ref, zb_ref, wp_ref, sp_ref, cw_ref, y_ref):
        i = pl.program_id(0)
        first = i == 0
        pos = i * tm + _row((tm, LANES))
        for g in range(len(POOL_WINDOWS)):
            sl = slice(g * LANES, (g + 1) * LANES)
            halo = jnp.where(first, 0.0, zb_ref[:, sl])
            z_ext = jnp.concatenate([halo, z_ref[:, sl]], axis=0)
            p = _pool_p(z_ext, g, tm, pos)
            y_ref[:, sl] = (_dot(p, wp_ref[g]) * sp_ref[:, sl]).astype(y_ref.dtype)
        for j in range(CONV_WIDTH // LANES):
            sl = slice(j * LANES, (j + 1) * LANES)
            hd = slice(POOL_WIDTH + j * LANES, POOL_WIDTH + (j + 1) * LANES)
            gb = slice(POOL_WIDTH + CONV_WIDTH + j * LANES, POOL_WIDTH + CONV_WIDTH + (j + 1) * LANES)
            gc = slice(POOL_WIDTH + 2 * CONV_WIDTH + j * LANES, POOL_WIDTH + 2 * CONV_WIDTH + (j + 1) * LANES)
            xg_b = jnp.where(first, 0.0, zb_ref[:, gc] * zb_ref[:, hd])
            xg = jnp.concatenate([xg_b, z_ref[:, gc] * z_ref[:, hd]], axis=0)
            conv = (cw_ref[0:1, sl] * _shift_down(xg, 2, tm) + cw_ref[1:2, sl] * _shift_down(xg, 1, tm)
                    + cw_ref[2:3, sl] * xg[HALO:HALO + tm])
            y_ref[:, POOL_WIDTH + j * LANES:POOL_WIDTH + (j + 1) * LANES] = (z_ref[:, gb] * conv).astype(y_ref.dtype)

    full = lambda s: pl.BlockSpec(s, lambda i: tuple(0 for _ in s))
    return pl.pallas_call(
        body, name=name, grid=(T // tm,),
        in_specs=[pl.BlockSpec((tm, ODD_IN), lambda i: (i, 0)),
                  pl.BlockSpec((HALO, ODD_IN), lambda i: (jnp.maximum(i * r - 1, 0), 0)),
                  full((4, LANES, LANES)), full((1, POOL_WIDTH)), full((SUBLANES, CONV_WIDTH))],
        out_specs=pl.BlockSpec((tm, 1024), lambda i: (i, 0)),
        out_shape=jax.ShapeDtypeStruct((T, 1024), _MXU),
        compiler_params=_cp(("parallel",)),
    )(z, z, w_pool, s_pool, conv_w8)


def _mem_fwd(name, mem, g_mem, wkv, gk):
    M = mem.shape[0]

    def body(mem_ref, g_ref, w_ref, gk_ref, m_ref, kv_ref, kn_ref, v_ref):
        x = mem_ref[...]
        m = ((x * _rms_rows(x)) * g_ref[...]).astype(m_ref.dtype)
        m_ref[...] = m
        kv = _dot(m, w_ref[...])
        kv_ref[...] = kv
        for h in range(XA_HEADS):
            sl = slice(h * LANES, (h + 1) * LANES)
            k = kv[:, sl]
            kn_ref[:, sl] = ((k * _rms_rows(k)) * gk_ref[...]).astype(kn_ref.dtype)
        v_ref[...] = kv[:, XA_WIDTH:].astype(v_ref.dtype)

    return pl.pallas_call(
        body, name=name,
        out_shape=[jax.ShapeDtypeStruct((M, D_MODEL), _MXU), jax.ShapeDtypeStruct((M, 2 * XA_WIDTH), F32),
                   jax.ShapeDtypeStruct((M, XA_WIDTH), _MXU), jax.ShapeDtypeStruct((M, XA_WIDTH), _MXU)],
        compiler_params=pltpu.CompilerParams(vmem_limit_bytes=VMEM_LIMIT),
    )(mem, g_mem.reshape(1, D_MODEL), wkv, gk.reshape(1, XA_HEAD_DIM))


def _xa_probs(qx, gq, kn_h):
    r = _rms_rows(qx)
    qhat = qx * r
    qn = qhat * gq
    s = _dot_nt(qn, kn_h) * (1.0 / math.sqrt(XA_HEAD_DIM))
    s = s - jnp.max(s, axis=-1, keepdims=True)
    e = jnp.exp(s)
    return qhat, r, qn, e / jnp.sum(e, axis=-1, keepdims=True)


def _xa_fwd(name, qx, gq, kn, vx):
    T, M = qx.shape[0], kn.shape[0]
    tm = _tile(T, 512, 16)

    def body(q_ref, gq_ref, k_ref, v_ref, o_ref):
        for h in range(XA_HEADS):
            sl = slice(h * LANES, (h + 1) * LANES)
            _, _, _, p = _xa_probs(q_ref[:, sl], gq_ref[...], k_ref[:, sl])
            o_ref[:, sl] = _dot(p, v_ref[:, sl]).astype(o_ref.dtype)

    full = lambda s: pl.BlockSpec(s, lambda i: tuple(0 for _ in s))
    return pl.pallas_call(
        body, name=name, grid=(T // tm,),
        in_specs=[pl.BlockSpec((tm, XA_WIDTH), lambda i: (i, 0)), full((1, LANES)), full((M, XA_WIDTH)), full((M, XA_WIDTH))],
        out_specs=pl.BlockSpec((tm, XA_WIDTH), lambda i: (i, 0)),
        out_shape=jax.ShapeDtypeStruct((T, XA_WIDTH), _MXU),
        compiler_params=_cp(("parallel",)),
    )(qx, gq.reshape(1, XA_HEAD_DIM), kn, vx)


def _ffn_up(name, h, wg, wu):
    T, K = h.shape
    N = wg.shape[1]
    tm, tn = _tile(T, 512, 16), _tile(N, 1536, LANES)

    def body(h_ref, wg_ref, wu_ref, a_ref, b_ref, s_ref):
        hv = h_ref[...]
        a, b = _dot(hv, wg_ref[...]), _dot(hv, wu_ref[...])
        a_ref[...] = a
        b_ref[...] = b
        s_ref[...] = (jax.nn.silu(a) * b).astype(s_ref.dtype)

    w_spec = pl.BlockSpec((K, tn), lambda j, i: (0, j))
    o_spec = pl.BlockSpec((tm, tn), lambda j, i: (i, j))
    return pl.pallas_call(
        body, name=name, grid=(N // tn, T // tm),
        in_specs=[pl.BlockSpec((tm, K), lambda j, i: (i, 0)), w_spec, w_spec],
        out_specs=[o_spec, o_spec, o_spec],
        out_shape=[jax.ShapeDtypeStruct((T, N), F32), jax.ShapeDtypeStruct((T, N), F32), jax.ShapeDtypeStruct((T, N), _MXU)],
        compiler_params=_cp(("parallel", "parallel")),
    )(h, wg, wu)


def _loss_grad(name, y, target):
    T, Dm = y.shape
    tm = _tile(T, 512, 8)

    def body(y_ref, t_ref, dy_ref, l_ref):
        e = y_ref[...] - t_ref[...]
        dy_ref[...] = e * (1.0 / Dm)

        @pl.when(pl.program_id(0) == 0)
        def _():
            l_ref[...] = jnp.zeros_like(l_ref)

        l_ref[...] += jnp.sum(e * e, axis=0, keepdims=True) * (0.5 / Dm)

    blk = pl.BlockSpec((tm, Dm), lambda i: (i, 0))
    return pl.pallas_call(
        body, name=name, grid=(T // tm,),
        in_specs=[blk, blk], out_specs=[blk, pl.BlockSpec((1, Dm), lambda i: (0, 0))],
        out_shape=[jax.ShapeDtypeStruct((T, Dm), F32), jax.ShapeDtypeStruct((1, Dm), F32)],
        compiler_params=_cp(("arbitrary",)),
    )(y, target)


def _pad_lanes(v, n=LANES):
    v = v.reshape(1, -1)
    return jnp.pad(v, ((0, 0), (0, n - v.shape[1])))


def _layer_fwd(l, x, mem, W):
    sv = {"x0": x}
    h1 = _norm_fwd(f"norm_mix{l}", x, W["g_mix"])
    sv["h1"] = h1
    if l % 2 == 0:
        z = _mm(f"e_in{l}", [(h1, W["w_in"])], "nn")
        ya, qn, kn, vv, c, ct = _even_pre(f"even_pre{l}", z, W["g_v"], W["w_s"], W["bs_cols"], W["g_qn2"], W["g_kn2"], W["b_f_pad"])
        yb, ybf, lse = _fox_fwd(f"fox_fwd{l}", qn, kn, vv, c, ct)
        sv.update(z=z, ya=ya, qn=qn, kn=kn, vv=vv, c=c, ct=ct, yb=yb, ybf=ybf, lse=lse)
        x1 = _mm(f"e_out{l}", [(ya, W["w_out"][:A_WIDTH]), (yb, W["w_out"][A_WIDTH:])], "nn", residual=x)
    else:
        z = _mm(f"o_in{l}", [(h1, W["w_in"])], "nn")
        ycat = _odd_fwd(f"odd_fwd{l}", z, W["w_pool"], W["s_pool"], W["conv_w8"])
        sv.update(z=z, ycat=ycat)
        x1 = _mm(f"o_out{l}", [(ycat, W["w_out"])], "nn", residual=x)
    sv["x1"] = x1
    h2 = _norm_fwd(f"norm_xa{l}", x1, W["g_xa"])
    qx = _mm(f"xa_q{l}", [(h2, W["wq"])], "nn")
    m, kv, kn_x, vx = _mem_fwd(f"mem_fwd{l}", mem, W["g_mem"], W["wkv"], W["gk"])
    o = _xa_fwd(f"xa_fwd{l}", qx, W["gq"], kn_x, vx)
    x2 = _mm(f"xa_o{l}", [(o, W["wo"])], "nn", residual=x1)
    sv.update(h2=h2, qx=qx, m=m, kv=kv, kn_x=kn_x, vx=vx, o=o, x2=x2)
    h3 = _norm_fwd(f"norm_ffn{l}", x2, W["g_ffn"])
    a, b, s = _ffn_up(f"ffn_up{l}", h3, W["w_gate"], W["w_up"])
    x3 = _mm(f"ffn_down{l}", [(s, W["w_down"])], "nn", residual=x2)
    sv.update(h3=h3, a=a, b=b, s=s)
    return x3, sv


def _layer_weights(l, P):
    i = l // 2
    W = {k: P[k][l] for k in ("g_mix", "g_xa", "g_mem", "g_ffn")}
    W.update(wq=_mx(P["xa_wq"][l]), wkv=_mx(P["xa_wkv"][l]), wo=_mx(P["xa_wo"][l]), gq=P["xa_gq"][l], gk=P["xa_gk"][l],
             w_gate=_mx(P["w_gate"][l]), w_up=_mx(P["w_up"][l]), w_down=_mx(P["w_down"][l]))
    if l % 2 == 0:
        w_in = _mx(P["e_w_in"][i])
        if w_in.shape[1] < EVEN_IN_PAD:
            w_in = jnp.pad(w_in, ((0, 0), (0, EVEN_IN_PAD - w_in.shape[1])))
        W.update(w_in=w_in, g_v=P["e_g_v"][i].reshape(1, A_WIDTH), w_s=P["e_w_s"][i],
                 bs_cols=jnp.pad(P["e_b_s"][i].T, ((0, 0), (0, LANES - A_GROUPS))),
                 g_qn2=jnp.tile(P["e_g_qn"][i], 2).reshape(1, LANES), g_kn2=jnp.tile(P["e_g_kn"][i], 2).reshape(1, LANES),
                 b_f_pad=_pad_lanes(P["e_b_f"][i]), w_out=_mx(P["e_w_out"][i]))
    else:
        W.update(w_in=_mx(P["o_w_in"][i]), w_pool=_mx(P["o_w_pool"][i]), s_pool=P["o_s_pool"][i].reshape(1, POOL_WIDTH),
                 conv_w8=jnp.pad(P["o_conv_w"][i], ((0, SUBLANES - CONV_K), (0, 0))), w_out=_mx(P["o_w_out"][i]))
    return W


def _ffn_bwd_act(name, g, wd, a, b):
    T, N = a.shape
    K = g.shape[1]
    tm, tn = _tile(T, 512, 16), _tile(N, 1536, LANES)

    def body(g_ref, wd_ref, a_ref, b_ref, da_ref, db_ref):
        ds = _dot_nt(g_ref[...], wd_ref[...])
        av = a_ref[...]
        sig = jax.nn.sigmoid(av)
        da_ref[...] = (ds * b_ref[...] * (sig * (1.0 + av * (1.0 - sig)))).astype(da_ref.dtype)
        db_ref[...] = (ds * (av * sig)).astype(db_ref.dtype)

    o_spec = pl.BlockSpec((tm, tn), lambda j, i: (i, j))
    return pl.pallas_call(
        body, name=name, grid=(N // tn, T // tm),
        in_specs=[pl.BlockSpec((tm, K), lambda j, i: (i, 0)), pl.BlockSpec((tn, K), lambda j, i: (j, 0)), o_spec, o_spec],
        out_specs=[o_spec, o_spec],
        out_shape=[jax.ShapeDtypeStruct((T, N), _MXU), jax.ShapeDtypeStruct((T, N), _MXU)],
        compiler_params=_cp(("parallel", "parallel")),
    )(g, wd, a, b)


def _xa_bwd(name, qx, do, gq, kn, vx):
    T, M = qx.shape[0], kn.shape[0]
    tm = _tile(T, 512, 16)

    def body(q_ref, do_ref, gq_ref, k_ref, v_ref, dq_ref, dk_ref, dv_ref, dg_ref):
        @pl.when(pl.program_id(0) == 0)
        def _():
            dk_ref[...] = jnp.zeros_like(dk_ref)
            dv_ref[...] = jnp.zeros_like(dv_ref)
            dg_ref[...] = jnp.zeros_like(dg_ref)

        gqv = gq_ref[...]
        for h in range(XA_HEADS):
            sl = slice(h * LANES, (h + 1) * LANES)
            qhat, r, qn, p = _xa_probs(q_ref[:, sl], gqv, k_ref[:, sl])
            do_h = do_ref[:, sl]
            dp = _dot_nt(do_h, v_ref[:, sl])
            dv_ref[:, sl] += _dot_tn(p, do_h)
            dsm = p * (dp - jnp.sum(p * dp, axis=-1, keepdims=True)) * (1.0 / math.sqrt(XA_HEAD_DIM))
            dqn = _dot(dsm, k_ref[:, sl])
            dk_ref[:, sl] += _dot_tn(dsm, qn)
            dg_ref[...] += jnp.sum(dqn * qhat, axis=0, keepdims=True)
            dxh = dqn * gqv
            dq_ref[:, sl] = (r * (dxh - qhat * jnp.mean(dxh * qhat, axis=-1, keepdims=True))).astype(dq_ref.dtype)

    full = lambda s: pl.BlockSpec(s, lambda i: tuple(0 for _ in s))
    blk = pl.BlockSpec((tm, XA_WIDTH), lambda i: (i, 0))
    return pl.pallas_call(
        body, name=name, grid=(T // tm,),
        in_specs=[blk, blk, full((1, LANES)), full((M, XA_WIDTH)), full((M, XA_WIDTH))],
        out_specs=[blk, full((M, XA_WIDTH)), full((M, XA_WIDTH)), full((1, LANES))],
        out_shape=[jax.ShapeDtypeStruct((T, XA_WIDTH), _MXU), jax.ShapeDtypeStruct((M, XA_WIDTH), F32),
                   jax.ShapeDtypeStruct((M, XA_WIDTH), F32), jax.ShapeDtypeStruct((1, LANES), F32)],
        compiler_params=_cp(("arbitrary",)),
    )(qx, do, gq.reshape(1, XA_HEAD_DIM), kn, vx)


def _mem_bwd(name, mem, g_mem, m, kv, gk, dkn, dvx, wkv):
    M = mem.shape[0]

    def body(mem_ref, g_ref, m_ref, kv_ref, gk_ref, dkn_ref, dvx_ref, w_ref, dw_ref, dgk_ref, dgm_ref):
        gkv = gk_ref[...]
        dgk = jnp.zeros((1, LANES), F32)
        parts = []
        for h in range(XA_HEADS):
            sl = slice(h * LANES, (h + 1) * LANES)
            k = kv_ref[:, sl]
            r = _rms_rows(k)
            khat = k * r
            dkn_h = dkn_ref[:, sl]
            dgk = dgk + jnp.sum(dkn_h * khat, axis=0, keepdims=True)
            dxh = dkn_h * gkv
            parts.append(r * (dxh - khat * jnp.mean(dxh * khat, axis=-1, keepdims=True)))
        dgk_ref[...] = dgk
        dkv = jnp.concatenate(parts + [dvx_ref[...]], axis=1)
        dw_ref[...] = _dot_tn(m_ref[...], dkv)
        dm = _dot_nt(dkv, w_ref[...])
        x = mem_ref[...]
        dgm_ref[...] = jnp.sum(dm * (x * _rms_rows(x)), axis=0, keepdims=True)

    return pl.pallas_call(
        body, name=name,
        out_shape=[jax.ShapeDtypeStruct((D_MODEL, 2 * XA_WIDTH), F32), jax.ShapeDtypeStruct((1, LANES), F32),
                   jax.ShapeDtypeStruct((1, D_MODEL), F32)],
        compiler_params=pltpu.CompilerParams(vmem_limit_bytes=VMEM_LIMIT),
    )(mem, g_mem.reshape(1, D_MODEL), m, kv, gk.reshape(1, XA_HEAD_DIM), dkn, dvx, wkv)


def _fox_bwd_common(q2, k2, v2, dy2, yf2, c_ref, ct_ref, lse_ref, e, h, lo, qi, ki, tq):
    sel = lo if e == 0 else ~lo
    qh = jnp.where(sel, q2, jnp.zeros_like(q2))
    kh = jnp.where(sel, k2, jnp.zeros_like(k2))
    dyh = jnp.where(sel, dy2, 0.0)
    s = _fox_logits(qh, k2, c_ref[:, h:h + 1], ct_ref[h:h + 1, :], qi, ki, tq, tq)
    p = jnp.exp(s - lse_ref[:, h:h + 1])
    dp = _dot_nt(dyh, v2)
    delta = jnp.sum(dyh * yf2, axis=-1, keepdims=True)
    return p, p * (dp - delta), qh, kh, dyh


def _fox_bwd_dq(name, qn, kn, vv, c, ct, lse, dycat, ybf):
    T = qn.shape[0]
    tq = _tile(T, 512, LANES)
    nblk = T // tq
    qs, ks = _tri_steps(nblk, True)
    scale = 1.0 / math.sqrt(FOX_HEAD_DIM)

    def body(qs_ref, ks_ref, q_ref, k_ref, v_ref, c_ref, ct_ref, lse_ref, dy_ref, yf_ref, dq_ref, drow_ref):
        s_id = pl.program_id(0)
        qi, ki = qs_ref[s_id], ks_ref[s_id]

        @pl.when(ki == 0)
        def _():
            dq_ref[...] = jnp.zeros_like(dq_ref)
            drow_ref[...] = jnp.zeros_like(drow_ref)

        lo = _lane((tq, LANES)) < FOX_HEAD_DIM
        for j in range(FOX_WIDTH // LANES):
            sl = slice(j * LANES, (j + 1) * LANES)
            q2, k2, v2, dy2, yf2 = q_ref[:, sl], k_ref[:, sl], v_ref[:, sl], dy_ref[:, sl], yf_ref[:, sl]
            tot = None
            for e in range(2):
                h = 2 * j + e
                _, ds, _, kh, _ = _fox_bwd_common(q2, k2, v2, dy2, yf2, c_ref, ct_ref, lse_ref, e, h, lo, qi, ki, tq)
                drow_ref[...] += jnp.where(_lane((tq, LANES)) == h, jnp.sum(ds, axis=-1, keepdims=True), 0.0)
                part = _dot(ds, kh)
                tot = part if tot is None else tot + part
            dq_ref[:, sl] += tot * scale

    qmap = lambda s, a, b: (a[s], 0)
    kmap = lambda s, a, b: (b[s], 0)
    grid_spec = pltpu.PrefetchScalarGridSpec(
        num_scalar_prefetch=2, grid=(int(qs.shape[0]),),
        in_specs=[pl.BlockSpec((tq, 512), qmap), pl.BlockSpec((tq, 512), kmap), pl.BlockSpec((tq, 512), kmap),
                  pl.BlockSpec((tq, LANES), qmap), pl.BlockSpec((SUBLANES, tq), lambda s, a, b: (0, b[s])),
                  pl.BlockSpec((tq, LANES), qmap), pl.BlockSpec((tq, 512), lambda s, a, b: (a[s], 1)), pl.BlockSpec((tq, 512), qmap)],
        out_specs=[pl.BlockSpec((tq, 512), qmap), pl.BlockSpec((tq, LANES), qmap)],
    )
    return pl.pallas_call(
        body, name=name, grid_spec=grid_spec,
        out_shape=[jax.ShapeDtypeStruct((T, 512), F32), jax.ShapeDtypeStruct((T, LANES), F32)],
        compiler_params=_cp(("arbitrary",)),
    )(qs, ks, qn, kn, vv, c, ct, lse, dycat, ybf)


def _fox_bwd_dkv(name, qn, kn, vv, c, ct, lse, dycat, ybf):
    T = qn.shape[0]
    tq = _tile(T, 512, LANES)
    nblk = T // tq
    qs, ks = _tri_steps(nblk, False)
    scale = 1.0 / math.sqrt(FOX_HEAD_DIM)

    def body(qs_ref, ks_ref, q_ref, k_ref, v_ref, c_ref, ct_ref, lse_ref, dy_ref, yf_ref, dk_ref, dv_ref, dct_ref):
        s_id = pl.program_id(0)
        qi, ki = qs_ref[s_id], ks_ref[s_id]

        @pl.when(qi == ki)
        def _():
            dk_ref[...] = jnp.zeros_like(dk_ref)
            dv_ref[...] = jnp.zeros_like(dv_ref)
            dct_ref[...] = jnp.zeros_like(dct_ref)

        lo = _lane((tq, LANES)) < FOX_HEAD_DIM
        for j in range(FOX_WIDTH // LANES):
            sl = slice(j * LANES, (j + 1) * LANES)
            q2, k2, v2, dy2, yf2 = q_ref[:, sl], k_ref[:, sl], v_ref[:, sl], dy_ref[:, sl], yf_ref[:, sl]
            dk_t, dv_t = None, None
            for e in range(2):
                h = 2 * j + e
                p, ds, qh, _, dyh = _fox_bwd_common(q2, k2, v2, dy2, yf2, c_ref, ct_ref, lse_ref, e, h, lo, qi, ki, tq)
                a, b = _dot_tn(ds, qh), _dot_tn(p, dyh)
                dk_t = a if dk_t is None else dk_t + a
                dv_t = b if dv_t is None else dv_t + b
                dct_ref[h:h + 1, :] -= jnp.sum(ds, axis=0, keepdims=True)
            dk_ref[:, sl] += dk_t * scale
            dv_ref[:, sl] += dv_t

    qmap = lambda s, a, b: (a[s], 0)
    kmap = lambda s, a, b: (b[s], 0)
    ctmap = lambda s, a, b: (0, b[s])
    grid_spec = pltpu.PrefetchScalarGridSpec(
        num_scalar_prefetch=2, grid=(int(qs.shape[0]),),
        in_specs=[pl.BlockSpec((tq, 512), qmap), pl.BlockSpec((tq, 512), kmap), pl.BlockSpec((tq, 512), kmap),
                  pl.BlockSpec((tq, LANES), qmap), pl.BlockSpec((SUBLANES, tq), ctmap),
                  pl.BlockSpec((tq, LANES), qmap), pl.BlockSpec((tq, 512), lambda s, a, b: (a[s], 1)), pl.BlockSpec((tq, 512), qmap)],
        out_specs=[pl.BlockSpec((tq, 512), kmap), pl.BlockSpec((tq, 512), kmap), pl.BlockSpec((SUBLANES, tq), ctmap)],
    )
    return pl.pallas_call(
        body, name=name, grid_spec=grid_spec,
        out_shape=[jax.ShapeDtypeStruct((T, 512), F32), jax.ShapeDtypeStruct((T, 512), F32), jax.ShapeDtypeStruct((SUBLANES, T), F32)],
        compiler_params=_cp(("arbitrary",)),
    )(qs, ks, qn, kn, vv, c, ct, lse, dycat, ybf)


def _fold64(row):
    return row + pltpu.roll(row, FOX_HEAD_DIM, axis=1)


def _even_bwd(name, z, dycat, dqn, dkn, dvv, dct, drow, g_v, w_s, bs_cols, g_qn2, g_kn2, b_f_pad):
    T = z.shape[0]
    tm = _tile(T, 256, A_BLOCK)
    nb, nt = tm // A_BLOCK, T // tm

    def body(z_ref, dya_ref, dq_ref, dk_ref, dv_ref, dct_ref, drow_ref, gv_ref, ws_ref, bs_ref, gq_ref, gk_ref, bf_ref,
             dz_ref, dws_ref, dbs_ref, dgv_ref, dgq_ref, dgk_ref, dbf_ref, carry):
        @pl.when(pl.program_id(0) == 0)
        def _():
            for r in (dws_ref, dbs_ref, dgv_ref, dgq_ref, dgk_ref, dbf_ref, carry):
                r[...] = jnp.zeros_like(r)

        keep = (_row((A_BLOCK, A_BLOCK)) // CHUNK) >= (_lane((A_BLOCK, A_BLOCK)) // CHUNK)
        for g in range(A_GROUPS):
            sl = slice(g * A_GROUP_DIM, (g + 1) * A_GROUP_DIM)
            wm = _mx(_masked_ws(ws_ref, g))
            zu = z_ref[:, sl]
            zv = z_ref[:, A_WIDTH + g * A_GROUP_DIM:A_WIDTH + (g + 1) * A_GROUP_DIM]
            u, v = _gelu(zu), _gelu(zv)
            r = _rms_rows(v)
            vhat = v * r
            gvv = gv_ref[:, sl]
            vn = _mx(vhat * gvv)
            dya = dya_ref[:, sl]
            du_parts, dvn_parts = [], []
            for n in range(nb):
                rows = slice(n * A_BLOCK, (n + 1) * A_BLOCK)
                s = jnp.dot(wm, vn[rows], preferred_element_type=F32) + bs_ref[:, g:g + 1]
                du_parts.append(dya[rows] * s)
                d_s = dya[rows] * u[rows]
                dbs_ref[...] += jnp.where(_lane((A_BLOCK, LANES)) == g, jnp.sum(d_s, axis=-1, keepdims=True), 0.0)
                dws_ref[g] += jnp.where(keep, _dot_nt(d_s, vn[rows]), 0.0)
                dvn_parts.append(_dot_tn(wm, d_s))
            dvn = jnp.concatenate(dvn_parts, axis=0)
            dgv_ref[:, sl] += jnp.sum(dvn * vhat, axis=0, keepdims=True)
            dxh = dvn * gvv
            dv = r * (dxh - vhat * jnp.mean(dxh * vhat, axis=-1, keepdims=True))
            dz_ref[:, sl] = (jnp.concatenate(du_parts, axis=0) * _gelu_grad(zu)).astype(dz_ref.dtype)
            dz_ref[:, A_WIDTH + g * A_GROUP_DIM:A_WIDTH + (g + 1) * A_GROUP_DIM] = (dv * _gelu_grad(zv)).astype(dz_ref.dtype)
        o = 2 * A_WIDTH
        for j in range(FOX_WIDTH // LANES):
            sl = slice(j * LANES, (j + 1) * LANES)
            for (off, d_ref, g_ref, dg_ref) in ((o, dq_ref, gq_ref, dgq_ref), (o + FOX_WIDTH, dk_ref, gk_ref, dgk_ref)):
                zq = z_ref[:, off + j * LANES:off + (j + 1) * LANES]
                r = _head64_rms(zq)
                qhat = zq * r
                dn = d_ref[:, sl]
                dg_ref[...] += _fold64(jnp.sum(dn * qhat, axis=0, keepdims=True))
                dxh = dn * g_ref[...]
                dz_ref[:, off + j * LANES:off + (j + 1) * LANES] = (r * (dxh - qhat * _head64_mean(dxh * qhat))).astype(dz_ref.dtype)
            dz_ref[:, o + 2 * FOX_WIDTH + j * LANES:o + 2 * FOX_WIDTH + (j + 1) * LANES] = dv_ref[:, sl].astype(dz_ref.dtype)
        dct_v = dct_ref[...] + drow_ref[...].T[0:SUBLANES, :]
        upper = (_row((tm, tm)) >= _lane((tm, tm))).astype(F32)
        dlf_t = _dot_f32(dct_v, upper) + carry[:, 0:1]
        carry[...] += jnp.sum(dct_v, axis=1, keepdims=True)
        dlf = jnp.concatenate([dlf_t, jnp.zeros((LANES - SUBLANES, tm), F32)], axis=0).T
        zf = z_ref[:, o + 3 * FOX_WIDTH:o + 3 * FOX_WIDTH + LANES]
        dzf = dlf * jax.nn.sigmoid(-(zf + bf_ref[...]))
        dbf_ref[...] += jnp.sum(dzf, axis=0, keepdims=True)
        dz_ref[:, o + 3 * FOX_WIDTH:o + 3 * FOX_WIDTH + LANES] = dzf.astype(dz_ref.dtype)

    rev = lambda i: nt - 1 - i
    wide = lambda n, col=0: pl.BlockSpec((tm, n), lambda i: (rev(i), col))
    full = lambda s: pl.BlockSpec(s, lambda i: tuple(0 for _ in s))
    vec = full((1, LANES))
    return pl.pallas_call(
        body, name=name, grid=(nt,),
        in_specs=[wide(EVEN_IN_PAD), wide(512), wide(512), wide(512), wide(512), pl.BlockSpec((SUBLANES, tm), lambda i: (0, rev(i))),
                  wide(LANES), full((1, A_WIDTH)), full((A_GROUPS, A_BLOCK, A_BLOCK)), full((A_BLOCK, LANES)), vec, vec, vec],
        out_specs=[wide(EVEN_IN_PAD), full((A_GROUPS, A_BLOCK, A_BLOCK)), full((A_BLOCK, LANES)), full((1, A_WIDTH)), vec, vec, vec],
        out_shape=[jax.ShapeDtypeStruct((T, EVEN_IN_PAD), _MXU), jax.ShapeDtypeStruct((A_GROUPS, A_BLOCK, A_BLOCK), F32),
                   jax.ShapeDtypeStruct((A_BLOCK, LANES), F32), jax.ShapeDtypeStruct((1, A_WIDTH), F32),
                   jax.ShapeDtypeStruct((1, LANES), F32), jax.ShapeDtypeStruct((1, LANES), F32), jax.ShapeDtypeStruct((1, LANES), F32)],
        scratch_shapes=[pltpu.VMEM((SUBLANES, LANES), F32)],
        compiler_params=_cp(("arbitrary",)),
    )(z, dycat, dqn, dkn, dvv, dct, drow, g_v, w_s, bs_cols, g_qn2, g_kn2, b_f_pad)


def _shift_up(ext, k, tm):
    return pltpu.roll(ext, ext.shape[0] - k, axis=0)[0:tm]


def _odd_bwd(name, z, dycat, w_pool, s_pool, conv_w8):
    T = z.shape[0]
    tm = _tile(T, 256, HALO)
    r, nt = tm // HALO, T // tm
    n_ext = tm + HALO

    def body(z_ref, zb_ref, zn_ref, dy_ref, dyn_ref, wp_ref, sp_ref, cw_ref, dz_ref, dwp_ref, dsp_ref, dcw_ref):
        i = pl.program_id(0)

        @pl.when(i == 0)
        def _():
            for rr in (dwp_ref, dsp_ref, dcw_ref):
                rr[...] = jnp.zeros_like(rr)

        first, last = i == 0, i == nt - 1
        pos = i * tm + _row((tm, LANES))
        pos_ext = i * tm + _row((n_ext, LANES))
        for g, w in enumerate(POOL_WINDOWS):
            sl = slice(g * LANES, (g + 1) * LANES)
            z_ext = jnp.concatenate([jnp.where(first, 0.0, zb_ref[:, sl]), z_ref[:, sl]], axis=0)
            p = _pool_p(z_ext, g, tm, pos)
            spv = sp_ref[:, sl]
            dyc = dy_ref[:, sl]
            dsp_ref[:, sl] += jnp.sum(dyc * _dot(p, wp_ref[g]), axis=0, keepdims=True)
            dpw_ext = jnp.concatenate([dyc, jnp.where(last, 0.0, dyn_ref[:, sl])], axis=0) * spv
            dwp_ref[g] += _dot_tn(p, dpw_ext[0:tm])
            dp_ext = _dot_nt(dpw_ext, wp_ref[g])
            f = dp_ext / jnp.minimum(pos_ext + 1, w).astype(F32)
            span = 1
            while span < w:
                f = f + pltpu.roll(f, n_ext - span, axis=0)
                span *= 2
            dz_ref[:, sl] = (f[0:tm] - dp_ext[0:tm]).astype(dz_ref.dtype)
        for j in range(CONV_WIDTH // LANES):
            sl = slice(j * LANES, (j + 1) * LANES)
            hd = slice(POOL_WIDTH + j * LANES, POOL_WIDTH + (j + 1) * LANES)
            gb = slice(POOL_WIDTH + CONV_WIDTH + j * LANES, POOL_WIDTH + CONV_WIDTH + (j + 1) * LANES)
            gc = slice(POOL_WIDTH + 2 * CONV_WIDTH + j * LANES, POOL_WIDTH + 2 * CONV_WIDTH + (j + 1) * LANES)
            ysl = slice(POOL_WIDTH + j * LANES, POOL_WIDTH + (j + 1) * LANES)
            hdv, gbv, gcv = z_ref[:, hd], z_ref[:, gb], z_ref[:, gc]
            xg = gcv * hdv
            xg_ext = jnp.concatenate([jnp.where(first, 0.0, zb_ref[:, gc] * zb_ref[:, hd]), xg], axis=0)
            xg1, xg2 = _shift_down(xg_ext, 1, tm), _shift_down(xg_ext, 2, tm)
            w0, w1, w2 = cw_ref[0:1, sl], cw_ref[1:2, sl], cw_ref[2:3, sl]
            conv = w0 * xg2 + w1 * xg1 + w2 * xg
            dyd = dy_ref[:, ysl]
            dconv = dyd * gbv
            dconv_ext = jnp.concatenate([dconv, jnp.where(last, 0.0, dyn_ref[:, ysl] * zn_ref[:, gb])], axis=0)
            dcw_ref[0:1, sl] += jnp.sum(dconv * xg2, axis=0, keepdims=True)
            dcw_ref[1:2, sl] += jnp.sum(dconv * xg1, axis=0, keepdims=True)
            dcw_ref[2:3, sl] += jnp.sum(dconv * xg, axis=0, keepdims=True)
            dxg = w2 * dconv + w1 * _shift_up(dconv_ext, 1, tm) + w0 * _shift_up(dconv_ext, 2, tm)
            dz_ref[:, hd] = (dxg * gcv).astype(dz_ref.dtype)
            dz_ref[:, gb] = (dyd * conv).astype(dz_ref.dtype)
            dz_ref[:, gc] = (dxg * hdv).astype(dz_ref.dtype)

    full = lambda s: pl.BlockSpec(s, lambda i: tuple(0 for _ in s))
    back = lambda n: pl.BlockSpec((HALO, n), lambda i: (jnp.maximum(i * r - 1, 0), 0))
    nxt = lambda n: pl.BlockSpec((HALO, n), lambda i: (jnp.minimum((i + 1) * r, T // HALO - 1), 0))
    return pl.pallas_call(
        body, name=name, grid=(nt,),
        in_specs=[pl.BlockSpec((tm, ODD_IN), lambda i: (i, 0)), back(ODD_IN), nxt(ODD_IN),
                  pl.BlockSpec((tm, 1024), lambda i: (i, 0)), nxt(1024),
                  full((4, LANES, LANES)), full((1, POOL_WIDTH)), full((SUBLANES, CONV_WIDTH))],
        out_specs=[pl.BlockSpec((tm, ODD_IN), lambda i: (i, 0)), full((4, LANES, LANES)), full((1, POOL_WIDTH)), full((SUBLANES, CONV_WIDTH))],
        out_shape=[jax.ShapeDtypeStruct((T, ODD_IN), _MXU), jax.ShapeDtypeStruct((4, LANES, LANES), F32),
                   jax.ShapeDtypeStruct((1, POOL_WIDTH), F32), jax.ShapeDtypeStruct((SUBLANES, CONV_WIDTH), F32)],
        compiler_params=_cp(("arbitrary",)),
    )(z, z, z, dycat, dycat, w_pool, s_pool, conv_w8)


def _layer_bwd(l, g3, mem, W, sv):
    i = l // 2
    G = {}
    da, db = _ffn_bwd_act(f"ffn_bwd_act{l}", g3, W["w_down"], sv["a"], sv["b"])
    G["w_down"] = _mm(f"d_w_down{l}", [(sv["s"], g3)], "tn")
    dh3 = _mm(f"d_h3{l}", [(da, W["w_gate"]), (db, W["w_up"])], "nt")
    G["w_gate"] = _mm(f"d_w_gate{l}", [(sv["h3"], da)], "tn")
    G["w_up"] = _mm(f"d_w_up{l}", [(sv["h3"], db)], "tn")
    g2, dg = _norm_bwd(f"norm_ffn_bwd{l}", dh3, sv["x2"], W["g_ffn"], g3)
    G["g_ffn"] = dg[0]

    do = _mm(f"d_o{l}", [(g2, W["wo"])], "nt", out_dtype=_MXU)
    G["xa_wo"] = _mm(f"d_wo{l}", [(sv["o"], g2)], "tn")
    dqx, dkn, dvx, dgq = _xa_bwd(f"xa_bwd{l}", sv["qx"], do, W["gq"], sv["kn_x"], sv["vx"])
    G["xa_gq"] = dgq[0]
    dh2 = _mm(f"d_h2{l}", [(dqx, W["wq"])], "nt")
    G["xa_wq"] = _mm(f"d_wq{l}", [(sv["h2"], dqx)], "tn")
    g1, dg = _norm_bwd(f"norm_xa_bwd{l}", dh2, sv["x1"], W["g_xa"], g2)
    G["g_xa"] = dg[0]
    dwkv, dgk, dgm = _mem_bwd(f"mem_bwd{l}", mem, W["g_mem"], sv["m"], sv["kv"], W["gk"], dkn, dvx, W["wkv"])
    G["xa_wkv"], G["xa_gk"], G["g_mem"] = dwkv, dgk[0], dgm[0]

    dycat = _mm(f"d_ycat{l}", [(g1, W["w_out"])], "nt")
    if l % 2 == 0:
        G["e_w_out"] = jnp.concatenate([_mm(f"d_e_wout_a{l}", [(sv["ya"], g1)], "tn"), _mm(f"d_e_wout_b{l}", [(sv["yb"], g1)], "tn")], axis=0)
        fa = (sv["qn"], sv["kn"], sv["vv"], sv["c"], sv["ct"], sv["lse"], dycat, sv["ybf"])
        dqn, drow = _fox_bwd_dq(f"fox_bwd_dq{l}", *fa)
        dkn_f, dvv, dct = _fox_bwd_dkv(f"fox_bwd_dkv{l}", *fa)
        dz, dws, dbs, dgv, dgq2, dgk2, dbf = _even_bwd(f"even_bwd{l}", sv["z"], dycat, dqn, dkn_f, dvv, dct, drow, W["g_v"], W["w_s"],
                                                      W["bs_cols"], W["g_qn2"], W["g_kn2"], W["b_f_pad"])
        G.update(e_w_s=dws, e_b_s=dbs[:, :A_GROUPS].T, e_g_v=dgv[0], e_g_qn=dgq2[0, :FOX_HEAD_DIM], e_g_kn=dgk2[0, :FOX_HEAD_DIM],
                 e_b_f=dbf[0, :FOX_HEADS])
        G["e_w_in"] = _mm(f"d_e_win{l}", [(sv["h1"], dz)], "tn")
    else:
        G["o_w_out"] = _mm(f"d_o_wout{l}", [(sv["ycat"], g1)], "tn")
        dz, dwp, dsp, dcw = _odd_bwd(f"odd_bwd{l}", sv["z"], dycat, W["w_pool"], W["s_pool"], W["conv_w8"])
        G.update(o_w_pool=dwp, o_s_pool=dsp[0], o_conv_w=dcw[:CONV_K])
        G["o_w_in"] = _mm(f"d_o_win{l}", [(sv["h1"], dz)], "tn")
    dh1 = _mm(f"d_h1{l}", [(dz, W["w_in"])], "nt")
    g0, dg = _norm_bwd(f"norm_mix_bwd{l}", dh1, sv["x0"], W["g_mix"], g1)
    G["g_mix"] = dg[0]
    return g0, G


PACK_W = 1024
MESH_T = pl.DeviceIdType.MESH
_ANY = pl.BlockSpec(memory_space=pl.ANY)


def _my_place():
    x, y, c = lax.axis_index("x"), lax.axis_index("y"), lax.axis_index("c")
    return x, y, c


def _flip(v, bit):
    return 1 - v if bit else v


N_PEERS = N_DEV - 1


def _all_gather(name, blks):
    n = len(blks)

    def body(*refs):
        x_refs, out_refs = refs[:n], refs[n:2 * n]
        send_sems, recv_sems, local_sems = refs[2 * n:]
        x, y, c = _my_place()
        me, sibling = (x, y, c), (x, y, 1 - c)
        chips = [(1 - x, y), (x, 1 - y), (1 - x, 1 - y)]

        def copy(t, k, block, to, src=None):
            slab = out_refs[t].at[4 * block[0] + 2 * block[1] + block[2]]
            return pltpu.make_async_remote_copy(
                src_ref=slab if src is None else src, dst_ref=slab,
                send_sem=send_sems.at[N_PEERS * t + k], recv_sem=recv_sems.at[N_PEERS * t + k], device_id=to, device_id_type=MESH_T)

        mine, first, passed = [], [], []
        for t in range(n):
            mine.append(pltpu.make_async_copy(x_refs[t], out_refs[t].at[4 * x + 2 * y + c], local_sems.at[t]))
            first.append(copy(t, 0, me, sibling, src=x_refs[t]))
            first += [copy(t, 1 + j, me, (*chip, c), src=x_refs[t]) for j, chip in enumerate(chips)]
        for cp in mine + first:
            cp.start()
        for j, chip in enumerate(chips):
            for t in range(n):
                copy(t, 1 + j, (*chip, c), me).wait_recv()
                cp = copy(t, 4 + j, (*chip, c), sibling)
                cp.start()
                passed.append(cp)
        for t in range(n):
            copy(t, 0, sibling, me).wait_recv()
            for j, chip in enumerate(chips):
                copy(t, 4 + j, (*chip, 1 - c), me).wait_recv()
        for cp in first + passed:
            cp.wait_send()
        for cp in mine:
            cp.wait()

    return pl.pallas_call(
        body, name=name,
        out_shape=[jax.ShapeDtypeStruct((N_DEV,) + b.shape, b.dtype) for b in blks],
        in_specs=[_ANY] * n, out_specs=[_ANY] * n,
        scratch_shapes=[pltpu.SemaphoreType.DMA((N_PEERS * n,)), pltpu.SemaphoreType.DMA((N_PEERS * n,)), pltpu.SemaphoreType.DMA((n,))],
    )(*blks)


def _all_to_all(name, gs):
    n = len(gs)

    def body(*refs):
        g_refs, out_refs = refs[:n], refs[n:2 * n]
        send_sems, recv_sems, local_sems = refs[2 * n:]
        x, y, c = _my_place()
        me = 4 * x + 2 * y + c
        mine, sends, recvs = [], [], []
        for t in range(n):
            mine.append(pltpu.make_async_copy(g_refs[t].at[me], out_refs[t].at[me], local_sems.at[t]))
            for m in range(1, N_DEV):
                px, py, pc = _flip(x, m & 4), _flip(y, m & 2), _flip(c, m & 1)
                pidx = 4 * px + 2 * py + pc
                sem = dict(send_sem=send_sems.at[N_PEERS * t + m - 1], recv_sem=recv_sems.at[N_PEERS * t + m - 1], device_id_type=MESH_T)
                sends.append(pltpu.make_async_remote_copy(src_ref=g_refs[t].at[pidx], dst_ref=out_refs[t].at[me], device_id=(px, py, pc), **sem))
                recvs.append(pltpu.make_async_remote_copy(src_ref=g_refs[t].at[pidx], dst_ref=out_refs[t].at[pidx], device_id=(x, y, c), **sem))
        for cp in mine + sends:
            cp.start()
        for cp in recvs:
            cp.wait_recv()
        for cp in sends:
            cp.wait_send()
        for cp in mine:
            cp.wait()

    return pl.pallas_call(
        body, name=name,
        out_shape=[jax.ShapeDtypeStruct(g.shape, g.dtype) for g in gs],
        in_specs=[_ANY] * n, out_specs=[_ANY] * n,
        scratch_shapes=[pltpu.SemaphoreType.DMA((N_PEERS * n,)), pltpu.SemaphoreType.DMA((N_PEERS * n,)), pltpu.SemaphoreType.DMA((n,))],
    )(*gs)


def _unblock(name, blocks, layer, n_out):
    _, _, K, nb = blocks.shape
    tk = _tile(K, 256, 16)

    def body(b_ref, o_ref):
        for j in range(N_DEV):
            o_ref[:, j * nb:(j + 1) * nb] = b_ref[j]
        if n_out > N_DEV * nb:
            o_ref[:, N_DEV * nb:n_out] = jnp.zeros((tk, n_out - N_DEV * nb), o_ref.dtype)

    return pl.pallas_call(
        body, name=name, grid=(K // tk,),
        in_specs=[pl.BlockSpec((N_DEV, None, tk, nb), lambda i: (0, layer, i, 0))],
        out_specs=pl.BlockSpec((tk, n_out), lambda i: (i, 0)),
        out_shape=jax.ShapeDtypeStruct((K, n_out), blocks.dtype),
        compiler_params=_cp(("parallel",)),
    )(blocks)


def _block(name, full, nb):
    K, N = full.shape
    tk = _tile(K, 256, 8)

    def body(f_ref, o_ref):
        for j in range(N_DEV):
            o_ref[j] = f_ref[:, j * nb:(j + 1) * nb]

    return pl.pallas_call(
        body, name=name, grid=(K // tk,),
        in_specs=[pl.BlockSpec((tk, N), lambda i: (i, 0))],
        out_specs=pl.BlockSpec((N_DEV, tk, nb), lambda i: (0, i, 0)),
        out_shape=jax.ShapeDtypeStruct((N_DEV, K, nb), full.dtype),
        compiler_params=_cp(("parallel",)),
    )(full)


ADAM_BLOCK_ELEMS = 256 * 1024


def _adamw(name, parts, w, m, v):
    shape = w.shape
    cols = shape[-1]
    rows = math.prod(shape[:-1])
    parts, w, m, v = parts.reshape(N_DEV, rows, cols), w.reshape(rows, cols), m.reshape(rows, cols), v.reshape(rows, cols)
    tr = _tile(rows, max(SUBLANES, ADAM_BLOCK_ELEMS // cols), SUBLANES)

    def body(p_ref, w_ref, m_ref, v_ref, g_ref, d_ref, mo_ref, vo_ref):
        g = p_ref[0]
        for s in range(1, N_DEV):
            g = g + p_ref[s]
        mn = ADAM_B1 * m_ref[...] + (1.0 - ADAM_B1) * g
        vn = ADAM_B2 * v_ref[...] + (1.0 - ADAM_B2) * jnp.square(g)
        m_hat = mn / (1.0 - ADAM_B1 ** ADAM_STEP)
        v_hat = vn / (1.0 - ADAM_B2 ** ADAM_STEP)
        g_ref[...] = g
        d_ref[...] = -ADAM_LR * (m_hat / (jnp.sqrt(v_hat) + ADAM_EPS) + ADAM_WD * w_ref[...])
        mo_ref[...] = mn
        vo_ref[...] = vn

    blk = pl.BlockSpec((tr, cols), lambda i: (i, 0))
    shp = jax.ShapeDtypeStruct((rows, cols), F32)
    res = pl.pallas_call(
        body, name=name, grid=(rows // tr,),
        in_specs=[pl.BlockSpec((N_DEV, tr, cols), lambda i: (0, i, 0)), blk, blk, blk],
        out_specs=[blk, blk, blk, blk], out_shape=[shp, shp, shp, shp],
        compiler_params=_cp(("parallel",)),
    )(parts, w, m, v)
    return [r.reshape(shape) for r in res]


SHARDED = dict(xa_wq=1, xa_wkv=1, xa_wo=2, w_gate=2, w_up=2, w_down=1, e_w_in=2, e_w_out=1, o_w_in=2, o_s_pool=1, o_conv_w=2, o_w_out=1)
ELEMENTWISE_SHARDED = ("o_s_pool", "o_conv_w")
REPLICATED = ("g_mix", "g_xa", "g_mem", "xa_gq", "xa_gk", "g_ffn", "e_b_f", "e_g_v", "e_w_s", "e_b_s", "e_g_qn", "e_g_kn", "o_w_pool")
WEIGHTS = ("g_mix", "g_xa", "g_mem", "xa_wq", "xa_wkv", "xa_wo", "xa_gq", "xa_gk", "g_ffn", "w_gate", "w_up", "w_down", "e_w_in", "e_b_f",
           "e_g_v", "e_w_s", "e_b_s", "e_g_qn", "e_g_kn", "e_w_out", "o_w_in", "o_w_pool", "o_s_pool", "o_conv_w", "o_w_out")


def _rows_for(n, mult):
    return -(-n // (PACK_W * mult)) * mult


def _pack(arrs, rows, dtype):
    flat = jnp.concatenate([a.reshape(-1).astype(dtype) for a in arrs])
    return jnp.pad(flat, (0, rows * PACK_W - flat.shape[0])).reshape(rows, PACK_W)


def _unpack(slab, shapes):
    flat, out, off = slab.reshape(-1), [], 0
    for s in shapes:
        n = math.prod(s)
        out.append(flat[off:off + n].reshape(s))
        off += n
    return out


def _to_blocks(full, axis):
    s = full.shape
    return jnp.moveaxis(full.reshape(s[:axis] + (N_DEV, s[axis] // N_DEV) + s[axis + 1:]), axis, 0)


def _from_blocks(blocks, axis):
    b = jnp.moveaxis(blocks, 0, axis)
    s = b.shape
    return b.reshape(s[:axis] + (s[axis] * s[axis + 1],) + s[axis + 2:])


def kernel(x, mem, g_mix, g_xa, g_mem, xa_wq, xa_wkv, xa_wo, xa_gq, xa_gk, g_ffn, w_gate, w_up, w_down, e_w_in, e_b_f, e_g_v, e_w_s, e_b_s, e_g_qn, e_g_kn, e_w_out, o_w_in, o_w_pool, o_s_pool, o_conv_w, o_w_out, loss_target, m_g_mix, m_g_xa, m_g_mem, m_xa_wq, m_xa_wkv, m_xa_wo, m_xa_gq, m_xa_gk, m_g_ffn, m_w_gate, m_w_up, m_w_down, m_e_w_in, m_e_b_f, m_e_g_v, m_e_w_s, m_e_b_s, m_e_g_qn, m_e_g_kn, m_e_w_out, m_o_w_in, m_o_w_pool, m_o_s_pool, m_o_conv_w, m_o_w_out, v_g_mix, v_g_xa, v_g_mem, v_xa_wq, v_xa_wkv, v_xa_wo, v_xa_gq, v_xa_gk, v_g_ffn, v_w_gate, v_w_up, v_w_down, v_e_w_in, v_e_b_f, v_e_g_v, v_e_w_s, v_e_b_s, v_e_g_qn, v_e_g_kn, v_e_w_out, v_o_w_in, v_o_w_pool, v_o_s_pool, v_o_conv_w, v_o_w_out):
    args = dict(locals())
    Wt = {n: args[n] for n in WEIGHTS}
    Mo = {n: args["m_" + n] for n in WEIGHTS}
    Vo = {n: args["v_" + n] for n in WEIGHTS}
    sh_names, rep_names = tuple(SHARDED), REPLICATED
    rep_shapes = [Wt[n].shape for n in rep_names]
    rep_rows = _rows_for(sum(math.prod(s) for s in rep_shapes) + 1, 16)

    gathered = dict(zip(sh_names, _all_gather("gather_weights", [Wt[n] if n in ELEMENTWISE_SHARDED else _mx(Wt[n]) for n in sh_names])))
    P = {n: Wt[n] for n in rep_names}
    for n in sh_names:
        gn = gathered[n]
        if SHARDED[n] == 1 or n in ELEMENTWISE_SHARDED:
            P[n] = _from_blocks(gn, SHARDED[n])
        else:
            n_out = EVEN_IN_PAD if n == "e_w_in" else N_DEV * gn.shape[-1]
            P[n] = jnp.stack([_unblock(f"unblock_{n}{l}", gn, l, n_out) for l in range(gn.shape[1])])

    xk, mem2, svs, Ws = x[0], mem[0], [], []
    for l in range(2):
        Ws.append(_layer_weights(l, P))
        xk, sv = _layer_fwd(l, xk, mem2, Ws[l])
        svs.append(sv)
    g, loss_row = _loss_grad("loss_grad", xk, loss_target[0])
    G = {}
    for l in (1, 0):
        g, Gl = _layer_bwd(l, g, mem2, Ws[l], svs[l])
        for k2, v2 in Gl.items():
            G.setdefault(k2, {})[l] = v2
    grad_x = g[None]

    send = []
    for n in sh_names:
        per_layer = [G[n][l] for l in sorted(G[n])]
        if SHARDED[n] == 1 or n in ELEMENTWISE_SHARDED:
            send.append(_to_blocks(jnp.stack(per_layer).reshape((len(per_layer),) + per_layer[0].shape[-2:]), SHARDED[n]).reshape((N_DEV,) + Wt[n].shape))
        else:
            send.append(jnp.stack([_block(f"block_{n}{l}", gl, Wt[n].shape[-1]) for l, gl in enumerate(per_layer)], axis=1))
    sh_parts = dict(zip(sh_names, _all_to_all("exchange_grads", send)))
    rep_g = [jnp.stack([G[n][l] for l in sorted(G[n])]).reshape(Wt[n].shape) for n in rep_names]
    rep_parts = _all_gather("gather_small_grads", [_pack(rep_g + [jnp.sum(loss_row).reshape(1)], rep_rows, F32)])[0]

    outs = {}
    for n in sh_names:
        for kind, a in zip(("grad", "delta", "new_m", "new_v"), _adamw("adamw_" + n, sh_parts[n], Wt[n], Mo[n], Vo[n])):
            outs[kind + "_" + n] = a
    res = _adamw("adamw_replicated", rep_parts, _pack([Wt[n] for n in rep_names], rep_rows, F32),
                 _pack([Mo[n] for n in rep_names], rep_rows, F32), _pack([Vo[n] for n in rep_names], rep_rows, F32))
    for kind, slab in zip(("grad", "delta", "new_m", "new_v"), res):
        for n, a in zip(rep_names, _unpack(slab, rep_shapes)):
            outs[kind + "_" + n] = a
    loss = res[0].reshape(-1)[sum(math.prod(s) for s in rep_shapes)]
    return (loss, grad_x, *[outs[k + "_" + n] for k in ("grad", "delta", "new_m", "new_v") for n in WEIGHTS])
```

```python
import functools
import math

import numpy as np
import jax
import jax.numpy as jnp
from jax import lax
from jax.experimental import pallas as pl
from jax.experimental.pallas import tpu as pltpu

F32 = jnp.float32
BF16 = jnp.bfloat16
_MXU = jnp.bfloat16

D_MODEL = 1024
N_DEV = 8
LANES = 128
SUBLANES = 8
HALO = 16
CHUNK = 64
A_GROUPS, A_GROUP_DIM, A_WIDTH, A_BLOCK = 4, 128, 512, 128
FOX_HEADS, FOX_HEAD_DIM, FOX_WIDTH = 8, 64, 512
POOL_WINDOWS = (2, 4, 8, 16)
POOL_WIDTH, CONV_WIDTH, CONV_K = 512, 512, 3
EVEN_IN, EVEN_IN_PAD, ODD_IN = 2568, 2688, 2048
XA_HEADS, XA_HEAD_DIM, XA_WIDTH = 4, 128, 512
D_FF = 2816
EPS = 1e-6
ADAM_LR, ADAM_B1, ADAM_B2, ADAM_EPS, ADAM_WD, ADAM_STEP = 0.001, 0.9, 0.999, 1e-08, 0.01, 10
VMEM_LIMIT = 48 * 1024 * 1024


def _cp(sem):
    return pltpu.CompilerParams(dimension_semantics=sem, vmem_limit_bytes=VMEM_LIMIT)


def _tile(n, cap, q):
    best = None
    for d in range(q, min(n, cap) + 1, q):
        if n % d == 0:
            best = d
    return best if best is not None else n


def _mx(a):
    return a.astype(_MXU)


def _dot(a, b):
    return jnp.dot(_mx(a), _mx(b), preferred_element_type=F32)


def _dot_nt(a, b):
    return lax.dot_general(_mx(a), _mx(b), (((1,), (1,)), ((), ())), preferred_element_type=F32)


def _dot_tn(a, b):
    return lax.dot_general(_mx(a), _mx(b), (((0,), (0,)), ((), ())), preferred_element_type=F32)


def _dot_f32(a, b):
    return jnp.dot(a, b, precision=lax.Precision.HIGHEST, preferred_element_type=F32)


def _lane(shape):
    return lax.broadcasted_iota(jnp.int32, shape, len(shape) - 1)


def _row(shape):
    return lax.broadcasted_iota(jnp.int32, shape, len(shape) - 2)


def _mm(name, pairs, mode, out_dtype=F32, residual=None, tm_cap=512, tn_cap=1536, tk_cap=2048):
    a0, b0 = pairs[0]
    if mode == "nn":
        (M, K), N = a0.shape, b0.shape[1]
    elif mode == "nt":
        (M, K), N = a0.shape, b0.shape[0]
    else:
        (K, M), N = a0.shape, b0.shape[1]
    if mode == "tn":
        tm, tk = _tile(M, tm_cap, LANES), _tile(K, 512, 16)
    else:
        tm, tk = _tile(M, tm_cap, 16), _tile(K, tk_cap, LANES)
    tn = _tile(N, tn_cap, LANES)
    nk = K // tk
    npairs = len(pairs)
    dot = {"nn": _dot, "nt": _dot_nt, "tn": _dot_tn}[mode]

    def body(*refs):
        ab = refs[:2 * npairs]
        res_ref = refs[2 * npairs] if residual is not None else None
        o_ref, acc = refs[-2], refs[-1]
        k = pl.program_id(2)

        @pl.when(k == 0)
        def _():
            acc[...] = jnp.zeros_like(acc)

        for p in range(npairs):
            acc[...] += dot(ab[2 * p][...], ab[2 * p + 1][...])

        @pl.when(k == nk - 1)
        def _():
            out = acc[...]
            if res_ref is not None:
                out = out + res_ref[...]
            o_ref[...] = out.astype(o_ref.dtype)

    if mode == "nn":
        a_spec = pl.BlockSpec((tm, tk), lambda j, i, k: (i, k))
        b_spec = pl.BlockSpec((tk, tn), lambda j, i, k: (k, j))
    elif mode == "nt":
        a_spec = pl.BlockSpec((tm, tk), lambda j, i, k: (i, k))
        b_spec = pl.BlockSpec((tn, tk), lambda j, i, k: (j, k))
    else:
        a_spec = pl.BlockSpec((tk, tm), lambda j, i, k: (k, i))
        b_spec = pl.BlockSpec((tk, tn), lambda j, i, k: (k, j))
    o_spec = pl.BlockSpec((tm, tn), lambda j, i, k: (i, j))
    in_specs, args = [], []
    for a, b in pairs:
        in_specs += [a_spec, b_spec]
        args += [a, b]
    if residual is not None:
        in_specs.append(o_spec)
        args.append(residual)
    return pl.pallas_call(
        body, name=name, grid=(N // tn, M // tm, nk),
        in_specs=in_specs, out_specs=o_spec,
        out_shape=jax.ShapeDtypeStruct((M, N), out_dtype),
        scratch_shapes=[pltpu.VMEM((tm, tn), F32)],
        compiler_params=_cp(("parallel", "parallel", "arbitrary")),
    )(*args)


def _rms_rows(x):
    return lax.rsqrt(jnp.mean(x * x, axis=-1, keepdims=True) + EPS)


def _norm_fwd(name, x, g):
    T, Dm = x.shape
    tm = _tile(T, 512, 16)

    def body(x_ref, g_ref, o_ref):
        xv = x_ref[...]
        o_ref[...] = ((xv * _rms_rows(xv)) * g_ref[...]).astype(o_ref.dtype)

    return pl.pallas_call(
        body, name=name, grid=(T // tm,),
        in_specs=[pl.BlockSpec((tm, Dm), lambda i: (i, 0)), pl.BlockSpec((1, Dm), lambda i: (0, 0))],
        out_specs=pl.BlockSpec((tm, Dm), lambda i: (i, 0)),
        out_shape=jax.ShapeDtypeStruct((T, Dm), _MXU),
        compiler_params=_cp(("parallel",)),
    )(x, g.reshape(1, Dm))


def _norm_bwd(name, dh, x, g, res):
    T, Dm = x.shape
    tm = _tile(T, 512, 8)

    def body(dh_ref, x_ref, g_ref, res_ref, dx_ref, dg_ref):
        xv, dhv = x_ref[...], dh_ref[...]
        r = _rms_rows(xv)
        xhat = xv * r
        dxhat = dhv * g_ref[...]
        dx_ref[...] = res_ref[...] + r * (dxhat - xhat * jnp.mean(dxhat * xhat, axis=-1, keepdims=True))

        @pl.when(pl.program_id(0) == 0)
        def _():
            dg_ref[...] = jnp.zeros_like(dg_ref)

        dg_ref[...] += jnp.sum(dhv * xhat, axis=0, keepdims=True)

    blk = pl.BlockSpec((tm, Dm), lambda i: (i, 0))
    vec = pl.BlockSpec((1, Dm), lambda i: (0, 0))
    return pl.pallas_call(
        body, name=name, grid=(T // tm,),
        in_specs=[blk, blk, vec, blk], out_specs=[blk, vec],
        out_shape=[jax.ShapeDtypeStruct((T, Dm), F32), jax.ShapeDtypeStruct((1, Dm), F32)],
        compiler_params=_cp(("arbitrary",)),
    )(dh, x, g.reshape(1, Dm), res)


def _gelu(x):
    return jax.nn.gelu(x)


def _gelu_grad(x):
    c0, c1 = math.sqrt(2.0 / math.pi), 0.044715
    t = jnp.tanh(c0 * (x + c1 * x * x * x))
    return 0.5 * (1.0 + t) + 0.5 * x * (1.0 - t * t) * c0 * (1.0 + 3.0 * c1 * x * x)


def _head64_rms(x):
    lo = _lane(x.shape) < FOX_HEAD_DIM
    xx = x * x
    sa = jnp.sum(jnp.where(lo, xx, 0.0), axis=-1, keepdims=True)
    sb = jnp.sum(jnp.where(lo, 0.0, xx), axis=-1, keepdims=True)
    inv = 1.0 / FOX_HEAD_DIM
    return jnp.where(lo, lax.rsqrt(sa * inv + EPS), lax.rsqrt(sb * inv + EPS))


def _head64_mean(x):
    lo = _lane(x.shape) < FOX_HEAD_DIM
    sa = jnp.sum(jnp.where(lo, x, 0.0), axis=-1, keepdims=True)
    sb = jnp.sum(jnp.where(lo, 0.0, x), axis=-1, keepdims=True)
    return jnp.where(lo, sa, sb) * (1.0 / FOX_HEAD_DIM)


def _masked_ws(ws_ref, g):
    w = ws_ref[g]
    keep = (_row(w.shape) // CHUNK) >= (_lane(w.shape) // CHUNK)
    return jnp.where(keep, w, 0.0)


def _even_pre(name, z, g_v, w_s, bs_cols, g_qn2, g_kn2, b_f_pad):
    T = z.shape[0]
    tm = _tile(T, 256, A_BLOCK)
    nb = tm // A_BLOCK

    def body(z_ref, gv_ref, ws_ref, bs_ref, gq_ref, gk_ref, bf_ref, ya_ref, q_ref, k_ref, v_ref, c_ref, ct_ref, carry):
        i = pl.program_id(0)

        @pl.when(i == 0)
        def _():
            carry[...] = jnp.zeros_like(carry)

        wm = [_mx(_masked_ws(ws_ref, g)) for g in range(A_GROUPS)]
        for g in range(A_GROUPS):
            sl = slice(g * A_GROUP_DIM, (g + 1) * A_GROUP_DIM)
            u = _gelu(z_ref[:, sl])
            v = _gelu(z_ref[:, A_WIDTH + g * A_GROUP_DIM:A_WIDTH + (g + 1) * A_GROUP_DIM])
            vn = _mx((v * _rms_rows(v)) * gv_ref[:, sl])
            for n in range(nb):
                rows = slice(n * A_BLOCK, (n + 1) * A_BLOCK)
                s = jnp.dot(wm[g], vn[rows], preferred_element_type=F32) + bs_ref[:, g:g + 1]
                ya_ref[rows, sl] = (u[rows] * s).astype(ya_ref.dtype)
        o = 2 * A_WIDTH
        for j in range(FOX_WIDTH // LANES):
            sl = slice(j * LANES, (j + 1) * LANES)
            q = z_ref[:, o + j * LANES:o + (j + 1) * LANES]
            q_ref[:, sl] = ((q * _head64_rms(q)) * gq_ref[...]).astype(q_ref.dtype)
            k = z_ref[:, o + FOX_WIDTH + j * LANES:o + FOX_WIDTH + (j + 1) * LANES]
            k_ref[:, sl] = ((k * _head64_rms(k)) * gk_ref[...]).astype(k_ref.dtype)
            v_ref[:, sl] = z_ref[:, o + 2 * FOX_WIDTH + j * LANES:o + 2 * FOX_WIDTH + (j + 1) * LANES].astype(v_ref.dtype)
        zf = z_ref[:, o + 3 * FOX_WIDTH:o + 3 * FOX_WIDTH + LANES]
        logf = jnp.where(_lane(zf.shape) < FOX_HEADS, jax.nn.log_sigmoid(zf + bf_ref[...]), 0.0)
        tri = (_row((tm, tm)) >= _lane((tm, tm))).astype(F32)
        c = _dot_f32(tri, logf) + carry[0:1, :]
        c_ref[...] = c
        ct_ref[...] = c.T[0:SUBLANES, :]
        carry[0:1, :] = c[tm - 1:tm, :]

    wide = lambda n: pl.BlockSpec((tm, n), lambda i: (i, 0))
    full = lambda s: pl.BlockSpec(s, lambda i: tuple(0 for _ in s))
    out512 = jax.ShapeDtypeStruct((T, 512), _MXU)
    return pl.pallas_call(
        body, name=name, grid=(T // tm,),
        in_specs=[wide(EVEN_IN_PAD), full((1, A_WIDTH)), full((A_GROUPS, A_BLOCK, A_BLOCK)), full((A_BLOCK, LANES)),
                  full((1, LANES)), full((1, LANES)), full((1, LANES))],
        out_specs=[wide(512), wide(512), wide(512), wide(512), wide(LANES), pl.BlockSpec((SUBLANES, tm), lambda i: (0, i))],
        out_shape=[out512, out512, out512, out512, jax.ShapeDtypeStruct((T, LANES), F32), jax.ShapeDtypeStruct((SUBLANES, T), F32)],
        scratch_shapes=[pltpu.VMEM((SUBLANES, LANES), F32)],
        compiler_params=_cp(("arbitrary",)),
    )(z, g_v, w_s, bs_cols, g_qn2, g_kn2, b_f_pad)


def _tri_steps(n, by_rows):
    if by_rows:
        pairs = [(q, k) for q in range(n) for k in range(q + 1)]
    else:
        pairs = [(q, k) for k in range(n) for q in range(k, n)]
    return (jnp.asarray(np.array([p[0] for p in pairs], np.int32)), jnp.asarray(np.array([p[1] for p in pairs], np.int32)))


def _fox_logits(q, k, c_col, c_row, qi, ki, tq, tk):
    s = _dot_nt(q, k) * (1.0 / math.sqrt(FOX_HEAD_DIM)) + (c_col - c_row)
    allowed = (qi * tq + _row((tq, tk))) >= (ki * tk + _lane((tq, tk)))
    return jnp.where(allowed, s, -jnp.inf)


def _side_specs(side):
    return ([pl.BlockSpec(memory_space=pl.ANY)] * side.n, list(side.arrays), list(side.out_shape), list(side.scratch)) if side is not None else ([], [], [], [])


def _side_run(side, s_id, nsteps, side_in, side_out, sems):
    if side is None:
        return

    @pl.when(s_id == 0)
    def _():
        side.start(side_in, side_out, sems)

    @pl.when(s_id == nsteps - 1)
    def _():
        side.finish(side_in, side_out, sems)


def _fox_fwd(name, qn, kn, vv, c, ct, side=None):
    T = qn.shape[0]
    tq = _tile(T, 512, LANES)
    nblk = T // tq
    qs, ks = _tri_steps(nblk, True)
    nsteps = int(qs.shape[0])
    npair = FOX_WIDTH // LANES
    s_specs, s_args, s_shapes, s_scratch = _side_specs(side)

    def body(qs_ref, ks_ref, *refs):
        main_in, side_in, main_out, side_out, main_scr, sems = _split_side(refs, 5, 3, 3, side)
        q_ref, k_ref, v_ref, c_ref, ct_ref = main_in
        yb_ref, yf_ref, lse_ref = main_out
        acc, m_s, l_s = main_scr
        s_id = pl.program_id(0)
        qi, ki = qs_ref[s_id], ks_ref[s_id]
        _side_run(side, s_id, nsteps, side_in, side_out, sems)

        @pl.when(ki == 0)
        def _():
            acc[...] = jnp.zeros_like(acc)
            m_s[...] = jnp.full_like(m_s, -jnp.inf)
            l_s[...] = jnp.zeros_like(l_s)

        lo = _lane((tq, LANES)) < FOX_HEAD_DIM
        for j in range(npair):
            sl = slice(j * LANES, (j + 1) * LANES)
            q2, k2, v2 = q_ref[:, sl], k_ref[:, sl], v_ref[:, sl]
            outs = []
            for e in range(2):
                h = 2 * j + e
                qh = jnp.where(lo if e == 0 else ~lo, q2, jnp.zeros_like(q2))
                s = _fox_logits(qh, k2, c_ref[:, h:h + 1], ct_ref[h:h + 1, :], qi, ki, tq, tq)
                m_old = m_s[h]
                m_new = jnp.maximum(m_old, jnp.max(s, axis=-1, keepdims=True))
                alpha = jnp.exp(m_old - m_new)
                p = jnp.exp(s - m_new)
                l_s[h] = alpha * l_s[h] + jnp.sum(p, axis=-1, keepdims=True)
                m_s[h] = m_new
                outs.append((alpha, _dot(p, v2)))
            alpha2 = jnp.where(lo, outs[0][0], outs[1][0])
            acc[:, sl] = alpha2 * acc[:, sl] + jnp.where(lo, outs[0][1], outs[1][1])

        @pl.when(ki == qi)
        def _():
            lse = jnp.zeros((tq, LANES), F32)
            for j in range(npair):
                sl = slice(j * LANES, (j + 1) * LANES)
                inv = jnp.where(lo, 1.0 / l_s[2 * j], 1.0 / l_s[2 * j + 1])
                y = acc[:, sl] * inv
                yf_ref[:, sl] = y
                yb_ref[:, sl] = y.astype(yb_ref.dtype)
            for h in range(FOX_HEADS):
                lse = jnp.where(_lane((tq, LANES)) == h, m_s[h] + jnp.log(l_s[h]), lse)
            lse_ref[...] = lse

    qmap = lambda s, qs_r, ks_r: (qs_r[s], 0)
    kmap = lambda s, qs_r, ks_r: (ks_r[s], 0)
    grid_spec = pltpu.PrefetchScalarGridSpec(
        num_scalar_prefetch=2, grid=(nsteps,),
        in_specs=[pl.BlockSpec((tq, 512), qmap), pl.BlockSpec((tq, 512), kmap), pl.BlockSpec((tq, 512), kmap),
                  pl.BlockSpec((tq, LANES), qmap), pl.BlockSpec((SUBLANES, tq), lambda s, qs_r, ks_r: (0, ks_r[s]))] + s_specs,
        out_specs=[pl.BlockSpec((tq, 512), qmap), pl.BlockSpec((tq, 512), qmap), pl.BlockSpec((tq, LANES), qmap)] + s_specs,
        scratch_shapes=[pltpu.VMEM((tq, 512), F32), pltpu.VMEM((FOX_HEADS, tq, 1), F32), pltpu.VMEM((FOX_HEADS, tq, 1), F32)] + s_scratch,
    )
    res = pl.pallas_call(
        body, name=name, grid_spec=grid_spec,
        out_shape=[jax.ShapeDtypeStruct((T, 512), _MXU), jax.ShapeDtypeStruct((T, 512), F32), jax.ShapeDtypeStruct((T, LANES), F32)] + s_shapes,
        compiler_params=_cp(("arbitrary",)),
    )(qs, ks, qn, kn, vv, c, ct, *s_args)
    return res[:3], res[3:]


def _shift_down(ext, k, tm):
    return pltpu.roll(ext, k, axis=0)[HALO:HALO + tm]


def _pool_p(z_ext, g, tm, pos):
    w = POOL_WINDOWS[g]
    s, span = z_ext, 1
    while span < w:
        s = s + pltpu.roll(s, span, axis=0)
        span *= 2
    cnt = jnp.minimum(pos + 1, w).astype(F32)
    return s[HALO:HALO + tm] / cnt - z_ext[HALO:HALO + tm]


def _odd_fwd(name, z, w_pool, s_pool, conv_w8):
    T = z.shape[0]
    tm = _tile(T, 256, HALO)
    r = tm // HALO

    def body(z_ref, zb_ref, wp_ref, sp_ref, cw_ref, y_ref):
        i = pl.program_id(0)
        first = i == 0
        pos = i * tm + _row((tm, LANES))
        for g in range(len(POOL_WINDOWS)):
            sl = slice(g * LANES, (g + 1) * LANES)
            halo = jnp.where(first, 0.0, zb_ref[:, sl])
            z_ext = jnp.concatenate([halo, z_ref[:, sl]], axis=0)
            p = _pool_p(z_ext, g, tm, pos)
            y_ref[:, sl] = (_dot(p, wp_ref[g]) * sp_ref[:, sl]).astype(y_ref.dtype)
        for j in range(CONV_WIDTH // LANES):
            sl = slice(j * LANES, (j + 1) * LANES)
            hd = slice(POOL_WIDTH + j * LANES, POOL_WIDTH + (j + 1) * LANES)
            gb = slice(POOL_WIDTH + CONV_WIDTH + j * LANES, POOL_WIDTH + CONV_WIDTH + (j + 1) * LANES)
            gc = slice(POOL_WIDTH + 2 * CONV_WIDTH + j * LANES, POOL_WIDTH + 2 * CONV_WIDTH + (j + 1) * LANES)
            xg_b = jnp.where(first, 0.0, zb_ref[:, gc] * zb_ref[:, hd])
            xg = jnp.concatenate([xg_b, z_ref[:, gc] * z_ref[:, hd]], axis=0)
            conv = (cw_ref[0:1, sl] * _shift_down(xg, 2, tm) + cw_ref[1:2, sl] * _shift_down(xg, 1, tm)
                    + cw_ref[2:3, sl] * xg[HALO:HALO + tm])
            y_ref[:, POOL_WIDTH + j * LANES:POOL_WIDTH + (j + 1) * LANES] = (z_ref[:, gb] * conv).astype(y_ref.dtype)

    full = lambda s: pl.BlockSpec(s, lambda i: tuple(0 for _ in s))
    return pl.pallas_call(
        body, name=name, grid=(T // tm,),
        in_specs=[pl.BlockSpec((tm, ODD_IN), lambda i: (i, 0)),
                  pl.BlockSpec((HALO, ODD_IN), lambda i: (jnp.maximum(i * r - 1, 0), 0)),
                  full((4, LANES, LANES)), full((1, POOL_WIDTH)), full((SUBLANES, CONV_WIDTH))],
        out_specs=pl.BlockSpec((tm, 1024), lambda i: (i, 0)),
        out_shape=jax.ShapeDtypeStruct((T, 1024), _MXU),
        compiler_params=_cp(("parallel",)),
    )(z, z, w_pool, s_pool, conv_w8)


def _mem_fwd(name, mem, g_mem, wkv, gk):
    M = mem.shape[0]

    def body(mem_ref, g_ref, w_ref, gk_ref, m_ref, kv_ref, kn_ref, v_ref):
        x = mem_ref[...]
        m = ((x * _rms_rows(x)) * g_ref[...]).astype(m_ref.dtype)
        m_ref[...] = m
        kv = _dot(m, w_ref[...])
        kv_ref[...] = kv
        for h in range(XA_HEADS):
            sl = slice(h * LANES, (h + 1) * LANES)
            k = kv[:, sl]
            kn_ref[:, sl] = ((k * _rms_rows(k)) * gk_ref[...]).astype(kn_ref.dtype)
        v_ref[...] = kv[:, XA_WIDTH:].astype(v_ref.dtype)

    return pl.pallas_call(
        body, name=name,
        out_shape=[jax.ShapeDtypeStruct((M, D_MODEL), _MXU), jax.ShapeDtypeStruct((M, 2 * XA_WIDTH), F32),
                   jax.ShapeDtypeStruct((M, XA_WIDTH), _MXU), jax.ShapeDtypeStruct((M, XA_WIDTH), _MXU)],
        compiler_params=pltpu.CompilerParams(vmem_limit_bytes=VMEM_LIMIT),
    )(mem, g_mem.reshape(1, D_MODEL), wkv, gk.reshape(1, XA_HEAD_DIM))


def _xa_probs(qx, gq, kn_h):
    r = _rms_rows(qx)
    qhat = qx * r
    qn = qhat * gq
    s = _dot_nt(qn, kn_h) * (1.0 / math.sqrt(XA_HEAD_DIM))
    s = s - jnp.max(s, axis=-1, keepdims=True)
    e = jnp.exp(s)
    return qhat, r, qn, e / jnp.sum(e, axis=-1, keepdims=True)


def _xa_fwd(name, qx, gq, kn, vx):
    T, M = qx.shape[0], kn.shape[0]
    tm = _tile(T, 512, 16)

    def body(q_ref, gq_ref, k_ref, v_ref, o_ref):
        for h in range(XA_HEADS):
            sl = slice(h * LANES, (h + 1) * LANES)
            _, _, _, p = _xa_probs(q_ref[:, sl], gq_ref[...], k_ref[:, sl])
            o_ref[:, sl] = _dot(p, v_ref[:, sl]).astype(o_ref.dtype)

    full = lambda s: pl.BlockSpec(s, lambda i: tuple(0 for _ in s))
    return pl.pallas_call(
        body, name=name, grid=(T // tm,),
        in_specs=[pl.BlockSpec((tm, XA_WIDTH), lambda i: (i, 0)), full((1, LANES)), full((M, XA_WIDTH)), full((M, XA_WIDTH))],
        out_specs=pl.BlockSpec((tm, XA_WIDTH), lambda i: (i, 0)),
        out_shape=jax.ShapeDtypeStruct((T, XA_WIDTH), _MXU),
        compiler_params=_cp(("parallel",)),
    )(qx, gq.reshape(1, XA_HEAD_DIM), kn, vx)


def _ffn_up(name, h, wg, wu):
    T, K = h.shape
    N = wg.shape[1]
    tm, tn = _tile(T, 512, 16), _tile(N, 1536, LANES)

    def body(h_ref, wg_ref, wu_ref, a_ref, b_ref, s_ref):
        hv = h_ref[...]
        a, b = _dot(hv, wg_ref[...]), _dot(hv, wu_ref[...])
        a_ref[...] = a
        b_ref[...] = b
        s_ref[...] = (jax.nn.silu(a) * b).astype(s_ref.dtype)

    w_spec = pl.BlockSpec((K, tn), lambda j, i: (0, j))
    o_spec = pl.BlockSpec((tm, tn), lambda j, i: (i, j))
    return pl.pallas_call(
        body, name=name, grid=(N // tn, T // tm),
        in_specs=[pl.BlockSpec((tm, K), lambda j, i: (i, 0)), w_spec, w_spec],
        out_specs=[o_spec, o_spec, o_spec],
        out_shape=[jax.ShapeDtypeStruct((T, N), F32), jax.ShapeDtypeStruct((T, N), F32), jax.ShapeDtypeStruct((T, N), _MXU)],
        compiler_params=_cp(("parallel", "parallel")),
    )(h, wg, wu)


def _loss_grad(name, y, target):
    T, Dm = y.shape
    tm = _tile(T, 512, 8)

    def body(y_ref, t_ref, dy_ref, l_ref):
        e = y_ref[...] - t_ref[...]
        dy_ref[...] = e * (1.0 / Dm)

        @pl.when(pl.program_id(0) == 0)
        def _():
            l_ref[...] = jnp.zeros_like(l_ref)

        l_ref[...] += jnp.sum(e * e, axis=0, keepdims=True) * (0.5 / Dm)

    blk = pl.BlockSpec((tm, Dm), lambda i: (i, 0))
    return pl.pallas_call(
        body, name=name, grid=(T // tm,),
        in_specs=[blk, blk], out_specs=[blk, pl.BlockSpec((1, Dm), lambda i: (0, 0))],
        out_shape=[jax.ShapeDtypeStruct((T, Dm), F32), jax.ShapeDtypeStruct((1, Dm), F32)],
        compiler_params=_cp(("arbitrary",)),
    )(y, target)


def _pad_lanes(v, n=LANES):
    v = v.reshape(1, -1)
    return jnp.pad(v, ((0, 0), (0, n - v.shape[1])))


def _layer_fwd(l, x, mem, W, side=None):
    sv = {"x0": x}
    h1 = _norm_fwd(f"norm_mix{l}", x, W["g_mix"])
    sv["h1"] = h1
    if l % 2 == 0:
        z = _mm(f"e_in{l}", [(h1, W["w_in"])], "nn")
        ya, qn, kn, vv, c, ct = _even_pre(f"even_pre{l}", z, W["g_v"], W["w_s"], W["bs_cols"], W["g_qn2"], W["g_kn2"], W["b_f_pad"])
        (yb, ybf, lse), side_out = _fox_fwd(f"fox_fwd{l}", qn, kn, vv, c, ct, side=side)
        sv.update(z=z, ya=ya, qn=qn, kn=kn, vv=vv, c=c, ct=ct, yb=yb, ybf=ybf, lse=lse, side=side_out)
        x1 = _mm(f"e_out{l}", [(ya, W["w_out"][:A_WIDTH]), (yb, W["w_out"][A_WIDTH:])], "nn", residual=x)
    else:
        z = _mm(f"o_in{l}", [(h1, W["w_in"])], "nn")
        ycat = _odd_fwd(f"odd_fwd{l}", z, W["w_pool"], W["s_pool"], W["conv_w8"])
        sv.update(z=z, ycat=ycat)
        x1 = _mm(f"o_out{l}", [(ycat, W["w_out"])], "nn", residual=x)
    sv["x1"] = x1
    h2 = _norm_fwd(f"norm_xa{l}", x1, W["g_xa"])
    qx = _mm(f"xa_q{l}", [(h2, W["wq"])], "nn")
    m, kv, kn_x, vx = _mem_fwd(f"mem_fwd{l}", mem, W["g_mem"], W["wkv"], W["gk"])
    o = _xa_fwd(f"xa_fwd{l}", qx, W["gq"], kn_x, vx)
    x2 = _mm(f"xa_o{l}", [(o, W["wo"])], "nn", residual=x1)
    sv.update(h2=h2, qx=qx, m=m, kv=kv, kn_x=kn_x, vx=vx, o=o, x2=x2)
    h3 = _norm_fwd(f"norm_ffn{l}", x2, W["g_ffn"])
    a, b, s = _ffn_up(f"ffn_up{l}", h3, W["w_gate"], W["w_up"])
    x3 = _mm(f"ffn_down{l}", [(s, W["w_down"])], "nn", residual=x2)
    sv.update(h3=h3, a=a, b=b, s=s)
    return x3, sv


def _layer_weights(l, P):
    i = l // 2
    W = {k: P[k][l] for k in ("g_mix", "g_xa", "g_mem", "g_ffn")}
    W.update(wq=_mx(P["xa_wq"][l]), wkv=_mx(P["xa_wkv"][l]), wo=_mx(P["xa_wo"][l]), gq=P["xa_gq"][l], gk=P["xa_gk"][l],
             w_gate=_mx(P["w_gate"][l]), w_up=_mx(P["w_up"][l]), w_down=_mx(P["w_down"][l]))
    if l % 2 == 0:
        w_in = _mx(P["e_w_in"][i])
        if w_in.shape[1] < EVEN_IN_PAD:
            w_in = jnp.pad(w_in, ((0, 0), (0, EVEN_IN_PAD - w_in.shape[1])))
        W.update(w_in=w_in, g_v=P["e_g_v"][i].reshape(1, A_WIDTH), w_s=P["e_w_s"][i],
                 bs_cols=jnp.pad(P["e_b_s"][i].T, ((0, 0), (0, LANES - A_GROUPS))),
                 g_qn2=jnp.tile(P["e_g_qn"][i], 2).reshape(1, LANES), g_kn2=jnp.tile(P["e_g_kn"][i], 2).reshape(1, LANES),
                 b_f_pad=_pad_lanes(P["e_b_f"][i]), w_out=_mx(P["e_w_out"][i]))
    else:
        W.update(w_in=_mx(P["o_w_in"][i]), w_pool=_mx(P["o_w_pool"][i]), s_pool=P["o_s_pool"][i].reshape(1, POOL_WIDTH),
                 conv_w8=jnp.pad(P["o_conv_w"][i], ((0, SUBLANES - CONV_K), (0, 0))), w_out=_mx(P["o_w_out"][i]))
    return W


def _ffn_bwd_act(name, g, wd, a, b):
    T, N = a.shape
    K = g.shape[1]
    tm, tn = _tile(T, 512, 16), _tile(N, 1536, LANES)

    def body(g_ref, wd_ref, a_ref, b_ref, da_ref, db_ref):
        ds = _dot_nt(g_ref[...], wd_ref[...])
        av = a_ref[...]
        sig = jax.nn.sigmoid(av)
        da_ref[...] = (ds * b_ref[...] * (sig * (1.0 + av * (1.0 - sig)))).astype(da_ref.dtype)
        db_ref[...] = (ds * (av * sig)).astype(db_ref.dtype)

    o_spec = pl.BlockSpec((tm, tn), lambda j, i: (i, j))
    return pl.pallas_call(
        body, name=name, grid=(N // tn, T // tm),
        in_specs=[pl.BlockSpec((tm, K), lambda j, i: (i, 0)), pl.BlockSpec((tn, K), lambda j, i: (j, 0)), o_spec, o_spec],
        out_specs=[o_spec, o_spec],
        out_shape=[jax.ShapeDtypeStruct((T, N), _MXU), jax.ShapeDtypeStruct((T, N), _MXU)],
        compiler_params=_cp(("parallel", "parallel")),
    )(g, wd, a, b)


def _xa_bwd(name, qx, do, gq, kn, vx):
    T, M = qx.shape[0], kn.shape[0]
    tm = _tile(T, 512, 16)

    def body(q_ref, do_ref, gq_ref, k_ref, v_ref, dq_ref, dk_ref, dv_ref, dg_ref):
        @pl.when(pl.program_id(0) == 0)
        def _():
            dk_ref[...] = jnp.zeros_like(dk_ref)
            dv_ref[...] = jnp.zeros_like(dv_ref)
            dg_ref[...] = jnp.zeros_like(dg_ref)

        gqv = gq_ref[...]
        for h in range(XA_HEADS):
            sl = slice(h * LANES, (h + 1) * LANES)
            qhat, r, qn, p = _xa_probs(q_ref[:, sl], gqv, k_ref[:, sl])
            do_h = do_ref[:, sl]
            dp = _dot_nt(do_h, v_ref[:, sl])
            dv_ref[:, sl] += _dot_tn(p, do_h)
            dsm = p * (dp - jnp.sum(p * dp, axis=-1, keepdims=True)) * (1.0 / math.sqrt(XA_HEAD_DIM))
            dqn = _dot(dsm, k_ref[:, sl])
            dk_ref[:, sl] += _dot_tn(dsm, qn)
            dg_ref[...] += jnp.sum(dqn * qhat, axis=0, keepdims=True)
            dxh = dqn * gqv
            dq_ref[:, sl] = (r * (dxh - qhat * jnp.mean(dxh * qhat, axis=-1, keepdims=True))).astype(dq_ref.dtype)

    full = lambda s: pl.BlockSpec(s, lambda i: tuple(0 for _ in s))
    blk = pl.BlockSpec((tm, XA_WIDTH), lambda i: (i, 0))
    return pl.pallas_call(
        body, name=name, grid=(T // tm,),
        in_specs=[blk, blk, full((1, LANES)), full((M, XA_WIDTH)), full((M, XA_WIDTH))],
        out_specs=[blk, full((M, XA_WIDTH)), full((M, XA_WIDTH)), full((1, LANES))],
        out_shape=[jax.ShapeDtypeStruct((T, XA_WIDTH), _MXU), jax.ShapeDtypeStruct((M, XA_WIDTH), F32),
                   jax.ShapeDtypeStruct((M, XA_WIDTH), F32), jax.ShapeDtypeStruct((1, LANES), F32)],
        compiler_params=_cp(("arbitrary",)),
    )(qx, do, gq.reshape(1, XA_HEAD_DIM), kn, vx)


def _mem_bwd(name, mem, g_mem, m, kv, gk, dkn, dvx, wkv):
    M = mem.shape[0]

    def body(mem_ref, g_ref, m_ref, kv_ref, gk_ref, dkn_ref, dvx_ref, w_ref, dw_ref, dgk_ref, dgm_ref):
        gkv = gk_ref[...]
        dgk = jnp.zeros((1, LANES), F32)
        parts = []
        for h in range(XA_HEADS):
            sl = slice(h * LANES, (h + 1) * LANES)
            k = kv_ref[:, sl]
            r = _rms_rows(k)
            khat = k * r
            dkn_h = dkn_ref[:, sl]
            dgk = dgk + jnp.sum(dkn_h * khat, axis=0, keepdims=True)
            dxh = dkn_h * gkv
            parts.append(r * (dxh - khat * jnp.mean(dxh * khat, axis=-1, keepdims=True)))
        dgk_ref[...] = dgk
        dkv = jnp.concatenate(parts + [dvx_ref[...]], axis=1)
        dw_ref[...] = _dot_tn(m_ref[...], dkv)
        dm = _dot_nt(dkv, w_ref[...])
        x = mem_ref[...]
        dgm_ref[...] = jnp.sum(dm * (x * _rms_rows(x)), axis=0, keepdims=True)

    return pl.pallas_call(
        body, name=name,
        out_shape=[jax.ShapeDtypeStruct((D_MODEL, 2 * XA_WIDTH), F32), jax.ShapeDtypeStruct((1, LANES), F32),
                   jax.ShapeDtypeStruct((1, D_MODEL), F32)],
        compiler_params=pltpu.CompilerParams(vmem_limit_bytes=VMEM_LIMIT),
    )(mem, g_mem.reshape(1, D_MODEL), m, kv, gk.reshape(1, XA_HEAD_DIM), dkn, dvx, wkv)


def _fox_bwd_common(q2, k2, v2, dy2, yf2, c_ref, ct_ref, lse_ref, e, h, lo, qi, ki, tq):
    sel = lo if e == 0 else ~lo
    qh = jnp.where(sel, q2, jnp.zeros_like(q2))
    kh = jnp.where(sel, k2, jnp.zeros_like(k2))
    dyh = jnp.where(sel, dy2, 0.0)
    s = _fox_logits(qh, k2, c_ref[:, h:h + 1], ct_ref[h:h + 1, :], qi, ki, tq, tq)
    p = jnp.exp(s - lse_ref[:, h:h + 1])
    dp = _dot_nt(dyh, v2)
    delta = jnp.sum(dyh * yf2, axis=-1, keepdims=True)
    return p, p * (dp - delta), qh, kh, dyh


def _fox_bwd_dq(name, qn, kn, vv, c, ct, lse, dycat, ybf, side=None):
    T = qn.shape[0]
    tq = _tile(T, 512, LANES)
    nblk = T // tq
    qs, ks = _tri_steps(nblk, True)
    nsteps = int(qs.shape[0])
    scale = 1.0 / math.sqrt(FOX_HEAD_DIM)
    s_specs, s_args, s_shapes, s_scratch = _side_specs(side)

    def body(qs_ref, ks_ref, *refs):
        main_in, side_in, main_out, side_out, _, sems = _split_side(refs, 8, 2, 0, side)
        q_ref, k_ref, v_ref, c_ref, ct_ref, lse_ref, dy_ref, yf_ref = main_in
        dq_ref, drow_ref = main_out
        s_id = pl.program_id(0)
        qi, ki = qs_ref[s_id], ks_ref[s_id]
        _side_run(side, s_id, nsteps, side_in, side_out, sems)

        @pl.when(ki == 0)
        def _():
            dq_ref[...] = jnp.zeros_like(dq_ref)
            drow_ref[...] = jnp.zeros_like(drow_ref)

        lo = _lane((tq, LANES)) < FOX_HEAD_DIM
        for j in range(FOX_WIDTH // LANES):
            sl = slice(j * LANES, (j + 1) * LANES)
            q2, k2, v2, dy2, yf2 = q_ref[:, sl], k_ref[:, sl], v_ref[:, sl], dy_ref[:, sl], yf_ref[:, sl]
            tot = None
            for e in range(2):
                h = 2 * j + e
                _, ds, _, kh, _ = _fox_bwd_common(q2, k2, v2, dy2, yf2, c_ref, ct_ref, lse_ref, e, h, lo, qi, ki, tq)
                drow_ref[...] += jnp.where(_lane((tq, LANES)) == h, jnp.sum(ds, axis=-1, keepdims=True), 0.0)
                part = _dot(ds, kh)
                tot = part if tot is None else tot + part
            dq_ref[:, sl] += tot * scale

    qmap = lambda s, a, b: (a[s], 0)
    kmap = lambda s, a, b: (b[s], 0)
    grid_spec = pltpu.PrefetchScalarGridSpec(
        num_scalar_prefetch=2, grid=(int(qs.shape[0]),),
        in_specs=[pl.BlockSpec((tq, 512), qmap), pl.BlockSpec((tq, 512), kmap), pl.BlockSpec((tq, 512), kmap),
                  pl.BlockSpec((tq, LANES), qmap), pl.BlockSpec((SUBLANES, tq), lambda s, a, b: (0, b[s])),
                  pl.BlockSpec((tq, LANES), qmap), pl.BlockSpec((tq, 512), lambda s, a, b: (a[s], 1)), pl.BlockSpec((tq, 512), qmap)] + s_specs,
        out_specs=[pl.BlockSpec((tq, 512), qmap), pl.BlockSpec((tq, LANES), qmap)] + s_specs,
        scratch_shapes=s_scratch,
    )
    res = pl.pallas_call(
        body, name=name, grid_spec=grid_spec,
        out_shape=[jax.ShapeDtypeStruct((T, 512), F32), jax.ShapeDtypeStruct((T, LANES), F32)] + s_shapes,
        compiler_params=_cp(("arbitrary",)),
    )(qs, ks, qn, kn, vv, c, ct, lse, dycat, ybf, *s_args)
    return res[:2], res[2:]


def _fox_bwd_dkv(name, qn, kn, vv, c, ct, lse, dycat, ybf, side=None):
    T = qn.shape[0]
    tq = _tile(T, 512, LANES)
    nblk = T // tq
    qs, ks = _tri_steps(nblk, False)
    nsteps = int(qs.shape[0])
    scale = 1.0 / math.sqrt(FOX_HEAD_DIM)
    s_specs, s_args, s_shapes, s_scratch = _side_specs(side)

    def body(qs_ref, ks_ref, *refs):
        main_in, side_in, main_out, side_out, _, sems = _split_side(refs, 8, 3, 0, side)
        q_ref, k_ref, v_ref, c_ref, ct_ref, lse_ref, dy_ref, yf_ref = main_in
        dk_ref, dv_ref, dct_ref = main_out
        s_id = pl.program_id(0)
        qi, ki = qs_ref[s_id], ks_ref[s_id]
        _side_run(side, s_id, nsteps, side_in, side_out, sems)

        @pl.when(qi == ki)
        def _():
            dk_ref[...] = jnp.zeros_like(dk_ref)
            dv_ref[...] = jnp.zeros_like(dv_ref)
            dct_ref[...] = jnp.zeros_like(dct_ref)

        lo = _lane((tq, LANES)) < FOX_HEAD_DIM
        for j in range(FOX_WIDTH // LANES):
            sl = slice(j * LANES, (j + 1) * LANES)
            q2, k2, v2, dy2, yf2 = q_ref[:, sl], k_ref[:, sl], v_ref[:, sl], dy_ref[:, sl], yf_ref[:, sl]
            dk_t, dv_t = None, None
            for e in range(2):
                h = 2 * j + e
                p, ds, qh, _, dyh = _fox_bwd_common(q2, k2, v2, dy2, yf2, c_ref, ct_ref, lse_ref, e, h, lo, qi, ki, tq)
                a, b = _dot_tn(ds, qh), _dot_tn(p, dyh)
                dk_t = a if dk_t is None else dk_t + a
                dv_t = b if dv_t is None else dv_t + b
                dct_ref[h:h + 1, :] -= jnp.sum(ds, axis=0, keepdims=True)
            dk_ref[:, sl] += dk_t * scale
            dv_ref[:, sl] += dv_t

    qmap = lambda s, a, b: (a[s], 0)
    kmap = lambda s, a, b: (b[s], 0)
    ctmap = lambda s, a, b: (0, b[s])
    grid_spec = pltpu.PrefetchScalarGridSpec(
        num_scalar_prefetch=2, grid=(int(qs.shape[0]),),
        in_specs=[pl.BlockSpec((tq, 512), qmap), pl.BlockSpec((tq, 512), kmap), pl.BlockSpec((tq, 512), kmap),
                  pl.BlockSpec((tq, LANES), qmap), pl.BlockSpec((SUBLANES, tq), ctmap),
                  pl.BlockSpec((tq, LANES), qmap), pl.BlockSpec((tq, 512), lambda s, a, b: (a[s], 1)), pl.BlockSpec((tq, 512), qmap)] + s_specs,
        out_specs=[pl.BlockSpec((tq, 512), kmap), pl.BlockSpec((tq, 512), kmap), pl.BlockSpec((SUBLANES, tq), ctmap)] + s_specs,
        scratch_shapes=s_scratch,
    )
    res = pl.pallas_call(
        body, name=name, grid_spec=grid_spec,
        out_shape=[jax.ShapeDtypeStruct((T, 512), F32), jax.ShapeDtypeStruct((T, 512), F32), jax.ShapeDtypeStruct((SUBLANES, T), F32)] + s_shapes,
        compiler_params=_cp(("arbitrary",)),
    )(qs, ks, qn, kn, vv, c, ct, lse, dycat, ybf, *s_args)
    return res[:3], res[3:]


def _fold64(row):
    return row + pltpu.roll(row, FOX_HEAD_DIM, axis=1)


def _even_bwd(name, z, dycat, dqn, dkn, dvv, dct, drow, g_v, w_s, bs_cols, g_qn2, g_kn2, b_f_pad):
    T = z.shape[0]
    tm = _tile(T, 256, A_BLOCK)
    nb, nt = tm // A_BLOCK, T // tm

    def body(z_ref, dya_ref, dq_ref, dk_ref, dv_ref, dct_ref, drow_ref, gv_ref, ws_ref, bs_ref, gq_ref, gk_ref, bf_ref,
             dz_ref, dws_ref, dbs_ref, dgv_ref, dgq_ref, dgk_ref, dbf_ref, carry):
        @pl.when(pl.program_id(0) == 0)
        def _():
            for r in (dws_ref, dbs_ref, dgv_ref, dgq_ref, dgk_ref, dbf_ref, carry):
                r[...] = jnp.zeros_like(r)

        keep = (_row((A_BLOCK, A_BLOCK)) // CHUNK) >= (_lane((A_BLOCK, A_BLOCK)) // CHUNK)
        for g in range(A_GROUPS):
            sl = slice(g * A_GROUP_DIM, (g + 1) * A_GROUP_DIM)
            wm = _mx(_masked_ws(ws_ref, g))
            zu = z_ref[:, sl]
            zv = z_ref[:, A_WIDTH + g * A_GROUP_DIM:A_WIDTH + (g + 1) * A_GROUP_DIM]
            u, v = _gelu(zu), _gelu(zv)
            r = _rms_rows(v)
            vhat = v * r
            gvv = gv_ref[:, sl]
            vn = _mx(vhat * gvv)
            dya = dya_ref[:, sl]
            du_parts, dvn_parts = [], []
            for n in range(nb):
                rows = slice(n * A_BLOCK, (n + 1) * A_BLOCK)
                s = jnp.dot(wm, vn[rows], preferred_element_type=F32) + bs_ref[:, g:g + 1]
                du_parts.append(dya[rows] * s)
                d_s = dya[rows] * u[rows]
                dbs_ref[...] += jnp.where(_lane((A_BLOCK, LANES)) == g, jnp.sum(d_s, axis=-1, keepdims=True), 0.0)
                dws_ref[g] += jnp.where(keep, _dot_nt(d_s, vn[rows]), 0.0)
                dvn_parts.append(_dot_tn(wm, d_s))
            dvn = jnp.concatenate(dvn_parts, axis=0)
            dgv_ref[:, sl] += jnp.sum(dvn * vhat, axis=0, keepdims=True)
            dxh = dvn * gvv
            dv = r * (dxh - vhat * jnp.mean(dxh * vhat, axis=-1, keepdims=True))
            dz_ref[:, sl] = (jnp.concatenate(du_parts, axis=0) * _gelu_grad(zu)).astype(dz_ref.dtype)
            dz_ref[:, A_WIDTH + g * A_GROUP_DIM:A_WIDTH + (g + 1) * A_GROUP_DIM] = (dv * _gelu_grad(zv)).astype(dz_ref.dtype)
        o = 2 * A_WIDTH
        for j in range(FOX_WIDTH // LANES):
            sl = slice(j * LANES, (j + 1) * LANES)
            for (off, d_ref, g_ref, dg_ref) in ((o, dq_ref, gq_ref, dgq_ref), (o + FOX_WIDTH, dk_ref, gk_ref, dgk_ref)):
                zq = z_ref[:, off + j * LANES:off + (j + 1) * LANES]
                r = _head64_rms(zq)
                qhat = zq * r
                dn = d_ref[:, sl]
                dg_ref[...] += _fold64(jnp.sum(dn * qhat, axis=0, keepdims=True))
                dxh = dn * g_ref[...]
                dz_ref[:, off + j * LANES:off + (j + 1) * LANES] = (r * (dxh - qhat * _head64_mean(dxh * qhat))).astype(dz_ref.dtype)
            dz_ref[:, o + 2 * FOX_WIDTH + j * LANES:o + 2 * FOX_WIDTH + (j + 1) * LANES] = dv_ref[:, sl].astype(dz_ref.dtype)
        dct_v = dct_ref[...] + drow_ref[...].T[0:SUBLANES, :]
        upper = (_row((tm, tm)) >= _lane((tm, tm))).astype(F32)
        dlf_t = _dot_f32(dct_v, upper) + carry[:, 0:1]
        carry[...] += jnp.sum(dct_v, axis=1, keepdims=True)
        dlf = jnp.concatenate([dlf_t, jnp.zeros((LANES - SUBLANES, tm), F32)], axis=0).T
        zf = z_ref[:, o + 3 * FOX_WIDTH:o + 3 * FOX_WIDTH + LANES]
        dzf = dlf * jax.nn.sigmoid(-(zf + bf_ref[...]))
        dbf_ref[...] += jnp.sum(dzf, axis=0, keepdims=True)
        dz_ref[:, o + 3 * FOX_WIDTH:o + 3 * FOX_WIDTH + LANES] = dzf.astype(dz_ref.dtype)

    rev = lambda i: nt - 1 - i
    wide = lambda n, col=0: pl.BlockSpec((tm, n), lambda i: (rev(i), col))
    full = lambda s: pl.BlockSpec(s, lambda i: tuple(0 for _ in s))
    vec = full((1, LANES))
    return pl.pallas_call(
        body, name=name, grid=(nt,),
        in_specs=[wide(EVEN_IN_PAD), wide(512), wide(512), wide(512), wide(512), pl.BlockSpec((SUBLANES, tm), lambda i: (0, rev(i))),
                  wide(LANES), full((1, A_WIDTH)), full((A_GROUPS, A_BLOCK, A_BLOCK)), full((A_BLOCK, LANES)), vec, vec, vec],
        out_specs=[wide(EVEN_IN_PAD), full((A_GROUPS, A_BLOCK, A_BLOCK)), full((A_BLOCK, LANES)), full((1, A_WIDTH)), vec, vec, vec],
        out_shape=[jax.ShapeDtypeStruct((T, EVEN_IN_PAD), _MXU), jax.ShapeDtypeStruct((A_GROUPS, A_BLOCK, A_BLOCK), F32),
                   jax.ShapeDtypeStruct((A_BLOCK, LANES), F32), jax.ShapeDtypeStruct((1, A_WIDTH), F32),
                   jax.ShapeDtypeStruct((1, LANES), F32), jax.ShapeDtypeStruct((1, LANES), F32), jax.ShapeDtypeStruct((1, LANES), F32)],
        scratch_shapes=[pltpu.VMEM((SUBLANES, LANES), F32)],
        compiler_params=_cp(("arbitrary",)),
    )(z, dycat, dqn, dkn, dvv, dct, drow, g_v, w_s, bs_cols, g_qn2, g_kn2, b_f_pad)


def _shift_up(ext, k, tm):
    return pltpu.roll(ext, ext.shape[0] - k, axis=0)[0:tm]


def _odd_bwd(name, z, dycat, w_pool, s_pool, conv_w8):
    T = z.shape[0]
    tm = _tile(T, 256, HALO)
    r, nt = tm // HALO, T // tm
    n_ext = tm + HALO

    def body(z_ref, zb_ref, zn_ref, dy_ref, dyn_ref, wp_ref, sp_ref, cw_ref, dz_ref, dwp_ref, dsp_ref, dcw_ref):
        i = pl.program_id(0)

        @pl.when(i == 0)
        def _():
            for rr in (dwp_ref, dsp_ref, dcw_ref):
                rr[...] = jnp.zeros_like(rr)

        first, last = i == 0, i == nt - 1
        pos = i * tm + _row((tm, LANES))
        pos_ext = i * tm + _row((n_ext, LANES))
        for g, w in enumerate(POOL_WINDOWS):
            sl = slice(g * LANES, (g + 1) * LANES)
            z_ext = jnp.concatenate([jnp.where(first, 0.0, zb_ref[:, sl]), z_ref[:, sl]], axis=0)
            p = _pool_p(z_ext, g, tm, pos)
            spv = sp_ref[:, sl]
            dyc = dy_ref[:, sl]
            dsp_ref[:, sl] += jnp.sum(dyc * _dot(p, wp_ref[g]), axis=0, keepdims=True)
            dpw_ext = jnp.concatenate([dyc, jnp.where(last, 0.0, dyn_ref[:, sl])], axis=0) * spv
            dwp_ref[g] += _dot_tn(p, dpw_ext[0:tm])
            dp_ext = _dot_nt(dpw_ext, wp_ref[g])
            f = dp_ext / jnp.minimum(pos_ext + 1, w).astype(F32)
            span = 1
            while span < w:
                f = f + pltpu.roll(f, n_ext - span, axis=0)
                span *= 2
            dz_ref[:, sl] = (f[0:tm] - dp_ext[0:tm]).astype(dz_ref.dtype)
        for j in range(CONV_WIDTH // LANES):
            sl = slice(j * LANES, (j + 1) * LANES)
            hd = slice(POOL_WIDTH + j * LANES, POOL_WIDTH + (j + 1) * LANES)
            gb = slice(POOL_WIDTH + CONV_WIDTH + j * LANES, POOL_WIDTH + CONV_WIDTH + (j + 1) * LANES)
            gc = slice(POOL_WIDTH + 2 * CONV_WIDTH + j * LANES, POOL_WIDTH + 2 * CONV_WIDTH + (j + 1) * LANES)
            ysl = slice(POOL_WIDTH + j * LANES, POOL_WIDTH + (j + 1) * LANES)
            hdv, gbv, gcv = z_ref[:, hd], z_ref[:, gb], z_ref[:, gc]
            xg = gcv * hdv
            xg_ext = jnp.concatenate([jnp.where(first, 0.0, zb_ref[:, gc] * zb_ref[:, hd]), xg], axis=0)
            xg1, xg2 = _shift_down(xg_ext, 1, tm), _shift_down(xg_ext, 2, tm)
            w0, w1, w2 = cw_ref[0:1, sl], cw_ref[1:2, sl], cw_ref[2:3, sl]
            conv = w0 * xg2 + w1 * xg1 + w2 * xg
            dyd = dy_ref[:, ysl]
            dconv = dyd * gbv
            dconv_ext = jnp.concatenate([dconv, jnp.where(last, 0.0, dyn_ref[:, ysl] * zn_ref[:, gb])], axis=0)
            dcw_ref[0:1, sl] += jnp.sum(dconv * xg2, axis=0, keepdims=True)
            dcw_ref[1:2, sl] += jnp.sum(dconv * xg1, axis=0, keepdims=True)
            dcw_ref[2:3, sl] += jnp.sum(dconv * xg, axis=0, keepdims=True)
            dxg = w2 * dconv + w1 * _shift_up(dconv_ext, 1, tm) + w0 * _shift_up(dconv_ext, 2, tm)
            dz_ref[:, hd] = (dxg * gcv).astype(dz_ref.dtype)
            dz_ref[:, gb] = (dyd * conv).astype(dz_ref.dtype)
            dz_ref[:, gc] = (dxg * hdv).astype(dz_ref.dtype)

    full = lambda s: pl.BlockSpec(s, lambda i: tuple(0 for _ in s))
    back = lambda n: pl.BlockSpec((HALO, n), lambda i: (jnp.maximum(i * r - 1, 0), 0))
    nxt = lambda n: pl.BlockSpec((HALO, n), lambda i: (jnp.minimum((i + 1) * r, T // HALO - 1), 0))
    return pl.pallas_call(
        body, name=name, grid=(nt,),
        in_specs=[pl.BlockSpec((tm, ODD_IN), lambda i: (i, 0)), back(ODD_IN), nxt(ODD_IN),
                  pl.BlockSpec((tm, 1024), lambda i: (i, 0)), nxt(1024),
                  full((4, LANES, LANES)), full((1, POOL_WIDTH)), full((SUBLANES, CONV_WIDTH))],
        out_specs=[pl.BlockSpec((tm, ODD_IN), lambda i: (i, 0)), full((4, LANES, LANES)), full((1, POOL_WIDTH)), full((SUBLANES, CONV_WIDTH))],
        out_shape=[jax.ShapeDtypeStruct((T, ODD_IN), _MXU), jax.ShapeDtypeStruct((4, LANES, LANES), F32),
                   jax.ShapeDtypeStruct((1, POOL_WIDTH), F32), jax.ShapeDtypeStruct((SUBLANES, CONV_WIDTH), F32)],
        compiler_params=_cp(("arbitrary",)),
    )(z, z, z, dycat, dycat, w_pool, s_pool, conv_w8)


def _layer_bwd(l, g3, mem, W, sv, side_first=None, side_second_fn=None):
    i = l // 2
    G = {}
    side_out = ((), ())
    da, db = _ffn_bwd_act(f"ffn_bwd_act{l}", g3, W["w_down"], sv["a"], sv["b"])
    G["w_down"] = _mm(f"d_w_down{l}", [(sv["s"], g3)], "tn")
    dh3 = _mm(f"d_h3{l}", [(da, W["w_gate"]), (db, W["w_up"])], "nt")
    G["w_gate"] = _mm(f"d_w_gate{l}", [(sv["h3"], da)], "tn")
    G["w_up"] = _mm(f"d_w_up{l}", [(sv["h3"], db)], "tn")
    g2, dg = _norm_bwd(f"norm_ffn_bwd{l}", dh3, sv["x2"], W["g_ffn"], g3)
    G["g_ffn"] = dg[0]

    do = _mm(f"d_o{l}", [(g2, W["wo"])], "nt", out_dtype=_MXU)
    G["xa_wo"] = _mm(f"d_wo{l}", [(sv["o"], g2)], "tn")
    dqx, dkn, dvx, dgq = _xa_bwd(f"xa_bwd{l}", sv["qx"], do, W["gq"], sv["kn_x"], sv["vx"])
    G["xa_gq"] = dgq[0]
    dh2 = _mm(f"d_h2{l}", [(dqx, W["wq"])], "nt")
    G["xa_wq"] = _mm(f"d_wq{l}", [(sv["h2"], dqx)], "tn")
    g1, dg = _norm_bwd(f"norm_xa_bwd{l}", dh2, sv["x1"], W["g_xa"], g2)
    G["g_xa"] = dg[0]
    dwkv, dgk, dgm = _mem_bwd(f"mem_bwd{l}", mem, W["g_mem"], sv["m"], sv["kv"], W["gk"], dkn, dvx, W["wkv"])
    G["xa_wkv"], G["xa_gk"], G["g_mem"] = dwkv, dgk[0], dgm[0]

    dycat = _mm(f"d_ycat{l}", [(g1, W["w_out"])], "nt")
    if l % 2 == 0:
        G["e_w_out"] = jnp.concatenate([_mm(f"d_e_wout_a{l}", [(sv["ya"], g1)], "tn"), _mm(f"d_e_wout_b{l}", [(sv["yb"], g1)], "tn")], axis=0)
        fa = (sv["qn"], sv["kn"], sv["vv"], sv["c"], sv["ct"], sv["lse"], dycat, sv["ybf"])
        (dkn_f, dvv, dct), so1 = _fox_bwd_dkv(f"fox_bwd_dkv{l}", *fa, side=side_first)
        (dqn, drow), so2 = _fox_bwd_dq(f"fox_bwd_dq{l}", *fa, side=side_second_fn(G) if side_second_fn is not None else None)
        side_out = (so1, so2)
        dz, dws, dbs, dgv, dgq2, dgk2, dbf = _even_bwd(f"even_bwd{l}", sv["z"], dycat, dqn, dkn_f, dvv, dct, drow, W["g_v"], W["w_s"],
                                                      W["bs_cols"], W["g_qn2"], W["g_kn2"], W["b_f_pad"])
        G.update(e_w_s=dws, e_b_s=dbs[:, :A_GROUPS].T, e_g_v=dgv[0], e_g_qn=dgq2[0, :FOX_HEAD_DIM], e_g_kn=dgk2[0, :FOX_HEAD_DIM],
                 e_b_f=dbf[0, :FOX_HEADS])
        G["e_w_in"] = _mm(f"d_e_win{l}", [(sv["h1"], dz)], "tn")
    else:
        G["o_w_out"] = _mm(f"d_o_wout{l}", [(sv["ycat"], g1)], "tn")
        dz, dwp, dsp, dcw = _odd_bwd(f"odd_bwd{l}", sv["z"], dycat, W["w_pool"], W["s_pool"], W["conv_w8"])
        G.update(o_w_pool=dwp, o_s_pool=dsp[0], o_conv_w=dcw[:CONV_K])
        G["o_w_in"] = _mm(f"d_o_win{l}", [(sv["h1"], dz)], "tn")
    dh1 = _mm(f"d_h1{l}", [(dz, W["w_in"])], "nt")
    g0, dg = _norm_bwd(f"norm_mix_bwd{l}", dh1, sv["x0"], W["g_mix"], g1)
    G["g_mix"] = dg[0]
    return g0, G, side_out


PACK_W = 1024
MESH_T = pl.DeviceIdType.MESH
_ANY = pl.BlockSpec(memory_space=pl.ANY)


def _my_place():
    x, y, c = lax.axis_index("x"), lax.axis_index("y"), lax.axis_index("c")
    return x, y, c


def _flip(v, bit):
    return 1 - v if bit else v


N_PEERS = N_DEV - 1


def _all_gather(name, blks):
    n = len(blks)

    def body(*refs):
        x_refs, out_refs = refs[:n], refs[n:2 * n]
        send_sems, recv_sems, local_sems = refs[2 * n:]
        x, y, c = _my_place()
        me, sibling = (x, y, c), (x, y, 1 - c)
        chips = [(1 - x, y), (x, 1 - y), (1 - x, 1 - y)]

        def copy(t, k, block, to, src=None):
            slab = out_refs[t].at[4 * block[0] + 2 * block[1] + block[2]]
            return pltpu.make_async_remote_copy(
                src_ref=slab if src is None else src, dst_ref=slab,
                send_sem=send_sems.at[N_PEERS * t + k], recv_sem=recv_sems.at[N_PEERS * t + k], device_id=to, device_id_type=MESH_T)

        mine, first, passed = [], [], []
        for t in range(n):
            mine.append(pltpu.make_async_copy(x_refs[t], out_refs[t].at[4 * x + 2 * y + c], local_sems.at[t]))
            first.append(copy(t, 0, me, sibling, src=x_refs[t]))
            first += [copy(t, 1 + j, me, (*chip, c), src=x_refs[t]) for j, chip in enumerate(chips)]
        for cp in mine + first:
            cp.start()
        for j, chip in enumerate(chips):
            for t in range(n):
                copy(t, 1 + j, (*chip, c), me).wait_recv()
                cp = copy(t, 4 + j, (*chip, c), sibling)
                cp.start()
                passed.append(cp)
        for t in range(n):
            copy(t, 0, sibling, me).wait_recv()
            for j, chip in enumerate(chips):
                copy(t, 4 + j, (*chip, 1 - c), me).wait_recv()
        for cp in first + passed:
            cp.wait_send()
        for cp in mine:
            cp.wait()

    return pl.pallas_call(
        body, name=name,
        out_shape=[jax.ShapeDtypeStruct((N_DEV,) + b.shape, b.dtype) for b in blks],
        in_specs=[_ANY] * n, out_specs=[_ANY] * n,
        scratch_shapes=[pltpu.SemaphoreType.DMA((N_PEERS * n,)), pltpu.SemaphoreType.DMA((N_PEERS * n,)), pltpu.SemaphoreType.DMA((n,))],
    )(*blks)


class _Direct:
    def __init__(self, kind, arrays):
        self.kind, self.arrays, self.n = kind, list(arrays), len(arrays)
        if kind == "scatter":
            self.out_shape = [jax.ShapeDtypeStruct(a.shape, a.dtype) for a in arrays]
        else:
            self.out_shape = [jax.ShapeDtypeStruct((N_DEV,) + a.shape, a.dtype) for a in arrays]
        self.scratch = [pltpu.SemaphoreType.DMA((N_PEERS * self.n,)), pltpu.SemaphoreType.DMA((N_PEERS * self.n,)),
                        pltpu.SemaphoreType.DMA((self.n,))]

    def _copies(self, in_refs, out_refs, sems):
        send_sems, recv_sems, local_sems = sems
        x, y, c = _my_place()
        me = 4 * x + 2 * y + c
        mine, sends, recvs = [], [], []
        for t in range(self.n):
            src_of = (lambda idx, t=t: in_refs[t].at[idx]) if self.kind == "scatter" else (lambda idx, t=t: in_refs[t])
            mine.append(pltpu.make_async_copy(src_of(me), out_refs[t].at[me], local_sems.at[t]))
            for m in range(1, N_DEV):
                px, py, pc = _flip(x, m & 4), _flip(y, m & 2), _flip(c, m & 1)
                pidx = 4 * px + 2 * py + pc
                sem = dict(send_sem=send_sems.at[N_PEERS * t + m - 1], recv_sem=recv_sems.at[N_PEERS * t + m - 1], device_id_type=MESH_T)
                sends.append(pltpu.make_async_remote_copy(src_ref=src_of(pidx), dst_ref=out_refs[t].at[me], device_id=(px, py, pc), **sem))
                recvs.append(pltpu.make_async_remote_copy(src_ref=src_of(pidx), dst_ref=out_refs[t].at[pidx], device_id=(x, y, c), **sem))
        return mine, sends, recvs

    def start(self, in_refs, out_refs, sems):
        mine, sends, _ = self._copies(in_refs, out_refs, sems)
        for cp in mine + sends:
            cp.start()

    def finish(self, in_refs, out_refs, sems):
        mine, sends, recvs = self._copies(in_refs, out_refs, sems)
        for cp in recvs:
            cp.wait_recv()
        for cp in sends:
            cp.wait_send()
        for cp in mine:
            cp.wait()


def _split_side(refs, n_main_in, n_main_out, n_main_scratch, side):
    ns = side.n if side is not None else 0
    i = 0
    main_in = refs[i:i + n_main_in]; i += n_main_in
    side_in = refs[i:i + ns]; i += ns
    main_out = refs[i:i + n_main_out]; i += n_main_out
    side_out = refs[i:i + ns]; i += ns
    main_scr = refs[i:i + n_main_scratch]; i += n_main_scratch
    return main_in, side_in, main_out, side_out, main_scr, refs[i:]


def _all_to_all(name, gs):
    side = _Direct("scatter", gs)

    def body(*refs):
        _, side_in, _, side_out, _, sems = _split_side(refs, 0, 0, 0, side)
        side.start(side_in, side_out, sems)
        side.finish(side_in, side_out, sems)

    return pl.pallas_call(
        body, name=name, out_shape=side.out_shape,
        in_specs=[_ANY] * side.n, out_specs=[_ANY] * side.n, scratch_shapes=side.scratch,
    )(*gs)


def _unblock(name, blocks, layer, n_out):
    _, _, K, nb = blocks.shape
    tk = _tile(K, 256, 16)

    def body(b_ref, o_ref):
        for j in range(N_DEV):
            o_ref[:, j * nb:(j + 1) * nb] = b_ref[j]
        if n_out > N_DEV * nb:
            o_ref[:, N_DEV * nb:n_out] = jnp.zeros((tk, n_out - N_DEV * nb), o_ref.dtype)

    return pl.pallas_call(
        body, name=name, grid=(K // tk,),
        in_specs=[pl.BlockSpec((N_DEV, None, tk, nb), lambda i: (0, layer, i, 0))],
        out_specs=pl.BlockSpec((tk, n_out), lambda i: (i, 0)),
        out_shape=jax.ShapeDtypeStruct((K, n_out), blocks.dtype),
        compiler_params=_cp(("parallel",)),
    )(blocks)


def _block(name, full, nb):
    K, N = full.shape
    tk = _tile(K, 256, 8)

    def body(f_ref, o_ref):
        for j in range(N_DEV):
            o_ref[j] = f_ref[:, j * nb:(j + 1) * nb]

    return pl.pallas_call(
        body, name=name, grid=(K // tk,),
        in_specs=[pl.BlockSpec((tk, N), lambda i: (i, 0))],
        out_specs=pl.BlockSpec((N_DEV, tk, nb), lambda i: (0, i, 0)),
        out_shape=jax.ShapeDtypeStruct((N_DEV, K, nb), full.dtype),
        compiler_params=_cp(("parallel",)),
    )(full)


ADAM_BLOCK_ELEMS = 256 * 1024


def _adamw(name, parts, w, m, v, layer=0):
    shape = w.shape[1:]
    cols = shape[-1]
    rows = math.prod(shape[:-1])
    nl = w.shape[0]
    parts, w, m, v = parts.reshape(N_DEV, rows, cols), w.reshape(nl, rows, cols), m.reshape(nl, rows, cols), v.reshape(nl, rows, cols)
    tr = _tile(rows, max(SUBLANES, ADAM_BLOCK_ELEMS // cols), SUBLANES)

    def body(p_ref, w_ref, m_ref, v_ref, g_ref, d_ref, mo_ref, vo_ref):
        g = p_ref[0]
        for s in range(1, N_DEV):
            g = g + p_ref[s]
        mn = ADAM_B1 * m_ref[...] + (1.0 - ADAM_B1) * g
        vn = ADAM_B2 * v_ref[...] + (1.0 - ADAM_B2) * jnp.square(g)
        m_hat = mn / (1.0 - ADAM_B1 ** ADAM_STEP)
        v_hat = vn / (1.0 - ADAM_B2 ** ADAM_STEP)
        g_ref[...] = g
        d_ref[...] = -ADAM_LR * (m_hat / (jnp.sqrt(v_hat) + ADAM_EPS) + ADAM_WD * w_ref[...])
        mo_ref[...] = mn
        vo_ref[...] = vn

    blk = pl.BlockSpec((tr, cols), lambda i: (i, 0))
    lblk = pl.BlockSpec((None, tr, cols), lambda i: (layer, i, 0))
    shp = jax.ShapeDtypeStruct((rows, cols), F32)
    res = pl.pallas_call(
        body, name=name, grid=(rows // tr,),
        in_specs=[pl.BlockSpec((N_DEV, tr, cols), lambda i: (0, i, 0)), lblk, lblk, lblk],
        out_specs=[blk, blk, blk, blk], out_shape=[shp, shp, shp, shp],
        compiler_params=_cp(("parallel",)),
    )(parts, w, m, v)
    return [r.reshape(shape) for r in res]


SHARDED = dict(xa_wq=1, xa_wkv=1, xa_wo=2, w_gate=2, w_up=2, w_down=1, e_w_in=2, e_w_out=1, o_w_in=2, o_s_pool=1, o_conv_w=2, o_w_out=1)
LAYER_SPLIT = ("xa_wq", "xa_wkv", "xa_wo", "w_gate", "w_up", "w_down")
ELEMENTWISE_SHARDED = ("o_s_pool", "o_conv_w")
REPLICATED = ("g_mix", "g_xa", "g_mem", "xa_gq", "xa_gk", "g_ffn", "e_b_f", "e_g_v", "e_w_s", "e_b_s", "e_g_qn", "e_g_kn", "o_w_pool")
WEIGHTS = ("g_mix", "g_xa", "g_mem", "xa_wq", "xa_wkv", "xa_wo", "xa_gq", "xa_gk", "g_ffn", "w_gate", "w_up", "w_down", "e_w_in", "e_b_f",
           "e_g_v", "e_w_s", "e_b_s", "e_g_qn", "e_g_kn", "e_w_out", "o_w_in", "o_w_pool", "o_s_pool", "o_conv_w", "o_w_out")


def _rows_for(n, mult):
    return -(-n // (PACK_W * mult)) * mult


def _pack(arrs, rows, dtype):
    flat = jnp.concatenate([a.reshape(-1).astype(dtype) for a in arrs])
    return jnp.pad(flat, (0, rows * PACK_W - flat.shape[0])).reshape(rows, PACK_W)


def _unpack(slab, shapes):
    flat, out, off = slab.reshape(-1), [], 0
    for s in shapes:
        n = math.prod(s)
        out.append(flat[off:off + n].reshape(s))
        off += n
    return out


def _to_blocks(full, axis):
    s = full.shape
    return jnp.moveaxis(full.reshape(s[:axis] + (N_DEV, s[axis] // N_DEV) + s[axis + 1:]), axis, 0)


def _from_blocks(blocks, axis):
    b = jnp.moveaxis(blocks, 0, axis)
    s = b.shape
    return b.reshape(s[:axis] + (s[axis] * s[axis + 1],) + s[axis + 2:])


def kernel(x, mem, g_mix, g_xa, g_mem, xa_wq, xa_wkv, xa_wo, xa_gq, xa_gk, g_ffn, w_gate, w_up, w_down, e_w_in, e_b_f, e_g_v, e_w_s, e_b_s, e_g_qn, e_g_kn, e_w_out, o_w_in, o_w_pool, o_s_pool, o_conv_w, o_w_out, loss_target, m_g_mix, m_g_xa, m_g_mem, m_xa_wq, m_xa_wkv, m_xa_wo, m_xa_gq, m_xa_gk, m_g_ffn, m_w_gate, m_w_up, m_w_down, m_e_w_in, m_e_b_f, m_e_g_v, m_e_w_s, m_e_b_s, m_e_g_qn, m_e_g_kn, m_e_w_out, m_o_w_in, m_o_w_pool, m_o_s_pool, m_o_conv_w, m_o_w_out, v_g_mix, v_g_xa, v_g_mem, v_xa_wq, v_xa_wkv, v_xa_wo, v_xa_gq, v_xa_gk, v_g_ffn, v_w_gate, v_w_up, v_w_down, v_e_w_in, v_e_b_f, v_e_g_v, v_e_w_s, v_e_b_s, v_e_g_qn, v_e_g_kn, v_e_w_out, v_o_w_in, v_o_w_pool, v_o_s_pool, v_o_conv_w, v_o_w_out):
    args = dict(locals())
    Wt = {n: args[n] for n in WEIGHTS}
    Mo = {n: args["m_" + n] for n in WEIGHTS}
    Vo = {n: args["v_" + n] for n in WEIGHTS}
    rep_names = REPLICATED
    rep_shapes = [Wt[n].shape for n in rep_names]
    rep_rows = _rows_for(sum(math.prod(s) for s in rep_shapes) + 1, 16)
    layer_names = [LAYER_SPLIT + ("e_w_in", "e_w_out"), LAYER_SPLIT + ("o_w_in", "o_w_out", "o_s_pool", "o_conv_w")]

    def shard_of(n, l, src=Wt):
        return src[n][l:l + 1] if n in LAYER_SPLIT else src[n]

    def to_wire(n, a):
        return a if n in ELEMENTWISE_SHARDED else _mx(a)

    def full_weight(n, l, gn):
        if SHARDED[n] == 1 or n in ELEMENTWISE_SHARDED:
            return _from_blocks(gn, SHARDED[n])[0]
        return _unblock(f"unblock_{n}{l}", gn, 0, EVEN_IN_PAD if n == "e_w_in" else N_DEV * gn.shape[-1])

    def grad_slabs(n, l, gl):
        if n in ELEMENTWISE_SHARDED or SHARDED[n] == 1:
            return _to_blocks(gl.reshape((1,) + Wt[n].shape[1:-1] + (-1,)) if SHARDED[n] == 2 or gl.ndim == 1 else gl[None], SHARDED[n])
        return _block(f"block_{n}{l}", gl, Wt[n].shape[-1])[:, None]

    def layer_params(l, gathered):
        P = {n: Wt[n] for n in rep_names}
        for n, gn in zip(layer_names[l], gathered):
            P[n] = {(l if n in LAYER_SPLIT else 0): full_weight(n, l, gn)}
        return _layer_weights(l, P)

    xk, mem2 = x[0], mem[0]
    W0 = layer_params(0, _all_gather("gather_weights0", [to_wire(n, shard_of(n, 0)) for n in layer_names[0]]))
    xk, sv0 = _layer_fwd(0, xk, mem2, W0, side=_Direct("gather", [to_wire(n, shard_of(n, 1)) for n in layer_names[1]]))
    W1 = layer_params(1, sv0["side"])
    xk, sv1 = _layer_fwd(1, xk, mem2, W1)

    g, loss_row = _loss_grad("loss_grad", xk, loss_target[0])
    g, G1, _ = _layer_bwd(1, g, mem2, W1, sv1)
    side1 = _Direct("scatter", [grad_slabs(n, 1, G1[n]) for n in layer_names[1]])
    g, G0, (parts1, parts0a) = _layer_bwd(0, g, mem2, W0, sv0, side_first=side1,
                                         side_second_fn=lambda G: _Direct("scatter", [grad_slabs(n, 0, G[n]) for n in LAYER_SPLIT]))
    grad_x = g[None]
    parts0b = _all_to_all("exchange_grads_mixer0", [grad_slabs(n, 0, G0[n]) for n in ("e_w_in", "e_w_out")])
    parts = {(n, 1): p for n, p in zip(layer_names[1], parts1)}
    parts.update({(n, 0): p for n, p in zip(LAYER_SPLIT, parts0a)})
    parts.update({(n, 0): p for n, p in zip(("e_w_in", "e_w_out"), parts0b)})
    G = {n: {0: G0[n]} for n in G0}
    for n in G1:
        G.setdefault(n, {})[1] = G1[n]
    rep_g = [jnp.stack([G[n][l] for l in sorted(G[n])]).reshape(Wt[n].shape) for n in rep_names]
    rep_parts = _all_gather("gather_small_grads", [_pack(rep_g + [jnp.sum(loss_row).reshape(1)], rep_rows, F32)])[0]

    outs = {}
    for n in SHARDED:
        per_layer = [_adamw(f"adamw_{n}{l}", parts[(n, l if n in LAYER_SPLIT else (0 if n.startswith("e_") else 1))], Wt[n], Mo[n], Vo[n], layer=l)
                     for l in range(Wt[n].shape[0])]
        for k, kind in enumerate(("grad", "delta", "new_m", "new_v")):
            outs[kind + "_" + n] = jnp.stack([r[k] for r in per_layer])
    res = _adamw("adamw_replicated", rep_parts, _pack([Wt[n] for n in rep_names], rep_rows, F32)[None],
                 _pack([Mo[n] for n in rep_names], rep_rows, F32)[None], _pack([Vo[n] for n in rep_names], rep_rows, F32)[None])
    for kind, slab in zip(("grad", "delta", "new_m", "new_v"), res):
        for n, a in zip(rep_names, _unpack(slab, rep_shapes)):
            outs[kind + "_" + n] = a
    loss = res[0].reshape(-1)[sum(math.prod(s) for s in rep_shapes)]
    return (loss, grad_x, *[outs[k + "_" + n] for k in ("grad", "delta", "new_m", "new_v") for n in WEIGHTS])
```

```python
import functools
import math

import numpy as np
import jax
import jax.numpy as jnp
from jax import lax
from jax.experimental import pallas as pl
from jax.experimental.pallas import tpu as pltpu

F32 = jnp.float32
BF16 = jnp.bfloat16
_MXU = jnp.bfloat16

D_MODEL = 1024
N_DEV = 8
LANES = 128
SUBLANES = 8
HALO = 16
CHUNK = 64
A_GROUPS, A_GROUP_DIM, A_WIDTH, A_BLOCK = 4, 128, 512, 128
FOX_HEADS, FOX_HEAD_DIM, FOX_WIDTH = 8, 64, 512
FOX_TILE = 512
POOL_WINDOWS = (2, 4, 8, 16)
POOL_WIDTH, CONV_WIDTH, CONV_K = 512, 512, 3
EVEN_IN, EVEN_IN_PAD, ODD_IN = 2568, 2688, 2048
XA_HEADS, XA_HEAD_DIM, XA_WIDTH = 4, 128, 512
D_FF = 2816
EPS = 1e-6
ADAM_LR, ADAM_B1, ADAM_B2, ADAM_EPS, ADAM_WD, ADAM_STEP = 0.001, 0.9, 0.999, 1e-08, 0.01, 10
VMEM_LIMIT = 48 * 1024 * 1024


def _cp(sem):
    return pltpu.CompilerParams(dimension_semantics=sem, vmem_limit_bytes=VMEM_LIMIT)


def _tile(n, cap, q):
    best = None
    for d in range(q, min(n, cap) + 1, q):
        if n % d == 0:
            best = d
    return best if best is not None else n


def _mx(a):
    return a.astype(_MXU)


def _dot(a, b):
    return jnp.dot(_mx(a), _mx(b), preferred_element_type=F32)


def _dot_nt(a, b):
    return lax.dot_general(_mx(a), _mx(b), (((1,), (1,)), ((), ())), preferred_element_type=F32)


def _dot_tn(a, b):
    return lax.dot_general(_mx(a), _mx(b), (((0,), (0,)), ((), ())), preferred_element_type=F32)


def _dot_f32(a, b):
    return jnp.dot(a, b, precision=lax.Precision.HIGHEST, preferred_element_type=F32)


def _lane(shape):
    return lax.broadcasted_iota(jnp.int32, shape, len(shape) - 1)


def _row(shape):
    return lax.broadcasted_iota(jnp.int32, shape, len(shape) - 2)


def _mm(name, pairs, mode, out_dtype=F32, residual=None, tm_cap=512, tn_cap=1536, tk_cap=2048):
    a0, b0 = pairs[0]
    if mode == "nn":
        (M, K), N = a0.shape, b0.shape[1]
    elif mode == "nt":
        (M, K), N = a0.shape, b0.shape[0]
    else:
        (K, M), N = a0.shape, b0.shape[1]
    if mode == "tn":
        tm, tk = _tile(M, 1408, LANES), _tile(K, 512, 16)
    else:
        tm, tk = _tile(M, tm_cap, 16), _tile(K, tk_cap, LANES)
    tn = _tile(N, tn_cap, LANES)
    nk = K // tk
    npairs = len(pairs)
    dot = {"nn": _dot, "nt": _dot_nt, "tn": _dot_tn}[mode]

    def body(*refs):
        ab = refs[:2 * npairs]
        res_ref = refs[2 * npairs] if residual is not None else None
        o_ref, acc = refs[-2], refs[-1]
        k = pl.program_id(2)

        @pl.when(k == 0)
        def _():
            acc[...] = jnp.zeros_like(acc)

        for p in range(npairs):
            acc[...] += dot(ab[2 * p][...], ab[2 * p + 1][...])

        @pl.when(k == nk - 1)
        def _():
            out = acc[...]
            if res_ref is not None:
                out = out + res_ref[...]
            o_ref[...] = out.astype(o_ref.dtype)

    if mode == "nn":
        a_spec = pl.BlockSpec((tm, tk), lambda j, i, k: (i, k))
        b_spec = pl.BlockSpec((tk, tn), lambda j, i, k: (k, j))
    elif mode == "nt":
        a_spec = pl.BlockSpec((tm, tk), lambda j, i, k: (i, k))
        b_spec = pl.BlockSpec((tn, tk), lambda j, i, k: (j, k))
    else:
        a_spec = pl.BlockSpec((tk, tm), lambda j, i, k: (k, i))
        b_spec = pl.BlockSpec((tk, tn), lambda j, i, k: (k, j))
    o_spec = pl.BlockSpec((tm, tn), lambda j, i, k: (i, j))
    in_specs, args = [], []
    for a, b in pairs:
        in_specs += [a_spec, b_spec]
        args += [a, b]
    if residual is not None:
        in_specs.append(o_spec)
        args.append(residual)
    return pl.pallas_call(
        body, name=name, grid=(N // tn, M // tm, nk),
        in_specs=in_specs, out_specs=o_spec,
        out_shape=jax.ShapeDtypeStruct((M, N), out_dtype),
        scratch_shapes=[pltpu.VMEM((tm, tn), F32)],
        compiler_params=_cp(("parallel", "parallel", "arbitrary")),
    )(*args)


def _rms_rows(x):
    return lax.rsqrt(jnp.mean(x * x, axis=-1, keepdims=True) + EPS)


def _norm_fwd(name, x, g):
    T, Dm = x.shape
    tm = _tile(T, 512, 16)

    def body(x_ref, g_ref, o_ref):
        xv = x_ref[...]
        o_ref[...] = ((xv * _rms_rows(xv)) * g_ref[...]).astype(o_ref.dtype)

    return pl.pallas_call(
        body, name=name, grid=(T // tm,),
        in_specs=[pl.BlockSpec((tm, Dm), lambda i: (i, 0)), pl.BlockSpec((1, Dm), lambda i: (0, 0))],
        out_specs=pl.BlockSpec((tm, Dm), lambda i: (i, 0)),
        out_shape=jax.ShapeDtypeStruct((T, Dm), _MXU),
        compiler_params=_cp(("parallel",)),
    )(x, g.reshape(1, Dm))


def _norm_bwd(name, dh, x, g, res):
    T, Dm = x.shape
    tm = _tile(T, 512, 8)

    def body(dh_ref, x_ref, g_ref, res_ref, dx_ref, dg_ref):
        xv, dhv = x_ref[...], dh_ref[...]
        r = _rms_rows(xv)
        xhat = xv * r
        dxhat = dhv * g_ref[...]
        dx_ref[...] = res_ref[...] + r * (dxhat - xhat * jnp.mean(dxhat * xhat, axis=-1, keepdims=True))

        @pl.when(pl.program_id(0) == 0)
        def _():
            dg_ref[...] = jnp.zeros_like(dg_ref)

        dg_ref[...] += jnp.sum(dhv * xhat, axis=0, keepdims=True)

    blk = pl.BlockSpec((tm, Dm), lambda i: (i, 0))
    vec = pl.BlockSpec((1, Dm), lambda i: (0, 0))
    return pl.pallas_call(
        body, name=name, grid=(T // tm,),
        in_specs=[blk, blk, vec, blk], out_specs=[blk, vec],
        out_shape=[jax.ShapeDtypeStruct((T, Dm), F32), jax.ShapeDtypeStruct((1, Dm), F32)],
        compiler_params=_cp(("arbitrary",)),
    )(dh, x, g.reshape(1, Dm), res)


def _gelu(x):
    return jax.nn.gelu(x)


def _gelu_grad(x):
    c0, c1 = math.sqrt(2.0 / math.pi), 0.044715
    t = jnp.tanh(c0 * (x + c1 * x * x * x))
    return 0.5 * (1.0 + t) + 0.5 * x * (1.0 - t * t) * c0 * (1.0 + 3.0 * c1 * x * x)


def _head64_rms(x):
    lo = _lane(x.shape) < FOX_HEAD_DIM
    xx = x * x
    sa = jnp.sum(jnp.where(lo, xx, 0.0), axis=-1, keepdims=True)
    sb = jnp.sum(jnp.where(lo, 0.0, xx), axis=-1, keepdims=True)
    inv = 1.0 / FOX_HEAD_DIM
    return jnp.where(lo, lax.rsqrt(sa * inv + EPS), lax.rsqrt(sb * inv + EPS))


def _head64_mean(x):
    lo = _lane(x.shape) < FOX_HEAD_DIM
    sa = jnp.sum(jnp.where(lo, x, 0.0), axis=-1, keepdims=True)
    sb = jnp.sum(jnp.where(lo, 0.0, x), axis=-1, keepdims=True)
    return jnp.where(lo, sa, sb) * (1.0 / FOX_HEAD_DIM)


def _masked_ws(ws_ref, g):
    w = ws_ref[g]
    keep = (_row(w.shape) // CHUNK) >= (_lane(w.shape) // CHUNK)
    return jnp.where(keep, w, 0.0)


def _even_pre(name, z, g_v, w_s, bs_cols, g_qn2, g_kn2, b_f_pad):
    T = z.shape[0]
    tm = _tile(T, 256, A_BLOCK)
    nb = tm // A_BLOCK

    def body(z_ref, gv_ref, ws_ref, bs_ref, gq_ref, gk_ref, bf_ref, ya_ref, q_ref, k_ref, v_ref, c_ref, ct_ref, carry):
        i = pl.program_id(0)

        @pl.when(i == 0)
        def _():
            carry[...] = jnp.zeros_like(carry)

        wm = [_mx(_masked_ws(ws_ref, g)) for g in range(A_GROUPS)]
        for g in range(A_GROUPS):
            sl = slice(g * A_GROUP_DIM, (g + 1) * A_GROUP_DIM)
            u = _gelu(z_ref[:, sl])
            v = _gelu(z_ref[:, A_WIDTH + g * A_GROUP_DIM:A_WIDTH + (g + 1) * A_GROUP_DIM])
            vn = _mx((v * _rms_rows(v)) * gv_ref[:, sl])
            for n in range(nb):
                rows = slice(n * A_BLOCK, (n + 1) * A_BLOCK)
                s = jnp.dot(wm[g], vn[rows], preferred_element_type=F32) + bs_ref[:, g:g + 1]
                ya_ref[rows, sl] = (u[rows] * s).astype(ya_ref.dtype)
        o = 2 * A_WIDTH
        for j in range(FOX_WIDTH // LANES):
            sl = slice(j * LANES, (j + 1) * LANES)
            q = z_ref[:, o + j * LANES:o + (j + 1) * LANES]
            q_ref[:, sl] = (((q * _head64_rms(q)) * gq_ref[...]) * (1.0 / math.sqrt(FOX_HEAD_DIM))).astype(q_ref.dtype)
            k = z_ref[:, o + FOX_WIDTH + j * LANES:o + FOX_WIDTH + (j + 1) * LANES]
            k_ref[:, sl] = ((k * _head64_rms(k)) * gk_ref[...]).astype(k_ref.dtype)
            v_ref[:, sl] = z_ref[:, o + 2 * FOX_WIDTH + j * LANES:o + 2 * FOX_WIDTH + (j + 1) * LANES].astype(v_ref.dtype)
        zf = z_ref[:, o + 3 * FOX_WIDTH:o + 3 * FOX_WIDTH + LANES]
        logf = jnp.where(_lane(zf.shape) < FOX_HEADS, jax.nn.log_sigmoid(zf + bf_ref[...]), 0.0)
        tri = (_row((tm, tm)) >= _lane((tm, tm))).astype(F32)
        c = _dot_f32(tri, logf) + carry[0:1, :]
        c_ref[...] = c
        ct_ref[...] = c.T[0:SUBLANES, :]
        carry[0:1, :] = c[tm - 1:tm, :]

    wide = lambda n: pl.BlockSpec((tm, n), lambda i: (i, 0))
    full = lambda s: pl.BlockSpec(s, lambda i: tuple(0 for _ in s))
    out512 = jax.ShapeDtypeStruct((T, 512), _MXU)
    return pl.pallas_call(
        body, name=name, grid=(T // tm,),
        in_specs=[wide(EVEN_IN_PAD), full((1, A_WIDTH)), full((A_GROUPS, A_BLOCK, A_BLOCK)), full((A_BLOCK, LANES)),
                  full((1, LANES)), full((1, LANES)), full((1, LANES))],
        out_specs=[wide(512), wide(512), wide(512), wide(512), wide(LANES), pl.BlockSpec((SUBLANES, tm), lambda i: (0, i))],
        out_shape=[out512, out512, out512, out512, jax.ShapeDtypeStruct((T, LANES), F32), jax.ShapeDtypeStruct((SUBLANES, T), F32)],
        scratch_shapes=[pltpu.VMEM((SUBLANES, LANES), F32)],
        compiler_params=_cp(("arbitrary",)),
    )(z, g_v, w_s, bs_cols, g_qn2, g_kn2, b_f_pad)


def _tri_steps(n, by_rows):
    if by_rows:
        pairs = [(q, k) for q in range(n) for k in range(q + 1)]
    else:
        pairs = [(q, k) for k in range(n) for q in range(k, n)]
    return (jnp.asarray(np.array([p[0] for p in pairs], np.int32)), jnp.asarray(np.array([p[1] for p in pairs], np.int32)))


def _fox_logits(q, k, c_q0, c_row, diag, tq):
    s = _dot_nt(q, k) + (c_q0 - c_row)
    if diag:
        s = jnp.where(_row((tq, tq)) >= _lane((tq, tq)), s, -jnp.inf)
    return s


def _on_diag_or_not(qi, ki, step):
    @pl.when(qi == ki)
    def _():
        step(True)

    @pl.when(qi != ki)
    def _():
        step(False)


def _side_specs(side):
    return ([pl.BlockSpec(memory_space=pl.ANY)] * side.n, list(side.arrays), list(side.out_shape), list(side.scratch)) if side is not None else ([], [], [], [])


def _side_run(side, s_id, nsteps, side_in, side_out, sems):
    if side is None:
        return

    @pl.when(s_id == 0)
    def _():
        side.start(side_in, side_out, sems)

    @pl.when(s_id == nsteps - 1)
    def _():
        side.finish(side_in, side_out, sems)


def _fox_fwd(name, qn, kn, vv, c, ct, side=None):
    T = qn.shape[0]
    tq = _tile(T, FOX_TILE, LANES)
    nblk = T // tq
    qs, ks = _tri_steps(nblk, True)
    nsteps = int(qs.shape[0])
    npair = FOX_WIDTH // LANES
    s_specs, s_args, s_shapes, s_scratch = _side_specs(side)

    def body(qs_ref, ks_ref, *refs):
        main_in, side_in, main_out, side_out, main_scr, sems = _split_side(refs, 5, 3, 3, side)
        q_ref, k_ref, v_ref, c_ref, ct_ref = main_in
        yb_ref, yf_ref, lse_ref = main_out
        acc, m_s, l_s = main_scr
        s_id = pl.program_id(0)
        qi, ki = qs_ref[s_id], ks_ref[s_id]
        _side_run(side, s_id, nsteps, side_in, side_out, sems)

        @pl.when(ki == 0)
        def _():
            acc[...] = jnp.zeros_like(acc)
            m_s[...] = jnp.full_like(m_s, -jnp.inf)
            l_s[...] = jnp.zeros_like(l_s)

        lo = _lane((tq, LANES)) < FOX_HEAD_DIM

        def step(diag):
            for j in range(npair):
                sl = slice(j * LANES, (j + 1) * LANES)
                q2, k2, v2 = q_ref[:, sl], k_ref[:, sl], v_ref[:, sl]
                outs = []
                for e in range(2):
                    h = 2 * j + e
                    qh = jnp.where(lo if e == 0 else ~lo, q2, jnp.zeros_like(q2))
                    s = _fox_logits(qh, k2, c_ref[0:1, h:h + 1], ct_ref[h:h + 1, :], diag, tq)
                    m_old = m_s[h]
                    m_new = jnp.maximum(m_old, jnp.max(s, axis=-1, keepdims=True))
                    alpha = jnp.exp(m_old - m_new)
                    p = jnp.exp(s - m_new)
                    l_s[h] = alpha * l_s[h] + jnp.sum(p, axis=-1, keepdims=True)
                    m_s[h] = m_new
                    outs.append((alpha, _dot(p, v2)))
                alpha2 = jnp.where(lo, outs[0][0], outs[1][0])
                acc[:, sl] = alpha2 * acc[:, sl] + jnp.where(lo, outs[0][1], outs[1][1])

        _on_diag_or_not(qi, ki, step)

        @pl.when(ki == qi)
        def _():
            lse = jnp.zeros((tq, LANES), F32)
            for j in range(npair):
                sl = slice(j * LANES, (j + 1) * LANES)
                inv = jnp.where(lo, 1.0 / l_s[2 * j], 1.0 / l_s[2 * j + 1])
                y = acc[:, sl] * inv
                yf_ref[:, sl] = y
                yb_ref[:, sl] = y.astype(yb_ref.dtype)
            for h in range(FOX_HEADS):
                lse = jnp.where(_lane((tq, LANES)) == h, m_s[h] + jnp.log(l_s[h]), lse)
            lse_ref[...] = lse

    qmap = lambda s, qs_r, ks_r: (qs_r[s], 0)
    kmap = lambda s, qs_r, ks_r: (ks_r[s], 0)
    grid_spec = pltpu.PrefetchScalarGridSpec(
        num_scalar_prefetch=2, grid=(nsteps,),
        in_specs=[pl.BlockSpec((tq, 512), qmap), pl.BlockSpec((tq, 512), kmap), pl.BlockSpec((tq, 512), kmap),
                  pl.BlockSpec((SUBLANES, LANES), lambda s, qs_r, ks_r: (qs_r[s] * (tq // SUBLANES), 0)),
                  pl.BlockSpec((SUBLANES, tq), lambda s, qs_r, ks_r: (0, ks_r[s]))] + s_specs,
        out_specs=[pl.BlockSpec((tq, 512), qmap), pl.BlockSpec((tq, 512), qmap), pl.BlockSpec((tq, LANES), qmap)] + s_specs,
        scratch_shapes=[pltpu.VMEM((tq, 512), F32), pltpu.VMEM((FOX_HEADS, tq, 1), F32), pltpu.VMEM((FOX_HEADS, tq, 1), F32)] + s_scratch,
    )
    res = pl.pallas_call(
        body, name=name, grid_spec=grid_spec,
        out_shape=[jax.ShapeDtypeStruct((T, 512), _MXU), jax.ShapeDtypeStruct((T, 512), F32), jax.ShapeDtypeStruct((T, LANES), F32)] + s_shapes,
        compiler_params=_cp(("arbitrary",)),
    )(qs, ks, qn, kn, vv, c, ct, *s_args)
    return res[:3], res[3:]


def _shift_down(ext, k, tm):
    return pltpu.roll(ext, k, axis=0)[HALO:HALO + tm]


def _pool_p(z_ext, g, tm, pos):
    w = POOL_WINDOWS[g]
    s, span = z_ext, 1
    while span < w:
        s = s + pltpu.roll(s, span, axis=0)
        span *= 2
    cnt = jnp.minimum(pos + 1, w).astype(F32)
    return s[HALO:HALO + tm] / cnt - z_ext[HALO:HALO + tm]


def _odd_fwd(name, z, w_pool, s_pool, conv_w8):
    T = z.shape[0]
    tm = _tile(T, 256, HALO)
    r = tm // HALO

    def body(z_ref, zb_ref, wp_ref, sp_ref, cw_ref, y_ref):
        i = pl.program_id(0)
        first = i == 0
        pos = i * tm + _row((tm, LANES))
        for g in range(len(POOL_WINDOWS)):
            sl = slice(g * LANES, (g + 1) * LANES)
            halo = jnp.where(first, 0.0, zb_ref[:, sl])
            z_ext = jnp.concatenate([halo, z_ref[:, sl]], axis=0)
            p = _pool_p(z_ext, g, tm, pos)
            y_ref[:, sl] = (_dot(p, wp_ref[g]) * sp_ref[:, sl]).astype(y_ref.dtype)
        for j in range(CONV_WIDTH // LANES):
            sl = slice(j * LANES, (j + 1) * LANES)
            hd = slice(POOL_WIDTH + j * LANES, POOL_WIDTH + (j + 1) * LANES)
            gb = slice(POOL_WIDTH + CONV_WIDTH + j * LANES, POOL_WIDTH + CONV_WIDTH + (j + 1) * LANES)
            gc = slice(POOL_WIDTH + 2 * CONV_WIDTH + j * LANES, POOL_WIDTH + 2 * CONV_WIDTH + (j + 1) * LANES)
            xg_b = jnp.where(first, 0.0, zb_ref[:, gc] * zb_ref[:, hd])
            xg = jnp.concatenate([xg_b, z_ref[:, gc] * z_ref[:, hd]], axis=0)
            conv = (cw_ref[0:1, sl] * _shift_down(xg, 2, tm) + cw_ref[1:2, sl] * _shift_down(xg, 1, tm)
                    + cw_ref[2:3, sl] * xg[HALO:HALO + tm])
            y_ref[:, POOL_WIDTH + j * LANES:POOL_WIDTH + (j + 1) * LANES] = (z_ref[:, gb] * conv).astype(y_ref.dtype)

    full = lambda s: pl.BlockSpec(s, lambda i: tuple(0 for _ in s))
    return pl.pallas_call(
        body, name=name, grid=(T // tm,),
        in_specs=[pl.BlockSpec((tm, ODD_IN), lambda i: (i, 0)),
                  pl.BlockSpec((HALO, ODD_IN), lambda i: (jnp.maximum(i * r - 1, 0), 0)),
                  full((4, LANES, LANES)), full((1, POOL_WIDTH)), full((SUBLANES, CONV_WIDTH))],
        out_specs=pl.BlockSpec((tm, 1024), lambda i: (i, 0)),
        out_shape=jax.ShapeDtypeStruct((T, 1024), _MXU),
        compiler_params=_cp(("parallel",)),
    )(z, z, w_pool, s_pool, conv_w8)


def _mem_fwd(name, mem, g_mem, wkv, gk):
    M = mem.shape[0]

    def body(mem_ref, g_ref, w_ref, gk_ref, m_ref, kv_ref, kn_ref, v_ref):
        x = mem_ref[...]
        m = ((x * _rms_rows(x)) * g_ref[...]).astype(m_ref.dtype)
        m_ref[...] = m
        kv = _dot(m, w_ref[...])
        kv_ref[...] = kv
        for h in range(XA_HEADS):
            sl = slice(h * LANES, (h + 1) * LANES)
            k = kv[:, sl]
            kn_ref[:, sl] = ((k * _rms_rows(k)) * gk_ref[...]).astype(kn_ref.dtype)
        v_ref[...] = kv[:, XA_WIDTH:].astype(v_ref.dtype)

    return pl.pallas_call(
        body, name=name,
        out_shape=[jax.ShapeDtypeStruct((M, D_MODEL), _MXU), jax.ShapeDtypeStruct((M, 2 * XA_WIDTH), F32),
                   jax.ShapeDtypeStruct((M, XA_WIDTH), _MXU), jax.ShapeDtypeStruct((M, XA_WIDTH), _MXU)],
        compiler_params=pltpu.CompilerParams(vmem_limit_bytes=VMEM_LIMIT),
    )(mem, g_mem.reshape(1, D_MODEL), wkv, gk.reshape(1, XA_HEAD_DIM))


def _xa_probs(qx, gq, kn_h):
    r = _rms_rows(qx)
    qhat = qx * r
    qn = qhat * gq
    s = _dot_nt(qn, kn_h) * (1.0 / math.sqrt(XA_HEAD_DIM))
    s = s - jnp.max(s, axis=-1, keepdims=True)
    e = jnp.exp(s)
    return qhat, r, qn, e / jnp.sum(e, axis=-1, keepdims=True)


def _xa_fwd(name, qx, gq, kn, vx):
    T, M = qx.shape[0], kn.shape[0]
    tm = _tile(T, 512, 16)

    def body(q_ref, gq_ref, k_ref, v_ref, o_ref):
        for h in range(XA_HEADS):
            sl = slice(h * LANES, (h + 1) * LANES)
            _, _, _, p = _xa_probs(q_ref[:, sl], gq_ref[...], k_ref[:, sl])
            o_ref[:, sl] = _dot(p, v_ref[:, sl]).astype(o_ref.dtype)

    full = lambda s: pl.BlockSpec(s, lambda i: tuple(0 for _ in s))
    return pl.pallas_call(
        body, name=name, grid=(T // tm,),
        in_specs=[pl.BlockSpec((tm, XA_WIDTH), lambda i: (i, 0)), full((1, LANES)), full((M, XA_WIDTH)), full((M, XA_WIDTH))],
        out_specs=pl.BlockSpec((tm, XA_WIDTH), lambda i: (i, 0)),
        out_shape=jax.ShapeDtypeStruct((T, XA_WIDTH), _MXU),
        compiler_params=_cp(("parallel",)),
    )(qx, gq.reshape(1, XA_HEAD_DIM), kn, vx)


def _ffn_up(name, h, wg, wu):
    T, K = h.shape
    N = wg.shape[1]
    tm, tn = _tile(T, 512, 16), _tile(N, 1536, LANES)

    def body(h_ref, wg_ref, wu_ref, a_ref, b_ref, s_ref):
        hv = h_ref[...]
        a, b = _dot(hv, wg_ref[...]), _dot(hv, wu_ref[...])
        a_ref[...] = a
        b_ref[...] = b
        s_ref[...] = (jax.nn.silu(a) * b).astype(s_ref.dtype)

    w_spec = pl.BlockSpec((K, tn), lambda j, i: (0, j))
    o_spec = pl.BlockSpec((tm, tn), lambda j, i: (i, j))
    return pl.pallas_call(
        body, name=name, grid=(N // tn, T // tm),
        in_specs=[pl.BlockSpec((tm, K), lambda j, i: (i, 0)), w_spec, w_spec],
        out_specs=[o_spec, o_spec, o_spec],
        out_shape=[jax.ShapeDtypeStruct((T, N), F32), jax.ShapeDtypeStruct((T, N), F32), jax.ShapeDtypeStruct((T, N), _MXU)],
        compiler_params=_cp(("parallel", "parallel")),
    )(h, wg, wu)


def _loss_grad(name, y, target):
    T, Dm = y.shape
    tm = _tile(T, 512, 8)

    def body(y_ref, t_ref, dy_ref, l_ref):
        e = y_ref[...] - t_ref[...]
        dy_ref[...] = e * (1.0 / Dm)

        @pl.when(pl.program_id(0) == 0)
        def _():
            l_ref[...] = jnp.zeros_like(l_ref)

        l_ref[...] += jnp.sum(e * e, axis=0, keepdims=True) * (0.5 / Dm)

    blk = pl.BlockSpec((tm, Dm), lambda i: (i, 0))
    return pl.pallas_call(
        body, name=name, grid=(T // tm,),
        in_specs=[blk, blk], out_specs=[blk, pl.BlockSpec((1, Dm), lambda i: (0, 0))],
        out_shape=[jax.ShapeDtypeStruct((T, Dm), F32), jax.ShapeDtypeStruct((1, Dm), F32)],
        compiler_params=_cp(("arbitrary",)),
    )(y, target)


def _pad_lanes(v, n=LANES):
    v = v.reshape(1, -1)
    return jnp.pad(v, ((0, 0), (0, n - v.shape[1])))


def _layer_fwd(l, x, mem, P, side=None, on_side=None):
    sv = {"x0": x}
    W = _layer_weights(l, P, "in")
    h1 = _norm_fwd(f"norm_mix{l}", x, W["g_mix"])
    sv["h1"] = h1
    if l % 2 == 0:
        z = _mm(f"e_in{l}", [(h1, W["w_in"])], "nn")
        ya, qn, kn, vv, c, ct = _even_pre(f"even_pre{l}", z, W["g_v"], W["w_s"], W["bs_cols"], W["g_qn2"], W["g_kn2"], W["b_f_pad"])
        (yb, ybf, lse), side_out = _fox_fwd(f"fox_fwd{l}", qn, kn, vv, c, ct, side=side)
        if on_side is not None:
            on_side(side_out)
        W.update(_layer_weights(l, P, "rest"))
        sv.update(z=z, ya=ya, qn=qn, kn=kn, vv=vv, c=c, ct=ct, yb=yb, ybf=ybf, lse=lse)
        x1 = _mm(f"e_out{l}", [(ya, W["w_out"][:A_WIDTH]), (yb, W["w_out"][A_WIDTH:])], "nn", residual=x)
    else:
        W.update(_layer_weights(l, P, "rest"))
        z = _mm(f"o_in{l}", [(h1, W["w_in"])], "nn")
        ycat = _odd_fwd(f"odd_fwd{l}", z, W["w_pool"], W["s_pool"], W["conv_w8"])
        sv.update(z=z, ycat=ycat)
        x1 = _mm(f"o_out{l}", [(ycat, W["w_out"])], "nn", residual=x)
    sv["x1"] = x1
    h2 = _norm_fwd(f"norm_xa{l}", x1, W["g_xa"])
    qx = _mm(f"xa_q{l}", [(h2, W["wq"])], "nn")
    m, kv, kn_x, vx = _mem_fwd(f"mem_fwd{l}", mem, W["g_mem"], W["wkv"], W["gk"])
    o = _xa_fwd(f"xa_fwd{l}", qx, W["gq"], kn_x, vx)
    x2 = _mm(f"xa_o{l}", [(o, W["wo"])], "nn", residual=x1)
    sv.update(h2=h2, qx=qx, m=m, kv=kv, kn_x=kn_x, vx=vx, o=o, x2=x2)
    h3 = _norm_fwd(f"norm_ffn{l}", x2, W["g_ffn"])
    a, b, s = _ffn_up(f"ffn_up{l}", h3, W["w_gate"], W["w_up"])
    x3 = _mm(f"ffn_down{l}", [(s, W["w_down"])], "nn", residual=x2)
    sv.update(h3=h3, a=a, b=b, s=s)
    return x3, sv, W


def _layer_weights(l, P, part):
    i = l // 2
    if part == "in":
        W = {"g_mix": P["g_mix"][l]}
        if l % 2 == 0:
            w_in = _mx(P["e_w_in"][i])
            if w_in.shape[1] < EVEN_IN_PAD:
                w_in = jnp.pad(w_in, ((0, 0), (0, EVEN_IN_PAD - w_in.shape[1])))
            W.update(w_in=w_in, g_v=P["e_g_v"][i].reshape(1, A_WIDTH), w_s=P["e_w_s"][i],
                     bs_cols=jnp.pad(P["e_b_s"][i].T, ((0, 0), (0, LANES - A_GROUPS))),
                     g_qn2=jnp.tile(P["e_g_qn"][i], 2).reshape(1, LANES), g_kn2=jnp.tile(P["e_g_kn"][i], 2).reshape(1, LANES),
                     b_f_pad=_pad_lanes(P["e_b_f"][i]))
        return W
    W = {k: P[k][l] for k in ("g_xa", "g_mem", "g_ffn")}
    W.update(wq=_mx(P["xa_wq"][l]), wkv=_mx(P["xa_wkv"][l]), wo=_mx(P["xa_wo"][l]), gq=P["xa_gq"][l], gk=P["xa_gk"][l],
             w_gate=_mx(P["w_gate"][l]), w_up=_mx(P["w_up"][l]), w_down=_mx(P["w_down"][l]))
    if l % 2 == 0:
        W.update(w_out=_mx(P["e_w_out"][i]))
    else:
        W.update(w_in=_mx(P["o_w_in"][i]), w_pool=_mx(P["o_w_pool"][i]), s_pool=P["o_s_pool"][i].reshape(1, POOL_WIDTH),
                 conv_w8=jnp.pad(P["o_conv_w"][i], ((0, SUBLANES - CONV_K), (0, 0))), w_out=_mx(P["o_w_out"][i]))
    return W


def _ffn_bwd_act(name, g, wd, a, b):
    T, N = a.shape
    K = g.shape[1]
    tm, tn = _tile(T, 512, 16), _tile(N, 1536, LANES)

    def body(g_ref, wd_ref, a_ref, b_ref, da_ref, db_ref):
        ds = _dot_nt(g_ref[...], wd_ref[...])
        av = a_ref[...]
        sig = jax.nn.sigmoid(av)
        da_ref[...] = (ds * b_ref[...] * (sig * (1.0 + av * (1.0 - sig)))).astype(da_ref.dtype)
        db_ref[...] = (ds * (av * sig)).astype(db_ref.dtype)

    o_spec = pl.BlockSpec((tm, tn), lambda j, i: (i, j))
    return pl.pallas_call(
        body, name=name, grid=(N // tn, T // tm),
        in_specs=[pl.BlockSpec((tm, K), lambda j, i: (i, 0)), pl.BlockSpec((tn, K), lambda j, i: (j, 0)), o_spec, o_spec],
        out_specs=[o_spec, o_spec],
        out_shape=[jax.ShapeDtypeStruct((T, N), _MXU), jax.ShapeDtypeStruct((T, N), _MXU)],
        compiler_params=_cp(("parallel", "parallel")),
    )(g, wd, a, b)


def _xa_bwd(name, qx, do, gq, kn, vx):
    T, M = qx.shape[0], kn.shape[0]
    tm = _tile(T, 512, 16)

    def body(q_ref, do_ref, gq_ref, k_ref, v_ref, dq_ref, dk_ref, dv_ref, dg_ref):
        @pl.when(pl.program_id(0) == 0)
        def _():
            dk_ref[...] = jnp.zeros_like(dk_ref)
            dv_ref[...] = jnp.zeros_like(dv_ref)
            dg_ref[...] = jnp.zeros_like(dg_ref)

        gqv = gq_ref[...]
        for h in range(XA_HEADS):
            sl = slice(h * LANES, (h + 1) * LANES)
            qhat, r, qn, p = _xa_probs(q_ref[:, sl], gqv, k_ref[:, sl])
            do_h = do_ref[:, sl]
            dp = _dot_nt(do_h, v_ref[:, sl])
            dv_ref[:, sl] += _dot_tn(p, do_h)
            dsm = p * (dp - jnp.sum(p * dp, axis=-1, keepdims=True)) * (1.0 / math.sqrt(XA_HEAD_DIM))
            dqn = _dot(dsm, k_ref[:, sl])
            dk_ref[:, sl] += _dot_tn(dsm, qn)
            dg_ref[...] += jnp.sum(dqn * qhat, axis=0, keepdims=True)
            dxh = dqn * gqv
            dq_ref[:, sl] = (r * (dxh - qhat * jnp.mean(dxh * qhat, axis=-1, keepdims=True))).astype(dq_ref.dtype)

    full = lambda s: pl.BlockSpec(s, lambda i: tuple(0 for _ in s))
    blk = pl.BlockSpec((tm, XA_WIDTH), lambda i: (i, 0))
    return pl.pallas_call(
        body, name=name, grid=(T // tm,),
        in_specs=[blk, blk, full((1, LANES)), full((M, XA_WIDTH)), full((M, XA_WIDTH))],
        out_specs=[blk, full((M, XA_WIDTH)), full((M, XA_WIDTH)), full((1, LANES))],
        out_shape=[jax.ShapeDtypeStruct((T, XA_WIDTH), _MXU), jax.ShapeDtypeStruct((M, XA_WIDTH), F32),
                   jax.ShapeDtypeStruct((M, XA_WIDTH), F32), jax.ShapeDtypeStruct((1, LANES), F32)],
        compiler_params=_cp(("arbitrary",)),
    )(qx, do, gq.reshape(1, XA_HEAD_DIM), kn, vx)


def _mem_bwd(name, mem, g_mem, m, kv, gk, dkn, dvx, wkv):
    M = mem.shape[0]

    def body(mem_ref, g_ref, m_ref, kv_ref, gk_ref, dkn_ref, dvx_ref, w_ref, dw_ref, dgk_ref, dgm_ref):
        gkv = gk_ref[...]
        dgk = jnp.zeros((1, LANES), F32)
        parts = []
        for h in range(XA_HEADS):
            sl = slice(h * LANES, (h + 1) * LANES)
            k = kv_ref[:, sl]
            r = _rms_rows(k)
            khat = k * r
            dkn_h = dkn_ref[:, sl]
            dgk = dgk + jnp.sum(dkn_h * khat, axis=0, keepdims=True)
            dxh = dkn_h * gkv
            parts.append(r * (dxh - khat * jnp.mean(dxh * khat, axis=-1, keepdims=True)))
        dgk_ref[...] = dgk
        dkv = jnp.concatenate(parts + [dvx_ref[...]], axis=1)
        dw_ref[...] = _dot_tn(m_ref[...], dkv)
        dm = _dot_nt(dkv, w_ref[...])
        x = mem_ref[...]
        dgm_ref[...] = jnp.sum(dm * (x * _rms_rows(x)), axis=0, keepdims=True)

    return pl.pallas_call(
        body, name=name,
        out_shape=[jax.ShapeDtypeStruct((D_MODEL, 2 * XA_WIDTH), F32), jax.ShapeDtypeStruct((1, LANES), F32),
                   jax.ShapeDtypeStruct((1, D_MODEL), F32)],
        compiler_params=pltpu.CompilerParams(vmem_limit_bytes=VMEM_LIMIT),
    )(mem, g_mem.reshape(1, D_MODEL), m, kv, gk.reshape(1, XA_HEAD_DIM), dkn, dvx, wkv)


def _fox_bwd_common(q2, k2, v2, dy2, yf2, c_ref, ct_ref, lse_ref, e, h, lo, diag, tq):
    sel = lo if e == 0 else ~lo
    qh = jnp.where(sel, q2, jnp.zeros_like(q2))
    kh = jnp.where(sel, k2, jnp.zeros_like(k2))
    dyh = jnp.where(sel, dy2, 0.0)
    s = _fox_logits(qh, k2, c_ref[0:1, h:h + 1], ct_ref[h:h + 1, :], diag, tq)
    p = jnp.exp(s - lse_ref[:, h:h + 1])
    dp = _dot_nt(dyh, v2)
    delta = jnp.sum(dyh * yf2, axis=-1, keepdims=True)
    return p, p * (dp - delta), qh, kh, dyh


def _fox_bwd_dq(name, qn, kn, vv, c, ct, lse, dycat, ybf, side=None):
    T = qn.shape[0]
    tq = _tile(T, FOX_TILE, LANES)
    nblk = T // tq
    qs, ks = _tri_steps(nblk, True)
    nsteps = int(qs.shape[0])
    scale = 1.0 / math.sqrt(FOX_HEAD_DIM)
    s_specs, s_args, s_shapes, s_scratch = _side_specs(side)

    def body(qs_ref, ks_ref, *refs):
        main_in, side_in, main_out, side_out, _, sems = _split_side(refs, 8, 2, 0, side)
        q_ref, k_ref, v_ref, c_ref, ct_ref, lse_ref, dy_ref, yf_ref = main_in
        dq_ref, drow_ref = main_out
        s_id = pl.program_id(0)
        qi, ki = qs_ref[s_id], ks_ref[s_id]
        _side_run(side, s_id, nsteps, side_in, side_out, sems)

        @pl.when(ki == 0)
        def _():
            dq_ref[...] = jnp.zeros_like(dq_ref)
            drow_ref[...] = jnp.zeros_like(drow_ref)

        lo = _lane((tq, LANES)) < FOX_HEAD_DIM

        def step(diag):
            for j in range(FOX_WIDTH // LANES):
                sl = slice(j * LANES, (j + 1) * LANES)
                q2, k2, v2, dy2, yf2 = q_ref[:, sl], k_ref[:, sl], v_ref[:, sl], dy_ref[:, sl], yf_ref[:, sl]
                tot = None
                for e in range(2):
                    h = 2 * j + e
                    _, ds, _, kh, _ = _fox_bwd_common(q2, k2, v2, dy2, yf2, c_ref, ct_ref, lse_ref, e, h, lo, diag, tq)
                    drow_ref[...] += jnp.where(_lane((tq, LANES)) == h, jnp.sum(ds, axis=-1, keepdims=True), 0.0)
                    part = _dot(ds, kh)
                    tot = part if tot is None else tot + part
                dq_ref[:, sl] += tot * scale

        _on_diag_or_not(qi, ki, step)

    qmap = lambda s, a, b: (a[s], 0)
    kmap = lambda s, a, b: (b[s], 0)
    grid_spec = pltpu.PrefetchScalarGridSpec(
        num_scalar_prefetch=2, grid=(int(qs.shape[0]),),
        in_specs=[pl.BlockSpec((tq, 512), qmap), pl.BlockSpec((tq, 512), kmap), pl.BlockSpec((tq, 512), kmap),
                  pl.BlockSpec((SUBLANES, LANES), lambda s, a, b: (a[s] * (tq // SUBLANES), 0)), pl.BlockSpec((SUBLANES, tq), lambda s, a, b: (0, b[s])),
                  pl.BlockSpec((tq, LANES), qmap), pl.BlockSpec((tq, 512), lambda s, a, b: (a[s], 1)), pl.BlockSpec((tq, 512), qmap)] + s_specs,
        out_specs=[pl.BlockSpec((tq, 512), qmap), pl.BlockSpec((tq, LANES), qmap)] + s_specs,
        scratch_shapes=s_scratch,
    )
    res = pl.pallas_call(
        body, name=name, grid_spec=grid_spec,
        out_shape=[jax.ShapeDtypeStruct((T, 512), F32), jax.ShapeDtypeStruct((T, LANES), F32)] + s_shapes,
        compiler_params=_cp(("arbitrary",)),
    )(qs, ks, qn, kn, vv, c, ct, lse, dycat, ybf, *s_args)
    return res[:2], res[2:]


def _fox_bwd_dkv(name, qn, kn, vv, c, ct, lse, dycat, ybf, side=None):
    T = qn.shape[0]
    tq = _tile(T, FOX_TILE, LANES)
    nblk = T // tq
    qs, ks = _tri_steps(nblk, False)
    nsteps = int(qs.shape[0])
    scale = 1.0 / math.sqrt(FOX_HEAD_DIM)
    s_specs, s_args, s_shapes, s_scratch = _side_specs(side)

    def body(qs_ref, ks_ref, *refs):
        main_in, side_in, main_out, side_out, _, sems = _split_side(refs, 8, 3, 0, side)
        q_ref, k_ref, v_ref, c_ref, ct_ref, lse_ref, dy_ref, yf_ref = main_in
        dk_ref, dv_ref, dct_ref = main_out
        s_id = pl.program_id(0)
        qi, ki = qs_ref[s_id], ks_ref[s_id]
        _side_run(side, s_id, nsteps, side_in, side_out, sems)

        @pl.when(qi == ki)
        def _():
            dk_ref[...] = jnp.zeros_like(dk_ref)
            dv_ref[...] = jnp.zeros_like(dv_ref)
            dct_ref[...] = jnp.zeros_like(dct_ref)

        lo = _lane((tq, LANES)) < FOX_HEAD_DIM

        def step(diag):
            for j in range(FOX_WIDTH // LANES):
                sl = slice(j * LANES, (j + 1) * LANES)
                q2, k2, v2, dy2, yf2 = q_ref[:, sl], k_ref[:, sl], v_ref[:, sl], dy_ref[:, sl], yf_ref[:, sl]
                dk_t, dv_t = None, None
                for e in range(2):
                    h = 2 * j + e
                    p, ds, qh, _, dyh = _fox_bwd_common(q2, k2, v2, dy2, yf2, c_ref, ct_ref, lse_ref, e, h, lo, diag, tq)
                    a, b = _dot_tn(ds, qh), _dot_tn(p, dyh)
                    dk_t = a if dk_t is None else dk_t + a
                    dv_t = b if dv_t is None else dv_t + b
                    dct_ref[h:h + 1, :] -= jnp.sum(ds, axis=0, keepdims=True)
                dk_ref[:, sl] += dk_t
                dv_ref[:, sl] += dv_t

        _on_diag_or_not(qi, ki, step)

    qmap = lambda s, a, b: (a[s], 0)
    kmap = lambda s, a, b: (b[s], 0)
    ctmap = lambda s, a, b: (0, b[s])
    grid_spec = pltpu.PrefetchScalarGridSpec(
        num_scalar_prefetch=2, grid=(int(qs.shape[0]),),
        in_specs=[pl.BlockSpec((tq, 512), qmap), pl.BlockSpec((tq, 512), kmap), pl.BlockSpec((tq, 512), kmap),
                  pl.BlockSpec((SUBLANES, LANES), lambda s, a, b: (a[s] * (tq // SUBLANES), 0)), pl.BlockSpec((SUBLANES, tq), ctmap),
                  pl.BlockSpec((tq, LANES), qmap), pl.BlockSpec((tq, 512), lambda s, a, b: (a[s], 1)), pl.BlockSpec((tq, 512), qmap)] + s_specs,
        out_specs=[pl.BlockSpec((tq, 512), kmap), pl.BlockSpec((tq, 512), kmap), pl.BlockSpec((SUBLANES, tq), ctmap)] + s_specs,
        scratch_shapes=s_scratch,
    )
    res = pl.pallas_call(
        body, name=name, grid_spec=grid_spec,
        out_shape=[jax.ShapeDtypeStruct((T, 512), F32), jax.ShapeDtypeStruct((T, 512), F32), jax.ShapeDtypeStruct((SUBLANES, T), F32)] + s_shapes,
        compiler_params=_cp(("arbitrary",)),
    )(qs, ks, qn, kn, vv, c, ct, lse, dycat, ybf, *s_args)
    return res[:3], res[3:]


def _fold64(row):
    return row + pltpu.roll(row, FOX_HEAD_DIM, axis=1)


def _even_bwd(name, z, dycat, dqn, dkn, dvv, dct, drow, g_v, w_s, bs_cols, g_qn2, g_kn2, b_f_pad):
    T = z.shape[0]
    tm = _tile(T, 256, A_BLOCK)
    nb, nt = tm // A_BLOCK, T // tm

    def body(z_ref, dya_ref, dq_ref, dk_ref, dv_ref, dct_ref, drow_ref, gv_ref, ws_ref, bs_ref, gq_ref, gk_ref, bf_ref,
             dz_ref, dws_ref, dbs_ref, dgv_ref, dgq_ref, dgk_ref, dbf_ref, carry):
        @pl.when(pl.program_id(0) == 0)
        def _():
            for r in (dws_ref, dbs_ref, dgv_ref, dgq_ref, dgk_ref, dbf_ref, carry):
                r[...] = jnp.zeros_like(r)

        keep = (_row((A_BLOCK, A_BLOCK)) // CHUNK) >= (_lane((A_BLOCK, A_BLOCK)) // CHUNK)
        for g in range(A_GROUPS):
            sl = slice(g * A_GROUP_DIM, (g + 1) * A_GROUP_DIM)
            wm = _mx(_masked_ws(ws_ref, g))
            zu = z_ref[:, sl]
            zv = z_ref[:, A_WIDTH + g * A_GROUP_DIM:A_WIDTH + (g + 1) * A_GROUP_DIM]
            u, v = _gelu(zu), _gelu(zv)
            r = _rms_rows(v)
            vhat = v * r
            gvv = gv_ref[:, sl]
            vn = _mx(vhat * gvv)
            dya = dya_ref[:, sl]
            du_parts, dvn_parts = [], []
            for n in range(nb):
                rows = slice(n * A_BLOCK, (n + 1) * A_BLOCK)
                s = jnp.dot(wm, vn[rows], preferred_element_type=F32) + bs_ref[:, g:g + 1]
                du_parts.append(dya[rows] * s)
                d_s = dya[rows] * u[rows]
                dbs_ref[...] += jnp.where(_lane((A_BLOCK, LANES)) == g, jnp.sum(d_s, axis=-1, keepdims=True), 0.0)
                dws_ref[g] += jnp.where(keep, _dot_nt(d_s, vn[rows]), 0.0)
                dvn_parts.append(_dot_tn(wm, d_s))
            dvn = jnp.concatenate(dvn_parts, axis=0)
            dgv_ref[:, sl] += jnp.sum(dvn * vhat, axis=0, keepdims=True)
            dxh = dvn * gvv
            dv = r * (dxh - vhat * jnp.mean(dxh * vhat, axis=-1, keepdims=True))
            dz_ref[:, sl] = (jnp.concatenate(du_parts, axis=0) * _gelu_grad(zu)).astype(dz_ref.dtype)
            dz_ref[:, A_WIDTH + g * A_GROUP_DIM:A_WIDTH + (g + 1) * A_GROUP_DIM] = (dv * _gelu_grad(zv)).astype(dz_ref.dtype)
        o = 2 * A_WIDTH
        for j in range(FOX_WIDTH // LANES):
            sl = slice(j * LANES, (j + 1) * LANES)
            for (off, d_ref, g_ref, dg_ref) in ((o, dq_ref, gq_ref, dgq_ref), (o + FOX_WIDTH, dk_ref, gk_ref, dgk_ref)):
                zq = z_ref[:, off + j * LANES:off + (j + 1) * LANES]
                r = _head64_rms(zq)
                qhat = zq * r
                dn = d_ref[:, sl]
                dg_ref[...] += _fold64(jnp.sum(dn * qhat, axis=0, keepdims=True))
                dxh = dn * g_ref[...]
                dz_ref[:, off + j * LANES:off + (j + 1) * LANES] = (r * (dxh - qhat * _head64_mean(dxh * qhat))).astype(dz_ref.dtype)
            dz_ref[:, o + 2 * FOX_WIDTH + j * LANES:o + 2 * FOX_WIDTH + (j + 1) * LANES] = dv_ref[:, sl].astype(dz_ref.dtype)
        dct_v = dct_ref[...] + drow_ref[...].T[0:SUBLANES, :]
        upper = (_row((tm, tm)) >= _lane((tm, tm))).astype(F32)
        dlf_t = _dot_f32(dct_v, upper) + carry[:, 0:1]
        carry[...] += jnp.sum(dct_v, axis=1, keepdims=True)
        dlf = jnp.concatenate([dlf_t, jnp.zeros((LANES - SUBLANES, tm), F32)], axis=0).T
        zf = z_ref[:, o + 3 * FOX_WIDTH:o + 3 * FOX_WIDTH + LANES]
        dzf = dlf * jax.nn.sigmoid(-(zf + bf_ref[...]))
        dbf_ref[...] += jnp.sum(dzf, axis=0, keepdims=True)
        dz_ref[:, o + 3 * FOX_WIDTH:o + 3 * FOX_WIDTH + LANES] = dzf.astype(dz_ref.dtype)

    rev = lambda i: nt - 1 - i
    wide = lambda n, col=0: pl.BlockSpec((tm, n), lambda i: (rev(i), col))
    full = lambda s: pl.BlockSpec(s, lambda i: tuple(0 for _ in s))
    vec = full((1, LANES))
    return pl.pallas_call(
        body, name=name, grid=(nt,),
        in_specs=[wide(EVEN_IN_PAD), wide(512), wide(512), wide(512), wide(512), pl.BlockSpec((SUBLANES, tm), lambda i: (0, rev(i))),
                  wide(LANES), full((1, A_WIDTH)), full((A_GROUPS, A_BLOCK, A_BLOCK)), full((A_BLOCK, LANES)), vec, vec, vec],
        out_specs=[wide(EVEN_IN_PAD), full((A_GROUPS, A_BLOCK, A_BLOCK)), full((A_BLOCK, LANES)), full((1, A_WIDTH)), vec, vec, vec],
        out_shape=[jax.ShapeDtypeStruct((T, EVEN_IN_PAD), _MXU), jax.ShapeDtypeStruct((A_GROUPS, A_BLOCK, A_BLOCK), F32),
                   jax.ShapeDtypeStruct((A_BLOCK, LANES), F32), jax.ShapeDtypeStruct((1, A_WIDTH), F32),
                   jax.ShapeDtypeStruct((1, LANES), F32), jax.ShapeDtypeStruct((1, LANES), F32), jax.ShapeDtypeStruct((1, LANES), F32)],
        scratch_shapes=[pltpu.VMEM((SUBLANES, LANES), F32)],
        compiler_params=_cp(("arbitrary",)),
    )(z, dycat, dqn, dkn, dvv, dct, drow, g_v, w_s, bs_cols, g_qn2, g_kn2, b_f_pad)


def _shift_up(ext, k, tm):
    return pltpu.roll(ext, ext.shape[0] - k, axis=0)[0:tm]


def _odd_bwd(name, z, dycat, w_pool, s_pool, conv_w8):
    T = z.shape[0]
    tm = _tile(T, 256, HALO)
    r, nt = tm // HALO, T // tm
    n_ext = tm + HALO

    def body(z_ref, zb_ref, zn_ref, dy_ref, dyn_ref, wp_ref, sp_ref, cw_ref, dz_ref, dwp_ref, dsp_ref, dcw_ref):
        i = pl.program_id(0)

        @pl.when(i == 0)
        def _():
            for rr in (dwp_ref, dsp_ref, dcw_ref):
                rr[...] = jnp.zeros_like(rr)

        first, last = i == 0, i == nt - 1
        pos = i * tm + _row((tm, LANES))
        pos_ext = i * tm + _row((n_ext, LANES))
        for g, w in enumerate(POOL_WINDOWS):
            sl = slice(g * LANES, (g + 1) * LANES)
            z_ext = jnp.concatenate([jnp.where(first, 0.0, zb_ref[:, sl]), z_ref[:, sl]], axis=0)
            p = _pool_p(z_ext, g, tm, pos)
            spv = sp_ref[:, sl]
            dyc = dy_ref[:, sl]
            dsp_ref[:, sl] += jnp.sum(dyc * _dot(p, wp_ref[g]), axis=0, keepdims=True)
            dpw_ext = jnp.concatenate([dyc, jnp.where(last, 0.0, dyn_ref[:, sl])], axis=0) * spv
            dwp_ref[g] += _dot_tn(p, dpw_ext[0:tm])
            dp_ext = _dot_nt(dpw_ext, wp_ref[g])
            f = dp_ext / jnp.minimum(pos_ext + 1, w).astype(F32)
            span = 1
            while span < w:
                f = f + pltpu.roll(f, n_ext - span, axis=0)
                span *= 2
            dz_ref[:, sl] = (f[0:tm] - dp_ext[0:tm]).astype(dz_ref.dtype)
        for j in range(CONV_WIDTH // LANES):
            sl = slice(j * LANES, (j + 1) * LANES)
            hd = slice(POOL_WIDTH + j * LANES, POOL_WIDTH + (j + 1) * LANES)
            gb = slice(POOL_WIDTH + CONV_WIDTH + j * LANES, POOL_WIDTH + CONV_WIDTH + (j + 1) * LANES)
            gc = slice(POOL_WIDTH + 2 * CONV_WIDTH + j * LANES, POOL_WIDTH + 2 * CONV_WIDTH + (j + 1) * LANES)
            ysl = slice(POOL_WIDTH + j * LANES, POOL_WIDTH + (j + 1) * LANES)
            hdv, gbv, gcv = z_ref[:, hd], z_ref[:, gb], z_ref[:, gc]
            xg = gcv * hdv
            xg_ext = jnp.concatenate([jnp.where(first, 0.0, zb_ref[:, gc] * zb_ref[:, hd]), xg], axis=0)
            xg1, xg2 = _shift_down(xg_ext, 1, tm), _shift_down(xg_ext, 2, tm)
            w0, w1, w2 = cw_ref[0:1, sl], cw_ref[1:2, sl], cw_ref[2:3, sl]
            conv = w0 * xg2 + w1 * xg1 + w2 * xg
            dyd = dy_ref[:, ysl]
            dconv = dyd * gbv
            dconv_ext = jnp.concatenate([dconv, jnp.where(last, 0.0, dyn_ref[:, ysl] * zn_ref[:, gb])], axis=0)
            dcw_ref[0:1, sl] += jnp.sum(dconv * xg2, axis=0, keepdims=True)
            dcw_ref[1:2, sl] += jnp.sum(dconv * xg1, axis=0, keepdims=True)
            dcw_ref[2:3, sl] += jnp.sum(dconv * xg, axis=0, keepdims=True)
            dxg = w2 * dconv + w1 * _shift_up(dconv_ext, 1, tm) + w0 * _shift_up(dconv_ext, 2, tm)
            dz_ref[:, hd] = (dxg * gcv).astype(dz_ref.dtype)
            dz_ref[:, gb] = (dyd * conv).astype(dz_ref.dtype)
            dz_ref[:, gc] = (dxg * hdv).astype(dz_ref.dtype)

    full = lambda s: pl.BlockSpec(s, lambda i: tuple(0 for _ in s))
    back = lambda n: pl.BlockSpec((HALO, n), lambda i: (jnp.maximum(i * r - 1, 0), 0))
    nxt = lambda n: pl.BlockSpec((HALO, n), lambda i: (jnp.minimum((i + 1) * r, T // HALO - 1), 0))
    return pl.pallas_call(
        body, name=name, grid=(nt,),
        in_specs=[pl.BlockSpec((tm, ODD_IN), lambda i: (i, 0)), back(ODD_IN), nxt(ODD_IN),
                  pl.BlockSpec((tm, 1024), lambda i: (i, 0)), nxt(1024),
                  full((4, LANES, LANES)), full((1, POOL_WIDTH)), full((SUBLANES, CONV_WIDTH))],
        out_specs=[pl.BlockSpec((tm, ODD_IN), lambda i: (i, 0)), full((4, LANES, LANES)), full((1, POOL_WIDTH)), full((SUBLANES, CONV_WIDTH))],
        out_shape=[jax.ShapeDtypeStruct((T, ODD_IN), _MXU), jax.ShapeDtypeStruct((4, LANES, LANES), F32),
                   jax.ShapeDtypeStruct((1, POOL_WIDTH), F32), jax.ShapeDtypeStruct((SUBLANES, CONV_WIDTH), F32)],
        compiler_params=_cp(("arbitrary",)),
    )(z, z, z, dycat, dycat, w_pool, s_pool, conv_w8)


def _layer_bwd(l, g3, mem, W, sv, side_first=None, side_second_fn=None):
    i = l // 2
    G = {}
    side_out = ((), ())
    da, db = _ffn_bwd_act(f"ffn_bwd_act{l}", g3, W["w_down"], sv["a"], sv["b"])
    G["w_down"] = _mm(f"d_w_down{l}", [(sv["s"], g3)], "tn")
    dh3 = _mm(f"d_h3{l}", [(da, W["w_gate"]), (db, W["w_up"])], "nt")
    G["w_gate"] = _mm(f"d_w_gate{l}", [(sv["h3"], da)], "tn")
    G["w_up"] = _mm(f"d_w_up{l}", [(sv["h3"], db)], "tn")
    g2, dg = _norm_bwd(f"norm_ffn_bwd{l}", dh3, sv["x2"], W["g_ffn"], g3)
    G["g_ffn"] = dg[0]

    do = _mm(f"d_o{l}", [(g2, W["wo"])], "nt", out_dtype=_MXU)
    G["xa_wo"] = _mm(f"d_wo{l}", [(sv["o"], g2)], "tn")
    dqx, dkn, dvx, dgq = _xa_bwd(f"xa_bwd{l}", sv["qx"], do, W["gq"], sv["kn_x"], sv["vx"])
    G["xa_gq"] = dgq[0]
    dh2 = _mm(f"d_h2{l}", [(dqx, W["wq"])], "nt")
    G["xa_wq"] = _mm(f"d_wq{l}", [(sv["h2"], dqx)], "tn")
    g1, dg = _norm_bwd(f"norm_xa_bwd{l}", dh2, sv["x1"], W["g_xa"], g2)
    G["g_xa"] = dg[0]
    dwkv, dgk, dgm = _mem_bwd(f"mem_bwd{l}", mem, W["g_mem"], sv["m"], sv["kv"], W["gk"], dkn, dvx, W["wkv"])
    G["xa_wkv"], G["xa_gk"], G["g_mem"] = dwkv, dgk[0], dgm[0]

    dycat = _mm(f"d_ycat{l}", [(g1, W["w_out"])], "nt")
    if l % 2 == 0:
        G["e_w_out"] = jnp.concatenate([_mm(f"d_e_wout_a{l}", [(sv["ya"], g1)], "tn"), _mm(f"d_e_wout_b{l}", [(sv["yb"], g1)], "tn")], axis=0)
        fa = (sv["qn"], sv["kn"], sv["vv"], sv["c"], sv["ct"], sv["lse"], dycat, sv["ybf"])
        (dkn_f, dvv, dct), so1 = _fox_bwd_dkv(f"fox_bwd_dkv{l}", *fa, side=side_first)
        (dqn, drow), so2 = _fox_bwd_dq(f"fox_bwd_dq{l}", *fa, side=side_second_fn(G) if side_second_fn is not None else None)
        side_out = (so1, so2)
        dz, dws, dbs, dgv, dgq2, dgk2, dbf = _even_bwd(f"even_bwd{l}", sv["z"], dycat, dqn, dkn_f, dvv, dct, drow, W["g_v"], W["w_s"],
                                                      W["bs_cols"], W["g_qn2"], W["g_kn2"], W["b_f_pad"])
        G.update(e_w_s=dws, e_b_s=dbs[:, :A_GROUPS].T, e_g_v=dgv[0], e_g_qn=dgq2[0, :FOX_HEAD_DIM], e_g_kn=dgk2[0, :FOX_HEAD_DIM],
                 e_b_f=dbf[0, :FOX_HEADS])
        G["e_w_in"] = _mm(f"d_e_win{l}", [(sv["h1"], dz)], "tn")
    else:
        G["o_w_out"] = _mm(f"d_o_wout{l}", [(sv["ycat"], g1)], "tn")
        dz, dwp, dsp, dcw = _odd_bwd(f"odd_bwd{l}", sv["z"], dycat, W["w_pool"], W["s_pool"], W["conv_w8"])
        G.update(o_w_pool=dwp, o_s_pool=dsp[0], o_conv_w=dcw[:CONV_K])
        G["o_w_in"] = _mm(f"d_o_win{l}", [(sv["h1"], dz)], "tn")
    dh1 = _mm(f"d_h1{l}", [(dz, W["w_in"])], "nt")
    g0, dg = _norm_bwd(f"norm_mix_bwd{l}", dh1, sv["x0"], W["g_mix"], g1)
    G["g_mix"] = dg[0]
    return g0, G, side_out


PACK_W = 1024
MESH_T = pl.DeviceIdType.MESH
_ANY = pl.BlockSpec(memory_space=pl.ANY)


def _my_place():
    x, y, c = lax.axis_index("x"), lax.axis_index("y"), lax.axis_index("c")
    return x, y, c


def _flip(v, bit):
    return 1 - v if bit else v


N_PEERS = N_DEV - 1


def _all_gather(name, blks):
    n = len(blks)

    def body(*refs):
        x_refs, out_refs = refs[:n], refs[n:2 * n]
        send_sems, recv_sems, local_sems = refs[2 * n:]
        x, y, c = _my_place()
        me, sibling = (x, y, c), (x, y, 1 - c)
        chips = [(1 - x, y), (x, 1 - y), (1 - x, 1 - y)]

        def copy(t, k, block, to, src=None):
            slab = out_refs[t].at[4 * block[0] + 2 * block[1] + block[2]]
            return pltpu.make_async_remote_copy(
                src_ref=slab if src is None else src, dst_ref=slab,
                send_sem=send_sems.at[N_PEERS * t + k], recv_sem=recv_sems.at[N_PEERS * t + k], device_id=to, device_id_type=MESH_T)

        mine, first, passed = [], [], []
        for t in range(n):
            mine.append(pltpu.make_async_copy(x_refs[t], out_refs[t].at[4 * x + 2 * y + c], local_sems.at[t]))
            first.append(copy(t, 0, me, sibling, src=x_refs[t]))
            first += [copy(t, 1 + j, me, (*chip, c), src=x_refs[t]) for j, chip in enumerate(chips)]
        for cp in mine + first:
            cp.start()
        for j, chip in enumerate(chips):
            for t in range(n):
                copy(t, 1 + j, (*chip, c), me).wait_recv()
                cp = copy(t, 4 + j, (*chip, c), sibling)
                cp.start()
                passed.append(cp)
        for t in range(n):
            copy(t, 0, sibling, me).wait_recv()
            for j, chip in enumerate(chips):
                copy(t, 4 + j, (*chip, 1 - c), me).wait_recv()
        for cp in first + passed:
            cp.wait_send()
        for cp in mine:
            cp.wait()

    return pl.pallas_call(
        body, name=name,
        out_shape=[jax.ShapeDtypeStruct((N_DEV,) + b.shape, b.dtype) for b in blks],
        in_specs=[_ANY] * n, out_specs=[_ANY] * n,
        scratch_shapes=[pltpu.SemaphoreType.DMA((N_PEERS * n,)), pltpu.SemaphoreType.DMA((N_PEERS * n,)), pltpu.SemaphoreType.DMA((n,))],
    )(*blks)


class _Direct:
    def __init__(self, kind, arrays):
        self.kind, self.arrays, self.n = kind, list(arrays), len(arrays)
        if kind == "scatter":
            self.out_shape = [jax.ShapeDtypeStruct(a.shape, a.dtype) for a in arrays]
        else:
            self.out_shape = [jax.ShapeDtypeStruct((N_DEV,) + a.shape, a.dtype) for a in arrays]
        self.scratch = [pltpu.SemaphoreType.DMA((N_PEERS * self.n,)), pltpu.SemaphoreType.DMA((N_PEERS * self.n,)),
                        pltpu.SemaphoreType.DMA((self.n,))]

    def _copies(self, in_refs, out_refs, sems):
        send_sems, recv_sems, local_sems = sems
        x, y, c = _my_place()
        me = 4 * x + 2 * y + c
        mine, sends, recvs = [], [], []
        for t in range(self.n):
            src_of = (lambda idx, t=t: in_refs[t].at[idx]) if self.kind == "scatter" else (lambda idx, t=t: in_refs[t])
            mine.append(pltpu.make_async_copy(src_of(me), out_refs[t].at[me], local_sems.at[t]))
            for m in range(1, N_DEV):
                px, py, pc = _flip(x, m & 4), _flip(y, m & 2), _flip(c, m & 1)
                pidx = 4 * px + 2 * py + pc
                sem = dict(send_sem=send_sems.at[N_PEERS * t + m - 1], recv_sem=recv_sems.at[N_PEERS * t + m - 1], device_id_type=MESH_T)
                sends.append(pltpu.make_async_remote_copy(src_ref=src_of(pidx), dst_ref=out_refs[t].at[me], device_id=(px, py, pc), **sem))
                recvs.append(pltpu.make_async_remote_copy(src_ref=src_of(pidx), dst_ref=out_refs[t].at[pidx], device_id=(x, y, c), **sem))
        return mine, sends, recvs

    def start(self, in_refs, out_refs, sems):
        mine, sends, _ = self._copies(in_refs, out_refs, sems)
        for cp in mine + sends:
            cp.start()

    def finish(self, in_refs, out_refs, sems):
        mine, sends, recvs = self._copies(in_refs, out_refs, sems)
        for cp in recvs:
            cp.wait_recv()
        for cp in sends:
            cp.wait_send()
        for cp in mine:
            cp.wait()


def _split_side(refs, n_main_in, n_main_out, n_main_scratch, side):
    ns = side.n if side is not None else 0
    i = 0
    main_in = refs[i:i + n_main_in]; i += n_main_in
    side_in = refs[i:i + ns]; i += ns
    main_out = refs[i:i + n_main_out]; i += n_main_out
    side_out = refs[i:i + ns]; i += ns
    main_scr = refs[i:i + n_main_scratch]; i += n_main_scratch
    return main_in, side_in, main_out, side_out, main_scr, refs[i:]


def _all_to_all(name, gs):
    side = _Direct("scatter", gs)

    def body(*refs):
        _, side_in, _, side_out, _, sems = _split_side(refs, 0, 0, 0, side)
        side.start(side_in, side_out, sems)
        side.finish(side_in, side_out, sems)

    return pl.pallas_call(
        body, name=name, out_shape=side.out_shape,
        in_specs=[_ANY] * side.n, out_specs=[_ANY] * side.n, scratch_shapes=side.scratch,
    )(*gs)


def _unblock(name, blocks, layer, n_out):
    _, _, K, nb = blocks.shape
    tk = _tile(K, 256, 16)

    def body(b_ref, o_ref):
        for j in range(N_DEV):
            o_ref[:, j * nb:(j + 1) * nb] = b_ref[j]
        if n_out > N_DEV * nb:
            o_ref[:, N_DEV * nb:n_out] = jnp.zeros((tk, n_out - N_DEV * nb), o_ref.dtype)

    return pl.pallas_call(
        body, name=name, grid=(K // tk,),
        in_specs=[pl.BlockSpec((N_DEV, None, tk, nb), lambda i: (0, layer, i, 0))],
        out_specs=pl.BlockSpec((tk, n_out), lambda i: (i, 0)),
        out_shape=jax.ShapeDtypeStruct((K, n_out), blocks.dtype),
        compiler_params=_cp(("parallel",)),
    )(blocks)


def _block(name, full, nb):
    K, N = full.shape
    tk = _tile(K, 256, 8)

    def body(f_ref, o_ref):
        for j in range(N_DEV):
            o_ref[j] = f_ref[:, j * nb:(j + 1) * nb]

    return pl.pallas_call(
        body, name=name, grid=(K // tk,),
        in_specs=[pl.BlockSpec((tk, N), lambda i: (i, 0))],
        out_specs=pl.BlockSpec((N_DEV, tk, nb), lambda i: (0, i, 0)),
        out_shape=jax.ShapeDtypeStruct((N_DEV, K, nb), full.dtype),
        compiler_params=_cp(("parallel",)),
    )(full)


ADAM_BLOCK_ELEMS = 256 * 1024


def _adamw(name, parts, w, m, v, layer=0):
    shape = w.shape[1:]
    cols = shape[-1]
    rows = math.prod(shape[:-1])
    nl = w.shape[0]
    parts, w, m, v = parts.reshape(N_DEV, rows, cols), w.reshape(nl, rows, cols), m.reshape(nl, rows, cols), v.reshape(nl, rows, cols)
    tr = _tile(rows, max(SUBLANES, ADAM_BLOCK_ELEMS // cols), SUBLANES)

    def body(p_ref, w_ref, m_ref, v_ref, g_ref, d_ref, mo_ref, vo_ref):
        g = p_ref[0]
        for s in range(1, N_DEV):
            g = g + p_ref[s]
        mn = ADAM_B1 * m_ref[...] + (1.0 - ADAM_B1) * g
        vn = ADAM_B2 * v_ref[...] + (1.0 - ADAM_B2) * jnp.square(g)
        m_hat = mn / (1.0 - ADAM_B1 ** ADAM_STEP)
        v_hat = vn / (1.0 - ADAM_B2 ** ADAM_STEP)
        g_ref[...] = g
        d_ref[...] = -ADAM_LR * (m_hat / (jnp.sqrt(v_hat) + ADAM_EPS) + ADAM_WD * w_ref[...])
        mo_ref[...] = mn
        vo_ref[...] = vn

    blk = pl.BlockSpec((tr, cols), lambda i: (i, 0))
    lblk = pl.BlockSpec((None, tr, cols), lambda i: (layer, i, 0))
    shp = jax.ShapeDtypeStruct((rows, cols), F32)
    res = pl.pallas_call(
        body, name=name, grid=(rows // tr,),
        in_specs=[pl.BlockSpec((N_DEV, tr, cols), lambda i: (0, i, 0)), lblk, lblk, lblk],
        out_specs=[blk, blk, blk, blk], out_shape=[shp, shp, shp, shp],
        compiler_params=_cp(("parallel",)),
    )(parts, w, m, v)
    return [r.reshape(shape) for r in res]


SHARDED = dict(xa_wq=1, xa_wkv=1, xa_wo=2, w_gate=2, w_up=2, w_down=1, e_w_in=2, e_w_out=1, o_w_in=2, o_s_pool=1, o_conv_w=2, o_w_out=1)
LAYER_SPLIT = ("xa_wq", "xa_wkv", "xa_wo", "w_gate", "w_up", "w_down")
ELEMENTWISE_SHARDED = ("o_s_pool", "o_conv_w")
REPLICATED = ("g_mix", "g_xa", "g_mem", "xa_gq", "xa_gk", "g_ffn", "e_b_f", "e_g_v", "e_w_s", "e_b_s", "e_g_qn", "e_g_kn", "o_w_pool")
WEIGHTS = ("g_mix", "g_xa", "g_mem", "xa_wq", "xa_wkv", "xa_wo", "xa_gq", "xa_gk", "g_ffn", "w_gate", "w_up", "w_down", "e_w_in", "e_b_f",
           "e_g_v", "e_w_s", "e_b_s", "e_g_qn", "e_g_kn", "e_w_out", "o_w_in", "o_w_pool", "o_s_pool", "o_conv_w", "o_w_out")


def _rows_for(n, mult):
    return -(-n // (PACK_W * mult)) * mult


def _pack(arrs, rows, dtype):
    flat = jnp.concatenate([a.reshape(-1).astype(dtype) for a in arrs])
    return jnp.pad(flat, (0, rows * PACK_W - flat.shape[0])).reshape(rows, PACK_W)


def _unpack(slab, shapes):
    flat, out, off = slab.reshape(-1), [], 0
    for s in shapes:
        n = math.prod(s)
        out.append(flat[off:off + n].reshape(s))
        off += n
    return out


def _to_blocks(full, axis):
    s = full.shape
    return jnp.moveaxis(full.reshape(s[:axis] + (N_DEV, s[axis] // N_DEV) + s[axis + 1:]), axis, 0)


def _from_blocks(blocks, axis):
    b = jnp.moveaxis(blocks, 0, axis)
    s = b.shape
    return b.reshape(s[:axis] + (s[axis] * s[axis + 1],) + s[axis + 2:])


def kernel(x, mem, g_mix, g_xa, g_mem, xa_wq, xa_wkv, xa_wo, xa_gq, xa_gk, g_ffn, w_gate, w_up, w_down, e_w_in, e_b_f, e_g_v, e_w_s, e_b_s, e_g_qn, e_g_kn, e_w_out, o_w_in, o_w_pool, o_s_pool, o_conv_w, o_w_out, loss_target, m_g_mix, m_g_xa, m_g_mem, m_xa_wq, m_xa_wkv, m_xa_wo, m_xa_gq, m_xa_gk, m_g_ffn, m_w_gate, m_w_up, m_w_down, m_e_w_in, m_e_b_f, m_e_g_v, m_e_w_s, m_e_b_s, m_e_g_qn, m_e_g_kn, m_e_w_out, m_o_w_in, m_o_w_pool, m_o_s_pool, m_o_conv_w, m_o_w_out, v_g_mix, v_g_xa, v_g_mem, v_xa_wq, v_xa_wkv, v_xa_wo, v_xa_gq, v_xa_gk, v_g_ffn, v_w_gate, v_w_up, v_w_down, v_e_w_in, v_e_b_f, v_e_g_v, v_e_w_s, v_e_b_s, v_e_g_qn, v_e_g_kn, v_e_w_out, v_o_w_in, v_o_w_pool, v_o_s_pool, v_o_conv_w, v_o_w_out):
    args = dict(locals())
    Wt = {n: args[n] for n in WEIGHTS}
    Mo = {n: args["m_" + n] for n in WEIGHTS}
    Vo = {n: args["v_" + n] for n in WEIGHTS}
    rep_names = REPLICATED
    rep_shapes = [Wt[n].shape for n in rep_names]
    rep_rows = _rows_for(sum(math.prod(s) for s in rep_shapes) + 1, 16)
    layer_names = [LAYER_SPLIT + ("e_w_in", "e_w_out"), LAYER_SPLIT + ("o_w_in", "o_w_out", "o_s_pool", "o_conv_w")]

    def shard_of(n, l, src=Wt):
        return src[n][l:l + 1] if n in LAYER_SPLIT else src[n]

    def to_wire(n, a):
        return a if n in ELEMENTWISE_SHARDED else _mx(a)

    def full_weight(n, l, gn):
        if SHARDED[n] == 1 or n in ELEMENTWISE_SHARDED:
            return _from_blocks(gn, SHARDED[n])[0]
        return _unblock(f"unblock_{n}{l}", gn, 0, EVEN_IN_PAD if n == "e_w_in" else N_DEV * gn.shape[-1])

    def grad_slabs(n, l, gl):
        if n in ELEMENTWISE_SHARDED or SHARDED[n] == 1:
            return _to_blocks(gl.reshape((1,) + Wt[n].shape[1:-1] + (-1,)) if SHARDED[n] == 2 or gl.ndim == 1 else gl[None], SHARDED[n])
        return _block(f"block_{n}{l}", gl, Wt[n].shape[-1])[:, None]

    xk, mem2 = x[0], mem[0]
    later = [(n, 0) for n in layer_names[0] if n != "e_w_in"] + [(n, 1) for n in layer_names[1]]
    w_in0 = full_weight("e_w_in", 0, _all_gather("gather_e_w_in", [to_wire("e_w_in", Wt["e_w_in"])])[0])
    side = _Direct("gather", [to_wire(n, shard_of(n, l)) for n, l in later])
    P = {n: Wt[n] for n in rep_names}
    P["e_w_in"] = {0: w_in0}

    def on_side(gathered):
        for (n, l), gn in zip(later, gathered):
            P.setdefault(n, {})[l if n in LAYER_SPLIT else 0] = full_weight(n, l, gn)

    xk, sv0, W0 = _layer_fwd(0, xk, mem2, P, side=side, on_side=on_side)
    xk, sv1, W1 = _layer_fwd(1, xk, mem2, P)

    g, loss_row = _loss_grad("loss_grad", xk, loss_target[0])
    g, G1, _ = _layer_bwd(1, g, mem2, W1, sv1)
    side1 = _Direct("scatter", [grad_slabs(n, 1, G1[n]) for n in layer_names[1]])
    g, G0, (parts1, parts0a) = _layer_bwd(0, g, mem2, W0, sv0, side_first=side1,
                                         side_second_fn=lambda G: _Direct("scatter", [grad_slabs(n, 0, G[n]) for n in LAYER_SPLIT]))
    grad_x = g[None]
    parts0b = _all_to_all("exchange_grads_mixer0", [grad_slabs(n, 0, G0[n]) for n in ("e_w_in", "e_w_out")])
    parts = {(n, 1): p for n, p in zip(layer_names[1], parts1)}
    parts.update({(n, 0): p for n, p in zip(LAYER_SPLIT, parts0a)})
    parts.update({(n, 0): p for n, p in zip(("e_w_in", "e_w_out"), parts0b)})
    G = {n: {0: G0[n]} for n in G0}
    for n in G1:
        G.setdefault(n, {})[1] = G1[n]
    rep_g = [jnp.stack([G[n][l] for l in sorted(G[n])]).reshape(Wt[n].shape) for n in rep_names]
    rep_parts = _all_gather("gather_small_grads", [_pack(rep_g + [jnp.sum(loss_row).reshape(1)], rep_rows, F32)])[0]

    outs = {}
    for n in SHARDED:
        per_layer = [_adamw(f"adamw_{n}{l}", parts[(n, l if n in LAYER_SPLIT else (0 if n.startswith("e_") else 1))], Wt[n], Mo[n], Vo[n], layer=l)
                     for l in range(Wt[n].shape[0])]
        for k, kind in enumerate(("grad", "delta", "new_m", "new_v")):
            outs[kind + "_" + n] = jnp.stack([r[k] for r in per_layer])
    res = _adamw("adamw_replicated", rep_parts, _pack([Wt[n] for n in rep_names], rep_rows, F32)[None],
                 _pack([Mo[n] for n in rep_names], rep_rows, F32)[None], _pack([Vo[n] for n in rep_names], rep_rows, F32)[None])
    for kind, slab in zip(("grad", "delta", "new_m", "new_v"), res):
        for n, a in zip(rep_names, _unpack(slab, rep_shapes)):
            outs[kind + "_" + n] = a
    loss = res[0].reshape(-1)[sum(math.prod(s) for s in rep_shapes)]
    return (loss, grad_x, *[outs[k + "_" + n] for k in ("grad", "delta", "new_m", "new_v") for n in WEIGHTS])
```

```python
import functools
import math

import numpy as np
import jax
import jax.numpy as jnp
from jax import lax
from jax.experimental import pallas as pl
from jax.experimental.pallas import tpu as pltpu

F32 = jnp.float32
BF16 = jnp.bfloat16
_MXU = jnp.bfloat16

D_MODEL = 1024
N_DEV = 8
LANES = 128
SUBLANES = 8
HALO = 16
CHUNK = 64
A_GROUPS, A_GROUP_DIM, A_WIDTH, A_BLOCK = 4, 128, 512, 128
FOX_HEADS, FOX_HEAD_DIM, FOX_WIDTH = 8, 64, 512
FOX_TILE = 512
POOL_WINDOWS = (2, 4, 8, 16)
POOL_WIDTH, CONV_WIDTH, CONV_K = 512, 512, 3
EVEN_IN, EVEN_IN_PAD, ODD_IN = 2568, 2688, 2048
XA_HEADS, XA_HEAD_DIM, XA_WIDTH = 4, 128, 512
D_FF = 2816
EPS = 1e-6
ADAM_LR, ADAM_B1, ADAM_B2, ADAM_EPS, ADAM_WD, ADAM_STEP = 0.001, 0.9, 0.999, 1e-08, 0.01, 10
VMEM_LIMIT = 48 * 1024 * 1024


def _cp(sem):
    return pltpu.CompilerParams(dimension_semantics=sem, vmem_limit_bytes=VMEM_LIMIT)


def _tile(n, cap, q):
    best = None
    for d in range(q, min(n, cap) + 1, q):
        if n % d == 0:
            best = d
    return best if best is not None else n


def _mx(a):
    return a.astype(_MXU)


def _dot(a, b):
    return jnp.dot(_mx(a), _mx(b), preferred_element_type=F32)


def _dot_nt(a, b):
    return lax.dot_general(_mx(a), _mx(b), (((1,), (1,)), ((), ())), preferred_element_type=F32)


def _dot_tn(a, b):
    return lax.dot_general(_mx(a), _mx(b), (((0,), (0,)), ((), ())), preferred_element_type=F32)


def _dot_f32(a, b):
    return jnp.dot(a, b, precision=lax.Precision.HIGHEST, preferred_element_type=F32)


def _lane(shape):
    return lax.broadcasted_iota(jnp.int32, shape, len(shape) - 1)


def _row(shape):
    return lax.broadcasted_iota(jnp.int32, shape, len(shape) - 2)


def _mm(name, pairs, mode, out_dtype=F32, residual=None, tm_cap=512, tn_cap=1536, tk_cap=2048):
    a0, b0 = pairs[0]
    if mode == "nn":
        (M, K), N = a0.shape, b0.shape[1]
    elif mode == "nt":
        (M, K), N = a0.shape, b0.shape[0]
    else:
        (K, M), N = a0.shape, b0.shape[1]
    if mode == "tn":
        tm, tk = _tile(M, 1408, LANES), _tile(K, 512, 16)
    else:
        tm, tk = _tile(M, tm_cap, 16), _tile(K, tk_cap, LANES)
    tn = _tile(N, tn_cap, LANES)
    nk = K // tk
    npairs = len(pairs)
    dot = {"nn": _dot, "nt": _dot_nt, "tn": _dot_tn}[mode]

    def body(*refs):
        ab = refs[:2 * npairs]
        res_ref = refs[2 * npairs] if residual is not None else None
        o_ref, acc = refs[-2], refs[-1]
        k = pl.program_id(2)

        @pl.when(k == 0)
        def _():
            acc[...] = jnp.zeros_like(acc)

        for p in range(npairs):
            acc[...] += dot(ab[2 * p][...], ab[2 * p + 1][...])

        @pl.when(k == nk - 1)
        def _():
            out = acc[...]
            if res_ref is not None:
                out = out + res_ref[...]
            o_ref[...] = out.astype(o_ref.dtype)

    if mode == "nn":
        a_spec = pl.BlockSpec((tm, tk), lambda j, i, k: (i, k))
        b_spec = pl.BlockSpec((tk, tn), lambda j, i, k: (k, j))
    elif mode == "nt":
        a_spec = pl.BlockSpec((tm, tk), lambda j, i, k: (i, k))
        b_spec = pl.BlockSpec((tn, tk), lambda j, i, k: (j, k))
    else:
        a_spec = pl.BlockSpec((tk, tm), lambda j, i, k: (k, i))
        b_spec = pl.BlockSpec((tk, tn), lambda j, i, k: (k, j))
    o_spec = pl.BlockSpec((tm, tn), lambda j, i, k: (i, j))
    in_specs, args = [], []
    for a, b in pairs:
        in_specs += [a_spec, b_spec]
        args += [a, b]
    if residual is not None:
        in_specs.append(o_spec)
        args.append(residual)
    return pl.pallas_call(
        body, name=name, grid=(N // tn, M // tm, nk),
        in_specs=in_specs, out_specs=o_spec,
        out_shape=jax.ShapeDtypeStruct((M, N), out_dtype),
        scratch_shapes=[pltpu.VMEM((tm, tn), F32)],
        compiler_params=_cp(("parallel", "parallel", "arbitrary")),
    )(*args)


def _rms_rows(x):
    return lax.rsqrt(jnp.mean(x * x, axis=-1, keepdims=True) + EPS)


def _norm_fwd(name, x, g):
    T, Dm = x.shape
    tm = _tile(T, 512, 16)

    def body(x_ref, g_ref, o_ref):
        xv = x_ref[...]
        o_ref[...] = ((xv * _rms_rows(xv)) * g_ref[...]).astype(o_ref.dtype)

    return pl.pallas_call(
        body, name=name, grid=(T // tm,),
        in_specs=[pl.BlockSpec((tm, Dm), lambda i: (i, 0)), pl.BlockSpec((1, Dm), lambda i: (0, 0))],
        out_specs=pl.BlockSpec((tm, Dm), lambda i: (i, 0)),
        out_shape=jax.ShapeDtypeStruct((T, Dm), _MXU),
        compiler_params=_cp(("parallel",)),
    )(x, g.reshape(1, Dm))


def _norm_bwd(name, dh, x, g, res):
    T, Dm = x.shape
    tm = _tile(T, 512, 8)

    def body(dh_ref, x_ref, g_ref, res_ref, dx_ref, dg_ref):
        xv, dhv = x_ref[...], dh_ref[...]
        r = _rms_rows(xv)
        xhat = xv * r
        dxhat = dhv * g_ref[...]
        dx_ref[...] = res_ref[...] + r * (dxhat - xhat * jnp.mean(dxhat * xhat, axis=-1, keepdims=True))

        @pl.when(pl.program_id(0) == 0)
        def _():
            dg_ref[...] = jnp.zeros_like(dg_ref)

        dg_ref[...] += jnp.sum(dhv * xhat, axis=0, keepdims=True)

    blk = pl.BlockSpec((tm, Dm), lambda i: (i, 0))
    vec = pl.BlockSpec((1, Dm), lambda i: (0, 0))
    return pl.pallas_call(
        body, name=name, grid=(T // tm,),
        in_specs=[blk, blk, vec, blk], out_specs=[blk, vec],
        out_shape=[jax.ShapeDtypeStruct((T, Dm), F32), jax.ShapeDtypeStruct((1, Dm), F32)],
        compiler_params=_cp(("arbitrary",)),
    )(dh, x, g.reshape(1, Dm), res)


def _gelu(x):
    return jax.nn.gelu(x)


def _gelu_grad(x):
    c0, c1 = math.sqrt(2.0 / math.pi), 0.044715
    t = jnp.tanh(c0 * (x + c1 * x * x * x))
    return 0.5 * (1.0 + t) + 0.5 * x * (1.0 - t * t) * c0 * (1.0 + 3.0 * c1 * x * x)


def _head64_rms(x):
    lo = _lane(x.shape) < FOX_HEAD_DIM
    xx = x * x
    sa = jnp.sum(jnp.where(lo, xx, 0.0), axis=-1, keepdims=True)
    sb = jnp.sum(jnp.where(lo, 0.0, xx), axis=-1, keepdims=True)
    inv = 1.0 / FOX_HEAD_DIM
    return jnp.where(lo, lax.rsqrt(sa * inv + EPS), lax.rsqrt(sb * inv + EPS))


def _head64_mean(x):
    lo = _lane(x.shape) < FOX_HEAD_DIM
    sa = jnp.sum(jnp.where(lo, x, 0.0), axis=-1, keepdims=True)
    sb = jnp.sum(jnp.where(lo, 0.0, x), axis=-1, keepdims=True)
    return jnp.where(lo, sa, sb) * (1.0 / FOX_HEAD_DIM)


def _masked_ws(ws_ref, g):
    w = ws_ref[g]
    keep = (_row(w.shape) // CHUNK) >= (_lane(w.shape) // CHUNK)
    return jnp.where(keep, w, 0.0)


def _even_pre(name, z, g_v, w_s, bs_cols, g_qn2, g_kn2, b_f_pad):
    T = z.shape[0]
    tm = _tile(T, 256, A_BLOCK)
    nb = tm // A_BLOCK

    def body(z_ref, gv_ref, ws_ref, bs_ref, gq_ref, gk_ref, bf_ref, ya_ref, q_ref, k_ref, v_ref, c_ref, ct_ref, carry):
        i = pl.program_id(0)

        @pl.when(i == 0)
        def _():
            carry[...] = jnp.zeros_like(carry)

        wm = [_mx(_masked_ws(ws_ref, g)) for g in range(A_GROUPS)]
        for g in range(A_GROUPS):
            sl = slice(g * A_GROUP_DIM, (g + 1) * A_GROUP_DIM)
            u = _gelu(z_ref[:, sl])
            v = _gelu(z_ref[:, A_WIDTH + g * A_GROUP_DIM:A_WIDTH + (g + 1) * A_GROUP_DIM])
            vn = _mx((v * _rms_rows(v)) * gv_ref[:, sl])
            for n in range(nb):
                rows = slice(n * A_BLOCK, (n + 1) * A_BLOCK)
                s = jnp.dot(wm[g], vn[rows], preferred_element_type=F32) + bs_ref[:, g:g + 1]
                ya_ref[rows, sl] = (u[rows] * s).astype(ya_ref.dtype)
        o = 2 * A_WIDTH
        for j in range(FOX_WIDTH // LANES):
            sl = slice(j * LANES, (j + 1) * LANES)
            q = z_ref[:, o + j * LANES:o + (j + 1) * LANES]
            q_ref[:, sl] = (((q * _head64_rms(q)) * gq_ref[...]) * (1.0 / math.sqrt(FOX_HEAD_DIM))).astype(q_ref.dtype)
            k = z_ref[:, o + FOX_WIDTH + j * LANES:o + FOX_WIDTH + (j + 1) * LANES]
            k_ref[:, sl] = ((k * _head64_rms(k)) * gk_ref[...]).astype(k_ref.dtype)
            v_ref[:, sl] = z_ref[:, o + 2 * FOX_WIDTH + j * LANES:o + 2 * FOX_WIDTH + (j + 1) * LANES].astype(v_ref.dtype)
        zf = z_ref[:, o + 3 * FOX_WIDTH:o + 3 * FOX_WIDTH + LANES]
        logf = jnp.where(_lane(zf.shape) < FOX_HEADS, jax.nn.log_sigmoid(zf + bf_ref[...]), 0.0)
        tri = (_row((tm, tm)) >= _lane((tm, tm))).astype(F32)
        c = _dot_f32(tri, logf) + carry[0:1, :]
        c_ref[...] = c
        ct_ref[...] = c.T[0:SUBLANES, :]
        carry[0:1, :] = c[tm - 1:tm, :]

    wide = lambda n: pl.BlockSpec((tm, n), lambda i: (i, 0))
    full = lambda s: pl.BlockSpec(s, lambda i: tuple(0 for _ in s))
    out512 = jax.ShapeDtypeStruct((T, 512), _MXU)
    return pl.pallas_call(
        body, name=name, grid=(T // tm,),
        in_specs=[wide(EVEN_IN_PAD), full((1, A_WIDTH)), full((A_GROUPS, A_BLOCK, A_BLOCK)), full((A_BLOCK, LANES)),
                  full((1, LANES)), full((1, LANES)), full((1, LANES))],
        out_specs=[wide(512), wide(512), wide(512), wide(512), wide(LANES), pl.BlockSpec((SUBLANES, tm), lambda i: (0, i))],
        out_shape=[out512, out512, out512, out512, jax.ShapeDtypeStruct((T, LANES), F32), jax.ShapeDtypeStruct((SUBLANES, T), F32)],
        scratch_shapes=[pltpu.VMEM((SUBLANES, LANES), F32)],
        compiler_params=_cp(("arbitrary",)),
    )(z, g_v, w_s, bs_cols, g_qn2, g_kn2, b_f_pad)


def _tri_steps(n, by_rows):
    if by_rows:
        pairs = [(q, k) for q in range(n) for k in range(q + 1)]
    else:
        pairs = [(q, k) for k in range(n) for q in range(k, n)]
    return (jnp.asarray(np.array([p[0] for p in pairs], np.int32)), jnp.asarray(np.array([p[1] for p in pairs], np.int32)))


def _fox_logits(q, k, c_q0, c_row, diag, tq):
    s = _dot_nt(q, k) + (c_q0 - c_row)
    if diag:
        s = jnp.where(_row((tq, tq)) >= _lane((tq, tq)), s, -jnp.inf)
    return s


def _on_diag_or_not(qi, ki, step):
    @pl.when(qi == ki)
    def _():
        step(True)

    @pl.when(qi != ki)
    def _():
        step(False)


def _side_specs(side):
    if side is None:
        return [], [], [], []
    return [pl.BlockSpec(memory_space=pl.ANY)] * side.n, list(side.arrays), list(side.out_shape), list(side.scratch)


def _side_run(side, s_id, nsteps, side_in, side_out, sems):
    if side is None:
        return

    @pl.when(s_id == 0)
    def _():
        side.start(side_in, side_out, sems)

    @pl.when(s_id == nsteps // 2)
    def _():
        side.middle(side_in, side_out, sems)

    @pl.when(s_id == nsteps - 1)
    def _():
        side.finish(side_in, side_out, sems)


def _fox_fwd(name, qn, kn, vv, c, ct, side=None):
    T = qn.shape[0]
    tq = _tile(T, FOX_TILE, LANES)
    nblk = T // tq
    qs, ks = _tri_steps(nblk, True)
    nsteps = int(qs.shape[0])
    npair = FOX_WIDTH // LANES
    s_specs, s_args, s_shapes, s_scratch = _side_specs(side)

    def body(qs_ref, ks_ref, *refs):
        main_in, side_in, main_out, side_out, main_scr, sems = _split_side(refs, 5, 3, 3, side)
        q_ref, k_ref, v_ref, c_ref, ct_ref = main_in
        yb_ref, yf_ref, lse_ref = main_out
        acc, m_s, l_s = main_scr
        s_id = pl.program_id(0)
        qi, ki = qs_ref[s_id], ks_ref[s_id]
        _side_run(side, s_id, nsteps, side_in, side_out, sems)

        @pl.when(ki == 0)
        def _():
            acc[...] = jnp.zeros_like(acc)
            m_s[...] = jnp.full_like(m_s, -jnp.inf)
            l_s[...] = jnp.zeros_like(l_s)

        lo = _lane((tq, LANES)) < FOX_HEAD_DIM

        def step(diag):
            for j in range(npair):
                sl = slice(j * LANES, (j + 1) * LANES)
                q2, k2, v2 = q_ref[:, sl], k_ref[:, sl], v_ref[:, sl]
                outs = []
                for e in range(2):
                    h = 2 * j + e
                    qh = jnp.where(lo if e == 0 else ~lo, q2, jnp.zeros_like(q2))
                    s = _fox_logits(qh, k2, c_ref[0:1, h:h + 1], ct_ref[h:h + 1, :], diag, tq)
                    m_old = m_s[h]
                    m_new = jnp.maximum(m_old, jnp.max(s, axis=-1, keepdims=True))
                    alpha = jnp.exp(m_old - m_new)
                    p = jnp.exp(s - m_new)
                    l_s[h] = alpha * l_s[h] + jnp.sum(p, axis=-1, keepdims=True)
                    m_s[h] = m_new
                    outs.append((alpha, _dot(p, v2)))
                alpha2 = jnp.where(lo, outs[0][0], outs[1][0])
                acc[:, sl] = alpha2 * acc[:, sl] + jnp.where(lo, outs[0][1], outs[1][1])

        _on_diag_or_not(qi, ki, step)

        @pl.when(ki == qi)
        def _():
            lse = jnp.zeros((tq, LANES), F32)
            for j in range(npair):
                sl = slice(j * LANES, (j + 1) * LANES)
                inv = jnp.where(lo, 1.0 / l_s[2 * j], 1.0 / l_s[2 * j + 1])
                y = acc[:, sl] * inv
                yf_ref[:, sl] = y
                yb_ref[:, sl] = y.astype(yb_ref.dtype)
            for h in range(FOX_HEADS):
                lse = jnp.where(_lane((tq, LANES)) == h, m_s[h] + jnp.log(l_s[h]), lse)
            lse_ref[...] = lse

    qmap = lambda s, qs_r, ks_r: (qs_r[s], 0)
    kmap = lambda s, qs_r, ks_r: (ks_r[s], 0)
    grid_spec = pltpu.PrefetchScalarGridSpec(
        num_scalar_prefetch=2, grid=(nsteps,),
        in_specs=[pl.BlockSpec((tq, 512), qmap), pl.BlockSpec((tq, 512), kmap), pl.BlockSpec((tq, 512), kmap),
                  pl.BlockSpec((SUBLANES, LANES), lambda s, qs_r, ks_r: (qs_r[s] * (tq // SUBLANES), 0)),
                  pl.BlockSpec((SUBLANES, tq), lambda s, qs_r, ks_r: (0, ks_r[s]))] + s_specs,
        out_specs=[pl.BlockSpec((tq, 512), qmap), pl.BlockSpec((tq, 512), qmap), pl.BlockSpec((tq, LANES), qmap)] + s_specs,
        scratch_shapes=[pltpu.VMEM((tq, 512), F32), pltpu.VMEM((FOX_HEADS, tq, 1), F32), pltpu.VMEM((FOX_HEADS, tq, 1), F32)] + s_scratch,
    )
    res = pl.pallas_call(
        body, name=name, grid_spec=grid_spec,
        out_shape=[jax.ShapeDtypeStruct((T, 512), _MXU), jax.ShapeDtypeStruct((T, 512), F32), jax.ShapeDtypeStruct((T, LANES), F32)] + s_shapes,
        compiler_params=_cp(("arbitrary",)),
    )(qs, ks, qn, kn, vv, c, ct, *s_args)
    return res[:3], res[3:]


def _shift_down(ext, k, tm):
    return pltpu.roll(ext, k, axis=0)[HALO:HALO + tm]


def _pool_p(z_ext, g, tm, pos):
    w = POOL_WINDOWS[g]
    s, span = z_ext, 1
    while span < w:
        s = s + pltpu.roll(s, span, axis=0)
        span *= 2
    cnt = jnp.minimum(pos + 1, w).astype(F32)
    return s[HALO:HALO + tm] / cnt - z_ext[HALO:HALO + tm]


def _odd_fwd(name, z, w_pool, s_pool, conv_w8):
    T = z.shape[0]
    tm = _tile(T, 256, HALO)
    r = tm // HALO

    def body(z_ref, zb_ref, wp_ref, sp_ref, cw_ref, y_ref):
        i = pl.program_id(0)
        first = i == 0
        pos = i * tm + _row((tm, LANES))
        for g in range(len(POOL_WINDOWS)):
            sl = slice(g * LANES, (g + 1) * LANES)
            halo = jnp.where(first, 0.0, zb_ref[:, sl])
            z_ext = jnp.concatenate([halo, z_ref[:, sl]], axis=0)
            p = _pool_p(z_ext, g, tm, pos)
            y_ref[:, sl] = (_dot(p, wp_ref[g]) * sp_ref[:, sl]).astype(y_ref.dtype)
        for j in range(CONV_WIDTH // LANES):
            sl = slice(j * LANES, (j + 1) * LANES)
            hd = slice(POOL_WIDTH + j * LANES, POOL_WIDTH + (j + 1) * LANES)
            gb = slice(POOL_WIDTH + CONV_WIDTH + j * LANES, POOL_WIDTH + CONV_WIDTH + (j + 1) * LANES)
            gc = slice(POOL_WIDTH + 2 * CONV_WIDTH + j * LANES, POOL_WIDTH + 2 * CONV_WIDTH + (j + 1) * LANES)
            xg_b = jnp.where(first, 0.0, zb_ref[:, gc] * zb_ref[:, hd])
            xg = jnp.concatenate([xg_b, z_ref[:, gc] * z_ref[:, hd]], axis=0)
            conv = (cw_ref[0:1, sl] * _shift_down(xg, 2, tm) + cw_ref[1:2, sl] * _shift_down(xg, 1, tm)
                    + cw_ref[2:3, sl] * xg[HALO:HALO + tm])
            y_ref[:, POOL_WIDTH + j * LANES:POOL_WIDTH + (j + 1) * LANES] = (z_ref[:, gb] * conv).astype(y_ref.dtype)

    full = lambda s: pl.BlockSpec(s, lambda i: tuple(0 for _ in s))
    return pl.pallas_call(
        body, name=name, grid=(T // tm,),
        in_specs=[pl.BlockSpec((tm, ODD_IN), lambda i: (i, 0)),
                  pl.BlockSpec((HALO, ODD_IN), lambda i: (jnp.maximum(i * r - 1, 0), 0)),
                  full((4, LANES, LANES)), full((1, POOL_WIDTH)), full((SUBLANES, CONV_WIDTH))],
        out_specs=pl.BlockSpec((tm, 1024), lambda i: (i, 0)),
        out_shape=jax.ShapeDtypeStruct((T, 1024), _MXU),
        compiler_params=_cp(("parallel",)),
    )(z, z, w_pool, s_pool, conv_w8)


def _mem_fwd(name, mem, g_mem, wkv, gk):
    M = mem.shape[0]

    def body(mem_ref, g_ref, w_ref, gk_ref, m_ref, kv_ref, kn_ref, v_ref):
        x = mem_ref[...]
        m = ((x * _rms_rows(x)) * g_ref[...]).astype(m_ref.dtype)
        m_ref[...] = m
        kv = _dot(m, w_ref[...])
        kv_ref[...] = kv
        for h in range(XA_HEADS):
            sl = slice(h * LANES, (h + 1) * LANES)
            k = kv[:, sl]
            kn_ref[:, sl] = ((k * _rms_rows(k)) * gk_ref[...]).astype(kn_ref.dtype)
        v_ref[...] = kv[:, XA_WIDTH:].astype(v_ref.dtype)

    return pl.pallas_call(
        body, name=name,
        out_shape=[jax.ShapeDtypeStruct((M, D_MODEL), _MXU), jax.ShapeDtypeStruct((M, 2 * XA_WIDTH), F32),
                   jax.ShapeDtypeStruct((M, XA_WIDTH), _MXU), jax.ShapeDtypeStruct((M, XA_WIDTH), _MXU)],
        compiler_params=pltpu.CompilerParams(vmem_limit_bytes=VMEM_LIMIT),
    )(mem, g_mem.reshape(1, D_MODEL), wkv, gk.reshape(1, XA_HEAD_DIM))


def _xa_probs(qx, gq, kn_h):
    r = _rms_rows(qx)
    qhat = qx * r
    qn = qhat * gq
    s = _dot_nt(qn, kn_h) * (1.0 / math.sqrt(XA_HEAD_DIM))
    s = s - jnp.max(s, axis=-1, keepdims=True)
    e = jnp.exp(s)
    return qhat, r, qn, e / jnp.sum(e, axis=-1, keepdims=True)


def _xa_fwd(name, qx, gq, kn, vx):
    T, M = qx.shape[0], kn.shape[0]
    tm = _tile(T, 512, 16)

    def body(q_ref, gq_ref, k_ref, v_ref, o_ref):
        for h in range(XA_HEADS):
            sl = slice(h * LANES, (h + 1) * LANES)
            _, _, _, p = _xa_probs(q_ref[:, sl], gq_ref[...], k_ref[:, sl])
            o_ref[:, sl] = _dot(p, v_ref[:, sl]).astype(o_ref.dtype)

    full = lambda s: pl.BlockSpec(s, lambda i: tuple(0 for _ in s))
    return pl.pallas_call(
        body, name=name, grid=(T // tm,),
        in_specs=[pl.BlockSpec((tm, XA_WIDTH), lambda i: (i, 0)), full((1, LANES)), full((M, XA_WIDTH)), full((M, XA_WIDTH))],
        out_specs=pl.BlockSpec((tm, XA_WIDTH), lambda i: (i, 0)),
        out_shape=jax.ShapeDtypeStruct((T, XA_WIDTH), _MXU),
        compiler_params=_cp(("parallel",)),
    )(qx, gq.reshape(1, XA_HEAD_DIM), kn, vx)


def _ffn_up(name, h, wg, wu):
    T, K = h.shape
    N = wg.shape[1]
    tm, tn = _tile(T, 512, 16), _tile(N, 1536, LANES)

    def body(h_ref, wg_ref, wu_ref, a_ref, b_ref, s_ref):
        hv = h_ref[...]
        a, b = _dot(hv, wg_ref[...]), _dot(hv, wu_ref[...])
        a_ref[...] = a
        b_ref[...] = b
        s_ref[...] = (jax.nn.silu(a) * b).astype(s_ref.dtype)

    w_spec = pl.BlockSpec((K, tn), lambda j, i: (0, j))
    o_spec = pl.BlockSpec((tm, tn), lambda j, i: (i, j))
    return pl.pallas_call(
        body, name=name, grid=(N // tn, T // tm),
        in_specs=[pl.BlockSpec((tm, K), lambda j, i: (i, 0)), w_spec, w_spec],
        out_specs=[o_spec, o_spec, o_spec],
        out_shape=[jax.ShapeDtypeStruct((T, N), F32), jax.ShapeDtypeStruct((T, N), F32), jax.ShapeDtypeStruct((T, N), _MXU)],
        compiler_params=_cp(("parallel", "parallel")),
    )(h, wg, wu)


def _loss_grad(name, y, target):
    T, Dm = y.shape
    tm = _tile(T, 512, 8)

    def body(y_ref, t_ref, dy_ref, l_ref):
        e = y_ref[...] - t_ref[...]
        dy_ref[...] = e * (1.0 / Dm)

        @pl.when(pl.program_id(0) == 0)
        def _():
            l_ref[...] = jnp.zeros_like(l_ref)

        l_ref[...] += jnp.sum(e * e, axis=0, keepdims=True) * (0.5 / Dm)

    blk = pl.BlockSpec((tm, Dm), lambda i: (i, 0))
    return pl.pallas_call(
        body, name=name, grid=(T // tm,),
        in_specs=[blk, blk], out_specs=[blk, pl.BlockSpec((1, Dm), lambda i: (0, 0))],
        out_shape=[jax.ShapeDtypeStruct((T, Dm), F32), jax.ShapeDtypeStruct((1, Dm), F32)],
        compiler_params=_cp(("arbitrary",)),
    )(y, target)


def _pad_lanes(v, n=LANES):
    v = v.reshape(1, -1)
    return jnp.pad(v, ((0, 0), (0, n - v.shape[1])))


def _layer_fwd(l, x, mem, P, side=None, on_side=None):
    sv = {"x0": x}
    W = _layer_weights(l, P, "in")
    h1 = _norm_fwd(f"norm_mix{l}", x, W["g_mix"])
    sv["h1"] = h1
    if l % 2 == 0:
        z = _mm(f"e_in{l}", [(h1, W["w_in"])], "nn")
        ya, qn, kn, vv, c, ct = _even_pre(f"even_pre{l}", z, W["g_v"], W["w_s"], W["bs_cols"], W["g_qn2"], W["g_kn2"], W["b_f_pad"])
        (yb, ybf, lse), side_out = _fox_fwd(f"fox_fwd{l}", qn, kn, vv, c, ct, side=side)
        if on_side is not None:
            on_side(side_out)
        W.update(_layer_weights(l, P, "rest"))
        sv.update(z=z, ya=ya, qn=qn, kn=kn, vv=vv, c=c, ct=ct, yb=yb, ybf=ybf, lse=lse)
        x1 = _mm(f"e_out{l}", [(ya, W["w_out"][:A_WIDTH]), (yb, W["w_out"][A_WIDTH:])], "nn", residual=x)
    else:
        W.update(_layer_weights(l, P, "rest"))
        z = _mm(f"o_in{l}", [(h1, W["w_in"])], "nn")
        ycat = _odd_fwd(f"odd_fwd{l}", z, W["w_pool"], W["s_pool"], W["conv_w8"])
        sv.update(z=z, ycat=ycat)
        x1 = _mm(f"o_out{l}", [(ycat, W["w_out"])], "nn", residual=x)
    sv["x1"] = x1
    h2 = _norm_fwd(f"norm_xa{l}", x1, W["g_xa"])
    qx = _mm(f"xa_q{l}", [(h2, W["wq"])], "nn")
    m, kv, kn_x, vx = _mem_fwd(f"mem_fwd{l}", mem, W["g_mem"], W["wkv"], W["gk"])
    o = _xa_fwd(f"xa_fwd{l}", qx, W["gq"], kn_x, vx)
    x2 = _mm(f"xa_o{l}", [(o, W["wo"])], "nn", residual=x1)
    sv.update(h2=h2, qx=qx, m=m, kv=kv, kn_x=kn_x, vx=vx, o=o, x2=x2)
    h3 = _norm_fwd(f"norm_ffn{l}", x2, W["g_ffn"])
    a, b, s = _ffn_up(f"ffn_up{l}", h3, W["w_gate"], W["w_up"])
    x3 = _mm(f"ffn_down{l}", [(s, W["w_down"])], "nn", residual=x2)
    sv.update(h3=h3, a=a, b=b, s=s)
    return x3, sv, W


def _layer_weights(l, P, part):
    i = l // 2
    if part == "in":
        W = {"g_mix": P["g_mix"][l]}
        if l % 2 == 0:
            w_in = _mx(P["e_w_in"][i])
            if w_in.shape[1] < EVEN_IN_PAD:
                w_in = jnp.pad(w_in, ((0, 0), (0, EVEN_IN_PAD - w_in.shape[1])))
            W.update(w_in=w_in, g_v=P["e_g_v"][i].reshape(1, A_WIDTH), w_s=P["e_w_s"][i],
                     bs_cols=jnp.pad(P["e_b_s"][i].T, ((0, 0), (0, LANES - A_GROUPS))),
                     g_qn2=jnp.tile(P["e_g_qn"][i], 2).reshape(1, LANES), g_kn2=jnp.tile(P["e_g_kn"][i], 2).reshape(1, LANES),
                     b_f_pad=_pad_lanes(P["e_b_f"][i]))
        return W
    W = {k: P[k][l] for k in ("g_xa", "g_mem", "g_ffn")}
    W.update(wq=_mx(P["xa_wq"][l]), wkv=_mx(P["xa_wkv"][l]), wo=_mx(P["xa_wo"][l]), gq=P["xa_gq"][l], gk=P["xa_gk"][l],
             w_gate=_mx(P["w_gate"][l]), w_up=_mx(P["w_up"][l]), w_down=_mx(P["w_down"][l]))
    if l % 2 == 0:
        W.update(w_out=_mx(P["e_w_out"][i]))
    else:
        W.update(w_in=_mx(P["o_w_in"][i]), w_pool=_mx(P["o_w_pool"][i]), s_pool=P["o_s_pool"][i].reshape(1, POOL_WIDTH),
                 conv_w8=jnp.pad(P["o_conv_w"][i], ((0, SUBLANES - CONV_K), (0, 0))), w_out=_mx(P["o_w_out"][i]))
    return W


def _ffn_bwd_act(name, g, wd, a, b):
    T, N = a.shape
    K = g.shape[1]
    tm, tn = _tile(T, 512, 16), _tile(N, 1536, LANES)

    def body(g_ref, wd_ref, a_ref, b_ref, da_ref, db_ref):
        ds = _dot_nt(g_ref[...], wd_ref[...])
        av = a_ref[...]
        sig = jax.nn.sigmoid(av)
        da_ref[...] = (ds * b_ref[...] * (sig * (1.0 + av * (1.0 - sig)))).astype(da_ref.dtype)
        db_ref[...] = (ds * (av * sig)).astype(db_ref.dtype)

    o_spec = pl.BlockSpec((tm, tn), lambda j, i: (i, j))
    return pl.pallas_call(
        body, name=name, grid=(N // tn, T // tm),
        in_specs=[pl.BlockSpec((tm, K), lambda j, i: (i, 0)), pl.BlockSpec((tn, K), lambda j, i: (j, 0)), o_spec, o_spec],
        out_specs=[o_spec, o_spec],
        out_shape=[jax.ShapeDtypeStruct((T, N), _MXU), jax.ShapeDtypeStruct((T, N), _MXU)],
        compiler_params=_cp(("parallel", "parallel")),
    )(g, wd, a, b)


def _xa_bwd(name, qx, do, gq, kn, vx):
    T, M = qx.shape[0], kn.shape[0]
    tm = _tile(T, 512, 16)

    def body(q_ref, do_ref, gq_ref, k_ref, v_ref, dq_ref, dk_ref, dv_ref, dg_ref):
        @pl.when(pl.program_id(0) == 0)
        def _():
            dk_ref[...] = jnp.zeros_like(dk_ref)
            dv_ref[...] = jnp.zeros_like(dv_ref)
            dg_ref[...] = jnp.zeros_like(dg_ref)

        gqv = gq_ref[...]
        for h in range(XA_HEADS):
            sl = slice(h * LANES, (h + 1) * LANES)
            qhat, r, qn, p = _xa_probs(q_ref[:, sl], gqv, k_ref[:, sl])
            do_h = do_ref[:, sl]
            dp = _dot_nt(do_h, v_ref[:, sl])
            dv_ref[:, sl] += _dot_tn(p, do_h)
            dsm = p * (dp - jnp.sum(p * dp, axis=-1, keepdims=True)) * (1.0 / math.sqrt(XA_HEAD_DIM))
            dqn = _dot(dsm, k_ref[:, sl])
            dk_ref[:, sl] += _dot_tn(dsm, qn)
            dg_ref[...] += jnp.sum(dqn * qhat, axis=0, keepdims=True)
            dxh = dqn * gqv
            dq_ref[:, sl] = (r * (dxh - qhat * jnp.mean(dxh * qhat, axis=-1, keepdims=True))).astype(dq_ref.dtype)

    full = lambda s: pl.BlockSpec(s, lambda i: tuple(0 for _ in s))
    blk = pl.BlockSpec((tm, XA_WIDTH), lambda i: (i, 0))
    return pl.pallas_call(
        body, name=name, grid=(T // tm,),
        in_specs=[blk, blk, full((1, LANES)), full((M, XA_WIDTH)), full((M, XA_WIDTH))],
        out_specs=[blk, full((M, XA_WIDTH)), full((M, XA_WIDTH)), full((1, LANES))],
        out_shape=[jax.ShapeDtypeStruct((T, XA_WIDTH), _MXU), jax.ShapeDtypeStruct((M, XA_WIDTH), F32),
                   jax.ShapeDtypeStruct((M, XA_WIDTH), F32), jax.ShapeDtypeStruct((1, LANES), F32)],
        compiler_params=_cp(("arbitrary",)),
    )(qx, do, gq.reshape(1, XA_HEAD_DIM), kn, vx)


def _mem_bwd(name, mem, g_mem, m, kv, gk, dkn, dvx, wkv):
    M = mem.shape[0]

    def body(mem_ref, g_ref, m_ref, kv_ref, gk_ref, dkn_ref, dvx_ref, w_ref, dw_ref, dgk_ref, dgm_ref):
        gkv = gk_ref[...]
        dgk = jnp.zeros((1, LANES), F32)
        parts = []
        for h in range(XA_HEADS):
            sl = slice(h * LANES, (h + 1) * LANES)
            k = kv_ref[:, sl]
            r = _rms_rows(k)
            khat = k * r
            dkn_h = dkn_ref[:, sl]
            dgk = dgk + jnp.sum(dkn_h * khat, axis=0, keepdims=True)
            dxh = dkn_h * gkv
            parts.append(r * (dxh - khat * jnp.mean(dxh * khat, axis=-1, keepdims=True)))
        dgk_ref[...] = dgk
        dkv = jnp.concatenate(parts + [dvx_ref[...]], axis=1)
        dw_ref[...] = _dot_tn(m_ref[...], dkv)
        dm = _dot_nt(dkv, w_ref[...])
        x = mem_ref[...]
        dgm_ref[...] = jnp.sum(dm * (x * _rms_rows(x)), axis=0, keepdims=True)

    return pl.pallas_call(
        body, name=name,
        out_shape=[jax.ShapeDtypeStruct((D_MODEL, 2 * XA_WIDTH), F32), jax.ShapeDtypeStruct((1, LANES), F32),
                   jax.ShapeDtypeStruct((1, D_MODEL), F32)],
        compiler_params=pltpu.CompilerParams(vmem_limit_bytes=VMEM_LIMIT),
    )(mem, g_mem.reshape(1, D_MODEL), m, kv, gk.reshape(1, XA_HEAD_DIM), dkn, dvx, wkv)


def _fox_bwd_common(q2, k2, v2, dy2, yf2, c_ref, ct_ref, lse_ref, e, h, lo, diag, tq):
    sel = lo if e == 0 else ~lo
    qh = jnp.where(sel, q2, jnp.zeros_like(q2))
    kh = jnp.where(sel, k2, jnp.zeros_like(k2))
    dyh = jnp.where(sel, dy2, 0.0)
    s = _fox_logits(qh, k2, c_ref[0:1, h:h + 1], ct_ref[h:h + 1, :], diag, tq)
    p = jnp.exp(s - lse_ref[:, h:h + 1])
    dp = _dot_nt(dyh, v2)
    delta = jnp.sum(dyh * yf2, axis=-1, keepdims=True)
    return p, p * (dp - delta), qh, kh, dyh


def _fox_bwd(name, qn, kn, vv, c, ct, lse, dycat, ybf, side=None):
    T = qn.shape[0]
    tq = _tile(T, FOX_TILE, LANES)
    nblk = T // tq
    qs, ks = _tri_steps(nblk, False)
    nsteps = int(qs.shape[0])
    scale = 1.0 / math.sqrt(FOX_HEAD_DIM)
    s_specs, s_args, s_shapes, s_scratch = _side_specs(side)

    def body(qs_ref, ks_ref, *refs):
        main_in, side_in, main_out, side_out, main_scr, sems = _split_side(refs, 8, 5, 3, side)
        q_ref, k_ref, v_ref, c_ref, ct_ref, lse_ref, dy_ref, yf_ref = main_in
        dq_hbm, drow_hbm, dk_ref, dv_ref, dct_ref = main_out
        dq_acc, drow_acc, out_sem = main_scr
        s_id = pl.program_id(0)
        qi, ki = qs_ref[s_id], ks_ref[s_id]
        _side_run(side, s_id, nsteps, side_in, side_out, sems)
        rows = pl.ds(pl.multiple_of(qi * tq, tq), tq)

        @pl.when(qi == ki)
        def _():
            dk_ref[...] = jnp.zeros_like(dk_ref)
            dv_ref[...] = jnp.zeros_like(dv_ref)
            dct_ref[...] = jnp.zeros_like(dct_ref)

        @pl.when(ki == 0)
        def _():
            dq_acc[rows, :] = jnp.zeros((tq, FOX_WIDTH), F32)
            drow_acc[rows, :] = jnp.zeros((tq, LANES), F32)

        lo = _lane((tq, LANES)) < FOX_HEAD_DIM

        def step(diag):
            drow = jnp.zeros((tq, LANES), F32)
            for j in range(FOX_WIDTH // LANES):
                sl = slice(j * LANES, (j + 1) * LANES)
                q2, k2, v2, dy2, yf2 = q_ref[:, sl], k_ref[:, sl], v_ref[:, sl], dy_ref[:, sl], yf_ref[:, sl]
                dq_t, dk_t, dv_t = None, None, None
                for e in range(2):
                    h = 2 * j + e
                    p, ds, qh, kh, dyh = _fox_bwd_common(q2, k2, v2, dy2, yf2, c_ref, ct_ref, lse_ref, e, h, lo, diag, tq)
                    drow = jnp.where(_lane((tq, LANES)) == h, jnp.sum(ds, axis=-1, keepdims=True), drow)
                    dct_ref[h:h + 1, :] -= jnp.sum(ds, axis=0, keepdims=True)
                    a, b, d = _dot(ds, kh), _dot_tn(ds, qh), _dot_tn(p, dyh)
                    dq_t = a if dq_t is None else dq_t + a
                    dk_t = b if dk_t is None else dk_t + b
                    dv_t = d if dv_t is None else dv_t + d
                dq_acc[rows, sl] += dq_t * scale
                dk_ref[:, sl] += dk_t
                dv_ref[:, sl] += dv_t
            drow_acc[rows, :] += drow

        _on_diag_or_not(qi, ki, step)

        @pl.when(s_id == nsteps - 1)
        def _():
            for src, dst in ((dq_acc, dq_hbm), (drow_acc, drow_hbm)):
                cp = pltpu.make_async_copy(src, dst, out_sem)
                cp.start()
                cp.wait()

    qmap = lambda s, a, b: (a[s], 0)
    kmap = lambda s, a, b: (b[s], 0)
    ctmap = lambda s, a, b: (0, b[s])
    any_spec = pl.BlockSpec(memory_space=pl.ANY)
    grid_spec = pltpu.PrefetchScalarGridSpec(
        num_scalar_prefetch=2, grid=(nsteps,),
        in_specs=[pl.BlockSpec((tq, 512), qmap), pl.BlockSpec((tq, 512), kmap), pl.BlockSpec((tq, 512), kmap),
                  pl.BlockSpec((SUBLANES, LANES), lambda s, a, b: (a[s] * (tq // SUBLANES), 0)), pl.BlockSpec((SUBLANES, tq), ctmap),
                  pl.BlockSpec((tq, LANES), qmap), pl.BlockSpec((tq, 512), lambda s, a, b: (a[s], 1)), pl.BlockSpec((tq, 512), qmap)] + s_specs,
        out_specs=[any_spec, any_spec, pl.BlockSpec((tq, 512), kmap), pl.BlockSpec((tq, 512), kmap), pl.BlockSpec((SUBLANES, tq), ctmap)] + s_specs,
        scratch_shapes=[pltpu.VMEM((T, FOX_WIDTH), F32), pltpu.VMEM((T, LANES), F32), pltpu.SemaphoreType.DMA] + s_scratch,
    )
    res = pl.pallas_call(
        body, name=name, grid_spec=grid_spec,
        out_shape=[jax.ShapeDtypeStruct((T, 512), F32), jax.ShapeDtypeStruct((T, LANES), F32), jax.ShapeDtypeStruct((T, 512), F32),
                   jax.ShapeDtypeStruct((T, 512), F32), jax.ShapeDtypeStruct((SUBLANES, T), F32)] + s_shapes,
        compiler_params=_cp(("arbitrary",)),
    )(qs, ks, qn, kn, vv, c, ct, lse, dycat, ybf, *s_args)
    return res[:5], res[5:]


def _fold64(row):
    return row + pltpu.roll(row, FOX_HEAD_DIM, axis=1)


def _even_bwd(name, z, dycat, dqn, dkn, dvv, dct, drow, g_v, w_s, bs_cols, g_qn2, g_kn2, b_f_pad):
    T = z.shape[0]
    tm = _tile(T, 256, A_BLOCK)
    nb, nt = tm // A_BLOCK, T // tm

    def body(z_ref, dya_ref, dq_ref, dk_ref, dv_ref, dct_ref, drow_ref, gv_ref, ws_ref, bs_ref, gq_ref, gk_ref, bf_ref,
             dz_ref, dws_ref, dbs_ref, dgv_ref, dgq_ref, dgk_ref, dbf_ref, carry):
        @pl.when(pl.program_id(0) == 0)
        def _():
            for r in (dws_ref, dbs_ref, dgv_ref, dgq_ref, dgk_ref, dbf_ref, carry):
                r[...] = jnp.zeros_like(r)

        keep = (_row((A_BLOCK, A_BLOCK)) // CHUNK) >= (_lane((A_BLOCK, A_BLOCK)) // CHUNK)
        for g in range(A_GROUPS):
            sl = slice(g * A_GROUP_DIM, (g + 1) * A_GROUP_DIM)
            wm = _mx(_masked_ws(ws_ref, g))
            zu = z_ref[:, sl]
            zv = z_ref[:, A_WIDTH + g * A_GROUP_DIM:A_WIDTH + (g + 1) * A_GROUP_DIM]
            u, v = _gelu(zu), _gelu(zv)
            r = _rms_rows(v)
            vhat = v * r
            gvv = gv_ref[:, sl]
            vn = _mx(vhat * gvv)
            dya = dya_ref[:, sl]
            du_parts, dvn_parts = [], []
            for n in range(nb):
                rows = slice(n * A_BLOCK, (n + 1) * A_BLOCK)
                s = jnp.dot(wm, vn[rows], preferred_element_type=F32) + bs_ref[:, g:g + 1]
                du_parts.append(dya[rows] * s)
                d_s = dya[rows] * u[rows]
                dbs_ref[...] += jnp.where(_lane((A_BLOCK, LANES)) == g, jnp.sum(d_s, axis=-1, keepdims=True), 0.0)
                dws_ref[g] += jnp.where(keep, _dot_nt(d_s, vn[rows]), 0.0)
                dvn_parts.append(_dot_tn(wm, d_s))
            dvn = jnp.concatenate(dvn_parts, axis=0)
            dgv_ref[:, sl] += jnp.sum(dvn * vhat, axis=0, keepdims=True)
            dxh = dvn * gvv
            dv = r * (dxh - vhat * jnp.mean(dxh * vhat, axis=-1, keepdims=True))
            dz_ref[:, sl] = (jnp.concatenate(du_parts, axis=0) * _gelu_grad(zu)).astype(dz_ref.dtype)
            dz_ref[:, A_WIDTH + g * A_GROUP_DIM:A_WIDTH + (g + 1) * A_GROUP_DIM] = (dv * _gelu_grad(zv)).astype(dz_ref.dtype)
        o = 2 * A_WIDTH
        for j in range(FOX_WIDTH // LANES):
            sl = slice(j * LANES, (j + 1) * LANES)
            for (off, d_ref, g_ref, dg_ref) in ((o, dq_ref, gq_ref, dgq_ref), (o + FOX_WIDTH, dk_ref, gk_ref, dgk_ref)):
                zq = z_ref[:, off + j * LANES:off + (j + 1) * LANES]
                r = _head64_rms(zq)
                qhat = zq * r
                dn = d_ref[:, sl]
                dg_ref[...] += _fold64(jnp.sum(dn * qhat, axis=0, keepdims=True))
                dxh = dn * g_ref[...]
                dz_ref[:, off + j * LANES:off + (j + 1) * LANES] = (r * (dxh - qhat * _head64_mean(dxh * qhat))).astype(dz_ref.dtype)
            dz_ref[:, o + 2 * FOX_WIDTH + j * LANES:o + 2 * FOX_WIDTH + (j + 1) * LANES] = dv_ref[:, sl].astype(dz_ref.dtype)
        dct_v = dct_ref[...] + drow_ref[...].T[0:SUBLANES, :]
        upper = (_row((tm, tm)) >= _lane((tm, tm))).astype(F32)
        dlf_t = _dot_f32(dct_v, upper) + carry[:, 0:1]
        carry[...] += jnp.sum(dct_v, axis=1, keepdims=True)
        dlf = jnp.concatenate([dlf_t, jnp.zeros((LANES - SUBLANES, tm), F32)], axis=0).T
        zf = z_ref[:, o + 3 * FOX_WIDTH:o + 3 * FOX_WIDTH + LANES]
        dzf = dlf * jax.nn.sigmoid(-(zf + bf_ref[...]))
        dbf_ref[...] += jnp.sum(dzf, axis=0, keepdims=True)
        dz_ref[:, o + 3 * FOX_WIDTH:o + 3 * FOX_WIDTH + LANES] = dzf.astype(dz_ref.dtype)

    rev = lambda i: nt - 1 - i
    wide = lambda n, col=0: pl.BlockSpec((tm, n), lambda i: (rev(i), col))
    full = lambda s: pl.BlockSpec(s, lambda i: tuple(0 for _ in s))
    vec = full((1, LANES))
    return pl.pallas_call(
        body, name=name, grid=(nt,),
        in_specs=[wide(EVEN_IN_PAD), wide(512), wide(512), wide(512), wide(512), pl.BlockSpec((SUBLANES, tm), lambda i: (0, rev(i))),
                  wide(LANES), full((1, A_WIDTH)), full((A_GROUPS, A_BLOCK, A_BLOCK)), full((A_BLOCK, LANES)), vec, vec, vec],
        out_specs=[wide(EVEN_IN_PAD), full((A_GROUPS, A_BLOCK, A_BLOCK)), full((A_BLOCK, LANES)), full((1, A_WIDTH)), vec, vec, vec],
        out_shape=[jax.ShapeDtypeStruct((T, EVEN_IN_PAD), _MXU), jax.ShapeDtypeStruct((A_GROUPS, A_BLOCK, A_BLOCK), F32),
                   jax.ShapeDtypeStruct((A_BLOCK, LANES), F32), jax.ShapeDtypeStruct((1, A_WIDTH), F32),
                   jax.ShapeDtypeStruct((1, LANES), F32), jax.ShapeDtypeStruct((1, LANES), F32), jax.ShapeDtypeStruct((1, LANES), F32)],
        scratch_shapes=[pltpu.VMEM((SUBLANES, LANES), F32)],
        compiler_params=_cp(("arbitrary",)),
    )(z, dycat, dqn, dkn, dvv, dct, drow, g_v, w_s, bs_cols, g_qn2, g_kn2, b_f_pad)


def _shift_up(ext, k, tm):
    return pltpu.roll(ext, ext.shape[0] - k, axis=0)[0:tm]


def _odd_bwd(name, z, dycat, w_pool, s_pool, conv_w8):
    T = z.shape[0]
    tm = _tile(T, 256, HALO)
    r, nt = tm // HALO, T // tm
    n_ext = tm + HALO

    def body(z_ref, zb_ref, zn_ref, dy_ref, dyn_ref, wp_ref, sp_ref, cw_ref, dz_ref, dwp_ref, dsp_ref, dcw_ref):
        i = pl.program_id(0)

        @pl.when(i == 0)
        def _():
            for rr in (dwp_ref, dsp_ref, dcw_ref):
                rr[...] = jnp.zeros_like(rr)

        first, last = i == 0, i == nt - 1
        pos = i * tm + _row((tm, LANES))
        pos_ext = i * tm + _row((n_ext, LANES))
        for g, w in enumerate(POOL_WINDOWS):
            sl = slice(g * LANES, (g + 1) * LANES)
            z_ext = jnp.concatenate([jnp.where(first, 0.0, zb_ref[:, sl]), z_ref[:, sl]], axis=0)
            p = _pool_p(z_ext, g, tm, pos)
            spv = sp_ref[:, sl]
            dyc = dy_ref[:, sl]
            dsp_ref[:, sl] += jnp.sum(dyc * _dot(p, wp_ref[g]), axis=0, keepdims=True)
            dpw_ext = jnp.concatenate([dyc, jnp.where(last, 0.0, dyn_ref[:, sl])], axis=0) * spv
            dwp_ref[g] += _dot_tn(p, dpw_ext[0:tm])
            dp_ext = _dot_nt(dpw_ext, wp_ref[g])
            f = dp_ext / jnp.minimum(pos_ext + 1, w).astype(F32)
            span = 1
            while span < w:
                f = f + pltpu.roll(f, n_ext - span, axis=0)
                span *= 2
            dz_ref[:, sl] = (f[0:tm] - dp_ext[0:tm]).astype(dz_ref.dtype)
        for j in range(CONV_WIDTH // LANES):
            sl = slice(j * LANES, (j + 1) * LANES)
            hd = slice(POOL_WIDTH + j * LANES, POOL_WIDTH + (j + 1) * LANES)
            gb = slice(POOL_WIDTH + CONV_WIDTH + j * LANES, POOL_WIDTH + CONV_WIDTH + (j + 1) * LANES)
            gc = slice(POOL_WIDTH + 2 * CONV_WIDTH + j * LANES, POOL_WIDTH + 2 * CONV_WIDTH + (j + 1) * LANES)
            ysl = slice(POOL_WIDTH + j * LANES, POOL_WIDTH + (j + 1) * LANES)
            hdv, gbv, gcv = z_ref[:, hd], z_ref[:, gb], z_ref[:, gc]
            xg = gcv * hdv
            xg_ext = jnp.concatenate([jnp.where(first, 0.0, zb_ref[:, gc] * zb_ref[:, hd]), xg], axis=0)
            xg1, xg2 = _shift_down(xg_ext, 1, tm), _shift_down(xg_ext, 2, tm)
            w0, w1, w2 = cw_ref[0:1, sl], cw_ref[1:2, sl], cw_ref[2:3, sl]
            conv = w0 * xg2 + w1 * xg1 + w2 * xg
            dyd = dy_ref[:, ysl]
            dconv = dyd * gbv
            dconv_ext = jnp.concatenate([dconv, jnp.where(last, 0.0, dyn_ref[:, ysl] * zn_ref[:, gb])], axis=0)
            dcw_ref[0:1, sl] += jnp.sum(dconv * xg2, axis=0, keepdims=True)
            dcw_ref[1:2, sl] += jnp.sum(dconv * xg1, axis=0, keepdims=True)
            dcw_ref[2:3, sl] += jnp.sum(dconv * xg, axis=0, keepdims=True)
            dxg = w2 * dconv + w1 * _shift_up(dconv_ext, 1, tm) + w0 * _shift_up(dconv_ext, 2, tm)
            dz_ref[:, hd] = (dxg * gcv).astype(dz_ref.dtype)
            dz_ref[:, gb] = (dyd * conv).astype(dz_ref.dtype)
            dz_ref[:, gc] = (dxg * hdv).astype(dz_ref.dtype)

    full = lambda s: pl.BlockSpec(s, lambda i: tuple(0 for _ in s))
    back = lambda n: pl.BlockSpec((HALO, n), lambda i: (jnp.maximum(i * r - 1, 0), 0))
    nxt = lambda n: pl.BlockSpec((HALO, n), lambda i: (jnp.minimum((i + 1) * r, T // HALO - 1), 0))
    return pl.pallas_call(
        body, name=name, grid=(nt,),
        in_specs=[pl.BlockSpec((tm, ODD_IN), lambda i: (i, 0)), back(ODD_IN), nxt(ODD_IN),
                  pl.BlockSpec((tm, 1024), lambda i: (i, 0)), nxt(1024),
                  full((4, LANES, LANES)), full((1, POOL_WIDTH)), full((SUBLANES, CONV_WIDTH))],
        out_specs=[pl.BlockSpec((tm, ODD_IN), lambda i: (i, 0)), full((4, LANES, LANES)), full((1, POOL_WIDTH)), full((SUBLANES, CONV_WIDTH))],
        out_shape=[jax.ShapeDtypeStruct((T, ODD_IN), _MXU), jax.ShapeDtypeStruct((4, LANES, LANES), F32),
                   jax.ShapeDtypeStruct((1, POOL_WIDTH), F32), jax.ShapeDtypeStruct((SUBLANES, CONV_WIDTH), F32)],
        compiler_params=_cp(("arbitrary",)),
    )(z, z, z, dycat, dycat, w_pool, s_pool, conv_w8)


def _layer_bwd(l, g3, mem, W, sv, side_fn=None):
    i = l // 2
    G = {}
    side_out = ()
    da, db = _ffn_bwd_act(f"ffn_bwd_act{l}", g3, W["w_down"], sv["a"], sv["b"])
    G["w_down"] = _mm(f"d_w_down{l}", [(sv["s"], g3)], "tn")
    dh3 = _mm(f"d_h3{l}", [(da, W["w_gate"]), (db, W["w_up"])], "nt")
    G["w_gate"] = _mm(f"d_w_gate{l}", [(sv["h3"], da)], "tn")
    G["w_up"] = _mm(f"d_w_up{l}", [(sv["h3"], db)], "tn")
    g2, dg = _norm_bwd(f"norm_ffn_bwd{l}", dh3, sv["x2"], W["g_ffn"], g3)
    G["g_ffn"] = dg[0]

    do = _mm(f"d_o{l}", [(g2, W["wo"])], "nt", out_dtype=_MXU)
    G["xa_wo"] = _mm(f"d_wo{l}", [(sv["o"], g2)], "tn")
    dqx, dkn, dvx, dgq = _xa_bwd(f"xa_bwd{l}", sv["qx"], do, W["gq"], sv["kn_x"], sv["vx"])
    G["xa_gq"] = dgq[0]
    dh2 = _mm(f"d_h2{l}", [(dqx, W["wq"])], "nt")
    G["xa_wq"] = _mm(f"d_wq{l}", [(sv["h2"], dqx)], "tn")
    g1, dg = _norm_bwd(f"norm_xa_bwd{l}", dh2, sv["x1"], W["g_xa"], g2)
    G["g_xa"] = dg[0]
    dwkv, dgk, dgm = _mem_bwd(f"mem_bwd{l}", mem, W["g_mem"], sv["m"], sv["kv"], W["gk"], dkn, dvx, W["wkv"])
    G["xa_wkv"], G["xa_gk"], G["g_mem"] = dwkv, dgk[0], dgm[0]

    dycat = _mm(f"d_ycat{l}", [(g1, W["w_out"])], "nt")
    if l % 2 == 0:
        G["e_w_out"] = jnp.concatenate([_mm(f"d_e_wout_a{l}", [(sv["ya"], g1)], "tn"), _mm(f"d_e_wout_b{l}", [(sv["yb"], g1)], "tn")], axis=0)
        fa = (sv["qn"], sv["kn"], sv["vv"], sv["c"], sv["ct"], sv["lse"], dycat, sv["ybf"])
        (dqn, drow, dkn_f, dvv, dct), side_out = _fox_bwd(f"fox_bwd{l}", *fa, side=side_fn(G) if side_fn is not None else None)
        dz, dws, dbs, dgv, dgq2, dgk2, dbf = _even_bwd(f"even_bwd{l}", sv["z"], dycat, dqn, dkn_f, dvv, dct, drow, W["g_v"], W["w_s"],
                                                      W["bs_cols"], W["g_qn2"], W["g_kn2"], W["b_f_pad"])
        G.update(e_w_s=dws, e_b_s=dbs[:, :A_GROUPS].T, e_g_v=dgv[0], e_g_qn=dgq2[0, :FOX_HEAD_DIM], e_g_kn=dgk2[0, :FOX_HEAD_DIM],
                 e_b_f=dbf[0, :FOX_HEADS])
        G["e_w_in"] = _mm(f"d_e_win{l}", [(sv["h1"], dz)], "tn")
    else:
        G["o_w_out"] = _mm(f"d_o_wout{l}", [(sv["ycat"], g1)], "tn")
        dz, dwp, dsp, dcw = _odd_bwd(f"odd_bwd{l}", sv["z"], dycat, W["w_pool"], W["s_pool"], W["conv_w8"])
        G.update(o_w_pool=dwp, o_s_pool=dsp[0], o_conv_w=dcw[:CONV_K])
        G["o_w_in"] = _mm(f"d_o_win{l}", [(sv["h1"], dz)], "tn")
    dh1 = _mm(f"d_h1{l}", [(dz, W["w_in"])], "nt")
    g0, dg = _norm_bwd(f"norm_mix_bwd{l}", dh1, sv["x0"], W["g_mix"], g1)
    G["g_mix"] = dg[0]
    return g0, G, side_out


PACK_W = 1024
MESH_T = pl.DeviceIdType.MESH
_ANY = pl.BlockSpec(memory_space=pl.ANY)


def _my_place():
    x, y, c = lax.axis_index("x"), lax.axis_index("y"), lax.axis_index("c")
    return x, y, c


def _flip(v, bit):
    return 1 - v if bit else v


N_PEERS = N_DEV - 1


def _all_gather(name, blks):
    side = _TwoLevelGather(blks)

    def body(*refs):
        _, side_in, _, side_out, _, sems = _split_side(refs, 0, 0, 0, side)
        side.start(side_in, side_out, sems)
        side.middle(side_in, side_out, sems)
        side.finish(side_in, side_out, sems)

    return pl.pallas_call(
        body, name=name, out_shape=side.out_shape,
        in_specs=[_ANY] * side.n, out_specs=[_ANY] * side.n, scratch_shapes=side.scratch,
    )(*blks)


class _Direct:
    def __init__(self, kind, arrays):
        self.kind, self.arrays, self.n = kind, list(arrays), len(arrays)
        if kind == "scatter":
            self.out_shape = [jax.ShapeDtypeStruct(a.shape, a.dtype) for a in arrays]
        else:
            self.out_shape = [jax.ShapeDtypeStruct((N_DEV,) + a.shape, a.dtype) for a in arrays]
        self.scratch = [pltpu.SemaphoreType.DMA((N_PEERS * self.n,)), pltpu.SemaphoreType.DMA((N_PEERS * self.n,)),
                        pltpu.SemaphoreType.DMA((self.n,))]

    def _copies(self, in_refs, out_refs, sems):
        send_sems, recv_sems, local_sems = sems
        x, y, c = _my_place()
        me = 4 * x + 2 * y + c
        mine, sends, recvs = [], [], []
        for t in range(self.n):
            src_of = (lambda idx, t=t: in_refs[t].at[idx]) if self.kind == "scatter" else (lambda idx, t=t: in_refs[t])
            mine.append(pltpu.make_async_copy(src_of(me), out_refs[t].at[me], local_sems.at[t]))
            for m in range(1, N_DEV):
                px, py, pc = _flip(x, m & 4), _flip(y, m & 2), _flip(c, m & 1)
                pidx = 4 * px + 2 * py + pc
                sem = dict(send_sem=send_sems.at[N_PEERS * t + m - 1], recv_sem=recv_sems.at[N_PEERS * t + m - 1], device_id_type=MESH_T)
                sends.append(pltpu.make_async_remote_copy(src_ref=src_of(pidx), dst_ref=out_refs[t].at[me], device_id=(px, py, pc), **sem))
                recvs.append(pltpu.make_async_remote_copy(src_ref=src_of(pidx), dst_ref=out_refs[t].at[pidx], device_id=(x, y, c), **sem))
        return mine, sends, recvs

    def start(self, in_refs, out_refs, sems):
        mine, sends, _ = self._copies(in_refs, out_refs, sems)
        for cp in mine + sends:
            cp.start()

    def middle(self, in_refs, out_refs, sems):
        pass

    def finish(self, in_refs, out_refs, sems):
        mine, sends, recvs = self._copies(in_refs, out_refs, sems)
        for cp in recvs:
            cp.wait_recv()
        for cp in sends:
            cp.wait_send()
        for cp in mine:
            cp.wait()


class _TwoLevelGather:
    def __init__(self, arrays):
        self.arrays, self.n = list(arrays), len(arrays)
        self.out_shape = [jax.ShapeDtypeStruct((N_DEV,) + a.shape, a.dtype) for a in arrays]
        self.scratch = [pltpu.SemaphoreType.DMA((N_PEERS * self.n,)), pltpu.SemaphoreType.DMA((N_PEERS * self.n,)),
                        pltpu.SemaphoreType.DMA((self.n,))]

    def _place(self):
        x, y, c = _my_place()
        return (x, y, c), (x, y, 1 - c), [(1 - x, y), (x, 1 - y), (1 - x, 1 - y)]

    def _copy(self, in_refs, out_refs, sems, t, k, block, to, own=False):
        slab = out_refs[t].at[4 * block[0] + 2 * block[1] + block[2]]
        return pltpu.make_async_remote_copy(
            src_ref=in_refs[t] if own else slab, dst_ref=slab, send_sem=sems[0].at[N_PEERS * t + k],
            recv_sem=sems[1].at[N_PEERS * t + k], device_id=to, device_id_type=MESH_T)

    def _mine(self, in_refs, out_refs, sems):
        x, y, c = _my_place()
        return [pltpu.make_async_copy(in_refs[t], out_refs[t].at[4 * x + 2 * y + c], sems[2].at[t]) for t in range(self.n)]

    def _first(self, in_refs, out_refs, sems):
        me, sibling, chips = self._place()
        cps = []
        for t in range(self.n):
            cps.append(self._copy(in_refs, out_refs, sems, t, 0, me, sibling, own=True))
            cps += [self._copy(in_refs, out_refs, sems, t, 1 + j, me, (*chip, me[2]), own=True) for j, chip in enumerate(chips)]
        return cps

    def _passed(self, in_refs, out_refs, sems):
        me, sibling, chips = self._place()
        return [self._copy(in_refs, out_refs, sems, t, 4 + j, (*chip, me[2]), sibling) for j, chip in enumerate(chips) for t in range(self.n)]

    def start(self, in_refs, out_refs, sems):
        for cp in self._mine(in_refs, out_refs, sems) + self._first(in_refs, out_refs, sems):
            cp.start()

    def middle(self, in_refs, out_refs, sems):
        me, _, chips = self._place()
        for j, chip in enumerate(chips):
            for t in range(self.n):
                self._copy(in_refs, out_refs, sems, t, 1 + j, (*chip, me[2]), me).wait_recv()
        for cp in self._passed(in_refs, out_refs, sems):
            cp.start()

    def finish(self, in_refs, out_refs, sems):
        me, sibling, chips = self._place()
        for t in range(self.n):
            self._copy(in_refs, out_refs, sems, t, 0, sibling, me).wait_recv()
            for j, chip in enumerate(chips):
                self._copy(in_refs, out_refs, sems, t, 4 + j, (*chip, 1 - me[2]), me).wait_recv()
        for cp in self._first(in_refs, out_refs, sems) + self._passed(in_refs, out_refs, sems):
            cp.wait_send()
        for cp in self._mine(in_refs, out_refs, sems):
            cp.wait()


def _split_side(refs, n_main_in, n_main_out, n_main_scratch, side):
    ns = side.n if side is not None else 0
    i = 0
    main_in = refs[i:i + n_main_in]; i += n_main_in
    side_in = refs[i:i + ns]; i += ns
    main_out = refs[i:i + n_main_out]; i += n_main_out
    side_out = refs[i:i + ns]; i += ns
    main_scr = refs[i:i + n_main_scratch]; i += n_main_scratch
    return main_in, side_in, main_out, side_out, main_scr, refs[i:]


def _all_to_all(name, gs):
    side = _Direct("scatter", gs)

    def body(*refs):
        _, side_in, _, side_out, _, sems = _split_side(refs, 0, 0, 0, side)
        side.start(side_in, side_out, sems)
        side.finish(side_in, side_out, sems)

    return pl.pallas_call(
        body, name=name, out_shape=side.out_shape,
        in_specs=[_ANY] * side.n, out_specs=[_ANY] * side.n, scratch_shapes=side.scratch,
    )(*gs)


def _unblock(name, blocks, layer, n_out):
    _, _, K, nb = blocks.shape
    tk = _tile(K, 256, 16)

    def body(b_ref, o_ref):
        for j in range(N_DEV):
            o_ref[:, j * nb:(j + 1) * nb] = b_ref[j]
        if n_out > N_DEV * nb:
            o_ref[:, N_DEV * nb:n_out] = jnp.zeros((tk, n_out - N_DEV * nb), o_ref.dtype)

    return pl.pallas_call(
        body, name=name, grid=(K // tk,),
        in_specs=[pl.BlockSpec((N_DEV, None, tk, nb), lambda i: (0, layer, i, 0))],
        out_specs=pl.BlockSpec((tk, n_out), lambda i: (i, 0)),
        out_shape=jax.ShapeDtypeStruct((K, n_out), blocks.dtype),
        compiler_params=_cp(("parallel",)),
    )(blocks)


def _block(name, full, nb, dtype):
    K, N = full.shape
    tk = _tile(K, 256, 16)

    def body(f_ref, o_ref):
        for j in range(N_DEV):
            o_ref[j] = f_ref[:, j * nb:(j + 1) * nb].astype(o_ref.dtype)

    return pl.pallas_call(
        body, name=name, grid=(K // tk,),
        in_specs=[pl.BlockSpec((tk, N), lambda i: (i, 0))],
        out_specs=pl.BlockSpec((N_DEV, tk, nb), lambda i: (0, i, 0)),
        out_shape=jax.ShapeDtypeStruct((N_DEV, K, nb), dtype),
        compiler_params=_cp(("parallel",)),
    )(full)


ADAM_BLOCK_ELEMS = 256 * 1024
GRAD_WIRE = jnp.bfloat16


def _adamw(name, parts, w, m, v, layer=0):
    shape = w.shape[1:]
    cols = shape[-1]
    rows = math.prod(shape[:-1])
    nl = w.shape[0]
    parts, w, m, v = parts.reshape(N_DEV, rows, cols), w.reshape(nl, rows, cols), m.reshape(nl, rows, cols), v.reshape(nl, rows, cols)
    tr = _tile(rows, max(16, ADAM_BLOCK_ELEMS // cols), 16)

    def body(p_ref, w_ref, m_ref, v_ref, g_ref, d_ref, mo_ref, vo_ref):
        g = p_ref[0].astype(F32)
        for s in range(1, N_DEV):
            g = g + p_ref[s].astype(F32)
        mn = ADAM_B1 * m_ref[...] + (1.0 - ADAM_B1) * g
        vn = ADAM_B2 * v_ref[...] + (1.0 - ADAM_B2) * jnp.square(g)
        m_hat = mn / (1.0 - ADAM_B1 ** ADAM_STEP)
        v_hat = vn / (1.0 - ADAM_B2 ** ADAM_STEP)
        g_ref[...] = g
        d_ref[...] = -ADAM_LR * (m_hat / (jnp.sqrt(v_hat) + ADAM_EPS) + ADAM_WD * w_ref[...])
        mo_ref[...] = mn
        vo_ref[...] = vn

    blk = pl.BlockSpec((tr, cols), lambda i: (i, 0))
    lblk = pl.BlockSpec((None, tr, cols), lambda i: (layer, i, 0))
    shp = jax.ShapeDtypeStruct((rows, cols), F32)
    res = pl.pallas_call(
        body, name=name, grid=(rows // tr,),
        in_specs=[pl.BlockSpec((N_DEV, tr, cols), lambda i: (0, i, 0)), lblk, lblk, lblk],
        out_specs=[blk, blk, blk, blk], out_shape=[shp, shp, shp, shp],
        compiler_params=_cp(("parallel",)),
    )(parts, w, m, v)
    return [r.reshape(shape) for r in res]


SHARDED = dict(xa_wq=1, xa_wkv=1, xa_wo=2, w_gate=2, w_up=2, w_down=1, e_w_in=2, e_w_out=1, o_w_in=2, o_s_pool=1, o_conv_w=2, o_w_out=1)
LAYER_SPLIT = ("xa_wq", "xa_wkv", "xa_wo", "w_gate", "w_up", "w_down")
ELEMENTWISE_SHARDED = ("o_s_pool", "o_conv_w")
REPLICATED = ("g_mix", "g_xa", "g_mem", "xa_gq", "xa_gk", "g_ffn", "e_b_f", "e_g_v", "e_w_s", "e_b_s", "e_g_qn", "e_g_kn", "o_w_pool")
WEIGHTS = ("g_mix", "g_xa", "g_mem", "xa_wq", "xa_wkv", "xa_wo", "xa_gq", "xa_gk", "g_ffn", "w_gate", "w_up", "w_down", "e_w_in", "e_b_f",
           "e_g_v", "e_w_s", "e_b_s", "e_g_qn", "e_g_kn", "e_w_out", "o_w_in", "o_w_pool", "o_s_pool", "o_conv_w", "o_w_out")


def _rows_for(n, mult):
    return -(-n // (PACK_W * mult)) * mult


def _pack(arrs, rows, dtype):
    flat = jnp.concatenate([a.reshape(-1).astype(dtype) for a in arrs])
    return jnp.pad(flat, (0, rows * PACK_W - flat.shape[0])).reshape(rows, PACK_W)


def _unpack(slab, shapes):
    flat, out, off = slab.reshape(-1), [], 0
    for s in shapes:
        n = math.prod(s)
        out.append(flat[off:off + n].reshape(s))
        off += n
    return out


def _to_blocks(full, axis):
    s = full.shape
    return jnp.moveaxis(full.reshape(s[:axis] + (N_DEV, s[axis] // N_DEV) + s[axis + 1:]), axis, 0)


def _from_blocks(blocks, axis):
    b = jnp.moveaxis(blocks, 0, axis)
    s = b.shape
    return b.reshape(s[:axis] + (s[axis] * s[axis + 1],) + s[axis + 2:])


def kernel(x, mem, g_mix, g_xa, g_mem, xa_wq, xa_wkv, xa_wo, xa_gq, xa_gk, g_ffn, w_gate, w_up, w_down, e_w_in, e_b_f, e_g_v, e_w_s, e_b_s, e_g_qn, e_g_kn, e_w_out, o_w_in, o_w_pool, o_s_pool, o_conv_w, o_w_out, loss_target, m_g_mix, m_g_xa, m_g_mem, m_xa_wq, m_xa_wkv, m_xa_wo, m_xa_gq, m_xa_gk, m_g_ffn, m_w_gate, m_w_up, m_w_down, m_e_w_in, m_e_b_f, m_e_g_v, m_e_w_s, m_e_b_s, m_e_g_qn, m_e_g_kn, m_e_w_out, m_o_w_in, m_o_w_pool, m_o_s_pool, m_o_conv_w, m_o_w_out, v_g_mix, v_g_xa, v_g_mem, v_xa_wq, v_xa_wkv, v_xa_wo, v_xa_gq, v_xa_gk, v_g_ffn, v_w_gate, v_w_up, v_w_down, v_e_w_in, v_e_b_f, v_e_g_v, v_e_w_s, v_e_b_s, v_e_g_qn, v_e_g_kn, v_e_w_out, v_o_w_in, v_o_w_pool, v_o_s_pool, v_o_conv_w, v_o_w_out):
    args = dict(locals())
    Wt = {n: args[n] for n in WEIGHTS}
    Mo = {n: args["m_" + n] for n in WEIGHTS}
    Vo = {n: args["v_" + n] for n in WEIGHTS}
    rep_names = REPLICATED
    rep_shapes = [Wt[n].shape for n in rep_names]
    rep_rows = _rows_for(sum(math.prod(s) for s in rep_shapes) + 1, 16)
    layer_names = [LAYER_SPLIT + ("e_w_in", "e_w_out"), LAYER_SPLIT + ("o_w_in", "o_w_out", "o_s_pool", "o_conv_w")]

    def shard_of(n, l, src=Wt):
        return src[n][l:l + 1] if n in LAYER_SPLIT else src[n]

    def to_wire(n, a):
        return a if n in ELEMENTWISE_SHARDED else _mx(a)

    def full_weight(n, l, gn):
        if SHARDED[n] == 1 or n in ELEMENTWISE_SHARDED:
            return _from_blocks(gn, SHARDED[n])[0]
        return _unblock(f"unblock_{n}{l}", gn, 0, EVEN_IN_PAD if n == "e_w_in" else N_DEV * gn.shape[-1])

    def grad_slabs(n, l, gl):
        if n in ELEMENTWISE_SHARDED:
            return _to_blocks(gl.reshape((1,) + Wt[n].shape[1:-1] + (-1,)), SHARDED[n])
        if SHARDED[n] == 1:
            return _to_blocks(gl[None], 1).astype(GRAD_WIRE)
        return _block(f"block_{n}{l}", gl, Wt[n].shape[-1], GRAD_WIRE)[:, None]

    xk, mem2 = x[0], mem[0]
    later = [(n, 0) for n in layer_names[0] if n != "e_w_in"] + [(n, 1) for n in layer_names[1]]
    w_in0 = full_weight("e_w_in", 0, _all_gather("gather_e_w_in", [to_wire("e_w_in", Wt["e_w_in"])])[0])
    side = _TwoLevelGather([to_wire(n, shard_of(n, l)) for n, l in later])
    P = {n: Wt[n] for n in rep_names}
    P["e_w_in"] = {0: w_in0}

    def on_side(gathered):
        for (n, l), gn in zip(later, gathered):
            P.setdefault(n, {})[l if n in LAYER_SPLIT else 0] = full_weight(n, l, gn)

    xk, sv0, W0 = _layer_fwd(0, xk, mem2, P, side=side, on_side=on_side)
    xk, sv1, W1 = _layer_fwd(1, xk, mem2, P)

    g, loss_row = _loss_grad("loss_grad", xk, loss_target[0])
    g, G1, _ = _layer_bwd(1, g, mem2, W1, sv1)
    slabs1 = [grad_slabs(n, 1, G1[n]) for n in layer_names[1]]
    g, G0, beside = _layer_bwd(0, g, mem2, W0, sv0,
                               side_fn=lambda G: _Direct("scatter", slabs1 + [grad_slabs(n, 0, G[n]) for n in LAYER_SPLIT]))
    parts1, parts0a = beside[:len(slabs1)], beside[len(slabs1):]
    grad_x = g[None]
    parts0b = _all_to_all("exchange_grads_mixer0", [grad_slabs(n, 0, G0[n]) for n in ("e_w_in", "e_w_out")])
    parts = {(n, 1): p for n, p in zip(layer_names[1], parts1)}
    parts.update({(n, 0): p for n, p in zip(LAYER_SPLIT, parts0a)})
    parts.update({(n, 0): p for n, p in zip(("e_w_in", "e_w_out"), parts0b)})
    G = {n: {0: G0[n]} for n in G0}
    for n in G1:
        G.setdefault(n, {})[1] = G1[n]
    rep_g = [jnp.stack([G[n][l] for l in sorted(G[n])]).reshape(Wt[n].shape) for n in rep_names]
    rep_parts = _all_gather("gather_small_grads", [_pack(rep_g + [jnp.sum(loss_row).reshape(1)], rep_rows, F32)])[0]

    outs = {}
    for n in SHARDED:
        per_layer = [_adamw(f"adamw_{n}{l}", parts[(n, l if n in LAYER_SPLIT else (0 if n.startswith("e_") else 1))], Wt[n], Mo[n], Vo[n], layer=l)
                     for l in range(Wt[n].shape[0])]
        for k, kind in enumerate(("grad", "delta", "new_m", "new_v")):
            outs[kind + "_" + n] = jnp.stack([r[k] for r in per_layer])
    res = _adamw("adamw_replicated", rep_parts, _pack([Wt[n] for n in rep_names], rep_rows, F32)[None],
                 _pack([Mo[n] for n in rep_names], rep_rows, F32)[None], _pack([Vo[n] for n in rep_names], rep_rows, F32)[None])
    for kind, slab in zip(("grad", "delta", "new_m", "new_v"), res):
        for n, a in zip(rep_names, _unpack(slab, rep_shapes)):
            outs[kind + "_" + n] = a
    loss = res[0].reshape(-1)[sum(math.prod(s) for s in rep_shapes)]
    return (loss, grad_x, *[outs[k + "_" + n] for k in ("grad", "delta", "new_m", "new_v") for n in WEIGHTS])
```

```python
import functools
import math

import numpy as np
import jax
import jax.numpy as jnp
from jax import lax
from jax.experimental import pallas as pl
from jax.experimental.pallas import tpu as pltpu

F32 = jnp.float32
BF16 = jnp.bfloat16
_MXU = jnp.bfloat16

D_MODEL = 1024
N_DEV = 8
LANES = 128
SUBLANES = 8
HALO = 16
CHUNK = 64
A_GROUPS, A_GROUP_DIM, A_WIDTH, A_BLOCK = 4, 128, 512, 128
FOX_HEADS, FOX_HEAD_DIM, FOX_WIDTH = 8, 64, 512
FOX_TILE = 512
POOL_WINDOWS = (2, 4, 8, 16)
POOL_WIDTH, CONV_WIDTH, CONV_K = 512, 512, 3
EVEN_IN, EVEN_IN_PAD, ODD_IN = 2568, 2688, 2048
XA_HEADS, XA_HEAD_DIM, XA_WIDTH = 4, 128, 512
D_FF = 2816
EPS = 1e-6
ADAM_LR, ADAM_B1, ADAM_B2, ADAM_EPS, ADAM_WD, ADAM_STEP = 0.001, 0.9, 0.999, 1e-08, 0.01, 10
VMEM_LIMIT = 48 * 1024 * 1024


def _cp(sem):
    return pltpu.CompilerParams(dimension_semantics=sem, vmem_limit_bytes=VMEM_LIMIT)


def _tile(n, cap, q):
    best = None
    for d in range(q, min(n, cap) + 1, q):
        if n % d == 0:
            best = d
    return best if best is not None else n


def _mx(a):
    return a.astype(_MXU)


def _dot(a, b):
    return jnp.dot(_mx(a), _mx(b), preferred_element_type=F32)


def _dot_nt(a, b):
    return lax.dot_general(_mx(a), _mx(b), (((1,), (1,)), ((), ())), preferred_element_type=F32)


def _dot_tn(a, b):
    return lax.dot_general(_mx(a), _mx(b), (((0,), (0,)), ((), ())), preferred_element_type=F32)


def _dot_f32(a, b):
    return jnp.dot(a, b, precision=lax.Precision.HIGHEST, preferred_element_type=F32)


def _lane(shape):
    return lax.broadcasted_iota(jnp.int32, shape, len(shape) - 1)


def _row(shape):
    return lax.broadcasted_iota(jnp.int32, shape, len(shape) - 2)


def _mm(name, pairs, mode, out_dtype=F32, residual=None, tm_cap=512, tn_cap=1536, tk_cap=2048):
    a0, b0 = pairs[0]
    if mode == "nn":
        (M, K), N = a0.shape, b0.shape[1]
    elif mode == "nt":
        (M, K), N = a0.shape, b0.shape[0]
    else:
        (K, M), N = a0.shape, b0.shape[1]
    if mode == "tn":
        tm, tk = _tile(M, 1408, LANES), _tile(K, 512, 16)
    else:
        tm, tk = _tile(M, tm_cap, 16), _tile(K, tk_cap, LANES)
    tn = _tile(N, tn_cap, LANES)
    nk = K // tk
    npairs = len(pairs)
    dot = {"nn": _dot, "nt": _dot_nt, "tn": _dot_tn}[mode]

    def body(*refs):
        ab = refs[:2 * npairs]
        res_ref = refs[2 * npairs] if residual is not None else None
        o_ref, acc = refs[-2], refs[-1]
        k = pl.program_id(2)

        @pl.when(k == 0)
        def _():
            acc[...] = jnp.zeros_like(acc)

        for p in range(npairs):
            acc[...] += dot(ab[2 * p][...], ab[2 * p + 1][...])

        @pl.when(k == nk - 1)
        def _():
            out = acc[...]
            if res_ref is not None:
                out = out + res_ref[...]
            o_ref[...] = out.astype(o_ref.dtype)

    if mode == "nn":
        a_spec = pl.BlockSpec((tm, tk), lambda j, i, k: (i, k))
        b_spec = pl.BlockSpec((tk, tn), lambda j, i, k: (k, j))
    elif mode == "nt":
        a_spec = pl.BlockSpec((tm, tk), lambda j, i, k: (i, k))
        b_spec = pl.BlockSpec((tn, tk), lambda j, i, k: (j, k))
    else:
        a_spec = pl.BlockSpec((tk, tm), lambda j, i, k: (k, i))
        b_spec = pl.BlockSpec((tk, tn), lambda j, i, k: (k, j))
    o_spec = pl.BlockSpec((tm, tn), lambda j, i, k: (i, j))
    in_specs, args = [], []
    for a, b in pairs:
        in_specs += [a_spec, b_spec]
        args += [a, b]
    if residual is not None:
        in_specs.append(o_spec)
        args.append(residual)
    return pl.pallas_call(
        body, name=name, grid=(N // tn, M // tm, nk),
        in_specs=in_specs, out_specs=o_spec,
        out_shape=jax.ShapeDtypeStruct((M, N), out_dtype),
        scratch_shapes=[pltpu.VMEM((tm, tn), F32)],
        compiler_params=_cp(("parallel", "parallel", "arbitrary")),
    )(*args)


def _rms_rows(x):
    return lax.rsqrt(jnp.mean(x * x, axis=-1, keepdims=True) + EPS)


def _norm_fwd(name, x, g):
    T, Dm = x.shape
    tm = _tile(T, 512, 16)

    def body(x_ref, g_ref, o_ref):
        xv = x_ref[...]
        o_ref[...] = ((xv * _rms_rows(xv)) * g_ref[...]).astype(o_ref.dtype)

    return pl.pallas_call(
        body, name=name, grid=(T // tm,),
        in_specs=[pl.BlockSpec((tm, Dm), lambda i: (i, 0)), pl.BlockSpec((1, Dm), lambda i: (0, 0))],
        out_specs=pl.BlockSpec((tm, Dm), lambda i: (i, 0)),
        out_shape=jax.ShapeDtypeStruct((T, Dm), _MXU),
        compiler_params=_cp(("parallel",)),
    )(x, g.reshape(1, Dm))


def _norm_bwd(name, dh, x, g, res):
    T, Dm = x.shape
    tm = _tile(T, 512, 8)

    def body(dh_ref, x_ref, g_ref, res_ref, dx_ref, dg_ref):
        xv, dhv = x_ref[...], dh_ref[...]
        r = _rms_rows(xv)
        xhat = xv * r
        dxhat = dhv * g_ref[...]
        dx_ref[...] = res_ref[...] + r * (dxhat - xhat * jnp.mean(dxhat * xhat, axis=-1, keepdims=True))

        @pl.when(pl.program_id(0) == 0)
        def _():
            dg_ref[...] = jnp.zeros_like(dg_ref)

        dg_ref[...] += jnp.sum(dhv * xhat, axis=0, keepdims=True)

    blk = pl.BlockSpec((tm, Dm), lambda i: (i, 0))
    vec = pl.BlockSpec((1, Dm), lambda i: (0, 0))
    return pl.pallas_call(
        body, name=name, grid=(T // tm,),
        in_specs=[blk, blk, vec, blk], out_specs=[blk, vec],
        out_shape=[jax.ShapeDtypeStruct((T, Dm), F32), jax.ShapeDtypeStruct((1, Dm), F32)],
        compiler_params=_cp(("arbitrary",)),
    )(dh, x, g.reshape(1, Dm), res)


def _gelu(x):
    return jax.nn.gelu(x)


def _gelu_grad(x):
    c0, c1 = math.sqrt(2.0 / math.pi), 0.044715
    t = jnp.tanh(c0 * (x + c1 * x * x * x))
    return 0.5 * (1.0 + t) + 0.5 * x * (1.0 - t * t) * c0 * (1.0 + 3.0 * c1 * x * x)


def _head64_rms(x):
    lo = _lane(x.shape) < FOX_HEAD_DIM
    xx = x * x
    sa = jnp.sum(jnp.where(lo, xx, 0.0), axis=-1, keepdims=True)
    sb = jnp.sum(jnp.where(lo, 0.0, xx), axis=-1, keepdims=True)
    inv = 1.0 / FOX_HEAD_DIM
    return jnp.where(lo, lax.rsqrt(sa * inv + EPS), lax.rsqrt(sb * inv + EPS))


def _head64_mean(x):
    lo = _lane(x.shape) < FOX_HEAD_DIM
    sa = jnp.sum(jnp.where(lo, x, 0.0), axis=-1, keepdims=True)
    sb = jnp.sum(jnp.where(lo, 0.0, x), axis=-1, keepdims=True)
    return jnp.where(lo, sa, sb) * (1.0 / FOX_HEAD_DIM)


def _masked_ws(ws_ref, g):
    w = ws_ref[g]
    keep = (_row(w.shape) // CHUNK) >= (_lane(w.shape) // CHUNK)
    return jnp.where(keep, w, 0.0)


def _even_pre(name, z, g_v, w_s, bs_cols, g_qn2, g_kn2, b_f_pad):
    T = z.shape[0]
    tm = _tile(T, 256, A_BLOCK)
    nb = tm // A_BLOCK

    def body(z_ref, gv_ref, ws_ref, bs_ref, gq_ref, gk_ref, bf_ref, ya_ref, q_ref, k_ref, v_ref, c_ref, ct_ref, carry):
        i = pl.program_id(0)

        @pl.when(i == 0)
        def _():
            carry[...] = jnp.zeros_like(carry)

        wm = [_mx(_masked_ws(ws_ref, g)) for g in range(A_GROUPS)]
        for g in range(A_GROUPS):
            sl = slice(g * A_GROUP_DIM, (g + 1) * A_GROUP_DIM)
            u = _gelu(z_ref[:, sl])
            v = _gelu(z_ref[:, A_WIDTH + g * A_GROUP_DIM:A_WIDTH + (g + 1) * A_GROUP_DIM])
            vn = _mx((v * _rms_rows(v)) * gv_ref[:, sl])
            for n in range(nb):
                rows = slice(n * A_BLOCK, (n + 1) * A_BLOCK)
                s = jnp.dot(wm[g], vn[rows], preferred_element_type=F32) + bs_ref[:, g:g + 1]
                ya_ref[rows, sl] = (u[rows] * s).astype(ya_ref.dtype)
        o = 2 * A_WIDTH
        for j in range(FOX_WIDTH // LANES):
            sl = slice(j * LANES, (j + 1) * LANES)
            q = z_ref[:, o + j * LANES:o + (j + 1) * LANES]
            q_ref[:, sl] = (((q * _head64_rms(q)) * gq_ref[...]) * (1.0 / math.sqrt(FOX_HEAD_DIM))).astype(q_ref.dtype)
            k = z_ref[:, o + FOX_WIDTH + j * LANES:o + FOX_WIDTH + (j + 1) * LANES]
            k_ref[:, sl] = ((k * _head64_rms(k)) * gk_ref[...]).astype(k_ref.dtype)
            v_ref[:, sl] = z_ref[:, o + 2 * FOX_WIDTH + j * LANES:o + 2 * FOX_WIDTH + (j + 1) * LANES].astype(v_ref.dtype)
        zf = z_ref[:, o + 3 * FOX_WIDTH:o + 3 * FOX_WIDTH + LANES]
        logf = jnp.where(_lane(zf.shape) < FOX_HEADS, jax.nn.log_sigmoid(zf + bf_ref[...]), 0.0)
        tri = (_row((tm, tm)) >= _lane((tm, tm))).astype(F32)
        c = _dot_f32(tri, logf) + carry[0:1, :]
        c_ref[...] = c
        ct_ref[...] = c.T[0:SUBLANES, :]
        carry[0:1, :] = c[tm - 1:tm, :]

    wide = lambda n: pl.BlockSpec((tm, n), lambda i: (i, 0))
    full = lambda s: pl.BlockSpec(s, lambda i: tuple(0 for _ in s))
    out512 = jax.ShapeDtypeStruct((T, 512), _MXU)
    return pl.pallas_call(
        body, name=name, grid=(T // tm,),
        in_specs=[wide(EVEN_IN_PAD), full((1, A_WIDTH)), full((A_GROUPS, A_BLOCK, A_BLOCK)), full((A_BLOCK, LANES)),
                  full((1, LANES)), full((1, LANES)), full((1, LANES))],
        out_specs=[wide(512), wide(512), wide(512), wide(512), wide(LANES), pl.BlockSpec((SUBLANES, tm), lambda i: (0, i))],
        out_shape=[out512, out512, out512, out512, jax.ShapeDtypeStruct((T, LANES), F32), jax.ShapeDtypeStruct((SUBLANES, T), F32)],
        scratch_shapes=[pltpu.VMEM((SUBLANES, LANES), F32)],
        compiler_params=_cp(("arbitrary",)),
    )(z, g_v, w_s, bs_cols, g_qn2, g_kn2, b_f_pad)


def _tri_steps(n, by_rows):
    if by_rows:
        pairs = [(q, k) for q in range(n) for k in range(q + 1)]
    else:
        pairs = [(q, k) for k in range(n) for q in range(k, n)]
    return (jnp.asarray(np.array([p[0] for p in pairs], np.int32)), jnp.asarray(np.array([p[1] for p in pairs], np.int32)))


def _fox_logits(q, k, c_q0, c_row, diag, tq):
    s = _dot_nt(q, k) + (c_q0 - c_row)
    if diag:
        s = jnp.where(_row((tq, tq)) >= _lane((tq, tq)), s, -jnp.inf)
    return s


def _on_diag_or_not(qi, ki, step):
    @pl.when(qi == ki)
    def _():
        step(True)

    @pl.when(qi != ki)
    def _():
        step(False)


def _side_specs(side):
    if side is None:
        return [], [], [], []
    return [pl.BlockSpec(memory_space=pl.ANY)] * side.n, list(side.arrays), list(side.out_shape), list(side.scratch)


def _side_run(side, s_id, nsteps, side_in, side_out, sems):
    if side is None:
        return
    mid = nsteps * 3 // 4
    stride = max(1, mid // side.n)
    for t in range(side.n):
        @pl.when(s_id == min(t * stride, nsteps - 1))
        def _(t=t):
            side.start(side_in, side_out, sems, only=t)

    @pl.when(s_id == mid)
    def _():
        side.middle(side_in, side_out, sems)

    @pl.when(s_id == nsteps - 1)
    def _():
        side.finish(side_in, side_out, sems)


def _fox_fwd(name, qn, kn, vv, c, ct, side=None):
    T = qn.shape[0]
    tq = _tile(T, FOX_TILE, LANES)
    nblk = T // tq
    qs, ks = _tri_steps(nblk, True)
    nsteps = int(qs.shape[0])
    npair = FOX_WIDTH // LANES
    s_specs, s_args, s_shapes, s_scratch = _side_specs(side)

    def body(qs_ref, ks_ref, *refs):
        main_in, side_in, main_out, side_out, main_scr, sems = _split_side(refs, 5, 3, 3, side)
        q_ref, k_ref, v_ref, c_ref, ct_ref = main_in
        yb_ref, yf_ref, lse_ref = main_out
        acc, m_s, l_s = main_scr
        s_id = pl.program_id(0)
        qi, ki = qs_ref[s_id], ks_ref[s_id]
        _side_run(side, s_id, nsteps, side_in, side_out, sems)

        @pl.when(ki == 0)
        def _():
            acc[...] = jnp.zeros_like(acc)
            m_s[...] = jnp.full_like(m_s, -jnp.inf)
            l_s[...] = jnp.zeros_like(l_s)

        lo = _lane((tq, LANES)) < FOX_HEAD_DIM

        def step(diag):
            for j in range(npair):
                sl = slice(j * LANES, (j + 1) * LANES)
                q2, k2, v2 = q_ref[:, sl], k_ref[:, sl], v_ref[:, sl]
                outs = []
                for e in range(2):
                    h = 2 * j + e
                    qh = jnp.where(lo if e == 0 else ~lo, q2, jnp.zeros_like(q2))
                    s = _fox_logits(qh, k2, c_ref[0:1, h:h + 1], ct_ref[h:h + 1, :], diag, tq)
                    m_old = m_s[h]
                    m_new = jnp.maximum(m_old, jnp.max(s, axis=-1, keepdims=True))
                    alpha = jnp.exp(m_old - m_new)
                    p = jnp.exp(s - m_new)
                    l_s[h] = alpha * l_s[h] + jnp.sum(p, axis=-1, keepdims=True)
                    m_s[h] = m_new
                    outs.append((alpha, _dot(p, v2)))
                alpha2 = jnp.where(lo, outs[0][0], outs[1][0])
                acc[:, sl] = alpha2 * acc[:, sl] + jnp.where(lo, outs[0][1], outs[1][1])

        _on_diag_or_not(qi, ki, step)

        @pl.when(ki == qi)
        def _():
            lse = jnp.zeros((tq, LANES), F32)
            for j in range(npair):
                sl = slice(j * LANES, (j + 1) * LANES)
                inv = jnp.where(lo, 1.0 / l_s[2 * j], 1.0 / l_s[2 * j + 1])
                y = acc[:, sl] * inv
                yf_ref[:, sl] = y
                yb_ref[:, sl] = y.astype(yb_ref.dtype)
            for h in range(FOX_HEADS):
                lse = jnp.where(_lane((tq, LANES)) == h, m_s[h] + jnp.log(l_s[h]), lse)
            lse_ref[...] = lse

    qmap = lambda s, qs_r, ks_r: (qs_r[s], 0)
    kmap = lambda s, qs_r, ks_r: (ks_r[s], 0)
    grid_spec = pltpu.PrefetchScalarGridSpec(
        num_scalar_prefetch=2, grid=(nsteps,),
        in_specs=[pl.BlockSpec((tq, 512), qmap), pl.BlockSpec((tq, 512), kmap), pl.BlockSpec((tq, 512), kmap),
                  pl.BlockSpec((SUBLANES, LANES), lambda s, qs_r, ks_r: (qs_r[s] * (tq // SUBLANES), 0)),
                  pl.BlockSpec((SUBLANES, tq), lambda s, qs_r, ks_r: (0, ks_r[s]))] + s_specs,
        out_specs=[pl.BlockSpec((tq, 512), qmap), pl.BlockSpec((tq, 512), qmap), pl.BlockSpec((tq, LANES), qmap)] + s_specs,
        scratch_shapes=[pltpu.VMEM((tq, 512), F32), pltpu.VMEM((FOX_HEADS, tq, 1), F32), pltpu.VMEM((FOX_HEADS, tq, 1), F32)] + s_scratch,
    )
    res = pl.pallas_call(
        body, name=name, grid_spec=grid_spec,
        out_shape=[jax.ShapeDtypeStruct((T, 512), _MXU), jax.ShapeDtypeStruct((T, 512), F32), jax.ShapeDtypeStruct((T, LANES), F32)] + s_shapes,
        compiler_params=_cp(("arbitrary",)),
    )(qs, ks, qn, kn, vv, c, ct, *s_args)
    return res[:3], res[3:]


def _shift_down(ext, k, tm):
    return pltpu.roll(ext, k, axis=0)[HALO:HALO + tm]


def _pool_p(z_ext, g, tm, pos):
    w = POOL_WINDOWS[g]
    s, span = z_ext, 1
    while span < w:
        s = s + pltpu.roll(s, span, axis=0)
        span *= 2
    cnt = jnp.minimum(pos + 1, w).astype(F32)
    return s[HALO:HALO + tm] / cnt - z_ext[HALO:HALO + tm]


def _odd_fwd(name, z, w_pool, s_pool, conv_w8):
    T = z.shape[0]
    tm = _tile(T, 256, HALO)
    r = tm // HALO

    def body(z_ref, zb_ref, wp_ref, sp_ref, cw_ref, y_ref):
        i = pl.program_id(0)
        first = i == 0
        pos = i * tm + _row((tm, LANES))
        for g in range(len(POOL_WINDOWS)):
            sl = slice(g * LANES, (g + 1) * LANES)
            halo = jnp.where(first, 0.0, zb_ref[:, sl])
            z_ext = jnp.concatenate([halo, z_ref[:, sl]], axis=0)
            p = _pool_p(z_ext, g, tm, pos)
            y_ref[:, sl] = (_dot(p, wp_ref[g]) * sp_ref[:, sl]).astype(y_ref.dtype)
        for j in range(CONV_WIDTH // LANES):
            sl = slice(j * LANES, (j + 1) * LANES)
            hd = slice(POOL_WIDTH + j * LANES, POOL_WIDTH + (j + 1) * LANES)
            gb = slice(POOL_WIDTH + CONV_WIDTH + j * LANES, POOL_WIDTH + CONV_WIDTH + (j + 1) * LANES)
            gc = slice(POOL_WIDTH + 2 * CONV_WIDTH + j * LANES, POOL_WIDTH + 2 * CONV_WIDTH + (j + 1) * LANES)
            xg_b = jnp.where(first, 0.0, zb_ref[:, gc] * zb_ref[:, hd])
            xg = jnp.concatenate([xg_b, z_ref[:, gc] * z_ref[:, hd]], axis=0)
            conv = (cw_ref[0:1, sl] * _shift_down(xg, 2, tm) + cw_ref[1:2, sl] * _shift_down(xg, 1, tm)
                    + cw_ref[2:3, sl] * xg[HALO:HALO + tm])
            y_ref[:, POOL_WIDTH + j * LANES:POOL_WIDTH + (j + 1) * LANES] = (z_ref[:, gb] * conv).astype(y_ref.dtype)

    full = lambda s: pl.BlockSpec(s, lambda i: tuple(0 for _ in s))
    return pl.pallas_call(
        body, name=name, grid=(T // tm,),
        in_specs=[pl.BlockSpec((tm, ODD_IN), lambda i: (i, 0)),
                  pl.BlockSpec((HALO, ODD_IN), lambda i: (jnp.maximum(i * r - 1, 0), 0)),
                  full((4, LANES, LANES)), full((1, POOL_WIDTH)), full((SUBLANES, CONV_WIDTH))],
        out_specs=pl.BlockSpec((tm, 1024), lambda i: (i, 0)),
        out_shape=jax.ShapeDtypeStruct((T, 1024), _MXU),
        compiler_params=_cp(("parallel",)),
    )(z, z, w_pool, s_pool, conv_w8)


def _mem_fwd(name, mem, g_mem, wkv, gk):
    M = mem.shape[0]

    def body(mem_ref, g_ref, w_ref, gk_ref, m_ref, kv_ref, kn_ref, v_ref):
        x = mem_ref[...]
        m = ((x * _rms_rows(x)) * g_ref[...]).astype(m_ref.dtype)
        m_ref[...] = m
        kv = _dot(m, w_ref[...])
        kv_ref[...] = kv
        for h in range(XA_HEADS):
            sl = slice(h * LANES, (h + 1) * LANES)
            k = kv[:, sl]
            kn_ref[:, sl] = ((k * _rms_rows(k)) * gk_ref[...]).astype(kn_ref.dtype)
        v_ref[...] = kv[:, XA_WIDTH:].astype(v_ref.dtype)

    return pl.pallas_call(
        body, name=name,
        out_shape=[jax.ShapeDtypeStruct((M, D_MODEL), _MXU), jax.ShapeDtypeStruct((M, 2 * XA_WIDTH), F32),
                   jax.ShapeDtypeStruct((M, XA_WIDTH), _MXU), jax.ShapeDtypeStruct((M, XA_WIDTH), _MXU)],
        compiler_params=pltpu.CompilerParams(vmem_limit_bytes=VMEM_LIMIT),
    )(mem, g_mem.reshape(1, D_MODEL), wkv, gk.reshape(1, XA_HEAD_DIM))


def _xa_probs(qx, gq, kn_h):
    r = _rms_rows(qx)
    qhat = qx * r
    qn = qhat * gq
    s = _dot_nt(qn, kn_h) * (1.0 / math.sqrt(XA_HEAD_DIM))
    s = s - jnp.max(s, axis=-1, keepdims=True)
    e = jnp.exp(s)
    return qhat, r, qn, e / jnp.sum(e, axis=-1, keepdims=True)


def _xa_fwd(name, qx, gq, kn, vx):
    T, M = qx.shape[0], kn.shape[0]
    tm = _tile(T, 512, 16)

    def body(q_ref, gq_ref, k_ref, v_ref, o_ref):
        for h in range(XA_HEADS):
            sl = slice(h * LANES, (h + 1) * LANES)
            _, _, _, p = _xa_probs(q_ref[:, sl], gq_ref[...], k_ref[:, sl])
            o_ref[:, sl] = _dot(p, v_ref[:, sl]).astype(o_ref.dtype)

    full = lambda s: pl.BlockSpec(s, lambda i: tuple(0 for _ in s))
    return pl.pallas_call(
        body, name=name, grid=(T // tm,),
        in_specs=[pl.BlockSpec((tm, XA_WIDTH), lambda i: (i, 0)), full((1, LANES)), full((M, XA_WIDTH)), full((M, XA_WIDTH))],
        out_specs=pl.BlockSpec((tm, XA_WIDTH), lambda i: (i, 0)),
        out_shape=jax.ShapeDtypeStruct((T, XA_WIDTH), _MXU),
        compiler_params=_cp(("parallel",)),
    )(qx, gq.reshape(1, XA_HEAD_DIM), kn, vx)


def _ffn_up(name, h, wg, wu):
    T, K = h.shape
    N = wg.shape[1]
    tm, tn = _tile(T, 512, 16), _tile(N, 1536, LANES)

    def body(h_ref, wg_ref, wu_ref, a_ref, b_ref, s_ref):
        hv = h_ref[...]
        a, b = _dot(hv, wg_ref[...]), _dot(hv, wu_ref[...])
        a_ref[...] = a
        b_ref[...] = b
        s_ref[...] = (jax.nn.silu(a) * b).astype(s_ref.dtype)

    w_spec = pl.BlockSpec((K, tn), lambda j, i: (0, j))
    o_spec = pl.BlockSpec((tm, tn), lambda j, i: (i, j))
    return pl.pallas_call(
        body, name=name, grid=(N // tn, T // tm),
        in_specs=[pl.BlockSpec((tm, K), lambda j, i: (i, 0)), w_spec, w_spec],
        out_specs=[o_spec, o_spec, o_spec],
        out_shape=[jax.ShapeDtypeStruct((T, N), F32), jax.ShapeDtypeStruct((T, N), F32), jax.ShapeDtypeStruct((T, N), _MXU)],
        compiler_params=_cp(("parallel", "parallel")),
    )(h, wg, wu)


def _loss_grad(name, y, target):
    T, Dm = y.shape
    tm = _tile(T, 512, 8)

    def body(y_ref, t_ref, dy_ref, l_ref):
        e = y_ref[...] - t_ref[...]
        dy_ref[...] = e * (1.0 / Dm)

        @pl.when(pl.program_id(0) == 0)
        def _():
            l_ref[...] = jnp.zeros_like(l_ref)

        l_ref[...] += jnp.sum(e * e, axis=0, keepdims=True) * (0.5 / Dm)

    blk = pl.BlockSpec((tm, Dm), lambda i: (i, 0))
    return pl.pallas_call(
        body, name=name, grid=(T // tm,),
        in_specs=[blk, blk], out_specs=[blk, pl.BlockSpec((1, Dm), lambda i: (0, 0))],
        out_shape=[jax.ShapeDtypeStruct((T, Dm), F32), jax.ShapeDtypeStruct((1, Dm), F32)],
        compiler_params=_cp(("arbitrary",)),
    )(y, target)


def _pad_lanes(v, n=LANES):
    v = v.reshape(1, -1)
    return jnp.pad(v, ((0, 0), (0, n - v.shape[1])))


def _layer_fwd(l, x, mem, P, side=None, on_side=None):
    sv = {"x0": x}
    W = _layer_weights(l, P, "in")
    h1 = _norm_fwd(f"norm_mix{l}", x, W["g_mix"])
    sv["h1"] = h1
    if l % 2 == 0:
        z = _mm(f"e_in{l}", [(h1, W["w_in"])], "nn")
        ya, qn, kn, vv, c, ct = _even_pre(f"even_pre{l}", z, W["g_v"], W["w_s"], W["bs_cols"], W["g_qn2"], W["g_kn2"], W["b_f_pad"])
        (yb, ybf, lse), side_out = _fox_fwd(f"fox_fwd{l}", qn, kn, vv, c, ct, side=side)
        if on_side is not None:
            on_side(side_out)
        W.update(_layer_weights(l, P, "rest"))
        sv.update(z=z, ya=ya, qn=qn, kn=kn, vv=vv, c=c, ct=ct, yb=yb, ybf=ybf, lse=lse)
        x1 = _mm(f"e_out{l}", [(ya, W["w_out"][:A_WIDTH]), (yb, W["w_out"][A_WIDTH:])], "nn", residual=x)
    else:
        W.update(_layer_weights(l, P, "rest"))
        z = _mm(f"o_in{l}", [(h1, W["w_in"])], "nn")
        ycat = _odd_fwd(f"odd_fwd{l}", z, W["w_pool"], W["s_pool"], W["conv_w8"])
        sv.update(z=z, ycat=ycat)
        x1 = _mm(f"o_out{l}", [(ycat, W["w_out"])], "nn", residual=x)
    sv["x1"] = x1
    h2 = _norm_fwd(f"norm_xa{l}", x1, W["g_xa"])
    qx = _mm(f"xa_q{l}", [(h2, W["wq"])], "nn")
    m, kv, kn_x, vx = _mem_fwd(f"mem_fwd{l}", mem, W["g_mem"], W["wkv"], W["gk"])
    o = _xa_fwd(f"xa_fwd{l}", qx, W["gq"], kn_x, vx)
    x2 = _mm(f"xa_o{l}", [(o, W["wo"])], "nn", residual=x1)
    sv.update(h2=h2, qx=qx, m=m, kv=kv, kn_x=kn_x, vx=vx, o=o, x2=x2)
    h3 = _norm_fwd(f"norm_ffn{l}", x2, W["g_ffn"])
    a, b, s = _ffn_up(f"ffn_up{l}", h3, W["w_gate"], W["w_up"])
    x3 = _mm(f"ffn_down{l}", [(s, W["w_down"])], "nn", residual=x2)
    sv.update(h3=h3, a=a, b=b, s=s)
    return x3, sv, W


def _layer_weights(l, P, part):
    i = l // 2
    if part == "in":
        W = {"g_mix": P["g_mix"][l]}
        if l % 2 == 0:
            w_in = _mx(P["e_w_in"][i])
            if w_in.shape[1] < EVEN_IN_PAD:
                w_in = jnp.pad(w_in, ((0, 0), (0, EVEN_IN_PAD - w_in.shape[1])))
            W.update(w_in=w_in, g_v=P["e_g_v"][i].reshape(1, A_WIDTH), w_s=P["e_w_s"][i],
                     bs_cols=jnp.pad(P["e_b_s"][i].T, ((0, 0), (0, LANES - A_GROUPS))),
                     g_qn2=jnp.tile(P["e_g_qn"][i], 2).reshape(1, LANES), g_kn2=jnp.tile(P["e_g_kn"][i], 2).reshape(1, LANES),
                     b_f_pad=_pad_lanes(P["e_b_f"][i]))
        return W
    W = {k: P[k][l] for k in ("g_xa", "g_mem", "g_ffn")}
    W.update(wq=_mx(P["xa_wq"][l]), wkv=_mx(P["xa_wkv"][l]), wo=_mx(P["xa_wo"][l]), gq=P["xa_gq"][l], gk=P["xa_gk"][l],
             w_gate=_mx(P["w_gate"][l]), w_up=_mx(P["w_up"][l]), w_down=_mx(P["w_down"][l]))
    if l % 2 == 0:
        W.update(w_out=_mx(P["e_w_out"][i]))
    else:
        W.update(w_in=_mx(P["o_w_in"][i]), w_pool=_mx(P["o_w_pool"][i]), s_pool=P["o_s_pool"][i].reshape(1, POOL_WIDTH),
                 conv_w8=jnp.pad(P["o_conv_w"][i], ((0, SUBLANES - CONV_K), (0, 0))), w_out=_mx(P["o_w_out"][i]))
    return W


def _ffn_bwd_act(name, g, wd, a, b):
    T, N = a.shape
    K = g.shape[1]
    tm, tn = _tile(T, 512, 16), _tile(N, 1536, LANES)

    def body(g_ref, wd_ref, a_ref, b_ref, da_ref, db_ref):
        ds = _dot_nt(g_ref[...], wd_ref[...])
        av = a_ref[...]
        sig = jax.nn.sigmoid(av)
        da_ref[...] = (ds * b_ref[...] * (sig * (1.0 + av * (1.0 - sig)))).astype(da_ref.dtype)
        db_ref[...] = (ds * (av * sig)).astype(db_ref.dtype)

    o_spec = pl.BlockSpec((tm, tn), lambda j, i: (i, j))
    return pl.pallas_call(
        body, name=name, grid=(N // tn, T // tm),
        in_specs=[pl.BlockSpec((tm, K), lambda j, i: (i, 0)), pl.BlockSpec((tn, K), lambda j, i: (j, 0)), o_spec, o_spec],
        out_specs=[o_spec, o_spec],
        out_shape=[jax.ShapeDtypeStruct((T, N), _MXU), jax.ShapeDtypeStruct((T, N), _MXU)],
        compiler_params=_cp(("parallel", "parallel")),
    )(g, wd, a, b)


def _xa_bwd(name, qx, do, gq, kn, vx):
    T, M = qx.shape[0], kn.shape[0]
    tm = _tile(T, 512, 16)

    def body(q_ref, do_ref, gq_ref, k_ref, v_ref, dq_ref, dk_ref, dv_ref, dg_ref):
        @pl.when(pl.program_id(0) == 0)
        def _():
            dk_ref[...] = jnp.zeros_like(dk_ref)
            dv_ref[...] = jnp.zeros_like(dv_ref)
            dg_ref[...] = jnp.zeros_like(dg_ref)

        gqv = gq_ref[...]
        for h in range(XA_HEADS):
            sl = slice(h * LANES, (h + 1) * LANES)
            qhat, r, qn, p = _xa_probs(q_ref[:, sl], gqv, k_ref[:, sl])
            do_h = do_ref[:, sl]
            dp = _dot_nt(do_h, v_ref[:, sl])
            dv_ref[:, sl] += _dot_tn(p, do_h)
            dsm = p * (dp - jnp.sum(p * dp, axis=-1, keepdims=True)) * (1.0 / math.sqrt(XA_HEAD_DIM))
            dqn = _dot(dsm, k_ref[:, sl])
            dk_ref[:, sl] += _dot_tn(dsm, qn)
            dg_ref[...] += jnp.sum(dqn * qhat, axis=0, keepdims=True)
            dxh = dqn * gqv
            dq_ref[:, sl] = (r * (dxh - qhat * jnp.mean(dxh * qhat, axis=-1, keepdims=True))).astype(dq_ref.dtype)

    full = lambda s: pl.BlockSpec(s, lambda i: tuple(0 for _ in s))
    blk = pl.BlockSpec((tm, XA_WIDTH), lambda i: (i, 0))
    return pl.pallas_call(
        body, name=name, grid=(T // tm,),
        in_specs=[blk, blk, full((1, LANES)), full((M, XA_WIDTH)), full((M, XA_WIDTH))],
        out_specs=[blk, full((M, XA_WIDTH)), full((M, XA_WIDTH)), full((1, LANES))],
        out_shape=[jax.ShapeDtypeStruct((T, XA_WIDTH), _MXU), jax.ShapeDtypeStruct((M, XA_WIDTH), F32),
                   jax.ShapeDtypeStruct((M, XA_WIDTH), F32), jax.ShapeDtypeStruct((1, LANES), F32)],
        compiler_params=_cp(("arbitrary",)),
    )(qx, do, gq.reshape(1, XA_HEAD_DIM), kn, vx)


def _mem_bwd(name, mem, g_mem, m, kv, gk, dkn, dvx, wkv):
    M = mem.shape[0]

    def body(mem_ref, g_ref, m_ref, kv_ref, gk_ref, dkn_ref, dvx_ref, w_ref, dw_ref, dgk_ref, dgm_ref):
        gkv = gk_ref[...]
        dgk = jnp.zeros((1, LANES), F32)
        parts = []
        for h in range(XA_HEADS):
            sl = slice(h * LANES, (h + 1) * LANES)
            k = kv_ref[:, sl]
            r = _rms_rows(k)
            khat = k * r
            dkn_h = dkn_ref[:, sl]
            dgk = dgk + jnp.sum(dkn_h * khat, axis=0, keepdims=True)
            dxh = dkn_h * gkv
            parts.append(r * (dxh - khat * jnp.mean(dxh * khat, axis=-1, keepdims=True)))
        dgk_ref[...] = dgk
        dkv = jnp.concatenate(parts + [dvx_ref[...]], axis=1)
        dw_ref[...] = _dot_tn(m_ref[...], dkv)
        dm = _dot_nt(dkv, w_ref[...])
        x = mem_ref[...]
        dgm_ref[...] = jnp.sum(dm * (x * _rms_rows(x)), axis=0, keepdims=True)

    return pl.pallas_call(
        body, name=name,
        out_shape=[jax.ShapeDtypeStruct((D_MODEL, 2 * XA_WIDTH), F32), jax.ShapeDtypeStruct((1, LANES), F32),
                   jax.ShapeDtypeStruct((1, D_MODEL), F32)],
        compiler_params=pltpu.CompilerParams(vmem_limit_bytes=VMEM_LIMIT),
    )(mem, g_mem.reshape(1, D_MODEL), m, kv, gk.reshape(1, XA_HEAD_DIM), dkn, dvx, wkv)


def _fox_bwd_common(q2, k2, v2, dy2, yf2, c_ref, ct_ref, lse_ref, e, h, lo, diag, tq):
    sel = lo if e == 0 else ~lo
    qh = jnp.where(sel, q2, jnp.zeros_like(q2))
    kh = jnp.where(sel, k2, jnp.zeros_like(k2))
    dyh = jnp.where(sel, dy2, 0.0)
    s = _fox_logits(qh, k2, c_ref[0:1, h:h + 1], ct_ref[h:h + 1, :], diag, tq)
    p = jnp.exp(s - lse_ref[:, h:h + 1])
    dp = _dot_nt(dyh, v2)
    delta = jnp.sum(dyh * yf2, axis=-1, keepdims=True)
    return p, p * (dp - delta), qh, kh, dyh


def _fox_bwd(name, qn, kn, vv, c, ct, lse, dycat, ybf, side=None):
    T = qn.shape[0]
    tq = _tile(T, FOX_TILE, LANES)
    nblk = T // tq
    qs, ks = _tri_steps(nblk, False)
    nsteps = int(qs.shape[0])
    scale = 1.0 / math.sqrt(FOX_HEAD_DIM)
    s_specs, s_args, s_shapes, s_scratch = _side_specs(side)

    def body(qs_ref, ks_ref, *refs):
        main_in, side_in, main_out, side_out, main_scr, sems = _split_side(refs, 8, 5, 3, side)
        q_ref, k_ref, v_ref, c_ref, ct_ref, lse_ref, dy_ref, yf_ref = main_in
        dq_hbm, drow_hbm, dk_ref, dv_ref, dct_ref = main_out
        dq_acc, drow_acc, out_sem = main_scr
        s_id = pl.program_id(0)
        qi, ki = qs_ref[s_id], ks_ref[s_id]
        _side_run(side, s_id, nsteps, side_in, side_out, sems)
        rows = pl.ds(pl.multiple_of(qi * tq, tq), tq)

        @pl.when(qi == ki)
        def _():
            dk_ref[...] = jnp.zeros_like(dk_ref)
            dv_ref[...] = jnp.zeros_like(dv_ref)
            dct_ref[...] = jnp.zeros_like(dct_ref)

        @pl.when(ki == 0)
        def _():
            dq_acc[rows, :] = jnp.zeros((tq, FOX_WIDTH), F32)
            drow_acc[rows, :] = jnp.zeros((tq, LANES), F32)

        lo = _lane((tq, LANES)) < FOX_HEAD_DIM

        def step(diag):
            drow = jnp.zeros((tq, LANES), F32)
            for j in range(FOX_WIDTH // LANES):
                sl = slice(j * LANES, (j + 1) * LANES)
                q2, k2, v2, dy2, yf2 = q_ref[:, sl], k_ref[:, sl], v_ref[:, sl], dy_ref[:, sl], yf_ref[:, sl]
                dq_t, dk_t, dv_t = None, None, None
                for e in range(2):
                    h = 2 * j + e
                    p, ds, qh, kh, dyh = _fox_bwd_common(q2, k2, v2, dy2, yf2, c_ref, ct_ref, lse_ref, e, h, lo, diag, tq)
                    drow = jnp.where(_lane((tq, LANES)) == h, jnp.sum(ds, axis=-1, keepdims=True), drow)
                    dct_ref[h:h + 1, :] -= jnp.sum(ds, axis=0, keepdims=True)
                    a, b, d = _dot(ds, kh), _dot_tn(ds, qh), _dot_tn(p, dyh)
                    dq_t = a if dq_t is None else dq_t + a
                    dk_t = b if dk_t is None else dk_t + b
                    dv_t = d if dv_t is None else dv_t + d
                dq_acc[rows, sl] += dq_t * scale
                dk_ref[:, sl] += dk_t
                dv_ref[:, sl] += dv_t
            drow_acc[rows, :] += drow

        _on_diag_or_not(qi, ki, step)

        @pl.when(s_id == nsteps - 1)
        def _():
            for src, dst in ((dq_acc, dq_hbm), (drow_acc, drow_hbm)):
                cp = pltpu.make_async_copy(src, dst, out_sem)
                cp.start()
                cp.wait()

    qmap = lambda s, a, b: (a[s], 0)
    kmap = lambda s, a, b: (b[s], 0)
    ctmap = lambda s, a, b: (0, b[s])
    any_spec = pl.BlockSpec(memory_space=pl.ANY)
    grid_spec = pltpu.PrefetchScalarGridSpec(
        num_scalar_prefetch=2, grid=(nsteps,),
        in_specs=[pl.BlockSpec((tq, 512), qmap), pl.BlockSpec((tq, 512), kmap), pl.BlockSpec((tq, 512), kmap),
                  pl.BlockSpec((SUBLANES, LANES), lambda s, a, b: (a[s] * (tq // SUBLANES), 0)), pl.BlockSpec((SUBLANES, tq), ctmap),
                  pl.BlockSpec((tq, LANES), qmap), pl.BlockSpec((tq, 512), lambda s, a, b: (a[s], 1)), pl.BlockSpec((tq, 512), qmap)] + s_specs,
        out_specs=[any_spec, any_spec, pl.BlockSpec((tq, 512), kmap), pl.BlockSpec((tq, 512), kmap), pl.BlockSpec((SUBLANES, tq), ctmap)] + s_specs,
        scratch_shapes=[pltpu.VMEM((T, FOX_WIDTH), F32), pltpu.VMEM((T, LANES), F32), pltpu.SemaphoreType.DMA] + s_scratch,
    )
    res = pl.pallas_call(
        body, name=name, grid_spec=grid_spec,
        out_shape=[jax.ShapeDtypeStruct((T, 512), F32), jax.ShapeDtypeStruct((T, LANES), F32), jax.ShapeDtypeStruct((T, 512), F32),
                   jax.ShapeDtypeStruct((T, 512), F32), jax.ShapeDtypeStruct((SUBLANES, T), F32)] + s_shapes,
        compiler_params=_cp(("arbitrary",)),
    )(qs, ks, qn, kn, vv, c, ct, lse, dycat, ybf, *s_args)
    return res[:5], res[5:]


def _fold64(row):
    return row + pltpu.roll(row, FOX_HEAD_DIM, axis=1)


def _even_bwd(name, z, dycat, dqn, dkn, dvv, dct, drow, g_v, w_s, bs_cols, g_qn2, g_kn2, b_f_pad):
    T = z.shape[0]
    tm = _tile(T, 256, A_BLOCK)
    nb, nt = tm // A_BLOCK, T // tm

    def body(z_ref, dya_ref, dq_ref, dk_ref, dv_ref, dct_ref, drow_ref, gv_ref, ws_ref, bs_ref, gq_ref, gk_ref, bf_ref,
             dz_ref, dws_ref, dbs_ref, dgv_ref, dgq_ref, dgk_ref, dbf_ref, carry):
        @pl.when(pl.program_id(0) == 0)
        def _():
            for r in (dws_ref, dbs_ref, dgv_ref, dgq_ref, dgk_ref, dbf_ref, carry):
                r[...] = jnp.zeros_like(r)

        keep = (_row((A_BLOCK, A_BLOCK)) // CHUNK) >= (_lane((A_BLOCK, A_BLOCK)) // CHUNK)
        for g in range(A_GROUPS):
            sl = slice(g * A_GROUP_DIM, (g + 1) * A_GROUP_DIM)
            wm = _mx(_masked_ws(ws_ref, g))
            zu = z_ref[:, sl]
            zv = z_ref[:, A_WIDTH + g * A_GROUP_DIM:A_WIDTH + (g + 1) * A_GROUP_DIM]
            u, v = _gelu(zu), _gelu(zv)
            r = _rms_rows(v)
            vhat = v * r
            gvv = gv_ref[:, sl]
            vn = _mx(vhat * gvv)
            dya = dya_ref[:, sl]
            du_parts, dvn_parts = [], []
            for n in range(nb):
                rows = slice(n * A_BLOCK, (n + 1) * A_BLOCK)
                s = jnp.dot(wm, vn[rows], preferred_element_type=F32) + bs_ref[:, g:g + 1]
                du_parts.append(dya[rows] * s)
                d_s = dya[rows] * u[rows]
                dbs_ref[...] += jnp.where(_lane((A_BLOCK, LANES)) == g, jnp.sum(d_s, axis=-1, keepdims=True), 0.0)
                dws_ref[g] += jnp.where(keep, _dot_nt(d_s, vn[rows]), 0.0)
                dvn_parts.append(_dot_tn(wm, d_s))
            dvn = jnp.concatenate(dvn_parts, axis=0)
            dgv_ref[:, sl] += jnp.sum(dvn * vhat, axis=0, keepdims=True)
            dxh = dvn * gvv
            dv = r * (dxh - vhat * jnp.mean(dxh * vhat, axis=-1, keepdims=True))
            dz_ref[:, sl] = (jnp.concatenate(du_parts, axis=0) * _gelu_grad(zu)).astype(dz_ref.dtype)
            dz_ref[:, A_WIDTH + g * A_GROUP_DIM:A_WIDTH + (g + 1) * A_GROUP_DIM] = (dv * _gelu_grad(zv)).astype(dz_ref.dtype)
        o = 2 * A_WIDTH
        for j in range(FOX_WIDTH // LANES):
            sl = slice(j * LANES, (j + 1) * LANES)
            for (off, d_ref, g_ref, dg_ref) in ((o, dq_ref, gq_ref, dgq_ref), (o + FOX_WIDTH, dk_ref, gk_ref, dgk_ref)):
                zq = z_ref[:, off + j * LANES:off + (j + 1) * LANES]
                r = _head64_rms(zq)
                qhat = zq * r
                dn = d_ref[:, sl]
                dg_ref[...] += _fold64(jnp.sum(dn * qhat, axis=0, keepdims=True))
                dxh = dn * g_ref[...]
                dz_ref[:, off + j * LANES:off + (j + 1) * LANES] = (r * (dxh - qhat * _head64_mean(dxh * qhat))).astype(dz_ref.dtype)
            dz_ref[:, o + 2 * FOX_WIDTH + j * LANES:o + 2 * FOX_WIDTH + (j + 1) * LANES] = dv_ref[:, sl].astype(dz_ref.dtype)
        dct_v = dct_ref[...] + drow_ref[...].T[0:SUBLANES, :]
        upper = (_row((tm, tm)) >= _lane((tm, tm))).astype(F32)
        dlf_t = _dot_f32(dct_v, upper) + carry[:, 0:1]
        carry[...] += jnp.sum(dct_v, axis=1, keepdims=True)
        dlf = jnp.concatenate([dlf_t, jnp.zeros((LANES - SUBLANES, tm), F32)], axis=0).T
        zf = z_ref[:, o + 3 * FOX_WIDTH:o + 3 * FOX_WIDTH + LANES]
        dzf = dlf * jax.nn.sigmoid(-(zf + bf_ref[...]))
        dbf_ref[...] += jnp.sum(dzf, axis=0, keepdims=True)
        dz_ref[:, o + 3 * FOX_WIDTH:o + 3 * FOX_WIDTH + LANES] = dzf.astype(dz_ref.dtype)

    rev = lambda i: nt - 1 - i
    wide = lambda n, col=0: pl.BlockSpec((tm, n), lambda i: (rev(i), col))
    full = lambda s: pl.BlockSpec(s, lambda i: tuple(0 for _ in s))
    vec = full((1, LANES))
    return pl.pallas_call(
        body, name=name, grid=(nt,),
        in_specs=[wide(EVEN_IN_PAD), wide(512), wide(512), wide(512), wide(512), pl.BlockSpec((SUBLANES, tm), lambda i: (0, rev(i))),
                  wide(LANES), full((1, A_WIDTH)), full((A_GROUPS, A_BLOCK, A_BLOCK)), full((A_BLOCK, LANES)), vec, vec, vec],
        out_specs=[wide(EVEN_IN_PAD), full((A_GROUPS, A_BLOCK, A_BLOCK)), full((A_BLOCK, LANES)), full((1, A_WIDTH)), vec, vec, vec],
        out_shape=[jax.ShapeDtypeStruct((T, EVEN_IN_PAD), _MXU), jax.ShapeDtypeStruct((A_GROUPS, A_BLOCK, A_BLOCK), F32),
                   jax.ShapeDtypeStruct((A_BLOCK, LANES), F32), jax.ShapeDtypeStruct((1, A_WIDTH), F32),
                   jax.ShapeDtypeStruct((1, LANES), F32), jax.ShapeDtypeStruct((1, LANES), F32), jax.ShapeDtypeStruct((1, LANES), F32)],
        scratch_shapes=[pltpu.VMEM((SUBLANES, LANES), F32)],
        compiler_params=_cp(("arbitrary",)),
    )(z, dycat, dqn, dkn, dvv, dct, drow, g_v, w_s, bs_cols, g_qn2, g_kn2, b_f_pad)


def _shift_up(ext, k, tm):
    return pltpu.roll(ext, ext.shape[0] - k, axis=0)[0:tm]


def _odd_bwd(name, z, dycat, w_pool, s_pool, conv_w8):
    T = z.shape[0]
    tm = _tile(T, 256, HALO)
    r, nt = tm // HALO, T // tm
    n_ext = tm + HALO

    def body(z_ref, zb_ref, zn_ref, dy_ref, dyn_ref, wp_ref, sp_ref, cw_ref, dz_ref, dwp_ref, dsp_ref, dcw_ref):
        i = pl.program_id(0)

        @pl.when(i == 0)
        def _():
            for rr in (dwp_ref, dsp_ref, dcw_ref):
                rr[...] = jnp.zeros_like(rr)

        first, last = i == 0, i == nt - 1
        pos = i * tm + _row((tm, LANES))
        pos_ext = i * tm + _row((n_ext, LANES))
        for g, w in enumerate(POOL_WINDOWS):
            sl = slice(g * LANES, (g + 1) * LANES)
            z_ext = jnp.concatenate([jnp.where(first, 0.0, zb_ref[:, sl]), z_ref[:, sl]], axis=0)
            p = _pool_p(z_ext, g, tm, pos)
            spv = sp_ref[:, sl]
            dyc = dy_ref[:, sl]
            dsp_ref[:, sl] += jnp.sum(dyc * _dot(p, wp_ref[g]), axis=0, keepdims=True)
            dpw_ext = jnp.concatenate([dyc, jnp.where(last, 0.0, dyn_ref[:, sl])], axis=0) * spv
            dwp_ref[g] += _dot_tn(p, dpw_ext[0:tm])
            dp_ext = _dot_nt(dpw_ext, wp_ref[g])
            f = dp_ext / jnp.minimum(pos_ext + 1, w).astype(F32)
            span = 1
            while span < w:
                f = f + pltpu.roll(f, n_ext - span, axis=0)
                span *= 2
            dz_ref[:, sl] = (f[0:tm] - dp_ext[0:tm]).astype(dz_ref.dtype)
        for j in range(CONV_WIDTH // LANES):
            sl = slice(j * LANES, (j + 1) * LANES)
            hd = slice(POOL_WIDTH + j * LANES, POOL_WIDTH + (j + 1) * LANES)
            gb = slice(POOL_WIDTH + CONV_WIDTH + j * LANES, POOL_WIDTH + CONV_WIDTH + (j + 1) * LANES)
            gc = slice(POOL_WIDTH + 2 * CONV_WIDTH + j * LANES, POOL_WIDTH + 2 * CONV_WIDTH + (j + 1) * LANES)
            ysl = slice(POOL_WIDTH + j * LANES, POOL_WIDTH + (j + 1) * LANES)
            hdv, gbv, gcv = z_ref[:, hd], z_ref[:, gb], z_ref[:, gc]
            xg = gcv * hdv
            xg_ext = jnp.concatenate([jnp.where(first, 0.0, zb_ref[:, gc] * zb_ref[:, hd]), xg], axis=0)
            xg1, xg2 = _shift_down(xg_ext, 1, tm), _shift_down(xg_ext, 2, tm)
            w0, w1, w2 = cw_ref[0:1, sl], cw_ref[1:2, sl], cw_ref[2:3, sl]
            conv = w0 * xg2 + w1 * xg1 + w2 * xg
            dyd = dy_ref[:, ysl]
            dconv = dyd * gbv
            dconv_ext = jnp.concatenate([dconv, jnp.where(last, 0.0, dyn_ref[:, ysl] * zn_ref[:, gb])], axis=0)
            dcw_ref[0:1, sl] += jnp.sum(dconv * xg2, axis=0, keepdims=True)
            dcw_ref[1:2, sl] += jnp.sum(dconv * xg1, axis=0, keepdims=True)
            dcw_ref[2:3, sl] += jnp.sum(dconv * xg, axis=0, keepdims=True)
            dxg = w2 * dconv + w1 * _shift_up(dconv_ext, 1, tm) + w0 * _shift_up(dconv_ext, 2, tm)
            dz_ref[:, hd] = (dxg * gcv).astype(dz_ref.dtype)
            dz_ref[:, gb] = (dyd * conv).astype(dz_ref.dtype)
            dz_ref[:, gc] = (dxg * hdv).astype(dz_ref.dtype)

    full = lambda s: pl.BlockSpec(s, lambda i: tuple(0 for _ in s))
    back = lambda n: pl.BlockSpec((HALO, n), lambda i: (jnp.maximum(i * r - 1, 0), 0))
    nxt = lambda n: pl.BlockSpec((HALO, n), lambda i: (jnp.minimum((i + 1) * r, T // HALO - 1), 0))
    return pl.pallas_call(
        body, name=name, grid=(nt,),
        in_specs=[pl.BlockSpec((tm, ODD_IN), lambda i: (i, 0)), back(ODD_IN), nxt(ODD_IN),
                  pl.BlockSpec((tm, 1024), lambda i: (i, 0)), nxt(1024),
                  full((4, LANES, LANES)), full((1, POOL_WIDTH)), full((SUBLANES, CONV_WIDTH))],
        out_specs=[pl.BlockSpec((tm, ODD_IN), lambda i: (i, 0)), full((4, LANES, LANES)), full((1, POOL_WIDTH)), full((SUBLANES, CONV_WIDTH))],
        out_shape=[jax.ShapeDtypeStruct((T, ODD_IN), _MXU), jax.ShapeDtypeStruct((4, LANES, LANES), F32),
                   jax.ShapeDtypeStruct((1, POOL_WIDTH), F32), jax.ShapeDtypeStruct((SUBLANES, CONV_WIDTH), F32)],
        compiler_params=_cp(("arbitrary",)),
    )(z, z, z, dycat, dycat, w_pool, s_pool, conv_w8)


def _layer_bwd(l, g3, mem, W, sv, side_fn=None):
    i = l // 2
    G = {}
    side_out = ()
    da, db = _ffn_bwd_act(f"ffn_bwd_act{l}", g3, W["w_down"], sv["a"], sv["b"])
    G["w_down"] = _mm(f"d_w_down{l}", [(sv["s"], g3)], "tn")
    dh3 = _mm(f"d_h3{l}", [(da, W["w_gate"]), (db, W["w_up"])], "nt")
    G["w_gate"] = _mm(f"d_w_gate{l}", [(sv["h3"], da)], "tn")
    G["w_up"] = _mm(f"d_w_up{l}", [(sv["h3"], db)], "tn")
    g2, dg = _norm_bwd(f"norm_ffn_bwd{l}", dh3, sv["x2"], W["g_ffn"], g3)
    G["g_ffn"] = dg[0]

    do = _mm(f"d_o{l}", [(g2, W["wo"])], "nt", out_dtype=_MXU)
    G["xa_wo"] = _mm(f"d_wo{l}", [(sv["o"], g2)], "tn")
    dqx, dkn, dvx, dgq = _xa_bwd(f"xa_bwd{l}", sv["qx"], do, W["gq"], sv["kn_x"], sv["vx"])
    G["xa_gq"] = dgq[0]
    dh2 = _mm(f"d_h2{l}", [(dqx, W["wq"])], "nt")
    G["xa_wq"] = _mm(f"d_wq{l}", [(sv["h2"], dqx)], "tn")
    g1, dg = _norm_bwd(f"norm_xa_bwd{l}", dh2, sv["x1"], W["g_xa"], g2)
    G["g_xa"] = dg[0]
    dwkv, dgk, dgm = _mem_bwd(f"mem_bwd{l}", mem, W["g_mem"], sv["m"], sv["kv"], W["gk"], dkn, dvx, W["wkv"])
    G["xa_wkv"], G["xa_gk"], G["g_mem"] = dwkv, dgk[0], dgm[0]

    dycat = _mm(f"d_ycat{l}", [(g1, W["w_out"])], "nt")
    if l % 2 == 0:
        G["e_w_out"] = jnp.concatenate([_mm(f"d_e_wout_a{l}", [(sv["ya"], g1)], "tn"), _mm(f"d_e_wout_b{l}", [(sv["yb"], g1)], "tn")], axis=0)
        fa = (sv["qn"], sv["kn"], sv["vv"], sv["c"], sv["ct"], sv["lse"], dycat, sv["ybf"])
        (dqn, drow, dkn_f, dvv, dct), side_out = _fox_bwd(f"fox_bwd{l}", *fa, side=side_fn(G) if side_fn is not None else None)
        dz, dws, dbs, dgv, dgq2, dgk2, dbf = _even_bwd(f"even_bwd{l}", sv["z"], dycat, dqn, dkn_f, dvv, dct, drow, W["g_v"], W["w_s"],
                                                      W["bs_cols"], W["g_qn2"], W["g_kn2"], W["b_f_pad"])
        G.update(e_w_s=dws, e_b_s=dbs[:, :A_GROUPS].T, e_g_v=dgv[0], e_g_qn=dgq2[0, :FOX_HEAD_DIM], e_g_kn=dgk2[0, :FOX_HEAD_DIM],
                 e_b_f=dbf[0, :FOX_HEADS])
        G["e_w_in"] = _mm(f"d_e_win{l}", [(sv["h1"], dz)], "tn")
    else:
        G["o_w_out"] = _mm(f"d_o_wout{l}", [(sv["ycat"], g1)], "tn")
        dz, dwp, dsp, dcw = _odd_bwd(f"odd_bwd{l}", sv["z"], dycat, W["w_pool"], W["s_pool"], W["conv_w8"])
        G.update(o_w_pool=dwp, o_s_pool=dsp[0], o_conv_w=dcw[:CONV_K])
        G["o_w_in"] = _mm(f"d_o_win{l}", [(sv["h1"], dz)], "tn")
    dh1 = _mm(f"d_h1{l}", [(dz, W["w_in"])], "nt")
    g0, dg = _norm_bwd(f"norm_mix_bwd{l}", dh1, sv["x0"], W["g_mix"], g1)
    G["g_mix"] = dg[0]
    return g0, G, side_out


PACK_W = 1024
MESH_T = pl.DeviceIdType.MESH
_ANY = pl.BlockSpec(memory_space=pl.ANY)


def _my_place():
    x, y, c = lax.axis_index("x"), lax.axis_index("y"), lax.axis_index("c")
    return x, y, c


def _flip(v, bit):
    return 1 - v if bit else v


N_PEERS = N_DEV - 1


def _all_gather(name, blks):
    side = _TwoLevelGather(blks)

    def body(*refs):
        _, side_in, _, side_out, _, sems = _split_side(refs, 0, 0, 0, side)
        side.start(side_in, side_out, sems)
        side.middle(side_in, side_out, sems)
        side.finish(side_in, side_out, sems)

    return pl.pallas_call(
        body, name=name, out_shape=side.out_shape,
        in_specs=[_ANY] * side.n, out_specs=[_ANY] * side.n, scratch_shapes=side.scratch,
    )(*blks)


class _Direct:
    def __init__(self, kind, arrays):
        self.kind, self.arrays, self.n = kind, list(arrays), len(arrays)
        if kind == "scatter":
            self.out_shape = [jax.ShapeDtypeStruct(a.shape, a.dtype) for a in arrays]
        else:
            self.out_shape = [jax.ShapeDtypeStruct((N_DEV,) + a.shape, a.dtype) for a in arrays]
        self.scratch = [pltpu.SemaphoreType.DMA((N_PEERS * self.n,)), pltpu.SemaphoreType.DMA((N_PEERS * self.n,)),
                        pltpu.SemaphoreType.DMA((self.n,))]

    def _copies(self, in_refs, out_refs, sems):
        send_sems, recv_sems, local_sems = sems
        x, y, c = _my_place()
        me = 4 * x + 2 * y + c
        mine, sends, recvs = [], [], []
        for t in range(self.n):
            src_of = (lambda idx, t=t: in_refs[t].at[idx]) if self.kind == "scatter" else (lambda idx, t=t: in_refs[t])
            mine.append(pltpu.make_async_copy(src_of(me), out_refs[t].at[me], local_sems.at[t]))
            for m in range(1, N_DEV):
                px, py, pc = _flip(x, m & 4), _flip(y, m & 2), _flip(c, m & 1)
                pidx = 4 * px + 2 * py + pc
                sem = dict(send_sem=send_sems.at[N_PEERS * t + m - 1], recv_sem=recv_sems.at[N_PEERS * t + m - 1], device_id_type=MESH_T)
                sends.append(pltpu.make_async_remote_copy(src_ref=src_of(pidx), dst_ref=out_refs[t].at[me], device_id=(px, py, pc), **sem))
                recvs.append(pltpu.make_async_remote_copy(src_ref=src_of(pidx), dst_ref=out_refs[t].at[pidx], device_id=(x, y, c), **sem))
        return mine, sends, recvs

    def start(self, in_refs, out_refs, sems, only=None):
        mine, sends, _ = self._copies(in_refs, out_refs, sems)
        if only is not None:
            mine, sends = mine[only:only + 1], sends[N_PEERS * only:N_PEERS * (only + 1)]
        for cp in mine + sends:
            cp.start()

    def middle(self, in_refs, out_refs, sems):
        pass

    def finish(self, in_refs, out_refs, sems):
        mine, sends, recvs = self._copies(in_refs, out_refs, sems)
        for cp in recvs:
            cp.wait_recv()
        for cp in sends:
            cp.wait_send()
        for cp in mine:
            cp.wait()


class _TwoLevelGather:
    def __init__(self, arrays):
        self.arrays, self.n = list(arrays), len(arrays)
        self.out_shape = [jax.ShapeDtypeStruct((N_DEV,) + a.shape, a.dtype) for a in arrays]
        self.scratch = [pltpu.SemaphoreType.DMA((N_PEERS * self.n,)), pltpu.SemaphoreType.DMA((N_PEERS * self.n,)),
                        pltpu.SemaphoreType.DMA((self.n,))]

    def _place(self):
        x, y, c = _my_place()
        return (x, y, c), (x, y, 1 - c), [(1 - x, y), (x, 1 - y), (1 - x, 1 - y)]

    def _copy(self, in_refs, out_refs, sems, t, k, block, to, own=False):
        slab = out_refs[t].at[4 * block[0] + 2 * block[1] + block[2]]
        return pltpu.make_async_remote_copy(
            src_ref=in_refs[t] if own else slab, dst_ref=slab, send_sem=sems[0].at[N_PEERS * t + k],
            recv_sem=sems[1].at[N_PEERS * t + k], device_id=to, device_id_type=MESH_T)

    def _mine(self, in_refs, out_refs, sems):
        x, y, c = _my_place()
        return [pltpu.make_async_copy(in_refs[t], out_refs[t].at[4 * x + 2 * y + c], sems[2].at[t]) for t in range(self.n)]

    def _first(self, in_refs, out_refs, sems):
        me, sibling, chips = self._place()
        cps = []
        for t in range(self.n):
            cps.append(self._copy(in_refs, out_refs, sems, t, 0, me, sibling, own=True))
            cps += [self._copy(in_refs, out_refs, sems, t, 1 + j, me, (*chip, me[2]), own=True) for j, chip in enumerate(chips)]
        return cps

    def _passed(self, in_refs, out_refs, sems):
        me, sibling, chips = self._place()
        return [self._copy(in_refs, out_refs, sems, t, 4 + j, (*chip, me[2]), sibling) for j, chip in enumerate(chips) for t in range(self.n)]

    def start(self, in_refs, out_refs, sems, only=None):
        mine, first = self._mine(in_refs, out_refs, sems), self._first(in_refs, out_refs, sems)
        if only is not None:
            mine, first = mine[only:only + 1], first[4 * only:4 * (only + 1)]
        for cp in mine + first:
            cp.start()

    def middle(self, in_refs, out_refs, sems):
        me, _, chips = self._place()
        for j, chip in enumerate(chips):
            for t in range(self.n):
                self._copy(in_refs, out_refs, sems, t, 1 + j, (*chip, me[2]), me).wait_recv()
        for cp in self._passed(in_refs, out_refs, sems):
            cp.start()

    def finish(self, in_refs, out_refs, sems):
        me, sibling, chips = self._place()
        for t in range(self.n):
            self._copy(in_refs, out_refs, sems, t, 0, sibling, me).wait_recv()
            for j, chip in enumerate(chips):
                self._copy(in_refs, out_refs, sems, t, 4 + j, (*chip, 1 - me[2]), me).wait_recv()
        for cp in self._first(in_refs, out_refs, sems) + self._passed(in_refs, out_refs, sems):
            cp.wait_send()
        for cp in self._mine(in_refs, out_refs, sems):
            cp.wait()


def _split_side(refs, n_main_in, n_main_out, n_main_scratch, side):
    ns = side.n if side is not None else 0
    i = 0
    main_in = refs[i:i + n_main_in]; i += n_main_in
    side_in = refs[i:i + ns]; i += ns
    main_out = refs[i:i + n_main_out]; i += n_main_out
    side_out = refs[i:i + ns]; i += ns
    main_scr = refs[i:i + n_main_scratch]; i += n_main_scratch
    return main_in, side_in, main_out, side_out, main_scr, refs[i:]


def _all_to_all(name, gs):
    side = _Direct("scatter", gs)

    def body(*refs):
        _, side_in, _, side_out, _, sems = _split_side(refs, 0, 0, 0, side)
        side.start(side_in, side_out, sems)
        side.finish(side_in, side_out, sems)

    return pl.pallas_call(
        body, name=name, out_shape=side.out_shape,
        in_specs=[_ANY] * side.n, out_specs=[_ANY] * side.n, scratch_shapes=side.scratch,
    )(*gs)


def _unblock(name, blocks, layer, n_out):
    _, _, K, nb = blocks.shape
    tk = _tile(K, 256, 16)

    def body(b_ref, o_ref):
        for j in range(N_DEV):
            o_ref[:, j * nb:(j + 1) * nb] = b_ref[j]
        if n_out > N_DEV * nb:
            o_ref[:, N_DEV * nb:n_out] = jnp.zeros((tk, n_out - N_DEV * nb), o_ref.dtype)

    return pl.pallas_call(
        body, name=name, grid=(K // tk,),
        in_specs=[pl.BlockSpec((N_DEV, None, tk, nb), lambda i: (0, layer, i, 0))],
        out_specs=pl.BlockSpec((tk, n_out), lambda i: (i, 0)),
        out_shape=jax.ShapeDtypeStruct((K, n_out), blocks.dtype),
        compiler_params=_cp(("parallel",)),
    )(blocks)


def _block(name, full, nb, dtype):
    K, N = full.shape
    tk = _tile(K, 256, 16)

    def body(f_ref, o_ref):
        for j in range(N_DEV):
            o_ref[j] = f_ref[:, j * nb:(j + 1) * nb].astype(o_ref.dtype)

    return pl.pallas_call(
        body, name=name, grid=(K // tk,),
        in_specs=[pl.BlockSpec((tk, N), lambda i: (i, 0))],
        out_specs=pl.BlockSpec((N_DEV, tk, nb), lambda i: (0, i, 0)),
        out_shape=jax.ShapeDtypeStruct((N_DEV, K, nb), dtype),
        compiler_params=_cp(("parallel",)),
    )(full)


ADAM_BLOCK_ELEMS = 256 * 1024
GRAD_WIRE = jnp.bfloat16


def _adamw(name, parts, w, m, v, layer=0):
    shape = w.shape[1:]
    cols = shape[-1]
    rows = math.prod(shape[:-1])
    nl = w.shape[0]
    parts, w, m, v = parts.reshape(N_DEV, rows, cols), w.reshape(nl, rows, cols), m.reshape(nl, rows, cols), v.reshape(nl, rows, cols)
    tr = _tile(rows, max(16, ADAM_BLOCK_ELEMS // cols), 16)

    def body(p_ref, w_ref, m_ref, v_ref, g_ref, d_ref, mo_ref, vo_ref):
        g = p_ref[0].astype(F32)
        for s in range(1, N_DEV):
            g = g + p_ref[s].astype(F32)
        mn = ADAM_B1 * m_ref[...] + (1.0 - ADAM_B1) * g
        vn = ADAM_B2 * v_ref[...] + (1.0 - ADAM_B2) * jnp.square(g)
        m_hat = mn / (1.0 - ADAM_B1 ** ADAM_STEP)
        v_hat = vn / (1.0 - ADAM_B2 ** ADAM_STEP)
        g_ref[...] = g
        d_ref[...] = -ADAM_LR * (m_hat / (jnp.sqrt(v_hat) + ADAM_EPS) + ADAM_WD * w_ref[...])
        mo_ref[...] = mn
        vo_ref[...] = vn

    blk = pl.BlockSpec((tr, cols), lambda i: (i, 0))
    lblk = pl.BlockSpec((None, tr, cols), lambda i: (layer, i, 0))
    shp = jax.ShapeDtypeStruct((rows, cols), F32)
    res = pl.pallas_call(
        body, name=name, grid=(rows // tr,),
        in_specs=[pl.BlockSpec((N_DEV, tr, cols), lambda i: (0, i, 0)), lblk, lblk, lblk],
        out_specs=[blk, blk, blk, blk], out_shape=[shp, shp, shp, shp],
        compiler_params=_cp(("parallel",)),
    )(parts, w, m, v)
    return [r.reshape(shape) for r in res]


SHARDED = dict(xa_wq=1, xa_wkv=1, xa_wo=2, w_gate=2, w_up=2, w_down=1, e_w_in=2, e_w_out=1, o_w_in=2, o_s_pool=1, o_conv_w=2, o_w_out=1)
LAYER_SPLIT = ("xa_wq", "xa_wkv", "xa_wo", "w_gate", "w_up", "w_down")
ELEMENTWISE_SHARDED = ("o_s_pool", "o_conv_w")
REPLICATED = ("g_mix", "g_xa", "g_mem", "xa_gq", "xa_gk", "g_ffn", "e_b_f", "e_g_v", "e_w_s", "e_b_s", "e_g_qn", "e_g_kn", "o_w_pool")
WEIGHTS = ("g_mix", "g_xa", "g_mem", "xa_wq", "xa_wkv", "xa_wo", "xa_gq", "xa_gk", "g_ffn", "w_gate", "w_up", "w_down", "e_w_in", "e_b_f",
           "e_g_v", "e_w_s", "e_b_s", "e_g_qn", "e_g_kn", "e_w_out", "o_w_in", "o_w_pool", "o_s_pool", "o_conv_w", "o_w_out")


def _rows_for(n, mult):
    return -(-n // (PACK_W * mult)) * mult


def _pack(arrs, rows, dtype):
    flat = jnp.concatenate([a.reshape(-1).astype(dtype) for a in arrs])
    return jnp.pad(flat, (0, rows * PACK_W - flat.shape[0])).reshape(rows, PACK_W)


def _unpack(slab, shapes):
    flat, out, off = slab.reshape(-1), [], 0
    for s in shapes:
        n = math.prod(s)
        out.append(flat[off:off + n].reshape(s))
        off += n
    return out


def _to_blocks(full, axis):
    s = full.shape
    return jnp.moveaxis(full.reshape(s[:axis] + (N_DEV, s[axis] // N_DEV) + s[axis + 1:]), axis, 0)


def _from_blocks(blocks, axis):
    b = jnp.moveaxis(blocks, 0, axis)
    s = b.shape
    return b.reshape(s[:axis] + (s[axis] * s[axis + 1],) + s[axis + 2:])


def kernel(x, mem, g_mix, g_xa, g_mem, xa_wq, xa_wkv, xa_wo, xa_gq, xa_gk, g_ffn, w_gate, w_up, w_down, e_w_in, e_b_f, e_g_v, e_w_s, e_b_s, e_g_qn, e_g_kn, e_w_out, o_w_in, o_w_pool, o_s_pool, o_conv_w, o_w_out, loss_target, m_g_mix, m_g_xa, m_g_mem, m_xa_wq, m_xa_wkv, m_xa_wo, m_xa_gq, m_xa_gk, m_g_ffn, m_w_gate, m_w_up, m_w_down, m_e_w_in, m_e_b_f, m_e_g_v, m_e_w_s, m_e_b_s, m_e_g_qn, m_e_g_kn, m_e_w_out, m_o_w_in, m_o_w_pool, m_o_s_pool, m_o_conv_w, m_o_w_out, v_g_mix, v_g_xa, v_g_mem, v_xa_wq, v_xa_wkv, v_xa_wo, v_xa_gq, v_xa_gk, v_g_ffn, v_w_gate, v_w_up, v_w_down, v_e_w_in, v_e_b_f, v_e_g_v, v_e_w_s, v_e_b_s, v_e_g_qn, v_e_g_kn, v_e_w_out, v_o_w_in, v_o_w_pool, v_o_s_pool, v_o_conv_w, v_o_w_out):
    args = dict(locals())
    Wt = {n: args[n] for n in WEIGHTS}
    Mo = {n: args["m_" + n] for n in WEIGHTS}
    Vo = {n: args["v_" + n] for n in WEIGHTS}
    rep_names = REPLICATED
    rep_shapes = [Wt[n].shape for n in rep_names]
    rep_rows = _rows_for(sum(math.prod(s) for s in rep_shapes) + 1, 16)
    layer_names = [LAYER_SPLIT + ("e_w_in", "e_w_out"), LAYER_SPLIT + ("o_w_in", "o_w_out", "o_s_pool", "o_conv_w")]

    def shard_of(n, l, src=Wt):
        return src[n][l:l + 1] if n in LAYER_SPLIT else src[n]

    def to_wire(n, a):
        return a if n in ELEMENTWISE_SHARDED else _mx(a)

    def full_weight(n, l, gn):
        if SHARDED[n] == 1 or n in ELEMENTWISE_SHARDED:
            return _from_blocks(gn, SHARDED[n])[0]
        return _unblock(f"unblock_{n}{l}", gn, 0, EVEN_IN_PAD if n == "e_w_in" else N_DEV * gn.shape[-1])

    def grad_slabs(n, l, gl):
        if n in ELEMENTWISE_SHARDED:
            return _to_blocks(gl.reshape((1,) + Wt[n].shape[1:-1] + (-1,)), SHARDED[n])
        if SHARDED[n] == 1:
            return _to_blocks(gl[None], 1).astype(GRAD_WIRE)
        return _block(f"block_{n}{l}", gl, Wt[n].shape[-1], GRAD_WIRE)[:, None]

    xk, mem2 = x[0], mem[0]
    later = [(n, 0) for n in layer_names[0] if n != "e_w_in"] + [(n, 1) for n in layer_names[1]]
    w_in0 = full_weight("e_w_in", 0, _all_gather("gather_e_w_in", [to_wire("e_w_in", Wt["e_w_in"])])[0])
    side = _TwoLevelGather([to_wire(n, shard_of(n, l)) for n, l in later])
    P = {n: Wt[n] for n in rep_names}
    P["e_w_in"] = {0: w_in0}

    def on_side(gathered):
        for (n, l), gn in zip(later, gathered):
            P.setdefault(n, {})[l if n in LAYER_SPLIT else 0] = full_weight(n, l, gn)

    xk, sv0, W0 = _layer_fwd(0, xk, mem2, P, side=side, on_side=on_side)
    xk, sv1, W1 = _layer_fwd(1, xk, mem2, P)

    g, loss_row = _loss_grad("loss_grad", xk, loss_target[0])
    g, G1, _ = _layer_bwd(1, g, mem2, W1, sv1)
    slabs1 = [grad_slabs(n, 1, G1[n]) for n in layer_names[1]]
    g, G0, beside = _layer_bwd(0, g, mem2, W0, sv0,
                               side_fn=lambda G: _Direct("scatter", slabs1 + [grad_slabs(n, 0, G[n]) for n in LAYER_SPLIT]))
    parts1, parts0a = beside[:len(slabs1)], beside[len(slabs1):]
    grad_x = g[None]
    parts0b = _all_to_all("exchange_grads_mixer0", [grad_slabs(n, 0, G0[n]) for n in ("e_w_in", "e_w_out")])
    parts = {(n, 1): p for n, p in zip(layer_names[1], parts1)}
    parts.update({(n, 0): p for n, p in zip(LAYER_SPLIT, parts0a)})
    parts.update({(n, 0): p for n, p in zip(("e_w_in", "e_w_out"), parts0b)})
    G = {n: {0: G0[n]} for n in G0}
    for n in G1:
        G.setdefault(n, {})[1] = G1[n]
    rep_g = [jnp.stack([G[n][l] for l in sorted(G[n])]).reshape(Wt[n].shape) for n in rep_names]
    rep_parts = _all_gather("gather_small_grads", [_pack(rep_g + [jnp.sum(loss_row).reshape(1)], rep_rows, F32)])[0]

    outs = {}
    for n in SHARDED:
        per_layer = [_adamw(f"adamw_{n}{l}", parts[(n, l if n in LAYER_SPLIT else (0 if n.startswith("e_") else 1))], Wt[n], Mo[n], Vo[n], layer=l)
                     for l in range(Wt[n].shape[0])]
        for k, kind in enumerate(("grad", "delta", "new_m", "new_v")):
            outs[kind + "_" + n] = jnp.stack([r[k] for r in per_layer])
    res = _adamw("adamw_replicated", rep_parts, _pack([Wt[n] for n in rep_names], rep_rows, F32)[None],
                 _pack([Mo[n] for n in rep_names], rep_rows, F32)[None], _pack([Vo[n] for n in rep_names], rep_rows, F32)[None])
    for kind, slab in zip(("grad", "delta", "new_m", "new_v"), res):
        for n, a in zip(rep_names, _unpack(slab, rep_shapes)):
            outs[kind + "_" + n] = a
    loss = res[0].reshape(-1)[sum(math.prod(s) for s in rep_shapes)]
    return (loss, grad_x, *[outs[k + "_" + n] for k in ("grad", "delta", "new_m", "new_v") for n in WEIGHTS])
```

```python
import functools
import math

import numpy as np
import jax
import jax.numpy as jnp
from jax import lax
from jax.experimental import pallas as pl
from jax.experimental.pallas import tpu as pltpu

F32 = jnp.float32
BF16 = jnp.bfloat16
_MXU = jnp.bfloat16

D_MODEL = 1024
N_DEV = 8
LANES = 128
SUBLANES = 8
HALO = 16
CHUNK = 64
A_GROUPS, A_GROUP_DIM, A_WIDTH, A_BLOCK = 4, 128, 512, 128
FOX_HEADS, FOX_HEAD_DIM, FOX_WIDTH = 8, 64, 512
FOX_TILE = 512
POOL_WINDOWS = (2, 4, 8, 16)
POOL_WIDTH, CONV_WIDTH, CONV_K = 512, 512, 3
EVEN_IN, EVEN_IN_PAD, ODD_IN = 2568, 2688, 2048
XA_HEADS, XA_HEAD_DIM, XA_WIDTH = 4, 128, 512
D_FF = 2816
EPS = 1e-6
ADAM_LR, ADAM_B1, ADAM_B2, ADAM_EPS, ADAM_WD, ADAM_STEP = 0.001, 0.9, 0.999, 1e-08, 0.01, 10
VMEM_LIMIT = 48 * 1024 * 1024


def _cp(sem):
    return pltpu.CompilerParams(dimension_semantics=sem, vmem_limit_bytes=VMEM_LIMIT)


def _tile(n, cap, q):
    best = None
    for d in range(q, min(n, cap) + 1, q):
        if n % d == 0:
            best = d
    return best if best is not None else n


def _mx(a):
    return a.astype(_MXU)


def _dot(a, b):
    return jnp.dot(_mx(a), _mx(b), preferred_element_type=F32)


def _dot_nt(a, b):
    return lax.dot_general(_mx(a), _mx(b), (((1,), (1,)), ((), ())), preferred_element_type=F32)


def _dot_tn(a, b):
    return lax.dot_general(_mx(a), _mx(b), (((0,), (0,)), ((), ())), preferred_element_type=F32)


def _dot_f32(a, b):
    return jnp.dot(a, b, precision=lax.Precision.HIGHEST, preferred_element_type=F32)


def _lane(shape):
    return lax.broadcasted_iota(jnp.int32, shape, len(shape) - 1)


def _row(shape):
    return lax.broadcasted_iota(jnp.int32, shape, len(shape) - 2)


def _mm(name, pairs, mode, out_dtype=F32, residual=None, tm_cap=512, tn_cap=1536, tk_cap=2048):
    a0, b0 = pairs[0]
    if mode == "nn":
        (M, K), N = a0.shape, b0.shape[1]
    elif mode == "nt":
        (M, K), N = a0.shape, b0.shape[0]
    else:
        (K, M), N = a0.shape, b0.shape[1]
    if mode == "tn":
        tm, tk = _tile(M, 1408, LANES), _tile(K, 512, 16)
    else:
        tm, tk = _tile(M, tm_cap, 16), _tile(K, tk_cap, LANES)
    tn = _tile(N, tn_cap, LANES)
    nk = K // tk
    npairs = len(pairs)
    dot = {"nn": _dot, "nt": _dot_nt, "tn": _dot_tn}[mode]

    def body(*refs):
        ab = refs[:2 * npairs]
        res_ref = refs[2 * npairs] if residual is not None else None
        o_ref, acc = refs[-2], refs[-1]
        k = pl.program_id(2)

        @pl.when(k == 0)
        def _():
            acc[...] = jnp.zeros_like(acc)

        for p in range(npairs):
            acc[...] += dot(ab[2 * p][...], ab[2 * p + 1][...])

        @pl.when(k == nk - 1)
        def _():
            out = acc[...]
            if res_ref is not None:
                out = out + res_ref[...]
            o_ref[...] = out.astype(o_ref.dtype)

    if mode == "nn":
        a_spec = pl.BlockSpec((tm, tk), lambda j, i, k: (i, k))
        b_spec = pl.BlockSpec((tk, tn), lambda j, i, k: (k, j))
    elif mode == "nt":
        a_spec = pl.BlockSpec((tm, tk), lambda j, i, k: (i, k))
        b_spec = pl.BlockSpec((tn, tk), lambda j, i, k: (j, k))
    else:
        a_spec = pl.BlockSpec((tk, tm), lambda j, i, k: (k, i))
        b_spec = pl.BlockSpec((tk, tn), lambda j, i, k: (k, j))
    o_spec = pl.BlockSpec((tm, tn), lambda j, i, k: (i, j))
    in_specs, args = [], []
    for a, b in pairs:
        in_specs += [a_spec, b_spec]
        args += [a, b]
    if residual is not None:
        in_specs.append(o_spec)
        args.append(residual)
    return pl.pallas_call(
        body, name=name, grid=(N // tn, M // tm, nk),
        in_specs=in_specs, out_specs=o_spec,
        out_shape=jax.ShapeDtypeStruct((M, N), out_dtype),
        scratch_shapes=[pltpu.VMEM((tm, tn), F32)],
        compiler_params=_cp(("parallel", "parallel", "arbitrary")),
    )(*args)


def _rms_rows(x):
    return lax.rsqrt(jnp.mean(x * x, axis=-1, keepdims=True) + EPS)


def _norm_fwd(name, x, g):
    T, Dm = x.shape
    tm = _tile(T, 512, 16)

    def body(x_ref, g_ref, o_ref):
        xv = x_ref[...]
        o_ref[...] = ((xv * _rms_rows(xv)) * g_ref[...]).astype(o_ref.dtype)

    return pl.pallas_call(
        body, name=name, grid=(T // tm,),
        in_specs=[pl.BlockSpec((tm, Dm), lambda i: (i, 0)), pl.BlockSpec((1, Dm), lambda i: (0, 0))],
        out_specs=pl.BlockSpec((tm, Dm), lambda i: (i, 0)),
        out_shape=jax.ShapeDtypeStruct((T, Dm), _MXU),
        compiler_params=_cp(("parallel",)),
    )(x, g.reshape(1, Dm))


def _norm_bwd(name, dh, x, g, res):
    T, Dm = x.shape
    tm = _tile(T, 512, 8)

    def body(dh_ref, x_ref, g_ref, res_ref, dx_ref, dg_ref):
        xv, dhv = x_ref[...], dh_ref[...]
        r = _rms_rows(xv)
        xhat = xv * r
        dxhat = dhv * g_ref[...]
        dx_ref[...] = res_ref[...] + r * (dxhat - xhat * jnp.mean(dxhat * xhat, axis=-1, keepdims=True))

        @pl.when(pl.program_id(0) == 0)
        def _():
            dg_ref[...] = jnp.zeros_like(dg_ref)

        dg_ref[...] += jnp.sum(dhv * xhat, axis=0, keepdims=True)

    blk = pl.BlockSpec((tm, Dm), lambda i: (i, 0))
    vec = pl.BlockSpec((1, Dm), lambda i: (0, 0))
    return pl.pallas_call(
        body, name=name, grid=(T // tm,),
        in_specs=[blk, blk, vec, blk], out_specs=[blk, vec],
        out_shape=[jax.ShapeDtypeStruct((T, Dm), F32), jax.ShapeDtypeStruct((1, Dm), F32)],
        compiler_params=_cp(("arbitrary",)),
    )(dh, x, g.reshape(1, Dm), res)


def _gelu(x):
    return jax.nn.gelu(x)


def _gelu_grad(x):
    c0, c1 = math.sqrt(2.0 / math.pi), 0.044715
    t = jnp.tanh(c0 * (x + c1 * x * x * x))
    return 0.5 * (1.0 + t) + 0.5 * x * (1.0 - t * t) * c0 * (1.0 + 3.0 * c1 * x * x)


def _head64_rms(x):
    lo = _lane(x.shape) < FOX_HEAD_DIM
    xx = x * x
    sa = jnp.sum(jnp.where(lo, xx, 0.0), axis=-1, keepdims=True)
    sb = jnp.sum(jnp.where(lo, 0.0, xx), axis=-1, keepdims=True)
    inv = 1.0 / FOX_HEAD_DIM
    return jnp.where(lo, lax.rsqrt(sa * inv + EPS), lax.rsqrt(sb * inv + EPS))


def _head64_mean(x):
    lo = _lane(x.shape) < FOX_HEAD_DIM
    sa = jnp.sum(jnp.where(lo, x, 0.0), axis=-1, keepdims=True)
    sb = jnp.sum(jnp.where(lo, 0.0, x), axis=-1, keepdims=True)
    return jnp.where(lo, sa, sb) * (1.0 / FOX_HEAD_DIM)


def _masked_ws(ws_ref, g):
    w = ws_ref[g]
    keep = (_row(w.shape) // CHUNK) >= (_lane(w.shape) // CHUNK)
    return jnp.where(keep, w, 0.0)


def _even_pre(name, z, g_v, w_s, bs_cols, g_qn2, g_kn2, b_f_pad):
    T = z.shape[0]
    tm = _tile(T, 256, A_BLOCK)
    nb = tm // A_BLOCK

    def body(z_ref, gv_ref, ws_ref, bs_ref, gq_ref, gk_ref, bf_ref, ya_ref, q_ref, k_ref, v_ref, c_ref, ct_ref, carry):
        i = pl.program_id(0)

        @pl.when(i == 0)
        def _():
            carry[...] = jnp.zeros_like(carry)

        wm = [_mx(_masked_ws(ws_ref, g)) for g in range(A_GROUPS)]
        for g in range(A_GROUPS):
            sl = slice(g * A_GROUP_DIM, (g + 1) * A_GROUP_DIM)
            u = _gelu(z_ref[:, sl])
            v = _gelu(z_ref[:, A_WIDTH + g * A_GROUP_DIM:A_WIDTH + (g + 1) * A_GROUP_DIM])
            vn = _mx((v * _rms_rows(v)) * gv_ref[:, sl])
            for n in range(nb):
                rows = slice(n * A_BLOCK, (n + 1) * A_BLOCK)
                s = jnp.dot(wm[g], vn[rows], preferred_element_type=F32) + bs_ref[:, g:g + 1]
                ya_ref[rows, sl] = (u[rows] * s).astype(ya_ref.dtype)
        o = 2 * A_WIDTH
        for j in range(FOX_WIDTH // LANES):
            sl = slice(j * LANES, (j + 1) * LANES)
            q = z_ref[:, o + j * LANES:o + (j + 1) * LANES]
            q_ref[:, sl] = (((q * _head64_rms(q)) * gq_ref[...]) * (1.0 / math.sqrt(FOX_HEAD_DIM))).astype(q_ref.dtype)
            k = z_ref[:, o + FOX_WIDTH + j * LANES:o + FOX_WIDTH + (j + 1) * LANES]
            k_ref[:, sl] = ((k * _head64_rms(k)) * gk_ref[...]).astype(k_ref.dtype)
            v_ref[:, sl] = z_ref[:, o + 2 * FOX_WIDTH + j * LANES:o + 2 * FOX_WIDTH + (j + 1) * LANES].astype(v_ref.dtype)
        zf = z_ref[:, o + 3 * FOX_WIDTH:o + 3 * FOX_WIDTH + LANES]
        logf = jnp.where(_lane(zf.shape) < FOX_HEADS, jax.nn.log_sigmoid(zf + bf_ref[...]), 0.0)
        tri = (_row((tm, tm)) >= _lane((tm, tm))).astype(F32)
        c = _dot_f32(tri, logf) + carry[0:1, :]
        c_ref[...] = c
        ct_ref[...] = c.T[0:SUBLANES, :]
        carry[0:1, :] = c[tm - 1:tm, :]

    wide = lambda n: pl.BlockSpec((tm, n), lambda i: (i, 0))
    full = lambda s: pl.BlockSpec(s, lambda i: tuple(0 for _ in s))
    out512 = jax.ShapeDtypeStruct((T, 512), _MXU)
    return pl.pallas_call(
        body, name=name, grid=(T // tm,),
        in_specs=[wide(EVEN_IN_PAD), full((1, A_WIDTH)), full((A_GROUPS, A_BLOCK, A_BLOCK)), full((A_BLOCK, LANES)),
                  full((1, LANES)), full((1, LANES)), full((1, LANES))],
        out_specs=[wide(512), wide(512), wide(512), wide(512), wide(LANES), pl.BlockSpec((SUBLANES, tm), lambda i: (0, i))],
        out_shape=[out512, out512, out512, out512, jax.ShapeDtypeStruct((T, LANES), F32), jax.ShapeDtypeStruct((SUBLANES, T), F32)],
        scratch_shapes=[pltpu.VMEM((SUBLANES, LANES), F32)],
        compiler_params=_cp(("arbitrary",)),
    )(z, g_v, w_s, bs_cols, g_qn2, g_kn2, b_f_pad)


def _tri_steps(n, by_rows):
    if by_rows:
        pairs = [(q, k) for q in range(n) for k in range(q + 1)]
    else:
        pairs = [(q, k) for k in range(n) for q in range(k, n)]
    return (jnp.asarray(np.array([p[0] for p in pairs], np.int32)), jnp.asarray(np.array([p[1] for p in pairs], np.int32)))


def _fox_logits(q, k, c_q0, c_row, off, tq):
    s = _dot_nt(q, k) + (c_q0 - c_row)
    return jnp.where(off + _row((tq, tq)) >= _lane((tq, tq)), s, -jnp.inf)


def _side_specs(side):
    if side is None:
        return [], [], [], []
    return [pl.BlockSpec(memory_space=pl.ANY)] * side.n, list(side.arrays), list(side.out_shape), list(side.scratch)


def _side_run(side, s_id, nsteps, side_in, side_out, sems):
    if side is None:
        return

    @pl.when(s_id == 0)
    def _():
        side.start(side_in, side_out, sems)

    @pl.when(s_id == nsteps // 2)
    def _():
        side.middle(side_in, side_out, sems)

    @pl.when(s_id == nsteps - 1)
    def _():
        side.finish(side_in, side_out, sems)


def _fox_fwd(name, qn, kn, vv, c, ct, side=None):
    T = qn.shape[0]
    tq = _tile(T, FOX_TILE, LANES)
    nblk = T // tq
    qs, ks = _tri_steps(nblk, True)
    nsteps = int(qs.shape[0])
    npair = FOX_WIDTH // LANES
    s_specs, s_args, s_shapes, s_scratch = _side_specs(side)

    def body(qs_ref, ks_ref, *refs):
        main_in, side_in, main_out, side_out, main_scr, sems = _split_side(refs, 5, 3, 3, side)
        q_ref, k_ref, v_ref, c_ref, ct_ref = main_in
        yb_ref, yf_ref, lse_ref = main_out
        acc, m_s, l_s = main_scr
        s_id = pl.program_id(0)
        qi, ki = qs_ref[s_id], ks_ref[s_id]
        _side_run(side, s_id, nsteps, side_in, side_out, sems)

        @pl.when(ki == 0)
        def _():
            acc[...] = jnp.zeros_like(acc)
            m_s[...] = jnp.full_like(m_s, -jnp.inf)
            l_s[...] = jnp.zeros_like(l_s)

        lo = _lane((tq, LANES)) < FOX_HEAD_DIM

        off = (qi - ki) * tq
        for j in range(npair):
            sl = slice(j * LANES, (j + 1) * LANES)
            q2, k2, v2 = q_ref[:, sl], k_ref[:, sl], v_ref[:, sl]
            outs = []
            for e in range(2):
                h = 2 * j + e
                qh = jnp.where(lo if e == 0 else ~lo, q2, jnp.zeros_like(q2))
                s = _fox_logits(qh, k2, c_ref[0:1, h:h + 1], ct_ref[h:h + 1, :], off, tq)
                m_old = m_s[h]
                m_new = jnp.maximum(m_old, jnp.max(s, axis=-1, keepdims=True))
                alpha = jnp.exp(m_old - m_new)
                p = jnp.exp(s - m_new)
                l_s[h] = alpha * l_s[h] + jnp.sum(p, axis=-1, keepdims=True)
                m_s[h] = m_new
                outs.append((alpha, _dot(p, v2)))
            alpha2 = jnp.where(lo, outs[0][0], outs[1][0])
            acc[:, sl] = alpha2 * acc[:, sl] + jnp.where(lo, outs[0][1], outs[1][1])

        @pl.when(ki == qi)
        def _():
            lse = jnp.zeros((tq, LANES), F32)
            for j in range(npair):
                sl = slice(j * LANES, (j + 1) * LANES)
                inv = jnp.where(lo, 1.0 / l_s[2 * j], 1.0 / l_s[2 * j + 1])
                y = acc[:, sl] * inv
                yf_ref[:, sl] = y
                yb_ref[:, sl] = y.astype(yb_ref.dtype)
            for h in range(FOX_HEADS):
                lse = jnp.where(_lane((tq, LANES)) == h, m_s[h] + jnp.log(l_s[h]), lse)
            lse_ref[...] = lse

    qmap = lambda s, qs_r, ks_r: (qs_r[s], 0)
    kmap = lambda s, qs_r, ks_r: (ks_r[s], 0)
    grid_spec = pltpu.PrefetchScalarGridSpec(
        num_scalar_prefetch=2, grid=(nsteps,),
        in_specs=[pl.BlockSpec((tq, 512), qmap), pl.BlockSpec((tq, 512), kmap), pl.BlockSpec((tq, 512), kmap),
                  pl.BlockSpec((SUBLANES, LANES), lambda s, qs_r, ks_r: (qs_r[s] * (tq // SUBLANES), 0)),
                  pl.BlockSpec((SUBLANES, tq), lambda s, qs_r, ks_r: (0, ks_r[s]))] + s_specs,
        out_specs=[pl.BlockSpec((tq, 512), qmap), pl.BlockSpec((tq, 512), qmap), pl.BlockSpec((tq, LANES), qmap)] + s_specs,
        scratch_shapes=[pltpu.VMEM((tq, 512), F32), pltpu.VMEM((FOX_HEADS, tq, 1), F32), pltpu.VMEM((FOX_HEADS, tq, 1), F32)] + s_scratch,
    )
    res = pl.pallas_call(
        body, name=name, grid_spec=grid_spec,
        out_shape=[jax.ShapeDtypeStruct((T, 512), _MXU), jax.ShapeDtypeStruct((T, 512), F32), jax.ShapeDtypeStruct((T, LANES), F32)] + s_shapes,
        compiler_params=_cp(("arbitrary",)),
    )(qs, ks, qn, kn, vv, c, ct, *s_args)
    return res[:3], res[3:]


def _shift_down(ext, k, tm):
    return pltpu.roll(ext, k, axis=0)[HALO:HALO + tm]


def _pool_p(z_ext, g, tm, pos):
    w = POOL_WINDOWS[g]
    s, span = z_ext, 1
    while span < w:
        s = s + pltpu.roll(s, span, axis=0)
        span *= 2
    cnt = jnp.minimum(pos + 1, w).astype(F32)
    return s[HALO:HALO + tm] / cnt - z_ext[HALO:HALO + tm]


def _odd_fwd(name, z, w_pool, s_pool, conv_w8):
    T = z.shape[0]
    tm = _tile(T, 256, HALO)
    r = tm // HALO

    def body(z_ref, zb_ref, wp_ref, sp_ref, cw_ref, y_ref):
        i = pl.program_id(0)
        first = i == 0
        pos = i * tm + _row((tm, LANES))
        for g in range(len(POOL_WINDOWS)):
            sl = slice(g * LANES, (g + 1) * LANES)
            halo = jnp.where(first, 0.0, zb_ref[:, sl])
            z_ext = jnp.concatenate([halo, z_ref[:, sl]], axis=0)
            p = _pool_p(z_ext, g, tm, pos)
            y_ref[:, sl] = (_dot(p, wp_ref[g]) * sp_ref[:, sl]).astype(y_ref.dtype)
        for j in range(CONV_WIDTH // LANES):
            sl = slice(j * LANES, (j + 1) * LANES)
            hd = slice(POOL_WIDTH + j * LANES, POOL_WIDTH + (j + 1) * LANES)
            gb = slice(POOL_WIDTH + CONV_WIDTH + j * LANES, POOL_WIDTH + CONV_WIDTH + (j + 1) * LANES)
            gc = slice(POOL_WIDTH + 2 * CONV_WIDTH + j * LANES, POOL_WIDTH + 2 * CONV_WIDTH + (j + 1) * LANES)
            xg_b = jnp.where(first, 0.0, zb_ref[:, gc] * zb_ref[:, hd])
            xg = jnp.concatenate([xg_b, z_ref[:, gc] * z_ref[:, hd]], axis=0)
            conv = (cw_ref[0:1, sl] * _shift_down(xg, 2, tm) + cw_ref[1:2, sl] * _shift_down(xg, 1, tm)
                    + cw_ref[2:3, sl] * xg[HALO:HALO + tm])
            y_ref[:, POOL_WIDTH + j * LANES:POOL_WIDTH + (j + 1) * LANES] = (z_ref[:, gb] * conv).astype(y_ref.dtype)

    full = lambda s: pl.BlockSpec(s, lambda i: tuple(0 for _ in s))
    return pl.pallas_call(
        body, name=name, grid=(T // tm,),
        in_specs=[pl.BlockSpec((tm, ODD_IN), lambda i: (i, 0)),
                  pl.BlockSpec((HALO, ODD_IN), lambda i: (jnp.maximum(i * r - 1, 0), 0)),
                  full((4, LANES, LANES)), full((1, POOL_WIDTH)), full((SUBLANES, CONV_WIDTH))],
        out_specs=pl.BlockSpec((tm, 1024), lambda i: (i, 0)),
        out_shape=jax.ShapeDtypeStruct((T, 1024), _MXU),
        compiler_params=_cp(("parallel",)),
    )(z, z, w_pool, s_pool, conv_w8)


def _mem_fwd(name, mem, g_mem, wkv, gk):
    M = mem.shape[0]

    def body(mem_ref, g_ref, w_ref, gk_ref, m_ref, kv_ref, kn_ref, v_ref):
        x = mem_ref[...]
        m = ((x * _rms_rows(x)) * g_ref[...]).astype(m_ref.dtype)
        m_ref[...] = m
        kv = _dot(m, w_ref[...])
        kv_ref[...] = kv
        for h in range(XA_HEADS):
            sl = slice(h * LANES, (h + 1) * LANES)
            k = kv[:, sl]
            kn_ref[:, sl] = ((k * _rms_rows(k)) * gk_ref[...]).astype(kn_ref.dtype)
        v_ref[...] = kv[:, XA_WIDTH:].astype(v_ref.dtype)

    return pl.pallas_call(
        body, name=name,
        out_shape=[jax.ShapeDtypeStruct((M, D_MODEL), _MXU), jax.ShapeDtypeStruct((M, 2 * XA_WIDTH), F32),
                   jax.ShapeDtypeStruct((M, XA_WIDTH), _MXU), jax.ShapeDtypeStruct((M, XA_WIDTH), _MXU)],
        compiler_params=pltpu.CompilerParams(vmem_limit_bytes=VMEM_LIMIT),
    )(mem, g_mem.reshape(1, D_MODEL), wkv, gk.reshape(1, XA_HEAD_DIM))


def _xa_probs(qx, gq, kn_h):
    r = _rms_rows(qx)
    qhat = qx * r
    qn = qhat * gq
    s = _dot_nt(qn, kn_h) * (1.0 / math.sqrt(XA_HEAD_DIM))
    s = s - jnp.max(s, axis=-1, keepdims=True)
    e = jnp.exp(s)
    return qhat, r, qn, e / jnp.sum(e, axis=-1, keepdims=True)


def _xa_fwd(name, qx, gq, kn, vx):
    T, M = qx.shape[0], kn.shape[0]
    tm = _tile(T, 512, 16)

    def body(q_ref, gq_ref, k_ref, v_ref, o_ref):
        for h in range(XA_HEADS):
            sl = slice(h * LANES, (h + 1) * LANES)
            _, _, _, p = _xa_probs(q_ref[:, sl], gq_ref[...], k_ref[:, sl])
            o_ref[:, sl] = _dot(p, v_ref[:, sl]).astype(o_ref.dtype)

    full = lambda s: pl.BlockSpec(s, lambda i: tuple(0 for _ in s))
    return pl.pallas_call(
        body, name=name, grid=(T // tm,),
        in_specs=[pl.BlockSpec((tm, XA_WIDTH), lambda i: (i, 0)), full((1, LANES)), full((M, XA_WIDTH)), full((M, XA_WIDTH))],
        out_specs=pl.BlockSpec((tm, XA_WIDTH), lambda i: (i, 0)),
        out_shape=jax.ShapeDtypeStruct((T, XA_WIDTH), _MXU),
        compiler_params=_cp(("parallel",)),
    )(qx, gq.reshape(1, XA_HEAD_DIM), kn, vx)


def _ffn_up(name, h, wg, wu):
    T, K = h.shape
    N = wg.shape[1]
    tm, tn = _tile(T, 512, 16), _tile(N, 1536, LANES)

    def body(h_ref, wg_ref, wu_ref, a_ref, b_ref, s_ref):
        hv = h_ref[...]
        a, b = _dot(hv, wg_ref[...]), _dot(hv, wu_ref[...])
        a_ref[...] = a
        b_ref[...] = b
        s_ref[...] = (jax.nn.silu(a) * b).astype(s_ref.dtype)

    w_spec = pl.BlockSpec((K, tn), lambda j, i: (0, j))
    o_spec = pl.BlockSpec((tm, tn), lambda j, i: (i, j))
    return pl.pallas_call(
        body, name=name, grid=(N // tn, T // tm),
        in_specs=[pl.BlockSpec((tm, K), lambda j, i: (i, 0)), w_spec, w_spec],
        out_specs=[o_spec, o_spec, o_spec],
        out_shape=[jax.ShapeDtypeStruct((T, N), F32), jax.ShapeDtypeStruct((T, N), F32), jax.ShapeDtypeStruct((T, N), _MXU)],
        compiler_params=_cp(("parallel", "parallel")),
    )(h, wg, wu)


def _loss_grad(name, y, target):
    T, Dm = y.shape
    tm = _tile(T, 512, 8)

    def body(y_ref, t_ref, dy_ref, l_ref):
        e = y_ref[...] - t_ref[...]
        dy_ref[...] = e * (1.0 / Dm)

        @pl.when(pl.program_id(0) == 0)
        def _():
            l_ref[...] = jnp.zeros_like(l_ref)

        l_ref[...] += jnp.sum(e * e, axis=0, keepdims=True) * (0.5 / Dm)

    blk = pl.BlockSpec((tm, Dm), lambda i: (i, 0))
    return pl.pallas_call(
        body, name=name, grid=(T // tm,),
        in_specs=[blk, blk], out_specs=[blk, pl.BlockSpec((1, Dm), lambda i: (0, 0))],
        out_shape=[jax.ShapeDtypeStruct((T, Dm), F32), jax.ShapeDtypeStruct((1, Dm), F32)],
        compiler_params=_cp(("arbitrary",)),
    )(y, target)


def _pad_lanes(v, n=LANES):
    v = v.reshape(1, -1)
    return jnp.pad(v, ((0, 0), (0, n - v.shape[1])))


def _layer_fwd(l, x, mem, P, side=None, on_side=None):
    sv = {"x0": x}
    W = _layer_weights(l, P, "in")
    h1 = _norm_fwd(f"norm_mix{l}", x, W["g_mix"])
    sv["h1"] = h1
    if l % 2 == 0:
        z = _mm(f"e_in{l}", [(h1, W["w_in"])], "nn")
        ya, qn, kn, vv, c, ct = _even_pre(f"even_pre{l}", z, W["g_v"], W["w_s"], W["bs_cols"], W["g_qn2"], W["g_kn2"], W["b_f_pad"])
        (yb, ybf, lse), side_out = _fox_fwd(f"fox_fwd{l}", qn, kn, vv, c, ct, side=side)
        if on_side is not None:
            on_side(side_out)
        W.update(_layer_weights(l, P, "rest"))
        sv.update(z=z, ya=ya, qn=qn, kn=kn, vv=vv, c=c, ct=ct, yb=yb, ybf=ybf, lse=lse)
        x1 = _mm(f"e_out{l}", [(ya, W["w_out"][:A_WIDTH]), (yb, W["w_out"][A_WIDTH:])], "nn", residual=x)
    else:
        W.update(_layer_weights(l, P, "rest"))
        z = _mm(f"o_in{l}", [(h1, W["w_in"])], "nn")
        ycat = _odd_fwd(f"odd_fwd{l}", z, W["w_pool"], W["s_pool"], W["conv_w8"])
        sv.update(z=z, ycat=ycat)
        x1 = _mm(f"o_out{l}", [(ycat, W["w_out"])], "nn", residual=x)
    sv["x1"] = x1
    h2 = _norm_fwd(f"norm_xa{l}", x1, W["g_xa"])
    qx = _mm(f"xa_q{l}", [(h2, W["wq"])], "nn")
    m, kv, kn_x, vx = _mem_fwd(f"mem_fwd{l}", mem, W["g_mem"], W["wkv"], W["gk"])
    o = _xa_fwd(f"xa_fwd{l}", qx, W["gq"], kn_x, vx)
    x2 = _mm(f"xa_o{l}", [(o, W["wo"])], "nn", residual=x1)
    sv.update(h2=h2, qx=qx, m=m, kv=kv, kn_x=kn_x, vx=vx, o=o, x2=x2)
    h3 = _norm_fwd(f"norm_ffn{l}", x2, W["g_ffn"])
    a, b, s = _ffn_up(f"ffn_up{l}", h3, W["w_gate"], W["w_up"])
    x3 = _mm(f"ffn_down{l}", [(s, W["w_down"])], "nn", residual=x2)
    sv.update(h3=h3, a=a, b=b, s=s)
    return x3, sv, W


def _layer_weights(l, P, part):
    i = l // 2
    if part == "in":
        W = {"g_mix": P["g_mix"][l]}
        if l % 2 == 0:
            w_in = _mx(P["e_w_in"][i])
            if w_in.shape[1] < EVEN_IN_PAD:
                w_in = jnp.pad(w_in, ((0, 0), (0, EVEN_IN_PAD - w_in.shape[1])))
            W.update(w_in=w_in, g_v=P["e_g_v"][i].reshape(1, A_WIDTH), w_s=P["e_w_s"][i],
                     bs_cols=jnp.pad(P["e_b_s"][i].T, ((0, 0), (0, LANES - A_GROUPS))),
                     g_qn2=jnp.tile(P["e_g_qn"][i], 2).reshape(1, LANES), g_kn2=jnp.tile(P["e_g_kn"][i], 2).reshape(1, LANES),
                     b_f_pad=_pad_lanes(P["e_b_f"][i]))
        return W
    W = {k: P[k][l] for k in ("g_xa", "g_mem", "g_ffn")}
    W.update(wq=_mx(P["xa_wq"][l]), wkv=_mx(P["xa_wkv"][l]), wo=_mx(P["xa_wo"][l]), gq=P["xa_gq"][l], gk=P["xa_gk"][l],
             w_gate=_mx(P["w_gate"][l]), w_up=_mx(P["w_up"][l]), w_down=_mx(P["w_down"][l]))
    if l % 2 == 0:
        W.update(w_out=_mx(P["e_w_out"][i]))
    else:
        W.update(w_in=_mx(P["o_w_in"][i]), w_pool=_mx(P["o_w_pool"][i]), s_pool=P["o_s_pool"][i].reshape(1, POOL_WIDTH),
                 conv_w8=jnp.pad(P["o_conv_w"][i], ((0, SUBLANES - CONV_K), (0, 0))), w_out=_mx(P["o_w_out"][i]))
    return W


def _ffn_bwd_act(name, g, wd, a, b):
    T, N = a.shape
    K = g.shape[1]
    tm, tn = _tile(T, 512, 16), _tile(N, 1536, LANES)

    def body(g_ref, wd_ref, a_ref, b_ref, da_ref, db_ref):
        ds = _dot_nt(g_ref[...], wd_ref[...])
        av = a_ref[...]
        sig = jax.nn.sigmoid(av)
        da_ref[...] = (ds * b_ref[...] * (sig * (1.0 + av * (1.0 - sig)))).astype(da_ref.dtype)
        db_ref[...] = (ds * (av * sig)).astype(db_ref.dtype)

    o_spec = pl.BlockSpec((tm, tn), lambda j, i: (i, j))
    return pl.pallas_call(
        body, name=name, grid=(N // tn, T // tm),
        in_specs=[pl.BlockSpec((tm, K), lambda j, i: (i, 0)), pl.BlockSpec((tn, K), lambda j, i: (j, 0)), o_spec, o_spec],
        out_specs=[o_spec, o_spec],
        out_shape=[jax.ShapeDtypeStruct((T, N), _MXU), jax.ShapeDtypeStruct((T, N), _MXU)],
        compiler_params=_cp(("parallel", "parallel")),
    )(g, wd, a, b)


def _xa_bwd(name, qx, do, gq, kn, vx):
    T, M = qx.shape[0], kn.shape[0]
    tm = _tile(T, 512, 16)

    def body(q_ref, do_ref, gq_ref, k_ref, v_ref, dq_ref, dk_ref, dv_ref, dg_ref):
        @pl.when(pl.program_id(0) == 0)
        def _():
            dk_ref[...] = jnp.zeros_like(dk_ref)
            dv_ref[...] = jnp.zeros_like(dv_ref)
            dg_ref[...] = jnp.zeros_like(dg_ref)

        gqv = gq_ref[...]
        for h in range(XA_HEADS):
            sl = slice(h * LANES, (h + 1) * LANES)
            qhat, r, qn, p = _xa_probs(q_ref[:, sl], gqv, k_ref[:, sl])
            do_h = do_ref[:, sl]
            dp = _dot_nt(do_h, v_ref[:, sl])
            dv_ref[:, sl] += _dot_tn(p, do_h)
            dsm = p * (dp - jnp.sum(p * dp, axis=-1, keepdims=True)) * (1.0 / math.sqrt(XA_HEAD_DIM))
            dqn = _dot(dsm, k_ref[:, sl])
            dk_ref[:, sl] += _dot_tn(dsm, qn)
            dg_ref[...] += jnp.sum(dqn * qhat, axis=0, keepdims=True)
            dxh = dqn * gqv
            dq_ref[:, sl] = (r * (dxh - qhat * jnp.mean(dxh * qhat, axis=-1, keepdims=True))).astype(dq_ref.dtype)

    full = lambda s: pl.BlockSpec(s, lambda i: tuple(0 for _ in s))
    blk = pl.BlockSpec((tm, XA_WIDTH), lambda i: (i, 0))
    return pl.pallas_call(
        body, name=name, grid=(T // tm,),
        in_specs=[blk, blk, full((1, LANES)), full((M, XA_WIDTH)), full((M, XA_WIDTH))],
        out_specs=[blk, full((M, XA_WIDTH)), full((M, XA_WIDTH)), full((1, LANES))],
        out_shape=[jax.ShapeDtypeStruct((T, XA_WIDTH), _MXU), jax.ShapeDtypeStruct((M, XA_WIDTH), F32),
                   jax.ShapeDtypeStruct((M, XA_WIDTH), F32), jax.ShapeDtypeStruct((1, LANES), F32)],
        compiler_params=_cp(("arbitrary",)),
    )(qx, do, gq.reshape(1, XA_HEAD_DIM), kn, vx)


def _mem_bwd(name, mem, g_mem, m, kv, gk, dkn, dvx, wkv):
    M = mem.shape[0]

    def body(mem_ref, g_ref, m_ref, kv_ref, gk_ref, dkn_ref, dvx_ref, w_ref, dw_ref, dgk_ref, dgm_ref):
        gkv = gk_ref[...]
        dgk = jnp.zeros((1, LANES), F32)
        parts = []
        for h in range(XA_HEADS):
            sl = slice(h * LANES, (h + 1) * LANES)
            k = kv_ref[:, sl]
            r = _rms_rows(k)
            khat = k * r
            dkn_h = dkn_ref[:, sl]
            dgk = dgk + jnp.sum(dkn_h * khat, axis=0, keepdims=True)
            dxh = dkn_h * gkv
            parts.append(r * (dxh - khat * jnp.mean(dxh * khat, axis=-1, keepdims=True)))
        dgk_ref[...] = dgk
        dkv = jnp.concatenate(parts + [dvx_ref[...]], axis=1)
        dw_ref[...] = _dot_tn(m_ref[...], dkv)
        dm = _dot_nt(dkv, w_ref[...])
        x = mem_ref[...]
        dgm_ref[...] = jnp.sum(dm * (x * _rms_rows(x)), axis=0, keepdims=True)

    return pl.pallas_call(
        body, name=name,
        out_shape=[jax.ShapeDtypeStruct((D_MODEL, 2 * XA_WIDTH), F32), jax.ShapeDtypeStruct((1, LANES), F32),
                   jax.ShapeDtypeStruct((1, D_MODEL), F32)],
        compiler_params=pltpu.CompilerParams(vmem_limit_bytes=VMEM_LIMIT),
    )(mem, g_mem.reshape(1, D_MODEL), m, kv, gk.reshape(1, XA_HEAD_DIM), dkn, dvx, wkv)


def _fox_bwd_common(q2, k2, v2, dy2, yf2, c_ref, ct_ref, lse_ref, e, h, lo, off, tq):
    sel = lo if e == 0 else ~lo
    qh = jnp.where(sel, q2, jnp.zeros_like(q2))
    kh = jnp.where(sel, k2, jnp.zeros_like(k2))
    dyh = jnp.where(sel, dy2, 0.0)
    s = _fox_logits(qh, k2, c_ref[0:1, h:h + 1], ct_ref[h:h + 1, :], off, tq)
    p = jnp.exp(s - lse_ref[:, h:h + 1])
    dp = _dot_nt(dyh, v2)
    delta = jnp.sum(dyh * yf2, axis=-1, keepdims=True)
    return p, p * (dp - delta), qh, kh, dyh


def _fox_bwd(name, qn, kn, vv, c, ct, lse, dycat, ybf, side=None):
    T = qn.shape[0]
    tq = _tile(T, FOX_TILE, LANES)
    nblk = T // tq
    qs, ks = _tri_steps(nblk, False)
    nsteps = int(qs.shape[0])
    scale = 1.0 / math.sqrt(FOX_HEAD_DIM)
    s_specs, s_args, s_shapes, s_scratch = _side_specs(side)

    def body(qs_ref, ks_ref, *refs):
        main_in, side_in, main_out, side_out, main_scr, sems = _split_side(refs, 8, 5, 3, side)
        q_ref, k_ref, v_ref, c_ref, ct_ref, lse_ref, dy_ref, yf_ref = main_in
        dq_hbm, drow_hbm, dk_ref, dv_ref, dct_ref = main_out
        dq_acc, drow_acc, out_sem = main_scr
        s_id = pl.program_id(0)
        qi, ki = qs_ref[s_id], ks_ref[s_id]
        _side_run(side, s_id, nsteps, side_in, side_out, sems)
        rows = pl.ds(pl.multiple_of(qi * tq, tq), tq)

        @pl.when(qi == ki)
        def _():
            dk_ref[...] = jnp.zeros_like(dk_ref)
            dv_ref[...] = jnp.zeros_like(dv_ref)
            dct_ref[...] = jnp.zeros_like(dct_ref)

        @pl.when(ki == 0)
        def _():
            dq_acc[rows, :] = jnp.zeros((tq, FOX_WIDTH), F32)
            drow_acc[rows, :] = jnp.zeros((tq, LANES), F32)

        lo = _lane((tq, LANES)) < FOX_HEAD_DIM

        off = (qi - ki) * tq
        drow = jnp.zeros((tq, LANES), F32)
        for j in range(FOX_WIDTH // LANES):
            sl = slice(j * LANES, (j + 1) * LANES)
            q2, k2, v2, dy2, yf2 = q_ref[:, sl], k_ref[:, sl], v_ref[:, sl], dy_ref[:, sl], yf_ref[:, sl]
            dq_t, dk_t, dv_t = None, None, None
            for e in range(2):
                h = 2 * j + e
                p, ds, qh, kh, dyh = _fox_bwd_common(q2, k2, v2, dy2, yf2, c_ref, ct_ref, lse_ref, e, h, lo, off, tq)
                drow = jnp.where(_lane((tq, LANES)) == h, jnp.sum(ds, axis=-1, keepdims=True), drow)
                dct_ref[h:h + 1, :] -= jnp.sum(ds, axis=0, keepdims=True)
                a, b, d = _dot(ds, kh), _dot_tn(ds, qh), _dot_tn(p, dyh)
                dq_t = a if dq_t is None else dq_t + a
                dk_t = b if dk_t is None else dk_t + b
                dv_t = d if dv_t is None else dv_t + d
            dq_acc[rows, sl] += dq_t * scale
            dk_ref[:, sl] += dk_t
            dv_ref[:, sl] += dv_t
        drow_acc[rows, :] += drow

        @pl.when(s_id == nsteps - 1)
        def _():
            for src, dst in ((dq_acc, dq_hbm), (drow_acc, drow_hbm)):
                cp = pltpu.make_async_copy(src, dst, out_sem)
                cp.start()
                cp.wait()

    qmap = lambda s, a, b: (a[s], 0)
    kmap = lambda s, a, b: (b[s], 0)
    ctmap = lambda s, a, b: (0, b[s])
    any_spec = pl.BlockSpec(memory_space=pl.ANY)
    grid_spec = pltpu.PrefetchScalarGridSpec(
        num_scalar_prefetch=2, grid=(nsteps,),
        in_specs=[pl.BlockSpec((tq, 512), qmap), pl.BlockSpec((tq, 512), kmap), pl.BlockSpec((tq, 512), kmap),
                  pl.BlockSpec((SUBLANES, LANES), lambda s, a, b: (a[s] * (tq // SUBLANES), 0)), pl.BlockSpec((SUBLANES, tq), ctmap),
                  pl.BlockSpec((tq, LANES), qmap), pl.BlockSpec((tq, 512), lambda s, a, b: (a[s], 1)), pl.BlockSpec((tq, 512), qmap)] + s_specs,
        out_specs=[any_spec, any_spec, pl.BlockSpec((tq, 512), kmap), pl.BlockSpec((tq, 512), kmap), pl.BlockSpec((SUBLANES, tq), ctmap)] + s_specs,
        scratch_shapes=[pltpu.VMEM((T, FOX_WIDTH), F32), pltpu.VMEM((T, LANES), F32), pltpu.SemaphoreType.DMA] + s_scratch,
    )
    res = pl.pallas_call(
        body, name=name, grid_spec=grid_spec,
        out_shape=[jax.ShapeDtypeStruct((T, 512), F32), jax.ShapeDtypeStruct((T, LANES), F32), jax.ShapeDtypeStruct((T, 512), F32),
                   jax.ShapeDtypeStruct((T, 512), F32), jax.ShapeDtypeStruct((SUBLANES, T), F32)] + s_shapes,
        compiler_params=_cp(("arbitrary",)),
    )(qs, ks, qn, kn, vv, c, ct, lse, dycat, ybf, *s_args)
    return res[:5], res[5:]


def _fold64(row):
    return row + pltpu.roll(row, FOX_HEAD_DIM, axis=1)


def _even_bwd(name, z, dycat, dqn, dkn, dvv, dct, drow, g_v, w_s, bs_cols, g_qn2, g_kn2, b_f_pad):
    T = z.shape[0]
    tm = _tile(T, 256, A_BLOCK)
    nb, nt = tm // A_BLOCK, T // tm

    def body(z_ref, dya_ref, dq_ref, dk_ref, dv_ref, dct_ref, drow_ref, gv_ref, ws_ref, bs_ref, gq_ref, gk_ref, bf_ref,
             dz_ref, dws_ref, dbs_ref, dgv_ref, dgq_ref, dgk_ref, dbf_ref, carry):
        @pl.when(pl.program_id(0) == 0)
        def _():
            for r in (dws_ref, dbs_ref, dgv_ref, dgq_ref, dgk_ref, dbf_ref, carry):
                r[...] = jnp.zeros_like(r)

        keep = (_row((A_BLOCK, A_BLOCK)) // CHUNK) >= (_lane((A_BLOCK, A_BLOCK)) // CHUNK)
        for g in range(A_GROUPS):
            sl = slice(g * A_GROUP_DIM, (g + 1) * A_GROUP_DIM)
            wm = _mx(_masked_ws(ws_ref, g))
            zu = z_ref[:, sl]
            zv = z_ref[:, A_WIDTH + g * A_GROUP_DIM:A_WIDTH + (g + 1) * A_GROUP_DIM]
            u, v = _gelu(zu), _gelu(zv)
            r = _rms_rows(v)
            vhat = v * r
            gvv = gv_ref[:, sl]
            vn = _mx(vhat * gvv)
            dya = dya_ref[:, sl]
            du_parts, dvn_parts = [], []
            for n in range(nb):
                rows = slice(n * A_BLOCK, (n + 1) * A_BLOCK)
                s = jnp.dot(wm, vn[rows], preferred_element_type=F32) + bs_ref[:, g:g + 1]
                du_parts.append(dya[rows] * s)
                d_s = dya[rows] * u[rows]
                dbs_ref[...] += jnp.where(_lane((A_BLOCK, LANES)) == g, jnp.sum(d_s, axis=-1, keepdims=True), 0.0)
                dws_ref[g] += jnp.where(keep, _dot_nt(d_s, vn[rows]), 0.0)
                dvn_parts.append(_dot_tn(wm, d_s))
            dvn = jnp.concatenate(dvn_parts, axis=0)
            dgv_ref[:, sl] += jnp.sum(dvn * vhat, axis=0, keepdims=True)
            dxh = dvn * gvv
            dv = r * (dxh - vhat * jnp.mean(dxh * vhat, axis=-1, keepdims=True))
            dz_ref[:, sl] = (jnp.concatenate(du_parts, axis=0) * _gelu_grad(zu)).astype(dz_ref.dtype)
            dz_ref[:, A_WIDTH + g * A_GROUP_DIM:A_WIDTH + (g + 1) * A_GROUP_DIM] = (dv * _gelu_grad(zv)).astype(dz_ref.dtype)
        o = 2 * A_WIDTH
        for j in range(FOX_WIDTH // LANES):
            sl = slice(j * LANES, (j + 1) * LANES)
            for (off, d_ref, g_ref, dg_ref) in ((o, dq_ref, gq_ref, dgq_ref), (o + FOX_WIDTH, dk_ref, gk_ref, dgk_ref)):
                zq = z_ref[:, off + j * LANES:off + (j + 1) * LANES]
                r = _head64_rms(zq)
                qhat = zq * r
                dn = d_ref[:, sl]
                dg_ref[...] += _fold64(jnp.sum(dn * qhat, axis=0, keepdims=True))
                dxh = dn * g_ref[...]
                dz_ref[:, off + j * LANES:off + (j + 1) * LANES] = (r * (dxh - qhat * _head64_mean(dxh * qhat))).astype(dz_ref.dtype)
            dz_ref[:, o + 2 * FOX_WIDTH + j * LANES:o + 2 * FOX_WIDTH + (j + 1) * LANES] = dv_ref[:, sl].astype(dz_ref.dtype)
        dct_v = dct_ref[...] + drow_ref[...].T[0:SUBLANES, :]
        upper = (_row((tm, tm)) >= _lane((tm, tm))).astype(F32)
        dlf_t = _dot_f32(dct_v, upper) + carry[:, 0:1]
        carry[...] += jnp.sum(dct_v, axis=1, keepdims=True)
        dlf = jnp.concatenate([dlf_t, jnp.zeros((LANES - SUBLANES, tm), F32)], axis=0).T
        zf = z_ref[:, o + 3 * FOX_WIDTH:o + 3 * FOX_WIDTH + LANES]
        dzf = dlf * jax.nn.sigmoid(-(zf + bf_ref[...]))
        dbf_ref[...] += jnp.sum(dzf, axis=0, keepdims=True)
        dz_ref[:, o + 3 * FOX_WIDTH:o + 3 * FOX_WIDTH + LANES] = dzf.astype(dz_ref.dtype)

    rev = lambda i: nt - 1 - i
    wide = lambda n, col=0: pl.BlockSpec((tm, n), lambda i: (rev(i), col))
    full = lambda s: pl.BlockSpec(s, lambda i: tuple(0 for _ in s))
    vec = full((1, LANES))
    return pl.pallas_call(
        body, name=name, grid=(nt,),
        in_specs=[wide(EVEN_IN_PAD), wide(512), wide(512), wide(512), wide(512), pl.BlockSpec((SUBLANES, tm), lambda i: (0, rev(i))),
                  wide(LANES), full((1, A_WIDTH)), full((A_GROUPS, A_BLOCK, A_BLOCK)), full((A_BLOCK, LANES)), vec, vec, vec],
        out_specs=[wide(EVEN_IN_PAD), full((A_GROUPS, A_BLOCK, A_BLOCK)), full((A_BLOCK, LANES)), full((1, A_WIDTH)), vec, vec, vec],
        out_shape=[jax.ShapeDtypeStruct((T, EVEN_IN_PAD), _MXU), jax.ShapeDtypeStruct((A_GROUPS, A_BLOCK, A_BLOCK), F32),
                   jax.ShapeDtypeStruct((A_BLOCK, LANES), F32), jax.ShapeDtypeStruct((1, A_WIDTH), F32),
                   jax.ShapeDtypeStruct((1, LANES), F32), jax.ShapeDtypeStruct((1, LANES), F32), jax.ShapeDtypeStruct((1, LANES), F32)],
        scratch_shapes=[pltpu.VMEM((SUBLANES, LANES), F32)],
        compiler_params=_cp(("arbitrary",)),
    )(z, dycat, dqn, dkn, dvv, dct, drow, g_v, w_s, bs_cols, g_qn2, g_kn2, b_f_pad)


def _shift_up(ext, k, tm):
    return pltpu.roll(ext, ext.shape[0] - k, axis=0)[0:tm]


def _odd_bwd(name, z, dycat, w_pool, s_pool, conv_w8):
    T = z.shape[0]
    tm = _tile(T, 256, HALO)
    r, nt = tm // HALO, T // tm
    n_ext = tm + HALO

    def body(z_ref, zb_ref, zn_ref, dy_ref, dyn_ref, wp_ref, sp_ref, cw_ref, dz_ref, dwp_ref, dsp_ref, dcw_ref):
        i = pl.program_id(0)

        @pl.when(i == 0)
        def _():
            for rr in (dwp_ref, dsp_ref, dcw_ref):
                rr[...] = jnp.zeros_like(rr)

        first, last = i == 0, i == nt - 1
        pos = i * tm + _row((tm, LANES))
        pos_ext = i * tm + _row((n_ext, LANES))
        for g, w in enumerate(POOL_WINDOWS):
            sl = slice(g * LANES, (g + 1) * LANES)
            z_ext = jnp.concatenate([jnp.where(first, 0.0, zb_ref[:, sl]), z_ref[:, sl]], axis=0)
            p = _pool_p(z_ext, g, tm, pos)
            spv = sp_ref[:, sl]
            dyc = dy_ref[:, sl]
            dsp_ref[:, sl] += jnp.sum(dyc * _dot(p, wp_ref[g]), axis=0, keepdims=True)
            dpw_ext = jnp.concatenate([dyc, jnp.where(last, 0.0, dyn_ref[:, sl])], axis=0) * spv
            dwp_ref[g] += _dot_tn(p, dpw_ext[0:tm])
            dp_ext = _dot_nt(dpw_ext, wp_ref[g])
            f = dp_ext / jnp.minimum(pos_ext + 1, w).astype(F32)
            span = 1
            while span < w:
                f = f + pltpu.roll(f, n_ext - span, axis=0)
                span *= 2
            dz_ref[:, sl] = (f[0:tm] - dp_ext[0:tm]).astype(dz_ref.dtype)
        for j in range(CONV_WIDTH // LANES):
            sl = slice(j * LANES, (j + 1) * LANES)
            hd = slice(POOL_WIDTH + j * LANES, POOL_WIDTH + (j + 1) * LANES)
            gb = slice(POOL_WIDTH + CONV_WIDTH + j * LANES, POOL_WIDTH + CONV_WIDTH + (j + 1) * LANES)
            gc = slice(POOL_WIDTH + 2 * CONV_WIDTH + j * LANES, POOL_WIDTH + 2 * CONV_WIDTH + (j + 1) * LANES)
            ysl = slice(POOL_WIDTH + j * LANES, POOL_WIDTH + (j + 1) * LANES)
            hdv, gbv, gcv = z_ref[:, hd], z_ref[:, gb], z_ref[:, gc]
            xg = gcv * hdv
            xg_ext = jnp.concatenate([jnp.where(first, 0.0, zb_ref[:, gc] * zb_ref[:, hd]), xg], axis=0)
            xg1, xg2 = _shift_down(xg_ext, 1, tm), _shift_down(xg_ext, 2, tm)
            w0, w1, w2 = cw_ref[0:1, sl], cw_ref[1:2, sl], cw_ref[2:3, sl]
            conv = w0 * xg2 + w1 * xg1 + w2 * xg
            dyd = dy_ref[:, ysl]
            dconv = dyd * gbv
            dconv_ext = jnp.concatenate([dconv, jnp.where(last, 0.0, dyn_ref[:, ysl] * zn_ref[:, gb])], axis=0)
            dcw_ref[0:1, sl] += jnp.sum(dconv * xg2, axis=0, keepdims=True)
            dcw_ref[1:2, sl] += jnp.sum(dconv * xg1, axis=0, keepdims=True)
            dcw_ref[2:3, sl] += jnp.sum(dconv * xg, axis=0, keepdims=True)
            dxg = w2 * dconv + w1 * _shift_up(dconv_ext, 1, tm) + w0 * _shift_up(dconv_ext, 2, tm)
            dz_ref[:, hd] = (dxg * gcv).astype(dz_ref.dtype)
            dz_ref[:, gb] = (dyd * conv).astype(dz_ref.dtype)
            dz_ref[:, gc] = (dxg * hdv).astype(dz_ref.dtype)

    full = lambda s: pl.BlockSpec(s, lambda i: tuple(0 for _ in s))
    back = lambda n: pl.BlockSpec((HALO, n), lambda i: (jnp.maximum(i * r - 1, 0), 0))
    nxt = lambda n: pl.BlockSpec((HALO, n), lambda i: (jnp.minimum((i + 1) * r, T // HALO - 1), 0))
    return pl.pallas_call(
        body, name=name, grid=(nt,),
        in_specs=[pl.BlockSpec((tm, ODD_IN), lambda i: (i, 0)), back(ODD_IN), nxt(ODD_IN),
                  pl.BlockSpec((tm, 1024), lambda i: (i, 0)), nxt(1024),
                  full((4, LANES, LANES)), full((1, POOL_WIDTH)), full((SUBLANES, CONV_WIDTH))],
        out_specs=[pl.BlockSpec((tm, ODD_IN), lambda i: (i, 0)), full((4, LANES, LANES)), full((1, POOL_WIDTH)), full((SUBLANES, CONV_WIDTH))],
        out_shape=[jax.ShapeDtypeStruct((T, ODD_IN), _MXU), jax.ShapeDtypeStruct((4, LANES, LANES), F32),
                   jax.ShapeDtypeStruct((1, POOL_WIDTH), F32), jax.ShapeDtypeStruct((SUBLANES, CONV_WIDTH), F32)],
        compiler_params=_cp(("arbitrary",)),
    )(z, z, z, dycat, dycat, w_pool, s_pool, conv_w8)


def _layer_bwd(l, g3, mem, W, sv, side_fn=None):
    i = l // 2
    G = {}
    side_out = ()
    da, db = _ffn_bwd_act(f"ffn_bwd_act{l}", g3, W["w_down"], sv["a"], sv["b"])
    G["w_down"] = _mm(f"d_w_down{l}", [(sv["s"], g3)], "tn")
    dh3 = _mm(f"d_h3{l}", [(da, W["w_gate"]), (db, W["w_up"])], "nt")
    G["w_gate"] = _mm(f"d_w_gate{l}", [(sv["h3"], da)], "tn")
    G["w_up"] = _mm(f"d_w_up{l}", [(sv["h3"], db)], "tn")
    g2, dg = _norm_bwd(f"norm_ffn_bwd{l}", dh3, sv["x2"], W["g_ffn"], g3)
    G["g_ffn"] = dg[0]

    do = _mm(f"d_o{l}", [(g2, W["wo"])], "nt", out_dtype=_MXU)
    G["xa_wo"] = _mm(f"d_wo{l}", [(sv["o"], g2)], "tn")
    dqx, dkn, dvx, dgq = _xa_bwd(f"xa_bwd{l}", sv["qx"], do, W["gq"], sv["kn_x"], sv["vx"])
    G["xa_gq"] = dgq[0]
    dh2 = _mm(f"d_h2{l}", [(dqx, W["wq"])], "nt")
    G["xa_wq"] = _mm(f"d_wq{l}", [(sv["h2"], dqx)], "tn")
    g1, dg = _norm_bwd(f"norm_xa_bwd{l}", dh2, sv["x1"], W["g_xa"], g2)
    G["g_xa"] = dg[0]
    dwkv, dgk, dgm = _mem_bwd(f"mem_bwd{l}", mem, W["g_mem"], sv["m"], sv["kv"], W["gk"], dkn, dvx, W["wkv"])
    G["xa_wkv"], G["xa_gk"], G["g_mem"] = dwkv, dgk[0], dgm[0]

    dycat = _mm(f"d_ycat{l}", [(g1, W["w_out"])], "nt")
    if l % 2 == 0:
        G["e_w_out"] = jnp.concatenate([_mm(f"d_e_wout_a{l}", [(sv["ya"], g1)], "tn"), _mm(f"d_e_wout_b{l}", [(sv["yb"], g1)], "tn")], axis=0)
        fa = (sv["qn"], sv["kn"], sv["vv"], sv["c"], sv["ct"], sv["lse"], dycat, sv["ybf"])
        (dqn, drow, dkn_f, dvv, dct), side_out = _fox_bwd(f"fox_bwd{l}", *fa, side=side_fn(G) if side_fn is not None else None)
        dz, dws, dbs, dgv, dgq2, dgk2, dbf = _even_bwd(f"even_bwd{l}", sv["z"], dycat, dqn, dkn_f, dvv, dct, drow, W["g_v"], W["w_s"],
                                                      W["bs_cols"], W["g_qn2"], W["g_kn2"], W["b_f_pad"])
        G.update(e_w_s=dws, e_b_s=dbs[:, :A_GROUPS].T, e_g_v=dgv[0], e_g_qn=dgq2[0, :FOX_HEAD_DIM], e_g_kn=dgk2[0, :FOX_HEAD_DIM],
                 e_b_f=dbf[0, :FOX_HEADS])
        G["e_w_in"] = _mm(f"d_e_win{l}", [(sv["h1"], dz)], "tn")
    else:
        G["o_w_out"] = _mm(f"d_o_wout{l}", [(sv["ycat"], g1)], "tn")
        dz, dwp, dsp, dcw = _odd_bwd(f"odd_bwd{l}", sv["z"], dycat, W["w_pool"], W["s_pool"], W["conv_w8"])
        G.update(o_w_pool=dwp, o_s_pool=dsp[0], o_conv_w=dcw[:CONV_K])
        G["o_w_in"] = _mm(f"d_o_win{l}", [(sv["h1"], dz)], "tn")
    dh1 = _mm(f"d_h1{l}", [(dz, W["w_in"])], "nt")
    g0, dg = _norm_bwd(f"norm_mix_bwd{l}", dh1, sv["x0"], W["g_mix"], g1)
    G["g_mix"] = dg[0]
    return g0, G, side_out


PACK_W = 1024
MESH_T = pl.DeviceIdType.MESH
_ANY = pl.BlockSpec(memory_space=pl.ANY)


def _my_place():
    x, y, c = lax.axis_index("x"), lax.axis_index("y"), lax.axis_index("c")
    return x, y, c


def _flip(v, bit):
    return 1 - v if bit else v


N_PEERS = N_DEV - 1


def _all_gather(name, blks):
    side = _TwoLevelGather(blks)

    def body(*refs):
        _, side_in, _, side_out, _, sems = _split_side(refs, 0, 0, 0, side)
        side.start(side_in, side_out, sems)
        side.middle(side_in, side_out, sems)
        side.finish(side_in, side_out, sems)

    return pl.pallas_call(
        body, name=name, out_shape=side.out_shape,
        in_specs=[_ANY] * side.n, out_specs=[_ANY] * side.n, scratch_shapes=side.scratch,
    )(*blks)


class _Direct:
    def __init__(self, kind, arrays):
        self.kind, self.arrays, self.n = kind, list(arrays), len(arrays)
        if kind == "scatter":
            self.out_shape = [jax.ShapeDtypeStruct(a.shape, a.dtype) for a in arrays]
        else:
            self.out_shape = [jax.ShapeDtypeStruct((N_DEV,) + a.shape, a.dtype) for a in arrays]
        self.scratch = [pltpu.SemaphoreType.DMA((N_PEERS * self.n,)), pltpu.SemaphoreType.DMA((N_PEERS * self.n,)),
                        pltpu.SemaphoreType.DMA((self.n,))]

    def _copies(self, in_refs, out_refs, sems):
        send_sems, recv_sems, local_sems = sems
        x, y, c = _my_place()
        me = 4 * x + 2 * y + c
        mine, sends, recvs = [], [], []
        for t in range(self.n):
            src_of = (lambda idx, t=t: in_refs[t].at[idx]) if self.kind == "scatter" else (lambda idx, t=t: in_refs[t])
            mine.append(pltpu.make_async_copy(src_of(me), out_refs[t].at[me], local_sems.at[t]))
            for m in range(1, N_DEV):
                px, py, pc = _flip(x, m & 4), _flip(y, m & 2), _flip(c, m & 1)
                pidx = 4 * px + 2 * py + pc
                sem = dict(send_sem=send_sems.at[N_PEERS * t + m - 1], recv_sem=recv_sems.at[N_PEERS * t + m - 1], device_id_type=MESH_T)
                sends.append(pltpu.make_async_remote_copy(src_ref=src_of(pidx), dst_ref=out_refs[t].at[me], device_id=(px, py, pc), **sem))
                recvs.append(pltpu.make_async_remote_copy(src_ref=src_of(pidx), dst_ref=out_refs[t].at[pidx], device_id=(x, y, c), **sem))
        return mine, sends, recvs

    def start(self, in_refs, out_refs, sems):
        mine, sends, _ = self._copies(in_refs, out_refs, sems)
        for cp in mine + sends:
            cp.start()

    def middle(self, in_refs, out_refs, sems):
        pass

    def finish(self, in_refs, out_refs, sems):
        mine, sends, recvs = self._copies(in_refs, out_refs, sems)
        for cp in recvs:
            cp.wait_recv()
        for cp in sends:
            cp.wait_send()
        for cp in mine:
            cp.wait()


class _TwoLevelGather:
    def __init__(self, arrays):
        self.arrays, self.n = list(arrays), len(arrays)
        self.out_shape = [jax.ShapeDtypeStruct((N_DEV,) + a.shape, a.dtype) for a in arrays]
        self.scratch = [pltpu.SemaphoreType.DMA((N_PEERS * self.n,)), pltpu.SemaphoreType.DMA((N_PEERS * self.n,)),
                        pltpu.SemaphoreType.DMA((self.n,))]

    def _place(self):
        x, y, c = _my_place()
        return (x, y, c), (x, y, 1 - c), [(1 - x, y), (x, 1 - y), (1 - x, 1 - y)]

    def _copy(self, in_refs, out_refs, sems, t, k, block, to, own=False):
        slab = out_refs[t].at[4 * block[0] + 2 * block[1] + block[2]]
        return pltpu.make_async_remote_copy(
            src_ref=in_refs[t] if own else slab, dst_ref=slab, send_sem=sems[0].at[N_PEERS * t + k],
            recv_sem=sems[1].at[N_PEERS * t + k], device_id=to, device_id_type=MESH_T)

    def _mine(self, in_refs, out_refs, sems):
        x, y, c = _my_place()
        return [pltpu.make_async_copy(in_refs[t], out_refs[t].at[4 * x + 2 * y + c], sems[2].at[t]) for t in range(self.n)]

    def _first(self, in_refs, out_refs, sems):
        me, sibling, chips = self._place()
        cps = []
        for t in range(self.n):
            cps.append(self._copy(in_refs, out_refs, sems, t, 0, me, sibling, own=True))
            cps += [self._copy(in_refs, out_refs, sems, t, 1 + j, me, (*chip, me[2]), own=True) for j, chip in enumerate(chips)]
        return cps

    def _passed(self, in_refs, out_refs, sems):
        me, sibling, chips = self._place()
        return [self._copy(in_refs, out_refs, sems, t, 4 + j, (*chip, me[2]), sibling) for j, chip in enumerate(chips) for t in range(self.n)]

    def start(self, in_refs, out_refs, sems):
        for cp in self._mine(in_refs, out_refs, sems) + self._first(in_refs, out_refs, sems):
            cp.start()

    def middle(self, in_refs, out_refs, sems):
        me, _, chips = self._place()
        for j, chip in enumerate(chips):
            for t in range(self.n):
                self._copy(in_refs, out_refs, sems, t, 1 + j, (*chip, me[2]), me).wait_recv()
        for cp in self._passed(in_refs, out_refs, sems):
            cp.start()

    def finish(self, in_refs, out_refs, sems):
        me, sibling, chips = self._place()
        for t in range(self.n):
            self._copy(in_refs, out_refs, sems, t, 0, sibling, me).wait_recv()
            for j, chip in enumerate(chips):
                self._copy(in_refs, out_refs, sems, t, 4 + j, (*chip, 1 - me[2]), me).wait_recv()
        for cp in self._first(in_refs, out_refs, sems) + self._passed(in_refs, out_refs, sems):
            cp.wait_send()
        for cp in self._mine(in_refs, out_refs, sems):
            cp.wait()


def _split_side(refs, n_main_in, n_main_out, n_main_scratch, side):
    ns = side.n if side is not None else 0
    i = 0
    main_in = refs[i:i + n_main_in]; i += n_main_in
    side_in = refs[i:i + ns]; i += ns
    main_out = refs[i:i + n_main_out]; i += n_main_out
    side_out = refs[i:i + ns]; i += ns
    main_scr = refs[i:i + n_main_scratch]; i += n_main_scratch
    return main_in, side_in, main_out, side_out, main_scr, refs[i:]


def _all_to_all(name, gs):
    side = _Direct("scatter", gs)

    def body(*refs):
        _, side_in, _, side_out, _, sems = _split_side(refs, 0, 0, 0, side)
        side.start(side_in, side_out, sems)
        side.finish(side_in, side_out, sems)

    return pl.pallas_call(
        body, name=name, out_shape=side.out_shape,
        in_specs=[_ANY] * side.n, out_specs=[_ANY] * side.n, scratch_shapes=side.scratch,
    )(*gs)


def _unblock(name, blocks, layer, n_out):
    _, _, K, nb = blocks.shape
    tk = _tile(K, 256, 16)

    def body(b_ref, o_ref):
        for j in range(N_DEV):
            o_ref[:, j * nb:(j + 1) * nb] = b_ref[j]
        if n_out > N_DEV * nb:
            o_ref[:, N_DEV * nb:n_out] = jnp.zeros((tk, n_out - N_DEV * nb), o_ref.dtype)

    return pl.pallas_call(
        body, name=name, grid=(K // tk,),
        in_specs=[pl.BlockSpec((N_DEV, None, tk, nb), lambda i: (0, layer, i, 0))],
        out_specs=pl.BlockSpec((tk, n_out), lambda i: (i, 0)),
        out_shape=jax.ShapeDtypeStruct((K, n_out), blocks.dtype),
        compiler_params=_cp(("parallel",)),
    )(blocks)


def _block(name, full, nb, dtype):
    K, N = full.shape
    tk = _tile(K, 256, 16)

    def body(f_ref, o_ref):
        for j in range(N_DEV):
            o_ref[j] = f_ref[:, j * nb:(j + 1) * nb].astype(o_ref.dtype)

    return pl.pallas_call(
        body, name=name, grid=(K // tk,),
        in_specs=[pl.BlockSpec((tk, N), lambda i: (i, 0))],
        out_specs=pl.BlockSpec((N_DEV, tk, nb), lambda i: (0, i, 0)),
        out_shape=jax.ShapeDtypeStruct((N_DEV, K, nb), dtype),
        compiler_params=_cp(("parallel",)),
    )(full)


ADAM_BLOCK_ELEMS = 256 * 1024
GRAD_WIRE = jnp.bfloat16


def _adamw(name, parts, w, m, v, layer=0):
    shape = w.shape[1:]
    cols = shape[-1]
    rows = math.prod(shape[:-1])
    nl = w.shape[0]
    parts, w, m, v = parts.reshape(N_DEV, rows, cols), w.reshape(nl, rows, cols), m.reshape(nl, rows, cols), v.reshape(nl, rows, cols)
    tr = _tile(rows, max(16, ADAM_BLOCK_ELEMS // cols), 16)

    def body(p_ref, w_ref, m_ref, v_ref, g_ref, d_ref, mo_ref, vo_ref):
        g = p_ref[0].astype(F32)
        for s in range(1, N_DEV):
            g = g + p_ref[s].astype(F32)
        mn = ADAM_B1 * m_ref[...] + (1.0 - ADAM_B1) * g
        vn = ADAM_B2 * v_ref[...] + (1.0 - ADAM_B2) * jnp.square(g)
        m_hat = mn / (1.0 - ADAM_B1 ** ADAM_STEP)
        v_hat = vn / (1.0 - ADAM_B2 ** ADAM_STEP)
        g_ref[...] = g
        d_ref[...] = -ADAM_LR * (m_hat / (jnp.sqrt(v_hat) + ADAM_EPS) + ADAM_WD * w_ref[...])
        mo_ref[...] = mn
        vo_ref[...] = vn

    blk = pl.BlockSpec((tr, cols), lambda i: (i, 0))
    lblk = pl.BlockSpec((None, tr, cols), lambda i: (layer, i, 0))
    shp = jax.ShapeDtypeStruct((rows, cols), F32)
    res = pl.pallas_call(
        body, name=name, grid=(rows // tr,),
        in_specs=[pl.BlockSpec((N_DEV, tr, cols), lambda i: (0, i, 0)), lblk, lblk, lblk],
        out_specs=[blk, blk, blk, blk], out_shape=[shp, shp, shp, shp],
        compiler_params=_cp(("parallel",)),
    )(parts, w, m, v)
    return [r.reshape(shape) for r in res]


SHARDED = dict(xa_wq=1, xa_wkv=1, xa_wo=2, w_gate=2, w_up=2, w_down=1, e_w_in=2, e_w_out=1, o_w_in=2, o_s_pool=1, o_conv_w=2, o_w_out=1)
LAYER_SPLIT = ("xa_wq", "xa_wkv", "xa_wo", "w_gate", "w_up", "w_down")
ELEMENTWISE_SHARDED = ("o_s_pool", "o_conv_w")
REPLICATED = ("g_mix", "g_xa", "g_mem", "xa_gq", "xa_gk", "g_ffn", "e_b_f", "e_g_v", "e_w_s", "e_b_s", "e_g_qn", "e_g_kn", "o_w_pool")
WEIGHTS = ("g_mix", "g_xa", "g_mem", "xa_wq", "xa_wkv", "xa_wo", "xa_gq", "xa_gk", "g_ffn", "w_gate", "w_up", "w_down", "e_w_in", "e_b_f",
           "e_g_v", "e_w_s", "e_b_s", "e_g_qn", "e_g_kn", "e_w_out", "o_w_in", "o_w_pool", "o_s_pool", "o_conv_w", "o_w_out")


def _rows_for(n, mult):
    return -(-n // (PACK_W * mult)) * mult


def _pack(arrs, rows, dtype):
    flat = jnp.concatenate([a.reshape(-1).astype(dtype) for a in arrs])
    return jnp.pad(flat, (0, rows * PACK_W - flat.shape[0])).reshape(rows, PACK_W)


def _unpack(slab, shapes):
    flat, out, off = slab.reshape(-1), [], 0
    for s in shapes:
        n = math.prod(s)
        out.append(flat[off:off + n].reshape(s))
        off += n
    return out


def _to_blocks(full, axis):
    s = full.shape
    return jnp.moveaxis(full.reshape(s[:axis] + (N_DEV, s[axis] // N_DEV) + s[axis + 1:]), axis, 0)


def _from_blocks(blocks, axis):
    b = jnp.moveaxis(blocks, 0, axis)
    s = b.shape
    return b.reshape(s[:axis] + (s[axis] * s[axis + 1],) + s[axis + 2:])


def kernel(x, mem, g_mix, g_xa, g_mem, xa_wq, xa_wkv, xa_wo, xa_gq, xa_gk, g_ffn, w_gate, w_up, w_down, e_w_in, e_b_f, e_g_v, e_w_s, e_b_s, e_g_qn, e_g_kn, e_w_out, o_w_in, o_w_pool, o_s_pool, o_conv_w, o_w_out, loss_target, m_g_mix, m_g_xa, m_g_mem, m_xa_wq, m_xa_wkv, m_xa_wo, m_xa_gq, m_xa_gk, m_g_ffn, m_w_gate, m_w_up, m_w_down, m_e_w_in, m_e_b_f, m_e_g_v, m_e_w_s, m_e_b_s, m_e_g_qn, m_e_g_kn, m_e_w_out, m_o_w_in, m_o_w_pool, m_o_s_pool, m_o_conv_w, m_o_w_out, v_g_mix, v_g_xa, v_g_mem, v_xa_wq, v_xa_wkv, v_xa_wo, v_xa_gq, v_xa_gk, v_g_ffn, v_w_gate, v_w_up, v_w_down, v_e_w_in, v_e_b_f, v_e_g_v, v_e_w_s, v_e_b_s, v_e_g_qn, v_e_g_kn, v_e_w_out, v_o_w_in, v_o_w_pool, v_o_s_pool, v_o_conv_w, v_o_w_out):
    args = dict(locals())
    Wt = {n: args[n] for n in WEIGHTS}
    Mo = {n: args["m_" + n] for n in WEIGHTS}
    Vo = {n: args["v_" + n] for n in WEIGHTS}
    rep_names = REPLICATED
    rep_shapes = [Wt[n].shape for n in rep_names]
    rep_rows = _rows_for(sum(math.prod(s) for s in rep_shapes) + 1, 16)
    layer_names = [LAYER_SPLIT + ("e_w_in", "e_w_out"), LAYER_SPLIT + ("o_w_in", "o_w_out", "o_s_pool", "o_conv_w")]

    def shard_of(n, l, src=Wt):
        return src[n][l:l + 1] if n in LAYER_SPLIT else src[n]

    def to_wire(n, a):
        return a if n in ELEMENTWISE_SHARDED else _mx(a)

    def full_weight(n, l, gn):
        if SHARDED[n] == 1 or n in ELEMENTWISE_SHARDED:
            return _from_blocks(gn, SHARDED[n])[0]
        return _unblock(f"unblock_{n}{l}", gn, 0, EVEN_IN_PAD if n == "e_w_in" else N_DEV * gn.shape[-1])

    def grad_slabs(n, l, gl):
        if n in ELEMENTWISE_SHARDED:
            return _to_blocks(gl.reshape((1,) + Wt[n].shape[1:-1] + (-1,)), SHARDED[n])
        if SHARDED[n] == 1:
            return _to_blocks(gl[None], 1).astype(GRAD_WIRE)
        return _block(f"block_{n}{l}", gl, Wt[n].shape[-1], GRAD_WIRE)[:, None]

    xk, mem2 = x[0], mem[0]
    later = [(n, 0) for n in layer_names[0] if n != "e_w_in"] + [(n, 1) for n in layer_names[1]]
    w_in0 = full_weight("e_w_in", 0, _all_gather("gather_e_w_in", [to_wire("e_w_in", Wt["e_w_in"])])[0])
    side = _TwoLevelGather([to_wire(n, shard_of(n, l)) for n, l in later])
    P = {n: Wt[n] for n in rep_names}
    P["e_w_in"] = {0: w_in0}

    def on_side(gathered):
        for (n, l), gn in zip(later, gathered):
            P.setdefault(n, {})[l if n in LAYER_SPLIT else 0] = full_weight(n, l, gn)

    xk, sv0, W0 = _layer_fwd(0, xk, mem2, P, side=side, on_side=on_side)
    xk, sv1, W1 = _layer_fwd(1, xk, mem2, P)

    g, loss_row = _loss_grad("loss_grad", xk, loss_target[0])
    g, G1, _ = _layer_bwd(1, g, mem2, W1, sv1)
    slabs1 = [grad_slabs(n, 1, G1[n]) for n in layer_names[1]]
    g, G0, beside = _layer_bwd(0, g, mem2, W0, sv0,
                               side_fn=lambda G: _Direct("scatter", slabs1 + [grad_slabs(n, 0, G[n]) for n in LAYER_SPLIT]))
    parts1, parts0a = beside[:len(slabs1)], beside[len(slabs1):]
    grad_x = g[None]
    parts0b = _all_to_all("exchange_grads_mixer0", [grad_slabs(n, 0, G0[n]) for n in ("e_w_in", "e_w_out")])
    parts = {(n, 1): p for n, p in zip(layer_names[1], parts1)}
    parts.update({(n, 0): p for n, p in zip(LAYER_SPLIT, parts0a)})
    parts.update({(n, 0): p for n, p in zip(("e_w_in", "e_w_out"), parts0b)})
    G = {n: {0: G0[n]} for n in G0}
    for n in G1:
        G.setdefault(n, {})[1] = G1[n]
    rep_g = [jnp.stack([G[n][l] for l in sorted(G[n])]).reshape(Wt[n].shape) for n in rep_names]
    rep_parts = _all_gather("gather_small_grads", [_pack(rep_g + [jnp.sum(loss_row).reshape(1)], rep_rows, F32)])[0]

    outs = {}
    for n in SHARDED:
        per_layer = [_adamw(f"adamw_{n}{l}", parts[(n, l if n in LAYER_SPLIT else (0 if n.startswith("e_") else 1))], Wt[n], Mo[n], Vo[n], layer=l)
                     for l in range(Wt[n].shape[0])]
        for k, kind in enumerate(("grad", "delta", "new_m", "new_v")):
            outs[kind + "_" + n] = jnp.stack([r[k] for r in per_layer])
    res = _adamw("adamw_replicated", rep_parts, _pack([Wt[n] for n in rep_names], rep_rows, F32)[None],
                 _pack([Mo[n] for n in rep_names], rep_rows, F32)[None], _pack([Vo[n] for n in rep_names], rep_rows, F32)[None])
    for kind, slab in zip(("grad", "delta", "new_m", "new_v"), res):
        for n, a in zip(rep_names, _unpack(slab, rep_shapes)):
            outs[kind + "_" + n] = a
    loss = res[0].reshape(-1)[sum(math.prod(s) for s in rep_shapes)]
    return (loss, grad_x, *[outs[k + "_" + n] for k in ("grad", "delta", "new_m", "new_v") for n in WEIGHTS])
```

```python
import functools
import math

import numpy as np
import jax
import jax.numpy as jnp
from jax import lax
from jax.experimental import pallas as pl
from jax.experimental.pallas import tpu as pltpu

F32 = jnp.float32
BF16 = jnp.bfloat16
_MXU = jnp.bfloat16

D_MODEL = 1024
N_DEV = 8
LANES = 128
SUBLANES = 8
HALO = 16
CHUNK = 64
A_GROUPS, A_GROUP_DIM, A_WIDTH, A_BLOCK = 4, 128, 512, 128
FOX_HEADS, FOX_HEAD_DIM, FOX_WIDTH = 8, 64, 512
FOX_TILE = 512
POOL_WINDOWS = (2, 4, 8, 16)
POOL_WIDTH, CONV_WIDTH, CONV_K = 512, 512, 3
EVEN_IN, EVEN_IN_PAD, ODD_IN = 2568, 2688, 2048
XA_HEADS, XA_HEAD_DIM, XA_WIDTH = 4, 128, 512
D_FF = 2816
EPS = 1e-6
ADAM_LR, ADAM_B1, ADAM_B2, ADAM_EPS, ADAM_WD, ADAM_STEP = 0.001, 0.9, 0.999, 1e-08, 0.01, 10
VMEM_LIMIT = 48 * 1024 * 1024


def _cp(sem):
    return pltpu.CompilerParams(dimension_semantics=sem, vmem_limit_bytes=VMEM_LIMIT)


def _tile(n, cap, q):
    best = None
    for d in range(q, min(n, cap) + 1, q):
        if n % d == 0:
            best = d
    return best if best is not None else n


def _mx(a):
    return a.astype(_MXU)


def _dot(a, b):
    return jnp.dot(_mx(a), _mx(b), preferred_element_type=F32)


def _dot_nt(a, b):
    return lax.dot_general(_mx(a), _mx(b), (((1,), (1,)), ((), ())), preferred_element_type=F32)


def _dot_tn(a, b):
    return lax.dot_general(_mx(a), _mx(b), (((0,), (0,)), ((), ())), preferred_element_type=F32)


def _dot_f32(a, b):
    return jnp.dot(a, b, precision=lax.Precision.HIGHEST, preferred_element_type=F32)


def _lane(shape):
    return lax.broadcasted_iota(jnp.int32, shape, len(shape) - 1)


def _row(shape):
    return lax.broadcasted_iota(jnp.int32, shape, len(shape) - 2)


def _mm(name, pairs, mode, out_dtype=F32, residual=None, norm_bwd=None, tm_cap=512, tn_cap=1536, tk_cap=2048):
    a0, b0 = pairs[0]
    if mode == "nn":
        (M, K), N = a0.shape, b0.shape[1]
    elif mode == "nt":
        (M, K), N = a0.shape, b0.shape[0]
    else:
        (K, M), N = a0.shape, b0.shape[1]
    if mode == "tn":
        tm, tk = _tile(M, 1408, LANES), _tile(K, 512, 16)
    else:
        tm, tk = _tile(M, tm_cap, 16), _tile(K, tk_cap, LANES)
    tn = _tile(N, tn_cap, LANES)
    nk = K // tk
    npairs = len(pairs)
    dot = {"nn": _dot, "nt": _dot_nt, "tn": _dot_tn}[mode]

    if norm_bwd is not None:
        assert tn == N and residual is None and mode != "tn", "the norm backward needs whole rows"

    def body(*refs):
        ab = refs[:2 * npairs]
        res_ref = refs[2 * npairs] if residual is not None else None
        k = pl.program_id(2)
        first_row_tile = pl.program_id(1) == 0
        if norm_bwd is not None:
            x_ref, g_ref, r_ref = refs[2 * npairs:2 * npairs + 3]
            o_ref, dg_ref, acc = refs[-3], refs[-2], refs[-1]
        else:
            o_ref, acc = refs[-2], refs[-1]

        @pl.when(k == 0)
        def _():
            acc[...] = jnp.zeros_like(acc)

        for p in range(npairs):
            acc[...] += dot(ab[2 * p][...], ab[2 * p + 1][...])

        @pl.when(k == nk - 1)
        def _():
            out = acc[...]
            if res_ref is not None:
                out = out + res_ref[...]
            if norm_bwd is not None:
                xv = x_ref[...]
                r = _rms_rows(xv)
                xhat = xv * r
                dxhat = out * g_ref[...]

                @pl.when(first_row_tile)
                def _():
                    dg_ref[...] = jnp.zeros_like(dg_ref)

                dg_ref[...] += jnp.sum(out * xhat, axis=0, keepdims=True)
                out = r_ref[...] + r * (dxhat - xhat * jnp.mean(dxhat * xhat, axis=-1, keepdims=True))
            o_ref[...] = out.astype(o_ref.dtype)

    if mode == "nn":
        a_spec = pl.BlockSpec((tm, tk), lambda j, i, k: (i, k))
        b_spec = pl.BlockSpec((tk, tn), lambda j, i, k: (k, j))
    elif mode == "nt":
        a_spec = pl.BlockSpec((tm, tk), lambda j, i, k: (i, k))
        b_spec = pl.BlockSpec((tn, tk), lambda j, i, k: (j, k))
    else:
        a_spec = pl.BlockSpec((tk, tm), lambda j, i, k: (k, i))
        b_spec = pl.BlockSpec((tk, tn), lambda j, i, k: (k, j))
    o_spec = pl.BlockSpec((tm, tn), lambda j, i, k: (i, j))
    in_specs, args = [], []
    for a, b in pairs:
        in_specs += [a_spec, b_spec]
        args += [a, b]
    if residual is not None:
        in_specs.append(o_spec)
        args.append(residual)
    out_specs, out_shape, sem = o_spec, jax.ShapeDtypeStruct((M, N), out_dtype), ("parallel", "parallel", "arbitrary")
    if norm_bwd is not None:
        x, g, res = norm_bwd
        vec = pl.BlockSpec((1, tn), lambda j, i, k: (0, j))
        in_specs += [o_spec, vec, o_spec]
        args += [x, g.reshape(1, N), res]
        out_specs, out_shape = [o_spec, vec], [out_shape, jax.ShapeDtypeStruct((1, N), F32)]
        sem = ("arbitrary", "arbitrary", "arbitrary")
    return pl.pallas_call(
        body, name=name, grid=(N // tn, M // tm, nk),
        in_specs=in_specs, out_specs=out_specs, out_shape=out_shape,
        scratch_shapes=[pltpu.VMEM((tm, tn), F32)],
        compiler_params=_cp(sem),
    )(*args)


def _rms_rows(x):
    return lax.rsqrt(jnp.mean(x * x, axis=-1, keepdims=True) + EPS)


def _norm_fwd(name, x, g):
    T, Dm = x.shape
    tm = _tile(T, 512, 16)

    def body(x_ref, g_ref, o_ref):
        xv = x_ref[...]
        o_ref[...] = ((xv * _rms_rows(xv)) * g_ref[...]).astype(o_ref.dtype)

    return pl.pallas_call(
        body, name=name, grid=(T // tm,),
        in_specs=[pl.BlockSpec((tm, Dm), lambda i: (i, 0)), pl.BlockSpec((1, Dm), lambda i: (0, 0))],
        out_specs=pl.BlockSpec((tm, Dm), lambda i: (i, 0)),
        out_shape=jax.ShapeDtypeStruct((T, Dm), _MXU),
        compiler_params=_cp(("parallel",)),
    )(x, g.reshape(1, Dm))


def _gelu(x):
    return jax.nn.gelu(x)


def _gelu_grad(x):
    c0, c1 = math.sqrt(2.0 / math.pi), 0.044715
    t = jnp.tanh(c0 * (x + c1 * x * x * x))
    return 0.5 * (1.0 + t) + 0.5 * x * (1.0 - t * t) * c0 * (1.0 + 3.0 * c1 * x * x)


def _head64_rms(x):
    lo = _lane(x.shape) < FOX_HEAD_DIM
    xx = x * x
    sa = jnp.sum(jnp.where(lo, xx, 0.0), axis=-1, keepdims=True)
    sb = jnp.sum(jnp.where(lo, 0.0, xx), axis=-1, keepdims=True)
    inv = 1.0 / FOX_HEAD_DIM
    return jnp.where(lo, lax.rsqrt(sa * inv + EPS), lax.rsqrt(sb * inv + EPS))


def _head64_mean(x):
    lo = _lane(x.shape) < FOX_HEAD_DIM
    sa = jnp.sum(jnp.where(lo, x, 0.0), axis=-1, keepdims=True)
    sb = jnp.sum(jnp.where(lo, 0.0, x), axis=-1, keepdims=True)
    return jnp.where(lo, sa, sb) * (1.0 / FOX_HEAD_DIM)


def _masked_ws(ws_ref, g):
    w = ws_ref[g]
    keep = (_row(w.shape) // CHUNK) >= (_lane(w.shape) // CHUNK)
    return jnp.where(keep, w, 0.0)


def _even_pre(name, z, g_v, w_s, bs_cols, g_qn2, g_kn2, b_f_pad):
    T = z.shape[0]
    tm = _tile(T, 256, A_BLOCK)
    nb = tm // A_BLOCK

    def body(z_ref, gv_ref, ws_ref, bs_ref, gq_ref, gk_ref, bf_ref, ya_ref, q_ref, k_ref, v_ref, c_ref, ct_ref, carry):
        i = pl.program_id(0)

        @pl.when(i == 0)
        def _():
            carry[...] = jnp.zeros_like(carry)

        wm = [_mx(_masked_ws(ws_ref, g)) for g in range(A_GROUPS)]
        for g in range(A_GROUPS):
            sl = slice(g * A_GROUP_DIM, (g + 1) * A_GROUP_DIM)
            u = _gelu(z_ref[:, sl])
            v = _gelu(z_ref[:, A_WIDTH + g * A_GROUP_DIM:A_WIDTH + (g + 1) * A_GROUP_DIM])
            vn = _mx((v * _rms_rows(v)) * gv_ref[:, sl])
            for n in range(nb):
                rows = slice(n * A_BLOCK, (n + 1) * A_BLOCK)
                s = jnp.dot(wm[g], vn[rows], preferred_element_type=F32) + bs_ref[:, g:g + 1]
                ya_ref[rows, sl] = (u[rows] * s).astype(ya_ref.dtype)
        o = 2 * A_WIDTH
        for j in range(FOX_WIDTH // LANES):
            sl = slice(j * LANES, (j + 1) * LANES)
            q = z_ref[:, o + j * LANES:o + (j + 1) * LANES]
            q_ref[:, sl] = (((q * _head64_rms(q)) * gq_ref[...]) * (1.0 / math.sqrt(FOX_HEAD_DIM))).astype(q_ref.dtype)
            k = z_ref[:, o + FOX_WIDTH + j * LANES:o + FOX_WIDTH + (j + 1) * LANES]
            k_ref[:, sl] = ((k * _head64_rms(k)) * gk_ref[...]).astype(k_ref.dtype)
            v_ref[:, sl] = z_ref[:, o + 2 * FOX_WIDTH + j * LANES:o + 2 * FOX_WIDTH + (j + 1) * LANES].astype(v_ref.dtype)
        zf = z_ref[:, o + 3 * FOX_WIDTH:o + 3 * FOX_WIDTH + LANES]
        logf = jnp.where(_lane(zf.shape) < FOX_HEADS, jax.nn.log_sigmoid(zf + bf_ref[...]), 0.0)
        tri = (_row((tm, tm)) >= _lane((tm, tm))).astype(F32)
        c = _dot_f32(tri, logf) + carry[0:1, :]
        c_ref[...] = c
        ct_ref[...] = c.T[0:SUBLANES, :]
        carry[0:1, :] = c[tm - 1:tm, :]

    wide = lambda n: pl.BlockSpec((tm, n), lambda i: (i, 0))
    full = lambda s: pl.BlockSpec(s, lambda i: tuple(0 for _ in s))
    out512 = jax.ShapeDtypeStruct((T, 512), _MXU)
    return pl.pallas_call(
        body, name=name, grid=(T // tm,),
        in_specs=[wide(EVEN_IN_PAD), full((1, A_WIDTH)), full((A_GROUPS, A_BLOCK, A_BLOCK)), full((A_BLOCK, LANES)),
                  full((1, LANES)), full((1, LANES)), full((1, LANES))],
        out_specs=[wide(512), wide(512), wide(512), wide(512), wide(LANES), pl.BlockSpec((SUBLANES, tm), lambda i: (0, i))],
        out_shape=[out512, out512, out512, out512, jax.ShapeDtypeStruct((T, LANES), F32), jax.ShapeDtypeStruct((SUBLANES, T), F32)],
        scratch_shapes=[pltpu.VMEM((SUBLANES, LANES), F32)],
        compiler_params=_cp(("arbitrary",)),
    )(z, g_v, w_s, bs_cols, g_qn2, g_kn2, b_f_pad)


def _tri_steps(n, by_rows):
    if by_rows:
        pairs = [(q, k) for q in range(n) for k in range(q + 1)]
    else:
        pairs = [(q, k) for k in range(n) for q in range(k, n)]
    return (jnp.asarray(np.array([p[0] for p in pairs], np.int32)), jnp.asarray(np.array([p[1] for p in pairs], np.int32)))


def _fox_logits(q, k, c_col, c_row, diag, tq):
    s = _dot_nt(q, k) + (c_col - c_row)
    if diag:
        s = jnp.where(_row((tq, tq)) >= _lane((tq, tq)), s, -jnp.inf)
    return s


def _on_diag_or_not(qi, ki, step):
    @pl.when(qi == ki)
    def _():
        step(True)

    @pl.when(qi != ki)
    def _():
        step(False)


def _side_specs(side):
    if side is None:
        return [], [], [], []
    return [pl.BlockSpec(memory_space=pl.ANY)] * side.n, list(side.arrays), list(side.out_shape), list(side.scratch)


def _side_run(side, s_id, nsteps, side_in, side_out, sems):
    if side is None:
        return

    @pl.when(s_id == 0)
    def _():
        side.start(side_in, side_out, sems)

    @pl.when(s_id == nsteps // 2)
    def _():
        side.middle(side_in, side_out, sems)

    @pl.when(s_id == nsteps - 1)
    def _():
        side.finish(side_in, side_out, sems)


def _fox_fwd(name, qn, kn, vv, c, ct, side=None):
    T = qn.shape[0]
    tq = _tile(T, FOX_TILE, LANES)
    nblk = T // tq
    qs, ks = _tri_steps(nblk, True)
    nsteps = int(qs.shape[0])
    npair = FOX_WIDTH // LANES
    s_specs, s_args, s_shapes, s_scratch = _side_specs(side)

    def body(qs_ref, ks_ref, *refs):
        main_in, side_in, main_out, side_out, main_scr, sems = _split_side(refs, 5, 3, 3, side)
        q_ref, k_ref, v_ref, c_ref, ct_ref = main_in
        yb_ref, yf_ref, lse_ref = main_out
        acc, m_s, l_s = main_scr
        s_id = pl.program_id(0)
        qi, ki = qs_ref[s_id], ks_ref[s_id]
        _side_run(side, s_id, nsteps, side_in, side_out, sems)

        @pl.when(ki == 0)
        def _():
            acc[...] = jnp.zeros_like(acc)
            m_s[...] = jnp.full_like(m_s, -jnp.inf)
            l_s[...] = jnp.zeros_like(l_s)

        lo = _lane((tq, LANES)) < FOX_HEAD_DIM

        def step(diag):
            for j in range(npair):
                sl = slice(j * LANES, (j + 1) * LANES)
                q2, k2, v2 = q_ref[:, sl], k_ref[:, sl], v_ref[:, sl]
                outs = []
                for e in range(2):
                    h = 2 * j + e
                    qh = jnp.where(lo if e == 0 else ~lo, q2, jnp.zeros_like(q2))
                    s = _fox_logits(qh, k2, c_ref[:, h:h + 1], ct_ref[h:h + 1, :], diag, tq)
                    m_old = m_s[h]
                    m_new = jnp.maximum(m_old, jnp.max(s, axis=-1, keepdims=True))
                    alpha = jnp.exp(m_old - m_new)
                    p = jnp.exp(s - m_new)
                    l_s[h] = alpha * l_s[h] + jnp.sum(p, axis=-1, keepdims=True)
                    m_s[h] = m_new
                    outs.append((alpha, _dot(p, v2)))
                alpha2 = jnp.where(lo, outs[0][0], outs[1][0])
                acc[:, sl] = alpha2 * acc[:, sl] + jnp.where(lo, outs[0][1], outs[1][1])

        _on_diag_or_not(qi, ki, step)

        @pl.when(ki == qi)
        def _():
            lse = jnp.zeros((tq, LANES), F32)
            for j in range(npair):
                sl = slice(j * LANES, (j + 1) * LANES)
                inv = jnp.where(lo, 1.0 / l_s[2 * j], 1.0 / l_s[2 * j + 1])
                y = acc[:, sl] * inv
                yf_ref[:, sl] = y
                yb_ref[:, sl] = y.astype(yb_ref.dtype)
            for h in range(FOX_HEADS):
                lse = jnp.where(_lane((tq, LANES)) == h, m_s[h] + jnp.log(l_s[h]), lse)
            lse_ref[...] = lse

    qmap = lambda s, qs_r, ks_r: (qs_r[s], 0)
    kmap = lambda s, qs_r, ks_r: (ks_r[s], 0)
    grid_spec = pltpu.PrefetchScalarGridSpec(
        num_scalar_prefetch=2, grid=(nsteps,),
        in_specs=[pl.BlockSpec((tq, 512), qmap), pl.BlockSpec((tq, 512), kmap), pl.BlockSpec((tq, 512), kmap),
                  pl.BlockSpec((tq, LANES), qmap),
                  pl.BlockSpec((SUBLANES, tq), lambda s, qs_r, ks_r: (0, ks_r[s]))] + s_specs,
        out_specs=[pl.BlockSpec((tq, 512), qmap), pl.BlockSpec((tq, 512), qmap), pl.BlockSpec((tq, LANES), qmap)] + s_specs,
        scratch_shapes=[pltpu.VMEM((tq, 512), F32), pltpu.VMEM((FOX_HEADS, tq, 1), F32), pltpu.VMEM((FOX_HEADS, tq, 1), F32)] + s_scratch,
    )
    res = pl.pallas_call(
        body, name=name, grid_spec=grid_spec,
        out_shape=[jax.ShapeDtypeStruct((T, 512), _MXU), jax.ShapeDtypeStruct((T, 512), F32), jax.ShapeDtypeStruct((T, LANES), F32)] + s_shapes,
        compiler_params=_cp(("arbitrary",)),
    )(qs, ks, qn, kn, vv, c, ct, *s_args)
    return res[:3], res[3:]


def _shift_down(ext, k, tm):
    return pltpu.roll(ext, k, axis=0)[HALO:HALO + tm]


def _pool_p(z_ext, g, tm, pos):
    w = POOL_WINDOWS[g]
    s, span = z_ext, 1
    while span < w:
        s = s + pltpu.roll(s, span, axis=0)
        span *= 2
    cnt = jnp.minimum(pos + 1, w).astype(F32)
    return s[HALO:HALO + tm] / cnt - z_ext[HALO:HALO + tm]


def _odd_fwd(name, z, w_pool, s_pool, conv_w8):
    T = z.shape[0]
    tm = _tile(T, 256, HALO)
    r = tm // HALO

    def body(z_ref, zb_ref, wp_ref, sp_ref, cw_ref, y_ref):
        i = pl.program_id(0)
        first = i == 0
        pos = i * tm + _row((tm, LANES))
        for g in range(len(POOL_WINDOWS)):
            sl = slice(g * LANES, (g + 1) * LANES)
            halo = jnp.where(first, 0.0, zb_ref[:, sl])
            z_ext = jnp.concatenate([halo, z_ref[:, sl]], axis=0)
            p = _pool_p(z_ext, g, tm, pos)
            y_ref[:, sl] = (_dot(p, wp_ref[g]) * sp_ref[:, sl]).astype(y_ref.dtype)
        for j in range(CONV_WIDTH // LANES):
            sl = slice(j * LANES, (j + 1) * LANES)
            hd = slice(POOL_WIDTH + j * LANES, POOL_WIDTH + (j + 1) * LANES)
            gb = slice(POOL_WIDTH + CONV_WIDTH + j * LANES, POOL_WIDTH + CONV_WIDTH + (j + 1) * LANES)
            gc = slice(POOL_WIDTH + 2 * CONV_WIDTH + j * LANES, POOL_WIDTH + 2 * CONV_WIDTH + (j + 1) * LANES)
            xg_b = jnp.where(first, 0.0, zb_ref[:, gc] * zb_ref[:, hd])
            xg = jnp.concatenate([xg_b, z_ref[:, gc] * z_ref[:, hd]], axis=0)
            conv = (cw_ref[0:1, sl] * _shift_down(xg, 2, tm) + cw_ref[1:2, sl] * _shift_down(xg, 1, tm)
                    + cw_ref[2:3, sl] * xg[HALO:HALO + tm])
            y_ref[:, POOL_WIDTH + j * LANES:POOL_WIDTH + (j + 1) * LANES] = (z_ref[:, gb] * conv).astype(y_ref.dtype)

    full = lambda s: pl.BlockSpec(s, lambda i: tuple(0 for _ in s))
    return pl.pallas_call(
        body, name=name, grid=(T // tm,),
        in_specs=[pl.BlockSpec((tm, ODD_IN), lambda i: (i, 0)),
                  pl.BlockSpec((HALO, ODD_IN), lambda i: (jnp.maximum(i * r - 1, 0), 0)),
                  full((4, LANES, LANES)), full((1, POOL_WIDTH)), full((SUBLANES, CONV_WIDTH))],
        out_specs=pl.BlockSpec((tm, 1024), lambda i: (i, 0)),
        out_shape=jax.ShapeDtypeStruct((T, 1024), _MXU),
        compiler_params=_cp(("parallel",)),
    )(z, z, w_pool, s_pool, conv_w8)


def _mem_fwd(name, mem, g_mem, wkv, gk):
    M = mem.shape[0]

    def body(mem_ref, g_ref, w_ref, gk_ref, m_ref, kv_ref, kn_ref, v_ref):
        x = mem_ref[...]
        m = ((x * _rms_rows(x)) * g_ref[...]).astype(m_ref.dtype)
        m_ref[...] = m
        kv = _dot(m, w_ref[...])
        kv_ref[...] = kv
        for h in range(XA_HEADS):
            sl = slice(h * LANES, (h + 1) * LANES)
            k = kv[:, sl]
            kn_ref[:, sl] = ((k * _rms_rows(k)) * gk_ref[...]).astype(kn_ref.dtype)
        v_ref[...] = kv[:, XA_WIDTH:].astype(v_ref.dtype)

    return pl.pallas_call(
        body, name=name,
        out_shape=[jax.ShapeDtypeStruct((M, D_MODEL), _MXU), jax.ShapeDtypeStruct((M, 2 * XA_WIDTH), F32),
                   jax.ShapeDtypeStruct((M, XA_WIDTH), _MXU), jax.ShapeDtypeStruct((M, XA_WIDTH), _MXU)],
        compiler_params=pltpu.CompilerParams(vmem_limit_bytes=VMEM_LIMIT),
    )(mem, g_mem.reshape(1, D_MODEL), wkv, gk.reshape(1, XA_HEAD_DIM))


def _xa_probs(qx, gq, kn_h):
    r = _rms_rows(qx)
    qhat = qx * r
    qn = qhat * gq
    s = _dot_nt(qn, kn_h) * (1.0 / math.sqrt(XA_HEAD_DIM))
    s = s - jnp.max(s, axis=-1, keepdims=True)
    e = jnp.exp(s)
    return qhat, r, qn, e / jnp.sum(e, axis=-1, keepdims=True)


def _xa_fwd(name, qx, gq, kn, vx):
    T, M = qx.shape[0], kn.shape[0]
    tm = _tile(T, 512, 16)

    def body(q_ref, gq_ref, k_ref, v_ref, o_ref):
        for h in range(XA_HEADS):
            sl = slice(h * LANES, (h + 1) * LANES)
            _, _, _, p = _xa_probs(q_ref[:, sl], gq_ref[...], k_ref[:, sl])
            o_ref[:, sl] = _dot(p, v_ref[:, sl]).astype(o_ref.dtype)

    full = lambda s: pl.BlockSpec(s, lambda i: tuple(0 for _ in s))
    return pl.pallas_call(
        body, name=name, grid=(T // tm,),
        in_specs=[pl.BlockSpec((tm, XA_WIDTH), lambda i: (i, 0)), full((1, LANES)), full((M, XA_WIDTH)), full((M, XA_WIDTH))],
        out_specs=pl.BlockSpec((tm, XA_WIDTH), lambda i: (i, 0)),
        out_shape=jax.ShapeDtypeStruct((T, XA_WIDTH), _MXU),
        compiler_params=_cp(("parallel",)),
    )(qx, gq.reshape(1, XA_HEAD_DIM), kn, vx)


def _ffn_up(name, h, wg, wu):
    T, K = h.shape
    N = wg.shape[1]
    tm, tn = _tile(T, 512, 16), _tile(N, 1536, LANES)

    def body(h_ref, wg_ref, wu_ref, a_ref, b_ref, s_ref):
        hv = h_ref[...]
        a, b = _dot(hv, wg_ref[...]), _dot(hv, wu_ref[...])
        a_ref[...] = a
        b_ref[...] = b
        s_ref[...] = (jax.nn.silu(a) * b).astype(s_ref.dtype)

    w_spec = pl.BlockSpec((K, tn), lambda j, i: (0, j))
    o_spec = pl.BlockSpec((tm, tn), lambda j, i: (i, j))
    return pl.pallas_call(
        body, name=name, grid=(N // tn, T // tm),
        in_specs=[pl.BlockSpec((tm, K), lambda j, i: (i, 0)), w_spec, w_spec],
        out_specs=[o_spec, o_spec, o_spec],
        out_shape=[jax.ShapeDtypeStruct((T, N), F32), jax.ShapeDtypeStruct((T, N), F32), jax.ShapeDtypeStruct((T, N), _MXU)],
        compiler_params=_cp(("parallel", "parallel")),
    )(h, wg, wu)


def _loss_grad(name, y, target):
    T, Dm = y.shape
    tm = _tile(T, 512, 8)

    def body(y_ref, t_ref, dy_ref, l_ref):
        e = y_ref[...] - t_ref[...]
        dy_ref[...] = e * (1.0 / Dm)

        @pl.when(pl.program_id(0) == 0)
        def _():
            l_ref[...] = jnp.zeros_like(l_ref)

        l_ref[...] += jnp.sum(e * e, axis=0, keepdims=True) * (0.5 / Dm)

    blk = pl.BlockSpec((tm, Dm), lambda i: (i, 0))
    return pl.pallas_call(
        body, name=name, grid=(T // tm,),
        in_specs=[blk, blk], out_specs=[blk, pl.BlockSpec((1, Dm), lambda i: (0, 0))],
        out_shape=[jax.ShapeDtypeStruct((T, Dm), F32), jax.ShapeDtypeStruct((1, Dm), F32)],
        compiler_params=_cp(("arbitrary",)),
    )(y, target)


def _pad_lanes(v, n=LANES):
    v = v.reshape(1, -1)
    return jnp.pad(v, ((0, 0), (0, n - v.shape[1])))


def _layer_fwd(l, x, mem, P, side=None, on_side=None):
    sv = {"x0": x}
    W = _layer_weights(l, P, "in")
    h1 = _norm_fwd(f"norm_mix{l}", x, W["g_mix"])
    sv["h1"] = h1
    if l % 2 == 0:
        z = _mm(f"e_in{l}", [(h1, W["w_in"])], "nn")
        ya, qn, kn, vv, c, ct = _even_pre(f"even_pre{l}", z, W["g_v"], W["w_s"], W["bs_cols"], W["g_qn2"], W["g_kn2"], W["b_f_pad"])
        (yb, ybf, lse), side_out = _fox_fwd(f"fox_fwd{l}", qn, kn, vv, c, ct, side=side)
        if on_side is not None:
            on_side(side_out)
        W.update(_layer_weights(l, P, "rest"))
        sv.update(z=z, ya=ya, qn=qn, kn=kn, vv=vv, c=c, ct=ct, yb=yb, ybf=ybf, lse=lse)
        x1 = _mm(f"e_out{l}", [(ya, W["w_out"][:A_WIDTH]), (yb, W["w_out"][A_WIDTH:])], "nn", residual=x)
    else:
        W.update(_layer_weights(l, P, "rest"))
        z = _mm(f"o_in{l}", [(h1, W["w_in"])], "nn")
        ycat = _odd_fwd(f"odd_fwd{l}", z, W["w_pool"], W["s_pool"], W["conv_w8"])
        sv.update(z=z, ycat=ycat)
        x1 = _mm(f"o_out{l}", [(ycat, W["w_out"])], "nn", residual=x)
    sv["x1"] = x1
    h2 = _norm_fwd(f"norm_xa{l}", x1, W["g_xa"])
    qx = _mm(f"xa_q{l}", [(h2, W["wq"])], "nn")
    m, kv, kn_x, vx = _mem_fwd(f"mem_fwd{l}", mem, W["g_mem"], W["wkv"], W["gk"])
    o = _xa_fwd(f"xa_fwd{l}", qx, W["gq"], kn_x, vx)
    x2 = _mm(f"xa_o{l}", [(o, W["wo"])], "nn", residual=x1)
    sv.update(h2=h2, qx=qx, m=m, kv=kv, kn_x=kn_x, vx=vx, o=o, x2=x2)
    h3 = _norm_fwd(f"norm_ffn{l}", x2, W["g_ffn"])
    a, b, s = _ffn_up(f"ffn_up{l}", h3, W["w_gate"], W["w_up"])
    x3 = _mm(f"ffn_down{l}", [(s, W["w_down"])], "nn", residual=x2)
    sv.update(h3=h3, a=a, b=b, s=s)
    return x3, sv, W


def _layer_weights(l, P, part):
    i = l // 2
    if part == "in":
        W = {"g_mix": P["g_mix"][l]}
        if l % 2 == 0:
            w_in = _mx(P["e_w_in"][i])
            if w_in.shape[1] < EVEN_IN_PAD:
                w_in = jnp.pad(w_in, ((0, 0), (0, EVEN_IN_PAD - w_in.shape[1])))
            W.update(w_in=w_in, g_v=P["e_g_v"][i].reshape(1, A_WIDTH), w_s=P["e_w_s"][i],
                     bs_cols=jnp.pad(P["e_b_s"][i].T, ((0, 0), (0, LANES - A_GROUPS))),
                     g_qn2=jnp.tile(P["e_g_qn"][i], 2).reshape(1, LANES), g_kn2=jnp.tile(P["e_g_kn"][i], 2).reshape(1, LANES),
                     b_f_pad=_pad_lanes(P["e_b_f"][i]))
        return W
    W = {k: P[k][l] for k in ("g_xa", "g_mem", "g_ffn")}
    W.update(wq=_mx(P["xa_wq"][l]), wkv=_mx(P["xa_wkv"][l]), wo=_mx(P["xa_wo"][l]), gq=P["xa_gq"][l], gk=P["xa_gk"][l],
             w_gate=_mx(P["w_gate"][l]), w_up=_mx(P["w_up"][l]), w_down=_mx(P["w_down"][l]))
    if l % 2 == 0:
        W.update(w_out=_mx(P["e_w_out"][i]))
    else:
        W.update(w_in=_mx(P["o_w_in"][i]), w_pool=_mx(P["o_w_pool"][i]), s_pool=P["o_s_pool"][i].reshape(1, POOL_WIDTH),
                 conv_w8=jnp.pad(P["o_conv_w"][i], ((0, SUBLANES - CONV_K), (0, 0))), w_out=_mx(P["o_w_out"][i]))
    return W


def _ffn_bwd_act(name, g, wd, a, b):
    T, N = a.shape
    K = g.shape[1]
    tm, tn = _tile(T, 512, 16), _tile(N, 1536, LANES)

    def body(g_ref, wd_ref, a_ref, b_ref, da_ref, db_ref):
        ds = _dot_nt(g_ref[...], wd_ref[...])
        av = a_ref[...]
        sig = jax.nn.sigmoid(av)
        da_ref[...] = (ds * b_ref[...] * (sig * (1.0 + av * (1.0 - sig)))).astype(da_ref.dtype)
        db_ref[...] = (ds * (av * sig)).astype(db_ref.dtype)

    o_spec = pl.BlockSpec((tm, tn), lambda j, i: (i, j))
    return pl.pallas_call(
        body, name=name, grid=(N // tn, T // tm),
        in_specs=[pl.BlockSpec((tm, K), lambda j, i: (i, 0)), pl.BlockSpec((tn, K), lambda j, i: (j, 0)), o_spec, o_spec],
        out_specs=[o_spec, o_spec],
        out_shape=[jax.ShapeDtypeStruct((T, N), _MXU), jax.ShapeDtypeStruct((T, N), _MXU)],
        compiler_params=_cp(("parallel", "parallel")),
    )(g, wd, a, b)


def _xa_bwd(name, qx, do, gq, kn, vx):
    T, M = qx.shape[0], kn.shape[0]
    tm = _tile(T, 512, 16)

    def body(q_ref, do_ref, gq_ref, k_ref, v_ref, dq_ref, dk_ref, dv_ref, dg_ref):
        @pl.when(pl.program_id(0) == 0)
        def _():
            dk_ref[...] = jnp.zeros_like(dk_ref)
            dv_ref[...] = jnp.zeros_like(dv_ref)
            dg_ref[...] = jnp.zeros_like(dg_ref)

        gqv = gq_ref[...]
        for h in range(XA_HEADS):
            sl = slice(h * LANES, (h + 1) * LANES)
            qhat, r, qn, p = _xa_probs(q_ref[:, sl], gqv, k_ref[:, sl])
            do_h = do_ref[:, sl]
            dp = _dot_nt(do_h, v_ref[:, sl])
            dv_ref[:, sl] += _dot_tn(p, do_h)
            dsm = p * (dp - jnp.sum(p * dp, axis=-1, keepdims=True)) * (1.0 / math.sqrt(XA_HEAD_DIM))
            dqn = _dot(dsm, k_ref[:, sl])
            dk_ref[:, sl] += _dot_tn(dsm, qn)
            dg_ref[...] += jnp.sum(dqn * qhat, axis=0, keepdims=True)
            dxh = dqn * gqv
            dq_ref[:, sl] = (r * (dxh - qhat * jnp.mean(dxh * qhat, axis=-1, keepdims=True))).astype(dq_ref.dtype)

    full = lambda s: pl.BlockSpec(s, lambda i: tuple(0 for _ in s))
    blk = pl.BlockSpec((tm, XA_WIDTH), lambda i: (i, 0))
    return pl.pallas_call(
        body, name=name, grid=(T // tm,),
        in_specs=[blk, blk, full((1, LANES)), full((M, XA_WIDTH)), full((M, XA_WIDTH))],
        out_specs=[blk, full((M, XA_WIDTH)), full((M, XA_WIDTH)), full((1, LANES))],
        out_shape=[jax.ShapeDtypeStruct((T, XA_WIDTH), _MXU), jax.ShapeDtypeStruct((M, XA_WIDTH), F32),
                   jax.ShapeDtypeStruct((M, XA_WIDTH), F32), jax.ShapeDtypeStruct((1, LANES), F32)],
        compiler_params=_cp(("arbitrary",)),
    )(qx, do, gq.reshape(1, XA_HEAD_DIM), kn, vx)


def _mem_bwd(name, mem, g_mem, m, kv, gk, dkn, dvx, wkv):
    M = mem.shape[0]

    def body(mem_ref, g_ref, m_ref, kv_ref, gk_ref, dkn_ref, dvx_ref, w_ref, dw_ref, dgk_ref, dgm_ref):
        gkv = gk_ref[...]
        dgk = jnp.zeros((1, LANES), F32)
        parts = []
        for h in range(XA_HEADS):
            sl = slice(h * LANES, (h + 1) * LANES)
            k = kv_ref[:, sl]
            r = _rms_rows(k)
            khat = k * r
            dkn_h = dkn_ref[:, sl]
            dgk = dgk + jnp.sum(dkn_h * khat, axis=0, keepdims=True)
            dxh = dkn_h * gkv
            parts.append(r * (dxh - khat * jnp.mean(dxh * khat, axis=-1, keepdims=True)))
        dgk_ref[...] = dgk
        dkv = jnp.concatenate(parts + [dvx_ref[...]], axis=1)
        dw_ref[...] = _dot_tn(m_ref[...], dkv)
        dm = _dot_nt(dkv, w_ref[...])
        x = mem_ref[...]
        dgm_ref[...] = jnp.sum(dm * (x * _rms_rows(x)), axis=0, keepdims=True)

    return pl.pallas_call(
        body, name=name,
        out_shape=[jax.ShapeDtypeStruct((D_MODEL, 2 * XA_WIDTH), F32), jax.ShapeDtypeStruct((1, LANES), F32),
                   jax.ShapeDtypeStruct((1, D_MODEL), F32)],
        compiler_params=pltpu.CompilerParams(vmem_limit_bytes=VMEM_LIMIT),
    )(mem, g_mem.reshape(1, D_MODEL), m, kv, gk.reshape(1, XA_HEAD_DIM), dkn, dvx, wkv)


def _fox_bwd_common(q2, k2, v2, dy2, yf2, c_ref, ct_ref, lse_ref, e, h, lo, diag, tq):
    sel = lo if e == 0 else ~lo
    qh = jnp.where(sel, q2, jnp.zeros_like(q2))
    kh = jnp.where(sel, k2, jnp.zeros_like(k2))
    dyh = jnp.where(sel, dy2, 0.0)
    s = _fox_logits(qh, k2, c_ref[:, h:h + 1], ct_ref[h:h + 1, :], diag, tq)
    p = jnp.exp(s - lse_ref[:, h:h + 1])
    dp = _dot_nt(dyh, v2)
    delta = jnp.sum(dyh * yf2, axis=-1, keepdims=True)
    return p, p * (dp - delta), qh, kh, dyh


def _fox_bwd(name, qn, kn, vv, c, ct, lse, dycat, ybf, side=None):
    T = qn.shape[0]
    tq = _tile(T, FOX_TILE, LANES)
    nblk = T // tq
    qs, ks = _tri_steps(nblk, False)
    nsteps = int(qs.shape[0])
    scale = 1.0 / math.sqrt(FOX_HEAD_DIM)
    s_specs, s_args, s_shapes, s_scratch = _side_specs(side)

    def body(qs_ref, ks_ref, *refs):
        main_in, side_in, main_out, side_out, main_scr, sems = _split_side(refs, 8, 5, 3, side)
        q_ref, k_ref, v_ref, c_ref, ct_ref, lse_ref, dy_ref, yf_ref = main_in
        dq_hbm, drow_hbm, dk_ref, dv_ref, dct_ref = main_out
        dq_acc, drow_acc, out_sem = main_scr
        s_id = pl.program_id(0)
        qi, ki = qs_ref[s_id], ks_ref[s_id]
        _side_run(side, s_id, nsteps, side_in, side_out, sems)
        rows = pl.ds(pl.multiple_of(qi * tq, tq), tq)

        @pl.when(qi == ki)
        def _():
            dk_ref[...] = jnp.zeros_like(dk_ref)
            dv_ref[...] = jnp.zeros_like(dv_ref)
            dct_ref[...] = jnp.zeros_like(dct_ref)

        @pl.when(ki == 0)
        def _():
            dq_acc[rows, :] = jnp.zeros((tq, FOX_WIDTH), F32)
            drow_acc[rows, :] = jnp.zeros((tq, LANES), F32)

        lo = _lane((tq, LANES)) < FOX_HEAD_DIM

        def step(diag):
            drow = jnp.zeros((tq, LANES), F32)
            for j in range(FOX_WIDTH // LANES):
                sl = slice(j * LANES, (j + 1) * LANES)
                q2, k2, v2, dy2, yf2 = q_ref[:, sl], k_ref[:, sl], v_ref[:, sl], dy_ref[:, sl], yf_ref[:, sl]
                dq_t, dk_t, dv_t = None, None, None
                for e in range(2):
                    h = 2 * j + e
                    p, ds, qh, kh, dyh = _fox_bwd_common(q2, k2, v2, dy2, yf2, c_ref, ct_ref, lse_ref, e, h, lo, diag, tq)
                    drow = jnp.where(_lane((tq, LANES)) == h, jnp.sum(ds, axis=-1, keepdims=True), drow)
                    dct_ref[h:h + 1, :] -= jnp.sum(ds, axis=0, keepdims=True)
                    a, b, d = _dot(ds, kh), _dot_tn(ds, qh), _dot_tn(p, dyh)
                    dq_t = a if dq_t is None else dq_t + a
                    dk_t = b if dk_t is None else dk_t + b
                    dv_t = d if dv_t is None else dv_t + d
                dq_acc[rows, sl] += dq_t * scale
                dk_ref[:, sl] += dk_t
                dv_ref[:, sl] += dv_t
            drow_acc[rows, :] += drow

        _on_diag_or_not(qi, ki, step)

        @pl.when(s_id == nsteps - 1)
        def _():
            for src, dst in ((dq_acc, dq_hbm), (drow_acc, drow_hbm)):
                cp = pltpu.make_async_copy(src, dst, out_sem)
                cp.start()
                cp.wait()

    qmap = lambda s, a, b: (a[s], 0)
    kmap = lambda s, a, b: (b[s], 0)
    ctmap = lambda s, a, b: (0, b[s])
    any_spec = pl.BlockSpec(memory_space=pl.ANY)
    grid_spec = pltpu.PrefetchScalarGridSpec(
        num_scalar_prefetch=2, grid=(nsteps,),
        in_specs=[pl.BlockSpec((tq, 512), qmap), pl.BlockSpec((tq, 512), kmap), pl.BlockSpec((tq, 512), kmap),
                  pl.BlockSpec((tq, LANES), qmap), pl.BlockSpec((SUBLANES, tq), ctmap),
                  pl.BlockSpec((tq, LANES), qmap), pl.BlockSpec((tq, 512), lambda s, a, b: (a[s], 1)), pl.BlockSpec((tq, 512), qmap)] + s_specs,
        out_specs=[any_spec, any_spec, pl.BlockSpec((tq, 512), kmap), pl.BlockSpec((tq, 512), kmap), pl.BlockSpec((SUBLANES, tq), ctmap)] + s_specs,
        scratch_shapes=[pltpu.VMEM((T, FOX_WIDTH), F32), pltpu.VMEM((T, LANES), F32), pltpu.SemaphoreType.DMA] + s_scratch,
    )
    res = pl.pallas_call(
        body, name=name, grid_spec=grid_spec,
        out_shape=[jax.ShapeDtypeStruct((T, 512), F32), jax.ShapeDtypeStruct((T, LANES), F32), jax.ShapeDtypeStruct((T, 512), F32),
                   jax.ShapeDtypeStruct((T, 512), F32), jax.ShapeDtypeStruct((SUBLANES, T), F32)] + s_shapes,
        compiler_params=_cp(("arbitrary",)),
    )(qs, ks, qn, kn, vv, c, ct, lse, dycat, ybf, *s_args)
    return res[:5], res[5:]


def _fold64(row):
    return row + pltpu.roll(row, FOX_HEAD_DIM, axis=1)


def _even_bwd(name, z, dycat, dqn, dkn, dvv, dct, drow, g_v, w_s, bs_cols, g_qn2, g_kn2, b_f_pad):
    T = z.shape[0]
    tm = _tile(T, 256, A_BLOCK)
    nb, nt = tm // A_BLOCK, T // tm

    def body(z_ref, dya_ref, dq_ref, dk_ref, dv_ref, dct_ref, drow_ref, gv_ref, ws_ref, bs_ref, gq_ref, gk_ref, bf_ref,
             dz_ref, dws_ref, dbs_ref, dgv_ref, dgq_ref, dgk_ref, dbf_ref, carry):
        @pl.when(pl.program_id(0) == 0)
        def _():
            for r in (dws_ref, dbs_ref, dgv_ref, dgq_ref, dgk_ref, dbf_ref, carry):
                r[...] = jnp.zeros_like(r)

        keep = (_row((A_BLOCK, A_BLOCK)) // CHUNK) >= (_lane((A_BLOCK, A_BLOCK)) // CHUNK)
        for g in range(A_GROUPS):
            sl = slice(g * A_GROUP_DIM, (g + 1) * A_GROUP_DIM)
            wm = _mx(_masked_ws(ws_ref, g))
            zu = z_ref[:, sl]
            zv = z_ref[:, A_WIDTH + g * A_GROUP_DIM:A_WIDTH + (g + 1) * A_GROUP_DIM]
            u, v = _gelu(zu), _gelu(zv)
            r = _rms_rows(v)
            vhat = v * r
            gvv = gv_ref[:, sl]
            vn = _mx(vhat * gvv)
            dya = dya_ref[:, sl]
            du_parts, dvn_parts = [], []
            for n in range(nb):
                rows = slice(n * A_BLOCK, (n + 1) * A_BLOCK)
                s = jnp.dot(wm, vn[rows], preferred_element_type=F32) + bs_ref[:, g:g + 1]
                du_parts.append(dya[rows] * s)
                d_s = dya[rows] * u[rows]
                dbs_ref[...] += jnp.where(_lane((A_BLOCK, LANES)) == g, jnp.sum(d_s, axis=-1, keepdims=True), 0.0)
                dws_ref[g] += jnp.where(keep, _dot_nt(d_s, vn[rows]), 0.0)
                dvn_parts.append(_dot_tn(wm, d_s))
            dvn = jnp.concatenate(dvn_parts, axis=0)
            dgv_ref[:, sl] += jnp.sum(dvn * vhat, axis=0, keepdims=True)
            dxh = dvn * gvv
            dv = r * (dxh - vhat * jnp.mean(dxh * vhat, axis=-1, keepdims=True))
            dz_ref[:, sl] = (jnp.concatenate(du_parts, axis=0) * _gelu_grad(zu)).astype(dz_ref.dtype)
            dz_ref[:, A_WIDTH + g * A_GROUP_DIM:A_WIDTH + (g + 1) * A_GROUP_DIM] = (dv * _gelu_grad(zv)).astype(dz_ref.dtype)
        o = 2 * A_WIDTH
        for j in range(FOX_WIDTH // LANES):
            sl = slice(j * LANES, (j + 1) * LANES)
            for (off, d_ref, g_ref, dg_ref) in ((o, dq_ref, gq_ref, dgq_ref), (o + FOX_WIDTH, dk_ref, gk_ref, dgk_ref)):
                zq = z_ref[:, off + j * LANES:off + (j + 1) * LANES]
                r = _head64_rms(zq)
                qhat = zq * r
                dn = d_ref[:, sl]
                dg_ref[...] += _fold64(jnp.sum(dn * qhat, axis=0, keepdims=True))
                dxh = dn * g_ref[...]
                dz_ref[:, off + j * LANES:off + (j + 1) * LANES] = (r * (dxh - qhat * _head64_mean(dxh * qhat))).astype(dz_ref.dtype)
            dz_ref[:, o + 2 * FOX_WIDTH + j * LANES:o + 2 * FOX_WIDTH + (j + 1) * LANES] = dv_ref[:, sl].astype(dz_ref.dtype)
        dct_v = dct_ref[...] + drow_ref[...].T[0:SUBLANES, :]
        upper = (_row((tm, tm)) >= _lane((tm, tm))).astype(F32)
        dlf_t = _dot_f32(dct_v, upper) + carry[:, 0:1]
        carry[...] += jnp.sum(dct_v, axis=1, keepdims=True)
        dlf = jnp.concatenate([dlf_t, jnp.zeros((LANES - SUBLANES, tm), F32)], axis=0).T
        zf = z_ref[:, o + 3 * FOX_WIDTH:o + 3 * FOX_WIDTH + LANES]
        dzf = dlf * jax.nn.sigmoid(-(zf + bf_ref[...]))
        dbf_ref[...] += jnp.sum(dzf, axis=0, keepdims=True)
        dz_ref[:, o + 3 * FOX_WIDTH:o + 3 * FOX_WIDTH + LANES] = dzf.astype(dz_ref.dtype)

    rev = lambda i: nt - 1 - i
    wide = lambda n, col=0: pl.BlockSpec((tm, n), lambda i: (rev(i), col))
    full = lambda s: pl.BlockSpec(s, lambda i: tuple(0 for _ in s))
    vec = full((1, LANES))
    return pl.pallas_call(
        body, name=name, grid=(nt,),
        in_specs=[wide(EVEN_IN_PAD), wide(512), wide(512), wide(512), wide(512), pl.BlockSpec((SUBLANES, tm), lambda i: (0, rev(i))),
                  wide(LANES), full((1, A_WIDTH)), full((A_GROUPS, A_BLOCK, A_BLOCK)), full((A_BLOCK, LANES)), vec, vec, vec],
        out_specs=[wide(EVEN_IN_PAD), full((A_GROUPS, A_BLOCK, A_BLOCK)), full((A_BLOCK, LANES)), full((1, A_WIDTH)), vec, vec, vec],
        out_shape=[jax.ShapeDtypeStruct((T, EVEN_IN_PAD), _MXU), jax.ShapeDtypeStruct((A_GROUPS, A_BLOCK, A_BLOCK), F32),
                   jax.ShapeDtypeStruct((A_BLOCK, LANES), F32), jax.ShapeDtypeStruct((1, A_WIDTH), F32),
                   jax.ShapeDtypeStruct((1, LANES), F32), jax.ShapeDtypeStruct((1, LANES), F32), jax.ShapeDtypeStruct((1, LANES), F32)],
        scratch_shapes=[pltpu.VMEM((SUBLANES, LANES), F32)],
        compiler_params=_cp(("arbitrary",)),
    )(z, dycat, dqn, dkn, dvv, dct, drow, g_v, w_s, bs_cols, g_qn2, g_kn2, b_f_pad)


def _shift_up(ext, k, tm):
    return pltpu.roll(ext, ext.shape[0] - k, axis=0)[0:tm]


def _odd_bwd(name, z, dycat, w_pool, s_pool, conv_w8):
    T = z.shape[0]
    tm = _tile(T, 256, HALO)
    r, nt = tm // HALO, T // tm
    n_ext = tm + HALO

    def body(z_ref, zb_ref, zn_ref, dy_ref, dyn_ref, wp_ref, sp_ref, cw_ref, dz_ref, dwp_ref, dsp_ref, dcw_ref):
        i = pl.program_id(0)

        @pl.when(i == 0)
        def _():
            for rr in (dwp_ref, dsp_ref, dcw_ref):
                rr[...] = jnp.zeros_like(rr)

        first, last = i == 0, i == nt - 1
        pos = i * tm + _row((tm, LANES))
        pos_ext = i * tm + _row((n_ext, LANES))
        for g, w in enumerate(POOL_WINDOWS):
            sl = slice(g * LANES, (g + 1) * LANES)
            z_ext = jnp.concatenate([jnp.where(first, 0.0, zb_ref[:, sl]), z_ref[:, sl]], axis=0)
            p = _pool_p(z_ext, g, tm, pos)
            spv = sp_ref[:, sl]
            dyc = dy_ref[:, sl]
            dsp_ref[:, sl] += jnp.sum(dyc * _dot(p, wp_ref[g]), axis=0, keepdims=True)
            dpw_ext = jnp.concatenate([dyc, jnp.where(last, 0.0, dyn_ref[:, sl])], axis=0) * spv
            dwp_ref[g] += _dot_tn(p, dpw_ext[0:tm])
            dp_ext = _dot_nt(dpw_ext, wp_ref[g])
            f = dp_ext / jnp.minimum(pos_ext + 1, w).astype(F32)
            span = 1
            while span < w:
                f = f + pltpu.roll(f, n_ext - span, axis=0)
                span *= 2
            dz_ref[:, sl] = (f[0:tm] - dp_ext[0:tm]).astype(dz_ref.dtype)
        for j in range(CONV_WIDTH // LANES):
            sl = slice(j * LANES, (j + 1) * LANES)
            hd = slice(POOL_WIDTH + j * LANES, POOL_WIDTH + (j + 1) * LANES)
            gb = slice(POOL_WIDTH + CONV_WIDTH + j * LANES, POOL_WIDTH + CONV_WIDTH + (j + 1) * LANES)
            gc = slice(POOL_WIDTH + 2 * CONV_WIDTH + j * LANES, POOL_WIDTH + 2 * CONV_WIDTH + (j + 1) * LANES)
            ysl = slice(POOL_WIDTH + j * LANES, POOL_WIDTH + (j + 1) * LANES)
            hdv, gbv, gcv = z_ref[:, hd], z_ref[:, gb], z_ref[:, gc]
            xg = gcv * hdv
            xg_ext = jnp.concatenate([jnp.where(first, 0.0, zb_ref[:, gc] * zb_ref[:, hd]), xg], axis=0)
            xg1, xg2 = _shift_down(xg_ext, 1, tm), _shift_down(xg_ext, 2, tm)
            w0, w1, w2 = cw_ref[0:1, sl], cw_ref[1:2, sl], cw_ref[2:3, sl]
            conv = w0 * xg2 + w1 * xg1 + w2 * xg
            dyd = dy_ref[:, ysl]
            dconv = dyd * gbv
            dconv_ext = jnp.concatenate([dconv, jnp.where(last, 0.0, dyn_ref[:, ysl] * zn_ref[:, gb])], axis=0)
            dcw_ref[0:1, sl] += jnp.sum(dconv * xg2, axis=0, keepdims=True)
            dcw_ref[1:2, sl] += jnp.sum(dconv * xg1, axis=0, keepdims=True)
            dcw_ref[2:3, sl] += jnp.sum(dconv * xg, axis=0, keepdims=True)
            dxg = w2 * dconv + w1 * _shift_up(dconv_ext, 1, tm) + w0 * _shift_up(dconv_ext, 2, tm)
            dz_ref[:, hd] = (dxg * gcv).astype(dz_ref.dtype)
            dz_ref[:, gb] = (dyd * conv).astype(dz_ref.dtype)
            dz_ref[:, gc] = (dxg * hdv).astype(dz_ref.dtype)

    full = lambda s: pl.BlockSpec(s, lambda i: tuple(0 for _ in s))
    back = lambda n: pl.BlockSpec((HALO, n), lambda i: (jnp.maximum(i * r - 1, 0), 0))
    nxt = lambda n: pl.BlockSpec((HALO, n), lambda i: (jnp.minimum((i + 1) * r, T // HALO - 1), 0))
    return pl.pallas_call(
        body, name=name, grid=(nt,),
        in_specs=[pl.BlockSpec((tm, ODD_IN), lambda i: (i, 0)), back(ODD_IN), nxt(ODD_IN),
                  pl.BlockSpec((tm, 1024), lambda i: (i, 0)), nxt(1024),
                  full((4, LANES, LANES)), full((1, POOL_WIDTH)), full((SUBLANES, CONV_WIDTH))],
        out_specs=[pl.BlockSpec((tm, ODD_IN), lambda i: (i, 0)), full((4, LANES, LANES)), full((1, POOL_WIDTH)), full((SUBLANES, CONV_WIDTH))],
        out_shape=[jax.ShapeDtypeStruct((T, ODD_IN), _MXU), jax.ShapeDtypeStruct((4, LANES, LANES), F32),
                   jax.ShapeDtypeStruct((1, POOL_WIDTH), F32), jax.ShapeDtypeStruct((SUBLANES, CONV_WIDTH), F32)],
        compiler_params=_cp(("arbitrary",)),
    )(z, z, z, dycat, dycat, w_pool, s_pool, conv_w8)


def _layer_bwd(l, g3, mem, W, sv, side_fn=None):
    i = l // 2
    G = {}
    side_out = ()
    da, db = _ffn_bwd_act(f"ffn_bwd_act{l}", g3, W["w_down"], sv["a"], sv["b"])
    G["w_down"] = _mm(f"d_w_down{l}", [(sv["s"], g3)], "tn")
    g2, dg = _mm(f"d_h3{l}", [(da, W["w_gate"]), (db, W["w_up"])], "nt", norm_bwd=(sv["x2"], W["g_ffn"], g3))
    G["w_gate"] = _mm(f"d_w_gate{l}", [(sv["h3"], da)], "tn")
    G["w_up"] = _mm(f"d_w_up{l}", [(sv["h3"], db)], "tn")
    G["g_ffn"] = dg[0]

    do = _mm(f"d_o{l}", [(g2, W["wo"])], "nt", out_dtype=_MXU)
    G["xa_wo"] = _mm(f"d_wo{l}", [(sv["o"], g2)], "tn")
    dqx, dkn, dvx, dgq = _xa_bwd(f"xa_bwd{l}", sv["qx"], do, W["gq"], sv["kn_x"], sv["vx"])
    G["xa_gq"] = dgq[0]
    g1, dg = _mm(f"d_h2{l}", [(dqx, W["wq"])], "nt", norm_bwd=(sv["x1"], W["g_xa"], g2))
    G["xa_wq"] = _mm(f"d_wq{l}", [(sv["h2"], dqx)], "tn")
    G["g_xa"] = dg[0]
    dwkv, dgk, dgm = _mem_bwd(f"mem_bwd{l}", mem, W["g_mem"], sv["m"], sv["kv"], W["gk"], dkn, dvx, W["wkv"])
    G["xa_wkv"], G["xa_gk"], G["g_mem"] = dwkv, dgk[0], dgm[0]

    dycat = _mm(f"d_ycat{l}", [(g1, W["w_out"])], "nt")
    if l % 2 == 0:
        G["e_w_out"] = jnp.concatenate([_mm(f"d_e_wout_a{l}", [(sv["ya"], g1)], "tn"), _mm(f"d_e_wout_b{l}", [(sv["yb"], g1)], "tn")], axis=0)
        fa = (sv["qn"], sv["kn"], sv["vv"], sv["c"], sv["ct"], sv["lse"], dycat, sv["ybf"])
        (dqn, drow, dkn_f, dvv, dct), side_out = _fox_bwd(f"fox_bwd{l}", *fa, side=side_fn(G) if side_fn is not None else None)
        dz, dws, dbs, dgv, dgq2, dgk2, dbf = _even_bwd(f"even_bwd{l}", sv["z"], dycat, dqn, dkn_f, dvv, dct, drow, W["g_v"], W["w_s"],
                                                      W["bs_cols"], W["g_qn2"], W["g_kn2"], W["b_f_pad"])
        G.update(e_w_s=dws, e_b_s=dbs[:, :A_GROUPS].T, e_g_v=dgv[0], e_g_qn=dgq2[0, :FOX_HEAD_DIM], e_g_kn=dgk2[0, :FOX_HEAD_DIM],
                 e_b_f=dbf[0, :FOX_HEADS])
        G["e_w_in"] = _mm(f"d_e_win{l}", [(sv["h1"], dz)], "tn")
    else:
        G["o_w_out"] = _mm(f"d_o_wout{l}", [(sv["ycat"], g1)], "tn")
        dz, dwp, dsp, dcw = _odd_bwd(f"odd_bwd{l}", sv["z"], dycat, W["w_pool"], W["s_pool"], W["conv_w8"])
        G.update(o_w_pool=dwp, o_s_pool=dsp[0], o_conv_w=dcw[:CONV_K])
        G["o_w_in"] = _mm(f"d_o_win{l}", [(sv["h1"], dz)], "tn")
    g0, dg = _mm(f"d_h1{l}", [(dz, W["w_in"])], "nt", norm_bwd=(sv["x0"], W["g_mix"], g1))
    G["g_mix"] = dg[0]
    return g0, G, side_out


PACK_W = 1024
MESH_T = pl.DeviceIdType.MESH
_ANY = pl.BlockSpec(memory_space=pl.ANY)


def _my_place():
    x, y, c = lax.axis_index("x"), lax.axis_index("y"), lax.axis_index("c")
    return x, y, c


def _flip(v, bit):
    return 1 - v if bit else v


N_PEERS = N_DEV - 1


def _all_gather(name, blks):
    side = _TwoLevelGather(blks)

    def body(*refs):
        _, side_in, _, side_out, _, sems = _split_side(refs, 0, 0, 0, side)
        side.start(side_in, side_out, sems)
        side.middle(side_in, side_out, sems)
        side.finish(side_in, side_out, sems)

    return pl.pallas_call(
        body, name=name, out_shape=side.out_shape,
        in_specs=[_ANY] * side.n, out_specs=[_ANY] * side.n, scratch_shapes=side.scratch,
    )(*blks)


class _Direct:
    def __init__(self, kind, arrays):
        self.kind, self.arrays, self.n = kind, list(arrays), len(arrays)
        if kind == "scatter":
            self.out_shape = [jax.ShapeDtypeStruct(a.shape, a.dtype) for a in arrays]
        else:
            self.out_shape = [jax.ShapeDtypeStruct((N_DEV,) + a.shape, a.dtype) for a in arrays]
        self.scratch = [pltpu.SemaphoreType.DMA((N_PEERS * self.n,)), pltpu.SemaphoreType.DMA((N_PEERS * self.n,)),
                        pltpu.SemaphoreType.DMA((self.n,))]

    def _copies(self, in_refs, out_refs, sems):
        send_sems, recv_sems, local_sems = sems
        x, y, c = _my_place()
        me = 4 * x + 2 * y + c
        mine, sends, recvs = [], [], []
        for t in range(self.n):
            src_of = (lambda idx, t=t: in_refs[t].at[idx]) if self.kind == "scatter" else (lambda idx, t=t: in_refs[t])
            mine.append(pltpu.make_async_copy(src_of(me), out_refs[t].at[me], local_sems.at[t]))
            for m in range(1, N_DEV):
                px, py, pc = _flip(x, m & 4), _flip(y, m & 2), _flip(c, m & 1)
                pidx = 4 * px + 2 * py + pc
                sem = dict(send_sem=send_sems.at[N_PEERS * t + m - 1], recv_sem=recv_sems.at[N_PEERS * t + m - 1], device_id_type=MESH_T)
                sends.append(pltpu.make_async_remote_copy(src_ref=src_of(pidx), dst_ref=out_refs[t].at[me], device_id=(px, py, pc), **sem))
                recvs.append(pltpu.make_async_remote_copy(src_ref=src_of(pidx), dst_ref=out_refs[t].at[pidx], device_id=(x, y, c), **sem))
        return mine, sends, recvs

    def start(self, in_refs, out_refs, sems):
        mine, sends, _ = self._copies(in_refs, out_refs, sems)
        for cp in mine + sends:
            cp.start()

    def middle(self, in_refs, out_refs, sems):
        pass

    def finish(self, in_refs, out_refs, sems):
        mine, sends, recvs = self._copies(in_refs, out_refs, sems)
        for cp in recvs:
            cp.wait_recv()
        for cp in sends:
            cp.wait_send()
        for cp in mine:
            cp.wait()


class _TwoLevelGather:
    def __init__(self, arrays):
        self.arrays, self.n = list(arrays), len(arrays)
        self.out_shape = [jax.ShapeDtypeStruct((N_DEV,) + a.shape, a.dtype) for a in arrays]
        self.scratch = [pltpu.SemaphoreType.DMA((N_PEERS * self.n,)), pltpu.SemaphoreType.DMA((N_PEERS * self.n,)),
                        pltpu.SemaphoreType.DMA((self.n,))]

    def _place(self):
        x, y, c = _my_place()
        return (x, y, c), (x, y, 1 - c), [(1 - x, y), (x, 1 - y), (1 - x, 1 - y)]

    def _copy(self, in_refs, out_refs, sems, t, k, block, to, own=False):
        slab = out_refs[t].at[4 * block[0] + 2 * block[1] + block[2]]
        return pltpu.make_async_remote_copy(
            src_ref=in_refs[t] if own else slab, dst_ref=slab, send_sem=sems[0].at[N_PEERS * t + k],
            recv_sem=sems[1].at[N_PEERS * t + k], device_id=to, device_id_type=MESH_T)

    def _mine(self, in_refs, out_refs, sems):
        x, y, c = _my_place()
        return [pltpu.make_async_copy(in_refs[t], out_refs[t].at[4 * x + 2 * y + c], sems[2].at[t]) for t in range(self.n)]

    def _first(self, in_refs, out_refs, sems):
        me, sibling, chips = self._place()
        cps = []
        for t in range(self.n):
            cps.append(self._copy(in_refs, out_refs, sems, t, 0, me, sibling, own=True))
            cps += [self._copy(in_refs, out_refs, sems, t, 1 + j, me, (*chip, me[2]), own=True) for j, chip in enumerate(chips)]
        return cps

    def _passed(self, in_refs, out_refs, sems):
        me, sibling, chips = self._place()
        return [self._copy(in_refs, out_refs, sems, t, 4 + j, (*chip, me[2]), sibling) for j, chip in enumerate(chips) for t in range(self.n)]

    def start(self, in_refs, out_refs, sems):
        for cp in self._mine(in_refs, out_refs, sems) + self._first(in_refs, out_refs, sems):
            cp.start()

    def middle(self, in_refs, out_refs, sems):
        me, _, chips = self._place()
        for j, chip in enumerate(chips):
            for t in range(self.n):
                self._copy(in_refs, out_refs, sems, t, 1 + j, (*chip, me[2]), me).wait_recv()
        for cp in self._passed(in_refs, out_refs, sems):
            cp.start()

    def finish(self, in_refs, out_refs, sems):
        me, sibling, chips = self._place()
        for t in range(self.n):
            self._copy(in_refs, out_refs, sems, t, 0, sibling, me).wait_recv()
            for j, chip in enumerate(chips):
                self._copy(in_refs, out_refs, sems, t, 4 + j, (*chip, 1 - me[2]), me).wait_recv()
        for cp in self._first(in_refs, out_refs, sems) + self._passed(in_refs, out_refs, sems):
            cp.wait_send()
        for cp in self._mine(in_refs, out_refs, sems):
            cp.wait()


def _split_side(refs, n_main_in, n_main_out, n_main_scratch, side):
    ns = side.n if side is not None else 0
    i = 0
    main_in = refs[i:i + n_main_in]; i += n_main_in
    side_in = refs[i:i + ns]; i += ns
    main_out = refs[i:i + n_main_out]; i += n_main_out
    side_out = refs[i:i + ns]; i += ns
    main_scr = refs[i:i + n_main_scratch]; i += n_main_scratch
    return main_in, side_in, main_out, side_out, main_scr, refs[i:]


def _all_to_all(name, gs):
    side = _Direct("scatter", gs)

    def body(*refs):
        _, side_in, _, side_out, _, sems = _split_side(refs, 0, 0, 0, side)
        side.start(side_in, side_out, sems)
        side.finish(side_in, side_out, sems)

    return pl.pallas_call(
        body, name=name, out_shape=side.out_shape,
        in_specs=[_ANY] * side.n, out_specs=[_ANY] * side.n, scratch_shapes=side.scratch,
    )(*gs)


def _unblock(name, blocks, layer, n_out):
    _, _, K, nb = blocks.shape
    tk = _tile(K, 256, 16)

    def body(b_ref, o_ref):
        for j in range(N_DEV):
            o_ref[:, j * nb:(j + 1) * nb] = b_ref[j]
        if n_out > N_DEV * nb:
            o_ref[:, N_DEV * nb:n_out] = jnp.zeros((tk, n_out - N_DEV * nb), o_ref.dtype)

    return pl.pallas_call(
        body, name=name, grid=(K // tk,),
        in_specs=[pl.BlockSpec((N_DEV, None, tk, nb), lambda i: (0, layer, i, 0))],
        out_specs=pl.BlockSpec((tk, n_out), lambda i: (i, 0)),
        out_shape=jax.ShapeDtypeStruct((K, n_out), blocks.dtype),
        compiler_params=_cp(("parallel",)),
    )(blocks)


def _block(name, full, nb, dtype):
    K, N = full.shape
    tk = _tile(K, 256, 16)

    def body(f_ref, o_ref):
        for j in range(N_DEV):
            o_ref[j] = f_ref[:, j * nb:(j + 1) * nb].astype(o_ref.dtype)

    return pl.pallas_call(
        body, name=name, grid=(K // tk,),
        in_specs=[pl.BlockSpec((tk, N), lambda i: (i, 0))],
        out_specs=pl.BlockSpec((N_DEV, tk, nb), lambda i: (0, i, 0)),
        out_shape=jax.ShapeDtypeStruct((N_DEV, K, nb), dtype),
        compiler_params=_cp(("parallel",)),
    )(full)


ADAM_BLOCK_ELEMS = 256 * 1024
GRAD_WIRE = jnp.bfloat16


def _adamw(name, parts, w, m, v, layer=0):
    shape = w.shape[1:]
    cols = shape[-1]
    rows = math.prod(shape[:-1])
    nl = w.shape[0]
    parts, w, m, v = parts.reshape(N_DEV, rows, cols), w.reshape(nl, rows, cols), m.reshape(nl, rows, cols), v.reshape(nl, rows, cols)
    tr = _tile(rows, max(16, ADAM_BLOCK_ELEMS // cols), 16)

    def body(p_ref, w_ref, m_ref, v_ref, g_ref, d_ref, mo_ref, vo_ref):
        g = p_ref[0].astype(F32)
        for s in range(1, N_DEV):
            g = g + p_ref[s].astype(F32)
        mn = ADAM_B1 * m_ref[...] + (1.0 - ADAM_B1) * g
        vn = ADAM_B2 * v_ref[...] + (1.0 - ADAM_B2) * jnp.square(g)
        m_hat = mn / (1.0 - ADAM_B1 ** ADAM_STEP)
        v_hat = vn / (1.0 - ADAM_B2 ** ADAM_STEP)
        g_ref[...] = g
        d_ref[...] = -ADAM_LR * (m_hat / (jnp.sqrt(v_hat) + ADAM_EPS) + ADAM_WD * w_ref[...])
        mo_ref[...] = mn
        vo_ref[...] = vn

    blk = pl.BlockSpec((tr, cols), lambda i: (i, 0))
    lblk = pl.BlockSpec((None, tr, cols), lambda i: (layer, i, 0))
    shp = jax.ShapeDtypeStruct((rows, cols), F32)
    res = pl.pallas_call(
        body, name=name, grid=(rows // tr,),
        in_specs=[pl.BlockSpec((N_DEV, tr, cols), lambda i: (0, i, 0)), lblk, lblk, lblk],
        out_specs=[blk, blk, blk, blk], out_shape=[shp, shp, shp, shp],
        compiler_params=_cp(("parallel",)),
    )(parts, w, m, v)
    return [r.reshape(shape) for r in res]


SHARDED = dict(xa_wq=1, xa_wkv=1, xa_wo=2, w_gate=2, w_up=2, w_down=1, e_w_in=2, e_w_out=1, o_w_in=2, o_s_pool=1, o_conv_w=2, o_w_out=1)
LAYER_SPLIT = ("xa_wq", "xa_wkv", "xa_wo", "w_gate", "w_up", "w_down")
ELEMENTWISE_SHARDED = ("o_s_pool", "o_conv_w")
REPLICATED = ("g_mix", "g_xa", "g_mem", "xa_gq", "xa_gk", "g_ffn", "e_b_f", "e_g_v", "e_w_s", "e_b_s", "e_g_qn", "e_g_kn", "o_w_pool")
WEIGHTS = ("g_mix", "g_xa", "g_mem", "xa_wq", "xa_wkv", "xa_wo", "xa_gq", "xa_gk", "g_ffn", "w_gate", "w_up", "w_down", "e_w_in", "e_b_f",
           "e_g_v", "e_w_s", "e_b_s", "e_g_qn", "e_g_kn", "e_w_out", "o_w_in", "o_w_pool", "o_s_pool", "o_conv_w", "o_w_out")


def _rows_for(n, mult):
    return -(-n // (PACK_W * mult)) * mult


def _pack(arrs, rows, dtype):
    flat = jnp.concatenate([a.reshape(-1).astype(dtype) for a in arrs])
    return jnp.pad(flat, (0, rows * PACK_W - flat.shape[0])).reshape(rows, PACK_W)


def _unpack(slab, shapes):
    flat, out, off = slab.reshape(-1), [], 0
    for s in shapes:
        n = math.prod(s)
        out.append(flat[off:off + n].reshape(s))
        off += n
    return out


def _to_blocks(full, axis):
    s = full.shape
    return jnp.moveaxis(full.reshape(s[:axis] + (N_DEV, s[axis] // N_DEV) + s[axis + 1:]), axis, 0)


def _from_blocks(blocks, axis):
    b = jnp.moveaxis(blocks, 0, axis)
    s = b.shape
    return b.reshape(s[:axis] + (s[axis] * s[axis + 1],) + s[axis + 2:])


def kernel(x, mem, g_mix, g_xa, g_mem, xa_wq, xa_wkv, xa_wo, xa_gq, xa_gk, g_ffn, w_gate, w_up, w_down, e_w_in, e_b_f, e_g_v, e_w_s, e_b_s, e_g_qn, e_g_kn, e_w_out, o_w_in, o_w_pool, o_s_pool, o_conv_w, o_w_out, loss_target, m_g_mix, m_g_xa, m_g_mem, m_xa_wq, m_xa_wkv, m_xa_wo, m_xa_gq, m_xa_gk, m_g_ffn, m_w_gate, m_w_up, m_w_down, m_e_w_in, m_e_b_f, m_e_g_v, m_e_w_s, m_e_b_s, m_e_g_qn, m_e_g_kn, m_e_w_out, m_o_w_in, m_o_w_pool, m_o_s_pool, m_o_conv_w, m_o_w_out, v_g_mix, v_g_xa, v_g_mem, v_xa_wq, v_xa_wkv, v_xa_wo, v_xa_gq, v_xa_gk, v_g_ffn, v_w_gate, v_w_up, v_w_down, v_e_w_in, v_e_b_f, v_e_g_v, v_e_w_s, v_e_b_s, v_e_g_qn, v_e_g_kn, v_e_w_out, v_o_w_in, v_o_w_pool, v_o_s_pool, v_o_conv_w, v_o_w_out):
    args = dict(locals())
    Wt = {n: args[n] for n in WEIGHTS}
    Mo = {n: args["m_" + n] for n in WEIGHTS}
    Vo = {n: args["v_" + n] for n in WEIGHTS}
    rep_names = REPLICATED
    rep_shapes = [Wt[n].shape for n in rep_names]
    rep_rows = _rows_for(sum(math.prod(s) for s in rep_shapes) + 1, 16)
    layer_names = [LAYER_SPLIT + ("e_w_in", "e_w_out"), LAYER_SPLIT + ("o_w_in", "o_w_out", "o_s_pool", "o_conv_w")]

    def shard_of(n, l, src=Wt):
        return src[n][l:l + 1] if n in LAYER_SPLIT else src[n]

    def to_wire(n, a):
        return a if n in ELEMENTWISE_SHARDED else _mx(a)

    def full_weight(n, l, gn):
        if SHARDED[n] == 1 or n in ELEMENTWISE_SHARDED:
            return _from_blocks(gn, SHARDED[n])[0]
        return _unblock(f"unblock_{n}{l}", gn, 0, EVEN_IN_PAD if n == "e_w_in" else N_DEV * gn.shape[-1])

    def grad_slabs(n, l, gl):
        if n in ELEMENTWISE_SHARDED:
            return _to_blocks(gl.reshape((1,) + Wt[n].shape[1:-1] + (-1,)), SHARDED[n])
        if SHARDED[n] == 1:
            return _to_blocks(gl[None], 1).astype(GRAD_WIRE)
        return _block(f"block_{n}{l}", gl, Wt[n].shape[-1], GRAD_WIRE)[:, None]

    xk, mem2 = x[0], mem[0]
    later = [(n, 0) for n in layer_names[0] if n != "e_w_in"] + [(n, 1) for n in layer_names[1]]
    w_in0 = full_weight("e_w_in", 0, _all_gather("gather_e_w_in", [to_wire("e_w_in", Wt["e_w_in"])])[0])
    side = _TwoLevelGather([to_wire(n, shard_of(n, l)) for n, l in later])
    P = {n: Wt[n] for n in rep_names}
    P["e_w_in"] = {0: w_in0}

    def on_side(gathered):
        for (n, l), gn in zip(later, gathered):
            P.setdefault(n, {})[l if n in LAYER_SPLIT else 0] = full_weight(n, l, gn)

    xk, sv0, W0 = _layer_fwd(0, xk, mem2, P, side=side, on_side=on_side)
    xk, sv1, W1 = _layer_fwd(1, xk, mem2, P)

    g, loss_row = _loss_grad("loss_grad", xk, loss_target[0])
    g, G1, _ = _layer_bwd(1, g, mem2, W1, sv1)
    slabs1 = [grad_slabs(n, 1, G1[n]) for n in layer_names[1]]
    g, G0, beside = _layer_bwd(0, g, mem2, W0, sv0,
                               side_fn=lambda G: _Direct("scatter", slabs1 + [grad_slabs(n, 0, G[n]) for n in LAYER_SPLIT]))
    parts1, parts0a = beside[:len(slabs1)], beside[len(slabs1):]
    grad_x = g[None]
    parts0b = _all_to_all("exchange_grads_mixer0", [grad_slabs(n, 0, G0[n]) for n in ("e_w_in", "e_w_out")])
    parts = {(n, 1): p for n, p in zip(layer_names[1], parts1)}
    parts.update({(n, 0): p for n, p in zip(LAYER_SPLIT, parts0a)})
    parts.update({(n, 0): p for n, p in zip(("e_w_in", "e_w_out"), parts0b)})
    G = {n: {0: G0[n]} for n in G0}
    for n in G1:
        G.setdefault(n, {})[1] = G1[n]
    rep_g = [jnp.stack([G[n][l] for l in sorted(G[n])]).reshape(Wt[n].shape) for n in rep_names]
    rep_parts = _all_gather("gather_small_grads", [_pack(rep_g + [jnp.sum(loss_row).reshape(1)], rep_rows, F32)])[0]

    outs = {}
    for n in SHARDED:
        per_layer = [_adamw(f"adamw_{n}{l}", parts[(n, l if n in LAYER_SPLIT else (0 if n.startswith("e_") else 1))], Wt[n], Mo[n], Vo[n], layer=l)
                     for l in range(Wt[n].shape[0])]
        for k, kind in enumerate(("grad", "delta", "new_m", "new_v")):
            outs[kind + "_" + n] = jnp.stack([r[k] for r in per_layer])
    res = _adamw("adamw_replicated", rep_parts, _pack([Wt[n] for n in rep_names], rep_rows, F32)[None],
                 _pack([Mo[n] for n in rep_names], rep_rows, F32)[None], _pack([Vo[n] for n in rep_names], rep_rows, F32)[None])
    for kind, slab in zip(("grad", "delta", "new_m", "new_v"), res):
        for n, a in zip(rep_names, _unpack(slab, rep_shapes)):
            outs[kind + "_" + n] = a
    loss = res[0].reshape(-1)[sum(math.prod(s) for s in rep_shapes)]
    return (loss, grad_x, *[outs[k + "_" + n] for k in ("grad", "delta", "new_m", "new_v") for n in WEIGHTS])
```

```python
import functools
import math

import numpy as np
import jax
import jax.numpy as jnp
from jax import lax
from jax.experimental import pallas as pl
from jax.experimental.pallas import tpu as pltpu

F32 = jnp.float32
BF16 = jnp.bfloat16
_MXU = jnp.bfloat16

D_MODEL = 1024
N_DEV = 8
LANES = 128
SUBLANES = 8
HALO = 16
CHUNK = 64
A_GROUPS, A_GROUP_DIM, A_WIDTH, A_BLOCK = 4, 128, 512, 128
FOX_HEADS, FOX_HEAD_DIM, FOX_WIDTH = 8, 64, 512
FOX_TILE = 512
POOL_WINDOWS = (2, 4, 8, 16)
POOL_WIDTH, CONV_WIDTH, CONV_K = 512, 512, 3
EVEN_IN, EVEN_IN_PAD, ODD_IN = 2568, 2688, 2048
XA_HEADS, XA_HEAD_DIM, XA_WIDTH = 4, 128, 512
D_FF = 2816
EPS = 1e-6
ADAM_LR, ADAM_B1, ADAM_B2, ADAM_EPS, ADAM_WD, ADAM_STEP = 0.001, 0.9, 0.999, 1e-08, 0.01, 10
VMEM_LIMIT = 48 * 1024 * 1024


def _cp(sem):
    return pltpu.CompilerParams(dimension_semantics=sem, vmem_limit_bytes=VMEM_LIMIT)


def _tile(n, cap, q):
    best = None
    for d in range(q, min(n, cap) + 1, q):
        if n % d == 0:
            best = d
    return best if best is not None else n


def _mx(a):
    return a.astype(_MXU)


def _dot(a, b):
    return jnp.dot(_mx(a), _mx(b), preferred_element_type=F32)


def _dot_nt(a, b):
    return lax.dot_general(_mx(a), _mx(b), (((1,), (1,)), ((), ())), preferred_element_type=F32)


def _dot_tn(a, b):
    return lax.dot_general(_mx(a), _mx(b), (((0,), (0,)), ((), ())), preferred_element_type=F32)


def _dot_f32(a, b):
    return jnp.dot(a, b, precision=lax.Precision.HIGHEST, preferred_element_type=F32)


def _lane(shape):
    return lax.broadcasted_iota(jnp.int32, shape, len(shape) - 1)


def _row(shape):
    return lax.broadcasted_iota(jnp.int32, shape, len(shape) - 2)


def _mm(name, pairs, mode, out_dtype=F32, residual=None, norm_bwd=None, tm_cap=512, tn_cap=1536, tk_cap=2048):
    a0, b0 = pairs[0]
    if mode == "nn":
        (M, K), N = a0.shape, b0.shape[1]
    elif mode == "nt":
        (M, K), N = a0.shape, b0.shape[0]
    else:
        (K, M), N = a0.shape, b0.shape[1]
    if mode == "tn":
        tm, tk = _tile(M, 1408, LANES), _tile(K, 512, 16)
    else:
        tm, tk = _tile(M, tm_cap, 16), _tile(K, tk_cap, LANES)
    tn = _tile(N, tn_cap, LANES)
    nk = K // tk
    npairs = len(pairs)
    dot = {"nn": _dot, "nt": _dot_nt, "tn": _dot_tn}[mode]

    if norm_bwd is not None:
        assert tn == N and residual is None and mode != "tn", "the norm backward needs whole rows"

    def body(*refs):
        ab = refs[:2 * npairs]
        res_ref = refs[2 * npairs] if residual is not None else None
        k = pl.program_id(2)
        first_row_tile = pl.program_id(1) == 0
        if norm_bwd is not None:
            x_ref, g_ref, r_ref = refs[2 * npairs:2 * npairs + 3]
            o_ref, dg_ref, acc = refs[-3], refs[-2], refs[-1]
        else:
            o_ref, acc = refs[-2], refs[-1]

        @pl.when(k == 0)
        def _():
            acc[...] = jnp.zeros_like(acc)

        for p in range(npairs):
            acc[...] += dot(ab[2 * p][...], ab[2 * p + 1][...])

        @pl.when(k == nk - 1)
        def _():
            out = acc[...]
            if res_ref is not None:
                out = out + res_ref[...]
            if norm_bwd is not None:
                xv = x_ref[...]
                r = _rms_rows(xv)
                xhat = xv * r
                dxhat = out * g_ref[...]

                @pl.when(first_row_tile)
                def _():
                    dg_ref[...] = jnp.zeros_like(dg_ref)

                dg_ref[...] += jnp.sum(out * xhat, axis=0, keepdims=True)
                out = r_ref[...] + r * (dxhat - xhat * jnp.mean(dxhat * xhat, axis=-1, keepdims=True))
            o_ref[...] = out.astype(o_ref.dtype)

    if mode == "nn":
        a_spec = pl.BlockSpec((tm, tk), lambda j, i, k: (i, k))
        b_spec = pl.BlockSpec((tk, tn), lambda j, i, k: (k, j))
    elif mode == "nt":
        a_spec = pl.BlockSpec((tm, tk), lambda j, i, k: (i, k))
        b_spec = pl.BlockSpec((tn, tk), lambda j, i, k: (j, k))
    else:
        a_spec = pl.BlockSpec((tk, tm), lambda j, i, k: (k, i))
        b_spec = pl.BlockSpec((tk, tn), lambda j, i, k: (k, j))
    o_spec = pl.BlockSpec((tm, tn), lambda j, i, k: (i, j))
    in_specs, args = [], []
    for a, b in pairs:
        in_specs += [a_spec, b_spec]
        args += [a, b]
    if residual is not None:
        in_specs.append(o_spec)
        args.append(residual)
    out_specs, out_shape, sem = o_spec, jax.ShapeDtypeStruct((M, N), out_dtype), ("parallel", "parallel", "arbitrary")
    if norm_bwd is not None:
        x, g, res = norm_bwd
        vec = pl.BlockSpec((1, tn), lambda j, i, k: (0, j))
        in_specs += [o_spec, vec, o_spec]
        args += [x, g.reshape(1, N), res]
        out_specs, out_shape = [o_spec, vec], [out_shape, jax.ShapeDtypeStruct((1, N), F32)]
        sem = ("arbitrary", "arbitrary", "arbitrary")
    return pl.pallas_call(
        body, name=name, grid=(N // tn, M // tm, nk),
        in_specs=in_specs, out_specs=out_specs, out_shape=out_shape,
        scratch_shapes=[pltpu.VMEM((tm, tn), F32)],
        compiler_params=_cp(sem),
    )(*args)


def _rms_rows(x):
    return lax.rsqrt(jnp.mean(x * x, axis=-1, keepdims=True) + EPS)


def _norm_fwd(name, x, g):
    T, Dm = x.shape
    tm = _tile(T, 512, 16)

    def body(x_ref, g_ref, o_ref):
        xv = x_ref[...]
        o_ref[...] = ((xv * _rms_rows(xv)) * g_ref[...]).astype(o_ref.dtype)

    return pl.pallas_call(
        body, name=name, grid=(T // tm,),
        in_specs=[pl.BlockSpec((tm, Dm), lambda i: (i, 0)), pl.BlockSpec((1, Dm), lambda i: (0, 0))],
        out_specs=pl.BlockSpec((tm, Dm), lambda i: (i, 0)),
        out_shape=jax.ShapeDtypeStruct((T, Dm), _MXU),
        compiler_params=_cp(("parallel",)),
    )(x, g.reshape(1, Dm))


def _gelu(x):
    return jax.nn.gelu(x)


def _gelu_grad(x):
    c0, c1 = math.sqrt(2.0 / math.pi), 0.044715
    t = jnp.tanh(c0 * (x + c1 * x * x * x))
    return 0.5 * (1.0 + t) + 0.5 * x * (1.0 - t * t) * c0 * (1.0 + 3.0 * c1 * x * x)


def _head64_rms(x):
    lo = _lane(x.shape) < FOX_HEAD_DIM
    xx = x * x
    sa = jnp.sum(jnp.where(lo, xx, 0.0), axis=-1, keepdims=True)
    sb = jnp.sum(jnp.where(lo, 0.0, xx), axis=-1, keepdims=True)
    inv = 1.0 / FOX_HEAD_DIM
    return jnp.where(lo, lax.rsqrt(sa * inv + EPS), lax.rsqrt(sb * inv + EPS))


def _head64_mean(x):
    lo = _lane(x.shape) < FOX_HEAD_DIM
    sa = jnp.sum(jnp.where(lo, x, 0.0), axis=-1, keepdims=True)
    sb = jnp.sum(jnp.where(lo, 0.0, x), axis=-1, keepdims=True)
    return jnp.where(lo, sa, sb) * (1.0 / FOX_HEAD_DIM)


def _masked_ws(ws_ref, g):
    w = ws_ref[g]
    keep = (_row(w.shape) // CHUNK) >= (_lane(w.shape) // CHUNK)
    return jnp.where(keep, w, 0.0)


def _even_pre(name, z, g_v, w_s, bs_cols, g_qn2, g_kn2, b_f_pad):
    T = z.shape[0]
    tm = _tile(T, 256, A_BLOCK)
    nb = tm // A_BLOCK

    def body(z_ref, gv_ref, ws_ref, bs_ref, gq_ref, gk_ref, bf_ref, ya_ref, q_ref, k_ref, v_ref, c_ref, ct_ref, carry):
        i = pl.program_id(0)

        @pl.when(i == 0)
        def _():
            carry[...] = jnp.zeros_like(carry)

        wm = [_mx(_masked_ws(ws_ref, g)) for g in range(A_GROUPS)]
        for g in range(A_GROUPS):
            sl = slice(g * A_GROUP_DIM, (g + 1) * A_GROUP_DIM)
            u = _gelu(z_ref[:, sl])
            v = _gelu(z_ref[:, A_WIDTH + g * A_GROUP_DIM:A_WIDTH + (g + 1) * A_GROUP_DIM])
            vn = _mx((v * _rms_rows(v)) * gv_ref[:, sl])
            for n in range(nb):
                rows = slice(n * A_BLOCK, (n + 1) * A_BLOCK)
                s = jnp.dot(wm[g], vn[rows], preferred_element_type=F32) + bs_ref[:, g:g + 1]
                ya_ref[rows, sl] = (u[rows] * s).astype(ya_ref.dtype)
        o = 2 * A_WIDTH
        for j in range(FOX_WIDTH // LANES):
            sl = slice(j * LANES, (j + 1) * LANES)
            q = z_ref[:, o + j * LANES:o + (j + 1) * LANES]
            q_ref[:, sl] = (((q * _head64_rms(q)) * gq_ref[...]) * (1.0 / math.sqrt(FOX_HEAD_DIM))).astype(q_ref.dtype)
            k = z_ref[:, o + FOX_WIDTH + j * LANES:o + FOX_WIDTH + (j + 1) * LANES]
            k_ref[:, sl] = ((k * _head64_rms(k)) * gk_ref[...]).astype(k_ref.dtype)
            v_ref[:, sl] = z_ref[:, o + 2 * FOX_WIDTH + j * LANES:o + 2 * FOX_WIDTH + (j + 1) * LANES].astype(v_ref.dtype)
        zf = z_ref[:, o + 3 * FOX_WIDTH:o + 3 * FOX_WIDTH + LANES]
        logf = jnp.where(_lane(zf.shape) < FOX_HEADS, jax.nn.log_sigmoid(zf + bf_ref[...]), 0.0)
        tri = (_row((tm, tm)) >= _lane((tm, tm))).astype(F32)
        c = _dot_f32(tri, logf) + carry[0:1, :]
        c_ref[...] = c
        ct_ref[...] = c.T[0:SUBLANES, :]
        carry[0:1, :] = c[tm - 1:tm, :]

    wide = lambda n: pl.BlockSpec((tm, n), lambda i: (i, 0))
    full = lambda s: pl.BlockSpec(s, lambda i: tuple(0 for _ in s))
    out512 = jax.ShapeDtypeStruct((T, 512), _MXU)
    return pl.pallas_call(
        body, name=name, grid=(T // tm,),
        in_specs=[wide(EVEN_IN_PAD), full((1, A_WIDTH)), full((A_GROUPS, A_BLOCK, A_BLOCK)), full((A_BLOCK, LANES)),
                  full((1, LANES)), full((1, LANES)), full((1, LANES))],
        out_specs=[wide(512), wide(512), wide(512), wide(512), wide(LANES), pl.BlockSpec((SUBLANES, tm), lambda i: (0, i))],
        out_shape=[out512, out512, out512, out512, jax.ShapeDtypeStruct((T, LANES), F32), jax.ShapeDtypeStruct((SUBLANES, T), F32)],
        scratch_shapes=[pltpu.VMEM((SUBLANES, LANES), F32)],
        compiler_params=_cp(("arbitrary",)),
    )(z, g_v, w_s, bs_cols, g_qn2, g_kn2, b_f_pad)


def _tri_steps(n, by_rows):
    if by_rows:
        pairs = [(q, k) for q in range(n) for k in range(q + 1)]
    else:
        pairs = [(q, k) for k in range(n) for q in range(k, n)]
    return (jnp.asarray(np.array([p[0] for p in pairs], np.int32)), jnp.asarray(np.array([p[1] for p in pairs], np.int32)))


def _fox_logits(q, k, c_col, c_row, diag, tq):
    s = _dot_nt(q, k) + (c_col - c_row)
    if diag:
        s = jnp.where(_row((tq, tq)) >= _lane((tq, tq)), s, -jnp.inf)
    return s


FOX_SKIP_BELOW = 104.0


def _fox_skip_flags(c, g_qn2, g_kn2):
    T = c.shape[0]
    tq = _tile(T, FOX_TILE, LANES)
    nblk = T // tq
    cb = c.reshape(nblk, tq, LANES)
    c_first, c_last = cb[:, 0, :FOX_HEADS], cb[:, tq - 1, :FOX_HEADS]
    bound = 17.0 * jnp.max(jnp.abs(g_qn2)) * jnp.max(jnp.abs(g_kn2))
    keep = bound + c_first[:, None, :] - c_last[None, :, :] > -FOX_SKIP_BELOW
    return keep.astype(F32).reshape(-1)


def _on_diag_or_not(qi, ki, step):
    @pl.when(qi == ki)
    def _():
        step(True)

    @pl.when(qi != ki)
    def _():
        step(False)


def _side_specs(side):
    if side is None:
        return [], [], [], []
    return [pl.BlockSpec(memory_space=pl.ANY)] * side.n, list(side.arrays), list(side.out_shape), list(side.scratch)


def _side_run(side, s_id, nsteps, side_in, side_out, sems):
    if side is None:
        return

    @pl.when(s_id == 0)
    def _():
        side.start(side_in, side_out, sems)

    @pl.when(s_id == nsteps // 2)
    def _():
        side.middle(side_in, side_out, sems)

    @pl.when(s_id == nsteps - 1)
    def _():
        side.finish(side_in, side_out, sems)


def _fox_fwd(name, qn, kn, vv, c, ct, flags, side=None):
    T = qn.shape[0]
    tq = _tile(T, FOX_TILE, LANES)
    nblk = T // tq
    qs, ks = _tri_steps(nblk, True)
    nsteps = int(qs.shape[0])
    npair = FOX_WIDTH // LANES
    s_specs, s_args, s_shapes, s_scratch = _side_specs(side)

    def body(qs_ref, ks_ref, *refs):
        main_in, side_in, main_out, side_out, main_scr, sems = _split_side(refs, 6, 3, 3, side)
        q_ref, k_ref, v_ref, c_ref, ct_ref, flag_ref = main_in
        yb_ref, yf_ref, lse_ref = main_out
        acc, m_s, l_s = main_scr
        s_id = pl.program_id(0)
        qi, ki = qs_ref[s_id], ks_ref[s_id]
        _side_run(side, s_id, nsteps, side_in, side_out, sems)

        @pl.when(ki == 0)
        def _():
            acc[...] = jnp.zeros_like(acc)
            m_s[...] = jnp.full_like(m_s, -jnp.inf)
            l_s[...] = jnp.zeros_like(l_s)

        lo = _lane((tq, LANES)) < FOX_HEAD_DIM

        def step(diag):
            for j in range(npair):
                sl = slice(j * LANES, (j + 1) * LANES)
                for e in range(2):
                    h = 2 * j + e
                    sel = lo if e == 0 else ~lo

                    def head(h=h, sel=sel, sl=sl):
                        q2, k2, v2 = q_ref[:, sl], k_ref[:, sl], v_ref[:, sl]
                        qh = jnp.where(sel, q2, jnp.zeros_like(q2))
                        s = _fox_logits(qh, k2, c_ref[:, h:h + 1], ct_ref[h:h + 1, :], diag, tq)
                        m_old = m_s[h]
                        m_new = jnp.maximum(m_old, jnp.max(s, axis=-1, keepdims=True))
                        alpha = jnp.exp(m_old - m_new)
                        p = jnp.exp(s - m_new)
                        l_s[h] = alpha * l_s[h] + jnp.sum(p, axis=-1, keepdims=True)
                        m_s[h] = m_new
                        acc[:, sl] = jnp.where(sel, alpha * acc[:, sl] + _dot(p, v2), acc[:, sl])

                    if diag:
                        head()
                    else:
                        pl.when(flag_ref[(qi * nblk + ki) * FOX_HEADS + h] > 0.5)(head)

        _on_diag_or_not(qi, ki, step)

        @pl.when(ki == qi)
        def _():
            lse = jnp.zeros((tq, LANES), F32)
            for j in range(npair):
                sl = slice(j * LANES, (j + 1) * LANES)
                inv = jnp.where(lo, 1.0 / l_s[2 * j], 1.0 / l_s[2 * j + 1])
                y = acc[:, sl] * inv
                yf_ref[:, sl] = y
                yb_ref[:, sl] = y.astype(yb_ref.dtype)
            for h in range(FOX_HEADS):
                lse = jnp.where(_lane((tq, LANES)) == h, m_s[h] + jnp.log(l_s[h]), lse)
            lse_ref[...] = lse

    qmap = lambda s, qs_r, ks_r: (qs_r[s], 0)
    kmap = lambda s, qs_r, ks_r: (ks_r[s], 0)
    grid_spec = pltpu.PrefetchScalarGridSpec(
        num_scalar_prefetch=2, grid=(nsteps,),
        in_specs=[pl.BlockSpec((tq, 512), qmap), pl.BlockSpec((tq, 512), kmap), pl.BlockSpec((tq, 512), kmap),
                  pl.BlockSpec((tq, LANES), qmap),
                  pl.BlockSpec((SUBLANES, tq), lambda s, qs_r, ks_r: (0, ks_r[s])), pl.BlockSpec(memory_space=pltpu.SMEM)] + s_specs,
        out_specs=[pl.BlockSpec((tq, 512), qmap), pl.BlockSpec((tq, 512), qmap), pl.BlockSpec((tq, LANES), qmap)] + s_specs,
        scratch_shapes=[pltpu.VMEM((tq, 512), F32), pltpu.VMEM((FOX_HEADS, tq, 1), F32), pltpu.VMEM((FOX_HEADS, tq, 1), F32)] + s_scratch,
    )
    res = pl.pallas_call(
        body, name=name, grid_spec=grid_spec,
        out_shape=[jax.ShapeDtypeStruct((T, 512), _MXU), jax.ShapeDtypeStruct((T, 512), F32), jax.ShapeDtypeStruct((T, LANES), F32)] + s_shapes,
        compiler_params=_cp(("arbitrary",)),
    )(qs, ks, qn, kn, vv, c, ct, flags, *s_args)
    return res[:3], res[3:]


def _shift_down(ext, k, tm):
    return pltpu.roll(ext, k, axis=0)[HALO:HALO + tm]


def _pool_p(z_ext, g, tm, pos):
    w = POOL_WINDOWS[g]
    s, span = z_ext, 1
    while span < w:
        s = s + pltpu.roll(s, span, axis=0)
        span *= 2
    cnt = jnp.minimum(pos + 1, w).astype(F32)
    return s[HALO:HALO + tm] / cnt - z_ext[HALO:HALO + tm]


def _odd_fwd(name, z, w_pool, s_pool, conv_w8):
    T = z.shape[0]
    tm = _tile(T, 256, HALO)
    r = tm // HALO

    def body(z_ref, zb_ref, wp_ref, sp_ref, cw_ref, y_ref):
        i = pl.program_id(0)
        first = i == 0
        pos = i * tm + _row((tm, LANES))
        for g in range(len(POOL_WINDOWS)):
            sl = slice(g * LANES, (g + 1) * LANES)
            halo = jnp.where(first, 0.0, zb_ref[:, sl])
            z_ext = jnp.concatenate([halo, z_ref[:, sl]], axis=0)
            p = _pool_p(z_ext, g, tm, pos)
            y_ref[:, sl] = (_dot(p, wp_ref[g]) * sp_ref[:, sl]).astype(y_ref.dtype)
        for j in range(CONV_WIDTH // LANES):
            sl = slice(j * LANES, (j + 1) * LANES)
            hd = slice(POOL_WIDTH + j * LANES, POOL_WIDTH + (j + 1) * LANES)
            gb = slice(POOL_WIDTH + CONV_WIDTH + j * LANES, POOL_WIDTH + CONV_WIDTH + (j + 1) * LANES)
            gc = slice(POOL_WIDTH + 2 * CONV_WIDTH + j * LANES, POOL_WIDTH + 2 * CONV_WIDTH + (j + 1) * LANES)
            xg_b = jnp.where(first, 0.0, zb_ref[:, gc] * zb_ref[:, hd])
            xg = jnp.concatenate([xg_b, z_ref[:, gc] * z_ref[:, hd]], axis=0)
            conv = (cw_ref[0:1, sl] * _shift_down(xg, 2, tm) + cw_ref[1:2, sl] * _shift_down(xg, 1, tm)
                    + cw_ref[2:3, sl] * xg[HALO:HALO + tm])
            y_ref[:, POOL_WIDTH + j * LANES:POOL_WIDTH + (j + 1) * LANES] = (z_ref[:, gb] * conv).astype(y_ref.dtype)

    full = lambda s: pl.BlockSpec(s, lambda i: tuple(0 for _ in s))
    return pl.pallas_call(
        body, name=name, grid=(T // tm,),
        in_specs=[pl.BlockSpec((tm, ODD_IN), lambda i: (i, 0)),
                  pl.BlockSpec((HALO, ODD_IN), lambda i: (jnp.maximum(i * r - 1, 0), 0)),
                  full((4, LANES, LANES)), full((1, POOL_WIDTH)), full((SUBLANES, CONV_WIDTH))],
        out_specs=pl.BlockSpec((tm, 1024), lambda i: (i, 0)),
        out_shape=jax.ShapeDtypeStruct((T, 1024), _MXU),
        compiler_params=_cp(("parallel",)),
    )(z, z, w_pool, s_pool, conv_w8)


def _mem_fwd(name, mem, g_mem, wkv, gk):
    M = mem.shape[0]

    def body(mem_ref, g_ref, w_ref, gk_ref, m_ref, kv_ref, kn_ref, v_ref):
        x = mem_ref[...]
        m = ((x * _rms_rows(x)) * g_ref[...]).astype(m_ref.dtype)
        m_ref[...] = m
        kv = _dot(m, w_ref[...])
        kv_ref[...] = kv
        for h in range(XA_HEADS):
            sl = slice(h * LANES, (h + 1) * LANES)
            k = kv[:, sl]
            kn_ref[:, sl] = ((k * _rms_rows(k)) * gk_ref[...]).astype(kn_ref.dtype)
        v_ref[...] = kv[:, XA_WIDTH:].astype(v_ref.dtype)

    return pl.pallas_call(
        body, name=name,
        out_shape=[jax.ShapeDtypeStruct((M, D_MODEL), _MXU), jax.ShapeDtypeStruct((M, 2 * XA_WIDTH), F32),
                   jax.ShapeDtypeStruct((M, XA_WIDTH), _MXU), jax.ShapeDtypeStruct((M, XA_WIDTH), _MXU)],
        compiler_params=pltpu.CompilerParams(vmem_limit_bytes=VMEM_LIMIT),
    )(mem, g_mem.reshape(1, D_MODEL), wkv, gk.reshape(1, XA_HEAD_DIM))


def _xa_probs(qx, gq, kn_h):
    r = _rms_rows(qx)
    qhat = qx * r
    qn = qhat * gq
    s = _dot_nt(qn, kn_h) * (1.0 / math.sqrt(XA_HEAD_DIM))
    s = s - jnp.max(s, axis=-1, keepdims=True)
    e = jnp.exp(s)
    return qhat, r, qn, e / jnp.sum(e, axis=-1, keepdims=True)


def _xa_fwd(name, qx, gq, kn, vx):
    T, M = qx.shape[0], kn.shape[0]
    tm = _tile(T, 512, 16)

    def body(q_ref, gq_ref, k_ref, v_ref, o_ref):
        for h in range(XA_HEADS):
            sl = slice(h * LANES, (h + 1) * LANES)
            _, _, _, p = _xa_probs(q_ref[:, sl], gq_ref[...], k_ref[:, sl])
            o_ref[:, sl] = _dot(p, v_ref[:, sl]).astype(o_ref.dtype)

    full = lambda s: pl.BlockSpec(s, lambda i: tuple(0 for _ in s))
    return pl.pallas_call(
        body, name=name, grid=(T // tm,),
        in_specs=[pl.BlockSpec((tm, XA_WIDTH), lambda i: (i, 0)), full((1, LANES)), full((M, XA_WIDTH)), full((M, XA_WIDTH))],
        out_specs=pl.BlockSpec((tm, XA_WIDTH), lambda i: (i, 0)),
        out_shape=jax.ShapeDtypeStruct((T, XA_WIDTH), _MXU),
        compiler_params=_cp(("parallel",)),
    )(qx, gq.reshape(1, XA_HEAD_DIM), kn, vx)


def _ffn_up(name, h, wg, wu):
    T, K = h.shape
    N = wg.shape[1]
    tm, tn = _tile(T, 512, 16), _tile(N, 1536, LANES)

    def body(h_ref, wg_ref, wu_ref, a_ref, b_ref, s_ref):
        hv = h_ref[...]
        a, b = _dot(hv, wg_ref[...]), _dot(hv, wu_ref[...])
        a_ref[...] = a
        b_ref[...] = b
        s_ref[...] = (jax.nn.silu(a) * b).astype(s_ref.dtype)

    w_spec = pl.BlockSpec((K, tn), lambda j, i: (0, j))
    o_spec = pl.BlockSpec((tm, tn), lambda j, i: (i, j))
    return pl.pallas_call(
        body, name=name, grid=(N // tn, T // tm),
        in_specs=[pl.BlockSpec((tm, K), lambda j, i: (i, 0)), w_spec, w_spec],
        out_specs=[o_spec, o_spec, o_spec],
        out_shape=[jax.ShapeDtypeStruct((T, N), F32), jax.ShapeDtypeStruct((T, N), F32), jax.ShapeDtypeStruct((T, N), _MXU)],
        compiler_params=_cp(("parallel", "parallel")),
    )(h, wg, wu)


def _loss_grad(name, y, target):
    T, Dm = y.shape
    tm = _tile(T, 512, 8)

    def body(y_ref, t_ref, dy_ref, l_ref):
        e = y_ref[...] - t_ref[...]
        dy_ref[...] = e * (1.0 / Dm)

        @pl.when(pl.program_id(0) == 0)
        def _():
            l_ref[...] = jnp.zeros_like(l_ref)

        l_ref[...] += jnp.sum(e * e, axis=0, keepdims=True) * (0.5 / Dm)

    blk = pl.BlockSpec((tm, Dm), lambda i: (i, 0))
    return pl.pallas_call(
        body, name=name, grid=(T // tm,),
        in_specs=[blk, blk], out_specs=[blk, pl.BlockSpec((1, Dm), lambda i: (0, 0))],
        out_shape=[jax.ShapeDtypeStruct((T, Dm), F32), jax.ShapeDtypeStruct((1, Dm), F32)],
        compiler_params=_cp(("arbitrary",)),
    )(y, target)


def _pad_lanes(v, n=LANES):
    v = v.reshape(1, -1)
    return jnp.pad(v, ((0, 0), (0, n - v.shape[1])))


def _layer_fwd(l, x, mem, P, side=None, on_side=None):
    sv = {"x0": x}
    W = _layer_weights(l, P, "in")
    h1 = _norm_fwd(f"norm_mix{l}", x, W["g_mix"])
    sv["h1"] = h1
    if l % 2 == 0:
        z = _mm(f"e_in{l}", [(h1, W["w_in"])], "nn")
        ya, qn, kn, vv, c, ct = _even_pre(f"even_pre{l}", z, W["g_v"], W["w_s"], W["bs_cols"], W["g_qn2"], W["g_kn2"], W["b_f_pad"])
        flags = _fox_skip_flags(c, W["g_qn2"], W["g_kn2"])
        (yb, ybf, lse), side_out = _fox_fwd(f"fox_fwd{l}", qn, kn, vv, c, ct, flags, side=side)
        if on_side is not None:
            on_side(side_out)
        W.update(_layer_weights(l, P, "rest"))
        sv.update(z=z, ya=ya, qn=qn, kn=kn, vv=vv, c=c, ct=ct, yb=yb, ybf=ybf, lse=lse, flags=flags)
        x1 = _mm(f"e_out{l}", [(ya, W["w_out"][:A_WIDTH]), (yb, W["w_out"][A_WIDTH:])], "nn", residual=x)
    else:
        W.update(_layer_weights(l, P, "rest"))
        z = _mm(f"o_in{l}", [(h1, W["w_in"])], "nn")
        ycat = _odd_fwd(f"odd_fwd{l}", z, W["w_pool"], W["s_pool"], W["conv_w8"])
        sv.update(z=z, ycat=ycat)
        x1 = _mm(f"o_out{l}", [(ycat, W["w_out"])], "nn", residual=x)
    sv["x1"] = x1
    h2 = _norm_fwd(f"norm_xa{l}", x1, W["g_xa"])
    qx = _mm(f"xa_q{l}", [(h2, W["wq"])], "nn")
    m, kv, kn_x, vx = _mem_fwd(f"mem_fwd{l}", mem, W["g_mem"], W["wkv"], W["gk"])
    o = _xa_fwd(f"xa_fwd{l}", qx, W["gq"], kn_x, vx)
    x2 = _mm(f"xa_o{l}", [(o, W["wo"])], "nn", residual=x1)
    sv.update(h2=h2, qx=qx, m=m, kv=kv, kn_x=kn_x, vx=vx, o=o, x2=x2)
    h3 = _norm_fwd(f"norm_ffn{l}", x2, W["g_ffn"])
    a, b, s = _ffn_up(f"ffn_up{l}", h3, W["w_gate"], W["w_up"])
    x3 = _mm(f"ffn_down{l}", [(s, W["w_down"])], "nn", residual=x2)
    sv.update(h3=h3, a=a, b=b, s=s)
    return x3, sv, W


def _layer_weights(l, P, part):
    i = l // 2
    if part == "in":
        W = {"g_mix": P["g_mix"][l]}
        if l % 2 == 0:
            w_in = _mx(P["e_w_in"][i])
            if w_in.shape[1] < EVEN_IN_PAD:
                w_in = jnp.pad(w_in, ((0, 0), (0, EVEN_IN_PAD - w_in.shape[1])))
            W.update(w_in=w_in, g_v=P["e_g_v"][i].reshape(1, A_WIDTH), w_s=P["e_w_s"][i],
                     bs_cols=jnp.pad(P["e_b_s"][i].T, ((0, 0), (0, LANES - A_GROUPS))),
                     g_qn2=jnp.tile(P["e_g_qn"][i], 2).reshape(1, LANES), g_kn2=jnp.tile(P["e_g_kn"][i], 2).reshape(1, LANES),
                     b_f_pad=_pad_lanes(P["e_b_f"][i]))
        return W
    W = {k: P[k][l] for k in ("g_xa", "g_mem", "g_ffn")}
    W.update(wq=_mx(P["xa_wq"][l]), wkv=_mx(P["xa_wkv"][l]), wo=_mx(P["xa_wo"][l]), gq=P["xa_gq"][l], gk=P["xa_gk"][l],
             w_gate=_mx(P["w_gate"][l]), w_up=_mx(P["w_up"][l]), w_down=_mx(P["w_down"][l]))
    if l % 2 == 0:
        W.update(w_out=_mx(P["e_w_out"][i]))
    else:
        W.update(w_in=_mx(P["o_w_in"][i]), w_pool=_mx(P["o_w_pool"][i]), s_pool=P["o_s_pool"][i].reshape(1, POOL_WIDTH),
                 conv_w8=jnp.pad(P["o_conv_w"][i], ((0, SUBLANES - CONV_K), (0, 0))), w_out=_mx(P["o_w_out"][i]))
    return W


def _ffn_bwd_act(name, g, wd, a, b):
    T, N = a.shape
    K = g.shape[1]
    tm, tn = _tile(T, 512, 16), _tile(N, 1536, LANES)

    def body(g_ref, wd_ref, a_ref, b_ref, da_ref, db_ref):
        ds = _dot_nt(g_ref[...], wd_ref[...])
        av = a_ref[...]
        sig = jax.nn.sigmoid(av)
        da_ref[...] = (ds * b_ref[...] * (sig * (1.0 + av * (1.0 - sig)))).astype(da_ref.dtype)
        db_ref[...] = (ds * (av * sig)).astype(db_ref.dtype)

    o_spec = pl.BlockSpec((tm, tn), lambda j, i: (i, j))
    return pl.pallas_call(
        body, name=name, grid=(N // tn, T // tm),
        in_specs=[pl.BlockSpec((tm, K), lambda j, i: (i, 0)), pl.BlockSpec((tn, K), lambda j, i: (j, 0)), o_spec, o_spec],
        out_specs=[o_spec, o_spec],
        out_shape=[jax.ShapeDtypeStruct((T, N), _MXU), jax.ShapeDtypeStruct((T, N), _MXU)],
        compiler_params=_cp(("parallel", "parallel")),
    )(g, wd, a, b)


def _xa_bwd(name, qx, do, gq, kn, vx):
    T, M = qx.shape[0], kn.shape[0]
    tm = _tile(T, 512, 16)

    def body(q_ref, do_ref, gq_ref, k_ref, v_ref, dq_ref, dk_ref, dv_ref, dg_ref):
        @pl.when(pl.program_id(0) == 0)
        def _():
            dk_ref[...] = jnp.zeros_like(dk_ref)
            dv_ref[...] = jnp.zeros_like(dv_ref)
            dg_ref[...] = jnp.zeros_like(dg_ref)

        gqv = gq_ref[...]
        for h in range(XA_HEADS):
            sl = slice(h * LANES, (h + 1) * LANES)
            qhat, r, qn, p = _xa_probs(q_ref[:, sl], gqv, k_ref[:, sl])
            do_h = do_ref[:, sl]
            dp = _dot_nt(do_h, v_ref[:, sl])
            dv_ref[:, sl] += _dot_tn(p, do_h)
            dsm = p * (dp - jnp.sum(p * dp, axis=-1, keepdims=True)) * (1.0 / math.sqrt(XA_HEAD_DIM))
            dqn = _dot(dsm, k_ref[:, sl])
            dk_ref[:, sl] += _dot_tn(dsm, qn)
            dg_ref[...] += jnp.sum(dqn * qhat, axis=0, keepdims=True)
            dxh = dqn * gqv
            dq_ref[:, sl] = (r * (dxh - qhat * jnp.mean(dxh * qhat, axis=-1, keepdims=True))).astype(dq_ref.dtype)

    full = lambda s: pl.BlockSpec(s, lambda i: tuple(0 for _ in s))
    blk = pl.BlockSpec((tm, XA_WIDTH), lambda i: (i, 0))
    return pl.pallas_call(
        body, name=name, grid=(T // tm,),
        in_specs=[blk, blk, full((1, LANES)), full((M, XA_WIDTH)), full((M, XA_WIDTH))],
        out_specs=[blk, full((M, XA_WIDTH)), full((M, XA_WIDTH)), full((1, LANES))],
        out_shape=[jax.ShapeDtypeStruct((T, XA_WIDTH), _MXU), jax.ShapeDtypeStruct((M, XA_WIDTH), F32),
                   jax.ShapeDtypeStruct((M, XA_WIDTH), F32), jax.ShapeDtypeStruct((1, LANES), F32)],
        compiler_params=_cp(("arbitrary",)),
    )(qx, do, gq.reshape(1, XA_HEAD_DIM), kn, vx)


def _mem_bwd(name, mem, g_mem, m, kv, gk, dkn, dvx, wkv):
    M = mem.shape[0]

    def body(mem_ref, g_ref, m_ref, kv_ref, gk_ref, dkn_ref, dvx_ref, w_ref, dw_ref, dgk_ref, dgm_ref):
        gkv = gk_ref[...]
        dgk = jnp.zeros((1, LANES), F32)
        parts = []
        for h in range(XA_HEADS):
            sl = slice(h * LANES, (h + 1) * LANES)
            k = kv_ref[:, sl]
            r = _rms_rows(k)
            khat = k * r
            dkn_h = dkn_ref[:, sl]
            dgk = dgk + jnp.sum(dkn_h * khat, axis=0, keepdims=True)
            dxh = dkn_h * gkv
            parts.append(r * (dxh - khat * jnp.mean(dxh * khat, axis=-1, keepdims=True)))
        dgk_ref[...] = dgk
        dkv = jnp.concatenate(parts + [dvx_ref[...]], axis=1)
        dw_ref[...] = _dot_tn(m_ref[...], dkv)
        dm = _dot_nt(dkv, w_ref[...])
        x = mem_ref[...]
        dgm_ref[...] = jnp.sum(dm * (x * _rms_rows(x)), axis=0, keepdims=True)

    return pl.pallas_call(
        body, name=name,
        out_shape=[jax.ShapeDtypeStruct((D_MODEL, 2 * XA_WIDTH), F32), jax.ShapeDtypeStruct((1, LANES), F32),
                   jax.ShapeDtypeStruct((1, D_MODEL), F32)],
        compiler_params=pltpu.CompilerParams(vmem_limit_bytes=VMEM_LIMIT),
    )(mem, g_mem.reshape(1, D_MODEL), m, kv, gk.reshape(1, XA_HEAD_DIM), dkn, dvx, wkv)


def _fox_bwd_common(q2, k2, v2, dy2, yf2, c_ref, ct_ref, lse_ref, e, h, lo, diag, tq):
    sel = lo if e == 0 else ~lo
    qh = jnp.where(sel, q2, jnp.zeros_like(q2))
    kh = jnp.where(sel, k2, jnp.zeros_like(k2))
    dyh = jnp.where(sel, dy2, 0.0)
    s = _fox_logits(qh, k2, c_ref[:, h:h + 1], ct_ref[h:h + 1, :], diag, tq)
    p = jnp.exp(s - lse_ref[:, h:h + 1])
    dp = _dot_nt(dyh, v2)
    delta = jnp.sum(dyh * yf2, axis=-1, keepdims=True)
    return p, p * (dp - delta), qh, kh, dyh


def _fox_bwd(name, qn, kn, vv, c, ct, lse, dycat, ybf, flags, side=None):
    T = qn.shape[0]
    tq = _tile(T, FOX_TILE, LANES)
    nblk = T // tq
    qs, ks = _tri_steps(nblk, False)
    nsteps = int(qs.shape[0])
    scale = 1.0 / math.sqrt(FOX_HEAD_DIM)
    s_specs, s_args, s_shapes, s_scratch = _side_specs(side)

    def body(qs_ref, ks_ref, *refs):
        main_in, side_in, main_out, side_out, main_scr, sems = _split_side(refs, 9, 5, 3, side)
        q_ref, k_ref, v_ref, c_ref, ct_ref, lse_ref, dy_ref, yf_ref, flag_ref = main_in
        dq_hbm, drow_hbm, dk_ref, dv_ref, dct_ref = main_out
        dq_acc, drow_acc, out_sem = main_scr
        s_id = pl.program_id(0)
        qi, ki = qs_ref[s_id], ks_ref[s_id]
        _side_run(side, s_id, nsteps, side_in, side_out, sems)
        rows = pl.ds(pl.multiple_of(qi * tq, tq), tq)

        @pl.when(qi == ki)
        def _():
            dk_ref[...] = jnp.zeros_like(dk_ref)
            dv_ref[...] = jnp.zeros_like(dv_ref)
            dct_ref[...] = jnp.zeros_like(dct_ref)

        @pl.when(ki == 0)
        def _():
            dq_acc[rows, :] = jnp.zeros((tq, FOX_WIDTH), F32)
            drow_acc[rows, :] = jnp.zeros((tq, LANES), F32)

        lo = _lane((tq, LANES)) < FOX_HEAD_DIM

        def step(diag):
            for j in range(FOX_WIDTH // LANES):
                sl = slice(j * LANES, (j + 1) * LANES)
                for e in range(2):
                    h = 2 * j + e

                    def head(h=h, e=e, sl=sl):
                        q2, k2, v2, dy2, yf2 = q_ref[:, sl], k_ref[:, sl], v_ref[:, sl], dy_ref[:, sl], yf_ref[:, sl]
                        p, ds, qh, kh, dyh = _fox_bwd_common(q2, k2, v2, dy2, yf2, c_ref, ct_ref, lse_ref, e, h, lo, diag, tq)
                        drow_acc[rows, :] += jnp.where(_lane((tq, LANES)) == h, jnp.sum(ds, axis=-1, keepdims=True), 0.0)
                        dct_ref[h:h + 1, :] -= jnp.sum(ds, axis=0, keepdims=True)
                        dq_acc[rows, sl] += _dot(ds, kh) * scale
                        dk_ref[:, sl] += _dot_tn(ds, qh)
                        dv_ref[:, sl] += _dot_tn(p, dyh)

                    if diag:
                        head()
                    else:
                        pl.when(flag_ref[(qi * nblk + ki) * FOX_HEADS + h] > 0.5)(head)

        _on_diag_or_not(qi, ki, step)

        @pl.when(s_id == nsteps - 1)
        def _():
            for src, dst in ((dq_acc, dq_hbm), (drow_acc, drow_hbm)):
                cp = pltpu.make_async_copy(src, dst, out_sem)
                cp.start()
                cp.wait()

    qmap = lambda s, a, b: (a[s], 0)
    kmap = lambda s, a, b: (b[s], 0)
    ctmap = lambda s, a, b: (0, b[s])
    any_spec = pl.BlockSpec(memory_space=pl.ANY)
    grid_spec = pltpu.PrefetchScalarGridSpec(
        num_scalar_prefetch=2, grid=(nsteps,),
        in_specs=[pl.BlockSpec((tq, 512), qmap), pl.BlockSpec((tq, 512), kmap), pl.BlockSpec((tq, 512), kmap),
                  pl.BlockSpec((tq, LANES), qmap), pl.BlockSpec((SUBLANES, tq), ctmap),
                  pl.BlockSpec((tq, LANES), qmap), pl.BlockSpec((tq, 512), lambda s, a, b: (a[s], 1)), pl.BlockSpec((tq, 512), qmap),
                  pl.BlockSpec(memory_space=pltpu.SMEM)] + s_specs,
        out_specs=[any_spec, any_spec, pl.BlockSpec((tq, 512), kmap), pl.BlockSpec((tq, 512), kmap), pl.BlockSpec((SUBLANES, tq), ctmap)] + s_specs,
        scratch_shapes=[pltpu.VMEM((T, FOX_WIDTH), F32), pltpu.VMEM((T, LANES), F32), pltpu.SemaphoreType.DMA] + s_scratch,
    )
    res = pl.pallas_call(
        body, name=name, grid_spec=grid_spec,
        out_shape=[jax.ShapeDtypeStruct((T, 512), F32), jax.ShapeDtypeStruct((T, LANES), F32), jax.ShapeDtypeStruct((T, 512), F32),
                   jax.ShapeDtypeStruct((T, 512), F32), jax.ShapeDtypeStruct((SUBLANES, T), F32)] + s_shapes,
        compiler_params=_cp(("arbitrary",)),
    )(qs, ks, qn, kn, vv, c, ct, lse, dycat, ybf, flags, *s_args)
    return res[:5], res[5:]


def _fold64(row):
    return row + pltpu.roll(row, FOX_HEAD_DIM, axis=1)


def _even_bwd(name, z, dycat, dqn, dkn, dvv, dct, drow, g_v, w_s, bs_cols, g_qn2, g_kn2, b_f_pad):
    T = z.shape[0]
    tm = _tile(T, 256, A_BLOCK)
    nb, nt = tm // A_BLOCK, T // tm

    def body(z_ref, dya_ref, dq_ref, dk_ref, dv_ref, dct_ref, drow_ref, gv_ref, ws_ref, bs_ref, gq_ref, gk_ref, bf_ref,
             dz_ref, dws_ref, dbs_ref, dgv_ref, dgq_ref, dgk_ref, dbf_ref, carry):
        @pl.when(pl.program_id(0) == 0)
        def _():
            for r in (dws_ref, dbs_ref, dgv_ref, dgq_ref, dgk_ref, dbf_ref, carry):
                r[...] = jnp.zeros_like(r)

        keep = (_row((A_BLOCK, A_BLOCK)) // CHUNK) >= (_lane((A_BLOCK, A_BLOCK)) // CHUNK)
        for g in range(A_GROUPS):
            sl = slice(g * A_GROUP_DIM, (g + 1) * A_GROUP_DIM)
            wm = _mx(_masked_ws(ws_ref, g))
            zu = z_ref[:, sl]
            zv = z_ref[:, A_WIDTH + g * A_GROUP_DIM:A_WIDTH + (g + 1) * A_GROUP_DIM]
            u, v = _gelu(zu), _gelu(zv)
            r = _rms_rows(v)
            vhat = v * r
            gvv = gv_ref[:, sl]
            vn = _mx(vhat * gvv)
            dya = dya_ref[:, sl]
            du_parts, dvn_parts = [], []
            for n in range(nb):
                rows = slice(n * A_BLOCK, (n + 1) * A_BLOCK)
                s = jnp.dot(wm, vn[rows], preferred_element_type=F32) + bs_ref[:, g:g + 1]
                du_parts.append(dya[rows] * s)
                d_s = dya[rows] * u[rows]
                dbs_ref[...] += jnp.where(_lane((A_BLOCK, LANES)) == g, jnp.sum(d_s, axis=-1, keepdims=True), 0.0)
                dws_ref[g] += jnp.where(keep, _dot_nt(d_s, vn[rows]), 0.0)
                dvn_parts.append(_dot_tn(wm, d_s))
            dvn = jnp.concatenate(dvn_parts, axis=0)
            dgv_ref[:, sl] += jnp.sum(dvn * vhat, axis=0, keepdims=True)
            dxh = dvn * gvv
            dv = r * (dxh - vhat * jnp.mean(dxh * vhat, axis=-1, keepdims=True))
            dz_ref[:, sl] = (jnp.concatenate(du_parts, axis=0) * _gelu_grad(zu)).astype(dz_ref.dtype)
            dz_ref[:, A_WIDTH + g * A_GROUP_DIM:A_WIDTH + (g + 1) * A_GROUP_DIM] = (dv * _gelu_grad(zv)).astype(dz_ref.dtype)
        o = 2 * A_WIDTH
        for j in range(FOX_WIDTH // LANES):
            sl = slice(j * LANES, (j + 1) * LANES)
            for (off, d_ref, g_ref, dg_ref) in ((o, dq_ref, gq_ref, dgq_ref), (o + FOX_WIDTH, dk_ref, gk_ref, dgk_ref)):
                zq = z_ref[:, off + j * LANES:off + (j + 1) * LANES]
                r = _head64_rms(zq)
                qhat = zq * r
                dn = d_ref[:, sl]
                dg_ref[...] += _fold64(jnp.sum(dn * qhat, axis=0, keepdims=True))
                dxh = dn * g_ref[...]
                dz_ref[:, off + j * LANES:off + (j + 1) * LANES] = (r * (dxh - qhat * _head64_mean(dxh * qhat))).astype(dz_ref.dtype)
            dz_ref[:, o + 2 * FOX_WIDTH + j * LANES:o + 2 * FOX_WIDTH + (j + 1) * LANES] = dv_ref[:, sl].astype(dz_ref.dtype)
        dct_v = dct_ref[...] + drow_ref[...].T[0:SUBLANES, :]
        upper = (_row((tm, tm)) >= _lane((tm, tm))).astype(F32)
        dlf_t = _dot_f32(dct_v, upper) + carry[:, 0:1]
        carry[...] += jnp.sum(dct_v, axis=1, keepdims=True)
        dlf = jnp.concatenate([dlf_t, jnp.zeros((LANES - SUBLANES, tm), F32)], axis=0).T
        zf = z_ref[:, o + 3 * FOX_WIDTH:o + 3 * FOX_WIDTH + LANES]
        dzf = dlf * jax.nn.sigmoid(-(zf + bf_ref[...]))
        dbf_ref[...] += jnp.sum(dzf, axis=0, keepdims=True)
        dz_ref[:, o + 3 * FOX_WIDTH:o + 3 * FOX_WIDTH + LANES] = dzf.astype(dz_ref.dtype)

    rev = lambda i: nt - 1 - i
    wide = lambda n, col=0: pl.BlockSpec((tm, n), lambda i: (rev(i), col))
    full = lambda s: pl.BlockSpec(s, lambda i: tuple(0 for _ in s))
    vec = full((1, LANES))
    return pl.pallas_call(
        body, name=name, grid=(nt,),
        in_specs=[wide(EVEN_IN_PAD), wide(512), wide(512), wide(512), wide(512), pl.BlockSpec((SUBLANES, tm), lambda i: (0, rev(i))),
                  wide(LANES), full((1, A_WIDTH)), full((A_GROUPS, A_BLOCK, A_BLOCK)), full((A_BLOCK, LANES)), vec, vec, vec],
        out_specs=[wide(EVEN_IN_PAD), full((A_GROUPS, A_BLOCK, A_BLOCK)), full((A_BLOCK, LANES)), full((1, A_WIDTH)), vec, vec, vec],
        out_shape=[jax.ShapeDtypeStruct((T, EVEN_IN_PAD), _MXU), jax.ShapeDtypeStruct((A_GROUPS, A_BLOCK, A_BLOCK), F32),
                   jax.ShapeDtypeStruct((A_BLOCK, LANES), F32), jax.ShapeDtypeStruct((1, A_WIDTH), F32),
                   jax.ShapeDtypeStruct((1, LANES), F32), jax.ShapeDtypeStruct((1, LANES), F32), jax.ShapeDtypeStruct((1, LANES), F32)],
        scratch_shapes=[pltpu.VMEM((SUBLANES, LANES), F32)],
        compiler_params=_cp(("arbitrary",)),
    )(z, dycat, dqn, dkn, dvv, dct, drow, g_v, w_s, bs_cols, g_qn2, g_kn2, b_f_pad)


def _shift_up(ext, k, tm):
    return pltpu.roll(ext, ext.shape[0] - k, axis=0)[0:tm]


def _odd_bwd(name, z, dycat, w_pool, s_pool, conv_w8):
    T = z.shape[0]
    tm = _tile(T, 256, HALO)
    r, nt = tm // HALO, T // tm
    n_ext = tm + HALO

    def body(z_ref, zb_ref, zn_ref, dy_ref, dyn_ref, wp_ref, sp_ref, cw_ref, dz_ref, dwp_ref, dsp_ref, dcw_ref):
        i = pl.program_id(0)

        @pl.when(i == 0)
        def _():
            for rr in (dwp_ref, dsp_ref, dcw_ref):
                rr[...] = jnp.zeros_like(rr)

        first, last = i == 0, i == nt - 1
        pos = i * tm + _row((tm, LANES))
        pos_ext = i * tm + _row((n_ext, LANES))
        for g, w in enumerate(POOL_WINDOWS):
            sl = slice(g * LANES, (g + 1) * LANES)
            z_ext = jnp.concatenate([jnp.where(first, 0.0, zb_ref[:, sl]), z_ref[:, sl]], axis=0)
            p = _pool_p(z_ext, g, tm, pos)
            spv = sp_ref[:, sl]
            dyc = dy_ref[:, sl]
            dsp_ref[:, sl] += jnp.sum(dyc * _dot(p, wp_ref[g]), axis=0, keepdims=True)
            dpw_ext = jnp.concatenate([dyc, jnp.where(last, 0.0, dyn_ref[:, sl])], axis=0) * spv
            dwp_ref[g] += _dot_tn(p, dpw_ext[0:tm])
            dp_ext = _dot_nt(dpw_ext, wp_ref[g])
            f = dp_ext / jnp.minimum(pos_ext + 1, w).astype(F32)
            span = 1
            while span < w:
                f = f + pltpu.roll(f, n_ext - span, axis=0)
                span *= 2
            dz_ref[:, sl] = (f[0:tm] - dp_ext[0:tm]).astype(dz_ref.dtype)
        for j in range(CONV_WIDTH // LANES):
            sl = slice(j * LANES, (j + 1) * LANES)
            hd = slice(POOL_WIDTH + j * LANES, POOL_WIDTH + (j + 1) * LANES)
            gb = slice(POOL_WIDTH + CONV_WIDTH + j * LANES, POOL_WIDTH + CONV_WIDTH + (j + 1) * LANES)
            gc = slice(POOL_WIDTH + 2 * CONV_WIDTH + j * LANES, POOL_WIDTH + 2 * CONV_WIDTH + (j + 1) * LANES)
            ysl = slice(POOL_WIDTH + j * LANES, POOL_WIDTH + (j + 1) * LANES)
            hdv, gbv, gcv = z_ref[:, hd], z_ref[:, gb], z_ref[:, gc]
            xg = gcv * hdv
            xg_ext = jnp.concatenate([jnp.where(first, 0.0, zb_ref[:, gc] * zb_ref[:, hd]), xg], axis=0)
            xg1, xg2 = _shift_down(xg_ext, 1, tm), _shift_down(xg_ext, 2, tm)
            w0, w1, w2 = cw_ref[0:1, sl], cw_ref[1:2, sl], cw_ref[2:3, sl]
            conv = w0 * xg2 + w1 * xg1 + w2 * xg
            dyd = dy_ref[:, ysl]
            dconv = dyd * gbv
            dconv_ext = jnp.concatenate([dconv, jnp.where(last, 0.0, dyn_ref[:, ysl] * zn_ref[:, gb])], axis=0)
            dcw_ref[0:1, sl] += jnp.sum(dconv * xg2, axis=0, keepdims=True)
            dcw_ref[1:2, sl] += jnp.sum(dconv * xg1, axis=0, keepdims=True)
            dcw_ref[2:3, sl] += jnp.sum(dconv * xg, axis=0, keepdims=True)
            dxg = w2 * dconv + w1 * _shift_up(dconv_ext, 1, tm) + w0 * _shift_up(dconv_ext, 2, tm)
            dz_ref[:, hd] = (dxg * gcv).astype(dz_ref.dtype)
            dz_ref[:, gb] = (dyd * conv).astype(dz_ref.dtype)
            dz_ref[:, gc] = (dxg * hdv).astype(dz_ref.dtype)

    full = lambda s: pl.BlockSpec(s, lambda i: tuple(0 for _ in s))
    back = lambda n: pl.BlockSpec((HALO, n), lambda i: (jnp.maximum(i * r - 1, 0), 0))
    nxt = lambda n: pl.BlockSpec((HALO, n), lambda i: (jnp.minimum((i + 1) * r, T // HALO - 1), 0))
    return pl.pallas_call(
        body, name=name, grid=(nt,),
        in_specs=[pl.BlockSpec((tm, ODD_IN), lambda i: (i, 0)), back(ODD_IN), nxt(ODD_IN),
                  pl.BlockSpec((tm, 1024), lambda i: (i, 0)), nxt(1024),
                  full((4, LANES, LANES)), full((1, POOL_WIDTH)), full((SUBLANES, CONV_WIDTH))],
        out_specs=[pl.BlockSpec((tm, ODD_IN), lambda i: (i, 0)), full((4, LANES, LANES)), full((1, POOL_WIDTH)), full((SUBLANES, CONV_WIDTH))],
        out_shape=[jax.ShapeDtypeStruct((T, ODD_IN), _MXU), jax.ShapeDtypeStruct((4, LANES, LANES), F32),
                   jax.ShapeDtypeStruct((1, POOL_WIDTH), F32), jax.ShapeDtypeStruct((SUBLANES, CONV_WIDTH), F32)],
        compiler_params=_cp(("arbitrary",)),
    )(z, z, z, dycat, dycat, w_pool, s_pool, conv_w8)


def _layer_bwd(l, g3, mem, W, sv, side_fn=None):
    i = l // 2
    G = {}
    side_out = ()
    da, db = _ffn_bwd_act(f"ffn_bwd_act{l}", g3, W["w_down"], sv["a"], sv["b"])
    G["w_down"] = _mm(f"d_w_down{l}", [(sv["s"], g3)], "tn")
    g2, dg = _mm(f"d_h3{l}", [(da, W["w_gate"]), (db, W["w_up"])], "nt", norm_bwd=(sv["x2"], W["g_ffn"], g3))
    G["w_gate"] = _mm(f"d_w_gate{l}", [(sv["h3"], da)], "tn")
    G["w_up"] = _mm(f"d_w_up{l}", [(sv["h3"], db)], "tn")
    G["g_ffn"] = dg[0]

    do = _mm(f"d_o{l}", [(g2, W["wo"])], "nt", out_dtype=_MXU)
    G["xa_wo"] = _mm(f"d_wo{l}", [(sv["o"], g2)], "tn")
    dqx, dkn, dvx, dgq = _xa_bwd(f"xa_bwd{l}", sv["qx"], do, W["gq"], sv["kn_x"], sv["vx"])
    G["xa_gq"] = dgq[0]
    g1, dg = _mm(f"d_h2{l}", [(dqx, W["wq"])], "nt", norm_bwd=(sv["x1"], W["g_xa"], g2))
    G["xa_wq"] = _mm(f"d_wq{l}", [(sv["h2"], dqx)], "tn")
    G["g_xa"] = dg[0]
    dwkv, dgk, dgm = _mem_bwd(f"mem_bwd{l}", mem, W["g_mem"], sv["m"], sv["kv"], W["gk"], dkn, dvx, W["wkv"])
    G["xa_wkv"], G["xa_gk"], G["g_mem"] = dwkv, dgk[0], dgm[0]

    dycat = _mm(f"d_ycat{l}", [(g1, W["w_out"])], "nt")
    if l % 2 == 0:
        G["e_w_out"] = jnp.concatenate([_mm(f"d_e_wout_a{l}", [(sv["ya"], g1)], "tn"), _mm(f"d_e_wout_b{l}", [(sv["yb"], g1)], "tn")], axis=0)
        fa = (sv["qn"], sv["kn"], sv["vv"], sv["c"], sv["ct"], sv["lse"], dycat, sv["ybf"], sv["flags"])
        (dqn, drow, dkn_f, dvv, dct), side_out = _fox_bwd(f"fox_bwd{l}", *fa, side=side_fn(G) if side_fn is not None else None)
        dz, dws, dbs, dgv, dgq2, dgk2, dbf = _even_bwd(f"even_bwd{l}", sv["z"], dycat, dqn, dkn_f, dvv, dct, drow, W["g_v"], W["w_s"],
                                                      W["bs_cols"], W["g_qn2"], W["g_kn2"], W["b_f_pad"])
        G.update(e_w_s=dws, e_b_s=dbs[:, :A_GROUPS].T, e_g_v=dgv[0], e_g_qn=dgq2[0, :FOX_HEAD_DIM], e_g_kn=dgk2[0, :FOX_HEAD_DIM],
                 e_b_f=dbf[0, :FOX_HEADS])
        G["e_w_in"] = _mm(f"d_e_win{l}", [(sv["h1"], dz)], "tn")
    else:
        G["o_w_out"] = _mm(f"d_o_wout{l}", [(sv["ycat"], g1)], "tn")
        dz, dwp, dsp, dcw = _odd_bwd(f"odd_bwd{l}", sv["z"], dycat, W["w_pool"], W["s_pool"], W["conv_w8"])
        G.update(o_w_pool=dwp, o_s_pool=dsp[0], o_conv_w=dcw[:CONV_K])
        G["o_w_in"] = _mm(f"d_o_win{l}", [(sv["h1"], dz)], "tn")
    g0, dg = _mm(f"d_h1{l}", [(dz, W["w_in"])], "nt", norm_bwd=(sv["x0"], W["g_mix"], g1))
    G["g_mix"] = dg[0]
    return g0, G, side_out


PACK_W = 1024
MESH_T = pl.DeviceIdType.MESH
_ANY = pl.BlockSpec(memory_space=pl.ANY)


def _my_place():
    x, y, c = lax.axis_index("x"), lax.axis_index("y"), lax.axis_index("c")
    return x, y, c


def _flip(v, bit):
    return 1 - v if bit else v


N_PEERS = N_DEV - 1


def _all_gather(name, blks):
    side = _TwoLevelGather(blks)

    def body(*refs):
        _, side_in, _, side_out, _, sems = _split_side(refs, 0, 0, 0, side)
        side.start(side_in, side_out, sems)
        side.middle(side_in, side_out, sems)
        side.finish(side_in, side_out, sems)

    return pl.pallas_call(
        body, name=name, out_shape=side.out_shape,
        in_specs=[_ANY] * side.n, out_specs=[_ANY] * side.n, scratch_shapes=side.scratch,
    )(*blks)


class _Direct:
    def __init__(self, kind, arrays):
        self.kind, self.arrays, self.n = kind, list(arrays), len(arrays)
        if kind == "scatter":
            self.out_shape = [jax.ShapeDtypeStruct(a.shape, a.dtype) for a in arrays]
        else:
            self.out_shape = [jax.ShapeDtypeStruct((N_DEV,) + a.shape, a.dtype) for a in arrays]
        self.scratch = [pltpu.SemaphoreType.DMA((N_PEERS * self.n,)), pltpu.SemaphoreType.DMA((N_PEERS * self.n,)),
                        pltpu.SemaphoreType.DMA((self.n,))]

    def _copies(self, in_refs, out_refs, sems):
        send_sems, recv_sems, local_sems = sems
        x, y, c = _my_place()
        me = 4 * x + 2 * y + c
        mine, sends, recvs = [], [], []
        for t in range(self.n):
            src_of = (lambda idx, t=t: in_refs[t].at[idx]) if self.kind == "scatter" else (lambda idx, t=t: in_refs[t])
            mine.append(pltpu.make_async_copy(src_of(me), out_refs[t].at[me], local_sems.at[t]))
            for m in range(1, N_DEV):
                px, py, pc = _flip(x, m & 4), _flip(y, m & 2), _flip(c, m & 1)
                pidx = 4 * px + 2 * py + pc
                sem = dict(send_sem=send_sems.at[N_PEERS * t + m - 1], recv_sem=recv_sems.at[N_PEERS * t + m - 1], device_id_type=MESH_T)
                sends.append(pltpu.make_async_remote_copy(src_ref=src_of(pidx), dst_ref=out_refs[t].at[me], device_id=(px, py, pc), **sem))
                recvs.append(pltpu.make_async_remote_copy(src_ref=src_of(pidx), dst_ref=out_refs[t].at[pidx], device_id=(x, y, c), **sem))
        return mine, sends, recvs

    def start(self, in_refs, out_refs, sems):
        mine, sends, _ = self._copies(in_refs, out_refs, sems)
        for cp in mine + sends:
            cp.start()

    def middle(self, in_refs, out_refs, sems):
        pass

    def finish(self, in_refs, out_refs, sems):
        mine, sends, recvs = self._copies(in_refs, out_refs, sems)
        for cp in recvs:
            cp.wait_recv()
        for cp in sends:
            cp.wait_send()
        for cp in mine:
            cp.wait()


class _TwoLevelGather:
    def __init__(self, arrays):
        self.arrays, self.n = list(arrays), len(arrays)
        self.out_shape = [jax.ShapeDtypeStruct((N_DEV,) + a.shape, a.dtype) for a in arrays]
        self.scratch = [pltpu.SemaphoreType.DMA((N_PEERS * self.n,)), pltpu.SemaphoreType.DMA((N_PEERS * self.n,)),
                        pltpu.SemaphoreType.DMA((self.n,))]

    def _place(self):
        x, y, c = _my_place()
        return (x, y, c), (x, y, 1 - c), [(1 - x, y), (x, 1 - y), (1 - x, 1 - y)]

    def _copy(self, in_refs, out_refs, sems, t, k, block, to, own=False):
        slab = out_refs[t].at[4 * block[0] + 2 * block[1] + block[2]]
        return pltpu.make_async_remote_copy(
            src_ref=in_refs[t] if own else slab, dst_ref=slab, send_sem=sems[0].at[N_PEERS * t + k],
            recv_sem=sems[1].at[N_PEERS * t + k], device_id=to, device_id_type=MESH_T)

    def _mine(self, in_refs, out_refs, sems):
        x, y, c = _my_place()
        return [pltpu.make_async_copy(in_refs[t], out_refs[t].at[4 * x + 2 * y + c], sems[2].at[t]) for t in range(self.n)]

    def _first(self, in_refs, out_refs, sems):
        me, sibling, chips = self._place()
        cps = []
        for t in range(self.n):
            cps.append(self._copy(in_refs, out_refs, sems, t, 0, me, sibling, own=True))
            cps += [self._copy(in_refs, out_refs, sems, t, 1 + j, me, (*chip, me[2]), own=True) for j, chip in enumerate(chips)]
        return cps

    def _passed(self, in_refs, out_refs, sems):
        me, sibling, chips = self._place()
        return [self._copy(in_refs, out_refs, sems, t, 4 + j, (*chip, me[2]), sibling) for j, chip in enumerate(chips) for t in range(self.n)]

    def start(self, in_refs, out_refs, sems):
        for cp in self._mine(in_refs, out_refs, sems) + self._first(in_refs, out_refs, sems):
            cp.start()

    def middle(self, in_refs, out_refs, sems):
        me, _, chips = self._place()
        for j, chip in enumerate(chips):
            for t in range(self.n):
                self._copy(in_refs, out_refs, sems, t, 1 + j, (*chip, me[2]), me).wait_recv()
        for cp in self._passed(in_refs, out_refs, sems):
            cp.start()

    def finish(self, in_refs, out_refs, sems):
        me, sibling, chips = self._place()
        for t in range(self.n):
            self._copy(in_refs, out_refs, sems, t, 0, sibling, me).wait_recv()
            for j, chip in enumerate(chips):
                self._copy(in_refs, out_refs, sems, t, 4 + j, (*chip, 1 - me[2]), me).wait_recv()
        for cp in self._first(in_refs, out_refs, sems) + self._passed(in_refs, out_refs, sems):
            cp.wait_send()
        for cp in self._mine(in_refs, out_refs, sems):
            cp.wait()


def _split_side(refs, n_main_in, n_main_out, n_main_scratch, side):
    ns = side.n if side is not None else 0
    i = 0
    main_in = refs[i:i + n_main_in]; i += n_main_in
    side_in = refs[i:i + ns]; i += ns
    main_out = refs[i:i + n_main_out]; i += n_main_out
    side_out = refs[i:i + ns]; i += ns
    main_scr = refs[i:i + n_main_scratch]; i += n_main_scratch
    return main_in, side_in, main_out, side_out, main_scr, refs[i:]


def _all_to_all(name, gs):
    side = _Direct("scatter", gs)

    def body(*refs):
        _, side_in, _, side_out, _, sems = _split_side(refs, 0, 0, 0, side)
        side.start(side_in, side_out, sems)
        side.finish(side_in, side_out, sems)

    return pl.pallas_call(
        body, name=name, out_shape=side.out_shape,
        in_specs=[_ANY] * side.n, out_specs=[_ANY] * side.n, scratch_shapes=side.scratch,
    )(*gs)


def _unblock(name, blocks, layer, n_out):
    _, _, K, nb = blocks.shape
    tk = _tile(K, 256, 16)

    def body(b_ref, o_ref):
        for j in range(N_DEV):
            o_ref[:, j * nb:(j + 1) * nb] = b_ref[j]
        if n_out > N_DEV * nb:
            o_ref[:, N_DEV * nb:n_out] = jnp.zeros((tk, n_out - N_DEV * nb), o_ref.dtype)

    return pl.pallas_call(
        body, name=name, grid=(K // tk,),
        in_specs=[pl.BlockSpec((N_DEV, None, tk, nb), lambda i: (0, layer, i, 0))],
        out_specs=pl.BlockSpec((tk, n_out), lambda i: (i, 0)),
        out_shape=jax.ShapeDtypeStruct((K, n_out), blocks.dtype),
        compiler_params=_cp(("parallel",)),
    )(blocks)


def _block(name, full, nb, dtype):
    K, N = full.shape
    tk = _tile(K, 256, 16)

    def body(f_ref, o_ref):
        for j in range(N_DEV):
            o_ref[j] = f_ref[:, j * nb:(j + 1) * nb].astype(o_ref.dtype)

    return pl.pallas_call(
        body, name=name, grid=(K // tk,),
        in_specs=[pl.BlockSpec((tk, N), lambda i: (i, 0))],
        out_specs=pl.BlockSpec((N_DEV, tk, nb), lambda i: (0, i, 0)),
        out_shape=jax.ShapeDtypeStruct((N_DEV, K, nb), dtype),
        compiler_params=_cp(("parallel",)),
    )(full)


ADAM_BLOCK_ELEMS = 256 * 1024
GRAD_WIRE = jnp.bfloat16


def _adamw(name, parts, w, m, v, layer=0):
    shape = w.shape[1:]
    cols = shape[-1]
    rows = math.prod(shape[:-1])
    nl = w.shape[0]
    parts, w, m, v = parts.reshape(N_DEV, rows, cols), w.reshape(nl, rows, cols), m.reshape(nl, rows, cols), v.reshape(nl, rows, cols)
    tr = _tile(rows, max(16, ADAM_BLOCK_ELEMS // cols), 16)

    def body(p_ref, w_ref, m_ref, v_ref, g_ref, d_ref, mo_ref, vo_ref):
        g = p_ref[0].astype(F32)
        for s in range(1, N_DEV):
            g = g + p_ref[s].astype(F32)
        mn = ADAM_B1 * m_ref[...] + (1.0 - ADAM_B1) * g
        vn = ADAM_B2 * v_ref[...] + (1.0 - ADAM_B2) * jnp.square(g)
        m_hat = mn / (1.0 - ADAM_B1 ** ADAM_STEP)
        v_hat = vn / (1.0 - ADAM_B2 ** ADAM_STEP)
        g_ref[...] = g
        d_ref[...] = -ADAM_LR * (m_hat / (jnp.sqrt(v_hat) + ADAM_EPS) + ADAM_WD * w_ref[...])
        mo_ref[...] = mn
        vo_ref[...] = vn

    blk = pl.BlockSpec((tr, cols), lambda i: (i, 0))
    lblk = pl.BlockSpec((None, tr, cols), lambda i: (layer, i, 0))
    shp = jax.ShapeDtypeStruct((rows, cols), F32)
    res = pl.pallas_call(
        body, name=name, grid=(rows // tr,),
        in_specs=[pl.BlockSpec((N_DEV, tr, cols), lambda i: (0, i, 0)), lblk, lblk, lblk],
        out_specs=[blk, blk, blk, blk], out_shape=[shp, shp, shp, shp],
        compiler_params=_cp(("parallel",)),
    )(parts, w, m, v)
    return [r.reshape(shape) for r in res]


SHARDED = dict(xa_wq=1, xa_wkv=1, xa_wo=2, w_gate=2, w_up=2, w_down=1, e_w_in=2, e_w_out=1, o_w_in=2, o_s_pool=1, o_conv_w=2, o_w_out=1)
LAYER_SPLIT = ("xa_wq", "xa_wkv", "xa_wo", "w_gate", "w_up", "w_down")
ELEMENTWISE_SHARDED = ("o_s_pool", "o_conv_w")
REPLICATED = ("g_mix", "g_xa", "g_mem", "xa_gq", "xa_gk", "g_ffn", "e_b_f", "e_g_v", "e_w_s", "e_b_s", "e_g_qn", "e_g_kn", "o_w_pool")
WEIGHTS = ("g_mix", "g_xa", "g_mem", "xa_wq", "xa_wkv", "xa_wo", "xa_gq", "xa_gk", "g_ffn", "w_gate", "w_up", "w_down", "e_w_in", "e_b_f",
           "e_g_v", "e_w_s", "e_b_s", "e_g_qn", "e_g_kn", "e_w_out", "o_w_in", "o_w_pool", "o_s_pool", "o_conv_w", "o_w_out")


def _rows_for(n, mult):
    return -(-n // (PACK_W * mult)) * mult


def _pack(arrs, rows, dtype):
    flat = jnp.concatenate([a.reshape(-1).astype(dtype) for a in arrs])
    return jnp.pad(flat, (0, rows * PACK_W - flat.shape[0])).reshape(rows, PACK_W)


def _unpack(slab, shapes):
    flat, out, off = slab.reshape(-1), [], 0
    for s in shapes:
        n = math.prod(s)
        out.append(flat[off:off + n].reshape(s))
        off += n
    return out


def _to_blocks(full, axis):
    s = full.shape
    return jnp.moveaxis(full.reshape(s[:axis] + (N_DEV, s[axis] // N_DEV) + s[axis + 1:]), axis, 0)


def _from_blocks(blocks, axis):
    b = jnp.moveaxis(blocks, 0, axis)
    s = b.shape
    return b.reshape(s[:axis] + (s[axis] * s[axis + 1],) + s[axis + 2:])


def kernel(x, mem, g_mix, g_xa, g_mem, xa_wq, xa_wkv, xa_wo, xa_gq, xa_gk, g_ffn, w_gate, w_up, w_down, e_w_in, e_b_f, e_g_v, e_w_s, e_b_s, e_g_qn, e_g_kn, e_w_out, o_w_in, o_w_pool, o_s_pool, o_conv_w, o_w_out, loss_target, m_g_mix, m_g_xa, m_g_mem, m_xa_wq, m_xa_wkv, m_xa_wo, m_xa_gq, m_xa_gk, m_g_ffn, m_w_gate, m_w_up, m_w_down, m_e_w_in, m_e_b_f, m_e_g_v, m_e_w_s, m_e_b_s, m_e_g_qn, m_e_g_kn, m_e_w_out, m_o_w_in, m_o_w_pool, m_o_s_pool, m_o_conv_w, m_o_w_out, v_g_mix, v_g_xa, v_g_mem, v_xa_wq, v_xa_wkv, v_xa_wo, v_xa_gq, v_xa_gk, v_g_ffn, v_w_gate, v_w_up, v_w_down, v_e_w_in, v_e_b_f, v_e_g_v, v_e_w_s, v_e_b_s, v_e_g_qn, v_e_g_kn, v_e_w_out, v_o_w_in, v_o_w_pool, v_o_s_pool, v_o_conv_w, v_o_w_out):
    args = dict(locals())
    Wt = {n: args[n] for n in WEIGHTS}
    Mo = {n: args["m_" + n] for n in WEIGHTS}
    Vo = {n: args["v_" + n] for n in WEIGHTS}
    rep_names = REPLICATED
    rep_shapes = [Wt[n].shape for n in rep_names]
    rep_rows = _rows_for(sum(math.prod(s) for s in rep_shapes) + 1, 16)
    layer_names = [LAYER_SPLIT + ("e_w_in", "e_w_out"), LAYER_SPLIT + ("o_w_in", "o_w_out", "o_s_pool", "o_conv_w")]

    def shard_of(n, l, src=Wt):
        return src[n][l:l + 1] if n in LAYER_SPLIT else src[n]

    def to_wire(n, a):
        return a if n in ELEMENTWISE_SHARDED else _mx(a)

    def full_weight(n, l, gn):
        if SHARDED[n] == 1 or n in ELEMENTWISE_SHARDED:
            return _from_blocks(gn, SHARDED[n])[0]
        return _unblock(f"unblock_{n}{l}", gn, 0, EVEN_IN_PAD if n == "e_w_in" else N_DEV * gn.shape[-1])

    def grad_slabs(n, l, gl):
        if n in ELEMENTWISE_SHARDED:
            return _to_blocks(gl.reshape((1,) + Wt[n].shape[1:-1] + (-1,)), SHARDED[n])
        if SHARDED[n] == 1:
            return _to_blocks(gl[None], 1).astype(GRAD_WIRE)
        return _block(f"block_{n}{l}", gl, Wt[n].shape[-1], GRAD_WIRE)[:, None]

    xk, mem2 = x[0], mem[0]
    later = [(n, 0) for n in layer_names[0] if n != "e_w_in"] + [(n, 1) for n in layer_names[1]]
    w_in0 = full_weight("e_w_in", 0, _all_gather("gather_e_w_in", [to_wire("e_w_in", Wt["e_w_in"])])[0])
    side = _TwoLevelGather([to_wire(n, shard_of(n, l)) for n, l in later])
    P = {n: Wt[n] for n in rep_names}
    P["e_w_in"] = {0: w_in0}

    def on_side(gathered):
        for (n, l), gn in zip(later, gathered):
            P.setdefault(n, {})[l if n in LAYER_SPLIT else 0] = full_weight(n, l, gn)

    xk, sv0, W0 = _layer_fwd(0, xk, mem2, P, side=side, on_side=on_side)
    xk, sv1, W1 = _layer_fwd(1, xk, mem2, P)

    g, loss_row = _loss_grad("loss_grad", xk, loss_target[0])
    g, G1, _ = _layer_bwd(1, g, mem2, W1, sv1)
    slabs1 = [grad_slabs(n, 1, G1[n]) for n in layer_names[1]]
    g, G0, beside = _layer_bwd(0, g, mem2, W0, sv0,
                               side_fn=lambda G: _Direct("scatter", slabs1 + [grad_slabs(n, 0, G[n]) for n in LAYER_SPLIT]))
    parts1, parts0a = beside[:len(slabs1)], beside[len(slabs1):]
    grad_x = g[None]
    parts0b = _all_to_all("exchange_grads_mixer0", [grad_slabs(n, 0, G0[n]) for n in ("e_w_in", "e_w_out")])
    parts = {(n, 1): p for n, p in zip(layer_names[1], parts1)}
    parts.update({(n, 0): p for n, p in zip(LAYER_SPLIT, parts0a)})
    parts.update({(n, 0): p for n, p in zip(("e_w_in", "e_w_out"), parts0b)})
    G = {n: {0: G0[n]} for n in G0}
    for n in G1:
        G.setdefault(n, {})[1] = G1[n]
    rep_g = [jnp.stack([G[n][l] for l in sorted(G[n])]).reshape(Wt[n].shape) for n in rep_names]
    rep_parts = _all_gather("gather_small_grads", [_pack(rep_g + [jnp.sum(loss_row).reshape(1)], rep_rows, F32)])[0]

    outs = {}
    for n in SHARDED:
        per_layer = [_adamw(f"adamw_{n}{l}", parts[(n, l if n in LAYER_SPLIT else (0 if n.startswith("e_") else 1))], Wt[n], Mo[n], Vo[n], layer=l)
                     for l in range(Wt[n].shape[0])]
        for k, kind in enumerate(("grad", "delta", "new_m", "new_v")):
            outs[kind + "_" + n] = jnp.stack([r[k] for r in per_layer])
    res = _adamw("adamw_replicated", rep_parts, _pack([Wt[n] for n in rep_names], rep_rows, F32)[None],
                 _pack([Mo[n] for n in rep_names], rep_rows, F32)[None], _pack([Vo[n] for n in rep_names], rep_rows, F32)[None])
    for kind, slab in zip(("grad", "delta", "new_m", "new_v"), res):
        for n, a in zip(rep_names, _unpack(slab, rep_shapes)):
            outs[kind + "_" + n] = a
    loss = res[0].reshape(-1)[sum(math.prod(s) for s in rep_shapes)]
    return (loss, grad_x, *[outs[k + "_" + n] for k in ("grad", "delta", "new_m", "new_v") for n in WEIGHTS])
```

```python
import functools
import math

import numpy as np
import jax
import jax.numpy as jnp
from jax import lax
from jax.experimental import pallas as pl
from jax.experimental.pallas import tpu as pltpu

F32 = jnp.float32
BF16 = jnp.bfloat16
_MXU = jnp.bfloat16

D_MODEL = 1024
N_DEV = 8
LANES = 128
SUBLANES = 8
HALO = 16
CHUNK = 64
A_GROUPS, A_GROUP_DIM, A_WIDTH, A_BLOCK = 4, 128, 512, 128
FOX_HEADS, FOX_HEAD_DIM, FOX_WIDTH = 8, 64, 512
FOX_TILE = 512
POOL_WINDOWS = (2, 4, 8, 16)
POOL_WIDTH, CONV_WIDTH, CONV_K = 512, 512, 3
EVEN_IN, EVEN_IN_PAD, ODD_IN = 2568, 2688, 2048
XA_HEADS, XA_HEAD_DIM, XA_WIDTH = 4, 128, 512
D_FF = 2816
EPS = 1e-6
ADAM_LR, ADAM_B1, ADAM_B2, ADAM_EPS, ADAM_WD, ADAM_STEP = 0.001, 0.9, 0.999, 1e-08, 0.01, 10
VMEM_LIMIT = 48 * 1024 * 1024


def _cp(sem):
    return pltpu.CompilerParams(dimension_semantics=sem, vmem_limit_bytes=VMEM_LIMIT)


def _tile(n, cap, q):
    best = None
    for d in range(q, min(n, cap) + 1, q):
        if n % d == 0:
            best = d
    return best if best is not None else n


def _mx(a):
    return a.astype(_MXU)


def _dot(a, b):
    return jnp.dot(_mx(a), _mx(b), preferred_element_type=F32)


def _dot_nt(a, b):
    return lax.dot_general(_mx(a), _mx(b), (((1,), (1,)), ((), ())), preferred_element_type=F32)


def _dot_tn(a, b):
    return lax.dot_general(_mx(a), _mx(b), (((0,), (0,)), ((), ())), preferred_element_type=F32)


def _dot_f32(a, b):
    return jnp.dot(a, b, precision=lax.Precision.HIGHEST, preferred_element_type=F32)


def _lane(shape):
    return lax.broadcasted_iota(jnp.int32, shape, len(shape) - 1)


def _row(shape):
    return lax.broadcasted_iota(jnp.int32, shape, len(shape) - 2)


def _mm(name, pairs, mode, out_dtype=F32, residual=None, norm_bwd=None, tm_cap=512, tn_cap=1536, tk_cap=2048):
    a0, b0 = pairs[0]
    if mode == "nn":
        (M, K), N = a0.shape, b0.shape[1]
    elif mode == "nt":
        (M, K), N = a0.shape, b0.shape[0]
    else:
        (K, M), N = a0.shape, b0.shape[1]
    if mode == "tn":
        tm, tk = _tile(M, 1408, LANES), _tile(K, 512, 16)
    else:
        tm, tk = _tile(M, tm_cap, 16), _tile(K, tk_cap, LANES)
    tn = _tile(N, tn_cap, LANES)
    nk = K // tk
    npairs = len(pairs)
    dot = {"nn": _dot, "nt": _dot_nt, "tn": _dot_tn}[mode]

    if norm_bwd is not None:
        assert tn == N and residual is None and mode != "tn", "the norm backward needs whole rows"

    def body(*refs):
        ab = refs[:2 * npairs]
        res_ref = refs[2 * npairs] if residual is not None else None
        k = pl.program_id(2)
        first_row_tile = pl.program_id(1) == 0
        if norm_bwd is not None:
            x_ref, g_ref, r_ref = refs[2 * npairs:2 * npairs + 3]
            o_ref, dg_ref, acc = refs[-3], refs[-2], refs[-1]
        else:
            o_ref, acc = refs[-2], refs[-1]

        @pl.when(k == 0)
        def _():
            acc[...] = jnp.zeros_like(acc)

        for p in range(npairs):
            acc[...] += dot(ab[2 * p][...], ab[2 * p + 1][...])

        @pl.when(k == nk - 1)
        def _():
            out = acc[...]
            if res_ref is not None:
                out = out + res_ref[...]
            if norm_bwd is not None:
                xv = x_ref[...]
                r = _rms_rows(xv)
                xhat = xv * r
                dxhat = out * g_ref[...]

                @pl.when(first_row_tile)
                def _():
                    dg_ref[...] = jnp.zeros_like(dg_ref)

                dg_ref[...] += jnp.sum(out * xhat, axis=0, keepdims=True)
                out = r_ref[...] + r * (dxhat - xhat * jnp.mean(dxhat * xhat, axis=-1, keepdims=True))
            o_ref[...] = out.astype(o_ref.dtype)

    if mode == "nn":
        a_spec = pl.BlockSpec((tm, tk), lambda j, i, k: (i, k))
        b_spec = pl.BlockSpec((tk, tn), lambda j, i, k: (k, j))
    elif mode == "nt":
        a_spec = pl.BlockSpec((tm, tk), lambda j, i, k: (i, k))
        b_spec = pl.BlockSpec((tn, tk), lambda j, i, k: (j, k))
    else:
        a_spec = pl.BlockSpec((tk, tm), lambda j, i, k: (k, i))
        b_spec = pl.BlockSpec((tk, tn), lambda j, i, k: (k, j))
    o_spec = pl.BlockSpec((tm, tn), lambda j, i, k: (i, j))
    in_specs, args = [], []
    for a, b in pairs:
        in_specs += [a_spec, b_spec]
        args += [a, b]
    if residual is not None:
        in_specs.append(o_spec)
        args.append(residual)
    out_specs, out_shape, sem = o_spec, jax.ShapeDtypeStruct((M, N), out_dtype), ("parallel", "parallel", "arbitrary")
    if norm_bwd is not None:
        x, g, res = norm_bwd
        vec = pl.BlockSpec((1, tn), lambda j, i, k: (0, j))
        in_specs += [o_spec, vec, o_spec]
        args += [x, g.reshape(1, N), res]
        out_specs, out_shape = [o_spec, vec], [out_shape, jax.ShapeDtypeStruct((1, N), F32)]
        sem = ("arbitrary", "arbitrary", "arbitrary")
    return pl.pallas_call(
        body, name=name, grid=(N // tn, M // tm, nk),
        in_specs=in_specs, out_specs=out_specs, out_shape=out_shape,
        scratch_shapes=[pltpu.VMEM((tm, tn), F32)],
        compiler_params=_cp(sem),
    )(*args)


def _rms_rows(x):
    return lax.rsqrt(jnp.mean(x * x, axis=-1, keepdims=True) + EPS)


def _norm_fwd(name, x, g):
    T, Dm = x.shape
    tm = _tile(T, 512, 16)

    def body(x_ref, g_ref, o_ref):
        xv = x_ref[...]
        o_ref[...] = ((xv * _rms_rows(xv)) * g_ref[...]).astype(o_ref.dtype)

    return pl.pallas_call(
        body, name=name, grid=(T // tm,),
        in_specs=[pl.BlockSpec((tm, Dm), lambda i: (i, 0)), pl.BlockSpec((1, Dm), lambda i: (0, 0))],
        out_specs=pl.BlockSpec((tm, Dm), lambda i: (i, 0)),
        out_shape=jax.ShapeDtypeStruct((T, Dm), _MXU),
        compiler_params=_cp(("parallel",)),
    )(x, g.reshape(1, Dm))


def _gelu(x):
    return jax.nn.gelu(x)


def _gelu_grad(x):
    c0, c1 = math.sqrt(2.0 / math.pi), 0.044715
    t = jnp.tanh(c0 * (x + c1 * x * x * x))
    return 0.5 * (1.0 + t) + 0.5 * x * (1.0 - t * t) * c0 * (1.0 + 3.0 * c1 * x * x)


def _head64_rms(x):
    lo = _lane(x.shape) < FOX_HEAD_DIM
    xx = x * x
    sa = jnp.sum(jnp.where(lo, xx, 0.0), axis=-1, keepdims=True)
    sb = jnp.sum(jnp.where(lo, 0.0, xx), axis=-1, keepdims=True)
    inv = 1.0 / FOX_HEAD_DIM
    return jnp.where(lo, lax.rsqrt(sa * inv + EPS), lax.rsqrt(sb * inv + EPS))


def _head64_mean(x):
    lo = _lane(x.shape) < FOX_HEAD_DIM
    sa = jnp.sum(jnp.where(lo, x, 0.0), axis=-1, keepdims=True)
    sb = jnp.sum(jnp.where(lo, 0.0, x), axis=-1, keepdims=True)
    return jnp.where(lo, sa, sb) * (1.0 / FOX_HEAD_DIM)


def _masked_ws(ws_ref, g):
    w = ws_ref[g]
    keep = (_row(w.shape) // CHUNK) >= (_lane(w.shape) // CHUNK)
    return jnp.where(keep, w, 0.0)


def _even_pre(name, z, g_v, w_s, bs_cols, g_qn2, g_kn2, b_f_pad):
    T = z.shape[0]
    tm = _tile(T, 256, A_BLOCK)
    nb = tm // A_BLOCK

    def body(z_ref, gv_ref, ws_ref, bs_ref, gq_ref, gk_ref, bf_ref, ya_ref, q_ref, k_ref, v_ref, c_ref, ct_ref, carry):
        i = pl.program_id(0)

        @pl.when(i == 0)
        def _():
            carry[...] = jnp.zeros_like(carry)

        wm = [_mx(_masked_ws(ws_ref, g)) for g in range(A_GROUPS)]
        for g in range(A_GROUPS):
            sl = slice(g * A_GROUP_DIM, (g + 1) * A_GROUP_DIM)
            u = _gelu(z_ref[:, sl])
            v = _gelu(z_ref[:, A_WIDTH + g * A_GROUP_DIM:A_WIDTH + (g + 1) * A_GROUP_DIM])
            vn = _mx((v * _rms_rows(v)) * gv_ref[:, sl])
            for n in range(nb):
                rows = slice(n * A_BLOCK, (n + 1) * A_BLOCK)
                s = jnp.dot(wm[g], vn[rows], preferred_element_type=F32) + bs_ref[:, g:g + 1]
                ya_ref[rows, sl] = (u[rows] * s).astype(ya_ref.dtype)
        o = 2 * A_WIDTH
        for j in range(FOX_WIDTH // LANES):
            sl = slice(j * LANES, (j + 1) * LANES)
            q = z_ref[:, o + j * LANES:o + (j + 1) * LANES]
            q_ref[:, sl] = (((q * _head64_rms(q)) * gq_ref[...]) * (1.0 / math.sqrt(FOX_HEAD_DIM))).astype(q_ref.dtype)
            k = z_ref[:, o + FOX_WIDTH + j * LANES:o + FOX_WIDTH + (j + 1) * LANES]
            k_ref[:, sl] = ((k * _head64_rms(k)) * gk_ref[...]).astype(k_ref.dtype)
            v_ref[:, sl] = z_ref[:, o + 2 * FOX_WIDTH + j * LANES:o + 2 * FOX_WIDTH + (j + 1) * LANES].astype(v_ref.dtype)
        zf = z_ref[:, o + 3 * FOX_WIDTH:o + 3 * FOX_WIDTH + LANES]
        logf = jnp.where(_lane(zf.shape) < FOX_HEADS, jax.nn.log_sigmoid(zf + bf_ref[...]), 0.0)
        tri = (_row((tm, tm)) >= _lane((tm, tm))).astype(F32)
        c = _dot_f32(tri, logf) + carry[0:1, :]
        c_ref[...] = c
        ct_ref[...] = c.T[0:SUBLANES, :]
        carry[0:1, :] = c[tm - 1:tm, :]

    wide = lambda n: pl.BlockSpec((tm, n), lambda i: (i, 0))
    full = lambda s: pl.BlockSpec(s, lambda i: tuple(0 for _ in s))
    out512 = jax.ShapeDtypeStruct((T, 512), _MXU)
    return pl.pallas_call(
        body, name=name, grid=(T // tm,),
        in_specs=[wide(EVEN_IN_PAD), full((1, A_WIDTH)), full((A_GROUPS, A_BLOCK, A_BLOCK)), full((A_BLOCK, LANES)),
                  full((1, LANES)), full((1, LANES)), full((1, LANES))],
        out_specs=[wide(512), wide(512), wide(512), wide(512), wide(LANES), pl.BlockSpec((SUBLANES, tm), lambda i: (0, i))],
        out_shape=[out512, out512, out512, out512, jax.ShapeDtypeStruct((T, LANES), F32), jax.ShapeDtypeStruct((SUBLANES, T), F32)],
        scratch_shapes=[pltpu.VMEM((SUBLANES, LANES), F32)],
        compiler_params=_cp(("arbitrary",)),
    )(z, g_v, w_s, bs_cols, g_qn2, g_kn2, b_f_pad)


def _tri_steps(n, by_rows):
    if by_rows:
        pairs = [(q, k) for q in range(n) for k in range(q + 1)]
    else:
        pairs = [(q, k) for k in range(n) for q in range(k, n)]
    return (jnp.asarray(np.array([p[0] for p in pairs], np.int32)), jnp.asarray(np.array([p[1] for p in pairs], np.int32)))


def _fox_logits(q, k, c_col, c_row, diag, tq):
    s = _dot_nt(q, k) + (c_col - c_row)
    if diag:
        s = jnp.where(_row((tq, tq)) >= _lane((tq, tq)), s, -jnp.inf)
    return s


FOX_SKIP_BELOW = 104.0


def _fox_skip_flags(c, g_qn2, g_kn2):
    T = c.shape[0]
    tq = _tile(T, FOX_TILE, LANES)
    nblk = T // tq
    cb = c.reshape(nblk, tq, LANES)
    c_first, c_last = cb[:, 0, :FOX_HEADS], cb[:, tq - 1, :FOX_HEADS]
    bound = 17.0 * jnp.max(jnp.abs(g_qn2)) * jnp.max(jnp.abs(g_kn2))
    keep = bound + c_first[:, None, :] - c_last[None, :, :] > -FOX_SKIP_BELOW
    return keep.astype(F32).reshape(-1)


def _on_diag_or_not(qi, ki, step):
    @pl.when(qi == ki)
    def _():
        step(True)

    @pl.when(qi != ki)
    def _():
        step(False)


def _side_specs(side):
    if side is None:
        return [], [], [], []
    return [pl.BlockSpec(memory_space=pl.ANY)] * side.n, list(side.arrays), list(side.out_shape), list(side.scratch)


def _side_run(side, s_id, nsteps, side_in, side_out, sems):
    if side is None:
        return

    @pl.when(s_id == 0)
    def _():
        side.start(side_in, side_out, sems)

    @pl.when(s_id == nsteps // 2)
    def _():
        side.middle(side_in, side_out, sems)

    @pl.when(s_id == nsteps - 1)
    def _():
        side.finish(side_in, side_out, sems)


def _fox_fwd(name, qn, kn, vv, c, ct, flags, side=None):
    T = qn.shape[0]
    tq = _tile(T, FOX_TILE, LANES)
    nblk = T // tq
    qs, ks = _tri_steps(nblk, True)
    nsteps = int(qs.shape[0])
    npair = FOX_WIDTH // LANES
    s_specs, s_args, s_shapes, s_scratch = _side_specs(side)

    def body(qs_ref, ks_ref, *refs):
        main_in, side_in, main_out, side_out, main_scr, sems = _split_side(refs, 6, 3, 3, side)
        q_ref, k_ref, v_ref, c_ref, ct_ref, flag_ref = main_in
        yb_ref, yf_ref, lse_ref = main_out
        acc, m_s, l_s = main_scr
        s_id = pl.program_id(0)
        qi, ki = qs_ref[s_id], ks_ref[s_id]
        _side_run(side, s_id, nsteps, side_in, side_out, sems)

        @pl.when(ki == 0)
        def _():
            acc[...] = jnp.zeros_like(acc)
            m_s[...] = jnp.full_like(m_s, -jnp.inf)
            l_s[...] = jnp.zeros_like(l_s)

        lo = _lane((tq, LANES)) < FOX_HEAD_DIM

        def step(diag):
            for j in range(npair):
                def pair(j=j):
                    sl = slice(j * LANES, (j + 1) * LANES)
                    q2, k2, v2 = q_ref[:, sl], k_ref[:, sl], v_ref[:, sl]
                    outs = []
                    for e in range(2):
                        h = 2 * j + e
                        qh = jnp.where(lo if e == 0 else ~lo, q2, jnp.zeros_like(q2))
                        s = _fox_logits(qh, k2, c_ref[:, h:h + 1], ct_ref[h:h + 1, :], diag, tq)
                        m_old = m_s[h]
                        m_new = jnp.maximum(m_old, jnp.max(s, axis=-1, keepdims=True))
                        alpha = jnp.exp(m_old - m_new)
                        p = jnp.exp(s - m_new)
                        l_s[h] = alpha * l_s[h] + jnp.sum(p, axis=-1, keepdims=True)
                        m_s[h] = m_new
                        outs.append((alpha, _dot(p, v2)))
                    alpha2 = jnp.where(lo, outs[0][0], outs[1][0])
                    acc[:, sl] = alpha2 * acc[:, sl] + jnp.where(lo, outs[0][1], outs[1][1])

                if diag:
                    pair()
                else:
                    f = (qi * nblk + ki) * FOX_HEADS + 2 * j
                    pl.when((flag_ref[f] > 0.5) | (flag_ref[f + 1] > 0.5))(pair)

        _on_diag_or_not(qi, ki, step)

        @pl.when(ki == qi)
        def _():
            lse = jnp.zeros((tq, LANES), F32)
            for j in range(npair):
                sl = slice(j * LANES, (j + 1) * LANES)
                inv = jnp.where(lo, 1.0 / l_s[2 * j], 1.0 / l_s[2 * j + 1])
                y = acc[:, sl] * inv
                yf_ref[:, sl] = y
                yb_ref[:, sl] = y.astype(yb_ref.dtype)
            for h in range(FOX_HEADS):
                lse = jnp.where(_lane((tq, LANES)) == h, m_s[h] + jnp.log(l_s[h]), lse)
            lse_ref[...] = lse

    qmap = lambda s, qs_r, ks_r: (qs_r[s], 0)
    kmap = lambda s, qs_r, ks_r: (ks_r[s], 0)
    grid_spec = pltpu.PrefetchScalarGridSpec(
        num_scalar_prefetch=2, grid=(nsteps,),
        in_specs=[pl.BlockSpec((tq, 512), qmap), pl.BlockSpec((tq, 512), kmap), pl.BlockSpec((tq, 512), kmap),
                  pl.BlockSpec((tq, LANES), qmap),
                  pl.BlockSpec((SUBLANES, tq), lambda s, qs_r, ks_r: (0, ks_r[s])), pl.BlockSpec(memory_space=pltpu.SMEM)] + s_specs,
        out_specs=[pl.BlockSpec((tq, 512), qmap), pl.BlockSpec((tq, 512), qmap), pl.BlockSpec((tq, LANES), qmap)] + s_specs,
        scratch_shapes=[pltpu.VMEM((tq, 512), F32), pltpu.VMEM((FOX_HEADS, tq, 1), F32), pltpu.VMEM((FOX_HEADS, tq, 1), F32)] + s_scratch,
    )
    res = pl.pallas_call(
        body, name=name, grid_spec=grid_spec,
        out_shape=[jax.ShapeDtypeStruct((T, 512), _MXU), jax.ShapeDtypeStruct((T, 512), F32), jax.ShapeDtypeStruct((T, LANES), F32)] + s_shapes,
        compiler_params=_cp(("arbitrary",)),
    )(qs, ks, qn, kn, vv, c, ct, flags, *s_args)
    return res[:3], res[3:]


def _shift_down(ext, k, tm):
    return pltpu.roll(ext, k, axis=0)[HALO:HALO + tm]


def _pool_p(z_ext, g, tm, pos):
    w = POOL_WINDOWS[g]
    s, span = z_ext, 1
    while span < w:
        s = s + pltpu.roll(s, span, axis=0)
        span *= 2
    cnt = jnp.minimum(pos + 1, w).astype(F32)
    return s[HALO:HALO + tm] / cnt - z_ext[HALO:HALO + tm]


def _odd_fwd(name, z, w_pool, s_pool, conv_w8):
    T = z.shape[0]
    tm = _tile(T, 256, HALO)
    r = tm // HALO

    def body(z_ref, zb_ref, wp_ref, sp_ref, cw_ref, y_ref):
        i = pl.program_id(0)
        first = i == 0
        pos = i * tm + _row((tm, LANES))
        for g in range(len(POOL_WINDOWS)):
            sl = slice(g * LANES, (g + 1) * LANES)
            halo = jnp.where(first, 0.0, zb_ref[:, sl])
            z_ext = jnp.concatenate([halo, z_ref[:, sl]], axis=0)
            p = _pool_p(z_ext, g, tm, pos)
            y_ref[:, sl] = (_dot(p, wp_ref[g]) * sp_ref[:, sl]).astype(y_ref.dtype)
        for j in range(CONV_WIDTH // LANES):
            sl = slice(j * LANES, (j + 1) * LANES)
            hd = slice(POOL_WIDTH + j * LANES, POOL_WIDTH + (j + 1) * LANES)
            gb = slice(POOL_WIDTH + CONV_WIDTH + j * LANES, POOL_WIDTH + CONV_WIDTH + (j + 1) * LANES)
            gc = slice(POOL_WIDTH + 2 * CONV_WIDTH + j * LANES, POOL_WIDTH + 2 * CONV_WIDTH + (j + 1) * LANES)
            xg_b = jnp.where(first, 0.0, zb_ref[:, gc] * zb_ref[:, hd])
            xg = jnp.concatenate([xg_b, z_ref[:, gc] * z_ref[:, hd]], axis=0)
            conv = (cw_ref[0:1, sl] * _shift_down(xg, 2, tm) + cw_ref[1:2, sl] * _shift_down(xg, 1, tm)
                    + cw_ref[2:3, sl] * xg[HALO:HALO + tm])
            y_ref[:, POOL_WIDTH + j * LANES:POOL_WIDTH + (j + 1) * LANES] = (z_ref[:, gb] * conv).astype(y_ref.dtype)

    full = lambda s: pl.BlockSpec(s, lambda i: tuple(0 for _ in s))
    return pl.pallas_call(
        body, name=name, grid=(T // tm,),
        in_specs=[pl.BlockSpec((tm, ODD_IN), lambda i: (i, 0)),
                  pl.BlockSpec((HALO, ODD_IN), lambda i: (jnp.maximum(i * r - 1, 0), 0)),
                  full((4, LANES, LANES)), full((1, POOL_WIDTH)), full((SUBLANES, CONV_WIDTH))],
        out_specs=pl.BlockSpec((tm, 1024), lambda i: (i, 0)),
        out_shape=jax.ShapeDtypeStruct((T, 1024), _MXU),
        compiler_params=_cp(("parallel",)),
    )(z, z, w_pool, s_pool, conv_w8)


def _mem_fwd(name, mem, g_mem, wkv, gk):
    M = mem.shape[0]

    def body(mem_ref, g_ref, w_ref, gk_ref, m_ref, kv_ref, kn_ref, v_ref):
        x = mem_ref[...]
        m = ((x * _rms_rows(x)) * g_ref[...]).astype(m_ref.dtype)
        m_ref[...] = m
        kv = _dot(m, w_ref[...])
        kv_ref[...] = kv
        for h in range(XA_HEADS):
            sl = slice(h * LANES, (h + 1) * LANES)
            k = kv[:, sl]
            kn_ref[:, sl] = ((k * _rms_rows(k)) * gk_ref[...]).astype(kn_ref.dtype)
        v_ref[...] = kv[:, XA_WIDTH:].astype(v_ref.dtype)

    return pl.pallas_call(
        body, name=name,
        out_shape=[jax.ShapeDtypeStruct((M, D_MODEL), _MXU), jax.ShapeDtypeStruct((M, 2 * XA_WIDTH), F32),
                   jax.ShapeDtypeStruct((M, XA_WIDTH), _MXU), jax.ShapeDtypeStruct((M, XA_WIDTH), _MXU)],
        compiler_params=pltpu.CompilerParams(vmem_limit_bytes=VMEM_LIMIT),
    )(mem, g_mem.reshape(1, D_MODEL), wkv, gk.reshape(1, XA_HEAD_DIM))


def _xa_probs(qx, gq, kn_h):
    r = _rms_rows(qx)
    qhat = qx * r
    qn = qhat * gq
    s = _dot_nt(qn, kn_h) * (1.0 / math.sqrt(XA_HEAD_DIM))
    s = s - jnp.max(s, axis=-1, keepdims=True)
    e = jnp.exp(s)
    return qhat, r, qn, e / jnp.sum(e, axis=-1, keepdims=True)


def _xa_fwd(name, qx, gq, kn, vx):
    T, M = qx.shape[0], kn.shape[0]
    tm = _tile(T, 512, 16)

    def body(q_ref, gq_ref, k_ref, v_ref, o_ref):
        for h in range(XA_HEADS):
            sl = slice(h * LANES, (h + 1) * LANES)
            _, _, _, p = _xa_probs(q_ref[:, sl], gq_ref[...], k_ref[:, sl])
            o_ref[:, sl] = _dot(p, v_ref[:, sl]).astype(o_ref.dtype)

    full = lambda s: pl.BlockSpec(s, lambda i: tuple(0 for _ in s))
    return pl.pallas_call(
        body, name=name, grid=(T // tm,),
        in_specs=[pl.BlockSpec((tm, XA_WIDTH), lambda i: (i, 0)), full((1, LANES)), full((M, XA_WIDTH)), full((M, XA_WIDTH))],
        out_specs=pl.BlockSpec((tm, XA_WIDTH), lambda i: (i, 0)),
        out_shape=jax.ShapeDtypeStruct((T, XA_WIDTH), _MXU),
        compiler_params=_cp(("parallel",)),
    )(qx, gq.reshape(1, XA_HEAD_DIM), kn, vx)


def _ffn_up(name, h, wg, wu):
    T, K = h.shape
    N = wg.shape[1]
    tm, tn = _tile(T, 512, 16), _tile(N, 1536, LANES)

    def body(h_ref, wg_ref, wu_ref, a_ref, b_ref, s_ref):
        hv = h_ref[...]
        a, b = _dot(hv, wg_ref[...]), _dot(hv, wu_ref[...])
        a_ref[...] = a.astype(a_ref.dtype)
        b_ref[...] = b.astype(b_ref.dtype)
        s_ref[...] = (jax.nn.silu(a) * b).astype(s_ref.dtype)

    w_spec = pl.BlockSpec((K, tn), lambda j, i: (0, j))
    o_spec = pl.BlockSpec((tm, tn), lambda j, i: (i, j))
    return pl.pallas_call(
        body, name=name, grid=(N // tn, T // tm),
        in_specs=[pl.BlockSpec((tm, K), lambda j, i: (i, 0)), w_spec, w_spec],
        out_specs=[o_spec, o_spec, o_spec],
        out_shape=[jax.ShapeDtypeStruct((T, N), _MXU), jax.ShapeDtypeStruct((T, N), _MXU), jax.ShapeDtypeStruct((T, N), _MXU)],
        compiler_params=_cp(("parallel", "parallel")),
    )(h, wg, wu)


def _loss_grad(name, y, target):
    T, Dm = y.shape
    tm = _tile(T, 512, 8)

    def body(y_ref, t_ref, dy_ref, l_ref):
        e = y_ref[...] - t_ref[...]
        dy_ref[...] = e * (1.0 / Dm)

        @pl.when(pl.program_id(0) == 0)
        def _():
            l_ref[...] = jnp.zeros_like(l_ref)

        l_ref[...] += jnp.sum(e * e, axis=0, keepdims=True) * (0.5 / Dm)

    blk = pl.BlockSpec((tm, Dm), lambda i: (i, 0))
    return pl.pallas_call(
        body, name=name, grid=(T // tm,),
        in_specs=[blk, blk], out_specs=[blk, pl.BlockSpec((1, Dm), lambda i: (0, 0))],
        out_shape=[jax.ShapeDtypeStruct((T, Dm), F32), jax.ShapeDtypeStruct((1, Dm), F32)],
        compiler_params=_cp(("arbitrary",)),
    )(y, target)


def _pad_lanes(v, n=LANES):
    v = v.reshape(1, -1)
    return jnp.pad(v, ((0, 0), (0, n - v.shape[1])))


def _layer_fwd(l, x, mem, P, side=None, on_side=None):
    sv = {"x0": x}
    W = _layer_weights(l, P, "in")
    h1 = _norm_fwd(f"norm_mix{l}", x, W["g_mix"])
    sv["h1"] = h1
    if l % 2 == 0:
        z = _mm(f"e_in{l}", [(h1, W["w_in"])], "nn")
        ya, qn, kn, vv, c, ct = _even_pre(f"even_pre{l}", z, W["g_v"], W["w_s"], W["bs_cols"], W["g_qn2"], W["g_kn2"], W["b_f_pad"])
        flags = _fox_skip_flags(c, W["g_qn2"], W["g_kn2"])
        (yb, ybf, lse), side_out = _fox_fwd(f"fox_fwd{l}", qn, kn, vv, c, ct, flags, side=side)
        if on_side is not None:
            on_side(side_out)
        W.update(_layer_weights(l, P, "rest"))
        sv.update(z=z, ya=ya, qn=qn, kn=kn, vv=vv, c=c, ct=ct, yb=yb, ybf=ybf, lse=lse, flags=flags)
        x1 = _mm(f"e_out{l}", [(ya, W["w_out"][:A_WIDTH]), (yb, W["w_out"][A_WIDTH:])], "nn", residual=x)
    else:
        W.update(_layer_weights(l, P, "rest"))
        z = _mm(f"o_in{l}", [(h1, W["w_in"])], "nn")
        ycat = _odd_fwd(f"odd_fwd{l}", z, W["w_pool"], W["s_pool"], W["conv_w8"])
        sv.update(z=z, ycat=ycat)
        x1 = _mm(f"o_out{l}", [(ycat, W["w_out"])], "nn", residual=x)
    sv["x1"] = x1
    h2 = _norm_fwd(f"norm_xa{l}", x1, W["g_xa"])
    qx = _mm(f"xa_q{l}", [(h2, W["wq"])], "nn")
    m, kv, kn_x, vx = _mem_fwd(f"mem_fwd{l}", mem, W["g_mem"], W["wkv"], W["gk"])
    o = _xa_fwd(f"xa_fwd{l}", qx, W["gq"], kn_x, vx)
    x2 = _mm(f"xa_o{l}", [(o, W["wo"])], "nn", residual=x1)
    sv.update(h2=h2, qx=qx, m=m, kv=kv, kn_x=kn_x, vx=vx, o=o, x2=x2)
    h3 = _norm_fwd(f"norm_ffn{l}", x2, W["g_ffn"])
    a, b, s = _ffn_up(f"ffn_up{l}", h3, W["w_gate"], W["w_up"])
    x3 = _mm(f"ffn_down{l}", [(s, W["w_down"])], "nn", residual=x2)
    sv.update(h3=h3, a=a, b=b, s=s)
    return x3, sv, W


def _layer_weights(l, P, part):
    i = l // 2
    if part == "in":
        W = {"g_mix": P["g_mix"][l]}
        if l % 2 == 0:
            w_in = _mx(P["e_w_in"][i])
            if w_in.shape[1] < EVEN_IN_PAD:
                w_in = jnp.pad(w_in, ((0, 0), (0, EVEN_IN_PAD - w_in.shape[1])))
            W.update(w_in=w_in, g_v=P["e_g_v"][i].reshape(1, A_WIDTH), w_s=P["e_w_s"][i],
                     bs_cols=jnp.pad(P["e_b_s"][i].T, ((0, 0), (0, LANES - A_GROUPS))),
                     g_qn2=jnp.tile(P["e_g_qn"][i], 2).reshape(1, LANES), g_kn2=jnp.tile(P["e_g_kn"][i], 2).reshape(1, LANES),
                     b_f_pad=_pad_lanes(P["e_b_f"][i]))
        return W
    W = {k: P[k][l] for k in ("g_xa", "g_mem", "g_ffn")}
    W.update(wq=_mx(P["xa_wq"][l]), wkv=_mx(P["xa_wkv"][l]), wo=_mx(P["xa_wo"][l]), gq=P["xa_gq"][l], gk=P["xa_gk"][l],
             w_gate=_mx(P["w_gate"][l]), w_up=_mx(P["w_up"][l]), w_down=_mx(P["w_down"][l]))
    if l % 2 == 0:
        W.update(w_out=_mx(P["e_w_out"][i]))
    else:
        W.update(w_in=_mx(P["o_w_in"][i]), w_pool=_mx(P["o_w_pool"][i]), s_pool=P["o_s_pool"][i].reshape(1, POOL_WIDTH),
                 conv_w8=jnp.pad(P["o_conv_w"][i], ((0, SUBLANES - CONV_K), (0, 0))), w_out=_mx(P["o_w_out"][i]))
    return W


def _ffn_bwd_act(name, g, wd, a, b):
    T, N = a.shape
    K = g.shape[1]
    tm, tn = _tile(T, 512, 16), _tile(N, 1536, LANES)

    def body(g_ref, wd_ref, a_ref, b_ref, da_ref, db_ref):
        ds = _dot_nt(g_ref[...], wd_ref[...])
        av = a_ref[...].astype(F32)
        sig = jax.nn.sigmoid(av)
        da_ref[...] = (ds * b_ref[...].astype(F32) * (sig * (1.0 + av * (1.0 - sig)))).astype(da_ref.dtype)
        db_ref[...] = (ds * (av * sig)).astype(db_ref.dtype)

    o_spec = pl.BlockSpec((tm, tn), lambda j, i: (i, j))
    return pl.pallas_call(
        body, name=name, grid=(N // tn, T // tm),
        in_specs=[pl.BlockSpec((tm, K), lambda j, i: (i, 0)), pl.BlockSpec((tn, K), lambda j, i: (j, 0)), o_spec, o_spec],
        out_specs=[o_spec, o_spec],
        out_shape=[jax.ShapeDtypeStruct((T, N), _MXU), jax.ShapeDtypeStruct((T, N), _MXU)],
        compiler_params=_cp(("parallel", "parallel")),
    )(g, wd, a, b)


def _xa_bwd(name, qx, do, gq, kn, vx):
    T, M = qx.shape[0], kn.shape[0]
    tm = _tile(T, 512, 16)

    def body(q_ref, do_ref, gq_ref, k_ref, v_ref, dq_ref, dk_ref, dv_ref, dg_ref):
        @pl.when(pl.program_id(0) == 0)
        def _():
            dk_ref[...] = jnp.zeros_like(dk_ref)
            dv_ref[...] = jnp.zeros_like(dv_ref)
            dg_ref[...] = jnp.zeros_like(dg_ref)

        gqv = gq_ref[...]
        for h in range(XA_HEADS):
            sl = slice(h * LANES, (h + 1) * LANES)
            qhat, r, qn, p = _xa_probs(q_ref[:, sl], gqv, k_ref[:, sl])
            do_h = do_ref[:, sl]
            dp = _dot_nt(do_h, v_ref[:, sl])
            dv_ref[:, sl] += _dot_tn(p, do_h)
            dsm = p * (dp - jnp.sum(p * dp, axis=-1, keepdims=True)) * (1.0 / math.sqrt(XA_HEAD_DIM))
            dqn = _dot(dsm, k_ref[:, sl])
            dk_ref[:, sl] += _dot_tn(dsm, qn)
            dg_ref[...] += jnp.sum(dqn * qhat, axis=0, keepdims=True)
            dxh = dqn * gqv
            dq_ref[:, sl] = (r * (dxh - qhat * jnp.mean(dxh * qhat, axis=-1, keepdims=True))).astype(dq_ref.dtype)

    full = lambda s: pl.BlockSpec(s, lambda i: tuple(0 for _ in s))
    blk = pl.BlockSpec((tm, XA_WIDTH), lambda i: (i, 0))
    return pl.pallas_call(
        body, name=name, grid=(T // tm,),
        in_specs=[blk, blk, full((1, LANES)), full((M, XA_WIDTH)), full((M, XA_WIDTH))],
        out_specs=[blk, full((M, XA_WIDTH)), full((M, XA_WIDTH)), full((1, LANES))],
        out_shape=[jax.ShapeDtypeStruct((T, XA_WIDTH), _MXU), jax.ShapeDtypeStruct((M, XA_WIDTH), F32),
                   jax.ShapeDtypeStruct((M, XA_WIDTH), F32), jax.ShapeDtypeStruct((1, LANES), F32)],
        compiler_params=_cp(("arbitrary",)),
    )(qx, do, gq.reshape(1, XA_HEAD_DIM), kn, vx)


def _mem_bwd(name, mem, g_mem, m, kv, gk, dkn, dvx, wkv):
    M = mem.shape[0]

    def body(mem_ref, g_ref, m_ref, kv_ref, gk_ref, dkn_ref, dvx_ref, w_ref, dw_ref, dgk_ref, dgm_ref):
        gkv = gk_ref[...]
        dgk = jnp.zeros((1, LANES), F32)
        parts = []
        for h in range(XA_HEADS):
            sl = slice(h * LANES, (h + 1) * LANES)
            k = kv_ref[:, sl]
            r = _rms_rows(k)
            khat = k * r
            dkn_h = dkn_ref[:, sl]
            dgk = dgk + jnp.sum(dkn_h * khat, axis=0, keepdims=True)
            dxh = dkn_h * gkv
            parts.append(r * (dxh - khat * jnp.mean(dxh * khat, axis=-1, keepdims=True)))
        dgk_ref[...] = dgk
        dkv = jnp.concatenate(parts + [dvx_ref[...]], axis=1)
        dw_ref[...] = _dot_tn(m_ref[...], dkv)
        dm = _dot_nt(dkv, w_ref[...])
        x = mem_ref[...]
        dgm_ref[...] = jnp.sum(dm * (x * _rms_rows(x)), axis=0, keepdims=True)

    return pl.pallas_call(
        body, name=name,
        out_shape=[jax.ShapeDtypeStruct((D_MODEL, 2 * XA_WIDTH), F32), jax.ShapeDtypeStruct((1, LANES), F32),
                   jax.ShapeDtypeStruct((1, D_MODEL), F32)],
        compiler_params=pltpu.CompilerParams(vmem_limit_bytes=VMEM_LIMIT),
    )(mem, g_mem.reshape(1, D_MODEL), m, kv, gk.reshape(1, XA_HEAD_DIM), dkn, dvx, wkv)


def _fox_bwd_common(q2, k2, v2, dy2, yf2, c_ref, ct_ref, lse_ref, e, h, lo, diag, tq):
    sel = lo if e == 0 else ~lo
    qh = jnp.where(sel, q2, jnp.zeros_like(q2))
    kh = jnp.where(sel, k2, jnp.zeros_like(k2))
    dyh = jnp.where(sel, dy2, 0.0)
    s = _fox_logits(qh, k2, c_ref[:, h:h + 1], ct_ref[h:h + 1, :], diag, tq)
    p = jnp.exp(s - lse_ref[:, h:h + 1])
    dp = _dot_nt(dyh, v2)
    delta = jnp.sum(dyh * yf2, axis=-1, keepdims=True)
    return p, p * (dp - delta), qh, kh, dyh


def _fox_bwd(name, qn, kn, vv, c, ct, lse, dycat, ybf, flags, side=None):
    T = qn.shape[0]
    tq = _tile(T, FOX_TILE, LANES)
    nblk = T // tq
    qs, ks = _tri_steps(nblk, False)
    nsteps = int(qs.shape[0])
    scale = 1.0 / math.sqrt(FOX_HEAD_DIM)
    s_specs, s_args, s_shapes, s_scratch = _side_specs(side)

    def body(qs_ref, ks_ref, *refs):
        main_in, side_in, main_out, side_out, main_scr, sems = _split_side(refs, 9, 5, 3, side)
        q_ref, k_ref, v_ref, c_ref, ct_ref, lse_ref, dy_ref, yf_ref, flag_ref = main_in
        dq_hbm, drow_hbm, dk_ref, dv_ref, dct_ref = main_out
        dq_acc, drow_acc, out_sem = main_scr
        s_id = pl.program_id(0)
        qi, ki = qs_ref[s_id], ks_ref[s_id]
        _side_run(side, s_id, nsteps, side_in, side_out, sems)
        rows = pl.ds(pl.multiple_of(qi * tq, tq), tq)

        @pl.when(qi == ki)
        def _():
            dk_ref[...] = jnp.zeros_like(dk_ref)
            dv_ref[...] = jnp.zeros_like(dv_ref)
            dct_ref[...] = jnp.zeros_like(dct_ref)

        @pl.when(ki == 0)
        def _():
            dq_acc[rows, :] = jnp.zeros((tq, FOX_WIDTH), F32)
            drow_acc[rows, :] = jnp.zeros((tq, LANES), F32)

        lo = _lane((tq, LANES)) < FOX_HEAD_DIM

        def step(diag):
            for j in range(FOX_WIDTH // LANES):
                sl = slice(j * LANES, (j + 1) * LANES)
                for e in range(2):
                    h = 2 * j + e

                    def head(h=h, e=e, sl=sl):
                        q2, k2, v2, dy2, yf2 = q_ref[:, sl], k_ref[:, sl], v_ref[:, sl], dy_ref[:, sl], yf_ref[:, sl]
                        p, ds, qh, kh, dyh = _fox_bwd_common(q2, k2, v2, dy2, yf2, c_ref, ct_ref, lse_ref, e, h, lo, diag, tq)
                        drow_acc[rows, :] += jnp.where(_lane((tq, LANES)) == h, jnp.sum(ds, axis=-1, keepdims=True), 0.0)
                        dct_ref[h:h + 1, :] -= jnp.sum(ds, axis=0, keepdims=True)
                        dq_acc[rows, sl] += _dot(ds, kh) * scale
                        dk_ref[:, sl] += _dot_tn(ds, qh)
                        dv_ref[:, sl] += _dot_tn(p, dyh)

                    if diag:
                        head()
                    else:
                        pl.when(flag_ref[(qi * nblk + ki) * FOX_HEADS + h] > 0.5)(head)

        _on_diag_or_not(qi, ki, step)

        @pl.when(s_id == nsteps - 1)
        def _():
            for src, dst in ((dq_acc, dq_hbm), (drow_acc, drow_hbm)):
                cp = pltpu.make_async_copy(src, dst, out_sem)
                cp.start()
                cp.wait()

    qmap = lambda s, a, b: (a[s], 0)
    kmap = lambda s, a, b: (b[s], 0)
    ctmap = lambda s, a, b: (0, b[s])
    any_spec = pl.BlockSpec(memory_space=pl.ANY)
    grid_spec = pltpu.PrefetchScalarGridSpec(
        num_scalar_prefetch=2, grid=(nsteps,),
        in_specs=[pl.BlockSpec((tq, 512), qmap), pl.BlockSpec((tq, 512), kmap), pl.BlockSpec((tq, 512), kmap),
                  pl.BlockSpec((tq, LANES), qmap), pl.BlockSpec((SUBLANES, tq), ctmap),
                  pl.BlockSpec((tq, LANES), qmap), pl.BlockSpec((tq, 512), lambda s, a, b: (a[s], 1)), pl.BlockSpec((tq, 512), qmap),
                  pl.BlockSpec(memory_space=pltpu.SMEM)] + s_specs,
        out_specs=[any_spec, any_spec, pl.BlockSpec((tq, 512), kmap), pl.BlockSpec((tq, 512), kmap), pl.BlockSpec((SUBLANES, tq), ctmap)] + s_specs,
        scratch_shapes=[pltpu.VMEM((T, FOX_WIDTH), F32), pltpu.VMEM((T, LANES), F32), pltpu.SemaphoreType.DMA] + s_scratch,
    )
    res = pl.pallas_call(
        body, name=name, grid_spec=grid_spec,
        out_shape=[jax.ShapeDtypeStruct((T, 512), F32), jax.ShapeDtypeStruct((T, LANES), F32), jax.ShapeDtypeStruct((T, 512), F32),
                   jax.ShapeDtypeStruct((T, 512), F32), jax.ShapeDtypeStruct((SUBLANES, T), F32)] + s_shapes,
        compiler_params=_cp(("arbitrary",)),
    )(qs, ks, qn, kn, vv, c, ct, lse, dycat, ybf, flags, *s_args)
    return res[:5], res[5:]


def _fold64(row):
    return row + pltpu.roll(row, FOX_HEAD_DIM, axis=1)


def _even_bwd(name, z, dycat, dqn, dkn, dvv, dct, drow, g_v, w_s, bs_cols, g_qn2, g_kn2, b_f_pad):
    T = z.shape[0]
    tm = _tile(T, 256, A_BLOCK)
    nb, nt = tm // A_BLOCK, T // tm

    def body(z_ref, dya_ref, dq_ref, dk_ref, dv_ref, dct_ref, drow_ref, gv_ref, ws_ref, bs_ref, gq_ref, gk_ref, bf_ref,
             dz_ref, dws_ref, dbs_ref, dgv_ref, dgq_ref, dgk_ref, dbf_ref, carry):
        @pl.when(pl.program_id(0) == 0)
        def _():
            for r in (dws_ref, dbs_ref, dgv_ref, dgq_ref, dgk_ref, dbf_ref, carry):
                r[...] = jnp.zeros_like(r)

        keep = (_row((A_BLOCK, A_BLOCK)) // CHUNK) >= (_lane((A_BLOCK, A_BLOCK)) // CHUNK)
        for g in range(A_GROUPS):
            sl = slice(g * A_GROUP_DIM, (g + 1) * A_GROUP_DIM)
            wm = _mx(_masked_ws(ws_ref, g))
            zu = z_ref[:, sl]
            zv = z_ref[:, A_WIDTH + g * A_GROUP_DIM:A_WIDTH + (g + 1) * A_GROUP_DIM]
            u, v = _gelu(zu), _gelu(zv)
            r = _rms_rows(v)
            vhat = v * r
            gvv = gv_ref[:, sl]
            vn = _mx(vhat * gvv)
            dya = dya_ref[:, sl]
            du_parts, dvn_parts = [], []
            for n in range(nb):
                rows = slice(n * A_BLOCK, (n + 1) * A_BLOCK)
                s = jnp.dot(wm, vn[rows], preferred_element_type=F32) + bs_ref[:, g:g + 1]
                du_parts.append(dya[rows] * s)
                d_s = dya[rows] * u[rows]
                dbs_ref[...] += jnp.where(_lane((A_BLOCK, LANES)) == g, jnp.sum(d_s, axis=-1, keepdims=True), 0.0)
                dws_ref[g] += jnp.where(keep, _dot_nt(d_s, vn[rows]), 0.0)
                dvn_parts.append(_dot_tn(wm, d_s))
            dvn = jnp.concatenate(dvn_parts, axis=0)
            dgv_ref[:, sl] += jnp.sum(dvn * vhat, axis=0, keepdims=True)
            dxh = dvn * gvv
            dv = r * (dxh - vhat * jnp.mean(dxh * vhat, axis=-1, keepdims=True))
            dz_ref[:, sl] = (jnp.concatenate(du_parts, axis=0) * _gelu_grad(zu)).astype(dz_ref.dtype)
            dz_ref[:, A_WIDTH + g * A_GROUP_DIM:A_WIDTH + (g + 1) * A_GROUP_DIM] = (dv * _gelu_grad(zv)).astype(dz_ref.dtype)
        o = 2 * A_WIDTH
        for j in range(FOX_WIDTH // LANES):
            sl = slice(j * LANES, (j + 1) * LANES)
            for (off, d_ref, g_ref, dg_ref) in ((o, dq_ref, gq_ref, dgq_ref), (o + FOX_WIDTH, dk_ref, gk_ref, dgk_ref)):
                zq = z_ref[:, off + j * LANES:off + (j + 1) * LANES]
                r = _head64_rms(zq)
                qhat = zq * r
                dn = d_ref[:, sl]
                dg_ref[...] += _fold64(jnp.sum(dn * qhat, axis=0, keepdims=True))
                dxh = dn * g_ref[...]
                dz_ref[:, off + j * LANES:off + (j + 1) * LANES] = (r * (dxh - qhat * _head64_mean(dxh * qhat))).astype(dz_ref.dtype)
            dz_ref[:, o + 2 * FOX_WIDTH + j * LANES:o + 2 * FOX_WIDTH + (j + 1) * LANES] = dv_ref[:, sl].astype(dz_ref.dtype)
        dct_v = dct_ref[...] + drow_ref[...].T[0:SUBLANES, :]
        upper = (_row((tm, tm)) >= _lane((tm, tm))).astype(F32)
        dlf_t = _dot_f32(dct_v, upper) + carry[:, 0:1]
        carry[...] += jnp.sum(dct_v, axis=1, keepdims=True)
        dlf = jnp.concatenate([dlf_t, jnp.zeros((LANES - SUBLANES, tm), F32)], axis=0).T
        zf = z_ref[:, o + 3 * FOX_WIDTH:o + 3 * FOX_WIDTH + LANES]
        dzf = dlf * jax.nn.sigmoid(-(zf + bf_ref[...]))
        dbf_ref[...] += jnp.sum(dzf, axis=0, keepdims=True)
        dz_ref[:, o + 3 * FOX_WIDTH:o + 3 * FOX_WIDTH + LANES] = dzf.astype(dz_ref.dtype)

    rev = lambda i: nt - 1 - i
    wide = lambda n, col=0: pl.BlockSpec((tm, n), lambda i: (rev(i), col))
    full = lambda s: pl.BlockSpec(s, lambda i: tuple(0 for _ in s))
    vec = full((1, LANES))
    return pl.pallas_call(
        body, name=name, grid=(nt,),
        in_specs=[wide(EVEN_IN_PAD), wide(512), wide(512), wide(512), wide(512), pl.BlockSpec((SUBLANES, tm), lambda i: (0, rev(i))),
                  wide(LANES), full((1, A_WIDTH)), full((A_GROUPS, A_BLOCK, A_BLOCK)), full((A_BLOCK, LANES)), vec, vec, vec],
        out_specs=[wide(EVEN_IN_PAD), full((A_GROUPS, A_BLOCK, A_BLOCK)), full((A_BLOCK, LANES)), full((1, A_WIDTH)), vec, vec, vec],
        out_shape=[jax.ShapeDtypeStruct((T, EVEN_IN_PAD), _MXU), jax.ShapeDtypeStruct((A_GROUPS, A_BLOCK, A_BLOCK), F32),
                   jax.ShapeDtypeStruct((A_BLOCK, LANES), F32), jax.ShapeDtypeStruct((1, A_WIDTH), F32),
                   jax.ShapeDtypeStruct((1, LANES), F32), jax.ShapeDtypeStruct((1, LANES), F32), jax.ShapeDtypeStruct((1, LANES), F32)],
        scratch_shapes=[pltpu.VMEM((SUBLANES, LANES), F32)],
        compiler_params=_cp(("arbitrary",)),
    )(z, dycat, dqn, dkn, dvv, dct, drow, g_v, w_s, bs_cols, g_qn2, g_kn2, b_f_pad)


def _shift_up(ext, k, tm):
    return pltpu.roll(ext, ext.shape[0] - k, axis=0)[0:tm]


def _odd_bwd(name, z, dycat, w_pool, s_pool, conv_w8):
    T = z.shape[0]
    tm = _tile(T, 256, HALO)
    r, nt = tm // HALO, T // tm
    n_ext = tm + HALO

    def body(z_ref, zb_ref, zn_ref, dy_ref, dyn_ref, wp_ref, sp_ref, cw_ref, dz_ref, dwp_ref, dsp_ref, dcw_ref):
        i = pl.program_id(0)

        @pl.when(i == 0)
        def _():
            for rr in (dwp_ref, dsp_ref, dcw_ref):
                rr[...] = jnp.zeros_like(rr)

        first, last = i == 0, i == nt - 1
        pos = i * tm + _row((tm, LANES))
        pos_ext = i * tm + _row((n_ext, LANES))
        for g, w in enumerate(POOL_WINDOWS):
            sl = slice(g * LANES, (g + 1) * LANES)
            z_ext = jnp.concatenate([jnp.where(first, 0.0, zb_ref[:, sl]), z_ref[:, sl]], axis=0)
            p = _pool_p(z_ext, g, tm, pos)
            spv = sp_ref[:, sl]
            dyc = dy_ref[:, sl]
            dsp_ref[:, sl] += jnp.sum(dyc * _dot(p, wp_ref[g]), axis=0, keepdims=True)
            dpw_ext = jnp.concatenate([dyc, jnp.where(last, 0.0, dyn_ref[:, sl])], axis=0) * spv
            dwp_ref[g] += _dot_tn(p, dpw_ext[0:tm])
            dp_ext = _dot_nt(dpw_ext, wp_ref[g])
            f = dp_ext / jnp.minimum(pos_ext + 1, w).astype(F32)
            span = 1
            while span < w:
                f = f + pltpu.roll(f, n_ext - span, axis=0)
                span *= 2
            dz_ref[:, sl] = (f[0:tm] - dp_ext[0:tm]).astype(dz_ref.dtype)
        for j in range(CONV_WIDTH // LANES):
            sl = slice(j * LANES, (j + 1) * LANES)
            hd = slice(POOL_WIDTH + j * LANES, POOL_WIDTH + (j + 1) * LANES)
            gb = slice(POOL_WIDTH + CONV_WIDTH + j * LANES, POOL_WIDTH + CONV_WIDTH + (j + 1) * LANES)
            gc = slice(POOL_WIDTH + 2 * CONV_WIDTH + j * LANES, POOL_WIDTH + 2 * CONV_WIDTH + (j + 1) * LANES)
            ysl = slice(POOL_WIDTH + j * LANES, POOL_WIDTH + (j + 1) * LANES)
            hdv, gbv, gcv = z_ref[:, hd], z_ref[:, gb], z_ref[:, gc]
            xg = gcv * hdv
            xg_ext = jnp.concatenate([jnp.where(first, 0.0, zb_ref[:, gc] * zb_ref[:, hd]), xg], axis=0)
            xg1, xg2 = _shift_down(xg_ext, 1, tm), _shift_down(xg_ext, 2, tm)
            w0, w1, w2 = cw_ref[0:1, sl], cw_ref[1:2, sl], cw_ref[2:3, sl]
            conv = w0 * xg2 + w1 * xg1 + w2 * xg
            dyd = dy_ref[:, ysl]
            dconv = dyd * gbv
            dconv_ext = jnp.concatenate([dconv, jnp.where(last, 0.0, dyn_ref[:, ysl] * zn_ref[:, gb])], axis=0)
            dcw_ref[0:1, sl] += jnp.sum(dconv * xg2, axis=0, keepdims=True)
            dcw_ref[1:2, sl] += jnp.sum(dconv * xg1, axis=0, keepdims=True)
            dcw_ref[2:3, sl] += jnp.sum(dconv * xg, axis=0, keepdims=True)
            dxg = w2 * dconv + w1 * _shift_up(dconv_ext, 1, tm) + w0 * _shift_up(dconv_ext, 2, tm)
            dz_ref[:, hd] = (dxg * gcv).astype(dz_ref.dtype)
            dz_ref[:, gb] = (dyd * conv).astype(dz_ref.dtype)
            dz_ref[:, gc] = (dxg * hdv).astype(dz_ref.dtype)

    full = lambda s: pl.BlockSpec(s, lambda i: tuple(0 for _ in s))
    back = lambda n: pl.BlockSpec((HALO, n), lambda i: (jnp.maximum(i * r - 1, 0), 0))
    nxt = lambda n: pl.BlockSpec((HALO, n), lambda i: (jnp.minimum((i + 1) * r, T // HALO - 1), 0))
    return pl.pallas_call(
        body, name=name, grid=(nt,),
        in_specs=[pl.BlockSpec((tm, ODD_IN), lambda i: (i, 0)), back(ODD_IN), nxt(ODD_IN),
                  pl.BlockSpec((tm, 1024), lambda i: (i, 0)), nxt(1024),
                  full((4, LANES, LANES)), full((1, POOL_WIDTH)), full((SUBLANES, CONV_WIDTH))],
        out_specs=[pl.BlockSpec((tm, ODD_IN), lambda i: (i, 0)), full((4, LANES, LANES)), full((1, POOL_WIDTH)), full((SUBLANES, CONV_WIDTH))],
        out_shape=[jax.ShapeDtypeStruct((T, ODD_IN), _MXU), jax.ShapeDtypeStruct((4, LANES, LANES), F32),
                   jax.ShapeDtypeStruct((1, POOL_WIDTH), F32), jax.ShapeDtypeStruct((SUBLANES, CONV_WIDTH), F32)],
        compiler_params=_cp(("arbitrary",)),
    )(z, z, z, dycat, dycat, w_pool, s_pool, conv_w8)


def _layer_bwd(l, g3, mem, W, sv, side_fn=None):
    i = l // 2
    G = {}
    side_out = ()
    da, db = _ffn_bwd_act(f"ffn_bwd_act{l}", g3, W["w_down"], sv["a"], sv["b"])
    G["w_down"] = _mm(f"d_w_down{l}", [(sv["s"], g3)], "tn")
    g2, dg = _mm(f"d_h3{l}", [(da, W["w_gate"]), (db, W["w_up"])], "nt", norm_bwd=(sv["x2"], W["g_ffn"], g3))
    G["w_gate"] = _mm(f"d_w_gate{l}", [(sv["h3"], da)], "tn")
    G["w_up"] = _mm(f"d_w_up{l}", [(sv["h3"], db)], "tn")
    G["g_ffn"] = dg[0]

    do = _mm(f"d_o{l}", [(g2, W["wo"])], "nt", out_dtype=_MXU)
    G["xa_wo"] = _mm(f"d_wo{l}", [(sv["o"], g2)], "tn")
    dqx, dkn, dvx, dgq = _xa_bwd(f"xa_bwd{l}", sv["qx"], do, W["gq"], sv["kn_x"], sv["vx"])
    G["xa_gq"] = dgq[0]
    g1, dg = _mm(f"d_h2{l}", [(dqx, W["wq"])], "nt", norm_bwd=(sv["x1"], W["g_xa"], g2))
    G["xa_wq"] = _mm(f"d_wq{l}", [(sv["h2"], dqx)], "tn")
    G["g_xa"] = dg[0]
    dwkv, dgk, dgm = _mem_bwd(f"mem_bwd{l}", mem, W["g_mem"], sv["m"], sv["kv"], W["gk"], dkn, dvx, W["wkv"])
    G["xa_wkv"], G["xa_gk"], G["g_mem"] = dwkv, dgk[0], dgm[0]

    dycat = _mm(f"d_ycat{l}", [(g1, W["w_out"])], "nt")
    if l % 2 == 0:
        G["e_w_out"] = jnp.concatenate([_mm(f"d_e_wout_a{l}", [(sv["ya"], g1)], "tn"), _mm(f"d_e_wout_b{l}", [(sv["yb"], g1)], "tn")], axis=0)
        fa = (sv["qn"], sv["kn"], sv["vv"], sv["c"], sv["ct"], sv["lse"], dycat, sv["ybf"], sv["flags"])
        (dqn, drow, dkn_f, dvv, dct), side_out = _fox_bwd(f"fox_bwd{l}", *fa, side=side_fn(G) if side_fn is not None else None)
        dz, dws, dbs, dgv, dgq2, dgk2, dbf = _even_bwd(f"even_bwd{l}", sv["z"], dycat, dqn, dkn_f, dvv, dct, drow, W["g_v"], W["w_s"],
                                                      W["bs_cols"], W["g_qn2"], W["g_kn2"], W["b_f_pad"])
        G.update(e_w_s=dws, e_b_s=dbs[:, :A_GROUPS].T, e_g_v=dgv[0], e_g_qn=dgq2[0, :FOX_HEAD_DIM], e_g_kn=dgk2[0, :FOX_HEAD_DIM],
                 e_b_f=dbf[0, :FOX_HEADS])
        G["e_w_in"] = _mm(f"d_e_win{l}", [(sv["h1"], dz)], "tn")
    else:
        G["o_w_out"] = _mm(f"d_o_wout{l}", [(sv["ycat"], g1)], "tn")
        dz, dwp, dsp, dcw = _odd_bwd(f"odd_bwd{l}", sv["z"], dycat, W["w_pool"], W["s_pool"], W["conv_w8"])
        G.update(o_w_pool=dwp, o_s_pool=dsp[0], o_conv_w=dcw[:CONV_K])
        G["o_w_in"] = _mm(f"d_o_win{l}", [(sv["h1"], dz)], "tn")
    g0, dg = _mm(f"d_h1{l}", [(dz, W["w_in"])], "nt", norm_bwd=(sv["x0"], W["g_mix"], g1))
    G["g_mix"] = dg[0]
    return g0, G, side_out


PACK_W = 1024
MESH_T = pl.DeviceIdType.MESH
_ANY = pl.BlockSpec(memory_space=pl.ANY)


def _my_place():
    x, y, c = lax.axis_index("x"), lax.axis_index("y"), lax.axis_index("c")
    return x, y, c


def _flip(v, bit):
    return 1 - v if bit else v


N_PEERS = N_DEV - 1


def _all_gather(name, blks):
    side = _TwoLevelGather(blks)

    def body(*refs):
        _, side_in, _, side_out, _, sems = _split_side(refs, 0, 0, 0, side)
        side.start(side_in, side_out, sems)
        side.middle(side_in, side_out, sems)
        side.finish(side_in, side_out, sems)

    return pl.pallas_call(
        body, name=name, out_shape=side.out_shape,
        in_specs=[_ANY] * side.n, out_specs=[_ANY] * side.n, scratch_shapes=side.scratch,
    )(*blks)


class _Direct:
    def __init__(self, kind, arrays):
        self.kind, self.arrays, self.n = kind, list(arrays), len(arrays)
        if kind == "scatter":
            self.out_shape = [jax.ShapeDtypeStruct(a.shape, a.dtype) for a in arrays]
        else:
            self.out_shape = [jax.ShapeDtypeStruct((N_DEV,) + a.shape, a.dtype) for a in arrays]
        self.scratch = [pltpu.SemaphoreType.DMA((N_PEERS * self.n,)), pltpu.SemaphoreType.DMA((N_PEERS * self.n,)),
                        pltpu.SemaphoreType.DMA((self.n,))]

    def _copies(self, in_refs, out_refs, sems):
        send_sems, recv_sems, local_sems = sems
        x, y, c = _my_place()
        me = 4 * x + 2 * y + c
        mine, sends, recvs = [], [], []
        for t in range(self.n):
            src_of = (lambda idx, t=t: in_refs[t].at[idx]) if self.kind == "scatter" else (lambda idx, t=t: in_refs[t])
            mine.append(pltpu.make_async_copy(src_of(me), out_refs[t].at[me], local_sems.at[t]))
            for m in range(1, N_DEV):
                px, py, pc = _flip(x, m & 4), _flip(y, m & 2), _flip(c, m & 1)
                pidx = 4 * px + 2 * py + pc
                sem = dict(send_sem=send_sems.at[N_PEERS * t + m - 1], recv_sem=recv_sems.at[N_PEERS * t + m - 1], device_id_type=MESH_T)
                sends.append(pltpu.make_async_remote_copy(src_ref=src_of(pidx), dst_ref=out_refs[t].at[me], device_id=(px, py, pc), **sem))
                recvs.append(pltpu.make_async_remote_copy(src_ref=src_of(pidx), dst_ref=out_refs[t].at[pidx], device_id=(x, y, c), **sem))
        return mine, sends, recvs

    def start(self, in_refs, out_refs, sems):
        mine, sends, _ = self._copies(in_refs, out_refs, sems)
        for cp in mine + sends:
            cp.start()

    def middle(self, in_refs, out_refs, sems):
        pass

    def finish(self, in_refs, out_refs, sems):
        mine, sends, recvs = self._copies(in_refs, out_refs, sems)
        for cp in recvs:
            cp.wait_recv()
        for cp in sends:
            cp.wait_send()
        for cp in mine:
            cp.wait()


class _TwoLevelGather:
    def __init__(self, arrays):
        self.arrays, self.n = list(arrays), len(arrays)
        self.out_shape = [jax.ShapeDtypeStruct((N_DEV,) + a.shape, a.dtype) for a in arrays]
        self.scratch = [pltpu.SemaphoreType.DMA((N_PEERS * self.n,)), pltpu.SemaphoreType.DMA((N_PEERS * self.n,)),
                        pltpu.SemaphoreType.DMA((self.n,))]

    def _place(self):
        x, y, c = _my_place()
        return (x, y, c), (x, y, 1 - c), [(1 - x, y), (x, 1 - y), (1 - x, 1 - y)]

    def _copy(self, in_refs, out_refs, sems, t, k, block, to, own=False):
        slab = out_refs[t].at[4 * block[0] + 2 * block[1] + block[2]]
        return pltpu.make_async_remote_copy(
            src_ref=in_refs[t] if own else slab, dst_ref=slab, send_sem=sems[0].at[N_PEERS * t + k],
            recv_sem=sems[1].at[N_PEERS * t + k], device_id=to, device_id_type=MESH_T)

    def _mine(self, in_refs, out_refs, sems):
        x, y, c = _my_place()
        return [pltpu.make_async_copy(in_refs[t], out_refs[t].at[4 * x + 2 * y + c], sems[2].at[t]) for t in range(self.n)]

    def _first(self, in_refs, out_refs, sems):
        me, sibling, chips = self._place()
        cps = []
        for t in range(self.n):
            cps.append(self._copy(in_refs, out_refs, sems, t, 0, me, sibling, own=True))
            cps += [self._copy(in_refs, out_refs, sems, t, 1 + j, me, (*chip, me[2]), own=True) for j, chip in enumerate(chips)]
        return cps

    def _passed(self, in_refs, out_refs, sems):
        me, sibling, chips = self._place()
        return [self._copy(in_refs, out_refs, sems, t, 4 + j, (*chip, me[2]), sibling) for j, chip in enumerate(chips) for t in range(self.n)]

    def start(self, in_refs, out_refs, sems):
        for cp in self._mine(in_refs, out_refs, sems) + self._first(in_refs, out_refs, sems):
            cp.start()

    def middle(self, in_refs, out_refs, sems):
        me, _, chips = self._place()
        for j, chip in enumerate(chips):
            for t in range(self.n):
                self._copy(in_refs, out_refs, sems, t, 1 + j, (*chip, me[2]), me).wait_recv()
        for cp in self._passed(in_refs, out_refs, sems):
            cp.start()

    def finish(self, in_refs, out_refs, sems):
        me, sibling, chips = self._place()
        for t in range(self.n):
            self._copy(in_refs, out_refs, sems, t, 0, sibling, me).wait_recv()
            for j, chip in enumerate(chips):
                self._copy(in_refs, out_refs, sems, t, 4 + j, (*chip, 1 - me[2]), me).wait_recv()
        for cp in self._first(in_refs, out_refs, sems) + self._passed(in_refs, out_refs, sems):
            cp.wait_send()
        for cp in self._mine(in_refs, out_refs, sems):
            cp.wait()


def _split_side(refs, n_main_in, n_main_out, n_main_scratch, side):
    ns = side.n if side is not None else 0
    i = 0
    main_in = refs[i:i + n_main_in]; i += n_main_in
    side_in = refs[i:i + ns]; i += ns
    main_out = refs[i:i + n_main_out]; i += n_main_out
    side_out = refs[i:i + ns]; i += ns
    main_scr = refs[i:i + n_main_scratch]; i += n_main_scratch
    return main_in, side_in, main_out, side_out, main_scr, refs[i:]


def _all_to_all(name, gs):
    side = _Direct("scatter", gs)

    def body(*refs):
        _, side_in, _, side_out, _, sems = _split_side(refs, 0, 0, 0, side)
        side.start(side_in, side_out, sems)
        side.finish(side_in, side_out, sems)

    return pl.pallas_call(
        body, name=name, out_shape=side.out_shape,
        in_specs=[_ANY] * side.n, out_specs=[_ANY] * side.n, scratch_shapes=side.scratch,
    )(*gs)


def _unblock(name, blocks, layer, n_out):
    _, _, K, nb = blocks.shape
    tk = _tile(K, 256, 16)

    def body(b_ref, o_ref):
        for j in range(N_DEV):
            o_ref[:, j * nb:(j + 1) * nb] = b_ref[j]
        if n_out > N_DEV * nb:
            o_ref[:, N_DEV * nb:n_out] = jnp.zeros((tk, n_out - N_DEV * nb), o_ref.dtype)

    return pl.pallas_call(
        body, name=name, grid=(K // tk,),
        in_specs=[pl.BlockSpec((N_DEV, None, tk, nb), lambda i: (0, layer, i, 0))],
        out_specs=pl.BlockSpec((tk, n_out), lambda i: (i, 0)),
        out_shape=jax.ShapeDtypeStruct((K, n_out), blocks.dtype),
        compiler_params=_cp(("parallel",)),
    )(blocks)


def _block(name, full, nb, dtype):
    K, N = full.shape
    tk = _tile(K, 256, 16)

    def body(f_ref, o_ref):
        for j in range(N_DEV):
            o_ref[j] = f_ref[:, j * nb:(j + 1) * nb].astype(o_ref.dtype)

    return pl.pallas_call(
        body, name=name, grid=(K // tk,),
        in_specs=[pl.BlockSpec((tk, N), lambda i: (i, 0))],
        out_specs=pl.BlockSpec((N_DEV, tk, nb), lambda i: (0, i, 0)),
        out_shape=jax.ShapeDtypeStruct((N_DEV, K, nb), dtype),
        compiler_params=_cp(("parallel",)),
    )(full)


ADAM_BLOCK_ELEMS = 256 * 1024
GRAD_WIRE = jnp.bfloat16


def _adamw(name, parts, w, m, v, layer=0):
    shape = w.shape[1:]
    cols = shape[-1]
    rows = math.prod(shape[:-1])
    nl = w.shape[0]
    parts, w, m, v = parts.reshape(N_DEV, rows, cols), w.reshape(nl, rows, cols), m.reshape(nl, rows, cols), v.reshape(nl, rows, cols)
    tr = _tile(rows, max(16, ADAM_BLOCK_ELEMS // cols), 16)

    def body(p_ref, w_ref, m_ref, v_ref, g_ref, d_ref, mo_ref, vo_ref):
        g = p_ref[0].astype(F32)
        for s in range(1, N_DEV):
            g = g + p_ref[s].astype(F32)
        mn = ADAM_B1 * m_ref[...] + (1.0 - ADAM_B1) * g
        vn = ADAM_B2 * v_ref[...] + (1.0 - ADAM_B2) * jnp.square(g)
        m_hat = mn / (1.0 - ADAM_B1 ** ADAM_STEP)
        v_hat = vn / (1.0 - ADAM_B2 ** ADAM_STEP)
        g_ref[...] = g
        d_ref[...] = -ADAM_LR * (m_hat / (jnp.sqrt(v_hat) + ADAM_EPS) + ADAM_WD * w_ref[...])
        mo_ref[...] = mn
        vo_ref[...] = vn

    blk = pl.BlockSpec((tr, cols), lambda i: (i, 0))
    lblk = pl.BlockSpec((None, tr, cols), lambda i: (layer, i, 0))
    shp = jax.ShapeDtypeStruct((rows, cols), F32)
    res = pl.pallas_call(
        body, name=name, grid=(rows // tr,),
        in_specs=[pl.BlockSpec((N_DEV, tr, cols), lambda i: (0, i, 0)), lblk, lblk, lblk],
        out_specs=[blk, blk, blk, blk], out_shape=[shp, shp, shp, shp],
        compiler_params=_cp(("parallel",)),
    )(parts, w, m, v)
    return [r.reshape(shape) for r in res]


SHARDED = dict(xa_wq=1, xa_wkv=1, xa_wo=2, w_gate=2, w_up=2, w_down=1, e_w_in=2, e_w_out=1, o_w_in=2, o_s_pool=1, o_conv_w=2, o_w_out=1)
LAYER_SPLIT = ("xa_wq", "xa_wkv", "xa_wo", "w_gate", "w_up", "w_down")
ELEMENTWISE_SHARDED = ("o_s_pool", "o_conv_w")
REPLICATED = ("g_mix", "g_xa", "g_mem", "xa_gq", "xa_gk", "g_ffn", "e_b_f", "e_g_v", "e_w_s", "e_b_s", "e_g_qn", "e_g_kn", "o_w_pool")
WEIGHTS = ("g_mix", "g_xa", "g_mem", "xa_wq", "xa_wkv", "xa_wo", "xa_gq", "xa_gk", "g_ffn", "w_gate", "w_up", "w_down", "e_w_in", "e_b_f",
           "e_g_v", "e_w_s", "e_b_s", "e_g_qn", "e_g_kn", "e_w_out", "o_w_in", "o_w_pool", "o_s_pool", "o_conv_w", "o_w_out")


def _rows_for(n, mult):
    return -(-n // (PACK_W * mult)) * mult


def _pack(arrs, rows, dtype):
    flat = jnp.concatenate([a.reshape(-1).astype(dtype) for a in arrs])
    return jnp.pad(flat, (0, rows * PACK_W - flat.shape[0])).reshape(rows, PACK_W)


def _unpack(slab, shapes):
    flat, out, off = slab.reshape(-1), [], 0
    for s in shapes:
        n = math.prod(s)
        out.append(flat[off:off + n].reshape(s))
        off += n
    return out


def _to_blocks(full, axis):
    s = full.shape
    return jnp.moveaxis(full.reshape(s[:axis] + (N_DEV, s[axis] // N_DEV) + s[axis + 1:]), axis, 0)


def _from_blocks(blocks, axis):
    b = jnp.moveaxis(blocks, 0, axis)
    s = b.shape
    return b.reshape(s[:axis] + (s[axis] * s[axis + 1],) + s[axis + 2:])


def kernel(x, mem, g_mix, g_xa, g_mem, xa_wq, xa_wkv, xa_wo, xa_gq, xa_gk, g_ffn, w_gate, w_up, w_down, e_w_in, e_b_f, e_g_v, e_w_s, e_b_s, e_g_qn, e_g_kn, e_w_out, o_w_in, o_w_pool, o_s_pool, o_conv_w, o_w_out, loss_target, m_g_mix, m_g_xa, m_g_mem, m_xa_wq, m_xa_wkv, m_xa_wo, m_xa_gq, m_xa_gk, m_g_ffn, m_w_gate, m_w_up, m_w_down, m_e_w_in, m_e_b_f, m_e_g_v, m_e_w_s, m_e_b_s, m_e_g_qn, m_e_g_kn, m_e_w_out, m_o_w_in, m_o_w_pool, m_o_s_pool, m_o_conv_w, m_o_w_out, v_g_mix, v_g_xa, v_g_mem, v_xa_wq, v_xa_wkv, v_xa_wo, v_xa_gq, v_xa_gk, v_g_ffn, v_w_gate, v_w_up, v_w_down, v_e_w_in, v_e_b_f, v_e_g_v, v_e_w_s, v_e_b_s, v_e_g_qn, v_e_g_kn, v_e_w_out, v_o_w_in, v_o_w_pool, v_o_s_pool, v_o_conv_w, v_o_w_out):
    args = dict(locals())
    Wt = {n: args[n] for n in WEIGHTS}
    Mo = {n: args["m_" + n] for n in WEIGHTS}
    Vo = {n: args["v_" + n] for n in WEIGHTS}
    rep_names = REPLICATED
    rep_shapes = [Wt[n].shape for n in rep_names]
    rep_rows = _rows_for(sum(math.prod(s) for s in rep_shapes) + 1, 16)
    layer_names = [LAYER_SPLIT + ("e_w_in", "e_w_out"), LAYER_SPLIT + ("o_w_in", "o_w_out", "o_s_pool", "o_conv_w")]

    def shard_of(n, l, src=Wt):
        return src[n][l:l + 1] if n in LAYER_SPLIT else src[n]

    def to_wire(n, a):
        return a if n in ELEMENTWISE_SHARDED else _mx(a)

    def full_weight(n, l, gn):
        if SHARDED[n] == 1 or n in ELEMENTWISE_SHARDED:
            return _from_blocks(gn, SHARDED[n])[0]
        return _unblock(f"unblock_{n}{l}", gn, 0, EVEN_IN_PAD if n == "e_w_in" else N_DEV * gn.shape[-1])

    def grad_slabs(n, l, gl):
        if n in ELEMENTWISE_SHARDED:
            return _to_blocks(gl.reshape((1,) + Wt[n].shape[1:-1] + (-1,)), SHARDED[n])
        if SHARDED[n] == 1:
            return _to_blocks(gl[None], 1).astype(GRAD_WIRE)
        return _block(f"block_{n}{l}", gl, Wt[n].shape[-1], GRAD_WIRE)[:, None]

    xk, mem2 = x[0], mem[0]
    later = [(n, 0) for n in layer_names[0] if n != "e_w_in"] + [(n, 1) for n in layer_names[1]]
    w_in0 = full_weight("e_w_in", 0, _all_gather("gather_e_w_in", [to_wire("e_w_in", Wt["e_w_in"])])[0])
    side = _TwoLevelGather([to_wire(n, shard_of(n, l)) for n, l in later])
    P = {n: Wt[n] for n in rep_names}
    P["e_w_in"] = {0: w_in0}

    def on_side(gathered):
        for (n, l), gn in zip(later, gathered):
            P.setdefault(n, {})[l if n in LAYER_SPLIT else 0] = full_weight(n, l, gn)

    xk, sv0, W0 = _layer_fwd(0, xk, mem2, P, side=side, on_side=on_side)
    xk, sv1, W1 = _layer_fwd(1, xk, mem2, P)

    g, loss_row = _loss_grad("loss_grad", xk, loss_target[0])
    g, G1, _ = _layer_bwd(1, g, mem2, W1, sv1)
    slabs1 = [grad_slabs(n, 1, G1[n]) for n in layer_names[1]]
    g, G0, beside = _layer_bwd(0, g, mem2, W0, sv0,
                               side_fn=lambda G: _Direct("scatter", slabs1 + [grad_slabs(n, 0, G[n]) for n in LAYER_SPLIT]))
    parts1, parts0a = beside[:len(slabs1)], beside[len(slabs1):]
    grad_x = g[None]
    parts0b = _all_to_all("exchange_grads_mixer0", [grad_slabs(n, 0, G0[n]) for n in ("e_w_in", "e_w_out")])
    parts = {(n, 1): p for n, p in zip(layer_names[1], parts1)}
    parts.update({(n, 0): p for n, p in zip(LAYER_SPLIT, parts0a)})
    parts.update({(n, 0): p for n, p in zip(("e_w_in", "e_w_out"), parts0b)})
    G = {n: {0: G0[n]} for n in G0}
    for n in G1:
        G.setdefault(n, {})[1] = G1[n]
    rep_g = [jnp.stack([G[n][l] for l in sorted(G[n])]).reshape(Wt[n].shape) for n in rep_names]
    rep_parts = _all_gather("gather_small_grads", [_pack(rep_g + [jnp.sum(loss_row).reshape(1)], rep_rows, F32)])[0]

    outs = {}
    for n in SHARDED:
        per_layer = [_adamw(f"adamw_{n}{l}", parts[(n, l if n in LAYER_SPLIT else (0 if n.startswith("e_") else 1))], Wt[n], Mo[n], Vo[n], layer=l)
                     for l in range(Wt[n].shape[0])]
        for k, kind in enumerate(("grad", "delta", "new_m", "new_v")):
            outs[kind + "_" + n] = jnp.stack([r[k] for r in per_layer])
    res = _adamw("adamw_replicated", rep_parts, _pack([Wt[n] for n in rep_names], rep_rows, F32)[None],
                 _pack([Mo[n] for n in rep_names], rep_rows, F32)[None], _pack([Vo[n] for n in rep_names], rep_rows, F32)[None])
    for kind, slab in zip(("grad", "delta", "new_m", "new_v"), res):
        for n, a in zip(rep_names, _unpack(slab, rep_shapes)):
            outs[kind + "_" + n] = a
    loss = res[0].reshape(-1)[sum(math.prod(s) for s in rep_shapes)]
    return (loss, grad_x, *[outs[k + "_" + n] for k in ("grad", "delta", "new_m", "new_v") for n in WEIGHTS])
```

```python
import functools
import math

import numpy as np
import jax
import jax.numpy as jnp
from jax import lax
from jax.experimental import pallas as pl
from jax.experimental.pallas import tpu as pltpu

F32 = jnp.float32
BF16 = jnp.bfloat16
_MXU = jnp.bfloat16

D_MODEL = 1024
N_DEV = 8
LANES = 128
SUBLANES = 8
HALO = 16
CHUNK = 64
A_GROUPS, A_GROUP_DIM, A_WIDTH, A_BLOCK = 4, 128, 512, 128
FOX_HEADS, FOX_HEAD_DIM, FOX_WIDTH = 8, 64, 512
FOX_TILE = 512
POOL_WINDOWS = (2, 4, 8, 16)
POOL_WIDTH, CONV_WIDTH, CONV_K = 512, 512, 3
EVEN_IN, EVEN_IN_PAD, ODD_IN = 2568, 2688, 2048
XA_HEADS, XA_HEAD_DIM, XA_WIDTH = 4, 128, 512
D_FF = 2816
EPS = 1e-6
ADAM_LR, ADAM_B1, ADAM_B2, ADAM_EPS, ADAM_WD, ADAM_STEP = 0.001, 0.9, 0.999, 1e-08, 0.01, 10
VMEM_LIMIT = 48 * 1024 * 1024


def _cp(sem):
    return pltpu.CompilerParams(dimension_semantics=sem, vmem_limit_bytes=VMEM_LIMIT)


def _tile(n, cap, q):
    best = None
    for d in range(q, min(n, cap) + 1, q):
        if n % d == 0:
            best = d
    return best if best is not None else n


def _mx(a):
    return a.astype(_MXU)


def _dot(a, b):
    return jnp.dot(_mx(a), _mx(b), preferred_element_type=F32)


def _dot_nt(a, b):
    return lax.dot_general(_mx(a), _mx(b), (((1,), (1,)), ((), ())), preferred_element_type=F32)


def _dot_tn(a, b):
    return lax.dot_general(_mx(a), _mx(b), (((0,), (0,)), ((), ())), preferred_element_type=F32)


def _dot_f32(a, b):
    return jnp.dot(a, b, precision=lax.Precision.HIGHEST, preferred_element_type=F32)


def _lane(shape):
    return lax.broadcasted_iota(jnp.int32, shape, len(shape) - 1)


def _row(shape):
    return lax.broadcasted_iota(jnp.int32, shape, len(shape) - 2)


def _mm(name, pairs, mode, out_dtype=F32, residual=None, norm_bwd=None, tm_cap=512, tn_cap=1536, tk_cap=2048):
    a0, b0 = pairs[0]
    if mode == "nn":
        (M, K), N = a0.shape, b0.shape[1]
    elif mode == "nt":
        (M, K), N = a0.shape, b0.shape[0]
    else:
        (K, M), N = a0.shape, b0.shape[1]
    if mode == "tn":
        tm, tk = _tile(M, 1408, LANES), _tile(K, 512, 16)
    else:
        tm, tk = _tile(M, tm_cap, 16), _tile(K, tk_cap, LANES)
    tn = _tile(N, tn_cap, LANES)
    nk = K // tk
    npairs = len(pairs)
    dot = {"nn": _dot, "nt": _dot_nt, "tn": _dot_tn}[mode]

    if norm_bwd is not None:
        assert tn == N and residual is None and mode != "tn", "the norm backward needs whole rows"

    def body(*refs):
        ab = refs[:2 * npairs]
        res_ref = refs[2 * npairs] if residual is not None else None
        k = pl.program_id(2)
        first_row_tile = pl.program_id(1) == 0
        if norm_bwd is not None:
            x_ref, g_ref, r_ref = refs[2 * npairs:2 * npairs + 3]
            o_ref, dg_ref, acc = refs[-3], refs[-2], refs[-1]
        else:
            o_ref, acc = refs[-2], refs[-1]

        @pl.when(k == 0)
        def _():
            acc[...] = jnp.zeros_like(acc)

        for p in range(npairs):
            acc[...] += dot(ab[2 * p][...], ab[2 * p + 1][...])

        @pl.when(k == nk - 1)
        def _():
            out = acc[...]
            if res_ref is not None:
                out = out + res_ref[...]
            if norm_bwd is not None:
                xv = x_ref[...]
                r = _rms_rows(xv)
                xhat = xv * r
                dxhat = out * g_ref[...]

                @pl.when(first_row_tile)
                def _():
                    dg_ref[...] = jnp.zeros_like(dg_ref)

                dg_ref[...] += jnp.sum(out * xhat, axis=0, keepdims=True)
                out = r_ref[...] + r * (dxhat - xhat * jnp.mean(dxhat * xhat, axis=-1, keepdims=True))
            o_ref[...] = out.astype(o_ref.dtype)

    if mode == "nn":
        a_spec = pl.BlockSpec((tm, tk), lambda j, i, k: (i, k))
        b_spec = pl.BlockSpec((tk, tn), lambda j, i, k: (k, j))
    elif mode == "nt":
        a_spec = pl.BlockSpec((tm, tk), lambda j, i, k: (i, k))
        b_spec = pl.BlockSpec((tn, tk), lambda j, i, k: (j, k))
    else:
        a_spec = pl.BlockSpec((tk, tm), lambda j, i, k: (k, i))
        b_spec = pl.BlockSpec((tk, tn), lambda j, i, k: (k, j))
    o_spec = pl.BlockSpec((tm, tn), lambda j, i, k: (i, j))
    in_specs, args = [], []
    for a, b in pairs:
        in_specs += [a_spec, b_spec]
        args += [a, b]
    if residual is not None:
        in_specs.append(o_spec)
        args.append(residual)
    out_specs, out_shape, sem = o_spec, jax.ShapeDtypeStruct((M, N), out_dtype), ("parallel", "parallel", "arbitrary")
    if norm_bwd is not None:
        x, g, res = norm_bwd
        vec = pl.BlockSpec((1, tn), lambda j, i, k: (0, j))
        in_specs += [o_spec, vec, o_spec]
        args += [x, g.reshape(1, N), res]
        out_specs, out_shape = [o_spec, vec], [out_shape, jax.ShapeDtypeStruct((1, N), F32)]
        sem = ("arbitrary", "arbitrary", "arbitrary")
    return pl.pallas_call(
        body, name=name, grid=(N // tn, M // tm, nk),
        in_specs=in_specs, out_specs=out_specs, out_shape=out_shape,
        scratch_shapes=[pltpu.VMEM((tm, tn), F32)],
        compiler_params=_cp(sem),
    )(*args)


def _rms_rows(x):
    return lax.rsqrt(jnp.mean(x * x, axis=-1, keepdims=True) + EPS)


def _norm_fwd(name, x, g):
    T, Dm = x.shape
    tm = _tile(T, 512, 16)

    def body(x_ref, g_ref, o_ref):
        xv = x_ref[...]
        o_ref[...] = ((xv * _rms_rows(xv)) * g_ref[...]).astype(o_ref.dtype)

    return pl.pallas_call(
        body, name=name, grid=(T // tm,),
        in_specs=[pl.BlockSpec((tm, Dm), lambda i: (i, 0)), pl.BlockSpec((1, Dm), lambda i: (0, 0))],
        out_specs=pl.BlockSpec((tm, Dm), lambda i: (i, 0)),
        out_shape=jax.ShapeDtypeStruct((T, Dm), _MXU),
        compiler_params=_cp(("parallel",)),
    )(x, g.reshape(1, Dm))


def _gelu(x):
    return jax.nn.gelu(x)


def _gelu_grad(x):
    c0, c1 = math.sqrt(2.0 / math.pi), 0.044715
    t = jnp.tanh(c0 * (x + c1 * x * x * x))
    return 0.5 * (1.0 + t) + 0.5 * x * (1.0 - t * t) * c0 * (1.0 + 3.0 * c1 * x * x)


def _head64_rms(x):
    lo = _lane(x.shape) < FOX_HEAD_DIM
    xx = x * x
    sa = jnp.sum(jnp.where(lo, xx, 0.0), axis=-1, keepdims=True)
    sb = jnp.sum(jnp.where(lo, 0.0, xx), axis=-1, keepdims=True)
    inv = 1.0 / FOX_HEAD_DIM
    return jnp.where(lo, lax.rsqrt(sa * inv + EPS), lax.rsqrt(sb * inv + EPS))


def _head64_mean(x):
    lo = _lane(x.shape) < FOX_HEAD_DIM
    sa = jnp.sum(jnp.where(lo, x, 0.0), axis=-1, keepdims=True)
    sb = jnp.sum(jnp.where(lo, 0.0, x), axis=-1, keepdims=True)
    return jnp.where(lo, sa, sb) * (1.0 / FOX_HEAD_DIM)


def _masked_ws(ws_ref, g):
    w = ws_ref[g]
    keep = (_row(w.shape) // CHUNK) >= (_lane(w.shape) // CHUNK)
    return jnp.where(keep, w, 0.0)


def _even_pre(name, z, g_v, w_s, bs_cols, g_qn2, g_kn2, b_f_pad):
    T = z.shape[0]
    tm = _tile(T, 256, A_BLOCK)
    nb = tm // A_BLOCK

    def body(z_ref, gv_ref, ws_ref, bs_ref, gq_ref, gk_ref, bf_ref, ya_ref, q_ref, k_ref, v_ref, c_ref, ct_ref, carry):
        i = pl.program_id(0)

        @pl.when(i == 0)
        def _():
            carry[...] = jnp.zeros_like(carry)

        wm = [_mx(_masked_ws(ws_ref, g)) for g in range(A_GROUPS)]
        for g in range(A_GROUPS):
            sl = slice(g * A_GROUP_DIM, (g + 1) * A_GROUP_DIM)
            u = _gelu(z_ref[:, sl])
            v = _gelu(z_ref[:, A_WIDTH + g * A_GROUP_DIM:A_WIDTH + (g + 1) * A_GROUP_DIM])
            vn = _mx((v * _rms_rows(v)) * gv_ref[:, sl])
            for n in range(nb):
                rows = slice(n * A_BLOCK, (n + 1) * A_BLOCK)
                s = jnp.dot(wm[g], vn[rows], preferred_element_type=F32) + bs_ref[:, g:g + 1]
                ya_ref[rows, sl] = (u[rows] * s).astype(ya_ref.dtype)
        o = 2 * A_WIDTH
        for j in range(FOX_WIDTH // LANES):
            sl = slice(j * LANES, (j + 1) * LANES)
            q = z_ref[:, o + j * LANES:o + (j + 1) * LANES]
            q_ref[:, sl] = (((q * _head64_rms(q)) * gq_ref[...]) * (1.0 / math.sqrt(FOX_HEAD_DIM))).astype(q_ref.dtype)
            k = z_ref[:, o + FOX_WIDTH + j * LANES:o + FOX_WIDTH + (j + 1) * LANES]
            k_ref[:, sl] = ((k * _head64_rms(k)) * gk_ref[...]).astype(k_ref.dtype)
            v_ref[:, sl] = z_ref[:, o + 2 * FOX_WIDTH + j * LANES:o + 2 * FOX_WIDTH + (j + 1) * LANES].astype(v_ref.dtype)
        zf = z_ref[:, o + 3 * FOX_WIDTH:o + 3 * FOX_WIDTH + LANES]
        logf = jnp.where(_lane(zf.shape) < FOX_HEADS, jax.nn.log_sigmoid(zf + bf_ref[...]), 0.0)
        tri = (_row((tm, tm)) >= _lane((tm, tm))).astype(F32)
        c = _dot_f32(tri, logf) + carry[0:1, :]
        c_ref[...] = c
        ct_ref[...] = c.T[0:SUBLANES, :]
        carry[0:1, :] = c[tm - 1:tm, :]

    wide = lambda n: pl.BlockSpec((tm, n), lambda i: (i, 0))
    full = lambda s: pl.BlockSpec(s, lambda i: tuple(0 for _ in s))
    out512 = jax.ShapeDtypeStruct((T, 512), _MXU)
    return pl.pallas_call(
        body, name=name, grid=(T // tm,),
        in_specs=[wide(EVEN_IN_PAD), full((1, A_WIDTH)), full((A_GROUPS, A_BLOCK, A_BLOCK)), full((A_BLOCK, LANES)),
                  full((1, LANES)), full((1, LANES)), full((1, LANES))],
        out_specs=[wide(512), wide(512), wide(512), wide(512), wide(LANES), pl.BlockSpec((SUBLANES, tm), lambda i: (0, i))],
        out_shape=[out512, out512, out512, out512, jax.ShapeDtypeStruct((T, LANES), F32), jax.ShapeDtypeStruct((SUBLANES, T), F32)],
        scratch_shapes=[pltpu.VMEM((SUBLANES, LANES), F32)],
        compiler_params=_cp(("arbitrary",)),
    )(z, g_v, w_s, bs_cols, g_qn2, g_kn2, b_f_pad)


def _tri_steps(n, by_rows):
    if by_rows:
        pairs = [(q, k) for q in range(n) for k in range(q + 1)]
    else:
        pairs = [(q, k) for k in range(n) for q in range(k, n)]
    return (jnp.asarray(np.array([p[0] for p in pairs], np.int32)), jnp.asarray(np.array([p[1] for p in pairs], np.int32)))


def _fox_logits(q, k, c_col, c_row, diag, tq):
    s = _dot_nt(q, k) + (c_col - c_row)
    if diag:
        s = jnp.where(_row((tq, tq)) >= _lane((tq, tq)), s, -jnp.inf)
    return s


FOX_SKIP_BELOW = 104.0


def _fox_skip_flags(c, g_qn2, g_kn2):
    T = c.shape[0]
    tq = _tile(T, FOX_TILE, LANES)
    nblk = T // tq
    cb = c.reshape(nblk, tq, LANES)
    c_first, c_last = cb[:, 0, :FOX_HEADS], cb[:, tq - 1, :FOX_HEADS]
    bound = 17.0 * jnp.max(jnp.abs(g_qn2)) * jnp.max(jnp.abs(g_kn2))
    keep = bound + c_first[:, None, :] - c_last[None, :, :] > -FOX_SKIP_BELOW
    return keep.astype(F32).reshape(-1)


def _on_diag_or_not(qi, ki, step):
    @pl.when(qi == ki)
    def _():
        step(True)

    @pl.when(qi != ki)
    def _():
        step(False)


def _side_specs(side):
    if side is None:
        return [], [], [], []
    return [pl.BlockSpec(memory_space=pl.ANY)] * side.n, list(side.arrays), list(side.out_shape), list(side.scratch)


def _side_run(side, s_id, nsteps, side_in, side_out, sems):
    if side is None:
        return

    @pl.when(s_id == 0)
    def _():
        side.start(side_in, side_out, sems)

    @pl.when(s_id == nsteps // 2)
    def _():
        side.middle(side_in, side_out, sems)

    @pl.when(s_id == nsteps - 1)
    def _():
        side.finish(side_in, side_out, sems)


def _fox_fwd(name, qn, kn, vv, c, ct, flags, side=None):
    T = qn.shape[0]
    tq = _tile(T, FOX_TILE, LANES)
    nblk = T // tq
    qs, ks = _tri_steps(nblk, True)
    nsteps = int(qs.shape[0])
    npair = FOX_WIDTH // LANES
    s_specs, s_args, s_shapes, s_scratch = _side_specs(side)

    def body(qs_ref, ks_ref, *refs):
        main_in, side_in, main_out, side_out, main_scr, sems = _split_side(refs, 6, 3, 3, side)
        q_ref, k_ref, v_ref, c_ref, ct_ref, flag_ref = main_in
        yb_ref, yf_ref, lse_ref = main_out
        acc, m_s, l_s = main_scr
        s_id = pl.program_id(0)
        qi, ki = qs_ref[s_id], ks_ref[s_id]
        _side_run(side, s_id, nsteps, side_in, side_out, sems)

        @pl.when(ki == 0)
        def _():
            acc[...] = jnp.zeros_like(acc)
            m_s[...] = jnp.full_like(m_s, -jnp.inf)
            l_s[...] = jnp.zeros_like(l_s)

        lo = _lane((tq, LANES)) < FOX_HEAD_DIM

        def pairs_from(first, diag):
            for j in range(first, npair):
                sl = slice(j * LANES, (j + 1) * LANES)
                q2, k2, v2 = q_ref[:, sl], k_ref[:, sl], v_ref[:, sl]
                outs = []
                for e in range(2):
                    h = 2 * j + e
                    qh = jnp.where(lo if e == 0 else ~lo, q2, jnp.zeros_like(q2))
                    s = _fox_logits(qh, k2, c_ref[:, h:h + 1], ct_ref[h:h + 1, :], diag, tq)
                    m_old = m_s[h]
                    m_new = jnp.maximum(m_old, jnp.max(s, axis=-1, keepdims=True))
                    alpha = jnp.exp(m_old - m_new)
                    p = jnp.exp(s - m_new)
                    l_s[h] = alpha * l_s[h] + jnp.sum(p, axis=-1, keepdims=True)
                    m_s[h] = m_new
                    outs.append((alpha, _dot(p, v2)))
                alpha2 = jnp.where(lo, outs[0][0], outs[1][0])
                acc[:, sl] = alpha2 * acc[:, sl] + jnp.where(lo, outs[0][1], outs[1][1])

        def step(diag):
            if diag:
                pairs_from(0, True)
                return
            f = (qi * nblk + ki) * FOX_HEADS
            need = [(flag_ref[f + 2 * j] > 0.5) | (flag_ref[f + 2 * j + 1] > 0.5) for j in range(npair - 1)]
            first = jnp.int32(npair - 1)
            for j in reversed(range(npair - 1)):
                first = jnp.where(need[j], j, first)
            for j in range(npair):
                pl.when(first == j)(functools.partial(pairs_from, j, False))

        _on_diag_or_not(qi, ki, step)

        @pl.when(ki == qi)
        def _():
            lse = jnp.zeros((tq, LANES), F32)
            for j in range(npair):
                sl = slice(j * LANES, (j + 1) * LANES)
                inv = jnp.where(lo, 1.0 / l_s[2 * j], 1.0 / l_s[2 * j + 1])
                y = acc[:, sl] * inv
                yf_ref[:, sl] = y
                yb_ref[:, sl] = y.astype(yb_ref.dtype)
            for h in range(FOX_HEADS):
                lse = jnp.where(_lane((tq, LANES)) == h, m_s[h] + jnp.log(l_s[h]), lse)
            lse_ref[...] = lse

    qmap = lambda s, qs_r, ks_r: (qs_r[s], 0)
    kmap = lambda s, qs_r, ks_r: (ks_r[s], 0)
    grid_spec = pltpu.PrefetchScalarGridSpec(
        num_scalar_prefetch=2, grid=(nsteps,),
        in_specs=[pl.BlockSpec((tq, 512), qmap), pl.BlockSpec((tq, 512), kmap), pl.BlockSpec((tq, 512), kmap),
                  pl.BlockSpec((tq, LANES), qmap),
                  pl.BlockSpec((SUBLANES, tq), lambda s, qs_r, ks_r: (0, ks_r[s])), pl.BlockSpec(memory_space=pltpu.SMEM)] + s_specs,
        out_specs=[pl.BlockSpec((tq, 512), qmap), pl.BlockSpec((tq, 512), qmap), pl.BlockSpec((tq, LANES), qmap)] + s_specs,
        scratch_shapes=[pltpu.VMEM((tq, 512), F32), pltpu.VMEM((FOX_HEADS, tq, 1), F32), pltpu.VMEM((FOX_HEADS, tq, 1), F32)] + s_scratch,
    )
    res = pl.pallas_call(
        body, name=name, grid_spec=grid_spec,
        out_shape=[jax.ShapeDtypeStruct((T, 512), _MXU), jax.ShapeDtypeStruct((T, 512), F32), jax.ShapeDtypeStruct((T, LANES), F32)] + s_shapes,
        compiler_params=_cp(("arbitrary",)),
    )(qs, ks, qn, kn, vv, c, ct, flags, *s_args)
    return res[:3], res[3:]


def _shift_down(ext, k, tm):
    return pltpu.roll(ext, k, axis=0)[HALO:HALO + tm]


def _pool_p(z_ext, g, tm, pos):
    w = POOL_WINDOWS[g]
    s, span = z_ext, 1
    while span < w:
        s = s + pltpu.roll(s, span, axis=0)
        span *= 2
    cnt = jnp.minimum(pos + 1, w).astype(F32)
    return s[HALO:HALO + tm] / cnt - z_ext[HALO:HALO + tm]


def _odd_fwd(name, z, w_pool, s_pool, conv_w8):
    T = z.shape[0]
    tm = _tile(T, 256, HALO)
    r = tm // HALO

    def body(z_ref, zb_ref, wp_ref, sp_ref, cw_ref, y_ref):
        i = pl.program_id(0)
        first = i == 0
        pos = i * tm + _row((tm, LANES))
        for g in range(len(POOL_WINDOWS)):
            sl = slice(g * LANES, (g + 1) * LANES)
            halo = jnp.where(first, 0.0, zb_ref[:, sl])
            z_ext = jnp.concatenate([halo, z_ref[:, sl]], axis=0)
            p = _pool_p(z_ext, g, tm, pos)
            y_ref[:, sl] = (_dot(p, wp_ref[g]) * sp_ref[:, sl]).astype(y_ref.dtype)
        for j in range(CONV_WIDTH // LANES):
            sl = slice(j * LANES, (j + 1) * LANES)
            hd = slice(POOL_WIDTH + j * LANES, POOL_WIDTH + (j + 1) * LANES)
            gb = slice(POOL_WIDTH + CONV_WIDTH + j * LANES, POOL_WIDTH + CONV_WIDTH + (j + 1) * LANES)
            gc = slice(POOL_WIDTH + 2 * CONV_WIDTH + j * LANES, POOL_WIDTH + 2 * CONV_WIDTH + (j + 1) * LANES)
            xg_b = jnp.where(first, 0.0, zb_ref[:, gc] * zb_ref[:, hd])
            xg = jnp.concatenate([xg_b, z_ref[:, gc] * z_ref[:, hd]], axis=0)
            conv = (cw_ref[0:1, sl] * _shift_down(xg, 2, tm) + cw_ref[1:2, sl] * _shift_down(xg, 1, tm)
                    + cw_ref[2:3, sl] * xg[HALO:HALO + tm])
            y_ref[:, POOL_WIDTH + j * LANES:POOL_WIDTH + (j + 1) * LANES] = (z_ref[:, gb] * conv).astype(y_ref.dtype)

    full = lambda s: pl.BlockSpec(s, lambda i: tuple(0 for _ in s))
    return pl.pallas_call(
        body, name=name, grid=(T // tm,),
        in_specs=[pl.BlockSpec((tm, ODD_IN), lambda i: (i, 0)),
                  pl.BlockSpec((HALO, ODD_IN), lambda i: (jnp.maximum(i * r - 1, 0), 0)),
                  full((4, LANES, LANES)), full((1, POOL_WIDTH)), full((SUBLANES, CONV_WIDTH))],
        out_specs=pl.BlockSpec((tm, 1024), lambda i: (i, 0)),
        out_shape=jax.ShapeDtypeStruct((T, 1024), _MXU),
        compiler_params=_cp(("parallel",)),
    )(z, z, w_pool, s_pool, conv_w8)


def _mem_fwd(name, mem, g_mem, wkv, gk):
    M = mem.shape[0]

    def body(mem_ref, g_ref, w_ref, gk_ref, m_ref, kv_ref, kn_ref, v_ref):
        x = mem_ref[...]
        m = ((x * _rms_rows(x)) * g_ref[...]).astype(m_ref.dtype)
        m_ref[...] = m
        kv = _dot(m, w_ref[...])
        kv_ref[...] = kv
        for h in range(XA_HEADS):
            sl = slice(h * LANES, (h + 1) * LANES)
            k = kv[:, sl]
            kn_ref[:, sl] = ((k * _rms_rows(k)) * gk_ref[...]).astype(kn_ref.dtype)
        v_ref[...] = kv[:, XA_WIDTH:].astype(v_ref.dtype)

    return pl.pallas_call(
        body, name=name,
        out_shape=[jax.ShapeDtypeStruct((M, D_MODEL), _MXU), jax.ShapeDtypeStruct((M, 2 * XA_WIDTH), F32),
                   jax.ShapeDtypeStruct((M, XA_WIDTH), _MXU), jax.ShapeDtypeStruct((M, XA_WIDTH), _MXU)],
        compiler_params=pltpu.CompilerParams(vmem_limit_bytes=VMEM_LIMIT),
    )(mem, g_mem.reshape(1, D_MODEL), wkv, gk.reshape(1, XA_HEAD_DIM))


def _xa_probs(qx, gq, kn_h):
    r = _rms_rows(qx)
    qhat = qx * r
    qn = qhat * gq
    s = _dot_nt(qn, kn_h) * (1.0 / math.sqrt(XA_HEAD_DIM))
    s = s - jnp.max(s, axis=-1, keepdims=True)
    e = jnp.exp(s)
    return qhat, r, qn, e / jnp.sum(e, axis=-1, keepdims=True)


def _xa_fwd(name, qx, gq, kn, vx):
    T, M = qx.shape[0], kn.shape[0]
    tm = _tile(T, 512, 16)

    def body(q_ref, gq_ref, k_ref, v_ref, o_ref):
        for h in range(XA_HEADS):
            sl = slice(h * LANES, (h + 1) * LANES)
            _, _, _, p = _xa_probs(q_ref[:, sl], gq_ref[...], k_ref[:, sl])
            o_ref[:, sl] = _dot(p, v_ref[:, sl]).astype(o_ref.dtype)

    full = lambda s: pl.BlockSpec(s, lambda i: tuple(0 for _ in s))
    return pl.pallas_call(
        body, name=name, grid=(T // tm,),
        in_specs=[pl.BlockSpec((tm, XA_WIDTH), lambda i: (i, 0)), full((1, LANES)), full((M, XA_WIDTH)), full((M, XA_WIDTH))],
        out_specs=pl.BlockSpec((tm, XA_WIDTH), lambda i: (i, 0)),
        out_shape=jax.ShapeDtypeStruct((T, XA_WIDTH), _MXU),
        compiler_params=_cp(("parallel",)),
    )(qx, gq.reshape(1, XA_HEAD_DIM), kn, vx)


def _ffn_up(name, h, wg, wu):
    T, K = h.shape
    N = wg.shape[1]
    tm, tn = _tile(T, 512, 16), _tile(N, 1536, LANES)

    def body(h_ref, wg_ref, wu_ref, a_ref, b_ref, s_ref):
        hv = h_ref[...]
        a, b = _dot(hv, wg_ref[...]), _dot(hv, wu_ref[...])
        a_ref[...] = a.astype(a_ref.dtype)
        b_ref[...] = b.astype(b_ref.dtype)
        s_ref[...] = (jax.nn.silu(a) * b).astype(s_ref.dtype)

    w_spec = pl.BlockSpec((K, tn), lambda j, i: (0, j))
    o_spec = pl.BlockSpec((tm, tn), lambda j, i: (i, j))
    return pl.pallas_call(
        body, name=name, grid=(N // tn, T // tm),
        in_specs=[pl.BlockSpec((tm, K), lambda j, i: (i, 0)), w_spec, w_spec],
        out_specs=[o_spec, o_spec, o_spec],
        out_shape=[jax.ShapeDtypeStruct((T, N), _MXU), jax.ShapeDtypeStruct((T, N), _MXU), jax.ShapeDtypeStruct((T, N), _MXU)],
        compiler_params=_cp(("parallel", "parallel")),
    )(h, wg, wu)


def _loss_grad(name, y, target):
    T, Dm = y.shape
    tm = _tile(T, 512, 8)

    def body(y_ref, t_ref, dy_ref, l_ref):
        e = y_ref[...] - t_ref[...]
        dy_ref[...] = e * (1.0 / Dm)

        @pl.when(pl.program_id(0) == 0)
        def _():
            l_ref[...] = jnp.zeros_like(l_ref)

        l_ref[...] += jnp.sum(e * e, axis=0, keepdims=True) * (0.5 / Dm)

    blk = pl.BlockSpec((tm, Dm), lambda i: (i, 0))
    return pl.pallas_call(
        body, name=name, grid=(T // tm,),
        in_specs=[blk, blk], out_specs=[blk, pl.BlockSpec((1, Dm), lambda i: (0, 0))],
        out_shape=[jax.ShapeDtypeStruct((T, Dm), F32), jax.ShapeDtypeStruct((1, Dm), F32)],
        compiler_params=_cp(("arbitrary",)),
    )(y, target)


def _pad_lanes(v, n=LANES):
    v = v.reshape(1, -1)
    return jnp.pad(v, ((0, 0), (0, n - v.shape[1])))


def _layer_fwd(l, x, mem, P, side=None, on_side=None):
    sv = {"x0": x}
    W = _layer_weights(l, P, "in")
    h1 = _norm_fwd(f"norm_mix{l}", x, W["g_mix"])
    sv["h1"] = h1
    if l % 2 == 0:
        z = _mm(f"e_in{l}", [(h1, W["w_in"])], "nn")
        ya, qn, kn, vv, c, ct = _even_pre(f"even_pre{l}", z, W["g_v"], W["w_s"], W["bs_cols"], W["g_qn2"], W["g_kn2"], W["b_f_pad"])
        flags = _fox_skip_flags(c, W["g_qn2"], W["g_kn2"])
        (yb, ybf, lse), side_out = _fox_fwd(f"fox_fwd{l}", qn, kn, vv, c, ct, flags, side=side)
        if on_side is not None:
            on_side(side_out)
        W.update(_layer_weights(l, P, "rest"))
        sv.update(z=z, ya=ya, qn=qn, kn=kn, vv=vv, c=c, ct=ct, yb=yb, ybf=ybf, lse=lse, flags=flags)
        x1 = _mm(f"e_out{l}", [(ya, W["w_out"][:A_WIDTH]), (yb, W["w_out"][A_WIDTH:])], "nn", residual=x)
    else:
        W.update(_layer_weights(l, P, "rest"))
        z = _mm(f"o_in{l}", [(h1, W["w_in"])], "nn")
        ycat = _odd_fwd(f"odd_fwd{l}", z, W["w_pool"], W["s_pool"], W["conv_w8"])
        sv.update(z=z, ycat=ycat)
        x1 = _mm(f"o_out{l}", [(ycat, W["w_out"])], "nn", residual=x)
    sv["x1"] = x1
    h2 = _norm_fwd(f"norm_xa{l}", x1, W["g_xa"])
    qx = _mm(f"xa_q{l}", [(h2, W["wq"])], "nn")
    m, kv, kn_x, vx = _mem_fwd(f"mem_fwd{l}", mem, W["g_mem"], W["wkv"], W["gk"])
    o = _xa_fwd(f"xa_fwd{l}", qx, W["gq"], kn_x, vx)
    x2 = _mm(f"xa_o{l}", [(o, W["wo"])], "nn", residual=x1)
    sv.update(h2=h2, qx=qx, m=m, kv=kv, kn_x=kn_x, vx=vx, o=o, x2=x2)
    h3 = _norm_fwd(f"norm_ffn{l}", x2, W["g_ffn"])
    a, b, s = _ffn_up(f"ffn_up{l}", h3, W["w_gate"], W["w_up"])
    x3 = _mm(f"ffn_down{l}", [(s, W["w_down"])], "nn", residual=x2)
    sv.update(h3=h3, a=a, b=b, s=s)
    return x3, sv, W


def _layer_weights(l, P, part):
    i = l // 2
    if part == "in":
        W = {"g_mix": P["g_mix"][l]}
        if l % 2 == 0:
            w_in = _mx(P["e_w_in"][i])
            if w_in.shape[1] < EVEN_IN_PAD:
                w_in = jnp.pad(w_in, ((0, 0), (0, EVEN_IN_PAD - w_in.shape[1])))
            W.update(w_in=w_in, g_v=P["e_g_v"][i].reshape(1, A_WIDTH), w_s=P["e_w_s"][i],
                     bs_cols=jnp.pad(P["e_b_s"][i].T, ((0, 0), (0, LANES - A_GROUPS))),
                     g_qn2=jnp.tile(P["e_g_qn"][i], 2).reshape(1, LANES), g_kn2=jnp.tile(P["e_g_kn"][i], 2).reshape(1, LANES),
                     b_f_pad=_pad_lanes(P["e_b_f"][i]))
        return W
    W = {k: P[k][l] for k in ("g_xa", "g_mem", "g_ffn")}
    W.update(wq=_mx(P["xa_wq"][l]), wkv=_mx(P["xa_wkv"][l]), wo=_mx(P["xa_wo"][l]), gq=P["xa_gq"][l], gk=P["xa_gk"][l],
             w_gate=_mx(P["w_gate"][l]), w_up=_mx(P["w_up"][l]), w_down=_mx(P["w_down"][l]))
    if l % 2 == 0:
        W.update(w_out=_mx(P["e_w_out"][i]))
    else:
        W.update(w_in=_mx(P["o_w_in"][i]), w_pool=_mx(P["o_w_pool"][i]), s_pool=P["o_s_pool"][i].reshape(1, POOL_WIDTH),
                 conv_w8=jnp.pad(P["o_conv_w"][i], ((0, SUBLANES - CONV_K), (0, 0))), w_out=_mx(P["o_w_out"][i]))
    return W


def _ffn_bwd_act(name, g, wd, a, b):
    T, N = a.shape
    K = g.shape[1]
    tm, tn = _tile(T, 512, 16), _tile(N, 1536, LANES)

    def body(g_ref, wd_ref, a_ref, b_ref, da_ref, db_ref):
        ds = _dot_nt(g_ref[...], wd_ref[...])
        av = a_ref[...].astype(F32)
        sig = jax.nn.sigmoid(av)
        da_ref[...] = (ds * b_ref[...].astype(F32) * (sig * (1.0 + av * (1.0 - sig)))).astype(da_ref.dtype)
        db_ref[...] = (ds * (av * sig)).astype(db_ref.dtype)

    o_spec = pl.BlockSpec((tm, tn), lambda j, i: (i, j))
    return pl.pallas_call(
        body, name=name, grid=(N // tn, T // tm),
        in_specs=[pl.BlockSpec((tm, K), lambda j, i: (i, 0)), pl.BlockSpec((tn, K), lambda j, i: (j, 0)), o_spec, o_spec],
        out_specs=[o_spec, o_spec],
        out_shape=[jax.ShapeDtypeStruct((T, N), _MXU), jax.ShapeDtypeStruct((T, N), _MXU)],
        compiler_params=_cp(("parallel", "parallel")),
    )(g, wd, a, b)


def _xa_bwd(name, qx, do, gq, kn, vx):
    T, M = qx.shape[0], kn.shape[0]
    tm = _tile(T, 512, 16)

    def body(q_ref, do_ref, gq_ref, k_ref, v_ref, dq_ref, dk_ref, dv_ref, dg_ref):
        @pl.when(pl.program_id(0) == 0)
        def _():
            dk_ref[...] = jnp.zeros_like(dk_ref)
            dv_ref[...] = jnp.zeros_like(dv_ref)
            dg_ref[...] = jnp.zeros_like(dg_ref)

        gqv = gq_ref[...]
        for h in range(XA_HEADS):
            sl = slice(h * LANES, (h + 1) * LANES)
            qhat, r, qn, p = _xa_probs(q_ref[:, sl], gqv, k_ref[:, sl])
            do_h = do_ref[:, sl]
            dp = _dot_nt(do_h, v_ref[:, sl])
            dv_ref[:, sl] += _dot_tn(p, do_h)
            dsm = p * (dp - jnp.sum(p * dp, axis=-1, keepdims=True)) * (1.0 / math.sqrt(XA_HEAD_DIM))
            dqn = _dot(dsm, k_ref[:, sl])
            dk_ref[:, sl] += _dot_tn(dsm, qn)
            dg_ref[...] += jnp.sum(dqn * qhat, axis=0, keepdims=True)
            dxh = dqn * gqv
            dq_ref[:, sl] = (r * (dxh - qhat * jnp.mean(dxh * qhat, axis=-1, keepdims=True))).astype(dq_ref.dtype)

    full = lambda s: pl.BlockSpec(s, lambda i: tuple(0 for _ in s))
    blk = pl.BlockSpec((tm, XA_WIDTH), lambda i: (i, 0))
    return pl.pallas_call(
        body, name=name, grid=(T // tm,),
        in_specs=[blk, blk, full((1, LANES)), full((M, XA_WIDTH)), full((M, XA_WIDTH))],
        out_specs=[blk, full((M, XA_WIDTH)), full((M, XA_WIDTH)), full((1, LANES))],
        out_shape=[jax.ShapeDtypeStruct((T, XA_WIDTH), _MXU), jax.ShapeDtypeStruct((M, XA_WIDTH), F32),
                   jax.ShapeDtypeStruct((M, XA_WIDTH), F32), jax.ShapeDtypeStruct((1, LANES), F32)],
        compiler_params=_cp(("arbitrary",)),
    )(qx, do, gq.reshape(1, XA_HEAD_DIM), kn, vx)


def _mem_bwd(name, mem, g_mem, m, kv, gk, dkn, dvx, wkv):
    M = mem.shape[0]

    def body(mem_ref, g_ref, m_ref, kv_ref, gk_ref, dkn_ref, dvx_ref, w_ref, dw_ref, dgk_ref, dgm_ref):
        gkv = gk_ref[...]
        dgk = jnp.zeros((1, LANES), F32)
        parts = []
        for h in range(XA_HEADS):
            sl = slice(h * LANES, (h + 1) * LANES)
            k = kv_ref[:, sl]
            r = _rms_rows(k)
            khat = k * r
            dkn_h = dkn_ref[:, sl]
            dgk = dgk + jnp.sum(dkn_h * khat, axis=0, keepdims=True)
            dxh = dkn_h * gkv
            parts.append(r * (dxh - khat * jnp.mean(dxh * khat, axis=-1, keepdims=True)))
        dgk_ref[...] = dgk
        dkv = jnp.concatenate(parts + [dvx_ref[...]], axis=1)
        dw_ref[...] = _dot_tn(m_ref[...], dkv)
        dm = _dot_nt(dkv, w_ref[...])
        x = mem_ref[...]
        dgm_ref[...] = jnp.sum(dm * (x * _rms_rows(x)), axis=0, keepdims=True)

    return pl.pallas_call(
        body, name=name,
        out_shape=[jax.ShapeDtypeStruct((D_MODEL, 2 * XA_WIDTH), F32), jax.ShapeDtypeStruct((1, LANES), F32),
                   jax.ShapeDtypeStruct((1, D_MODEL), F32)],
        compiler_params=pltpu.CompilerParams(vmem_limit_bytes=VMEM_LIMIT),
    )(mem, g_mem.reshape(1, D_MODEL), m, kv, gk.reshape(1, XA_HEAD_DIM), dkn, dvx, wkv)


def _fox_bwd_common(q2, k2, v2, dy2, yf2, c_ref, ct_ref, lse_ref, e, h, lo, diag, tq):
    sel = lo if e == 0 else ~lo
    qh = jnp.where(sel, q2, jnp.zeros_like(q2))
    kh = jnp.where(sel, k2, jnp.zeros_like(k2))
    dyh = jnp.where(sel, dy2, 0.0)
    s = _fox_logits(qh, k2, c_ref[:, h:h + 1], ct_ref[h:h + 1, :], diag, tq)
    p = jnp.exp(s - lse_ref[:, h:h + 1])
    dp = _dot_nt(dyh, v2)
    delta = jnp.sum(dyh * yf2, axis=-1, keepdims=True)
    return p, p * (dp - delta), qh, kh, dyh


def _fox_bwd(name, qn, kn, vv, c, ct, lse, dycat, ybf, flags, side=None):
    T = qn.shape[0]
    tq = _tile(T, FOX_TILE, LANES)
    nblk = T // tq
    qs, ks = _tri_steps(nblk, False)
    nsteps = int(qs.shape[0])
    scale = 1.0 / math.sqrt(FOX_HEAD_DIM)
    s_specs, s_args, s_shapes, s_scratch = _side_specs(side)

    def body(qs_ref, ks_ref, *refs):
        main_in, side_in, main_out, side_out, main_scr, sems = _split_side(refs, 9, 5, 3, side)
        q_ref, k_ref, v_ref, c_ref, ct_ref, lse_ref, dy_ref, yf_ref, flag_ref = main_in
        dq_hbm, drow_hbm, dk_ref, dv_ref, dct_ref = main_out
        dq_acc, drow_acc, out_sem = main_scr
        s_id = pl.program_id(0)
        qi, ki = qs_ref[s_id], ks_ref[s_id]
        _side_run(side, s_id, nsteps, side_in, side_out, sems)
        rows = pl.ds(pl.multiple_of(qi * tq, tq), tq)

        @pl.when(qi == ki)
        def _():
            dk_ref[...] = jnp.zeros_like(dk_ref)
            dv_ref[...] = jnp.zeros_like(dv_ref)
            dct_ref[...] = jnp.zeros_like(dct_ref)

        @pl.when(ki == 0)
        def _():
            dq_acc[rows, :] = jnp.zeros((tq, FOX_WIDTH), F32)
            drow_acc[rows, :] = jnp.zeros((tq, LANES), F32)

        lo = _lane((tq, LANES)) < FOX_HEAD_DIM

        def step(diag):
            for j in range(FOX_WIDTH // LANES):
                sl = slice(j * LANES, (j + 1) * LANES)
                for e in range(2):
                    h = 2 * j + e

                    def head(h=h, e=e, sl=sl):
                        q2, k2, v2, dy2, yf2 = q_ref[:, sl], k_ref[:, sl], v_ref[:, sl], dy_ref[:, sl], yf_ref[:, sl]
                        p, ds, qh, kh, dyh = _fox_bwd_common(q2, k2, v2, dy2, yf2, c_ref, ct_ref, lse_ref, e, h, lo, diag, tq)
                        drow_acc[rows, :] += jnp.where(_lane((tq, LANES)) == h, jnp.sum(ds, axis=-1, keepdims=True), 0.0)
                        dct_ref[h:h + 1, :] -= jnp.sum(ds, axis=0, keepdims=True)
                        dq_acc[rows, sl] += _dot(ds, kh) * scale
                        dk_ref[:, sl] += _dot_tn(ds, qh)
                        dv_ref[:, sl] += _dot_tn(p, dyh)

                    if diag:
                        head()
                    else:
                        pl.when(flag_ref[(qi * nblk + ki) * FOX_HEADS + h] > 0.5)(head)

        _on_diag_or_not(qi, ki, step)

        @pl.when(s_id == nsteps - 1)
        def _():
            for src, dst in ((dq_acc, dq_hbm), (drow_acc, drow_hbm)):
                cp = pltpu.make_async_copy(src, dst, out_sem)
                cp.start()
                cp.wait()

    qmap = lambda s, a, b: (a[s], 0)
    kmap = lambda s, a, b: (b[s], 0)
    ctmap = lambda s, a, b: (0, b[s])
    any_spec = pl.BlockSpec(memory_space=pl.ANY)
    grid_spec = pltpu.PrefetchScalarGridSpec(
        num_scalar_prefetch=2, grid=(nsteps,),
        in_specs=[pl.BlockSpec((tq, 512), qmap), pl.BlockSpec((tq, 512), kmap), pl.BlockSpec((tq, 512), kmap),
                  pl.BlockSpec((tq, LANES), qmap), pl.BlockSpec((SUBLANES, tq), ctmap),
                  pl.BlockSpec((tq, LANES), qmap), pl.BlockSpec((tq, 512), lambda s, a, b: (a[s], 1)), pl.BlockSpec((tq, 512), qmap),
                  pl.BlockSpec(memory_space=pltpu.SMEM)] + s_specs,
        out_specs=[any_spec, any_spec, pl.BlockSpec((tq, 512), kmap), pl.BlockSpec((tq, 512), kmap), pl.BlockSpec((SUBLANES, tq), ctmap)] + s_specs,
        scratch_shapes=[pltpu.VMEM((T, FOX_WIDTH), F32), pltpu.VMEM((T, LANES), F32), pltpu.SemaphoreType.DMA] + s_scratch,
    )
    res = pl.pallas_call(
        body, name=name, grid_spec=grid_spec,
        out_shape=[jax.ShapeDtypeStruct((T, 512), F32), jax.ShapeDtypeStruct((T, LANES), F32), jax.ShapeDtypeStruct((T, 512), F32),
                   jax.ShapeDtypeStruct((T, 512), F32), jax.ShapeDtypeStruct((SUBLANES, T), F32)] + s_shapes,
        compiler_params=_cp(("arbitrary",)),
    )(qs, ks, qn, kn, vv, c, ct, lse, dycat, ybf, flags, *s_args)
    return res[:5], res[5:]


def _fold64(row):
    return row + pltpu.roll(row, FOX_HEAD_DIM, axis=1)


def _even_bwd(name, z, dycat, dqn, dkn, dvv, dct, drow, g_v, w_s, bs_cols, g_qn2, g_kn2, b_f_pad):
    T = z.shape[0]
    tm = _tile(T, 256, A_BLOCK)
    nb, nt = tm // A_BLOCK, T // tm

    def body(z_ref, dya_ref, dq_ref, dk_ref, dv_ref, dct_ref, drow_ref, gv_ref, ws_ref, bs_ref, gq_ref, gk_ref, bf_ref,
             dz_ref, dws_ref, dbs_ref, dgv_ref, dgq_ref, dgk_ref, dbf_ref, carry):
        @pl.when(pl.program_id(0) == 0)
        def _():
            for r in (dws_ref, dbs_ref, dgv_ref, dgq_ref, dgk_ref, dbf_ref, carry):
                r[...] = jnp.zeros_like(r)

        keep = (_row((A_BLOCK, A_BLOCK)) // CHUNK) >= (_lane((A_BLOCK, A_BLOCK)) // CHUNK)
        for g in range(A_GROUPS):
            sl = slice(g * A_GROUP_DIM, (g + 1) * A_GROUP_DIM)
            wm = _mx(_masked_ws(ws_ref, g))
            zu = z_ref[:, sl]
            zv = z_ref[:, A_WIDTH + g * A_GROUP_DIM:A_WIDTH + (g + 1) * A_GROUP_DIM]
            u, v = _gelu(zu), _gelu(zv)
            r = _rms_rows(v)
            vhat = v * r
            gvv = gv_ref[:, sl]
            vn = _mx(vhat * gvv)
            dya = dya_ref[:, sl]
            du_parts, dvn_parts = [], []
            for n in range(nb):
                rows = slice(n * A_BLOCK, (n + 1) * A_BLOCK)
                s = jnp.dot(wm, vn[rows], preferred_element_type=F32) + bs_ref[:, g:g + 1]
                du_parts.append(dya[rows] * s)
                d_s = dya[rows] * u[rows]
                dbs_ref[...] += jnp.where(_lane((A_BLOCK, LANES)) == g, jnp.sum(d_s, axis=-1, keepdims=True), 0.0)
                dws_ref[g] += jnp.where(keep, _dot_nt(d_s, vn[rows]), 0.0)
                dvn_parts.append(_dot_tn(wm, d_s))
            dvn = jnp.concatenate(dvn_parts, axis=0)
            dgv_ref[:, sl] += jnp.sum(dvn * vhat, axis=0, keepdims=True)
            dxh = dvn * gvv
            dv = r * (dxh - vhat * jnp.mean(dxh * vhat, axis=-1, keepdims=True))
            dz_ref[:, sl] = (jnp.concatenate(du_parts, axis=0) * _gelu_grad(zu)).astype(dz_ref.dtype)
            dz_ref[:, A_WIDTH + g * A_GROUP_DIM:A_WIDTH + (g + 1) * A_GROUP_DIM] = (dv * _gelu_grad(zv)).astype(dz_ref.dtype)
        o = 2 * A_WIDTH
        for j in range(FOX_WIDTH // LANES):
            sl = slice(j * LANES, (j + 1) * LANES)
            for (off, d_ref, g_ref, dg_ref) in ((o, dq_ref, gq_ref, dgq_ref), (o + FOX_WIDTH, dk_ref, gk_ref, dgk_ref)):
                zq = z_ref[:, off + j * LANES:off + (j + 1) * LANES]
                r = _head64_rms(zq)
                qhat = zq * r
                dn = d_ref[:, sl]
                dg_ref[...] += _fold64(jnp.sum(dn * qhat, axis=0, keepdims=True))
                dxh = dn * g_ref[...]
                dz_ref[:, off + j * LANES:off + (j + 1) * LANES] = (r * (dxh - qhat * _head64_mean(dxh * qhat))).astype(dz_ref.dtype)
            dz_ref[:, o + 2 * FOX_WIDTH + j * LANES:o + 2 * FOX_WIDTH + (j + 1) * LANES] = dv_ref[:, sl].astype(dz_ref.dtype)
        dct_v = dct_ref[...] + drow_ref[...].T[0:SUBLANES, :]
        upper = (_row((tm, tm)) >= _lane((tm, tm))).astype(F32)
        dlf_t = _dot_f32(dct_v, upper) + carry[:, 0:1]
        carry[...] += jnp.sum(dct_v, axis=1, keepdims=True)
        dlf = jnp.concatenate([dlf_t, jnp.zeros((LANES - SUBLANES, tm), F32)], axis=0).T
        zf = z_ref[:, o + 3 * FOX_WIDTH:o + 3 * FOX_WIDTH + LANES]
        dzf = dlf * jax.nn.sigmoid(-(zf + bf_ref[...]))
        dbf_ref[...] += jnp.sum(dzf, axis=0, keepdims=True)
        dz_ref[:, o + 3 * FOX_WIDTH:o + 3 * FOX_WIDTH + LANES] = dzf.astype(dz_ref.dtype)

    rev = lambda i: nt - 1 - i
    wide = lambda n, col=0: pl.BlockSpec((tm, n), lambda i: (rev(i), col))
    full = lambda s: pl.BlockSpec(s, lambda i: tuple(0 for _ in s))
    vec = full((1, LANES))
    return pl.pallas_call(
        body, name=name, grid=(nt,),
        in_specs=[wide(EVEN_IN_PAD), wide(512), wide(512), wide(512), wide(512), pl.BlockSpec((SUBLANES, tm), lambda i: (0, rev(i))),
                  wide(LANES), full((1, A_WIDTH)), full((A_GROUPS, A_BLOCK, A_BLOCK)), full((A_BLOCK, LANES)), vec, vec, vec],
        out_specs=[wide(EVEN_IN_PAD), full((A_GROUPS, A_BLOCK, A_BLOCK)), full((A_BLOCK, LANES)), full((1, A_WIDTH)), vec, vec, vec],
        out_shape=[jax.ShapeDtypeStruct((T, EVEN_IN_PAD), _MXU), jax.ShapeDtypeStruct((A_GROUPS, A_BLOCK, A_BLOCK), F32),
                   jax.ShapeDtypeStruct((A_BLOCK, LANES), F32), jax.ShapeDtypeStruct((1, A_WIDTH), F32),
                   jax.ShapeDtypeStruct((1, LANES), F32), jax.ShapeDtypeStruct((1, LANES), F32), jax.ShapeDtypeStruct((1, LANES), F32)],
        scratch_shapes=[pltpu.VMEM((SUBLANES, LANES), F32)],
        compiler_params=_cp(("arbitrary",)),
    )(z, dycat, dqn, dkn, dvv, dct, drow, g_v, w_s, bs_cols, g_qn2, g_kn2, b_f_pad)


def _shift_up(ext, k, tm):
    return pltpu.roll(ext, ext.shape[0] - k, axis=0)[0:tm]


def _odd_bwd(name, z, dycat, w_pool, s_pool, conv_w8):
    T = z.shape[0]
    tm = _tile(T, 256, HALO)
    r, nt = tm // HALO, T // tm
    n_ext = tm + HALO

    def body(z_ref, zb_ref, zn_ref, dy_ref, dyn_ref, wp_ref, sp_ref, cw_ref, dz_ref, dwp_ref, dsp_ref, dcw_ref):
        i = pl.program_id(0)

        @pl.when(i == 0)
        def _():
            for rr in (dwp_ref, dsp_ref, dcw_ref):
                rr[...] = jnp.zeros_like(rr)

        first, last = i == 0, i == nt - 1
        pos = i * tm + _row((tm, LANES))
        pos_ext = i * tm + _row((n_ext, LANES))
        for g, w in enumerate(POOL_WINDOWS):
            sl = slice(g * LANES, (g + 1) * LANES)
            z_ext = jnp.concatenate([jnp.where(first, 0.0, zb_ref[:, sl]), z_ref[:, sl]], axis=0)
            p = _pool_p(z_ext, g, tm, pos)
            spv = sp_ref[:, sl]
            dyc = dy_ref[:, sl]
            dsp_ref[:, sl] += jnp.sum(dyc * _dot(p, wp_ref[g]), axis=0, keepdims=True)
            dpw_ext = jnp.concatenate([dyc, jnp.where(last, 0.0, dyn_ref[:, sl])], axis=0) * spv
            dwp_ref[g] += _dot_tn(p, dpw_ext[0:tm])
            dp_ext = _dot_nt(dpw_ext, wp_ref[g])
            f = dp_ext / jnp.minimum(pos_ext + 1, w).astype(F32)
            span = 1
            while span < w:
                f = f + pltpu.roll(f, n_ext - span, axis=0)
                span *= 2
            dz_ref[:, sl] = (f[0:tm] - dp_ext[0:tm]).astype(dz_ref.dtype)
        for j in range(CONV_WIDTH // LANES):
            sl = slice(j * LANES, (j + 1) * LANES)
            hd = slice(POOL_WIDTH + j * LANES, POOL_WIDTH + (j + 1) * LANES)
            gb = slice(POOL_WIDTH + CONV_WIDTH + j * LANES, POOL_WIDTH + CONV_WIDTH + (j + 1) * LANES)
            gc = slice(POOL_WIDTH + 2 * CONV_WIDTH + j * LANES, POOL_WIDTH + 2 * CONV_WIDTH + (j + 1) * LANES)
            ysl = slice(POOL_WIDTH + j * LANES, POOL_WIDTH + (j + 1) * LANES)
            hdv, gbv, gcv = z_ref[:, hd], z_ref[:, gb], z_ref[:, gc]
            xg = gcv * hdv
            xg_ext = jnp.concatenate([jnp.where(first, 0.0, zb_ref[:, gc] * zb_ref[:, hd]), xg], axis=0)
            xg1, xg2 = _shift_down(xg_ext, 1, tm), _shift_down(xg_ext, 2, tm)
            w0, w1, w2 = cw_ref[0:1, sl], cw_ref[1:2, sl], cw_ref[2:3, sl]
            conv = w0 * xg2 + w1 * xg1 + w2 * xg
            dyd = dy_ref[:, ysl]
            dconv = dyd * gbv
            dconv_ext = jnp.concatenate([dconv, jnp.where(last, 0.0, dyn_ref[:, ysl] * zn_ref[:, gb])], axis=0)
            dcw_ref[0:1, sl] += jnp.sum(dconv * xg2, axis=0, keepdims=True)
            dcw_ref[1:2, sl] += jnp.sum(dconv * xg1, axis=0, keepdims=True)
            dcw_ref[2:3, sl] += jnp.sum(dconv * xg, axis=0, keepdims=True)
            dxg = w2 * dconv + w1 * _shift_up(dconv_ext, 1, tm) + w0 * _shift_up(dconv_ext, 2, tm)
            dz_ref[:, hd] = (dxg * gcv).astype(dz_ref.dtype)
            dz_ref[:, gb] = (dyd * conv).astype(dz_ref.dtype)
            dz_ref[:, gc] = (dxg * hdv).astype(dz_ref.dtype)

    full = lambda s: pl.BlockSpec(s, lambda i: tuple(0 for _ in s))
    back = lambda n: pl.BlockSpec((HALO, n), lambda i: (jnp.maximum(i * r - 1, 0), 0))
    nxt = lambda n: pl.BlockSpec((HALO, n), lambda i: (jnp.minimum((i + 1) * r, T // HALO - 1), 0))
    return pl.pallas_call(
        body, name=name, grid=(nt,),
        in_specs=[pl.BlockSpec((tm, ODD_IN), lambda i: (i, 0)), back(ODD_IN), nxt(ODD_IN),
                  pl.BlockSpec((tm, 1024), lambda i: (i, 0)), nxt(1024),
                  full((4, LANES, LANES)), full((1, POOL_WIDTH)), full((SUBLANES, CONV_WIDTH))],
        out_specs=[pl.BlockSpec((tm, ODD_IN), lambda i: (i, 0)), full((4, LANES, LANES)), full((1, POOL_WIDTH)), full((SUBLANES, CONV_WIDTH))],
        out_shape=[jax.ShapeDtypeStruct((T, ODD_IN), _MXU), jax.ShapeDtypeStruct((4, LANES, LANES), F32),
                   jax.ShapeDtypeStruct((1, POOL_WIDTH), F32), jax.ShapeDtypeStruct((SUBLANES, CONV_WIDTH), F32)],
        compiler_params=_cp(("arbitrary",)),
    )(z, z, z, dycat, dycat, w_pool, s_pool, conv_w8)


def _layer_bwd(l, g3, mem, W, sv, side_fn=None):
    i = l // 2
    G = {}
    side_out = ()
    da, db = _ffn_bwd_act(f"ffn_bwd_act{l}", g3, W["w_down"], sv["a"], sv["b"])
    G["w_down"] = _mm(f"d_w_down{l}", [(sv["s"], g3)], "tn")
    g2, dg = _mm(f"d_h3{l}", [(da, W["w_gate"]), (db, W["w_up"])], "nt", norm_bwd=(sv["x2"], W["g_ffn"], g3))
    G["w_gate"] = _mm(f"d_w_gate{l}", [(sv["h3"], da)], "tn")
    G["w_up"] = _mm(f"d_w_up{l}", [(sv["h3"], db)], "tn")
    G["g_ffn"] = dg[0]

    do = _mm(f"d_o{l}", [(g2, W["wo"])], "nt", out_dtype=_MXU)
    G["xa_wo"] = _mm(f"d_wo{l}", [(sv["o"], g2)], "tn")
    dqx, dkn, dvx, dgq = _xa_bwd(f"xa_bwd{l}", sv["qx"], do, W["gq"], sv["kn_x"], sv["vx"])
    G["xa_gq"] = dgq[0]
    g1, dg = _mm(f"d_h2{l}", [(dqx, W["wq"])], "nt", norm_bwd=(sv["x1"], W["g_xa"], g2))
    G["xa_wq"] = _mm(f"d_wq{l}", [(sv["h2"], dqx)], "tn")
    G["g_xa"] = dg[0]
    dwkv, dgk, dgm = _mem_bwd(f"mem_bwd{l}", mem, W["g_mem"], sv["m"], sv["kv"], W["gk"], dkn, dvx, W["wkv"])
    G["xa_wkv"], G["xa_gk"], G["g_mem"] = dwkv, dgk[0], dgm[0]

    dycat = _mm(f"d_ycat{l}", [(g1, W["w_out"])], "nt")
    if l % 2 == 0:
        G["e_w_out"] = jnp.concatenate([_mm(f"d_e_wout_a{l}", [(sv["ya"], g1)], "tn"), _mm(f"d_e_wout_b{l}", [(sv["yb"], g1)], "tn")], axis=0)
        fa = (sv["qn"], sv["kn"], sv["vv"], sv["c"], sv["ct"], sv["lse"], dycat, sv["ybf"], sv["flags"])
        (dqn, drow, dkn_f, dvv, dct), side_out = _fox_bwd(f"fox_bwd{l}", *fa, side=side_fn(G) if side_fn is not None else None)
        dz, dws, dbs, dgv, dgq2, dgk2, dbf = _even_bwd(f"even_bwd{l}", sv["z"], dycat, dqn, dkn_f, dvv, dct, drow, W["g_v"], W["w_s"],
                                                      W["bs_cols"], W["g_qn2"], W["g_kn2"], W["b_f_pad"])
        G.update(e_w_s=dws, e_b_s=dbs[:, :A_GROUPS].T, e_g_v=dgv[0], e_g_qn=dgq2[0, :FOX_HEAD_DIM], e_g_kn=dgk2[0, :FOX_HEAD_DIM],
                 e_b_f=dbf[0, :FOX_HEADS])
        G["e_w_in"] = _mm(f"d_e_win{l}", [(sv["h1"], dz)], "tn")
    else:
        G["o_w_out"] = _mm(f"d_o_wout{l}", [(sv["ycat"], g1)], "tn")
        dz, dwp, dsp, dcw = _odd_bwd(f"odd_bwd{l}", sv["z"], dycat, W["w_pool"], W["s_pool"], W["conv_w8"])
        G.update(o_w_pool=dwp, o_s_pool=dsp[0], o_conv_w=dcw[:CONV_K])
        G["o_w_in"] = _mm(f"d_o_win{l}", [(sv["h1"], dz)], "tn")
    g0, dg = _mm(f"d_h1{l}", [(dz, W["w_in"])], "nt", norm_bwd=(sv["x0"], W["g_mix"], g1))
    G["g_mix"] = dg[0]
    return g0, G, side_out


PACK_W = 1024
MESH_T = pl.DeviceIdType.MESH
_ANY = pl.BlockSpec(memory_space=pl.ANY)


def _my_place():
    x, y, c = lax.axis_index("x"), lax.axis_index("y"), lax.axis_index("c")
    return x, y, c


def _flip(v, bit):
    return 1 - v if bit else v


N_PEERS = N_DEV - 1


def _all_gather(name, blks):
    side = _TwoLevelGather(blks)

    def body(*refs):
        _, side_in, _, side_out, _, sems = _split_side(refs, 0, 0, 0, side)
        side.start(side_in, side_out, sems)
        side.middle(side_in, side_out, sems)
        side.finish(side_in, side_out, sems)

    return pl.pallas_call(
        body, name=name, out_shape=side.out_shape,
        in_specs=[_ANY] * side.n, out_specs=[_ANY] * side.n, scratch_shapes=side.scratch,
    )(*blks)


class _Direct:
    def __init__(self, kind, arrays):
        self.kind, self.arrays, self.n = kind, list(arrays), len(arrays)
        if kind == "scatter":
            self.out_shape = [jax.ShapeDtypeStruct(a.shape, a.dtype) for a in arrays]
        else:
            self.out_shape = [jax.ShapeDtypeStruct((N_DEV,) + a.shape, a.dtype) for a in arrays]
        self.scratch = [pltpu.SemaphoreType.DMA((N_PEERS * self.n,)), pltpu.SemaphoreType.DMA((N_PEERS * self.n,)),
                        pltpu.SemaphoreType.DMA((self.n,))]

    def _copies(self, in_refs, out_refs, sems):
        send_sems, recv_sems, local_sems = sems
        x, y, c = _my_place()
        me = 4 * x + 2 * y + c
        mine, sends, recvs = [], [], []
        for t in range(self.n):
            src_of = (lambda idx, t=t: in_refs[t].at[idx]) if self.kind == "scatter" else (lambda idx, t=t: in_refs[t])
            mine.append(pltpu.make_async_copy(src_of(me), out_refs[t].at[me], local_sems.at[t]))
            for m in range(1, N_DEV):
                px, py, pc = _flip(x, m & 4), _flip(y, m & 2), _flip(c, m & 1)
                pidx = 4 * px + 2 * py + pc
                sem = dict(send_sem=send_sems.at[N_PEERS * t + m - 1], recv_sem=recv_sems.at[N_PEERS * t + m - 1], device_id_type=MESH_T)
                sends.append(pltpu.make_async_remote_copy(src_ref=src_of(pidx), dst_ref=out_refs[t].at[me], device_id=(px, py, pc), **sem))
                recvs.append(pltpu.make_async_remote_copy(src_ref=src_of(pidx), dst_ref=out_refs[t].at[pidx], device_id=(x, y, c), **sem))
        return mine, sends, recvs

    def start(self, in_refs, out_refs, sems):
        mine, sends, _ = self._copies(in_refs, out_refs, sems)
        for cp in mine + sends:
            cp.start()

    def middle(self, in_refs, out_refs, sems):
        pass

    def finish(self, in_refs, out_refs, sems):
        mine, sends, recvs = self._copies(in_refs, out_refs, sems)
        for cp in recvs:
            cp.wait_recv()
        for cp in sends:
            cp.wait_send()
        for cp in mine:
            cp.wait()


class _TwoLevelGather:
    def __init__(self, arrays):
        self.arrays, self.n = list(arrays), len(arrays)
        self.out_shape = [jax.ShapeDtypeStruct((N_DEV,) + a.shape, a.dtype) for a in arrays]
        self.scratch = [pltpu.SemaphoreType.DMA((N_PEERS * self.n,)), pltpu.SemaphoreType.DMA((N_PEERS * self.n,)),
                        pltpu.SemaphoreType.DMA((self.n,))]

    def _place(self):
        x, y, c = _my_place()
        return (x, y, c), (x, y, 1 - c), [(1 - x, y), (x, 1 - y), (1 - x, 1 - y)]

    def _copy(self, in_refs, out_refs, sems, t, k, block, to, own=False):
        slab = out_refs[t].at[4 * block[0] + 2 * block[1] + block[2]]
        return pltpu.make_async_remote_copy(
            src_ref=in_refs[t] if own else slab, dst_ref=slab, send_sem=sems[0].at[N_PEERS * t + k],
            recv_sem=sems[1].at[N_PEERS * t + k], device_id=to, device_id_type=MESH_T)

    def _mine(self, in_refs, out_refs, sems):
        x, y, c = _my_place()
        return [pltpu.make_async_copy(in_refs[t], out_refs[t].at[4 * x + 2 * y + c], sems[2].at[t]) for t in range(self.n)]

    def _first(self, in_refs, out_refs, sems):
        me, sibling, chips = self._place()
        cps = []
        for t in range(self.n):
            cps.append(self._copy(in_refs, out_refs, sems, t, 0, me, sibling, own=True))
            cps += [self._copy(in_refs, out_refs, sems, t, 1 + j, me, (*chip, me[2]), own=True) for j, chip in enumerate(chips)]
        return cps

    def _passed(self, in_refs, out_refs, sems):
        me, sibling, chips = self._place()
        return [self._copy(in_refs, out_refs, sems, t, 4 + j, (*chip, me[2]), sibling) for j, chip in enumerate(chips) for t in range(self.n)]

    def start(self, in_refs, out_refs, sems):
        for cp in self._mine(in_refs, out_refs, sems) + self._first(in_refs, out_refs, sems):
            cp.start()

    def middle(self, in_refs, out_refs, sems):
        me, _, chips = self._place()
        for j, chip in enumerate(chips):
            for t in range(self.n):
                self._copy(in_refs, out_refs, sems, t, 1 + j, (*chip, me[2]), me).wait_recv()
        for cp in self._passed(in_refs, out_refs, sems):
            cp.start()

    def finish(self, in_refs, out_refs, sems):
        me, sibling, chips = self._place()
        for t in range(self.n):
            self._copy(in_refs, out_refs, sems, t, 0, sibling, me).wait_recv()
            for j, chip in enumerate(chips):
                self._copy(in_refs, out_refs, sems, t, 4 + j, (*chip, 1 - me[2]), me).wait_recv()
        for cp in self._first(in_refs, out_refs, sems) + self._passed(in_refs, out_refs, sems):
            cp.wait_send()
        for cp in self._mine(in_refs, out_refs, sems):
            cp.wait()


def _split_side(refs, n_main_in, n_main_out, n_main_scratch, side):
    ns = side.n if side is not None else 0
    i = 0
    main_in = refs[i:i + n_main_in]; i += n_main_in
    side_in = refs[i:i + ns]; i += ns
    main_out = refs[i:i + n_main_out]; i += n_main_out
    side_out = refs[i:i + ns]; i += ns
    main_scr = refs[i:i + n_main_scratch]; i += n_main_scratch
    return main_in, side_in, main_out, side_out, main_scr, refs[i:]


def _all_to_all(name, gs):
    side = _Direct("scatter", gs)

    def body(*refs):
        _, side_in, _, side_out, _, sems = _split_side(refs, 0, 0, 0, side)
        side.start(side_in, side_out, sems)
        side.finish(side_in, side_out, sems)

    return pl.pallas_call(
        body, name=name, out_shape=side.out_shape,
        in_specs=[_ANY] * side.n, out_specs=[_ANY] * side.n, scratch_shapes=side.scratch,
    )(*gs)


def _unblock(name, blocks, layer, n_out):
    _, _, K, nb = blocks.shape
    tk = _tile(K, 256, 16)

    def body(b_ref, o_ref):
        for j in range(N_DEV):
            o_ref[:, j * nb:(j + 1) * nb] = b_ref[j]
        if n_out > N_DEV * nb:
            o_ref[:, N_DEV * nb:n_out] = jnp.zeros((tk, n_out - N_DEV * nb), o_ref.dtype)

    return pl.pallas_call(
        body, name=name, grid=(K // tk,),
        in_specs=[pl.BlockSpec((N_DEV, None, tk, nb), lambda i: (0, layer, i, 0))],
        out_specs=pl.BlockSpec((tk, n_out), lambda i: (i, 0)),
        out_shape=jax.ShapeDtypeStruct((K, n_out), blocks.dtype),
        compiler_params=_cp(("parallel",)),
    )(blocks)


def _block(name, full, nb, dtype):
    K, N = full.shape
    tk = _tile(K, 256, 16)

    def body(f_ref, o_ref):
        for j in range(N_DEV):
            o_ref[j] = f_ref[:, j * nb:(j + 1) * nb].astype(o_ref.dtype)

    return pl.pallas_call(
        body, name=name, grid=(K // tk,),
        in_specs=[pl.BlockSpec((tk, N), lambda i: (i, 0))],
        out_specs=pl.BlockSpec((N_DEV, tk, nb), lambda i: (0, i, 0)),
        out_shape=jax.ShapeDtypeStruct((N_DEV, K, nb), dtype),
        compiler_params=_cp(("parallel",)),
    )(full)


ADAM_BLOCK_ELEMS = 256 * 1024
GRAD_WIRE = jnp.bfloat16


def _adamw(name, parts, w, m, v, layer=0):
    shape = w.shape[1:]
    cols = shape[-1]
    rows = math.prod(shape[:-1])
    nl = w.shape[0]
    parts, w, m, v = parts.reshape(N_DEV, rows, cols), w.reshape(nl, rows, cols), m.reshape(nl, rows, cols), v.reshape(nl, rows, cols)
    tr = _tile(rows, max(16, ADAM_BLOCK_ELEMS // cols), 16)

    def body(p_ref, w_ref, m_ref, v_ref, g_ref, d_ref, mo_ref, vo_ref):
        g = p_ref[0].astype(F32)
        for s in range(1, N_DEV):
            g = g + p_ref[s].astype(F32)
        mn = ADAM_B1 * m_ref[...] + (1.0 - ADAM_B1) * g
        vn = ADAM_B2 * v_ref[...] + (1.0 - ADAM_B2) * jnp.square(g)
        m_hat = mn / (1.0 - ADAM_B1 ** ADAM_STEP)
        v_hat = vn / (1.0 - ADAM_B2 ** ADAM_STEP)
        g_ref[...] = g
        d_ref[...] = -ADAM_LR * (m_hat / (jnp.sqrt(v_hat) + ADAM_EPS) + ADAM_WD * w_ref[...])
        mo_ref[...] = mn
        vo_ref[...] = vn

    blk = pl.BlockSpec((tr, cols), lambda i: (i, 0))
    lblk = pl.BlockSpec((None, tr, cols), lambda i: (layer, i, 0))
    shp = jax.ShapeDtypeStruct((rows, cols), F32)
    res = pl.pallas_call(
        body, name=name, grid=(rows // tr,),
        in_specs=[pl.BlockSpec((N_DEV, tr, cols), lambda i: (0, i, 0)), lblk, lblk, lblk],
        out_specs=[blk, blk, blk, blk], out_shape=[shp, shp, shp, shp],
        compiler_params=_cp(("parallel",)),
    )(parts, w, m, v)
    return [r.reshape(shape) for r in res]


SHARDED = dict(xa_wq=1, xa_wkv=1, xa_wo=2, w_gate=2, w_up=2, w_down=1, e_w_in=2, e_w_out=1, o_w_in=2, o_s_pool=1, o_conv_w=2, o_w_out=1)
LAYER_SPLIT = ("xa_wq", "xa_wkv", "xa_wo", "w_gate", "w_up", "w_down")
ELEMENTWISE_SHARDED = ("o_s_pool", "o_conv_w")
REPLICATED = ("g_mix", "g_xa", "g_mem", "xa_gq", "xa_gk", "g_ffn", "e_b_f", "e_g_v", "e_w_s", "e_b_s", "e_g_qn", "e_g_kn", "o_w_pool")
WEIGHTS = ("g_mix", "g_xa", "g_mem", "xa_wq", "xa_wkv", "xa_wo", "xa_gq", "xa_gk", "g_ffn", "w_gate", "w_up", "w_down", "e_w_in", "e_b_f",
           "e_g_v", "e_w_s", "e_b_s", "e_g_qn", "e_g_kn", "e_w_out", "o_w_in", "o_w_pool", "o_s_pool", "o_conv_w", "o_w_out")


def _rows_for(n, mult):
    return -(-n // (PACK_W * mult)) * mult


def _pack(arrs, rows, dtype):
    flat = jnp.concatenate([a.reshape(-1).astype(dtype) for a in arrs])
    return jnp.pad(flat, (0, rows * PACK_W - flat.shape[0])).reshape(rows, PACK_W)


def _unpack(slab, shapes):
    flat, out, off = slab.reshape(-1), [], 0
    for s in shapes:
        n = math.prod(s)
        out.append(flat[off:off + n].reshape(s))
        off += n
    return out


def _to_blocks(full, axis):
    s = full.shape
    return jnp.moveaxis(full.reshape(s[:axis] + (N_DEV, s[axis] // N_DEV) + s[axis + 1:]), axis, 0)


def _from_blocks(blocks, axis):
    b = jnp.moveaxis(blocks, 0, axis)
    s = b.shape
    return b.reshape(s[:axis] + (s[axis] * s[axis + 1],) + s[axis + 2:])


def kernel(x, mem, g_mix, g_xa, g_mem, xa_wq, xa_wkv, xa_wo, xa_gq, xa_gk, g_ffn, w_gate, w_up, w_down, e_w_in, e_b_f, e_g_v, e_w_s, e_b_s, e_g_qn, e_g_kn, e_w_out, o_w_in, o_w_pool, o_s_pool, o_conv_w, o_w_out, loss_target, m_g_mix, m_g_xa, m_g_mem, m_xa_wq, m_xa_wkv, m_xa_wo, m_xa_gq, m_xa_gk, m_g_ffn, m_w_gate, m_w_up, m_w_down, m_e_w_in, m_e_b_f, m_e_g_v, m_e_w_s, m_e_b_s, m_e_g_qn, m_e_g_kn, m_e_w_out, m_o_w_in, m_o_w_pool, m_o_s_pool, m_o_conv_w, m_o_w_out, v_g_mix, v_g_xa, v_g_mem, v_xa_wq, v_xa_wkv, v_xa_wo, v_xa_gq, v_xa_gk, v_g_ffn, v_w_gate, v_w_up, v_w_down, v_e_w_in, v_e_b_f, v_e_g_v, v_e_w_s, v_e_b_s, v_e_g_qn, v_e_g_kn, v_e_w_out, v_o_w_in, v_o_w_pool, v_o_s_pool, v_o_conv_w, v_o_w_out):
    args = dict(locals())
    Wt = {n: args[n] for n in WEIGHTS}
    Mo = {n: args["m_" + n] for n in WEIGHTS}
    Vo = {n: args["v_" + n] for n in WEIGHTS}
    rep_names = REPLICATED
    rep_shapes = [Wt[n].shape for n in rep_names]
    rep_rows = _rows_for(sum(math.prod(s) for s in rep_shapes) + 1, 16)
    layer_names = [LAYER_SPLIT + ("e_w_in", "e_w_out"), LAYER_SPLIT + ("o_w_in", "o_w_out", "o_s_pool", "o_conv_w")]

    def shard_of(n, l, src=Wt):
        return src[n][l:l + 1] if n in LAYER_SPLIT else src[n]

    def to_wire(n, a):
        return a if n in ELEMENTWISE_SHARDED else _mx(a)

    def full_weight(n, l, gn):
        if SHARDED[n] == 1 or n in ELEMENTWISE_SHARDED:
            return _from_blocks(gn, SHARDED[n])[0]
        return _unblock(f"unblock_{n}{l}", gn, 0, EVEN_IN_PAD if n == "e_w_in" else N_DEV * gn.shape[-1])

    def grad_slabs(n, l, gl):
        if n in ELEMENTWISE_SHARDED:
            return _to_blocks(gl.reshape((1,) + Wt[n].shape[1:-1] + (-1,)), SHARDED[n])
        if SHARDED[n] == 1:
            return _to_blocks(gl[None], 1).astype(GRAD_WIRE)
        return _block(f"block_{n}{l}", gl, Wt[n].shape[-1], GRAD_WIRE)[:, None]

    xk, mem2 = x[0], mem[0]
    later = [(n, 0) for n in layer_names[0] if n != "e_w_in"] + [(n, 1) for n in layer_names[1]]
    w_in0 = full_weight("e_w_in", 0, _all_gather("gather_e_w_in", [to_wire("e_w_in", Wt["e_w_in"])])[0])
    side = _TwoLevelGather([to_wire(n, shard_of(n, l)) for n, l in later])
    P = {n: Wt[n] for n in rep_names}
    P["e_w_in"] = {0: w_in0}

    def on_side(gathered):
        for (n, l), gn in zip(later, gathered):
            P.setdefault(n, {})[l if n in LAYER_SPLIT else 0] = full_weight(n, l, gn)

    xk, sv0, W0 = _layer_fwd(0, xk, mem2, P, side=side, on_side=on_side)
    xk, sv1, W1 = _layer_fwd(1, xk, mem2, P)

    g, loss_row = _loss_grad("loss_grad", xk, loss_target[0])
    g, G1, _ = _layer_bwd(1, g, mem2, W1, sv1)
    slabs1 = [grad_slabs(n, 1, G1[n]) for n in layer_names[1]]
    g, G0, beside = _layer_bwd(0, g, mem2, W0, sv0,
                               side_fn=lambda G: _Direct("scatter", slabs1 + [grad_slabs(n, 0, G[n]) for n in LAYER_SPLIT]))
    parts1, parts0a = beside[:len(slabs1)], beside[len(slabs1):]
    grad_x = g[None]
    parts0b = _all_to_all("exchange_grads_mixer0", [grad_slabs(n, 0, G0[n]) for n in ("e_w_in", "e_w_out")])
    parts = {(n, 1): p for n, p in zip(layer_names[1], parts1)}
    parts.update({(n, 0): p for n, p in zip(LAYER_SPLIT, parts0a)})
    parts.update({(n, 0): p for n, p in zip(("e_w_in", "e_w_out"), parts0b)})
    G = {n: {0: G0[n]} for n in G0}
    for n in G1:
        G.setdefault(n, {})[1] = G1[n]
    rep_g = [jnp.stack([G[n][l] for l in sorted(G[n])]).reshape(Wt[n].shape) for n in rep_names]
    rep_parts = _all_gather("gather_small_grads", [_pack(rep_g + [jnp.sum(loss_row).reshape(1)], rep_rows, F32)])[0]

    outs = {}
    for n in SHARDED:
        per_layer = [_adamw(f"adamw_{n}{l}", parts[(n, l if n in LAYER_SPLIT else (0 if n.startswith("e_") else 1))], Wt[n], Mo[n], Vo[n], layer=l)
                     for l in range(Wt[n].shape[0])]
        for k, kind in enumerate(("grad", "delta", "new_m", "new_v")):
            outs[kind + "_" + n] = jnp.stack([r[k] for r in per_layer])
    res = _adamw("adamw_replicated", rep_parts, _pack([Wt[n] for n in rep_names], rep_rows, F32)[None],
                 _pack([Mo[n] for n in rep_names], rep_rows, F32)[None], _pack([Vo[n] for n in rep_names], rep_rows, F32)[None])
    for kind, slab in zip(("grad", "delta", "new_m", "new_v"), res):
        for n, a in zip(rep_names, _unpack(slab, rep_shapes)):
            outs[kind + "_" + n] = a
    loss = res[0].reshape(-1)[sum(math.prod(s) for s in rep_shapes)]
    return (loss, grad_x, *[outs[k + "_" + n] for k in ("grad", "delta", "new_m", "new_v") for n in WEIGHTS])
```

```python
import functools
import math

import numpy as np
import jax
import jax.numpy as jnp
from jax import lax
from jax.experimental import pallas as pl
from jax.experimental.pallas import tpu as pltpu

F32 = jnp.float32
BF16 = jnp.bfloat16
_MXU = jnp.bfloat16

D_MODEL = 1024
N_DEV = 8
LANES = 128
SUBLANES = 8
HALO = 16
CHUNK = 64
A_GROUPS, A_GROUP_DIM, A_WIDTH, A_BLOCK = 4, 128, 512, 128
FOX_HEADS, FOX_HEAD_DIM, FOX_WIDTH = 8, 64, 512
FOX_TILE = 512
POOL_WINDOWS = (2, 4, 8, 16)
POOL_WIDTH, CONV_WIDTH, CONV_K = 512, 512, 3
EVEN_IN, EVEN_IN_PAD, ODD_IN = 2568, 2688, 2048
XA_HEADS, XA_HEAD_DIM, XA_WIDTH = 4, 128, 512
D_FF = 2816
EPS = 1e-6
ADAM_LR, ADAM_B1, ADAM_B2, ADAM_EPS, ADAM_WD, ADAM_STEP = 0.001, 0.9, 0.999, 1e-08, 0.01, 10
VMEM_LIMIT = 48 * 1024 * 1024


def _cp(sem):
    return pltpu.CompilerParams(dimension_semantics=sem, vmem_limit_bytes=VMEM_LIMIT)


def _tile(n, cap, q):
    best = None
    for d in range(q, min(n, cap) + 1, q):
        if n % d == 0:
            best = d
    return best if best is not None else n


def _mx(a):
    return a.astype(_MXU)


def _dot(a, b):
    return jnp.dot(_mx(a), _mx(b), preferred_element_type=F32)


def _dot_nt(a, b):
    return lax.dot_general(_mx(a), _mx(b), (((1,), (1,)), ((), ())), preferred_element_type=F32)


def _dot_tn(a, b):
    return lax.dot_general(_mx(a), _mx(b), (((0,), (0,)), ((), ())), preferred_element_type=F32)


def _dot_f32(a, b):
    return jnp.dot(a, b, precision=lax.Precision.HIGHEST, preferred_element_type=F32)


def _lane(shape):
    return lax.broadcasted_iota(jnp.int32, shape, len(shape) - 1)


def _row(shape):
    return lax.broadcasted_iota(jnp.int32, shape, len(shape) - 2)


def _mm(name, pairs, mode, out_dtype=F32, residual=None, norm_bwd=None, tm_cap=512, tn_cap=1536, tk_cap=2048):
    a0, b0 = pairs[0]
    if mode == "nn":
        (M, K), N = a0.shape, b0.shape[1]
    elif mode == "nt":
        (M, K), N = a0.shape, b0.shape[0]
    else:
        (K, M), N = a0.shape, b0.shape[1]
    if mode == "tn":
        tm, tk = _tile(M, 1408, LANES), _tile(K, 512, 16)
    else:
        tm, tk = _tile(M, tm_cap, 16), _tile(K, tk_cap, LANES)
    tn = _tile(N, tn_cap, LANES)
    nk = K // tk
    npairs = len(pairs)
    dot = {"nn": _dot, "nt": _dot_nt, "tn": _dot_tn}[mode]

    if norm_bwd is not None:
        assert tn == N and residual is None and mode != "tn", "the norm backward needs whole rows"

    def body(*refs):
        ab = refs[:2 * npairs]
        res_ref = refs[2 * npairs] if residual is not None else None
        k = pl.program_id(2)
        first_row_tile = pl.program_id(1) == 0
        if norm_bwd is not None:
            x_ref, g_ref, r_ref = refs[2 * npairs:2 * npairs + 3]
            o_ref, dg_ref, acc = refs[-3], refs[-2], refs[-1]
        else:
            o_ref, acc = refs[-2], refs[-1]

        @pl.when(k == 0)
        def _():
            acc[...] = jnp.zeros_like(acc)

        for p in range(npairs):
            acc[...] += dot(ab[2 * p][...], ab[2 * p + 1][...])

        @pl.when(k == nk - 1)
        def _():
            out = acc[...]
            if res_ref is not None:
                out = out + res_ref[...]
            if norm_bwd is not None:
                xv = x_ref[...]
                r = _rms_rows(xv)
                xhat = xv * r
                dxhat = out * g_ref[...]

                @pl.when(first_row_tile)
                def _():
                    dg_ref[...] = jnp.zeros_like(dg_ref)

                dg_ref[...] += jnp.sum(out * xhat, axis=0, keepdims=True)
                out = r_ref[...] + r * (dxhat - xhat * jnp.mean(dxhat * xhat, axis=-1, keepdims=True))
            o_ref[...] = out.astype(o_ref.dtype)

    if mode == "nn":
        a_spec = pl.BlockSpec((tm, tk), lambda j, i, k: (i, k))
        b_spec = pl.BlockSpec((tk, tn), lambda j, i, k: (k, j))
    elif mode == "nt":
        a_spec = pl.BlockSpec((tm, tk), lambda j, i, k: (i, k))
        b_spec = pl.BlockSpec((tn, tk), lambda j, i, k: (j, k))
    else:
        a_spec = pl.BlockSpec((tk, tm), lambda j, i, k: (k, i))
        b_spec = pl.BlockSpec((tk, tn), lambda j, i, k: (k, j))
    o_spec = pl.BlockSpec((tm, tn), lambda j, i, k: (i, j))
    in_specs, args = [], []
    for a, b in pairs:
        in_specs += [a_spec, b_spec]
        args += [a, b]
    if residual is not None:
        in_specs.append(o_spec)
        args.append(residual)
    out_specs, out_shape, sem = o_spec, jax.ShapeDtypeStruct((M, N), out_dtype), ("parallel", "parallel", "arbitrary")
    if norm_bwd is not None:
        x, g, res = norm_bwd
        vec = pl.BlockSpec((1, tn), lambda j, i, k: (0, j))
        in_specs += [o_spec, vec, o_spec]
        args += [x, g.reshape(1, N), res]
        out_specs, out_shape = [o_spec, vec], [out_shape, jax.ShapeDtypeStruct((1, N), F32)]
        sem = ("arbitrary", "arbitrary", "arbitrary")
    return pl.pallas_call(
        body, name=name, grid=(N // tn, M // tm, nk),
        in_specs=in_specs, out_specs=out_specs, out_shape=out_shape,
        scratch_shapes=[pltpu.VMEM((tm, tn), F32)],
        compiler_params=_cp(sem),
    )(*args)


def _rms_rows(x):
    return lax.rsqrt(jnp.mean(x * x, axis=-1, keepdims=True) + EPS)


def _norm_fwd(name, x, g):
    T, Dm = x.shape
    tm = _tile(T, 512, 16)

    def body(x_ref, g_ref, o_ref):
        xv = x_ref[...]
        o_ref[...] = ((xv * _rms_rows(xv)) * g_ref[...]).astype(o_ref.dtype)

    return pl.pallas_call(
        body, name=name, grid=(T // tm,),
        in_specs=[pl.BlockSpec((tm, Dm), lambda i: (i, 0)), pl.BlockSpec((1, Dm), lambda i: (0, 0))],
        out_specs=pl.BlockSpec((tm, Dm), lambda i: (i, 0)),
        out_shape=jax.ShapeDtypeStruct((T, Dm), _MXU),
        compiler_params=_cp(("parallel",)),
    )(x, g.reshape(1, Dm))


def _gelu(x):
    return jax.nn.gelu(x)


def _gelu_grad(x):
    c0, c1 = math.sqrt(2.0 / math.pi), 0.044715
    t = jnp.tanh(c0 * (x + c1 * x * x * x))
    return 0.5 * (1.0 + t) + 0.5 * x * (1.0 - t * t) * c0 * (1.0 + 3.0 * c1 * x * x)


def _head64_rms(x):
    lo = _lane(x.shape) < FOX_HEAD_DIM
    xx = x * x
    sa = jnp.sum(jnp.where(lo, xx, 0.0), axis=-1, keepdims=True)
    sb = jnp.sum(jnp.where(lo, 0.0, xx), axis=-1, keepdims=True)
    inv = 1.0 / FOX_HEAD_DIM
    return jnp.where(lo, lax.rsqrt(sa * inv + EPS), lax.rsqrt(sb * inv + EPS))


def _head64_mean(x):
    lo = _lane(x.shape) < FOX_HEAD_DIM
    sa = jnp.sum(jnp.where(lo, x, 0.0), axis=-1, keepdims=True)
    sb = jnp.sum(jnp.where(lo, 0.0, x), axis=-1, keepdims=True)
    return jnp.where(lo, sa, sb) * (1.0 / FOX_HEAD_DIM)


def _masked_ws(ws_ref, g):
    w = ws_ref[g]
    keep = (_row(w.shape) // CHUNK) >= (_lane(w.shape) // CHUNK)
    return jnp.where(keep, w, 0.0)


def _even_pre(name, z, g_v, w_s, bs_cols, g_qn2, g_kn2, b_f_pad):
    T = z.shape[0]
    tm = _tile(T, 256, A_BLOCK)
    nb = tm // A_BLOCK

    def body(z_ref, gv_ref, ws_ref, bs_ref, gq_ref, gk_ref, bf_ref, ya_ref, q_ref, k_ref, v_ref, c_ref, ct_ref, carry):
        i = pl.program_id(0)

        @pl.when(i == 0)
        def _():
            carry[...] = jnp.zeros_like(carry)

        wm = [_mx(_masked_ws(ws_ref, g)) for g in range(A_GROUPS)]
        for g in range(A_GROUPS):
            sl = slice(g * A_GROUP_DIM, (g + 1) * A_GROUP_DIM)
            u = _gelu(z_ref[:, sl])
            v = _gelu(z_ref[:, A_WIDTH + g * A_GROUP_DIM:A_WIDTH + (g + 1) * A_GROUP_DIM])
            vn = _mx((v * _rms_rows(v)) * gv_ref[:, sl])
            for n in range(nb):
                rows = slice(n * A_BLOCK, (n + 1) * A_BLOCK)
                s = jnp.dot(wm[g], vn[rows], preferred_element_type=F32) + bs_ref[:, g:g + 1]
                ya_ref[rows, sl] = (u[rows] * s).astype(ya_ref.dtype)
        o = 2 * A_WIDTH
        for j in range(FOX_WIDTH // LANES):
            sl = slice(j * LANES, (j + 1) * LANES)
            q = z_ref[:, o + j * LANES:o + (j + 1) * LANES]
            q_ref[:, sl] = (((q * _head64_rms(q)) * gq_ref[...]) * (1.0 / math.sqrt(FOX_HEAD_DIM))).astype(q_ref.dtype)
            k = z_ref[:, o + FOX_WIDTH + j * LANES:o + FOX_WIDTH + (j + 1) * LANES]
            k_ref[:, sl] = ((k * _head64_rms(k)) * gk_ref[...]).astype(k_ref.dtype)
            v_ref[:, sl] = z_ref[:, o + 2 * FOX_WIDTH + j * LANES:o + 2 * FOX_WIDTH + (j + 1) * LANES].astype(v_ref.dtype)
        zf = z_ref[:, o + 3 * FOX_WIDTH:o + 3 * FOX_WIDTH + LANES]
        logf = jnp.where(_lane(zf.shape) < FOX_HEADS, jax.nn.log_sigmoid(zf + bf_ref[...]), 0.0)
        tri = (_row((tm, tm)) >= _lane((tm, tm))).astype(F32)
        c = _dot_f32(tri, logf) + carry[0:1, :]
        c_ref[...] = c
        ct_ref[...] = c.T[0:SUBLANES, :]
        carry[0:1, :] = c[tm - 1:tm, :]

    wide = lambda n: pl.BlockSpec((tm, n), lambda i: (i, 0))
    full = lambda s: pl.BlockSpec(s, lambda i: tuple(0 for _ in s))
    out512 = jax.ShapeDtypeStruct((T, 512), _MXU)
    return pl.pallas_call(
        body, name=name, grid=(T // tm,),
        in_specs=[wide(EVEN_IN_PAD), full((1, A_WIDTH)), full((A_GROUPS, A_BLOCK, A_BLOCK)), full((A_BLOCK, LANES)),
                  full((1, LANES)), full((1, LANES)), full((1, LANES))],
        out_specs=[wide(512), wide(512), wide(512), wide(512), wide(LANES), pl.BlockSpec((SUBLANES, tm), lambda i: (0, i))],
        out_shape=[out512, out512, out512, out512, jax.ShapeDtypeStruct((T, LANES), F32), jax.ShapeDtypeStruct((SUBLANES, T), F32)],
        scratch_shapes=[pltpu.VMEM((SUBLANES, LANES), F32)],
        compiler_params=_cp(("arbitrary",)),
    )(z, g_v, w_s, bs_cols, g_qn2, g_kn2, b_f_pad)


def _tri_steps(n, by_rows):
    if by_rows:
        pairs = [(q, k) for q in range(n) for k in range(q + 1)]
    else:
        pairs = [(q, k) for k in range(n) for q in range(k, n)]
    return (jnp.asarray(np.array([p[0] for p in pairs], np.int32)), jnp.asarray(np.array([p[1] for p in pairs], np.int32)))


def _fox_logits(q, k, c_col, c_row, diag, tq):
    s = _dot_nt(q, k) + (c_col - c_row)
    if diag:
        s = jnp.where(_row((tq, tq)) >= _lane((tq, tq)), s, -jnp.inf)
    return s


FOX_SKIP_BELOW = 104.0


def _fox_skip_flags(c, g_qn2, g_kn2):
    T = c.shape[0]
    tq = _tile(T, FOX_TILE, LANES)
    nblk = T // tq
    cb = c.reshape(nblk, tq, LANES)
    c_first, c_last = cb[:, 0, :FOX_HEADS], cb[:, tq - 1, :FOX_HEADS]
    bound = 17.0 * jnp.max(jnp.abs(g_qn2)) * jnp.max(jnp.abs(g_kn2))
    keep = bound + c_first[:, None, :] - c_last[None, :, :] > -FOX_SKIP_BELOW
    return keep.astype(F32).reshape(-1)


def _on_diag_or_not(qi, ki, step):
    @pl.when(qi == ki)
    def _():
        step(True)

    @pl.when(qi != ki)
    def _():
        step(False)


def _side_specs(side):
    if side is None:
        return [], [], [], []
    return [pl.BlockSpec(memory_space=pl.ANY)] * side.n, list(side.arrays), list(side.out_shape), list(side.scratch)


def _side_run(side, s_id, nsteps, side_in, side_out, sems):
    if side is None:
        return

    @pl.when(s_id == 0)
    def _():
        side.start(side_in, side_out, sems)

    @pl.when(s_id == nsteps // 2)
    def _():
        side.middle(side_in, side_out, sems)

    @pl.when(s_id == nsteps - 1)
    def _():
        side.finish(side_in, side_out, sems)


def _fox_fwd(name, qn, kn, vv, c, ct, flags, side=None):
    T = qn.shape[0]
    tq = _tile(T, FOX_TILE, LANES)
    nblk = T // tq
    qs, ks = _tri_steps(nblk, True)
    nsteps = int(qs.shape[0])
    npair = FOX_WIDTH // LANES
    s_specs, s_args, s_shapes, s_scratch = _side_specs(side)

    def body(qs_ref, ks_ref, *refs):
        main_in, side_in, main_out, side_out, main_scr, sems = _split_side(refs, 6, 3, 3, side)
        q_ref, k_ref, v_ref, c_ref, ct_ref, flag_ref = main_in
        yb_ref, yf_ref, lse_ref = main_out
        acc, m_s, l_s = main_scr
        s_id = pl.program_id(0)
        qi, ki = qs_ref[s_id], ks_ref[s_id]
        _side_run(side, s_id, nsteps, side_in, side_out, sems)

        @pl.when(ki == 0)
        def _():
            acc[...] = jnp.zeros_like(acc)
            m_s[...] = jnp.full_like(m_s, -jnp.inf)
            l_s[...] = jnp.zeros_like(l_s)

        lo = _lane((tq, LANES)) < FOX_HEAD_DIM

        def pairs_from(first, diag):
            for j in range(first, npair):
                sl = slice(j * LANES, (j + 1) * LANES)
                q2, k2, v2 = q_ref[:, sl], k_ref[:, sl], v_ref[:, sl]
                outs = []
                for e in range(2):
                    h = 2 * j + e
                    qh = jnp.where(lo if e == 0 else ~lo, q2, jnp.zeros_like(q2))
                    s = _fox_logits(qh, k2, c_ref[:, h:h + 1], ct_ref[h:h + 1, :], diag, tq)
                    m_old = m_s[h]
                    m_new = jnp.maximum(m_old, jnp.max(s, axis=-1, keepdims=True))
                    alpha = jnp.exp(m_old - m_new)
                    p = jnp.exp(s - m_new)
                    l_s[h] = alpha * l_s[h] + jnp.sum(p, axis=-1, keepdims=True)
                    m_s[h] = m_new
                    outs.append((alpha, _dot(p, v2)))
                alpha2 = jnp.where(lo, outs[0][0], outs[1][0])
                acc[:, sl] = alpha2 * acc[:, sl] + jnp.where(lo, outs[0][1], outs[1][1])

        def step(diag):
            if diag:
                pairs_from(0, True)
                return
            f = (qi * nblk + ki) * FOX_HEADS
            need = [(flag_ref[f + 2 * j] > 0.5) | (flag_ref[f + 2 * j + 1] > 0.5) for j in range(npair - 1)]
            first = jnp.int32(npair - 1)
            for j in reversed(range(npair - 1)):
                first = jnp.where(need[j], j, first)
            for j in range(npair):
                pl.when(first == j)(functools.partial(pairs_from, j, False))

        _on_diag_or_not(qi, ki, step)

        @pl.when(ki == qi)
        def _():
            lse = jnp.zeros((tq, LANES), F32)
            for j in range(npair):
                sl = slice(j * LANES, (j + 1) * LANES)
                inv = jnp.where(lo, 1.0 / l_s[2 * j], 1.0 / l_s[2 * j + 1])
                y = acc[:, sl] * inv
                yf_ref[:, sl] = y
                yb_ref[:, sl] = y.astype(yb_ref.dtype)
            for h in range(FOX_HEADS):
                lse = jnp.where(_lane((tq, LANES)) == h, m_s[h] + jnp.log(l_s[h]), lse)
            lse_ref[...] = lse

    qmap = lambda s, qs_r, ks_r: (qs_r[s], 0)
    kmap = lambda s, qs_r, ks_r: (ks_r[s], 0)
    grid_spec = pltpu.PrefetchScalarGridSpec(
        num_scalar_prefetch=2, grid=(nsteps,),
        in_specs=[pl.BlockSpec((tq, 512), qmap), pl.BlockSpec((tq, 512), kmap), pl.BlockSpec((tq, 512), kmap),
                  pl.BlockSpec((tq, LANES), qmap),
                  pl.BlockSpec((SUBLANES, tq), lambda s, qs_r, ks_r: (0, ks_r[s])), pl.BlockSpec(memory_space=pltpu.SMEM)] + s_specs,
        out_specs=[pl.BlockSpec((tq, 512), qmap), pl.BlockSpec((tq, 512), qmap), pl.BlockSpec((tq, LANES), qmap)] + s_specs,
        scratch_shapes=[pltpu.VMEM((tq, 512), F32), pltpu.VMEM((FOX_HEADS, tq, 1), F32), pltpu.VMEM((FOX_HEADS, tq, 1), F32)] + s_scratch,
    )
    res = pl.pallas_call(
        body, name=name, grid_spec=grid_spec,
        out_shape=[jax.ShapeDtypeStruct((T, 512), _MXU), jax.ShapeDtypeStruct((T, 512), F32), jax.ShapeDtypeStruct((T, LANES), F32)] + s_shapes,
        compiler_params=_cp(("arbitrary",)),
    )(qs, ks, qn, kn, vv, c, ct, flags, *s_args)
    return res[:3], res[3:]


def _shift_down(ext, k, tm):
    return pltpu.roll(ext, k, axis=0)[HALO:HALO + tm]


def _pool_p(z_ext, g, tm, pos):
    w = POOL_WINDOWS[g]
    s, span = z_ext, 1
    while span < w:
        s = s + pltpu.roll(s, span, axis=0)
        span *= 2
    cnt = jnp.minimum(pos + 1, w).astype(F32)
    return s[HALO:HALO + tm] / cnt - z_ext[HALO:HALO + tm]


def _odd_fwd(name, z, w_pool, s_pool, conv_w8):
    T = z.shape[0]
    tm = _tile(T, 256, HALO)
    r = tm // HALO

    def body(z_ref, zb_ref, wp_ref, sp_ref, cw_ref, y_ref):
        i = pl.program_id(0)
        first = i == 0
        pos = i * tm + _row((tm, LANES))
        for g in range(len(POOL_WINDOWS)):
            sl = slice(g * LANES, (g + 1) * LANES)
            halo = jnp.where(first, 0.0, zb_ref[:, sl])
            z_ext = jnp.concatenate([halo, z_ref[:, sl]], axis=0)
            p = _pool_p(z_ext, g, tm, pos)
            y_ref[:, sl] = (_dot(p, wp_ref[g]) * sp_ref[:, sl]).astype(y_ref.dtype)
        for j in range(CONV_WIDTH // LANES):
            sl = slice(j * LANES, (j + 1) * LANES)
            hd = slice(POOL_WIDTH + j * LANES, POOL_WIDTH + (j + 1) * LANES)
            gb = slice(POOL_WIDTH + CONV_WIDTH + j * LANES, POOL_WIDTH + CONV_WIDTH + (j + 1) * LANES)
            gc = slice(POOL_WIDTH + 2 * CONV_WIDTH + j * LANES, POOL_WIDTH + 2 * CONV_WIDTH + (j + 1) * LANES)
            xg_b = jnp.where(first, 0.0, zb_ref[:, gc] * zb_ref[:, hd])
            xg = jnp.concatenate([xg_b, z_ref[:, gc] * z_ref[:, hd]], axis=0)
            conv = (cw_ref[0:1, sl] * _shift_down(xg, 2, tm) + cw_ref[1:2, sl] * _shift_down(xg, 1, tm)
                    + cw_ref[2:3, sl] * xg[HALO:HALO + tm])
            y_ref[:, POOL_WIDTH + j * LANES:POOL_WIDTH + (j + 1) * LANES] = (z_ref[:, gb] * conv).astype(y_ref.dtype)

    full = lambda s: pl.BlockSpec(s, lambda i: tuple(0 for _ in s))
    return pl.pallas_call(
        body, name=name, grid=(T // tm,),
        in_specs=[pl.BlockSpec((tm, ODD_IN), lambda i: (i, 0)),
                  pl.BlockSpec((HALO, ODD_IN), lambda i: (jnp.maximum(i * r - 1, 0), 0)),
                  full((4, LANES, LANES)), full((1, POOL_WIDTH)), full((SUBLANES, CONV_WIDTH))],
        out_specs=pl.BlockSpec((tm, 1024), lambda i: (i, 0)),
        out_shape=jax.ShapeDtypeStruct((T, 1024), _MXU),
        compiler_params=_cp(("parallel",)),
    )(z, z, w_pool, s_pool, conv_w8)


def _mem_fwd(name, mem, g_mem, wkv, gk):
    M = mem.shape[0]

    def body(mem_ref, g_ref, w_ref, gk_ref, m_ref, kv_ref, kn_ref, v_ref):
        x = mem_ref[...]
        m = ((x * _rms_rows(x)) * g_ref[...]).astype(m_ref.dtype)
        m_ref[...] = m
        kv = _dot(m, w_ref[...])
        kv_ref[...] = kv
        for h in range(XA_HEADS):
            sl = slice(h * LANES, (h + 1) * LANES)
            k = kv[:, sl]
            kn_ref[:, sl] = ((k * _rms_rows(k)) * gk_ref[...]).astype(kn_ref.dtype)
        v_ref[...] = kv[:, XA_WIDTH:].astype(v_ref.dtype)

    return pl.pallas_call(
        body, name=name,
        out_shape=[jax.ShapeDtypeStruct((M, D_MODEL), _MXU), jax.ShapeDtypeStruct((M, 2 * XA_WIDTH), F32),
                   jax.ShapeDtypeStruct((M, XA_WIDTH), _MXU), jax.ShapeDtypeStruct((M, XA_WIDTH), _MXU)],
        compiler_params=pltpu.CompilerParams(vmem_limit_bytes=VMEM_LIMIT),
    )(mem, g_mem.reshape(1, D_MODEL), wkv, gk.reshape(1, XA_HEAD_DIM))


def _xa_probs(qx, gq, kn_h):
    r = _rms_rows(qx)
    qhat = qx * r
    qn = qhat * gq
    s = _dot_nt(qn, kn_h) * (1.0 / math.sqrt(XA_HEAD_DIM))
    s = s - jnp.max(s, axis=-1, keepdims=True)
    e = jnp.exp(s)
    return qhat, r, qn, e / jnp.sum(e, axis=-1, keepdims=True)


def _xa_fwd(name, qx, gq, kn, vx):
    T, M = qx.shape[0], kn.shape[0]
    tm = _tile(T, 512, 16)

    def body(q_ref, gq_ref, k_ref, v_ref, o_ref):
        for h in range(XA_HEADS):
            sl = slice(h * LANES, (h + 1) * LANES)
            _, _, _, p = _xa_probs(q_ref[:, sl], gq_ref[...], k_ref[:, sl])
            o_ref[:, sl] = _dot(p, v_ref[:, sl]).astype(o_ref.dtype)

    full = lambda s: pl.BlockSpec(s, lambda i: tuple(0 for _ in s))
    return pl.pallas_call(
        body, name=name, grid=(T // tm,),
        in_specs=[pl.BlockSpec((tm, XA_WIDTH), lambda i: (i, 0)), full((1, LANES)), full((M, XA_WIDTH)), full((M, XA_WIDTH))],
        out_specs=pl.BlockSpec((tm, XA_WIDTH), lambda i: (i, 0)),
        out_shape=jax.ShapeDtypeStruct((T, XA_WIDTH), _MXU),
        compiler_params=_cp(("parallel",)),
    )(qx, gq.reshape(1, XA_HEAD_DIM), kn, vx)


def _ffn_up(name, h, wg, wu):
    T, K = h.shape
    N = wg.shape[1]
    tm, tn = _tile(T, 512, 16), _tile(N, 1536, LANES)

    def body(h_ref, wg_ref, wu_ref, a_ref, b_ref, s_ref):
        hv = h_ref[...]
        a, b = _dot(hv, wg_ref[...]), _dot(hv, wu_ref[...])
        a_ref[...] = a.astype(a_ref.dtype)
        b_ref[...] = b.astype(b_ref.dtype)
        s_ref[...] = (jax.nn.silu(a) * b).astype(s_ref.dtype)

    w_spec = pl.BlockSpec((K, tn), lambda j, i: (0, j))
    o_spec = pl.BlockSpec((tm, tn), lambda j, i: (i, j))
    return pl.pallas_call(
        body, name=name, grid=(N // tn, T // tm),
        in_specs=[pl.BlockSpec((tm, K), lambda j, i: (i, 0)), w_spec, w_spec],
        out_specs=[o_spec, o_spec, o_spec],
        out_shape=[jax.ShapeDtypeStruct((T, N), _MXU), jax.ShapeDtypeStruct((T, N), _MXU), jax.ShapeDtypeStruct((T, N), _MXU)],
        compiler_params=_cp(("parallel", "parallel")),
    )(h, wg, wu)


def _loss_grad(name, y, target):
    T, Dm = y.shape
    tm = _tile(T, 512, 8)

    def body(y_ref, t_ref, dy_ref, l_ref):
        e = y_ref[...] - t_ref[...]
        dy_ref[...] = e * (1.0 / Dm)

        @pl.when(pl.program_id(0) == 0)
        def _():
            l_ref[...] = jnp.zeros_like(l_ref)

        l_ref[...] += jnp.sum(e * e, axis=0, keepdims=True) * (0.5 / Dm)

    blk = pl.BlockSpec((tm, Dm), lambda i: (i, 0))
    return pl.pallas_call(
        body, name=name, grid=(T // tm,),
        in_specs=[blk, blk], out_specs=[blk, pl.BlockSpec((1, Dm), lambda i: (0, 0))],
        out_shape=[jax.ShapeDtypeStruct((T, Dm), F32), jax.ShapeDtypeStruct((1, Dm), F32)],
        compiler_params=_cp(("arbitrary",)),
    )(y, target)


def _pad_lanes(v, n=LANES):
    v = v.reshape(1, -1)
    return jnp.pad(v, ((0, 0), (0, n - v.shape[1])))


def _layer_fwd(l, x, mem, P, side=None, on_side=None):
    sv = {"x0": x}
    W = _layer_weights(l, P, "in")
    h1 = _norm_fwd(f"norm_mix{l}", x, W["g_mix"])
    sv["h1"] = h1
    if l % 2 == 0:
        z = _mm(f"e_in{l}", [(h1, W["w_in"])], "nn")
        ya, qn, kn, vv, c, ct = _even_pre(f"even_pre{l}", z, W["g_v"], W["w_s"], W["bs_cols"], W["g_qn2"], W["g_kn2"], W["b_f_pad"])
        flags = _fox_skip_flags(c, W["g_qn2"], W["g_kn2"])
        (yb, ybf, lse), side_out = _fox_fwd(f"fox_fwd{l}", qn, kn, vv, c, ct, flags, side=side)
        if on_side is not None:
            on_side(side_out)
        W.update(_layer_weights(l, P, "rest"))
        sv.update(z=z, ya=ya, qn=qn, kn=kn, vv=vv, c=c, ct=ct, yb=yb, ybf=ybf, lse=lse, flags=flags)
        x1 = _mm(f"e_out{l}", [(ya, W["w_out"][:A_WIDTH]), (yb, W["w_out"][A_WIDTH:])], "nn", residual=x)
    else:
        W.update(_layer_weights(l, P, "rest"))
        z = _mm(f"o_in{l}", [(h1, W["w_in"])], "nn")
        ycat = _odd_fwd(f"odd_fwd{l}", z, W["w_pool"], W["s_pool"], W["conv_w8"])
        sv.update(z=z, ycat=ycat)
        x1 = _mm(f"o_out{l}", [(ycat, W["w_out"])], "nn", residual=x)
    sv["x1"] = x1
    h2 = _norm_fwd(f"norm_xa{l}", x1, W["g_xa"])
    qx = _mm(f"xa_q{l}", [(h2, W["wq"])], "nn")
    m, kv, kn_x, vx = _mem_fwd(f"mem_fwd{l}", mem, W["g_mem"], W["wkv"], W["gk"])
    o = _xa_fwd(f"xa_fwd{l}", qx, W["gq"], kn_x, vx)
    x2 = _mm(f"xa_o{l}", [(o, W["wo"])], "nn", residual=x1)
    sv.update(h2=h2, qx=qx, m=m, kv=kv, kn_x=kn_x, vx=vx, o=o, x2=x2)
    h3 = _norm_fwd(f"norm_ffn{l}", x2, W["g_ffn"])
    a, b, s = _ffn_up(f"ffn_up{l}", h3, W["w_gate"], W["w_up"])
    x3 = _mm(f"ffn_down{l}", [(s, W["w_down"])], "nn", residual=x2)
    sv.update(h3=h3, a=a, b=b, s=s)
    return x3, sv, W


def _layer_weights(l, P, part):
    i = l // 2
    if part == "in":
        W = {"g_mix": P["g_mix"][l]}
        if l % 2 == 0:
            w_in = _mx(P["e_w_in"][i])
            if w_in.shape[1] < EVEN_IN_PAD:
                w_in = jnp.pad(w_in, ((0, 0), (0, EVEN_IN_PAD - w_in.shape[1])))
            W.update(w_in=w_in, g_v=P["e_g_v"][i].reshape(1, A_WIDTH), w_s=P["e_w_s"][i],
                     bs_cols=jnp.pad(P["e_b_s"][i].T, ((0, 0), (0, LANES - A_GROUPS))),
                     g_qn2=jnp.tile(P["e_g_qn"][i], 2).reshape(1, LANES), g_kn2=jnp.tile(P["e_g_kn"][i], 2).reshape(1, LANES),
                     b_f_pad=_pad_lanes(P["e_b_f"][i]))
        return W
    W = {k: P[k][l] for k in ("g_xa", "g_mem", "g_ffn")}
    W.update(wq=_mx(P["xa_wq"][l]), wkv=_mx(P["xa_wkv"][l]), wo=_mx(P["xa_wo"][l]), gq=P["xa_gq"][l], gk=P["xa_gk"][l],
             w_gate=_mx(P["w_gate"][l]), w_up=_mx(P["w_up"][l]), w_down=_mx(P["w_down"][l]))
    if l % 2 == 0:
        W.update(w_out=_mx(P["e_w_out"][i]))
    else:
        W.update(w_in=_mx(P["o_w_in"][i]), w_pool=_mx(P["o_w_pool"][i]), s_pool=P["o_s_pool"][i].reshape(1, POOL_WIDTH),
                 conv_w8=jnp.pad(P["o_conv_w"][i], ((0, SUBLANES - CONV_K), (0, 0))), w_out=_mx(P["o_w_out"][i]))
    return W


def _ffn_bwd_act(name, g, wd, a, b):
    T, N = a.shape
    K = g.shape[1]
    tm, tn = _tile(T, 512, 16), _tile(N, 1536, LANES)

    def body(g_ref, wd_ref, a_ref, b_ref, da_ref, db_ref):
        ds = _dot_nt(g_ref[...], wd_ref[...])
        av = a_ref[...].astype(F32)
        sig = jax.nn.sigmoid(av)
        da_ref[...] = (ds * b_ref[...].astype(F32) * (sig * (1.0 + av * (1.0 - sig)))).astype(da_ref.dtype)
        db_ref[...] = (ds * (av * sig)).astype(db_ref.dtype)

    o_spec = pl.BlockSpec((tm, tn), lambda j, i: (i, j))
    return pl.pallas_call(
        body, name=name, grid=(N // tn, T // tm),
        in_specs=[pl.BlockSpec((tm, K), lambda j, i: (i, 0)), pl.BlockSpec((tn, K), lambda j, i: (j, 0)), o_spec, o_spec],
        out_specs=[o_spec, o_spec],
        out_shape=[jax.ShapeDtypeStruct((T, N), _MXU), jax.ShapeDtypeStruct((T, N), _MXU)],
        compiler_params=_cp(("parallel", "parallel")),
    )(g, wd, a, b)


def _xa_bwd(name, qx, do, gq, kn, vx):
    T, M = qx.shape[0], kn.shape[0]
    tm = _tile(T, 512, 16)

    def body(q_ref, do_ref, gq_ref, k_ref, v_ref, dq_ref, dk_ref, dv_ref, dg_ref):
        @pl.when(pl.program_id(0) == 0)
        def _():
            dk_ref[...] = jnp.zeros_like(dk_ref)
            dv_ref[...] = jnp.zeros_like(dv_ref)
            dg_ref[...] = jnp.zeros_like(dg_ref)

        gqv = gq_ref[...]
        for h in range(XA_HEADS):
            sl = slice(h * LANES, (h + 1) * LANES)
            qhat, r, qn, p = _xa_probs(q_ref[:, sl], gqv, k_ref[:, sl])
            do_h = do_ref[:, sl]
            dp = _dot_nt(do_h, v_ref[:, sl])
            dv_ref[:, sl] += _dot_tn(p, do_h)
            dsm = p * (dp - jnp.sum(p * dp, axis=-1, keepdims=True)) * (1.0 / math.sqrt(XA_HEAD_DIM))
            dqn = _dot(dsm, k_ref[:, sl])
            dk_ref[:, sl] += _dot_tn(dsm, qn)
            dg_ref[...] += jnp.sum(dqn * qhat, axis=0, keepdims=True)
            dxh = dqn * gqv
            dq_ref[:, sl] = (r * (dxh - qhat * jnp.mean(dxh * qhat, axis=-1, keepdims=True))).astype(dq_ref.dtype)

    full = lambda s: pl.BlockSpec(s, lambda i: tuple(0 for _ in s))
    blk = pl.BlockSpec((tm, XA_WIDTH), lambda i: (i, 0))
    return pl.pallas_call(
        body, name=name, grid=(T // tm,),
        in_specs=[blk, blk, full((1, LANES)), full((M, XA_WIDTH)), full((M, XA_WIDTH))],
        out_specs=[blk, full((M, XA_WIDTH)), full((M, XA_WIDTH)), full((1, LANES))],
        out_shape=[jax.ShapeDtypeStruct((T, XA_WIDTH), _MXU), jax.ShapeDtypeStruct((M, XA_WIDTH), F32),
                   jax.ShapeDtypeStruct((M, XA_WIDTH), F32), jax.ShapeDtypeStruct((1, LANES), F32)],
        compiler_params=_cp(("arbitrary",)),
    )(qx, do, gq.reshape(1, XA_HEAD_DIM), kn, vx)


def _mem_bwd(name, mem, g_mem, m, kv, gk, dkn, dvx, wkv):
    M = mem.shape[0]

    def body(mem_ref, g_ref, m_ref, kv_ref, gk_ref, dkn_ref, dvx_ref, w_ref, dw_ref, dgk_ref, dgm_ref):
        gkv = gk_ref[...]
        dgk = jnp.zeros((1, LANES), F32)
        parts = []
        for h in range(XA_HEADS):
            sl = slice(h * LANES, (h + 1) * LANES)
            k = kv_ref[:, sl]
            r = _rms_rows(k)
            khat = k * r
            dkn_h = dkn_ref[:, sl]
            dgk = dgk + jnp.sum(dkn_h * khat, axis=0, keepdims=True)
            dxh = dkn_h * gkv
            parts.append(r * (dxh - khat * jnp.mean(dxh * khat, axis=-1, keepdims=True)))
        dgk_ref[...] = dgk
        dkv = jnp.concatenate(parts + [dvx_ref[...]], axis=1)
        dw_ref[...] = _dot_tn(m_ref[...], dkv)
        dm = _dot_nt(dkv, w_ref[...])
        x = mem_ref[...]
        dgm_ref[...] = jnp.sum(dm * (x * _rms_rows(x)), axis=0, keepdims=True)

    return pl.pallas_call(
        body, name=name,
        out_shape=[jax.ShapeDtypeStruct((D_MODEL, 2 * XA_WIDTH), F32), jax.ShapeDtypeStruct((1, LANES), F32),
                   jax.ShapeDtypeStruct((1, D_MODEL), F32)],
        compiler_params=pltpu.CompilerParams(vmem_limit_bytes=VMEM_LIMIT),
    )(mem, g_mem.reshape(1, D_MODEL), m, kv, gk.reshape(1, XA_HEAD_DIM), dkn, dvx, wkv)


def _fox_bwd_common(q2, k2, v2, dy2, yf2, c_ref, ct_ref, lse_ref, e, h, lo, diag, tq):
    sel = lo if e == 0 else ~lo
    qh = jnp.where(sel, q2, jnp.zeros_like(q2))
    kh = jnp.where(sel, k2, jnp.zeros_like(k2))
    dyh = jnp.where(sel, dy2, 0.0)
    s = _fox_logits(qh, k2, c_ref[:, h:h + 1], ct_ref[h:h + 1, :], diag, tq)
    p = jnp.exp(s - lse_ref[:, h:h + 1])
    dp = _dot_nt(dyh, v2)
    delta = jnp.sum(dyh * yf2, axis=-1, keepdims=True)
    return p, p * (dp - delta), qh, kh, dyh


def _fox_bwd(name, qn, kn, vv, c, ct, lse, dycat, ybf, flags, side=None):
    T = qn.shape[0]
    tq = _tile(T, FOX_TILE, LANES)
    nblk = T // tq
    qs, ks = _tri_steps(nblk, False)
    nsteps = int(qs.shape[0])
    scale = 1.0 / math.sqrt(FOX_HEAD_DIM)
    s_specs, s_args, s_shapes, s_scratch = _side_specs(side)

    def body(qs_ref, ks_ref, *refs):
        main_in, side_in, main_out, side_out, main_scr, sems = _split_side(refs, 9, 5, 3, side)
        q_ref, k_ref, v_ref, c_ref, ct_ref, lse_ref, dy_ref, yf_ref, flag_ref = main_in
        dq_hbm, drow_hbm, dk_ref, dv_ref, dct_ref = main_out
        dq_acc, drow_acc, out_sem = main_scr
        s_id = pl.program_id(0)
        qi, ki = qs_ref[s_id], ks_ref[s_id]
        _side_run(side, s_id, nsteps, side_in, side_out, sems)
        rows = pl.ds(pl.multiple_of(qi * tq, tq), tq)

        @pl.when(qi == ki)
        def _():
            dk_ref[...] = jnp.zeros_like(dk_ref)
            dv_ref[...] = jnp.zeros_like(dv_ref)
            dct_ref[...] = jnp.zeros_like(dct_ref)

        @pl.when(ki == 0)
        def _():
            dq_acc[rows, :] = jnp.zeros((tq, FOX_WIDTH), F32)
            drow_acc[rows, :] = jnp.zeros((tq, LANES), F32)

        lo = _lane((tq, LANES)) < FOX_HEAD_DIM

        def step(diag):
            for j in range(FOX_WIDTH // LANES):
                sl = slice(j * LANES, (j + 1) * LANES)
                for e in range(2):
                    h = 2 * j + e

                    def head(h=h, e=e, sl=sl):
                        q2, k2, v2, dy2, yf2 = q_ref[:, sl], k_ref[:, sl], v_ref[:, sl], dy_ref[:, sl], yf_ref[:, sl]
                        p, ds, qh, kh, dyh = _fox_bwd_common(q2, k2, v2, dy2, yf2, c_ref, ct_ref, lse_ref, e, h, lo, diag, tq)
                        drow_acc[rows, :] += jnp.where(_lane((tq, LANES)) == h, jnp.sum(ds, axis=-1, keepdims=True), 0.0)
                        dct_ref[h:h + 1, :] -= jnp.sum(ds, axis=0, keepdims=True)
                        dq_acc[rows, sl] += _dot(ds, kh) * scale
                        dk_ref[:, sl] += _dot_tn(ds, qh)
                        dv_ref[:, sl] += _dot_tn(p, dyh)

                    if diag:
                        head()
                    else:
                        pl.when(flag_ref[(qi * nblk + ki) * FOX_HEADS + h] > 0.5)(head)

        _on_diag_or_not(qi, ki, step)

        @pl.when(s_id == nsteps - 1)
        def _():
            for src, dst in ((dq_acc, dq_hbm), (drow_acc, drow_hbm)):
                cp = pltpu.make_async_copy(src, dst, out_sem)
                cp.start()
                cp.wait()

    qmap = lambda s, a, b: (a[s], 0)
    kmap = lambda s, a, b: (b[s], 0)
    ctmap = lambda s, a, b: (0, b[s])
    any_spec = pl.BlockSpec(memory_space=pl.ANY)
    grid_spec = pltpu.PrefetchScalarGridSpec(
        num_scalar_prefetch=2, grid=(nsteps,),
        in_specs=[pl.BlockSpec((tq, 512), qmap), pl.BlockSpec((tq, 512), kmap), pl.BlockSpec((tq, 512), kmap),
                  pl.BlockSpec((tq, LANES), qmap), pl.BlockSpec((SUBLANES, tq), ctmap),
                  pl.BlockSpec((tq, LANES), qmap), pl.BlockSpec((tq, 512), lambda s, a, b: (a[s], 1)), pl.BlockSpec((tq, 512), qmap),
                  pl.BlockSpec(memory_space=pltpu.SMEM)] + s_specs,
        out_specs=[any_spec, any_spec, pl.BlockSpec((tq, 512), kmap), pl.BlockSpec((tq, 512), kmap), pl.BlockSpec((SUBLANES, tq), ctmap)] + s_specs,
        scratch_shapes=[pltpu.VMEM((T, FOX_WIDTH), F32), pltpu.VMEM((T, LANES), F32), pltpu.SemaphoreType.DMA] + s_scratch,
    )
    res = pl.pallas_call(
        body, name=name, grid_spec=grid_spec,
        out_shape=[jax.ShapeDtypeStruct((T, 512), F32), jax.ShapeDtypeStruct((T, LANES), F32), jax.ShapeDtypeStruct((T, 512), F32),
                   jax.ShapeDtypeStruct((T, 512), F32), jax.ShapeDtypeStruct((SUBLANES, T), F32)] + s_shapes,
        compiler_params=_cp(("arbitrary",)),
    )(qs, ks, qn, kn, vv, c, ct, lse, dycat, ybf, flags, *s_args)
    return res[:5], res[5:]


def _fold64(row):
    return row + pltpu.roll(row, FOX_HEAD_DIM, axis=1)


def _even_bwd(name, z, dycat, dqn, dkn, dvv, dct, drow, g_v, w_s, bs_cols, g_qn2, g_kn2, b_f_pad):
    T = z.shape[0]
    tm = _tile(T, 256, A_BLOCK)
    nb, nt = tm // A_BLOCK, T // tm

    def body(z_ref, dya_ref, dq_ref, dk_ref, dv_ref, dct_ref, drow_ref, gv_ref, ws_ref, bs_ref, gq_ref, gk_ref, bf_ref,
             dz_ref, dws_ref, dbs_ref, dgv_ref, dgq_ref, dgk_ref, dbf_ref, carry):
        @pl.when(pl.program_id(0) == 0)
        def _():
            for r in (dws_ref, dbs_ref, dgv_ref, dgq_ref, dgk_ref, dbf_ref, carry):
                r[...] = jnp.zeros_like(r)

        keep = (_row((A_BLOCK, A_BLOCK)) // CHUNK) >= (_lane((A_BLOCK, A_BLOCK)) // CHUNK)
        for g in range(A_GROUPS):
            sl = slice(g * A_GROUP_DIM, (g + 1) * A_GROUP_DIM)
            wm = _mx(_masked_ws(ws_ref, g))
            zu = z_ref[:, sl]
            zv = z_ref[:, A_WIDTH + g * A_GROUP_DIM:A_WIDTH + (g + 1) * A_GROUP_DIM]
            u, v = _gelu(zu), _gelu(zv)
            r = _rms_rows(v)
            vhat = v * r
            gvv = gv_ref[:, sl]
            vn = _mx(vhat * gvv)
            dya = dya_ref[:, sl]
            du_parts, dvn_parts = [], []
            for n in range(nb):
                rows = slice(n * A_BLOCK, (n + 1) * A_BLOCK)
                s = jnp.dot(wm, vn[rows], preferred_element_type=F32) + bs_ref[:, g:g + 1]
                du_parts.append(dya[rows] * s)
                d_s = dya[rows] * u[rows]
                dbs_ref[...] += jnp.where(_lane((A_BLOCK, LANES)) == g, jnp.sum(d_s, axis=-1, keepdims=True), 0.0)
                dws_ref[g] += jnp.where(keep, _dot_nt(d_s, vn[rows]), 0.0)
                dvn_parts.append(_dot_tn(wm, d_s))
            dvn = jnp.concatenate(dvn_parts, axis=0)
            dgv_ref[:, sl] += jnp.sum(dvn * vhat, axis=0, keepdims=True)
            dxh = dvn * gvv
            dv = r * (dxh - vhat * jnp.mean(dxh * vhat, axis=-1, keepdims=True))
            dz_ref[:, sl] = (jnp.concatenate(du_parts, axis=0) * _gelu_grad(zu)).astype(dz_ref.dtype)
            dz_ref[:, A_WIDTH + g * A_GROUP_DIM:A_WIDTH + (g + 1) * A_GROUP_DIM] = (dv * _gelu_grad(zv)).astype(dz_ref.dtype)
        o = 2 * A_WIDTH
        for j in range(FOX_WIDTH // LANES):
            sl = slice(j * LANES, (j + 1) * LANES)
            for (off, d_ref, g_ref, dg_ref) in ((o, dq_ref, gq_ref, dgq_ref), (o + FOX_WIDTH, dk_ref, gk_ref, dgk_ref)):
                zq = z_ref[:, off + j * LANES:off + (j + 1) * LANES]
                r = _head64_rms(zq)
                qhat = zq * r
                dn = d_ref[:, sl]
                dg_ref[...] += _fold64(jnp.sum(dn * qhat, axis=0, keepdims=True))
                dxh = dn * g_ref[...]
                dz_ref[:, off + j * LANES:off + (j + 1) * LANES] = (r * (dxh - qhat * _head64_mean(dxh * qhat))).astype(dz_ref.dtype)
            dz_ref[:, o + 2 * FOX_WIDTH + j * LANES:o + 2 * FOX_WIDTH + (j + 1) * LANES] = dv_ref[:, sl].astype(dz_ref.dtype)
        dct_v = dct_ref[...] + drow_ref[...].T[0:SUBLANES, :]
        upper = (_row((tm, tm)) >= _lane((tm, tm))).astype(F32)
        dlf_t = _dot_f32(dct_v, upper) + carry[:, 0:1]
        carry[...] += jnp.sum(dct_v, axis=1, keepdims=True)
        dlf = jnp.concatenate([dlf_t, jnp.zeros((LANES - SUBLANES, tm), F32)], axis=0).T
        zf = z_ref[:, o + 3 * FOX_WIDTH:o + 3 * FOX_WIDTH + LANES]
        dzf = dlf * jax.nn.sigmoid(-(zf + bf_ref[...]))
        dbf_ref[...] += jnp.sum(dzf, axis=0, keepdims=True)
        dz_ref[:, o + 3 * FOX_WIDTH:o + 3 * FOX_WIDTH + LANES] = dzf.astype(dz_ref.dtype)

    rev = lambda i: nt - 1 - i
    wide = lambda n, col=0: pl.BlockSpec((tm, n), lambda i: (rev(i), col))
    full = lambda s: pl.BlockSpec(s, lambda i: tuple(0 for _ in s))
    vec = full((1, LANES))
    return pl.pallas_call(
        body, name=name, grid=(nt,),
        in_specs=[wide(EVEN_IN_PAD), wide(512), wide(512), wide(512), wide(512), pl.BlockSpec((SUBLANES, tm), lambda i: (0, rev(i))),
                  wide(LANES), full((1, A_WIDTH)), full((A_GROUPS, A_BLOCK, A_BLOCK)), full((A_BLOCK, LANES)), vec, vec, vec],
        out_specs=[wide(EVEN_IN_PAD), full((A_GROUPS, A_BLOCK, A_BLOCK)), full((A_BLOCK, LANES)), full((1, A_WIDTH)), vec, vec, vec],
        out_shape=[jax.ShapeDtypeStruct((T, EVEN_IN_PAD), _MXU), jax.ShapeDtypeStruct((A_GROUPS, A_BLOCK, A_BLOCK), F32),
                   jax.ShapeDtypeStruct((A_BLOCK, LANES), F32), jax.ShapeDtypeStruct((1, A_WIDTH), F32),
                   jax.ShapeDtypeStruct((1, LANES), F32), jax.ShapeDtypeStruct((1, LANES), F32), jax.ShapeDtypeStruct((1, LANES), F32)],
        scratch_shapes=[pltpu.VMEM((SUBLANES, LANES), F32)],
        compiler_params=_cp(("arbitrary",)),
    )(z, dycat, dqn, dkn, dvv, dct, drow, g_v, w_s, bs_cols, g_qn2, g_kn2, b_f_pad)


def _shift_up(ext, k, tm):
    return pltpu.roll(ext, ext.shape[0] - k, axis=0)[0:tm]


def _odd_bwd(name, z, dycat, w_pool, s_pool, conv_w8):
    T = z.shape[0]
    tm = _tile(T, 256, HALO)
    r, nt = tm // HALO, T // tm
    n_ext = tm + HALO

    def body(z_ref, zb_ref, zn_ref, dy_ref, dyn_ref, wp_ref, sp_ref, cw_ref, dz_ref, dwp_ref, dsp_ref, dcw_ref):
        i = pl.program_id(0)

        @pl.when(i == 0)
        def _():
            for rr in (dwp_ref, dsp_ref, dcw_ref):
                rr[...] = jnp.zeros_like(rr)

        first, last = i == 0, i == nt - 1
        pos = i * tm + _row((tm, LANES))
        pos_ext = i * tm + _row((n_ext, LANES))
        for g, w in enumerate(POOL_WINDOWS):
            sl = slice(g * LANES, (g + 1) * LANES)
            z_ext = jnp.concatenate([jnp.where(first, 0.0, zb_ref[:, sl]), z_ref[:, sl]], axis=0)
            p = _pool_p(z_ext, g, tm, pos)
            spv = sp_ref[:, sl]
            dyc = dy_ref[:, sl]
            dsp_ref[:, sl] += jnp.sum(dyc * _dot(p, wp_ref[g]), axis=0, keepdims=True)
            dpw_ext = jnp.concatenate([dyc, jnp.where(last, 0.0, dyn_ref[:, sl])], axis=0) * spv
            dwp_ref[g] += _dot_tn(p, dpw_ext[0:tm])
            dp_ext = _dot_nt(dpw_ext, wp_ref[g])
            f = dp_ext / jnp.minimum(pos_ext + 1, w).astype(F32)
            span = 1
            while span < w:
                f = f + pltpu.roll(f, n_ext - span, axis=0)
                span *= 2
            dz_ref[:, sl] = (f[0:tm] - dp_ext[0:tm]).astype(dz_ref.dtype)
        for j in range(CONV_WIDTH // LANES):
            sl = slice(j * LANES, (j + 1) * LANES)
            hd = slice(POOL_WIDTH + j * LANES, POOL_WIDTH + (j + 1) * LANES)
            gb = slice(POOL_WIDTH + CONV_WIDTH + j * LANES, POOL_WIDTH + CONV_WIDTH + (j + 1) * LANES)
            gc = slice(POOL_WIDTH + 2 * CONV_WIDTH + j * LANES, POOL_WIDTH + 2 * CONV_WIDTH + (j + 1) * LANES)
            ysl = slice(POOL_WIDTH + j * LANES, POOL_WIDTH + (j + 1) * LANES)
            hdv, gbv, gcv = z_ref[:, hd], z_ref[:, gb], z_ref[:, gc]
            xg = gcv * hdv
            xg_ext = jnp.concatenate([jnp.where(first, 0.0, zb_ref[:, gc] * zb_ref[:, hd]), xg], axis=0)
            xg1, xg2 = _shift_down(xg_ext, 1, tm), _shift_down(xg_ext, 2, tm)
            w0, w1, w2 = cw_ref[0:1, sl], cw_ref[1:2, sl], cw_ref[2:3, sl]
            conv = w0 * xg2 + w1 * xg1 + w2 * xg
            dyd = dy_ref[:, ysl]
            dconv = dyd * gbv
            dconv_ext = jnp.concatenate([dconv, jnp.where(last, 0.0, dyn_ref[:, ysl] * zn_ref[:, gb])], axis=0)
            dcw_ref[0:1, sl] += jnp.sum(dconv * xg2, axis=0, keepdims=True)
            dcw_ref[1:2, sl] += jnp.sum(dconv * xg1, axis=0, keepdims=True)
            dcw_ref[2:3, sl] += jnp.sum(dconv * xg, axis=0, keepdims=True)
            dxg = w2 * dconv + w1 * _shift_up(dconv_ext, 1, tm) + w0 * _shift_up(dconv_ext, 2, tm)
            dz_ref[:, hd] = (dxg * gcv).astype(dz_ref.dtype)
            dz_ref[:, gb] = (dyd * conv).astype(dz_ref.dtype)
            dz_ref[:, gc] = (dxg * hdv).astype(dz_ref.dtype)

    full = lambda s: pl.BlockSpec(s, lambda i: tuple(0 for _ in s))
    back = lambda n: pl.BlockSpec((HALO, n), lambda i: (jnp.maximum(i * r - 1, 0), 0))
    nxt = lambda n: pl.BlockSpec((HALO, n), lambda i: (jnp.minimum((i + 1) * r, T // HALO - 1), 0))
    return pl.pallas_call(
        body, name=name, grid=(nt,),
        in_specs=[pl.BlockSpec((tm, ODD_IN), lambda i: (i, 0)), back(ODD_IN), nxt(ODD_IN),
                  pl.BlockSpec((tm, 1024), lambda i: (i, 0)), nxt(1024),
                  full((4, LANES, LANES)), full((1, POOL_WIDTH)), full((SUBLANES, CONV_WIDTH))],
        out_specs=[pl.BlockSpec((tm, ODD_IN), lambda i: (i, 0)), full((4, LANES, LANES)), full((1, POOL_WIDTH)), full((SUBLANES, CONV_WIDTH))],
        out_shape=[jax.ShapeDtypeStruct((T, ODD_IN), _MXU), jax.ShapeDtypeStruct((4, LANES, LANES), F32),
                   jax.ShapeDtypeStruct((1, POOL_WIDTH), F32), jax.ShapeDtypeStruct((SUBLANES, CONV_WIDTH), F32)],
        compiler_params=_cp(("arbitrary",)),
    )(z, z, z, dycat, dycat, w_pool, s_pool, conv_w8)


def _layer_bwd(l, g3, mem, W, sv, side_fn=None):
    i = l // 2
    G = {}
    side_out = ()
    da, db = _ffn_bwd_act(f"ffn_bwd_act{l}", g3, W["w_down"], sv["a"], sv["b"])
    G["w_down"] = _mm(f"d_w_down{l}", [(sv["s"], g3)], "tn")
    g2, dg = _mm(f"d_h3{l}", [(da, W["w_gate"]), (db, W["w_up"])], "nt", norm_bwd=(sv["x2"], W["g_ffn"], g3))
    G["w_gate"] = _mm(f"d_w_gate{l}", [(sv["h3"], da)], "tn")
    G["w_up"] = _mm(f"d_w_up{l}", [(sv["h3"], db)], "tn")
    G["g_ffn"] = dg[0]

    do = _mm(f"d_o{l}", [(g2, W["wo"])], "nt", out_dtype=_MXU)
    G["xa_wo"] = _mm(f"d_wo{l}", [(sv["o"], g2)], "tn")
    dqx, dkn, dvx, dgq = _xa_bwd(f"xa_bwd{l}", sv["qx"], do, W["gq"], sv["kn_x"], sv["vx"])
    G["xa_gq"] = dgq[0]
    g1, dg = _mm(f"d_h2{l}", [(dqx, W["wq"])], "nt", norm_bwd=(sv["x1"], W["g_xa"], g2))
    G["xa_wq"] = _mm(f"d_wq{l}", [(sv["h2"], dqx)], "tn")
    G["g_xa"] = dg[0]
    dwkv, dgk, dgm = _mem_bwd(f"mem_bwd{l}", mem, W["g_mem"], sv["m"], sv["kv"], W["gk"], dkn, dvx, W["wkv"])
    G["xa_wkv"], G["xa_gk"], G["g_mem"] = dwkv, dgk[0], dgm[0]

    dycat = _mm(f"d_ycat{l}", [(g1, W["w_out"])], "nt")
    if l % 2 == 0:
        G["e_w_out"] = jnp.concatenate([_mm(f"d_e_wout_a{l}", [(sv["ya"], g1)], "tn"), _mm(f"d_e_wout_b{l}", [(sv["yb"], g1)], "tn")], axis=0)
        fa = (sv["qn"], sv["kn"], sv["vv"], sv["c"], sv["ct"], sv["lse"], dycat, sv["ybf"], sv["flags"])
        (dqn, drow, dkn_f, dvv, dct), side_out = _fox_bwd(f"fox_bwd{l}", *fa, side=side_fn(G) if side_fn is not None else None)
        dz, dws, dbs, dgv, dgq2, dgk2, dbf = _even_bwd(f"even_bwd{l}", sv["z"], dycat, dqn, dkn_f, dvv, dct, drow, W["g_v"], W["w_s"],
                                                      W["bs_cols"], W["g_qn2"], W["g_kn2"], W["b_f_pad"])
        G.update(e_w_s=dws, e_b_s=dbs[:, :A_GROUPS].T, e_g_v=dgv[0], e_g_qn=dgq2[0, :FOX_HEAD_DIM], e_g_kn=dgk2[0, :FOX_HEAD_DIM],
                 e_b_f=dbf[0, :FOX_HEADS])
        G["e_w_in"] = _mm(f"d_e_win{l}", [(sv["h1"], dz)], "tn")
    else:
        G["o_w_out"] = _mm(f"d_o_wout{l}", [(sv["ycat"], g1)], "tn")
        dz, dwp, dsp, dcw = _odd_bwd(f"odd_bwd{l}", sv["z"], dycat, W["w_pool"], W["s_pool"], W["conv_w8"])
        G.update(o_w_pool=dwp, o_s_pool=dsp[0], o_conv_w=dcw[:CONV_K])
        G["o_w_in"] = _mm(f"d_o_win{l}", [(sv["h1"], dz)], "tn")
    g0, dg = _mm(f"d_h1{l}", [(dz, W["w_in"])], "nt", norm_bwd=(sv["x0"], W["g_mix"], g1))
    G["g_mix"] = dg[0]
    return g0, G, side_out


PACK_W = 1024
MESH_T = pl.DeviceIdType.MESH
_ANY = pl.BlockSpec(memory_space=pl.ANY)


def _my_place():
    x, y, c = lax.axis_index("x"), lax.axis_index("y"), lax.axis_index("c")
    return x, y, c


def _flip(v, bit):
    return 1 - v if bit else v


N_PEERS = N_DEV - 1


def _all_gather(name, blks):
    side = _TwoLevelGather(blks)

    def body(*refs):
        _, side_in, _, side_out, _, sems = _split_side(refs, 0, 0, 0, side)
        side.start(side_in, side_out, sems)
        side.middle(side_in, side_out, sems)
        side.finish(side_in, side_out, sems)

    return pl.pallas_call(
        body, name=name, out_shape=side.out_shape,
        in_specs=[_ANY] * side.n, out_specs=[_ANY] * side.n, scratch_shapes=side.scratch,
    )(*blks)


class _Direct:
    def __init__(self, kind, arrays):
        self.kind, self.arrays, self.n = kind, list(arrays), len(arrays)
        if kind == "scatter":
            self.out_shape = [jax.ShapeDtypeStruct(a.shape, a.dtype) for a in arrays]
        else:
            self.out_shape = [jax.ShapeDtypeStruct((N_DEV,) + a.shape, a.dtype) for a in arrays]
        self.scratch = [pltpu.SemaphoreType.DMA((N_PEERS * self.n,)), pltpu.SemaphoreType.DMA((N_PEERS * self.n,)),
                        pltpu.SemaphoreType.DMA((self.n,))]

    def _copies(self, in_refs, out_refs, sems):
        send_sems, recv_sems, local_sems = sems
        x, y, c = _my_place()
        me = 4 * x + 2 * y + c
        mine, sends, recvs = [], [], []
        for t in range(self.n):
            src_of = (lambda idx, t=t: in_refs[t].at[idx]) if self.kind == "scatter" else (lambda idx, t=t: in_refs[t])
            mine.append(pltpu.make_async_copy(src_of(me), out_refs[t].at[me], local_sems.at[t]))
            for m in range(1, N_DEV):
                px, py, pc = _flip(x, m & 4), _flip(y, m & 2), _flip(c, m & 1)
                pidx = 4 * px + 2 * py + pc
                sem = dict(send_sem=send_sems.at[N_PEERS * t + m - 1], recv_sem=recv_sems.at[N_PEERS * t + m - 1], device_id_type=MESH_T)
                sends.append(pltpu.make_async_remote_copy(src_ref=src_of(pidx), dst_ref=out_refs[t].at[me], device_id=(px, py, pc), **sem))
                recvs.append(pltpu.make_async_remote_copy(src_ref=src_of(pidx), dst_ref=out_refs[t].at[pidx], device_id=(x, y, c), **sem))
        return mine, sends, recvs

    def start(self, in_refs, out_refs, sems):
        mine, sends, _ = self._copies(in_refs, out_refs, sems)
        for cp in mine + sends:
            cp.start()

    def middle(self, in_refs, out_refs, sems):
        pass

    def finish(self, in_refs, out_refs, sems):
        mine, sends, recvs = self._copies(in_refs, out_refs, sems)
        for cp in recvs:
            cp.wait_recv()
        for cp in sends:
            cp.wait_send()
        for cp in mine:
            cp.wait()


class _TwoLevelGather:
    def __init__(self, arrays):
        self.arrays, self.n = list(arrays), len(arrays)
        self.out_shape = [jax.ShapeDtypeStruct((N_DEV,) + a.shape, a.dtype) for a in arrays]
        self.scratch = [pltpu.SemaphoreType.DMA((N_PEERS * self.n,)), pltpu.SemaphoreType.DMA((N_PEERS * self.n,)),
                        pltpu.SemaphoreType.DMA((self.n,))]

    def _place(self):
        x, y, c = _my_place()
        return (x, y, c), (x, y, 1 - c), [(1 - x, y), (x, 1 - y), (1 - x, 1 - y)]

    def _copy(self, in_refs, out_refs, sems, t, k, block, to, own=False):
        slab = out_refs[t].at[4 * block[0] + 2 * block[1] + block[2]]
        return pltpu.make_async_remote_copy(
            src_ref=in_refs[t] if own else slab, dst_ref=slab, send_sem=sems[0].at[N_PEERS * t + k],
            recv_sem=sems[1].at[N_PEERS * t + k], device_id=to, device_id_type=MESH_T)

    def _mine(self, in_refs, out_refs, sems):
        x, y, c = _my_place()
        return [pltpu.make_async_copy(in_refs[t], out_refs[t].at[4 * x + 2 * y + c], sems[2].at[t]) for t in range(self.n)]

    def _first(self, in_refs, out_refs, sems):
        me, sibling, chips = self._place()
        cps = []
        for t in range(self.n):
            cps.append(self._copy(in_refs, out_refs, sems, t, 0, me, sibling, own=True))
            cps += [self._copy(in_refs, out_refs, sems, t, 1 + j, me, (*chip, me[2]), own=True) for j, chip in enumerate(chips)]
        return cps

    def _passed(self, in_refs, out_refs, sems):
        me, sibling, chips = self._place()
        return [self._copy(in_refs, out_refs, sems, t, 4 + j, (*chip, me[2]), sibling) for j, chip in enumerate(chips) for t in range(self.n)]

    def start(self, in_refs, out_refs, sems):
        for cp in self._mine(in_refs, out_refs, sems) + self._first(in_refs, out_refs, sems):
            cp.start()

    def middle(self, in_refs, out_refs, sems):
        me, _, chips = self._place()
        for j, chip in enumerate(chips):
            for t in range(self.n):
                self._copy(in_refs, out_refs, sems, t, 1 + j, (*chip, me[2]), me).wait_recv()
        for cp in self._passed(in_refs, out_refs, sems):
            cp.start()

    def finish(self, in_refs, out_refs, sems):
        me, sibling, chips = self._place()
        for t in range(self.n):
            self._copy(in_refs, out_refs, sems, t, 0, sibling, me).wait_recv()
            for j, chip in enumerate(chips):
                self._copy(in_refs, out_refs, sems, t, 4 + j, (*chip, 1 - me[2]), me).wait_recv()
        for cp in self._first(in_refs, out_refs, sems) + self._passed(in_refs, out_refs, sems):
            cp.wait_send()
        for cp in self._mine(in_refs, out_refs, sems):
            cp.wait()


def _split_side(refs, n_main_in, n_main_out, n_main_scratch, side):
    ns = side.n if side is not None else 0
    i = 0
    main_in = refs[i:i + n_main_in]; i += n_main_in
    side_in = refs[i:i + ns]; i += ns
    main_out = refs[i:i + n_main_out]; i += n_main_out
    side_out = refs[i:i + ns]; i += ns
    main_scr = refs[i:i + n_main_scratch]; i += n_main_scratch
    return main_in, side_in, main_out, side_out, main_scr, refs[i:]


def _all_to_all(name, gs):
    side = _Direct("scatter", gs)

    def body(*refs):
        _, side_in, _, side_out, _, sems = _split_side(refs, 0, 0, 0, side)
        side.start(side_in, side_out, sems)
        side.finish(side_in, side_out, sems)

    return pl.pallas_call(
        body, name=name, out_shape=side.out_shape,
        in_specs=[_ANY] * side.n, out_specs=[_ANY] * side.n, scratch_shapes=side.scratch,
    )(*gs)


def _unblock(name, blocks, layer, n_out):
    _, _, K, nb = blocks.shape
    tk = _tile(K, 256, 16)

    def body(b_ref, o_ref):
        for j in range(N_DEV):
            o_ref[:, j * nb:(j + 1) * nb] = b_ref[j]
        if n_out > N_DEV * nb:
            o_ref[:, N_DEV * nb:n_out] = jnp.zeros((tk, n_out - N_DEV * nb), o_ref.dtype)

    return pl.pallas_call(
        body, name=name, grid=(K // tk,),
        in_specs=[pl.BlockSpec((N_DEV, None, tk, nb), lambda i: (0, layer, i, 0))],
        out_specs=pl.BlockSpec((tk, n_out), lambda i: (i, 0)),
        out_shape=jax.ShapeDtypeStruct((K, n_out), blocks.dtype),
        compiler_params=_cp(("parallel",)),
    )(blocks)


def _block(name, full, nb, dtype):
    K, N = full.shape
    tk = _tile(K, 256, 16)

    def body(f_ref, o_ref):
        for j in range(N_DEV):
            o_ref[j] = f_ref[:, j * nb:(j + 1) * nb].astype(o_ref.dtype)

    return pl.pallas_call(
        body, name=name, grid=(K // tk,),
        in_specs=[pl.BlockSpec((tk, N), lambda i: (i, 0))],
        out_specs=pl.BlockSpec((N_DEV, tk, nb), lambda i: (0, i, 0)),
        out_shape=jax.ShapeDtypeStruct((N_DEV, K, nb), dtype),
        compiler_params=_cp(("parallel",)),
    )(full)


ADAM_BLOCK_ELEMS = 256 * 1024
GRAD_WIRE = jnp.bfloat16


def _adamw(name, parts, w, m, v, layer=0):
    shape = w.shape[1:]
    cols = shape[-1]
    rows = math.prod(shape[:-1])
    nl = w.shape[0]
    parts, w, m, v = parts.reshape(N_DEV, rows, cols), w.reshape(nl, rows, cols), m.reshape(nl, rows, cols), v.reshape(nl, rows, cols)
    tr = _tile(rows, max(16, ADAM_BLOCK_ELEMS // cols), 16)

    def body(p_ref, w_ref, m_ref, v_ref, g_ref, d_ref, mo_ref, vo_ref):
        g = p_ref[0].astype(F32)
        for s in range(1, N_DEV):
            g = g + p_ref[s].astype(F32)
        mn = ADAM_B1 * m_ref[...] + (1.0 - ADAM_B1) * g
        vn = ADAM_B2 * v_ref[...] + (1.0 - ADAM_B2) * jnp.square(g)
        m_hat = mn / (1.0 - ADAM_B1 ** ADAM_STEP)
        v_hat = vn / (1.0 - ADAM_B2 ** ADAM_STEP)
        g_ref[...] = g
        d_ref[...] = -ADAM_LR * (m_hat / (jnp.sqrt(v_hat) + ADAM_EPS) + ADAM_WD * w_ref[...])
        mo_ref[...] = mn
        vo_ref[...] = vn

    blk = pl.BlockSpec((tr, cols), lambda i: (i, 0))
    lblk = pl.BlockSpec((None, tr, cols), lambda i: (layer, i, 0))
    shp = jax.ShapeDtypeStruct((rows, cols), F32)
    res = pl.pallas_call(
        body, name=name, grid=(rows // tr,),
        in_specs=[pl.BlockSpec((N_DEV, tr, cols), lambda i: (0, i, 0)), lblk, lblk, lblk],
        out_specs=[blk, blk, blk, blk], out_shape=[shp, shp, shp, shp],
        compiler_params=_cp(("parallel",)),
    )(parts, w, m, v)
    return [r.reshape(shape) for r in res]


SHARDED = dict(xa_wq=1, xa_wkv=1, xa_wo=2, w_gate=2, w_up=2, w_down=1, e_w_in=2, e_w_out=1, o_w_in=2, o_s_pool=1, o_conv_w=2, o_w_out=1)
LAYER_SPLIT = ("xa_wq", "xa_wkv", "xa_wo", "w_gate", "w_up", "w_down")
ELEMENTWISE_SHARDED = ("o_s_pool", "o_conv_w")
REPLICATED = ("g_mix", "g_xa", "g_mem", "xa_gq", "xa_gk", "g_ffn", "e_b_f", "e_g_v", "e_w_s", "e_b_s", "e_g_qn", "e_g_kn", "o_w_pool")
WEIGHTS = ("g_mix", "g_xa", "g_mem", "xa_wq", "xa_wkv", "xa_wo", "xa_gq", "xa_gk", "g_ffn", "w_gate", "w_up", "w_down", "e_w_in", "e_b_f",
           "e_g_v", "e_w_s", "e_b_s", "e_g_qn", "e_g_kn", "e_w_out", "o_w_in", "o_w_pool", "o_s_pool", "o_conv_w", "o_w_out")


def _rows_for(n, mult):
    return -(-n // (PACK_W * mult)) * mult


def _pack(arrs, rows, dtype):
    flat = jnp.concatenate([a.reshape(-1).astype(dtype) for a in arrs])
    return jnp.pad(flat, (0, rows * PACK_W - flat.shape[0])).reshape(rows, PACK_W)


def _unpack(slab, shapes):
    flat, out, off = slab.reshape(-1), [], 0
    for s in shapes:
        n = math.prod(s)
        out.append(flat[off:off + n].reshape(s))
        off += n
    return out


def _to_blocks(full, axis):
    s = full.shape
    return jnp.moveaxis(full.reshape(s[:axis] + (N_DEV, s[axis] // N_DEV) + s[axis + 1:]), axis, 0)


def _from_blocks(blocks, axis):
    b = jnp.moveaxis(blocks, 0, axis)
    s = b.shape
    return b.reshape(s[:axis] + (s[axis] * s[axis + 1],) + s[axis + 2:])


def kernel(x, mem, g_mix, g_xa, g_mem, xa_wq, xa_wkv, xa_wo, xa_gq, xa_gk, g_ffn, w_gate, w_up, w_down, e_w_in, e_b_f, e_g_v, e_w_s, e_b_s, e_g_qn, e_g_kn, e_w_out, o_w_in, o_w_pool, o_s_pool, o_conv_w, o_w_out, loss_target, m_g_mix, m_g_xa, m_g_mem, m_xa_wq, m_xa_wkv, m_xa_wo, m_xa_gq, m_xa_gk, m_g_ffn, m_w_gate, m_w_up, m_w_down, m_e_w_in, m_e_b_f, m_e_g_v, m_e_w_s, m_e_b_s, m_e_g_qn, m_e_g_kn, m_e_w_out, m_o_w_in, m_o_w_pool, m_o_s_pool, m_o_conv_w, m_o_w_out, v_g_mix, v_g_xa, v_g_mem, v_xa_wq, v_xa_wkv, v_xa_wo, v_xa_gq, v_xa_gk, v_g_ffn, v_w_gate, v_w_up, v_w_down, v_e_w_in, v_e_b_f, v_e_g_v, v_e_w_s, v_e_b_s, v_e_g_qn, v_e_g_kn, v_e_w_out, v_o_w_in, v_o_w_pool, v_o_s_pool, v_o_conv_w, v_o_w_out):
    args = dict(locals())
    Wt = {n: args[n] for n in WEIGHTS}
    Mo = {n: args["m_" + n] for n in WEIGHTS}
    Vo = {n: args["v_" + n] for n in WEIGHTS}
    rep_names = REPLICATED
    rep_shapes = [Wt[n].shape for n in rep_names]
    rep_rows = _rows_for(sum(math.prod(s) for s in rep_shapes) + 1, 16)
    layer_names = [LAYER_SPLIT + ("e_w_in", "e_w_out"), LAYER_SPLIT + ("o_w_in", "o_w_out", "o_s_pool", "o_conv_w")]

    def shard_of(n, l, src=Wt):
        return src[n][l:l + 1] if n in LAYER_SPLIT else src[n]

    def to_wire(n, a):
        return a if n in ELEMENTWISE_SHARDED else _mx(a)

    def full_weight(n, l, gn):
        if SHARDED[n] == 1 or n in ELEMENTWISE_SHARDED:
            return _from_blocks(gn, SHARDED[n])[0]
        return _unblock(f"unblock_{n}{l}", gn, 0, EVEN_IN_PAD if n == "e_w_in" else N_DEV * gn.shape[-1])

    def grad_slabs(n, l, gl):
        if n in ELEMENTWISE_SHARDED:
            return _to_blocks(gl.reshape((1,) + Wt[n].shape[1:-1] + (-1,)), SHARDED[n])
        if SHARDED[n] == 1:
            return _to_blocks(gl[None], 1).astype(GRAD_WIRE)
        return _block(f"block_{n}{l}", gl, Wt[n].shape[-1], GRAD_WIRE)[:, None]

    xk, mem2 = x[0], mem[0]
    later = [(n, 0) for n in layer_names[0] if n != "e_w_in"] + [(n, 1) for n in layer_names[1]]
    w_in0 = full_weight("e_w_in", 0, _all_gather("gather_e_w_in", [to_wire("e_w_in", Wt["e_w_in"])])[0])
    side = _TwoLevelGather([to_wire(n, shard_of(n, l)) for n, l in later])
    P = {n: Wt[n] for n in rep_names}
    P["e_w_in"] = {0: w_in0}

    def on_side(gathered):
        for (n, l), gn in zip(later, gathered):
            P.setdefault(n, {})[l if n in LAYER_SPLIT else 0] = full_weight(n, l, gn)

    xk, sv0, W0 = _layer_fwd(0, xk, mem2, P, side=side, on_side=on_side)
    xk, sv1, W1 = _layer_fwd(1, xk, mem2, P)

    g, loss_row = _loss_grad("loss_grad", xk, loss_target[0])
    g, G1, _ = _layer_bwd(1, g, mem2, W1, sv1)
    slabs1 = [grad_slabs(n, 1, G1[n]) for n in layer_names[1]]
    early0 = LAYER_SPLIT + ("e_w_out",)
    g, G0, beside = _layer_bwd(0, g, mem2, W0, sv0,
                               side_fn=lambda G: _Direct("scatter", slabs1 + [grad_slabs(n, 0, G[n]) for n in early0]))
    parts1, parts0a = beside[:len(slabs1)], beside[len(slabs1):]
    grad_x = g[None]
    parts0b = _all_to_all("exchange_grads_mixer0", [grad_slabs("e_w_in", 0, G0["e_w_in"])])
    parts = {(n, 1): p for n, p in zip(layer_names[1], parts1)}
    parts.update({(n, 0): p for n, p in zip(early0, parts0a)})
    parts[("e_w_in", 0)] = parts0b[0]
    G = {n: {0: G0[n]} for n in G0}
    for n in G1:
        G.setdefault(n, {})[1] = G1[n]
    rep_g = [jnp.stack([G[n][l] for l in sorted(G[n])]).reshape(Wt[n].shape) for n in rep_names]
    rep_parts = _all_gather("gather_small_grads", [_pack(rep_g + [jnp.sum(loss_row).reshape(1)], rep_rows, F32)])[0]

    outs = {}
    for n in SHARDED:
        per_layer = [_adamw(f"adamw_{n}{l}", parts[(n, l if n in LAYER_SPLIT else (0 if n.startswith("e_") else 1))], Wt[n], Mo[n], Vo[n], layer=l)
                     for l in range(Wt[n].shape[0])]
        for k, kind in enumerate(("grad", "delta", "new_m", "new_v")):
            outs[kind + "_" + n] = jnp.stack([r[k] for r in per_layer])
    res = _adamw("adamw_replicated", rep_parts, _pack([Wt[n] for n in rep_names], rep_rows, F32)[None],
                 _pack([Mo[n] for n in rep_names], rep_rows, F32)[None], _pack([Vo[n] for n in rep_names], rep_rows, F32)[None])
    for kind, slab in zip(("grad", "delta", "new_m", "new_v"), res):
        for n, a in zip(rep_names, _unpack(slab, rep_shapes)):
            outs[kind + "_" + n] = a
    loss = res[0].reshape(-1)[sum(math.prod(s) for s in rep_shapes)]
    return (loss, grad_x, *[outs[k + "_" + n] for k in ("grad", "delta", "new_m", "new_v") for n in WEIGHTS])
```

```python
import functools
import math

import numpy as np
import jax
import jax.numpy as jnp
from jax import lax
from jax.experimental import pallas as pl
from jax.experimental.pallas import tpu as pltpu

F32 = jnp.float32
BF16 = jnp.bfloat16
_MXU = jnp.bfloat16

D_MODEL = 1024
N_DEV = 8
LANES = 128
SUBLANES = 8
HALO = 16
CHUNK = 64
A_GROUPS, A_GROUP_DIM, A_WIDTH, A_BLOCK = 4, 128, 512, 128
FOX_HEADS, FOX_HEAD_DIM, FOX_WIDTH = 8, 64, 512
FOX_TILE = 512
POOL_WINDOWS = (2, 4, 8, 16)
POOL_WIDTH, CONV_WIDTH, CONV_K = 512, 512, 3
EVEN_IN, EVEN_IN_PAD, ODD_IN = 2568, 2688, 2048
XA_HEADS, XA_HEAD_DIM, XA_WIDTH = 4, 128, 512
D_FF = 2816
EPS = 1e-6
ADAM_LR, ADAM_B1, ADAM_B2, ADAM_EPS, ADAM_WD, ADAM_STEP = 0.001, 0.9, 0.999, 1e-08, 0.01, 10
VMEM_LIMIT = 48 * 1024 * 1024


def _cp(sem):
    return pltpu.CompilerParams(dimension_semantics=sem, vmem_limit_bytes=VMEM_LIMIT)


def _tile(n, cap, q):
    best = None
    for d in range(q, min(n, cap) + 1, q):
        if n % d == 0:
            best = d
    return best if best is not None else n


def _mx(a):
    return a.astype(_MXU)


def _dot(a, b):
    return jnp.dot(_mx(a), _mx(b), preferred_element_type=F32)


def _dot_nt(a, b):
    return lax.dot_general(_mx(a), _mx(b), (((1,), (1,)), ((), ())), preferred_element_type=F32)


def _dot_tn(a, b):
    return lax.dot_general(_mx(a), _mx(b), (((0,), (0,)), ((), ())), preferred_element_type=F32)


def _dot_f32(a, b):
    return jnp.dot(a, b, precision=lax.Precision.HIGHEST, preferred_element_type=F32)


def _lane(shape):
    return lax.broadcasted_iota(jnp.int32, shape, len(shape) - 1)


def _row(shape):
    return lax.broadcasted_iota(jnp.int32, shape, len(shape) - 2)


def _mm(name, pairs, mode, out_dtype=F32, residual=None, norm_bwd=None, tm_cap=512, tn_cap=1536, tk_cap=2048):
    a0, b0 = pairs[0]
    if mode == "nn":
        (M, K), N = a0.shape, b0.shape[1]
    elif mode == "nt":
        (M, K), N = a0.shape, b0.shape[0]
    else:
        (K, M), N = a0.shape, b0.shape[1]
    if mode == "tn":
        tm, tk = _tile(M, 1408, LANES), _tile(K, 512, 16)
    else:
        tm, tk = _tile(M, tm_cap if norm_bwd is not None else 2 * tm_cap, 16), _tile(K, tk_cap, LANES)
    tn = _tile(N, tn_cap, LANES)
    nk = K // tk
    npairs = len(pairs)
    dot = {"nn": _dot, "nt": _dot_nt, "tn": _dot_tn}[mode]

    if norm_bwd is not None:
        assert tn == N and residual is None and mode != "tn", "the norm backward needs whole rows"

    def body(*refs):
        ab = refs[:2 * npairs]
        res_ref = refs[2 * npairs] if residual is not None else None
        k = pl.program_id(2)
        first_row_tile = pl.program_id(1) == 0
        if norm_bwd is not None:
            x_ref, g_ref, r_ref = refs[2 * npairs:2 * npairs + 3]
            o_ref, dg_ref, acc = refs[-3], refs[-2], refs[-1]
        else:
            o_ref, acc = refs[-2], refs[-1]

        @pl.when(k == 0)
        def _():
            acc[...] = jnp.zeros_like(acc)

        for p in range(npairs):
            acc[...] += dot(ab[2 * p][...], ab[2 * p + 1][...])

        @pl.when(k == nk - 1)
        def _():
            out = acc[...]
            if res_ref is not None:
                out = out + res_ref[...]
            if norm_bwd is not None:
                xv = x_ref[...]
                r = _rms_rows(xv)
                xhat = xv * r
                dxhat = out * g_ref[...]

                @pl.when(first_row_tile)
                def _():
                    dg_ref[...] = jnp.zeros_like(dg_ref)

                dg_ref[...] += jnp.sum(out * xhat, axis=0, keepdims=True)
                out = r_ref[...] + r * (dxhat - xhat * jnp.mean(dxhat * xhat, axis=-1, keepdims=True))
            o_ref[...] = out.astype(o_ref.dtype)

    if mode == "nn":
        a_spec = pl.BlockSpec((tm, tk), lambda j, i, k: (i, k))
        b_spec = pl.BlockSpec((tk, tn), lambda j, i, k: (k, j))
    elif mode == "nt":
        a_spec = pl.BlockSpec((tm, tk), lambda j, i, k: (i, k))
        b_spec = pl.BlockSpec((tn, tk), lambda j, i, k: (j, k))
    else:
        a_spec = pl.BlockSpec((tk, tm), lambda j, i, k: (k, i))
        b_spec = pl.BlockSpec((tk, tn), lambda j, i, k: (k, j))
    o_spec = pl.BlockSpec((tm, tn), lambda j, i, k: (i, j))
    in_specs, args = [], []
    for a, b in pairs:
        in_specs += [a_spec, b_spec]
        args += [a, b]
    if residual is not None:
        in_specs.append(o_spec)
        args.append(residual)
    out_specs, out_shape, sem = o_spec, jax.ShapeDtypeStruct((M, N), out_dtype), ("parallel", "parallel", "arbitrary")
    if norm_bwd is not None:
        x, g, res = norm_bwd
        vec = pl.BlockSpec((1, tn), lambda j, i, k: (0, j))
        in_specs += [o_spec, vec, o_spec]
        args += [x, g.reshape(1, N), res]
        out_specs, out_shape = [o_spec, vec], [out_shape, jax.ShapeDtypeStruct((1, N), F32)]
        sem = ("arbitrary", "arbitrary", "arbitrary")
    return pl.pallas_call(
        body, name=name, grid=(N // tn, M // tm, nk),
        in_specs=in_specs, out_specs=out_specs, out_shape=out_shape,
        scratch_shapes=[pltpu.VMEM((tm, tn), F32)],
        compiler_params=_cp(sem),
    )(*args)


def _rms_rows(x):
    return lax.rsqrt(jnp.mean(x * x, axis=-1, keepdims=True) + EPS)


def _norm_fwd(name, x, g):
    T, Dm = x.shape
    tm = _tile(T, 512, 16)

    def body(x_ref, g_ref, o_ref):
        xv = x_ref[...]
        o_ref[...] = ((xv * _rms_rows(xv)) * g_ref[...]).astype(o_ref.dtype)

    return pl.pallas_call(
        body, name=name, grid=(T // tm,),
        in_specs=[pl.BlockSpec((tm, Dm), lambda i: (i, 0)), pl.BlockSpec((1, Dm), lambda i: (0, 0))],
        out_specs=pl.BlockSpec((tm, Dm), lambda i: (i, 0)),
        out_shape=jax.ShapeDtypeStruct((T, Dm), _MXU),
        compiler_params=_cp(("parallel",)),
    )(x, g.reshape(1, Dm))


def _gelu(x):
    return jax.nn.gelu(x)


def _gelu_grad(x):
    c0, c1 = math.sqrt(2.0 / math.pi), 0.044715
    t = jnp.tanh(c0 * (x + c1 * x * x * x))
    return 0.5 * (1.0 + t) + 0.5 * x * (1.0 - t * t) * c0 * (1.0 + 3.0 * c1 * x * x)


def _head64_rms(x):
    lo = _lane(x.shape) < FOX_HEAD_DIM
    xx = x * x
    sa = jnp.sum(jnp.where(lo, xx, 0.0), axis=-1, keepdims=True)
    sb = jnp.sum(jnp.where(lo, 0.0, xx), axis=-1, keepdims=True)
    inv = 1.0 / FOX_HEAD_DIM
    return jnp.where(lo, lax.rsqrt(sa * inv + EPS), lax.rsqrt(sb * inv + EPS))


def _head64_mean(x):
    lo = _lane(x.shape) < FOX_HEAD_DIM
    sa = jnp.sum(jnp.where(lo, x, 0.0), axis=-1, keepdims=True)
    sb = jnp.sum(jnp.where(lo, 0.0, x), axis=-1, keepdims=True)
    return jnp.where(lo, sa, sb) * (1.0 / FOX_HEAD_DIM)


def _masked_ws(ws_ref, g):
    w = ws_ref[g]
    keep = (_row(w.shape) // CHUNK) >= (_lane(w.shape) // CHUNK)
    return jnp.where(keep, w, 0.0)


def _even_pre(name, z, g_v, w_s, bs_cols, g_qn2, g_kn2, b_f_pad):
    T = z.shape[0]
    tm = _tile(T, 256, A_BLOCK)
    nb = tm // A_BLOCK

    def body(z_ref, gv_ref, ws_ref, bs_ref, gq_ref, gk_ref, bf_ref, ya_ref, q_ref, k_ref, v_ref, c_ref, ct_ref, carry):
        i = pl.program_id(0)

        @pl.when(i == 0)
        def _():
            carry[...] = jnp.zeros_like(carry)

        wm = [_mx(_masked_ws(ws_ref, g)) for g in range(A_GROUPS)]
        for g in range(A_GROUPS):
            sl = slice(g * A_GROUP_DIM, (g + 1) * A_GROUP_DIM)
            u = _gelu(z_ref[:, sl])
            v = _gelu(z_ref[:, A_WIDTH + g * A_GROUP_DIM:A_WIDTH + (g + 1) * A_GROUP_DIM])
            vn = _mx((v * _rms_rows(v)) * gv_ref[:, sl])
            for n in range(nb):
                rows = slice(n * A_BLOCK, (n + 1) * A_BLOCK)
                s = jnp.dot(wm[g], vn[rows], preferred_element_type=F32) + bs_ref[:, g:g + 1]
                ya_ref[rows, sl] = (u[rows] * s).astype(ya_ref.dtype)
        o = 2 * A_WIDTH
        for j in range(FOX_WIDTH // LANES):
            sl = slice(j * LANES, (j + 1) * LANES)
            q = z_ref[:, o + j * LANES:o + (j + 1) * LANES]
            q_ref[:, sl] = (((q * _head64_rms(q)) * gq_ref[...]) * (1.0 / math.sqrt(FOX_HEAD_DIM))).astype(q_ref.dtype)
            k = z_ref[:, o + FOX_WIDTH + j * LANES:o + FOX_WIDTH + (j + 1) * LANES]
            k_ref[:, sl] = ((k * _head64_rms(k)) * gk_ref[...]).astype(k_ref.dtype)
            v_ref[:, sl] = z_ref[:, o + 2 * FOX_WIDTH + j * LANES:o + 2 * FOX_WIDTH + (j + 1) * LANES].astype(v_ref.dtype)
        zf = z_ref[:, o + 3 * FOX_WIDTH:o + 3 * FOX_WIDTH + LANES]
        logf = jnp.where(_lane(zf.shape) < FOX_HEADS, jax.nn.log_sigmoid(zf + bf_ref[...]), 0.0)
        tri = (_row((tm, tm)) >= _lane((tm, tm))).astype(F32)
        c = _dot_f32(tri, logf) + carry[0:1, :]
        c_ref[...] = c
        ct_ref[...] = c.T[0:SUBLANES, :]
        carry[0:1, :] = c[tm - 1:tm, :]

    wide = lambda n: pl.BlockSpec((tm, n), lambda i: (i, 0))
    full = lambda s: pl.BlockSpec(s, lambda i: tuple(0 for _ in s))
    out512 = jax.ShapeDtypeStruct((T, 512), _MXU)
    return pl.pallas_call(
        body, name=name, grid=(T // tm,),
        in_specs=[wide(EVEN_IN_PAD), full((1, A_WIDTH)), full((A_GROUPS, A_BLOCK, A_BLOCK)), full((A_BLOCK, LANES)),
                  full((1, LANES)), full((1, LANES)), full((1, LANES))],
        out_specs=[wide(512), wide(512), wide(512), wide(512), wide(LANES), pl.BlockSpec((SUBLANES, tm), lambda i: (0, i))],
        out_shape=[out512, out512, out512, out512, jax.ShapeDtypeStruct((T, LANES), F32), jax.ShapeDtypeStruct((SUBLANES, T), F32)],
        scratch_shapes=[pltpu.VMEM((SUBLANES, LANES), F32)],
        compiler_params=_cp(("arbitrary",)),
    )(z, g_v, w_s, bs_cols, g_qn2, g_kn2, b_f_pad)


def _tri_steps(n, by_rows):
    if by_rows:
        pairs = [(q, k) for q in range(n) for k in range(q + 1)]
    else:
        pairs = [(q, k) for k in range(n) for q in range(k, n)]
    return (jnp.asarray(np.array([p[0] for p in pairs], np.int32)), jnp.asarray(np.array([p[1] for p in pairs], np.int32)))


def _fox_logits(q, k, c_col, c_row, diag, tq):
    s = _dot_nt(q, k) + (c_col - c_row)
    if diag:
        s = jnp.where(_row((tq, tq)) >= _lane((tq, tq)), s, -jnp.inf)
    return s


FOX_SKIP_BELOW = 104.0


def _fox_skip_flags(c, g_qn2, g_kn2):
    T = c.shape[0]
    tq = _tile(T, FOX_TILE, LANES)
    nblk = T // tq
    cb = c.reshape(nblk, tq, LANES)
    c_first, c_last = cb[:, 0, :FOX_HEADS], cb[:, tq - 1, :FOX_HEADS]
    bound = 17.0 * jnp.max(jnp.abs(g_qn2)) * jnp.max(jnp.abs(g_kn2))
    keep = bound + c_first[:, None, :] - c_last[None, :, :] > -FOX_SKIP_BELOW
    return keep.astype(F32).reshape(-1)


def _on_diag_or_not(qi, ki, step):
    @pl.when(qi == ki)
    def _():
        step(True)

    @pl.when(qi != ki)
    def _():
        step(False)


def _side_specs(side):
    if side is None:
        return [], [], [], []
    return [pl.BlockSpec(memory_space=pl.ANY)] * side.n, list(side.arrays), list(side.out_shape), list(side.scratch)


def _side_run(side, s_id, nsteps, side_in, side_out, sems):
    if side is None:
        return

    @pl.when(s_id == 0)
    def _():
        side.start(side_in, side_out, sems)

    @pl.when(s_id == nsteps // 2)
    def _():
        side.middle(side_in, side_out, sems)

    @pl.when(s_id == nsteps - 1)
    def _():
        side.finish(side_in, side_out, sems)


def _fox_fwd(name, qn, kn, vv, c, ct, flags, side=None):
    T = qn.shape[0]
    tq = _tile(T, FOX_TILE, LANES)
    nblk = T // tq
    qs, ks = _tri_steps(nblk, True)
    nsteps = int(qs.shape[0])
    npair = FOX_WIDTH // LANES
    s_specs, s_args, s_shapes, s_scratch = _side_specs(side)

    def body(qs_ref, ks_ref, *refs):
        main_in, side_in, main_out, side_out, main_scr, sems = _split_side(refs, 6, 3, 3, side)
        q_ref, k_ref, v_ref, c_ref, ct_ref, flag_ref = main_in
        yb_ref, yf_ref, lse_ref = main_out
        acc, m_s, l_s = main_scr
        s_id = pl.program_id(0)
        qi, ki = qs_ref[s_id], ks_ref[s_id]
        _side_run(side, s_id, nsteps, side_in, side_out, sems)

        @pl.when(ki == 0)
        def _():
            acc[...] = jnp.zeros_like(acc)
            m_s[...] = jnp.full_like(m_s, -jnp.inf)
            l_s[...] = jnp.zeros_like(l_s)

        lo = _lane((tq, LANES)) < FOX_HEAD_DIM

        def pairs_from(first, diag):
            for j in range(first, npair):
                sl = slice(j * LANES, (j + 1) * LANES)
                q2, k2, v2 = q_ref[:, sl], k_ref[:, sl], v_ref[:, sl]
                outs = []
                for e in range(2):
                    h = 2 * j + e
                    qh = jnp.where(lo if e == 0 else ~lo, q2, jnp.zeros_like(q2))
                    s = _fox_logits(qh, k2, c_ref[:, h:h + 1], ct_ref[h:h + 1, :], diag, tq)
                    m_old = m_s[h]
                    m_new = jnp.maximum(m_old, jnp.max(s, axis=-1, keepdims=True))
                    alpha = jnp.exp(m_old - m_new)
                    p = jnp.exp(s - m_new)
                    l_s[h] = alpha * l_s[h] + jnp.sum(p, axis=-1, keepdims=True)
                    m_s[h] = m_new
                    outs.append((alpha, _dot(p, v2)))
                alpha2 = jnp.where(lo, outs[0][0], outs[1][0])
                acc[:, sl] = alpha2 * acc[:, sl] + jnp.where(lo, outs[0][1], outs[1][1])

        def step(diag):
            if diag:
                pairs_from(0, True)
                return
            f = (qi * nblk + ki) * FOX_HEADS
            need = [(flag_ref[f + 2 * j] > 0.5) | (flag_ref[f + 2 * j + 1] > 0.5) for j in range(npair - 1)]
            first = jnp.int32(npair - 1)
            for j in reversed(range(npair - 1)):
                first = jnp.where(need[j], j, first)
            for j in range(npair):
                pl.when(first == j)(functools.partial(pairs_from, j, False))

        _on_diag_or_not(qi, ki, step)

        @pl.when(ki == qi)
        def _():
            lse = jnp.zeros((tq, LANES), F32)
            for j in range(npair):
                sl = slice(j * LANES, (j + 1) * LANES)
                inv = jnp.where(lo, 1.0 / l_s[2 * j], 1.0 / l_s[2 * j + 1])
                y = acc[:, sl] * inv
                yf_ref[:, sl] = y
                yb_ref[:, sl] = y.astype(yb_ref.dtype)
            for h in range(FOX_HEADS):
                lse = jnp.where(_lane((tq, LANES)) == h, m_s[h] + jnp.log(l_s[h]), lse)
            lse_ref[...] = lse

    qmap = lambda s, qs_r, ks_r: (qs_r[s], 0)
    kmap = lambda s, qs_r, ks_r: (ks_r[s], 0)
    grid_spec = pltpu.PrefetchScalarGridSpec(
        num_scalar_prefetch=2, grid=(nsteps,),
        in_specs=[pl.BlockSpec((tq, 512), qmap), pl.BlockSpec((tq, 512), kmap), pl.BlockSpec((tq, 512), kmap),
                  pl.BlockSpec((tq, LANES), qmap),
                  pl.BlockSpec((SUBLANES, tq), lambda s, qs_r, ks_r: (0, ks_r[s])), pl.BlockSpec(memory_space=pltpu.SMEM)] + s_specs,
        out_specs=[pl.BlockSpec((tq, 512), qmap), pl.BlockSpec((tq, 512), qmap), pl.BlockSpec((tq, LANES), qmap)] + s_specs,
        scratch_shapes=[pltpu.VMEM((tq, 512), F32), pltpu.VMEM((FOX_HEADS, tq, 1), F32), pltpu.VMEM((FOX_HEADS, tq, 1), F32)] + s_scratch,
    )
    res = pl.pallas_call(
        body, name=name, grid_spec=grid_spec,
        out_shape=[jax.ShapeDtypeStruct((T, 512), _MXU), jax.ShapeDtypeStruct((T, 512), F32), jax.ShapeDtypeStruct((T, LANES), F32)] + s_shapes,
        compiler_params=_cp(("arbitrary",)),
    )(qs, ks, qn, kn, vv, c, ct, flags, *s_args)
    return res[:3], res[3:]


def _shift_down(ext, k, tm):
    return pltpu.roll(ext, k, axis=0)[HALO:HALO + tm]


def _pool_p(z_ext, g, tm, pos):
    w = POOL_WINDOWS[g]
    s, span = z_ext, 1
    while span < w:
        s = s + pltpu.roll(s, span, axis=0)
        span *= 2
    cnt = jnp.minimum(pos + 1, w).astype(F32)
    return s[HALO:HALO + tm] / cnt - z_ext[HALO:HALO + tm]


def _odd_fwd(name, z, w_pool, s_pool, conv_w8):
    T = z.shape[0]
    tm = _tile(T, 256, HALO)
    r = tm // HALO

    def body(z_ref, zb_ref, wp_ref, sp_ref, cw_ref, y_ref):
        i = pl.program_id(0)
        first = i == 0
        pos = i * tm + _row((tm, LANES))
        for g in range(len(POOL_WINDOWS)):
            sl = slice(g * LANES, (g + 1) * LANES)
            halo = jnp.where(first, 0.0, zb_ref[:, sl])
            z_ext = jnp.concatenate([halo, z_ref[:, sl]], axis=0)
            p = _pool_p(z_ext, g, tm, pos)
            y_ref[:, sl] = (_dot(p, wp_ref[g]) * sp_ref[:, sl]).astype(y_ref.dtype)
        for j in range(CONV_WIDTH // LANES):
            sl = slice(j * LANES, (j + 1) * LANES)
            hd = slice(POOL_WIDTH + j * LANES, POOL_WIDTH + (j + 1) * LANES)
            gb = slice(POOL_WIDTH + CONV_WIDTH + j * LANES, POOL_WIDTH + CONV_WIDTH + (j + 1) * LANES)
            gc = slice(POOL_WIDTH + 2 * CONV_WIDTH + j * LANES, POOL_WIDTH + 2 * CONV_WIDTH + (j + 1) * LANES)
            xg_b = jnp.where(first, 0.0, zb_ref[:, gc] * zb_ref[:, hd])
            xg = jnp.concatenate([xg_b, z_ref[:, gc] * z_ref[:, hd]], axis=0)
            conv = (cw_ref[0:1, sl] * _shift_down(xg, 2, tm) + cw_ref[1:2, sl] * _shift_down(xg, 1, tm)
                    + cw_ref[2:3, sl] * xg[HALO:HALO + tm])
            y_ref[:, POOL_WIDTH + j * LANES:POOL_WIDTH + (j + 1) * LANES] = (z_ref[:, gb] * conv).astype(y_ref.dtype)

    full = lambda s: pl.BlockSpec(s, lambda i: tuple(0 for _ in s))
    return pl.pallas_call(
        body, name=name, grid=(T // tm,),
        in_specs=[pl.BlockSpec((tm, ODD_IN), lambda i: (i, 0)),
                  pl.BlockSpec((HALO, ODD_IN), lambda i: (jnp.maximum(i * r - 1, 0), 0)),
                  full((4, LANES, LANES)), full((1, POOL_WIDTH)), full((SUBLANES, CONV_WIDTH))],
        out_specs=pl.BlockSpec((tm, 1024), lambda i: (i, 0)),
        out_shape=jax.ShapeDtypeStruct((T, 1024), _MXU),
        compiler_params=_cp(("parallel",)),
    )(z, z, w_pool, s_pool, conv_w8)


def _mem_fwd(name, mem, g_mem, wkv, gk):
    M = mem.shape[0]

    def body(mem_ref, g_ref, w_ref, gk_ref, m_ref, kv_ref, kn_ref, v_ref):
        x = mem_ref[...]
        m = ((x * _rms_rows(x)) * g_ref[...]).astype(m_ref.dtype)
        m_ref[...] = m
        kv = _dot(m, w_ref[...])
        kv_ref[...] = kv
        for h in range(XA_HEADS):
            sl = slice(h * LANES, (h + 1) * LANES)
            k = kv[:, sl]
            kn_ref[:, sl] = ((k * _rms_rows(k)) * gk_ref[...]).astype(kn_ref.dtype)
        v_ref[...] = kv[:, XA_WIDTH:].astype(v_ref.dtype)

    return pl.pallas_call(
        body, name=name,
        out_shape=[jax.ShapeDtypeStruct((M, D_MODEL), _MXU), jax.ShapeDtypeStruct((M, 2 * XA_WIDTH), F32),
                   jax.ShapeDtypeStruct((M, XA_WIDTH), _MXU), jax.ShapeDtypeStruct((M, XA_WIDTH), _MXU)],
        compiler_params=pltpu.CompilerParams(vmem_limit_bytes=VMEM_LIMIT),
    )(mem, g_mem.reshape(1, D_MODEL), wkv, gk.reshape(1, XA_HEAD_DIM))


def _xa_probs(qx, gq, kn_h):
    r = _rms_rows(qx)
    qhat = qx * r
    qn = qhat * gq
    s = _dot_nt(qn, kn_h) * (1.0 / math.sqrt(XA_HEAD_DIM))
    s = s - jnp.max(s, axis=-1, keepdims=True)
    e = jnp.exp(s)
    return qhat, r, qn, e / jnp.sum(e, axis=-1, keepdims=True)


def _xa_fwd(name, qx, gq, kn, vx):
    T, M = qx.shape[0], kn.shape[0]
    tm = _tile(T, 512, 16)

    def body(q_ref, gq_ref, k_ref, v_ref, o_ref):
        for h in range(XA_HEADS):
            sl = slice(h * LANES, (h + 1) * LANES)
            _, _, _, p = _xa_probs(q_ref[:, sl], gq_ref[...], k_ref[:, sl])
            o_ref[:, sl] = _dot(p, v_ref[:, sl]).astype(o_ref.dtype)

    full = lambda s: pl.BlockSpec(s, lambda i: tuple(0 for _ in s))
    return pl.pallas_call(
        body, name=name, grid=(T // tm,),
        in_specs=[pl.BlockSpec((tm, XA_WIDTH), lambda i: (i, 0)), full((1, LANES)), full((M, XA_WIDTH)), full((M, XA_WIDTH))],
        out_specs=pl.BlockSpec((tm, XA_WIDTH), lambda i: (i, 0)),
        out_shape=jax.ShapeDtypeStruct((T, XA_WIDTH), _MXU),
        compiler_params=_cp(("parallel",)),
    )(qx, gq.reshape(1, XA_HEAD_DIM), kn, vx)


def _ffn_up(name, h, wg, wu):
    T, K = h.shape
    N = wg.shape[1]
    tm, tn = _tile(T, 512, 16), _tile(N, 1536, LANES)

    def body(h_ref, wg_ref, wu_ref, a_ref, b_ref, s_ref):
        hv = h_ref[...]
        a, b = _dot(hv, wg_ref[...]), _dot(hv, wu_ref[...])
        a_ref[...] = a.astype(a_ref.dtype)
        b_ref[...] = b.astype(b_ref.dtype)
        s_ref[...] = (jax.nn.silu(a) * b).astype(s_ref.dtype)

    w_spec = pl.BlockSpec((K, tn), lambda j, i: (0, j))
    o_spec = pl.BlockSpec((tm, tn), lambda j, i: (i, j))
    return pl.pallas_call(
        body, name=name, grid=(N // tn, T // tm),
        in_specs=[pl.BlockSpec((tm, K), lambda j, i: (i, 0)), w_spec, w_spec],
        out_specs=[o_spec, o_spec, o_spec],
        out_shape=[jax.ShapeDtypeStruct((T, N), _MXU), jax.ShapeDtypeStruct((T, N), _MXU), jax.ShapeDtypeStruct((T, N), _MXU)],
        compiler_params=_cp(("parallel", "parallel")),
    )(h, wg, wu)


def _loss_grad(name, y, target):
    T, Dm = y.shape
    tm = _tile(T, 512, 8)

    def body(y_ref, t_ref, dy_ref, l_ref):
        e = y_ref[...] - t_ref[...]
        dy_ref[...] = e * (1.0 / Dm)

        @pl.when(pl.program_id(0) == 0)
        def _():
            l_ref[...] = jnp.zeros_like(l_ref)

        l_ref[...] += jnp.sum(e * e, axis=0, keepdims=True) * (0.5 / Dm)

    blk = pl.BlockSpec((tm, Dm), lambda i: (i, 0))
    return pl.pallas_call(
        body, name=name, grid=(T // tm,),
        in_specs=[blk, blk], out_specs=[blk, pl.BlockSpec((1, Dm), lambda i: (0, 0))],
        out_shape=[jax.ShapeDtypeStruct((T, Dm), F32), jax.ShapeDtypeStruct((1, Dm), F32)],
        compiler_params=_cp(("arbitrary",)),
    )(y, target)


def _pad_lanes(v, n=LANES):
    v = v.reshape(1, -1)
    return jnp.pad(v, ((0, 0), (0, n - v.shape[1])))


def _layer_fwd(l, x, mem, P, side=None, on_side=None):
    sv = {"x0": x}
    W = _layer_weights(l, P, "in")
    h1 = _norm_fwd(f"norm_mix{l}", x, W["g_mix"])
    sv["h1"] = h1
    if l % 2 == 0:
        z = _mm(f"e_in{l}", [(h1, W["w_in"])], "nn")
        ya, qn, kn, vv, c, ct = _even_pre(f"even_pre{l}", z, W["g_v"], W["w_s"], W["bs_cols"], W["g_qn2"], W["g_kn2"], W["b_f_pad"])
        flags = _fox_skip_flags(c, W["g_qn2"], W["g_kn2"])
        (yb, ybf, lse), side_out = _fox_fwd(f"fox_fwd{l}", qn, kn, vv, c, ct, flags, side=side)
        if on_side is not None:
            on_side(side_out)
        W.update(_layer_weights(l, P, "rest"))
        sv.update(z=z, ya=ya, qn=qn, kn=kn, vv=vv, c=c, ct=ct, yb=yb, ybf=ybf, lse=lse, flags=flags)
        x1 = _mm(f"e_out{l}", [(ya, W["w_out"][:A_WIDTH]), (yb, W["w_out"][A_WIDTH:])], "nn", residual=x)
    else:
        W.update(_layer_weights(l, P, "rest"))
        z = _mm(f"o_in{l}", [(h1, W["w_in"])], "nn")
        ycat = _odd_fwd(f"odd_fwd{l}", z, W["w_pool"], W["s_pool"], W["conv_w8"])
        sv.update(z=z, ycat=ycat)
        x1 = _mm(f"o_out{l}", [(ycat, W["w_out"])], "nn", residual=x)
    sv["x1"] = x1
    h2 = _norm_fwd(f"norm_xa{l}", x1, W["g_xa"])
    qx = _mm(f"xa_q{l}", [(h2, W["wq"])], "nn")
    m, kv, kn_x, vx = _mem_fwd(f"mem_fwd{l}", mem, W["g_mem"], W["wkv"], W["gk"])
    o = _xa_fwd(f"xa_fwd{l}", qx, W["gq"], kn_x, vx)
    x2 = _mm(f"xa_o{l}", [(o, W["wo"])], "nn", residual=x1)
    sv.update(h2=h2, qx=qx, m=m, kv=kv, kn_x=kn_x, vx=vx, o=o, x2=x2)
    h3 = _norm_fwd(f"norm_ffn{l}", x2, W["g_ffn"])
    a, b, s = _ffn_up(f"ffn_up{l}", h3, W["w_gate"], W["w_up"])
    x3 = _mm(f"ffn_down{l}", [(s, W["w_down"])], "nn", residual=x2)
    sv.update(h3=h3, a=a, b=b, s=s)
    return x3, sv, W


def _layer_weights(l, P, part):
    i = l // 2
    if part == "in":
        W = {"g_mix": P["g_mix"][l]}
        if l % 2 == 0:
            w_in = _mx(P["e_w_in"][i])
            if w_in.shape[1] < EVEN_IN_PAD:
                w_in = jnp.pad(w_in, ((0, 0), (0, EVEN_IN_PAD - w_in.shape[1])))
            W.update(w_in=w_in, g_v=P["e_g_v"][i].reshape(1, A_WIDTH), w_s=P["e_w_s"][i],
                     bs_cols=jnp.pad(P["e_b_s"][i].T, ((0, 0), (0, LANES - A_GROUPS))),
                     g_qn2=jnp.tile(P["e_g_qn"][i], 2).reshape(1, LANES), g_kn2=jnp.tile(P["e_g_kn"][i], 2).reshape(1, LANES),
                     b_f_pad=_pad_lanes(P["e_b_f"][i]))
        return W
    W = {k: P[k][l] for k in ("g_xa", "g_mem", "g_ffn")}
    W.update(wq=_mx(P["xa_wq"][l]), wkv=_mx(P["xa_wkv"][l]), wo=_mx(P["xa_wo"][l]), gq=P["xa_gq"][l], gk=P["xa_gk"][l],
             w_gate=_mx(P["w_gate"][l]), w_up=_mx(P["w_up"][l]), w_down=_mx(P["w_down"][l]))
    if l % 2 == 0:
        W.update(w_out=_mx(P["e_w_out"][i]))
    else:
        W.update(w_in=_mx(P["o_w_in"][i]), w_pool=_mx(P["o_w_pool"][i]), s_pool=P["o_s_pool"][i].reshape(1, POOL_WIDTH),
                 conv_w8=jnp.pad(P["o_conv_w"][i], ((0, SUBLANES - CONV_K), (0, 0))), w_out=_mx(P["o_w_out"][i]))
    return W


def _ffn_bwd_act(name, g, wd, a, b):
    T, N = a.shape
    K = g.shape[1]
    tm, tn = _tile(T, 512, 16), _tile(N, 1536, LANES)

    def body(g_ref, wd_ref, a_ref, b_ref, da_ref, db_ref):
        ds = _dot_nt(g_ref[...], wd_ref[...])
        av = a_ref[...].astype(F32)
        sig = jax.nn.sigmoid(av)
        da_ref[...] = (ds * b_ref[...].astype(F32) * (sig * (1.0 + av * (1.0 - sig)))).astype(da_ref.dtype)
        db_ref[...] = (ds * (av * sig)).astype(db_ref.dtype)

    o_spec = pl.BlockSpec((tm, tn), lambda j, i: (i, j))
    return pl.pallas_call(
        body, name=name, grid=(N // tn, T // tm),
        in_specs=[pl.BlockSpec((tm, K), lambda j, i: (i, 0)), pl.BlockSpec((tn, K), lambda j, i: (j, 0)), o_spec, o_spec],
        out_specs=[o_spec, o_spec],
        out_shape=[jax.ShapeDtypeStruct((T, N), _MXU), jax.ShapeDtypeStruct((T, N), _MXU)],
        compiler_params=_cp(("parallel", "parallel")),
    )(g, wd, a, b)


def _xa_bwd(name, qx, do, gq, kn, vx):
    T, M = qx.shape[0], kn.shape[0]
    tm = _tile(T, 512, 16)

    def body(q_ref, do_ref, gq_ref, k_ref, v_ref, dq_ref, dk_ref, dv_ref, dg_ref):
        @pl.when(pl.program_id(0) == 0)
        def _():
            dk_ref[...] = jnp.zeros_like(dk_ref)
            dv_ref[...] = jnp.zeros_like(dv_ref)
            dg_ref[...] = jnp.zeros_like(dg_ref)

        gqv = gq_ref[...]
        for h in range(XA_HEADS):
            sl = slice(h * LANES, (h + 1) * LANES)
            qhat, r, qn, p = _xa_probs(q_ref[:, sl], gqv, k_ref[:, sl])
            do_h = do_ref[:, sl]
            dp = _dot_nt(do_h, v_ref[:, sl])
            dv_ref[:, sl] += _dot_tn(p, do_h)
            dsm = p * (dp - jnp.sum(p * dp, axis=-1, keepdims=True)) * (1.0 / math.sqrt(XA_HEAD_DIM))
            dqn = _dot(dsm, k_ref[:, sl])
            dk_ref[:, sl] += _dot_tn(dsm, qn)
            dg_ref[...] += jnp.sum(dqn * qhat, axis=0, keepdims=True)
            dxh = dqn * gqv
            dq_ref[:, sl] = (r * (dxh - qhat * jnp.mean(dxh * qhat, axis=-1, keepdims=True))).astype(dq_ref.dtype)

    full = lambda s: pl.BlockSpec(s, lambda i: tuple(0 for _ in s))
    blk = pl.BlockSpec((tm, XA_WIDTH), lambda i: (i, 0))
    return pl.pallas_call(
        body, name=name, grid=(T // tm,),
        in_specs=[blk, blk, full((1, LANES)), full((M, XA_WIDTH)), full((M, XA_WIDTH))],
        out_specs=[blk, full((M, XA_WIDTH)), full((M, XA_WIDTH)), full((1, LANES))],
        out_shape=[jax.ShapeDtypeStruct((T, XA_WIDTH), _MXU), jax.ShapeDtypeStruct((M, XA_WIDTH), F32),
                   jax.ShapeDtypeStruct((M, XA_WIDTH), F32), jax.ShapeDtypeStruct((1, LANES), F32)],
        compiler_params=_cp(("arbitrary",)),
    )(qx, do, gq.reshape(1, XA_HEAD_DIM), kn, vx)


def _mem_bwd(name, mem, g_mem, m, kv, gk, dkn, dvx, wkv):
    M = mem.shape[0]

    def body(mem_ref, g_ref, m_ref, kv_ref, gk_ref, dkn_ref, dvx_ref, w_ref, dw_ref, dgk_ref, dgm_ref):
        gkv = gk_ref[...]
        dgk = jnp.zeros((1, LANES), F32)
        parts = []
        for h in range(XA_HEADS):
            sl = slice(h * LANES, (h + 1) * LANES)
            k = kv_ref[:, sl]
            r = _rms_rows(k)
            khat = k * r
            dkn_h = dkn_ref[:, sl]
            dgk = dgk + jnp.sum(dkn_h * khat, axis=0, keepdims=True)
            dxh = dkn_h * gkv
            parts.append(r * (dxh - khat * jnp.mean(dxh * khat, axis=-1, keepdims=True)))
        dgk_ref[...] = dgk
        dkv = jnp.concatenate(parts + [dvx_ref[...]], axis=1)
        dw_ref[...] = _dot_tn(m_ref[...], dkv)
        dm = _dot_nt(dkv, w_ref[...])
        x = mem_ref[...]
        dgm_ref[...] = jnp.sum(dm * (x * _rms_rows(x)), axis=0, keepdims=True)

    return pl.pallas_call(
        body, name=name,
        out_shape=[jax.ShapeDtypeStruct((D_MODEL, 2 * XA_WIDTH), F32), jax.ShapeDtypeStruct((1, LANES), F32),
                   jax.ShapeDtypeStruct((1, D_MODEL), F32)],
        compiler_params=pltpu.CompilerParams(vmem_limit_bytes=VMEM_LIMIT),
    )(mem, g_mem.reshape(1, D_MODEL), m, kv, gk.reshape(1, XA_HEAD_DIM), dkn, dvx, wkv)


def _fox_bwd_common(q2, k2, v2, dy2, yf2, c_ref, ct_ref, lse_ref, e, h, lo, diag, tq):
    sel = lo if e == 0 else ~lo
    qh = jnp.where(sel, q2, jnp.zeros_like(q2))
    kh = jnp.where(sel, k2, jnp.zeros_like(k2))
    dyh = jnp.where(sel, dy2, 0.0)
    s = _fox_logits(qh, k2, c_ref[:, h:h + 1], ct_ref[h:h + 1, :], diag, tq)
    p = jnp.exp(s - lse_ref[:, h:h + 1])
    dp = _dot_nt(dyh, v2)
    delta = jnp.sum(dyh * yf2, axis=-1, keepdims=True)
    return p, p * (dp - delta), qh, kh, dyh


def _fox_bwd(name, qn, kn, vv, c, ct, lse, dycat, ybf, flags, side=None):
    T = qn.shape[0]
    tq = _tile(T, FOX_TILE, LANES)
    nblk = T // tq
    qs, ks = _tri_steps(nblk, False)
    nsteps = int(qs.shape[0])
    scale = 1.0 / math.sqrt(FOX_HEAD_DIM)
    s_specs, s_args, s_shapes, s_scratch = _side_specs(side)

    def body(qs_ref, ks_ref, *refs):
        main_in, side_in, main_out, side_out, main_scr, sems = _split_side(refs, 9, 5, 3, side)
        q_ref, k_ref, v_ref, c_ref, ct_ref, lse_ref, dy_ref, yf_ref, flag_ref = main_in
        dq_hbm, drow_hbm, dk_ref, dv_ref, dct_ref = main_out
        dq_acc, drow_acc, out_sem = main_scr
        s_id = pl.program_id(0)
        qi, ki = qs_ref[s_id], ks_ref[s_id]
        _side_run(side, s_id, nsteps, side_in, side_out, sems)
        rows = pl.ds(pl.multiple_of(qi * tq, tq), tq)

        @pl.when(qi == ki)
        def _():
            dk_ref[...] = jnp.zeros_like(dk_ref)
            dv_ref[...] = jnp.zeros_like(dv_ref)
            dct_ref[...] = jnp.zeros_like(dct_ref)

        @pl.when(ki == 0)
        def _():
            dq_acc[rows, :] = jnp.zeros((tq, FOX_WIDTH), F32)
            drow_acc[rows, :] = jnp.zeros((tq, LANES), F32)

        lo = _lane((tq, LANES)) < FOX_HEAD_DIM

        def step(diag):
            for j in range(FOX_WIDTH // LANES):
                sl = slice(j * LANES, (j + 1) * LANES)
                for e in range(2):
                    h = 2 * j + e

                    def head(h=h, e=e, sl=sl):
                        q2, k2, v2, dy2, yf2 = q_ref[:, sl], k_ref[:, sl], v_ref[:, sl], dy_ref[:, sl], yf_ref[:, sl]
                        p, ds, qh, kh, dyh = _fox_bwd_common(q2, k2, v2, dy2, yf2, c_ref, ct_ref, lse_ref, e, h, lo, diag, tq)
                        drow_acc[rows, :] += jnp.where(_lane((tq, LANES)) == h, jnp.sum(ds, axis=-1, keepdims=True), 0.0)
                        dct_ref[h:h + 1, :] -= jnp.sum(ds, axis=0, keepdims=True)
                        dq_acc[rows, sl] += _dot(ds, kh) * scale
                        dk_ref[:, sl] += _dot_tn(ds, qh)
                        dv_ref[:, sl] += _dot_tn(p, dyh)

                    if diag:
                        head()
                    else:
                        pl.when(flag_ref[(qi * nblk + ki) * FOX_HEADS + h] > 0.5)(head)

        _on_diag_or_not(qi, ki, step)

        @pl.when(s_id == nsteps - 1)
        def _():
            for src, dst in ((dq_acc, dq_hbm), (drow_acc, drow_hbm)):
                cp = pltpu.make_async_copy(src, dst, out_sem)
                cp.start()
                cp.wait()

    qmap = lambda s, a, b: (a[s], 0)
    kmap = lambda s, a, b: (b[s], 0)
    ctmap = lambda s, a, b: (0, b[s])
    any_spec = pl.BlockSpec(memory_space=pl.ANY)
    grid_spec = pltpu.PrefetchScalarGridSpec(
        num_scalar_prefetch=2, grid=(nsteps,),
        in_specs=[pl.BlockSpec((tq, 512), qmap), pl.BlockSpec((tq, 512), kmap), pl.BlockSpec((tq, 512), kmap),
                  pl.BlockSpec((tq, LANES), qmap), pl.BlockSpec((SUBLANES, tq), ctmap),
                  pl.BlockSpec((tq, LANES), qmap), pl.BlockSpec((tq, 512), lambda s, a, b: (a[s], 1)), pl.BlockSpec((tq, 512), qmap),
                  pl.BlockSpec(memory_space=pltpu.SMEM)] + s_specs,
        out_specs=[any_spec, any_spec, pl.BlockSpec((tq, 512), kmap), pl.BlockSpec((tq, 512), kmap), pl.BlockSpec((SUBLANES, tq), ctmap)] + s_specs,
        scratch_shapes=[pltpu.VMEM((T, FOX_WIDTH), F32), pltpu.VMEM((T, LANES), F32), pltpu.SemaphoreType.DMA] + s_scratch,
    )
    res = pl.pallas_call(
        body, name=name, grid_spec=grid_spec,
        out_shape=[jax.ShapeDtypeStruct((T, 512), F32), jax.ShapeDtypeStruct((T, LANES), F32), jax.ShapeDtypeStruct((T, 512), F32),
                   jax.ShapeDtypeStruct((T, 512), F32), jax.ShapeDtypeStruct((SUBLANES, T), F32)] + s_shapes,
        compiler_params=_cp(("arbitrary",)),
    )(qs, ks, qn, kn, vv, c, ct, lse, dycat, ybf, flags, *s_args)
    return res[:5], res[5:]


def _fold64(row):
    return row + pltpu.roll(row, FOX_HEAD_DIM, axis=1)


def _even_bwd(name, z, dycat, dqn, dkn, dvv, dct, drow, g_v, w_s, bs_cols, g_qn2, g_kn2, b_f_pad):
    T = z.shape[0]
    tm = _tile(T, 256, A_BLOCK)
    nb, nt = tm // A_BLOCK, T // tm

    def body(z_ref, dya_ref, dq_ref, dk_ref, dv_ref, dct_ref, drow_ref, gv_ref, ws_ref, bs_ref, gq_ref, gk_ref, bf_ref,
             dz_ref, dws_ref, dbs_ref, dgv_ref, dgq_ref, dgk_ref, dbf_ref, carry):
        @pl.when(pl.program_id(0) == 0)
        def _():
            for r in (dws_ref, dbs_ref, dgv_ref, dgq_ref, dgk_ref, dbf_ref, carry):
                r[...] = jnp.zeros_like(r)

        keep = (_row((A_BLOCK, A_BLOCK)) // CHUNK) >= (_lane((A_BLOCK, A_BLOCK)) // CHUNK)
        for g in range(A_GROUPS):
            sl = slice(g * A_GROUP_DIM, (g + 1) * A_GROUP_DIM)
            wm = _mx(_masked_ws(ws_ref, g))
            zu = z_ref[:, sl]
            zv = z_ref[:, A_WIDTH + g * A_GROUP_DIM:A_WIDTH + (g + 1) * A_GROUP_DIM]
            u, v = _gelu(zu), _gelu(zv)
            r = _rms_rows(v)
            vhat = v * r
            gvv = gv_ref[:, sl]
            vn = _mx(vhat * gvv)
            dya = dya_ref[:, sl]
            du_parts, dvn_parts = [], []
            for n in range(nb):
                rows = slice(n * A_BLOCK, (n + 1) * A_BLOCK)
                s = jnp.dot(wm, vn[rows], preferred_element_type=F32) + bs_ref[:, g:g + 1]
                du_parts.append(dya[rows] * s)
                d_s = dya[rows] * u[rows]
                dbs_ref[...] += jnp.where(_lane((A_BLOCK, LANES)) == g, jnp.sum(d_s, axis=-1, keepdims=True), 0.0)
                dws_ref[g] += jnp.where(keep, _dot_nt(d_s, vn[rows]), 0.0)
                dvn_parts.append(_dot_tn(wm, d_s))
            dvn = jnp.concatenate(dvn_parts, axis=0)
            dgv_ref[:, sl] += jnp.sum(dvn * vhat, axis=0, keepdims=True)
            dxh = dvn * gvv
            dv = r * (dxh - vhat * jnp.mean(dxh * vhat, axis=-1, keepdims=True))
            dz_ref[:, sl] = (jnp.concatenate(du_parts, axis=0) * _gelu_grad(zu)).astype(dz_ref.dtype)
            dz_ref[:, A_WIDTH + g * A_GROUP_DIM:A_WIDTH + (g + 1) * A_GROUP_DIM] = (dv * _gelu_grad(zv)).astype(dz_ref.dtype)
        o = 2 * A_WIDTH
        for j in range(FOX_WIDTH // LANES):
            sl = slice(j * LANES, (j + 1) * LANES)
            for (off, d_ref, g_ref, dg_ref) in ((o, dq_ref, gq_ref, dgq_ref), (o + FOX_WIDTH, dk_ref, gk_ref, dgk_ref)):
                zq = z_ref[:, off + j * LANES:off + (j + 1) * LANES]
                r = _head64_rms(zq)
                qhat = zq * r
                dn = d_ref[:, sl]
                dg_ref[...] += _fold64(jnp.sum(dn * qhat, axis=0, keepdims=True))
                dxh = dn * g_ref[...]
                dz_ref[:, off + j * LANES:off + (j + 1) * LANES] = (r * (dxh - qhat * _head64_mean(dxh * qhat))).astype(dz_ref.dtype)
            dz_ref[:, o + 2 * FOX_WIDTH + j * LANES:o + 2 * FOX_WIDTH + (j + 1) * LANES] = dv_ref[:, sl].astype(dz_ref.dtype)
        dct_v = dct_ref[...] + drow_ref[...].T[0:SUBLANES, :]
        upper = (_row((tm, tm)) >= _lane((tm, tm))).astype(F32)
        dlf_t = _dot_f32(dct_v, upper) + carry[:, 0:1]
        carry[...] += jnp.sum(dct_v, axis=1, keepdims=True)
        dlf = jnp.concatenate([dlf_t, jnp.zeros((LANES - SUBLANES, tm), F32)], axis=0).T
        zf = z_ref[:, o + 3 * FOX_WIDTH:o + 3 * FOX_WIDTH + LANES]
        dzf = dlf * jax.nn.sigmoid(-(zf + bf_ref[...]))
        dbf_ref[...] += jnp.sum(dzf, axis=0, keepdims=True)
        dz_ref[:, o + 3 * FOX_WIDTH:o + 3 * FOX_WIDTH + LANES] = dzf.astype(dz_ref.dtype)

    rev = lambda i: nt - 1 - i
    wide = lambda n, col=0: pl.BlockSpec((tm, n), lambda i: (rev(i), col))
    full = lambda s: pl.BlockSpec(s, lambda i: tuple(0 for _ in s))
    vec = full((1, LANES))
    return pl.pallas_call(
        body, name=name, grid=(nt,),
        in_specs=[wide(EVEN_IN_PAD), wide(512), wide(512), wide(512), wide(512), pl.BlockSpec((SUBLANES, tm), lambda i: (0, rev(i))),
                  wide(LANES), full((1, A_WIDTH)), full((A_GROUPS, A_BLOCK, A_BLOCK)), full((A_BLOCK, LANES)), vec, vec, vec],
        out_specs=[wide(EVEN_IN_PAD), full((A_GROUPS, A_BLOCK, A_BLOCK)), full((A_BLOCK, LANES)), full((1, A_WIDTH)), vec, vec, vec],
        out_shape=[jax.ShapeDtypeStruct((T, EVEN_IN_PAD), _MXU), jax.ShapeDtypeStruct((A_GROUPS, A_BLOCK, A_BLOCK), F32),
                   jax.ShapeDtypeStruct((A_BLOCK, LANES), F32), jax.ShapeDtypeStruct((1, A_WIDTH), F32),
                   jax.ShapeDtypeStruct((1, LANES), F32), jax.ShapeDtypeStruct((1, LANES), F32), jax.ShapeDtypeStruct((1, LANES), F32)],
        scratch_shapes=[pltpu.VMEM((SUBLANES, LANES), F32)],
        compiler_params=_cp(("arbitrary",)),
    )(z, dycat, dqn, dkn, dvv, dct, drow, g_v, w_s, bs_cols, g_qn2, g_kn2, b_f_pad)


def _shift_up(ext, k, tm):
    return pltpu.roll(ext, ext.shape[0] - k, axis=0)[0:tm]


def _odd_bwd(name, z, dycat, w_pool, s_pool, conv_w8):
    T = z.shape[0]
    tm = _tile(T, 256, HALO)
    r, nt = tm // HALO, T // tm
    n_ext = tm + HALO

    def body(z_ref, zb_ref, zn_ref, dy_ref, dyn_ref, wp_ref, sp_ref, cw_ref, dz_ref, dwp_ref, dsp_ref, dcw_ref):
        i = pl.program_id(0)

        @pl.when(i == 0)
        def _():
            for rr in (dwp_ref, dsp_ref, dcw_ref):
                rr[...] = jnp.zeros_like(rr)

        first, last = i == 0, i == nt - 1
        pos = i * tm + _row((tm, LANES))
        pos_ext = i * tm + _row((n_ext, LANES))
        for g, w in enumerate(POOL_WINDOWS):
            sl = slice(g * LANES, (g + 1) * LANES)
            z_ext = jnp.concatenate([jnp.where(first, 0.0, zb_ref[:, sl]), z_ref[:, sl]], axis=0)
            p = _pool_p(z_ext, g, tm, pos)
            spv = sp_ref[:, sl]
            dyc = dy_ref[:, sl]
            dsp_ref[:, sl] += jnp.sum(dyc * _dot(p, wp_ref[g]), axis=0, keepdims=True)
            dpw_ext = jnp.concatenate([dyc, jnp.where(last, 0.0, dyn_ref[:, sl])], axis=0) * spv
            dwp_ref[g] += _dot_tn(p, dpw_ext[0:tm])
            dp_ext = _dot_nt(dpw_ext, wp_ref[g])
            f = dp_ext / jnp.minimum(pos_ext + 1, w).astype(F32)
            span = 1
            while span < w:
                f = f + pltpu.roll(f, n_ext - span, axis=0)
                span *= 2
            dz_ref[:, sl] = (f[0:tm] - dp_ext[0:tm]).astype(dz_ref.dtype)
        for j in range(CONV_WIDTH // LANES):
            sl = slice(j * LANES, (j + 1) * LANES)
            hd = slice(POOL_WIDTH + j * LANES, POOL_WIDTH + (j + 1) * LANES)
            gb = slice(POOL_WIDTH + CONV_WIDTH + j * LANES, POOL_WIDTH + CONV_WIDTH + (j + 1) * LANES)
            gc = slice(POOL_WIDTH + 2 * CONV_WIDTH + j * LANES, POOL_WIDTH + 2 * CONV_WIDTH + (j + 1) * LANES)
            ysl = slice(POOL_WIDTH + j * LANES, POOL_WIDTH + (j + 1) * LANES)
            hdv, gbv, gcv = z_ref[:, hd], z_ref[:, gb], z_ref[:, gc]
            xg = gcv * hdv
            xg_ext = jnp.concatenate([jnp.where(first, 0.0, zb_ref[:, gc] * zb_ref[:, hd]), xg], axis=0)
            xg1, xg2 = _shift_down(xg_ext, 1, tm), _shift_down(xg_ext, 2, tm)
            w0, w1, w2 = cw_ref[0:1, sl], cw_ref[1:2, sl], cw_ref[2:3, sl]
            conv = w0 * xg2 + w1 * xg1 + w2 * xg
            dyd = dy_ref[:, ysl]
            dconv = dyd * gbv
            dconv_ext = jnp.concatenate([dconv, jnp.where(last, 0.0, dyn_ref[:, ysl] * zn_ref[:, gb])], axis=0)
            dcw_ref[0:1, sl] += jnp.sum(dconv * xg2, axis=0, keepdims=True)
            dcw_ref[1:2, sl] += jnp.sum(dconv * xg1, axis=0, keepdims=True)
            dcw_ref[2:3, sl] += jnp.sum(dconv * xg, axis=0, keepdims=True)
            dxg = w2 * dconv + w1 * _shift_up(dconv_ext, 1, tm) + w0 * _shift_up(dconv_ext, 2, tm)
            dz_ref[:, hd] = (dxg * gcv).astype(dz_ref.dtype)
            dz_ref[:, gb] = (dyd * conv).astype(dz_ref.dtype)
            dz_ref[:, gc] = (dxg * hdv).astype(dz_ref.dtype)

    full = lambda s: pl.BlockSpec(s, lambda i: tuple(0 for _ in s))
    back = lambda n: pl.BlockSpec((HALO, n), lambda i: (jnp.maximum(i * r - 1, 0), 0))
    nxt = lambda n: pl.BlockSpec((HALO, n), lambda i: (jnp.minimum((i + 1) * r, T // HALO - 1), 0))
    return pl.pallas_call(
        body, name=name, grid=(nt,),
        in_specs=[pl.BlockSpec((tm, ODD_IN), lambda i: (i, 0)), back(ODD_IN), nxt(ODD_IN),
                  pl.BlockSpec((tm, 1024), lambda i: (i, 0)), nxt(1024),
                  full((4, LANES, LANES)), full((1, POOL_WIDTH)), full((SUBLANES, CONV_WIDTH))],
        out_specs=[pl.BlockSpec((tm, ODD_IN), lambda i: (i, 0)), full((4, LANES, LANES)), full((1, POOL_WIDTH)), full((SUBLANES, CONV_WIDTH))],
        out_shape=[jax.ShapeDtypeStruct((T, ODD_IN), _MXU), jax.ShapeDtypeStruct((4, LANES, LANES), F32),
                   jax.ShapeDtypeStruct((1, POOL_WIDTH), F32), jax.ShapeDtypeStruct((SUBLANES, CONV_WIDTH), F32)],
        compiler_params=_cp(("arbitrary",)),
    )(z, z, z, dycat, dycat, w_pool, s_pool, conv_w8)


def _layer_bwd(l, g3, mem, W, sv, side_fn=None):
    i = l // 2
    G = {}
    side_out = ()
    da, db = _ffn_bwd_act(f"ffn_bwd_act{l}", g3, W["w_down"], sv["a"], sv["b"])
    G["w_down"] = _mm(f"d_w_down{l}", [(sv["s"], g3)], "tn")
    g2, dg = _mm(f"d_h3{l}", [(da, W["w_gate"]), (db, W["w_up"])], "nt", norm_bwd=(sv["x2"], W["g_ffn"], g3))
    G["w_gate"] = _mm(f"d_w_gate{l}", [(sv["h3"], da)], "tn")
    G["w_up"] = _mm(f"d_w_up{l}", [(sv["h3"], db)], "tn")
    G["g_ffn"] = dg[0]

    do = _mm(f"d_o{l}", [(g2, W["wo"])], "nt", out_dtype=_MXU)
    G["xa_wo"] = _mm(f"d_wo{l}", [(sv["o"], g2)], "tn")
    dqx, dkn, dvx, dgq = _xa_bwd(f"xa_bwd{l}", sv["qx"], do, W["gq"], sv["kn_x"], sv["vx"])
    G["xa_gq"] = dgq[0]
    g1, dg = _mm(f"d_h2{l}", [(dqx, W["wq"])], "nt", norm_bwd=(sv["x1"], W["g_xa"], g2))
    G["xa_wq"] = _mm(f"d_wq{l}", [(sv["h2"], dqx)], "tn")
    G["g_xa"] = dg[0]
    dwkv, dgk, dgm = _mem_bwd(f"mem_bwd{l}", mem, W["g_mem"], sv["m"], sv["kv"], W["gk"], dkn, dvx, W["wkv"])
    G["xa_wkv"], G["xa_gk"], G["g_mem"] = dwkv, dgk[0], dgm[0]

    dycat = _mm(f"d_ycat{l}", [(g1, W["w_out"])], "nt")
    if l % 2 == 0:
        G["e_w_out"] = jnp.concatenate([_mm(f"d_e_wout_a{l}", [(sv["ya"], g1)], "tn"), _mm(f"d_e_wout_b{l}", [(sv["yb"], g1)], "tn")], axis=0)
        fa = (sv["qn"], sv["kn"], sv["vv"], sv["c"], sv["ct"], sv["lse"], dycat, sv["ybf"], sv["flags"])
        (dqn, drow, dkn_f, dvv, dct), side_out = _fox_bwd(f"fox_bwd{l}", *fa, side=side_fn(G) if side_fn is not None else None)
        dz, dws, dbs, dgv, dgq2, dgk2, dbf = _even_bwd(f"even_bwd{l}", sv["z"], dycat, dqn, dkn_f, dvv, dct, drow, W["g_v"], W["w_s"],
                                                      W["bs_cols"], W["g_qn2"], W["g_kn2"], W["b_f_pad"])
        G.update(e_w_s=dws, e_b_s=dbs[:, :A_GROUPS].T, e_g_v=dgv[0], e_g_qn=dgq2[0, :FOX_HEAD_DIM], e_g_kn=dgk2[0, :FOX_HEAD_DIM],
                 e_b_f=dbf[0, :FOX_HEADS])
        G["e_w_in"] = _mm(f"d_e_win{l}", [(sv["h1"], dz)], "tn")
    else:
        G["o_w_out"] = _mm(f"d_o_wout{l}", [(sv["ycat"], g1)], "tn")
        dz, dwp, dsp, dcw = _odd_bwd(f"odd_bwd{l}", sv["z"], dycat, W["w_pool"], W["s_pool"], W["conv_w8"])
        G.update(o_w_pool=dwp, o_s_pool=dsp[0], o_conv_w=dcw[:CONV_K])
        G["o_w_in"] = _mm(f"d_o_win{l}", [(sv["h1"], dz)], "tn")
    g0, dg = _mm(f"d_h1{l}", [(dz, W["w_in"])], "nt", norm_bwd=(sv["x0"], W["g_mix"], g1))
    G["g_mix"] = dg[0]
    return g0, G, side_out


PACK_W = 1024
MESH_T = pl.DeviceIdType.MESH
_ANY = pl.BlockSpec(memory_space=pl.ANY)


def _my_place():
    x, y, c = lax.axis_index("x"), lax.axis_index("y"), lax.axis_index("c")
    return x, y, c


def _flip(v, bit):
    return 1 - v if bit else v


N_PEERS = N_DEV - 1


def _all_gather(name, blks):
    side = _TwoLevelGather(blks)

    def body(*refs):
        _, side_in, _, side_out, _, sems = _split_side(refs, 0, 0, 0, side)
        side.start(side_in, side_out, sems)
        side.middle(side_in, side_out, sems)
        side.finish(side_in, side_out, sems)

    return pl.pallas_call(
        body, name=name, out_shape=side.out_shape,
        in_specs=[_ANY] * side.n, out_specs=[_ANY] * side.n, scratch_shapes=side.scratch,
    )(*blks)


class _Direct:
    def __init__(self, kind, arrays):
        self.kind, self.arrays, self.n = kind, list(arrays), len(arrays)
        if kind == "scatter":
            self.out_shape = [jax.ShapeDtypeStruct(a.shape, a.dtype) for a in arrays]
        else:
            self.out_shape = [jax.ShapeDtypeStruct((N_DEV,) + a.shape, a.dtype) for a in arrays]
        self.scratch = [pltpu.SemaphoreType.DMA((N_PEERS * self.n,)), pltpu.SemaphoreType.DMA((N_PEERS * self.n,)),
                        pltpu.SemaphoreType.DMA((self.n,))]

    def _copies(self, in_refs, out_refs, sems):
        send_sems, recv_sems, local_sems = sems
        x, y, c = _my_place()
        me = 4 * x + 2 * y + c
        mine, sends, recvs = [], [], []
        for t in range(self.n):
            src_of = (lambda idx, t=t: in_refs[t].at[idx]) if self.kind == "scatter" else (lambda idx, t=t: in_refs[t])
            mine.append(pltpu.make_async_copy(src_of(me), out_refs[t].at[me], local_sems.at[t]))
            for m in range(1, N_DEV):
                px, py, pc = _flip(x, m & 4), _flip(y, m & 2), _flip(c, m & 1)
                pidx = 4 * px + 2 * py + pc
                sem = dict(send_sem=send_sems.at[N_PEERS * t + m - 1], recv_sem=recv_sems.at[N_PEERS * t + m - 1], device_id_type=MESH_T)
                sends.append(pltpu.make_async_remote_copy(src_ref=src_of(pidx), dst_ref=out_refs[t].at[me], device_id=(px, py, pc), **sem))
                recvs.append(pltpu.make_async_remote_copy(src_ref=src_of(pidx), dst_ref=out_refs[t].at[pidx], device_id=(x, y, c), **sem))
        return mine, sends, recvs

    def start(self, in_refs, out_refs, sems):
        mine, sends, _ = self._copies(in_refs, out_refs, sems)
        for cp in mine + sends:
            cp.start()

    def middle(self, in_refs, out_refs, sems):
        pass

    def finish(self, in_refs, out_refs, sems):
        mine, sends, recvs = self._copies(in_refs, out_refs, sems)
        for cp in recvs:
            cp.wait_recv()
        for cp in sends:
            cp.wait_send()
        for cp in mine:
            cp.wait()


class _TwoLevelGather:
    def __init__(self, arrays):
        self.arrays, self.n = list(arrays), len(arrays)
        self.out_shape = [jax.ShapeDtypeStruct((N_DEV,) + a.shape, a.dtype) for a in arrays]
        self.scratch = [pltpu.SemaphoreType.DMA((N_PEERS * self.n,)), pltpu.SemaphoreType.DMA((N_PEERS * self.n,)),
                        pltpu.SemaphoreType.DMA((self.n,))]

    def _place(self):
        x, y, c = _my_place()
        return (x, y, c), (x, y, 1 - c), [(1 - x, y), (x, 1 - y), (1 - x, 1 - y)]

    def _copy(self, in_refs, out_refs, sems, t, k, block, to, own=False):
        slab = out_refs[t].at[4 * block[0] + 2 * block[1] + block[2]]
        return pltpu.make_async_remote_copy(
            src_ref=in_refs[t] if own else slab, dst_ref=slab, send_sem=sems[0].at[N_PEERS * t + k],
            recv_sem=sems[1].at[N_PEERS * t + k], device_id=to, device_id_type=MESH_T)

    def _mine(self, in_refs, out_refs, sems):
        x, y, c = _my_place()
        return [pltpu.make_async_copy(in_refs[t], out_refs[t].at[4 * x + 2 * y + c], sems[2].at[t]) for t in range(self.n)]

    def _first(self, in_refs, out_refs, sems):
        me, sibling, chips = self._place()
        cps = []
        for t in range(self.n):
            cps.append(self._copy(in_refs, out_refs, sems, t, 0, me, sibling, own=True))
            cps += [self._copy(in_refs, out_refs, sems, t, 1 + j, me, (*chip, me[2]), own=True) for j, chip in enumerate(chips)]
        return cps

    def _passed(self, in_refs, out_refs, sems):
        me, sibling, chips = self._place()
        return [self._copy(in_refs, out_refs, sems, t, 4 + j, (*chip, me[2]), sibling) for j, chip in enumerate(chips) for t in range(self.n)]

    def start(self, in_refs, out_refs, sems):
        for cp in self._mine(in_refs, out_refs, sems) + self._first(in_refs, out_refs, sems):
            cp.start()

    def middle(self, in_refs, out_refs, sems):
        me, _, chips = self._place()
        for j, chip in enumerate(chips):
            for t in range(self.n):
                self._copy(in_refs, out_refs, sems, t, 1 + j, (*chip, me[2]), me).wait_recv()
        for cp in self._passed(in_refs, out_refs, sems):
            cp.start()

    def finish(self, in_refs, out_refs, sems):
        me, sibling, chips = self._place()
        for t in range(self.n):
            self._copy(in_refs, out_refs, sems, t, 0, sibling, me).wait_recv()
            for j, chip in enumerate(chips):
                self._copy(in_refs, out_refs, sems, t, 4 + j, (*chip, 1 - me[2]), me).wait_recv()
        for cp in self._first(in_refs, out_refs, sems) + self._passed(in_refs, out_refs, sems):
            cp.wait_send()
        for cp in self._mine(in_refs, out_refs, sems):
            cp.wait()


def _split_side(refs, n_main_in, n_main_out, n_main_scratch, side):
    ns = side.n if side is not None else 0
    i = 0
    main_in = refs[i:i + n_main_in]; i += n_main_in
    side_in = refs[i:i + ns]; i += ns
    main_out = refs[i:i + n_main_out]; i += n_main_out
    side_out = refs[i:i + ns]; i += ns
    main_scr = refs[i:i + n_main_scratch]; i += n_main_scratch
    return main_in, side_in, main_out, side_out, main_scr, refs[i:]


def _all_to_all(name, gs):
    side = _Direct("scatter", gs)

    def body(*refs):
        _, side_in, _, side_out, _, sems = _split_side(refs, 0, 0, 0, side)
        side.start(side_in, side_out, sems)
        side.finish(side_in, side_out, sems)

    return pl.pallas_call(
        body, name=name, out_shape=side.out_shape,
        in_specs=[_ANY] * side.n, out_specs=[_ANY] * side.n, scratch_shapes=side.scratch,
    )(*gs)


def _unblock(name, blocks, layer, n_out):
    _, _, K, nb = blocks.shape
    tk = _tile(K, 256, 16)

    def body(b_ref, o_ref):
        for j in range(N_DEV):
            o_ref[:, j * nb:(j + 1) * nb] = b_ref[j]
        if n_out > N_DEV * nb:
            o_ref[:, N_DEV * nb:n_out] = jnp.zeros((tk, n_out - N_DEV * nb), o_ref.dtype)

    return pl.pallas_call(
        body, name=name, grid=(K // tk,),
        in_specs=[pl.BlockSpec((N_DEV, None, tk, nb), lambda i: (0, layer, i, 0))],
        out_specs=pl.BlockSpec((tk, n_out), lambda i: (i, 0)),
        out_shape=jax.ShapeDtypeStruct((K, n_out), blocks.dtype),
        compiler_params=_cp(("parallel",)),
    )(blocks)


def _block(name, full, nb, dtype):
    K, N = full.shape
    tk = _tile(K, 256, 16)

    def body(f_ref, o_ref):
        for j in range(N_DEV):
            o_ref[j] = f_ref[:, j * nb:(j + 1) * nb].astype(o_ref.dtype)

    return pl.pallas_call(
        body, name=name, grid=(K // tk,),
        in_specs=[pl.BlockSpec((tk, N), lambda i: (i, 0))],
        out_specs=pl.BlockSpec((N_DEV, tk, nb), lambda i: (0, i, 0)),
        out_shape=jax.ShapeDtypeStruct((N_DEV, K, nb), dtype),
        compiler_params=_cp(("parallel",)),
    )(full)


ADAM_BLOCK_ELEMS = 256 * 1024
GRAD_WIRE = jnp.bfloat16


def _adamw(name, parts, w, m, v, layer=0):
    shape = w.shape[1:]
    cols = shape[-1]
    rows = math.prod(shape[:-1])
    nl = w.shape[0]
    parts, w, m, v = parts.reshape(N_DEV, rows, cols), w.reshape(nl, rows, cols), m.reshape(nl, rows, cols), v.reshape(nl, rows, cols)
    tr = _tile(rows, max(16, ADAM_BLOCK_ELEMS // cols), 16)

    def body(p_ref, w_ref, m_ref, v_ref, g_ref, d_ref, mo_ref, vo_ref):
        g = p_ref[0].astype(F32)
        for s in range(1, N_DEV):
            g = g + p_ref[s].astype(F32)
        mn = ADAM_B1 * m_ref[...] + (1.0 - ADAM_B1) * g
        vn = ADAM_B2 * v_ref[...] + (1.0 - ADAM_B2) * jnp.square(g)
        m_hat = mn / (1.0 - ADAM_B1 ** ADAM_STEP)
        v_hat = vn / (1.0 - ADAM_B2 ** ADAM_STEP)
        g_ref[...] = g
        d_ref[...] = -ADAM_LR * (m_hat / (jnp.sqrt(v_hat) + ADAM_EPS) + ADAM_WD * w_ref[...])
        mo_ref[...] = mn
        vo_ref[...] = vn

    blk = pl.BlockSpec((tr, cols), lambda i: (i, 0))
    lblk = pl.BlockSpec((None, tr, cols), lambda i: (layer, i, 0))
    shp = jax.ShapeDtypeStruct((rows, cols), F32)
    res = pl.pallas_call(
        body, name=name, grid=(rows // tr,),
        in_specs=[pl.BlockSpec((N_DEV, tr, cols), lambda i: (0, i, 0)), lblk, lblk, lblk],
        out_specs=[blk, blk, blk, blk], out_shape=[shp, shp, shp, shp],
        compiler_params=_cp(("parallel",)),
    )(parts, w, m, v)
    return [r.reshape(shape) for r in res]


SHARDED = dict(xa_wq=1, xa_wkv=1, xa_wo=2, w_gate=2, w_up=2, w_down=1, e_w_in=2, e_w_out=1, o_w_in=2, o_s_pool=1, o_conv_w=2, o_w_out=1)
LAYER_SPLIT = ("xa_wq", "xa_wkv", "xa_wo", "w_gate", "w_up", "w_down")
ELEMENTWISE_SHARDED = ("o_s_pool", "o_conv_w")
REPLICATED = ("g_mix", "g_xa", "g_mem", "xa_gq", "xa_gk", "g_ffn", "e_b_f", "e_g_v", "e_w_s", "e_b_s", "e_g_qn", "e_g_kn", "o_w_pool")
WEIGHTS = ("g_mix", "g_xa", "g_mem", "xa_wq", "xa_wkv", "xa_wo", "xa_gq", "xa_gk", "g_ffn", "w_gate", "w_up", "w_down", "e_w_in", "e_b_f",
           "e_g_v", "e_w_s", "e_b_s", "e_g_qn", "e_g_kn", "e_w_out", "o_w_in", "o_w_pool", "o_s_pool", "o_conv_w", "o_w_out")


def _rows_for(n, mult):
    return -(-n // (PACK_W * mult)) * mult


def _pack(arrs, rows, dtype):
    flat = jnp.concatenate([a.reshape(-1).astype(dtype) for a in arrs])
    return jnp.pad(flat, (0, rows * PACK_W - flat.shape[0])).reshape(rows, PACK_W)


def _unpack(slab, shapes):
    flat, out, off = slab.reshape(-1), [], 0
    for s in shapes:
        n = math.prod(s)
        out.append(flat[off:off + n].reshape(s))
        off += n
    return out


def _to_blocks(full, axis):
    s = full.shape
    return jnp.moveaxis(full.reshape(s[:axis] + (N_DEV, s[axis] // N_DEV) + s[axis + 1:]), axis, 0)


def _from_blocks(blocks, axis):
    b = jnp.moveaxis(blocks, 0, axis)
    s = b.shape
    return b.reshape(s[:axis] + (s[axis] * s[axis + 1],) + s[axis + 2:])


def kernel(x, mem, g_mix, g_xa, g_mem, xa_wq, xa_wkv, xa_wo, xa_gq, xa_gk, g_ffn, w_gate, w_up, w_down, e_w_in, e_b_f, e_g_v, e_w_s, e_b_s, e_g_qn, e_g_kn, e_w_out, o_w_in, o_w_pool, o_s_pool, o_conv_w, o_w_out, loss_target, m_g_mix, m_g_xa, m_g_mem, m_xa_wq, m_xa_wkv, m_xa_wo, m_xa_gq, m_xa_gk, m_g_ffn, m_w_gate, m_w_up, m_w_down, m_e_w_in, m_e_b_f, m_e_g_v, m_e_w_s, m_e_b_s, m_e_g_qn, m_e_g_kn, m_e_w_out, m_o_w_in, m_o_w_pool, m_o_s_pool, m_o_conv_w, m_o_w_out, v_g_mix, v_g_xa, v_g_mem, v_xa_wq, v_xa_wkv, v_xa_wo, v_xa_gq, v_xa_gk, v_g_ffn, v_w_gate, v_w_up, v_w_down, v_e_w_in, v_e_b_f, v_e_g_v, v_e_w_s, v_e_b_s, v_e_g_qn, v_e_g_kn, v_e_w_out, v_o_w_in, v_o_w_pool, v_o_s_pool, v_o_conv_w, v_o_w_out):
    args = dict(locals())
    Wt = {n: args[n] for n in WEIGHTS}
    Mo = {n: args["m_" + n] for n in WEIGHTS}
    Vo = {n: args["v_" + n] for n in WEIGHTS}
    rep_names = REPLICATED
    rep_shapes = [Wt[n].shape for n in rep_names]
    rep_rows = _rows_for(sum(math.prod(s) for s in rep_shapes) + 1, 16)
    layer_names = [LAYER_SPLIT + ("e_w_in", "e_w_out"), LAYER_SPLIT + ("o_w_in", "o_w_out", "o_s_pool", "o_conv_w")]

    def shard_of(n, l, src=Wt):
        return src[n][l:l + 1] if n in LAYER_SPLIT else src[n]

    def to_wire(n, a):
        return a if n in ELEMENTWISE_SHARDED else _mx(a)

    def full_weight(n, l, gn):
        if SHARDED[n] == 1 or n in ELEMENTWISE_SHARDED:
            return _from_blocks(gn, SHARDED[n])[0]
        return _unblock(f"unblock_{n}{l}", gn, 0, EVEN_IN_PAD if n == "e_w_in" else N_DEV * gn.shape[-1])

    def grad_slabs(n, l, gl):
        if n in ELEMENTWISE_SHARDED:
            return _to_blocks(gl.reshape((1,) + Wt[n].shape[1:-1] + (-1,)), SHARDED[n])
        if SHARDED[n] == 1:
            return _to_blocks(gl[None], 1).astype(GRAD_WIRE)
        return _block(f"block_{n}{l}", gl, Wt[n].shape[-1], GRAD_WIRE)[:, None]

    xk, mem2 = x[0], mem[0]
    later = [(n, 0) for n in layer_names[0] if n != "e_w_in"] + [(n, 1) for n in layer_names[1]]
    w_in0 = full_weight("e_w_in", 0, _all_gather("gather_e_w_in", [to_wire("e_w_in", Wt["e_w_in"])])[0])
    side = _TwoLevelGather([to_wire(n, shard_of(n, l)) for n, l in later])
    P = {n: Wt[n] for n in rep_names}
    P["e_w_in"] = {0: w_in0}

    def on_side(gathered):
        for (n, l), gn in zip(later, gathered):
            P.setdefault(n, {})[l if n in LAYER_SPLIT else 0] = full_weight(n, l, gn)

    xk, sv0, W0 = _layer_fwd(0, xk, mem2, P, side=side, on_side=on_side)
    xk, sv1, W1 = _layer_fwd(1, xk, mem2, P)

    g, loss_row = _loss_grad("loss_grad", xk, loss_target[0])
    g, G1, _ = _layer_bwd(1, g, mem2, W1, sv1)
    slabs1 = [grad_slabs(n, 1, G1[n]) for n in layer_names[1]]
    early0 = LAYER_SPLIT + ("e_w_out",)
    g, G0, beside = _layer_bwd(0, g, mem2, W0, sv0,
                               side_fn=lambda G: _Direct("scatter", slabs1 + [grad_slabs(n, 0, G[n]) for n in early0]))
    parts1, parts0a = beside[:len(slabs1)], beside[len(slabs1):]
    grad_x = g[None]
    parts0b = _all_to_all("exchange_grads_mixer0", [grad_slabs("e_w_in", 0, G0["e_w_in"])])
    parts = {(n, 1): p for n, p in zip(layer_names[1], parts1)}
    parts.update({(n, 0): p for n, p in zip(early0, parts0a)})
    parts[("e_w_in", 0)] = parts0b[0]
    G = {n: {0: G0[n]} for n in G0}
    for n in G1:
        G.setdefault(n, {})[1] = G1[n]
    rep_g = [jnp.stack([G[n][l] for l in sorted(G[n])]).reshape(Wt[n].shape) for n in rep_names]
    rep_parts = _all_gather("gather_small_grads", [_pack(rep_g + [jnp.sum(loss_row).reshape(1)], rep_rows, F32)])[0]

    outs = {}
    for n in SHARDED:
        per_layer = [_adamw(f"adamw_{n}{l}", parts[(n, l if n in LAYER_SPLIT else (0 if n.startswith("e_") else 1))], Wt[n], Mo[n], Vo[n], layer=l)
                     for l in range(Wt[n].shape[0])]
        for k, kind in enumerate(("grad", "delta", "new_m", "new_v")):
            outs[kind + "_" + n] = jnp.stack([r[k] for r in per_layer])
    res = _adamw("adamw_replicated", rep_parts, _pack([Wt[n] for n in rep_names], rep_rows, F32)[None],
                 _pack([Mo[n] for n in rep_names], rep_rows, F32)[None], _pack([Vo[n] for n in rep_names], rep_rows, F32)[None])
    for kind, slab in zip(("grad", "delta", "new_m", "new_v"), res):
        for n, a in zip(rep_names, _unpack(slab, rep_shapes)):
            outs[kind + "_" + n] = a
    loss = res[0].reshape(-1)[sum(math.prod(s) for s in rep_shapes)]
    return (loss, grad_x, *[outs[k + "_" + n] for k in ("grad", "delta", "new_m", "new_v") for n in WEIGHTS])
```
